```python
import math
import jax, jax.numpy as jnp
from jax import lax
import numpy as np


D_MODEL = 1024
BATCH = 4
SEQ = 8192
DEPTH = 2

HEAD_DIM = 64
N_HEADS_TOTAL = D_MODEL // HEAD_DIM
A_HEADS = N_HEADS_TOTAL // 4
B_HEADS = 3 * N_HEADS_TOTAL // 8
C_HEADS = N_HEADS_TOTAL - A_HEADS - B_HEADS
C_KV_HEADS = 2
C_GROUP = C_HEADS // C_KV_HEADS
A_QK_DIM = HEAD_DIM // 2
A_V_DIM = HEAD_DIM
MOBA_BLOCK = 256
MOBA_TOPK = 3
MOBA_QCHUNK = 32
WINDOW = 128
DENSE_QBLOCK = 128
ALIBI_MAX = 8.0
A_Q_W = A_HEADS * 2 * A_QK_DIM
A_K_W = A_HEADS * 2 * A_QK_DIM
A_V_W = A_HEADS * A_V_DIM
B_W = B_HEADS * HEAD_DIM
C_Q_W = C_HEADS * HEAD_DIM
C_KV_W = C_KV_HEADS * HEAD_DIM
IN_WIDTH = A_Q_W + A_K_W + A_V_W + 3 * B_W + C_Q_W + 2 * C_KV_W
MIX_WIDTH = A_V_W + B_W + C_Q_W
PEER_HEADS = 8
PEER_NKEYS = 128
PEER_N = PEER_NKEYS * PEER_NKEYS
PEER_TOPK = 16
PEER_QDIM = 256
PEER_CHUNK = 128
N_MOD = 6
EPS = 1e-6

kernel_name = "hymba_diff_moba_swa_peer_adaln"


def rmsnorm(x, g):
    xf = x.astype(jnp.float32)
    y = xf * lax.rsqrt(jnp.mean(xf * xf, axis=-1, keepdims=True) + EPS)
    return (y * g.astype(jnp.float32)).astype(x.dtype)


def alibi_slopes():
    n = N_HEADS_TOTAL
    return 2.0 ** (-ALIBI_MAX * jnp.arange(1, n + 1, dtype=jnp.float32) / n)


def diff_attention(q, k, v, slopes, lam, lam_init, subln_g):
    B, S, H, _, dk = q.shape
    nq = S // DENSE_QBLOCK
    scale = dk ** -0.5
    pos_k = jnp.arange(S, dtype=jnp.float32)
    qb = q.reshape(B, nq, DENSE_QBLOCK, H, 2, dk).transpose(1, 0, 2, 3, 4, 5)

    def block(args):
        i, qi = args
        pos_q = (i * DENSE_QBLOCK + jnp.arange(DENSE_QBLOCK)).astype(jnp.float32)
        dist = pos_q[:, None] - pos_k[None, :]
        s = jnp.einsum('bqhmd,bshmd->bhmqs', qi, k).astype(jnp.float32) * scale
        s = jnp.where((dist >= 0)[None, None, None],
                      s - slopes[None, :, None, None, None] * dist[None, None, None], -jnp.inf)
        p = jax.nn.softmax(s, axis=-1)
        w = (p[:, :, 0] - lam * p[:, :, 1]).astype(v.dtype)
        return jnp.einsum('bhqs,bshd->bqhd', w, v)

    o = lax.map(block, (jnp.arange(nq), qb))
    o = o.transpose(1, 0, 2, 3, 4).reshape(B, S, H, -1)
    return rmsnorm(o, subln_g) * (1.0 - lam_init)


def moba_attention(q, k, v, slopes):
    B, S, H, d = q.shape
    L = MOBA_BLOCK
    nb = -(-S // L)
    Sp = nb * L
    pad = ((0, 0), (0, Sp - S), (0, 0), (0, 0))
    q = jnp.pad(q, pad)
    k = jnp.pad(k, pad)
    v = jnp.pad(v, pad)
    K = min(MOBA_TOPK, nb)
    nc = Sp // MOBA_QCHUNK
    scale = d ** -0.5
    kbh = k.reshape(B, nb, L, H, d).transpose(0, 3, 1, 2, 4)
    vbh = v.reshape(B, nb, L, H, d).transpose(0, 3, 1, 2, 4)
    kmean = jnp.mean(kbh.astype(jnp.float32), axis=3)
    gate = jnp.einsum('bshd,bhnd->bshn', q.astype(jnp.float32), kmean)
    q_blk = jnp.arange(Sp) // L
    past = jnp.arange(nb)[None, :] < q_blk[:, None]
    gate = jnp.where(past[None, :, None, :], gate, -jnp.inf)
    gsc, sel = lax.top_k(gate, K)
    valid = jnp.isfinite(gsc)
    b_idx = jnp.arange(B)[:, None, None, None]
    h_idx = jnp.arange(H)[None, None, :, None]
    offs = jnp.arange(L)
    KL = K * L

    def chunk(i):
        t0 = i * MOBA_QCHUNK
        qc = lax.dynamic_slice_in_dim(q, t0, MOBA_QCHUNK, axis=1)
        selc = lax.dynamic_slice_in_dim(sel, t0, MOBA_QCHUNK, axis=1)
        valc = lax.dynamic_slice_in_dim(valid, t0, MOBA_QCHUNK, axis=1)
        pos_q = (t0 + jnp.arange(MOBA_QCHUNK)).astype(jnp.float32)
        k_sel = kbh[b_idx, h_idx, selc]
        v_sel = vbh[b_idx, h_idx, selc]
        own = t0 // L
        k_own = lax.dynamic_index_in_dim(kbh, own, axis=2, keepdims=False)
        v_own = lax.dynamic_index_in_dim(vbh, own, axis=2, keepdims=False)
        s_sel = jnp.einsum('bqhd,bqhkld->bqhkl', qc, k_sel).astype(jnp.float32) * scale
        pos_sel = (selc[..., None] * L + offs).astype(jnp.float32)
        dist_sel = pos_q[None, :, None, None, None] - pos_sel
        s_sel = jnp.where(valc[..., None], s_sel - slopes[:, None, None] * dist_sel, -jnp.inf)
        pos_own = (own * L + offs).astype(jnp.float32)
        dist_own = pos_q[:, None] - pos_own[None, :]
        s_own = jnp.einsum('bqhd,bhld->bqhl', qc, k_own).astype(jnp.float32) * scale
        s_own = jnp.where((dist_own >= 0)[None, :, None, :],
                          s_own - slopes[:, None] * dist_own[None, :, None, :], -jnp.inf)
        s_all = jnp.concatenate([s_sel.reshape(B, MOBA_QCHUNK, H, KL), s_own], axis=-1)
        p = jax.nn.softmax(s_all, axis=-1).astype(v.dtype)
        p_sel = p[..., :KL].reshape(B, MOBA_QCHUNK, H, K, L)
        p_own = p[..., KL:]
        return (jnp.einsum('bqhkl,bqhkld->bqhd', p_sel, v_sel)
                + jnp.einsum('bqhl,bhld->bqhd', p_own, v_own))

    o = lax.map(chunk, jnp.arange(nc))
    return o.transpose(1, 0, 2, 3, 4).reshape(B, Sp, H, d)[:, :S]


def swa_sink_attention(q, k, v, slopes, sinks):
    B, S, Hq, d = q.shape
    W = WINDOW
    nb = S // W
    scale = d ** -0.5
    qb = q.reshape(B, nb, W, C_KV_HEADS, C_GROUP, d)

    def with_prev(t):
        tb = t.reshape(B, nb, W, C_KV_HEADS, d)
        prev = jnp.pad(tb, ((0, 0), (1, 0), (0, 0), (0, 0), (0, 0)))[:, :-1]
        return jnp.concatenate([prev, tb], axis=2)

    kb = with_prev(k)
    vb = with_prev(v)
    s = jnp.einsum('bnqhgd,bnkhd->bnhgqk', qb, kb).astype(jnp.float32) * scale
    qq = jnp.arange(W)
    kk = jnp.arange(2 * W)
    rel = (qq[:, None] + W - kk[None, :]).astype(jnp.float32)
    key_pos = jnp.arange(nb)[:, None] * W - W + kk[None, :]
    mask = (rel >= 0)[None] & (rel < W)[None] & (key_pos >= 0)[:, None, :]
    sl = slopes.reshape(C_KV_HEADS, C_GROUP)
    s = jnp.where(mask[None, :, None, None], s - sl[:, :, None, None] * rel, -jnp.inf)
    sink = jnp.broadcast_to(sinks.astype(jnp.float32).reshape(C_KV_HEADS, C_GROUP)[None, None, :, :, None, None],
                            s.shape[:-1] + (1,))
    p = jax.nn.softmax(jnp.concatenate([s, sink], axis=-1), axis=-1)[..., :-1].astype(v.dtype)
    o = jnp.einsum('bnhgqk,bnkhd->bnqhgd', p, vb)
    return o.reshape(B, S, Hq, d)


def mixer(h, w_in, w_out, lam, lam_init, subln_g, sinks):
    B, S, _ = h.shape
    proj = jnp.einsum('bsd,de->bse', h, w_in)
    cuts = np.cumsum([A_Q_W, A_K_W, A_V_W, B_W, B_W, B_W, C_Q_W, C_KV_W]).tolist()
    qa, ka, va, qb, kb, vb, qc, kc, vc = jnp.split(proj, cuts, axis=-1)
    slopes = alibi_slopes()
    sl_c = slopes[:C_HEADS]
    sl_a = slopes[C_HEADS:C_HEADS + A_HEADS]
    sl_b = slopes[C_HEADS + A_HEADS:]
    oa = diff_attention(qa.reshape(B, S, A_HEADS, 2, A_QK_DIM), ka.reshape(B, S, A_HEADS, 2, A_QK_DIM),
                        va.reshape(B, S, A_HEADS, A_V_DIM), sl_a, lam, lam_init, subln_g)
    ob = moba_attention(qb.reshape(B, S, B_HEADS, HEAD_DIM), kb.reshape(B, S, B_HEADS, HEAD_DIM),
                        vb.reshape(B, S, B_HEADS, HEAD_DIM), sl_b)
    oc = swa_sink_attention(qc.reshape(B, S, C_HEADS, HEAD_DIM), kc.reshape(B, S, C_KV_HEADS, HEAD_DIM),
                            vc.reshape(B, S, C_KV_HEADS, HEAD_DIM), sl_c, sinks)
    o = jnp.concatenate([oa.reshape(B, S, A_V_W), ob.reshape(B, S, B_W), oc.reshape(B, S, C_Q_W)], axis=-1)
    return jnp.einsum('bse,ed->bsd', o, w_out)


def peer_ffn(h, wq, sub_keys, u, v):
    B, S, D = h.shape
    T = B * S
    xs = h.reshape(T // PEER_CHUNK, PEER_CHUNK, D)

    def chunk(xc):
        q = jnp.einsum('td,de->te', xc, wq).reshape(PEER_CHUNK, PEER_HEADS, 2, PEER_QDIM // 2)
        s = jnp.einsum('thpd,hpnd->thpn', q, sub_keys).astype(jnp.float32)
        sv, si = lax.top_k(s, PEER_TOPK)
        cand = (sv[:, :, 0, :, None] + sv[:, :, 1, None, :]).reshape(PEER_CHUNK, PEER_HEADS, -1)
        cidx = (si[:, :, 0, :, None] * PEER_NKEYS + si[:, :, 1, None, :]).reshape(PEER_CHUNK, PEER_HEADS, -1)
        top, pos = lax.top_k(cand, PEER_TOPK)
        eidx = jnp.take_along_axis(cidx, pos, axis=-1)
        g = jax.nn.softmax(top, axis=-1)
        ue = u[eidx]
        ve = v[eidx]
        a = jax.nn.gelu(jnp.einsum('thkd,td->thk', ue, xc).astype(jnp.float32), approximate=False)
        return jnp.einsum('thk,thkd->td', (g * a).astype(ve.dtype), ve)

    return lax.map(chunk, xs).reshape(B, S, D)


def setup_inputs(seed: int = 0) -> dict:
    key = jax.random.key(seed)
    ks = jax.random.split(key, 20)
    f32 = jnp.float32
    D = D_MODEL

    def nrm(k, shape, scale):
        return jax.random.normal(k, shape, f32) * scale

    return {
        "x": nrm(ks[0], (BATCH, SEQ, D), 1.0),
        "c": nrm(ks[1], (BATCH, D), 1.0),
        "norm1_g": 1.0 + nrm(ks[2], (DEPTH, D), 0.02),
        "norm2_g": 1.0 + nrm(ks[3], (DEPTH, D), 0.02),
        "w_ada": nrm(ks[4], (DEPTH, D, N_MOD * D), 0.5 * D ** -0.5),
        "b_ada": nrm(ks[5], (DEPTH, N_MOD * D), 0.02),
        "w_in": nrm(ks[6], (DEPTH, D, IN_WIDTH), D ** -0.5),
        "w_out": nrm(ks[7], (DEPTH, MIX_WIDTH, D), MIX_WIDTH ** -0.5),
        "lam_q1": nrm(ks[8], (DEPTH, A_QK_DIM), 0.1),
        "lam_k1": nrm(ks[9], (DEPTH, A_QK_DIM), 0.1),
        "lam_q2": nrm(ks[10], (DEPTH, A_QK_DIM), 0.1),
        "lam_k2": nrm(ks[11], (DEPTH, A_QK_DIM), 0.1),
        "subln_g": 1.0 + nrm(ks[12], (DEPTH, A_V_DIM), 0.02),
        "sinks": nrm(ks[13], (DEPTH, C_HEADS), 0.5),
        "peer_wq": nrm(ks[14], (DEPTH, D, PEER_HEADS * PEER_QDIM), D ** -0.5),
        "peer_keys": nrm(ks[15], (DEPTH, PEER_HEADS, 2, PEER_NKEYS, PEER_QDIM // 2), (PEER_QDIM // 2) ** -0.5),
        "peer_u": nrm(ks[16], (DEPTH, PEER_N, D), D ** -0.5),
        "peer_v": nrm(ks[17], (DEPTH, PEER_N, D), (PEER_HEADS * PEER_TOPK) ** -0.5),
        "final_g": 1.0 + nrm(ks[18], (D,), 0.02),
    }


def reference(x, c, norm1_g, norm2_g, w_ada, b_ada, w_in, w_out, lam_q1, lam_k1, lam_q2, lam_k2,
              subln_g, sinks, peer_wq, peer_keys, peer_u, peer_v, final_g):
    cs = jax.nn.silu(c)
    for l in range(DEPTH):
        mod = jnp.einsum('bd,de->be', cs, w_ada[l]) + b_ada[l]
        sh1, sc1, g1, sh2, sc2, g2 = [m[:, None, :] for m in jnp.split(mod, N_MOD, axis=-1)]
        lam_init = 0.8 - 0.6 * math.exp(-0.3 * l)
        lam = (jnp.exp(jnp.sum(lam_q1[l].astype(jnp.float32) * lam_k1[l].astype(jnp.float32)))
               - jnp.exp(jnp.sum(lam_q2[l].astype(jnp.float32) * lam_k2[l].astype(jnp.float32))) + lam_init)
        h = rmsnorm(x, norm1_g[l]) * (1.0 + sc1) + sh1
        x = x + g1 * mixer(h, w_in[l], w_out[l], lam, lam_init, subln_g[l], sinks[l])
        h = rmsnorm(x, norm2_g[l]) * (1.0 + sc2) + sh2
        x = x + g2 * peer_ffn(h, peer_wq[l], peer_keys[l], peer_u[l], peer_v[l])
    return rmsnorm(x, final_g)
```

```python
import functools
import math

import numpy as np
import jax
import jax.numpy as jnp
from jax import lax
from jax.experimental import pallas as pl
from jax.experimental.pallas import tpu as pltpu

F32 = jnp.float32
BF16 = jnp.bfloat16
I32 = jnp.int32

D_MODEL = 1024
HEAD_DIM = 64
N_HEADS_TOTAL = 16
A_HEADS = 4
B_HEADS = 6
C_HEADS = 6
C_KV_HEADS = 2
C_GROUP = 3
A_QK_DIM = 32
MOBA_BLOCK = 256
MOBA_TOPK = 3
WINDOW = 128
ALIBI_MAX = 8.0
PEER_HEADS = 8
PEER_NKEYS = 128
PEER_TOPK = 16
N_MOD = 6
EPS = 1e-6

LANES = 128
KT = 256
AUG0 = HEAD_DIM
SEL0 = AUG0 + 6
MAX_BLOCKS = LANES - SEL0
NEG_BIG = -1e30
VMEM_LIMIT = 56 * 1024 * 1024


def _alibi_slopes():
    n = N_HEADS_TOTAL
    return (2.0 ** (-ALIBI_MAX * np.arange(1, n + 1, dtype=np.float32) / n)).astype(np.float32)


def _split3(v):
    v = np.float32(v)
    hi = np.float32(np.asarray(v).astype(jnp.bfloat16).astype(np.float32))
    r = np.float32(v - hi)
    mid = np.float32(np.asarray(r).astype(jnp.bfloat16).astype(np.float32))
    lo = np.float32(np.float32(r - mid))
    lo = np.float32(np.asarray(lo).astype(jnp.bfloat16).astype(np.float32))
    return hi, mid, lo


def _slope_bias_row(slopes):
    row = np.zeros((1, LANES * len(slopes)), np.float32)
    for h, s in enumerate(slopes):
        hi, mid, lo = _split3(s)
        row[0, h * LANES + AUG0:h * LANES + AUG0 + 6] = [hi, mid, lo, hi, mid, lo]
    return jnp.asarray(row)


def _mod_kernel(c_ref, w_ref, b_ref, o_ref):
    c = c_ref[...]
    cs = c * (1.0 / (1.0 + jnp.exp(-c)))
    o_ref[0] = jnp.dot(cs, w_ref[0], preferred_element_type=F32) + b_ref[0]


def _adaln_mods(c, w_ada, b_ada):
    depth, d, n = w_ada.shape
    bsz = c.shape[0]
    rows = -(-bsz // 8) * 8
    cp = jnp.pad(c, ((0, rows - bsz), (0, 0)))
    tn = 1536
    out = pl.pallas_call(
        _mod_kernel,
        grid=(depth, n // tn),
        in_specs=[
            pl.BlockSpec((rows, d), lambda l, j: (0, 0)),
            pl.BlockSpec((1, d, tn), lambda l, j: (l, 0, j)),
            pl.BlockSpec((1, 1, tn), lambda l, j: (l, 0, j)),
        ],
        out_specs=pl.BlockSpec((1, rows, tn), lambda l, j: (l, 0, j)),
        out_shape=jax.ShapeDtypeStruct((depth, rows, n), F32),
        compiler_params=pltpu.CompilerParams(dimension_semantics=("arbitrary", "arbitrary"),
                                             vmem_limit_bytes=VMEM_LIMIT),
        name="adaln_mods",
    )(cp, w_ada, b_ada.reshape(depth, 1, n))
    return out[:, :bsz]


NN_WIDTHS = (A_HEADS * LANES, A_HEADS * HEAD_DIM, B_HEADS * LANES, B_HEADS * HEAD_DIM,
             C_HEADS * HEAD_DIM, 2 * C_KV_HEADS * HEAD_DIM)
NT_ROWS = (A_HEADS * LANES, B_HEADS * LANES, C_KV_HEADS * LANES)


def _prep_in_weights(w):
    d = w.shape[0]
    aq, ak, av = A_HEADS * 2 * A_QK_DIM, A_HEADS * 2 * A_QK_DIM, A_HEADS * HEAD_DIM
    bw = B_HEADS * HEAD_DIM
    cq, ckv = C_HEADS * HEAD_DIM, C_KV_HEADS * HEAD_DIM
    cuts = np.cumsum([aq, ak, av, bw, bw, bw, cq, ckv]).tolist()
    qa, ka, va, qb, kb, vb, qc, kc, vc = jnp.split(w, cuts, axis=-1)

    def pad_heads(m, nh, scale):
        m = (m * scale).reshape(d, nh, HEAD_DIM)
        return jnp.pad(m, ((0, 0), (0, 0), (0, LANES - HEAD_DIM))).reshape(d, nh * LANES)

    vc2 = vc.reshape(d, C_KV_HEADS, 1, HEAD_DIM)
    vc2 = jnp.broadcast_to(vc2, (d, C_KV_HEADS, 2, HEAD_DIM)).reshape(d, 2 * ckv)
    kc2 = jnp.broadcast_to(kc.reshape(d, C_KV_HEADS, 1, HEAD_DIM), (d, C_KV_HEADS, 2, HEAD_DIM)).reshape(d, 2 * ckv)
    wn = jnp.concatenate([pad_heads(qa, A_HEADS, A_QK_DIM ** -0.5), va,
                          pad_heads(qb, B_HEADS, HEAD_DIM ** -0.5), vb,
                          qc * (HEAD_DIM ** -0.5), vc2], axis=1)
    wt = jnp.concatenate([pad_heads(ka, A_HEADS, 1.0), pad_heads(kb, B_HEADS, 1.0), kc2], axis=1).T
    return wn.astype(BF16), wt.astype(BF16)


def _rms_mod(x, g, sc, sh):
    ms = jnp.mean(x * x, axis=-1, keepdims=True)
    return (x * lax.rsqrt(ms + EPS) * g) * (1.0 + sc) + sh


def _inproj_kernel(x_ref, mod_ref, g_ref, wn_ref, wt_ref, qab_ref, qbb_ref,
                   qa_ref, va_ref, qb_ref, vb_ref, qc_ref, vc_ref, kta_ref, ktb_ref, ktc_ref,
                   *, tm, seq):
    d = x_ref.shape[1]
    x = x_ref[...]
    sh = mod_ref[0, :, 0:d]
    sc = mod_ref[0, :, d:2 * d]
    h = _rms_mod(x, g_ref[...], sc, sh).astype(BF16)
    pn = jnp.dot(h, wn_ref[...], preferred_element_type=F32)
    pt = lax.dot_general(wt_ref[...], h, (((1,), (1,)), ((), ())),
                         preferred_element_type=F32)
    o = 0
    outs = (qa_ref, va_ref, qb_ref, vb_ref, qc_ref, vc_ref)
    bias = (qab_ref, None, qbb_ref, None, None, None)
    for ref, b, wd in zip(outs, bias, NN_WIDTHS):
        blk = pn[:, o:o + wd]
        if b is not None:
            blk = blk + b[...]
        ref[...] = blk.astype(BF16)
        o += wd

    pos = (pl.program_id(0) * tm) % seq + lax.broadcasted_iota(I32, (LANES, tm), 1)
    row = lax.broadcasted_iota(I32, (LANES, tm), 0)
    blk_id = pos // KT
    p_hi = (blk_id * KT).astype(F32)
    p_lo = (pos - blk_id * KT).astype(F32)
    aug_a = jnp.where((row >= AUG0) & (row < AUG0 + 3), p_hi,
                      jnp.where((row >= AUG0 + 3) & (row < AUG0 + 6), p_lo, 0.0))
    aug_b = jnp.where((row >= SEL0) & (row - SEL0 == blk_id), 1.0, aug_a)

    def emit(ref, r0, nheads, aug):
        for hh in range(nheads):
            blk = pt[r0 + hh * LANES:r0 + (hh + 1) * LANES, :]
            if aug is not None:
                blk = blk + aug
            blk = blk.astype(BF16)
            for cc in range(tm // KT):
                ref[cc, hh * LANES:(hh + 1) * LANES, :] = blk[:, cc * KT:(cc + 1) * KT]

    emit(kta_ref, 0, A_HEADS, aug_a)
    emit(ktb_ref, NT_ROWS[0], B_HEADS, aug_b)
    emit(ktc_ref, NT_ROWS[0] + NT_ROWS[1], C_KV_HEADS, None)


def _inproj(x2d, mod3, g, wn, wt, qa_bias, qb_bias, *, seq, tm=512):
    t, d = x2d.shape
    assert seq % tm == 0 and tm % KT == 0
    nt = t // tm
    nn_total = sum(NN_WIDTHS)
    row_specs = [pl.BlockSpec((tm, wd), lambda i: (i, 0)) for wd in NN_WIDTHS]
    kt_specs = [pl.BlockSpec((tm // KT, r, KT), lambda i: (i, 0, 0)) for r in NT_ROWS]
    out_shape = ([jax.ShapeDtypeStruct((t, wd), BF16) for wd in NN_WIDTHS]
                 + [jax.ShapeDtypeStruct((t // KT, r, KT), BF16) for r in NT_ROWS])
    return pl.pallas_call(
        functools.partial(_inproj_kernel, tm=tm, seq=seq),
        grid=(nt,),
        in_specs=[
            pl.BlockSpec((tm, d), lambda i: (i, 0)),
            pl.BlockSpec((1, 1, mod3.shape[2]), lambda i: ((i * tm) // seq, 0, 0)),
            pl.BlockSpec((1, d), lambda i: (0, 0)),
            pl.BlockSpec((d, nn_total), lambda i: (0, 0)),
            pl.BlockSpec((sum(NT_ROWS), d), lambda i: (0, 0)),
            pl.BlockSpec((1, NN_WIDTHS[0]), lambda i: (0, 0)),
            pl.BlockSpec((1, NN_WIDTHS[2]), lambda i: (0, 0)),
        ],
        out_specs=row_specs + kt_specs,
        out_shape=out_shape,
        compiler_params=pltpu.CompilerParams(dimension_semantics=("arbitrary",),
                                             vmem_limit_bytes=VMEM_LIMIT),
        name="inproj",
    )(x2d, mod3, g, wn, wt, qa_bias, qb_bias)


def _flash_step(q, kt, v, m_ref, l_ref, acc_ref, mask=None):
    s = jnp.dot(q, kt, preferred_element_type=F32)
    if mask is not None:
        s = jnp.where(mask, s, -jnp.inf)
    m_prev = m_ref[...]
    m_new = jnp.maximum(m_prev, jnp.max(s, axis=-1, keepdims=True))
    alpha = jnp.exp(m_prev - m_new)
    p = jnp.exp(s - m_new)
    l_ref[...] = alpha * l_ref[...] + jnp.sum(p, axis=-1, keepdims=True)
    acc_ref[...] = alpha * acc_ref[...] + jnp.dot(p.astype(BF16), v, preferred_element_type=F32)
    m_ref[...] = m_new


def _flash_init(m_ref, l_ref, acc_ref):
    m_ref[...] = jnp.full(m_ref.shape, -jnp.inf, F32)
    l_ref[...] = jnp.zeros(l_ref.shape, F32)
    acc_ref[...] = jnp.zeros(acc_ref.shape, F32)


def _diff_kernel(q_ref, kt_ref, v_ref, lamv_ref, sg_ref, o_ref, m_ref, l_ref, acc_ref, *, lam_init):
    tq = q_ref.shape[0]
    qi = pl.program_id(2)
    lv = lamv_ref[...]
    lam = (jnp.exp(jnp.sum(lv[0:1] * lv[1:2], axis=-1, keepdims=True))
           - jnp.exp(jnp.sum(lv[2:3] * lv[3:4], axis=-1, keepdims=True)) + lam_init)
    lane = lax.broadcasted_iota(I32, (tq, LANES), 1)
    r2 = lax.broadcasted_iota(I32, (2 * tq, KT), 0)
    c2 = lax.broadcasted_iota(I32, (2 * tq, KT), 1)
    causal = c2 <= jnp.where(r2 >= tq, r2 - tq, r2)
    res = []
    for hh in range(2):
        q = q_ref[:, hh * LANES:(hh + 1) * LANES]
        zero = jnp.zeros_like(q)
        q1 = jnp.where((lane < A_QK_DIM) | (lane >= AUG0), q, zero)
        q2 = jnp.where(lane >= A_QK_DIM, q, zero)
        qs = jnp.concatenate([q1, q2], axis=0)
        _flash_init(m_ref, l_ref, acc_ref)
        _flash_step(qs, kt_ref[qi, hh * LANES:(hh + 1) * LANES, :],
                    v_ref[pl.ds(pl.multiple_of(qi * KT, KT), KT), :], m_ref, l_ref, acc_ref, mask=causal)

        def body(kj, carry):
            _flash_step(qs, kt_ref[kj, hh * LANES:(hh + 1) * LANES, :],
                        v_ref[pl.ds(pl.multiple_of(kj * KT, KT), KT), :], m_ref, l_ref, acc_ref)
            return carry

        lax.fori_loop(0, qi, body, 0)
        o = acc_ref[...] / l_ref[...]
        od = o[:tq] - lam * o[tq:]
        mine = (lane >= hh * HEAD_DIM) & (lane < (hh + 1) * HEAD_DIM)
        ms = jnp.sum(jnp.where(mine, od * od, 0.0), axis=-1, keepdims=True) * (1.0 / HEAD_DIM)
        res.append(od * lax.rsqrt(ms + EPS) * sg_ref[...] * (1.0 - lam_init))
    o_ref[...] = jnp.where(lane < HEAD_DIM, res[0], res[1]).astype(BF16)


def _diff_attention(qa, kta, va, lamv, sg2, *, bsz, seq, lam_init):
    tq = KT
    nq = seq // tq
    t = bsz * seq
    return pl.pallas_call(
        functools.partial(_diff_kernel, lam_init=lam_init),
        grid=(bsz, A_HEADS // 2, nq),
        in_specs=[
            pl.BlockSpec((tq, 2 * LANES), lambda b, hp, qi: (b * nq + qi, hp)),
            pl.BlockSpec((nq, 2 * LANES, KT), lambda b, hp, qi: (b, hp, 0)),
            pl.BlockSpec((seq, LANES), lambda b, hp, qi: (b, hp)),
            pl.BlockSpec((4, A_QK_DIM), lambda b, hp, qi: (0, 0)),
            pl.BlockSpec((1, LANES), lambda b, hp, qi: (0, 0)),
        ],
        out_specs=pl.BlockSpec((tq, LANES), lambda b, hp, qi: (b * nq + qi, hp)),
        out_shape=jax.ShapeDtypeStruct((t, A_HEADS * HEAD_DIM), BF16),
        scratch_shapes=[pltpu.VMEM((2 * tq, 1), F32), pltpu.VMEM((2 * tq, 1), F32),
                        pltpu.VMEM((2 * tq, LANES), F32)],
        compiler_params=pltpu.CompilerParams(dimension_semantics=("arbitrary",) * 3,
                                             vmem_limit_bytes=VMEM_LIMIT),
        name="diff_attn",
    )(qa, kta, va, lamv, sg2)


def _moba_kernel(q_ref, kt_ref, v_ref, o_ref, m_ref, l_ref, acc_ref, km_ref):
    tq = q_ref.shape[0]
    nb = kt_ref.shape[0]
    qi = pl.program_id(2)

    @pl.when(qi == 0)
    def _():
        rr = lax.broadcasted_iota(I32, (KT, LANES), 1)
        for hh in range(2):
            def add_block(j, acc):
                pool = jnp.where(rr == SEL0 + j, 1.0 / KT, 0.0).astype(BF16)
                return acc + jnp.dot(kt_ref[j, hh * LANES:hh * LANES + HEAD_DIM, :], pool,
                                     preferred_element_type=F32)
            km = lax.fori_loop(0, nb, add_block, jnp.zeros((HEAD_DIM, LANES), F32))
            km_ref[hh] = jnp.concatenate([km, jnp.zeros((LANES - HEAD_DIM, LANES), F32)], axis=0)

    lane = lax.broadcasted_iota(I32, (tq, LANES), 1)
    r2 = lax.broadcasted_iota(I32, (tq, KT), 0)
    c2 = lax.broadcasted_iota(I32, (tq, KT), 1)
    causal = c2 <= r2
    in_sel = (lane >= SEL0) & (lane < SEL0 + MAX_BLOCKS)
    res = []
    for hh in range(2):
        q = q_ref[:, hh * LANES:(hh + 1) * LANES]
        km = km_ref[hh]
        km_hi = km.astype(BF16)
        km_lo = (km - km_hi.astype(F32)).astype(BF16)
        gate = (jnp.dot(q, km_hi, preferred_element_type=F32)
                + jnp.dot(q, km_lo, preferred_element_type=F32))
        cur = jnp.where((lane >= SEL0) & (lane < SEL0 + qi), gate, -jnp.inf)
        sel = jnp.zeros((tq, LANES), jnp.bool_)
        for _ in range(MOBA_TOPK):
            mx = jnp.max(cur, axis=-1, keepdims=True)
            first = jnp.min(jnp.where(cur == mx, lane, 4 * LANES), axis=-1, keepdims=True)
            pick = (lane == first) & (mx > -jnp.inf)
            sel = sel | pick
            cur = jnp.where(pick, -jnp.inf, cur)
        q_diag = jnp.where(in_sel, jnp.zeros_like(q), q)
        q_past = jnp.where(in_sel, jnp.where(sel, 0.0, NEG_BIG).astype(BF16), q)
        _flash_init(m_ref, l_ref, acc_ref)
        _flash_step(q_diag, kt_ref[qi, hh * LANES:(hh + 1) * LANES, :],
                    v_ref[pl.ds(pl.multiple_of(qi * KT, KT), KT), :], m_ref, l_ref, acc_ref, mask=causal)

        def body(kj, carry):
            _flash_step(q_past, kt_ref[kj, hh * LANES:(hh + 1) * LANES, :],
                        v_ref[pl.ds(pl.multiple_of(kj * KT, KT), KT), :], m_ref, l_ref, acc_ref)
            return carry

        lax.fori_loop(0, qi, body, 0)
        res.append(acc_ref[...] / l_ref[...])
    o_ref[...] = jnp.where(lane < HEAD_DIM, res[0], res[1]).astype(BF16)


def _moba_attention(qb, ktb, vb, *, bsz, seq):
    tq = KT
    nq = seq // tq
    assert nq <= MAX_BLOCKS
    t = bsz * seq
    return pl.pallas_call(
        _moba_kernel,
        grid=(bsz, B_HEADS // 2, nq),
        in_specs=[
            pl.BlockSpec((tq, 2 * LANES), lambda b, hp, qi: (b * nq + qi, hp)),
            pl.BlockSpec((nq, 2 * LANES, KT), lambda b, hp, qi: (b, hp, 0)),
            pl.BlockSpec((seq, LANES), lambda b, hp, qi: (b, hp)),
        ],
        out_specs=pl.BlockSpec((tq, LANES), lambda b, hp, qi: (b * nq + qi, hp)),
        out_shape=jax.ShapeDtypeStruct((t, B_HEADS * HEAD_DIM), BF16),
        scratch_shapes=[pltpu.VMEM((tq, 1), F32), pltpu.VMEM((tq, 1), F32),
                        pltpu.VMEM((tq, LANES), F32), pltpu.VMEM((2, LANES, LANES), F32)],
        compiler_params=pltpu.CompilerParams(dimension_semantics=("arbitrary",) * 3,
                                             vmem_limit_bytes=VMEM_LIMIT),
        name="moba_attn",
    )(qb, ktb, vb)


def _swa_kernel(sink_ref, q_ref, ktp_ref, ktc_ref, vp_ref, vc_ref, o_ref, *, slopes):
    tq = q_ref.shape[0]
    n = pl.program_id(1)
    lane = lax.broadcasted_iota(I32, (tq, LANES), 1)
    r2 = lax.broadcasted_iota(I32, (tq, 2 * KT), 0)
    c2 = lax.broadcasted_iota(I32, (tq, 2 * KT), 1)
    rel = r2 + KT - c2
    mask = (rel >= 0) & (rel < WINDOW) & ((c2 >= KT) | (n > 0))
    relf = rel.astype(F32)
    res = []
    for hq in range(C_HEADS):
        kv = hq // C_GROUP
        qp = q_ref[:, (hq // 2) * LANES:(hq // 2 + 1) * LANES]
        qm = jnp.where((lane < HEAD_DIM) == (hq % 2 == 0), qp, jnp.zeros_like(qp))
        kt = jnp.concatenate([ktp_ref[0, kv * LANES:(kv + 1) * LANES, :],
                              ktc_ref[0, kv * LANES:(kv + 1) * LANES, :]], axis=1)
        vv = jnp.concatenate([vp_ref[:, kv * LANES:(kv + 1) * LANES],
                              vc_ref[:, kv * LANES:(kv + 1) * LANES]], axis=0)
        s = jnp.dot(qm, kt, preferred_element_type=F32)
        s = jnp.where(mask, s - slopes[hq] * relf, -jnp.inf)
        sink = sink_ref[hq]
        m = jnp.maximum(jnp.max(s, axis=-1, keepdims=True), sink)
        e = jnp.exp(s - m)
        den = jnp.sum(e, axis=-1, keepdims=True) + jnp.exp(sink - m)
        p = (e / den).astype(BF16)
        res.append(jnp.dot(p, vv, preferred_element_type=F32))
    for pr in range(C_HEADS // 2):
        o_ref[:, pr * LANES:(pr + 1) * LANES] = jnp.where(lane < HEAD_DIM, res[2 * pr], res[2 * pr + 1]).astype(BF16)


def _swa_attention(sinks, qc, ktc, vc2, *, bsz, seq, slopes):
    tq = KT
    nq = seq // tq
    t = bsz * seq
    return pl.pallas_call(
        functools.partial(_swa_kernel, slopes=slopes),
        grid=(bsz, nq),
        in_specs=[
            pl.BlockSpec(memory_space=pltpu.SMEM),
            pl.BlockSpec((tq, C_HEADS * HEAD_DIM), lambda b, n: (b * nq + n, 0)),
            pl.BlockSpec((1, C_KV_HEADS * LANES, KT), lambda b, n: (b * nq + jnp.maximum(n - 1, 0), 0, 0)),
            pl.BlockSpec((1, C_KV_HEADS * LANES, KT), lambda b, n: (b * nq + n, 0, 0)),
            pl.BlockSpec((tq, C_KV_HEADS * LANES), lambda b, n: (b * nq + jnp.maximum(n - 1, 0), 0)),
            pl.BlockSpec((tq, C_KV_HEADS * LANES), lambda b, n: (b * nq + n, 0)),
        ],
        out_specs=pl.BlockSpec((tq, C_HEADS * HEAD_DIM), lambda b, n: (b * nq + n, 0)),
        out_shape=jax.ShapeDtypeStruct((t, C_HEADS * HEAD_DIM), BF16),
        compiler_params=pltpu.CompilerParams(dimension_semantics=("arbitrary",) * 2,
                                             vmem_limit_bytes=VMEM_LIMIT),
        name="swa_attn",
    )(sinks, qc, ktc, ktc, vc2, vc2)


def _outproj_kernel(oa_ref, ob_ref, oc_ref, woa_ref, wob_ref, woc_ref, x_ref, mod_ref, g_ref, wq_ref, keys_ref,
                    x1_ref, h2_ref, st_ref):
    d = x_ref.shape[1]
    mix = (jnp.dot(oa_ref[...], woa_ref[...], preferred_element_type=F32)
           + jnp.dot(ob_ref[...], wob_ref[...], preferred_element_type=F32)
           + jnp.dot(oc_ref[...], woc_ref[...], preferred_element_type=F32))
    g1 = mod_ref[0, :, 2 * d:3 * d]
    sh2 = mod_ref[0, :, 3 * d:4 * d]
    sc2 = mod_ref[0, :, 4 * d:5 * d]
    x1 = x_ref[...] + g1 * mix
    x1_ref[...] = x1
    h2 = _rms_mod(x1, g_ref[...], sc2, sh2)
    h2_ref[...] = h2
    pq = jnp.dot(h2.astype(BF16), wq_ref[...], preferred_element_type=F32).astype(BF16)
    for hp in range(2 * PEER_HEADS):
        st_ref[hp] = lax.dot_general(keys_ref[hp], pq[:, hp * LANES:(hp + 1) * LANES],
                                     (((1,), (1,)), ((), ())), preferred_element_type=F32)


def _outproj(oa, ob, oc, woa, wob, woc, x2d, mod3, g, wq, keys, *, seq, tm=256):
    t, d = x2d.shape
    nt = t // tm
    nq = wq.shape[1]
    full = lambda a: pl.BlockSpec(a.shape, lambda i: (0,) * a.ndim)
    return pl.pallas_call(
        _outproj_kernel,
        grid=(nt,),
        in_specs=[
            pl.BlockSpec((tm, oa.shape[1]), lambda i: (i, 0)),
            pl.BlockSpec((tm, ob.shape[1]), lambda i: (i, 0)),
            pl.BlockSpec((tm, oc.shape[1]), lambda i: (i, 0)),
            full(woa), full(wob), full(woc),
            pl.BlockSpec((tm, d), lambda i: (i, 0)),
            pl.BlockSpec((1, 1, mod3.shape[2]), lambda i: ((i * tm) // seq, 0, 0)),
            pl.BlockSpec((1, d), lambda i: (0, 0)),
            full(wq), full(keys),
        ],
        out_specs=[pl.BlockSpec((tm, d), lambda i: (i, 0)),
                   pl.BlockSpec((tm, d), lambda i: (i, 0)),
                   pl.BlockSpec((2 * PEER_HEADS, PEER_NKEYS, tm), lambda i: (0, 0, i))],
        out_shape=[jax.ShapeDtypeStruct((t, d), F32), jax.ShapeDtypeStruct((t, d), F32),
                   jax.ShapeDtypeStruct((2 * PEER_HEADS, PEER_NKEYS, t), F32)],
        compiler_params=pltpu.CompilerParams(dimension_semantics=("arbitrary",),
                                             vmem_limit_bytes=VMEM_LIMIT),
        name="outproj_peerq",
    )(oa, ob, oc, woa, wob, woc, x2d, mod3, g, wq, keys)


_CAND_BLOCKS = ((0, 16),) + tuple((i, 8) for i in range(1, 8))
_CAND_ROWS = 16 + 7 * 8 + 8
_BIG_I = np.int32(2 ** 30)


def _topk_rows(cur, key, val_ref, key_ref):
    def body(r, cur):
        m = jnp.max(cur, axis=0, keepdims=True)
        kmin = jnp.min(jnp.where(cur == m, key, _BIG_I), axis=0, keepdims=True)
        val_ref[pl.ds(r, 1), :] = m
        key_ref[pl.ds(r, 1), :] = kmin
        return jnp.where(key == kmin, -jnp.inf, cur)
    lax.fori_loop(0, PEER_TOPK, body, cur)


def _peer_topk_kernel(st_ref, eidx_ref, g_ref, v1_ref, k1_ref, v2_ref, k2_ref, vt_ref, kt_ref, ei_ref, gg_ref):
    tl = st_ref.shape[2]
    row = lax.broadcasted_iota(I32, (PEER_NKEYS, tl), 0)
    for h in range(PEER_HEADS):
        _topk_rows(st_ref[2 * h], row, v1_ref, k1_ref)
        _topk_rows(st_ref[2 * h + 1], row, v2_ref, k2_ref)
        sv1, si1, sv2, si2 = v1_ref[...], k1_ref[...], v2_ref[...], k2_ref[...]
        jrow8 = lax.broadcasted_iota(I32, (8, tl), 0)
        jrow16 = lax.broadcasted_iota(I32, (16, tl), 0)
        cands, keys = [], []
        for i, nj in _CAND_BLOCKS:
            jrow = jrow16 if nj == 16 else jrow8
            cands.append(sv1[i:i + 1] + sv2[0:nj])
            keys.append((i * PEER_TOPK + jrow) * (PEER_NKEYS * PEER_NKEYS) + si1[i:i + 1] * PEER_NKEYS + si2[0:nj])
        cands.append(sv1[8:16] + sv2[0:1])
        keys.append((jrow8 + 8) * (PEER_TOPK * PEER_NKEYS * PEER_NKEYS) + si1[8:16] * PEER_NKEYS + si2[0:1])
        cand = jnp.concatenate(cands, axis=0)
        ckey = jnp.concatenate(keys, axis=0)
        _topk_rows(cand, ckey, vt_ref, kt_ref)
        top = vt_ref[...]
        e = jnp.exp(top - top[0:1])
        gg_ref[h * PEER_TOPK:(h + 1) * PEER_TOPK, :] = e / jnp.sum(e, axis=0, keepdims=True)
        ei_ref[h * PEER_TOPK:(h + 1) * PEER_TOPK, :] = kt_ref[...] & (PEER_NKEYS * PEER_NKEYS - 1)
    eidx_ref[...] = ei_ref[...].T
    g_ref[...] = gg_ref[...].T


def _peer_topk(st, *, tl=128):
    t = st.shape[2]
    hk = PEER_HEADS * PEER_TOPK
    return pl.pallas_call(
        _peer_topk_kernel,
        grid=(t // tl,),
        in_specs=[pl.BlockSpec((2 * PEER_HEADS, PEER_NKEYS, tl), lambda i: (0, 0, i))],
        out_specs=[pl.BlockSpec((tl, hk), lambda i: (i, 0)), pl.BlockSpec((tl, hk), lambda i: (i, 0))],
        out_shape=[jax.ShapeDtypeStruct((t, hk), I32), jax.ShapeDtypeStruct((t, hk), F32)],
        scratch_shapes=[pltpu.VMEM((PEER_TOPK, tl), F32), pltpu.VMEM((PEER_TOPK, tl), I32),
                        pltpu.VMEM((PEER_TOPK, tl), F32), pltpu.VMEM((PEER_TOPK, tl), I32),
                        pltpu.VMEM((PEER_TOPK, tl), F32), pltpu.VMEM((PEER_TOPK, tl), I32),
                        pltpu.VMEM((hk, tl), I32), pltpu.VMEM((hk, tl), F32)],
        compiler_params=pltpu.CompilerParams(dimension_semantics=("arbitrary",),
                                             vmem_limit_bytes=VMEM_LIMIT),
        name="peer_topk",
    )(st)


def _erf(x):
    return lax.erf(x)


def _peer_gather_kernel(idx_ref, idxn_ref, h_ref, g_ref, x1_ref, mod_ref, fg_ref, u_hbm, v_hbm,
                        o_ref, ubuf, vbuf, usem, vsem, *, tq, final):
    i = pl.program_id(0)
    n = pl.num_programs(0)
    rows = tq * PEER_HEADS * PEER_TOPK
    d = h_ref.shape[1]

    def issue(iref, slot):
        def one(r, carry):
            e = iref[r]
            pltpu.make_async_copy(u_hbm.at[e], ubuf.at[slot, r], usem.at[slot]).start()
            pltpu.make_async_copy(v_hbm.at[e], vbuf.at[slot, r], vsem.at[slot]).start()
            return carry
        lax.fori_loop(0, rows, one, 0, unroll=8)

    @pl.when(i == 0)
    def _():
        issue(idx_ref, 0)

    slot = i % 2

    @pl.when(i + 1 < n)
    def _():
        issue(idxn_ref, 1 - slot)

    pltpu.make_async_copy(u_hbm.at[pl.ds(0, rows)], ubuf.at[slot], usem.at[slot]).wait()
    pltpu.make_async_copy(v_hbm.at[pl.ds(0, rows)], vbuf.at[slot], vsem.at[slot]).wait()

    hk = PEER_HEADS * PEER_TOPK
    gt = g_ref[...].T
    outs = []
    for tt in range(tq):
        ue = ubuf[slot, tt * hk:(tt + 1) * hk, :]
        ve = vbuf[slot, tt * hk:(tt + 1) * hk, :]
        a = jnp.sum(ue * h_ref[tt:tt + 1, :], axis=-1, keepdims=True)
        act = 0.5 * a * (1.0 + _erf(a * (2.0 ** -0.5)))
        w = gt[:, tt:tt + 1] * act
        outs.append(jnp.sum(w * ve, axis=0, keepdims=True))
    peer = jnp.concatenate(outs, axis=0)
    g2 = mod_ref[0, :, 5 * d:6 * d]
    y = x1_ref[...] + g2 * peer
    if final:
        ms = jnp.mean(y * y, axis=-1, keepdims=True)
        y = y * lax.rsqrt(ms + EPS) * fg_ref[...]
    o_ref[...] = y


def _peer_gather(eidx_flat, h2, g, x1, mod3, final_g, u, v, *, seq, final, tq=8):
    t, d = h2.shape
    hk = PEER_HEADS * PEER_TOPK
    rows = tq * hk
    n = t // tq
    return pl.pallas_call(
        functools.partial(_peer_gather_kernel, tq=tq, final=final),
        grid=(n,),
        in_specs=[
            pl.BlockSpec((rows,), lambda i: (i,), memory_space=pltpu.SMEM),
            pl.BlockSpec((rows,), lambda i: (jnp.minimum(i + 1, n - 1),), memory_space=pltpu.SMEM),
            pl.BlockSpec((tq, d), lambda i: (i, 0)),
            pl.BlockSpec((tq, hk), lambda i: (i, 0)),
            pl.BlockSpec((tq, d), lambda i: (i, 0)),
            pl.BlockSpec((1, 1, mod3.shape[2]), lambda i: ((i * tq) // seq, 0, 0)),
            pl.BlockSpec((1, d), lambda i: (0, 0)),
            pl.BlockSpec(memory_space=pl.ANY),
            pl.BlockSpec(memory_space=pl.ANY),
        ],
        out_specs=pl.BlockSpec((tq, d), lambda i: (i, 0)),
        out_shape=jax.ShapeDtypeStruct((t, d), F32),
        scratch_shapes=[pltpu.VMEM((2, rows, d), F32), pltpu.VMEM((2, rows, d), F32),
                        pltpu.SemaphoreType.DMA((2,)), pltpu.SemaphoreType.DMA((2,))],
        compiler_params=pltpu.CompilerParams(dimension_semantics=("arbitrary",),
                                             vmem_limit_bytes=VMEM_LIMIT),
        name="peer_gather",
    )(eidx_flat, eidx_flat, h2, g, x1, mod3, final_g, u, v)


def kernel(x, c, norm1_g, norm2_g, w_ada, b_ada, w_in, w_out, lam_q1, lam_k1, lam_q2, lam_k2, subln_g, sinks,
           peer_wq, peer_keys, peer_u, peer_v, final_g):
    bsz, seq, d = x.shape
    depth = w_in.shape[0]
    t = bsz * seq
    slopes = _alibi_slopes()
    sl_c = [float(s) for s in slopes[:C_HEADS]]
    qa_bias = _slope_bias_row(slopes[C_HEADS:C_HEADS + A_HEADS])
    qb_bias = _slope_bias_row(slopes[C_HEADS + A_HEADS:])
    mods = _adaln_mods(c, w_ada, b_ada)
    x2d = x.reshape(t, d)
    fg = final_g.reshape(1, d)
    av, bw = A_HEADS * HEAD_DIM, B_HEADS * HEAD_DIM
    for l in range(depth):
        lam_init = 0.8 - 0.6 * math.exp(-0.3 * l)
        mod3 = mods[l].reshape(bsz, 1, N_MOD * d)
        wn, wt = _prep_in_weights(w_in[l])
        qa, va, qb, vb, qc, vc2, kta, ktb, ktc = _inproj(
            x2d, mod3, norm1_g[l].reshape(1, d), wn, wt, qa_bias, qb_bias, seq=seq)
        lamv = jnp.stack([lam_q1[l], lam_k1[l], lam_q2[l], lam_k2[l]]).astype(F32)
        sg2 = jnp.concatenate([subln_g[l], subln_g[l]]).reshape(1, LANES).astype(F32)
        oa = _diff_attention(qa, kta, va, lamv, sg2, bsz=bsz, seq=seq, lam_init=lam_init)
        ob = _moba_attention(qb, ktb, vb, bsz=bsz, seq=seq)
        oc = _swa_attention(sinks[l].astype(F32), qc, ktc, vc2, bsz=bsz, seq=seq, slopes=sl_c)
        wo = w_out[l].astype(BF16)
        keys = peer_keys[l].reshape(2 * PEER_HEADS, PEER_NKEYS, -1).astype(BF16)
        x1, h2, st = _outproj(oa, ob, oc, wo[:av], wo[av:av + bw], wo[av + bw:], x2d, mod3,
                              norm2_g[l].reshape(1, d), peer_wq[l].astype(BF16), keys, seq=seq)
        eidx, g = _peer_topk(st)
        x2d = _peer_gather(eidx.reshape(-1), h2, g, x1, mod3, fg, peer_u[l], peer_v[l],
                           seq=seq, final=(l == depth - 1))
    return x2d.reshape(bsz, seq, d)
```

```python
import functools
import math

import numpy as np
import jax
import jax.numpy as jnp
from jax import lax
from jax.experimental import pallas as pl
from jax.experimental.pallas import tpu as pltpu

F32 = jnp.float32
BF16 = jnp.bfloat16
I32 = jnp.int32

D_MODEL = 1024
HEAD_DIM = 64
N_HEADS_TOTAL = 16
A_HEADS = 4
B_HEADS = 6
C_HEADS = 6
C_KV_HEADS = 2
C_GROUP = 3
A_QK_DIM = 32
MOBA_BLOCK = 256
MOBA_TOPK = 3
WINDOW = 128
ALIBI_MAX = 8.0
PEER_HEADS = 8
PEER_NKEYS = 128
PEER_TOPK = 16
N_MOD = 6
EPS = 1e-6

LANES = 128
KT = 256
AUG0 = HEAD_DIM
SEL0 = AUG0 + 6
MAX_BLOCKS = LANES - SEL0
NEG_BIG = -1e30
VMEM_LIMIT = 56 * 1024 * 1024


def _alibi_slopes():
    n = N_HEADS_TOTAL
    return (2.0 ** (-ALIBI_MAX * np.arange(1, n + 1, dtype=np.float32) / n)).astype(np.float32)


def _split3(v):
    v = np.float32(v)
    hi = np.float32(np.asarray(v).astype(jnp.bfloat16).astype(np.float32))
    r = np.float32(v - hi)
    mid = np.float32(np.asarray(r).astype(jnp.bfloat16).astype(np.float32))
    lo = np.float32(np.float32(r - mid))
    lo = np.float32(np.asarray(lo).astype(jnp.bfloat16).astype(np.float32))
    return hi, mid, lo


def _slope_bias_row(slopes):
    row = np.zeros((1, LANES * len(slopes)), np.float32)
    for h, s in enumerate(slopes):
        hi, mid, lo = _split3(s)
        row[0, h * LANES + AUG0:h * LANES + AUG0 + 6] = [hi, mid, lo, hi, mid, lo]
    return jnp.asarray(row)


def _mod_kernel(c_ref, w_ref, b_ref, o_ref):
    c = c_ref[...]
    cs = c * (1.0 / (1.0 + jnp.exp(-c)))
    o_ref[0] = jnp.dot(cs, w_ref[0], preferred_element_type=F32) + b_ref[0]


def _adaln_mods(c, w_ada, b_ada):
    depth, d, n = w_ada.shape
    bsz = c.shape[0]
    rows = -(-bsz // 8) * 8
    cp = jnp.pad(c, ((0, rows - bsz), (0, 0)))
    tn = 1536
    out = pl.pallas_call(
        _mod_kernel,
        grid=(depth, n // tn),
        in_specs=[
            pl.BlockSpec((rows, d), lambda l, j: (0, 0)),
            pl.BlockSpec((1, d, tn), lambda l, j: (l, 0, j)),
            pl.BlockSpec((1, 1, tn), lambda l, j: (l, 0, j)),
        ],
        out_specs=pl.BlockSpec((1, rows, tn), lambda l, j: (l, 0, j)),
        out_shape=jax.ShapeDtypeStruct((depth, rows, n), F32),
        compiler_params=pltpu.CompilerParams(dimension_semantics=("arbitrary", "arbitrary"),
                                             vmem_limit_bytes=VMEM_LIMIT),
        name="adaln_mods",
    )(cp, w_ada, b_ada.reshape(depth, 1, n))
    return out[:, :bsz]


NN_WIDTHS = (A_HEADS * LANES, A_HEADS * HEAD_DIM, B_HEADS * LANES, B_HEADS * HEAD_DIM,
             C_HEADS * HEAD_DIM, 2 * C_KV_HEADS * HEAD_DIM)
NT_ROWS = (A_HEADS * LANES, B_HEADS * LANES, C_KV_HEADS * LANES)


def _prep_in_weights(w):
    d = w.shape[0]
    aq, ak, av = A_HEADS * 2 * A_QK_DIM, A_HEADS * 2 * A_QK_DIM, A_HEADS * HEAD_DIM
    bw = B_HEADS * HEAD_DIM
    cq, ckv = C_HEADS * HEAD_DIM, C_KV_HEADS * HEAD_DIM
    cuts = np.cumsum([aq, ak, av, bw, bw, bw, cq, ckv]).tolist()
    qa, ka, va, qb, kb, vb, qc, kc, vc = jnp.split(w, cuts, axis=-1)

    def pad_heads(m, nh, scale):
        m = (m * scale).reshape(d, nh, HEAD_DIM)
        return jnp.pad(m, ((0, 0), (0, 0), (0, LANES - HEAD_DIM))).reshape(d, nh * LANES)

    vc2 = vc.reshape(d, C_KV_HEADS, 1, HEAD_DIM)
    vc2 = jnp.broadcast_to(vc2, (d, C_KV_HEADS, 2, HEAD_DIM)).reshape(d, 2 * ckv)
    kc2 = jnp.broadcast_to(kc.reshape(d, C_KV_HEADS, 1, HEAD_DIM), (d, C_KV_HEADS, 2, HEAD_DIM)).reshape(d, 2 * ckv)
    wn = jnp.concatenate([pad_heads(qa, A_HEADS, A_QK_DIM ** -0.5), va,
                          pad_heads(qb, B_HEADS, HEAD_DIM ** -0.5), vb,
                          qc * (HEAD_DIM ** -0.5), vc2], axis=1)
    wt = jnp.concatenate([pad_heads(ka, A_HEADS, 1.0), pad_heads(kb, B_HEADS, 1.0), kc2], axis=1).T
    return wn.astype(BF16), wt.astype(BF16)


def _rms_mod(x, g, sc, sh):
    ms = jnp.mean(x * x, axis=-1, keepdims=True)
    return (x * lax.rsqrt(ms + EPS) * g) * (1.0 + sc) + sh


def _inproj_kernel(x_ref, mod_ref, g_ref, wn_ref, wt_ref, qab_ref, qbb_ref,
                   qa_ref, va_ref, qb_ref, vb_ref, qc_ref, vc_ref, kta_ref, ktb_ref, ktc_ref,
                   *, tm, seq):
    d = x_ref.shape[1]
    x = x_ref[...]
    sh = mod_ref[0, :, 0:d]
    sc = mod_ref[0, :, d:2 * d]
    h = _rms_mod(x, g_ref[...], sc, sh).astype(BF16)
    pn = jnp.dot(h, wn_ref[...], preferred_element_type=F32)
    pt = lax.dot_general(wt_ref[...], h, (((1,), (1,)), ((), ())),
                         preferred_element_type=F32)
    o = 0
    outs = (qa_ref, va_ref, qb_ref, vb_ref, qc_ref, vc_ref)
    bias = (qab_ref, None, qbb_ref, None, None, None)
    for ref, b, wd in zip(outs, bias, NN_WIDTHS):
        blk = pn[:, o:o + wd]
        if b is not None:
            blk = blk + b[...]
        ref[...] = blk.astype(BF16)
        o += wd

    pos = (pl.program_id(0) * tm) % seq + lax.broadcasted_iota(I32, (LANES, tm), 1)
    row = lax.broadcasted_iota(I32, (LANES, tm), 0)
    blk_id = pos // KT
    p_hi = (blk_id * KT).astype(F32)
    p_lo = (pos - blk_id * KT).astype(F32)
    aug_a = jnp.where((row >= AUG0) & (row < AUG0 + 3), p_hi,
                      jnp.where((row >= AUG0 + 3) & (row < AUG0 + 6), p_lo, 0.0))
    aug_b = jnp.where((row >= SEL0) & (row - SEL0 == blk_id), 1.0, aug_a)

    def emit(ref, r0, nheads, aug):
        for hh in range(nheads):
            blk = pt[r0 + hh * LANES:r0 + (hh + 1) * LANES, :]
            if aug is not None:
                blk = blk + aug
            blk = blk.astype(BF16)
            for cc in range(tm // KT):
                ref[cc, hh * LANES:(hh + 1) * LANES, :] = blk[:, cc * KT:(cc + 1) * KT]

    emit(kta_ref, 0, A_HEADS, aug_a)
    emit(ktb_ref, NT_ROWS[0], B_HEADS, aug_b)
    emit(ktc_ref, NT_ROWS[0] + NT_ROWS[1], C_KV_HEADS, None)


def _inproj(x2d, mod3, g, wn, wt, qa_bias, qb_bias, *, seq, tm=512):
    t, d = x2d.shape
    assert seq % tm == 0 and tm % KT == 0
    nt = t // tm
    nn_total = sum(NN_WIDTHS)
    row_specs = [pl.BlockSpec((tm, wd), lambda i: (i, 0)) for wd in NN_WIDTHS]
    kt_specs = [pl.BlockSpec((tm // KT, r, KT), lambda i: (i, 0, 0)) for r in NT_ROWS]
    out_shape = ([jax.ShapeDtypeStruct((t, wd), BF16) for wd in NN_WIDTHS]
                 + [jax.ShapeDtypeStruct((t // KT, r, KT), BF16) for r in NT_ROWS])
    return pl.pallas_call(
        functools.partial(_inproj_kernel, tm=tm, seq=seq),
        grid=(nt,),
        in_specs=[
            pl.BlockSpec((tm, d), lambda i: (i, 0)),
            pl.BlockSpec((1, 1, mod3.shape[2]), lambda i: ((i * tm) // seq, 0, 0)),
            pl.BlockSpec((1, d), lambda i: (0, 0)),
            pl.BlockSpec((d, nn_total), lambda i: (0, 0)),
            pl.BlockSpec((sum(NT_ROWS), d), lambda i: (0, 0)),
            pl.BlockSpec((1, NN_WIDTHS[0]), lambda i: (0, 0)),
            pl.BlockSpec((1, NN_WIDTHS[2]), lambda i: (0, 0)),
        ],
        out_specs=row_specs + kt_specs,
        out_shape=out_shape,
        compiler_params=pltpu.CompilerParams(dimension_semantics=("arbitrary",),
                                             vmem_limit_bytes=VMEM_LIMIT),
        name="inproj",
    )(x2d, mod3, g, wn, wt, qa_bias, qb_bias)


def _flash_step(q, kt, v, m_ref, l_ref, acc_ref, mask=None):
    s = jnp.dot(q, kt, preferred_element_type=F32)
    if mask is not None:
        s = jnp.where(mask, s, -jnp.inf)
    m_prev = m_ref[...]
    m_new = jnp.maximum(m_prev, jnp.max(s, axis=-1, keepdims=True))
    alpha = jnp.exp(m_prev - m_new)
    p = jnp.exp(s - m_new)
    l_ref[...] = alpha * l_ref[...] + jnp.sum(p, axis=-1, keepdims=True)
    acc_ref[...] = alpha * acc_ref[...] + jnp.dot(p.astype(BF16), v, preferred_element_type=F32)
    m_ref[...] = m_new


def _flash_init(m_ref, l_ref, acc_ref):
    m_ref[...] = jnp.full(m_ref.shape, -jnp.inf, F32)
    l_ref[...] = jnp.zeros(l_ref.shape, F32)
    acc_ref[...] = jnp.zeros(acc_ref.shape, F32)


def _diff_kernel(q_ref, kt_ref, v_ref, lamv_ref, sg_ref, o_ref, m_ref, l_ref, acc_ref, *, lam_init):
    tq = q_ref.shape[0]
    qi = pl.program_id(2)
    lv = lamv_ref[...]
    lam = (jnp.exp(jnp.sum(lv[0:1] * lv[1:2], axis=-1, keepdims=True))
           - jnp.exp(jnp.sum(lv[2:3] * lv[3:4], axis=-1, keepdims=True)) + lam_init)
    lane = lax.broadcasted_iota(I32, (tq, LANES), 1)
    r2 = lax.broadcasted_iota(I32, (2 * tq, KT), 0)
    c2 = lax.broadcasted_iota(I32, (2 * tq, KT), 1)
    causal = c2 <= jnp.where(r2 >= tq, r2 - tq, r2)
    res = []
    for hh in range(2):
        q = q_ref[:, hh * LANES:(hh + 1) * LANES]
        zero = jnp.zeros_like(q)
        q1 = jnp.where((lane < A_QK_DIM) | (lane >= AUG0), q, zero)
        q2 = jnp.where(lane >= A_QK_DIM, q, zero)
        qs = jnp.concatenate([q1, q2], axis=0)
        _flash_init(m_ref, l_ref, acc_ref)
        _flash_step(qs, kt_ref[qi, hh * LANES:(hh + 1) * LANES, :],
                    v_ref[pl.ds(pl.multiple_of(qi * KT, KT), KT), :], m_ref, l_ref, acc_ref, mask=causal)

        def body(kj, carry):
            _flash_step(qs, kt_ref[kj, hh * LANES:(hh + 1) * LANES, :],
                        v_ref[pl.ds(pl.multiple_of(kj * KT, KT), KT), :], m_ref, l_ref, acc_ref)
            return carry

        lax.fori_loop(0, qi, body, 0)
        o = acc_ref[...] / l_ref[...]
        od = o[:tq] - lam * o[tq:]
        mine = (lane >= hh * HEAD_DIM) & (lane < (hh + 1) * HEAD_DIM)
        ms = jnp.sum(jnp.where(mine, od * od, 0.0), axis=-1, keepdims=True) * (1.0 / HEAD_DIM)
        res.append(od * lax.rsqrt(ms + EPS) * sg_ref[...] * (1.0 - lam_init))
    o_ref[...] = jnp.where(lane < HEAD_DIM, res[0], res[1]).astype(BF16)


def _diff_attention(qa, kta, va, lamv, sg2, *, bsz, seq, lam_init):
    tq = KT
    nq = seq // tq
    t = bsz * seq
    return pl.pallas_call(
        functools.partial(_diff_kernel, lam_init=lam_init),
        grid=(bsz, A_HEADS // 2, nq),
        in_specs=[
            pl.BlockSpec((tq, 2 * LANES), lambda b, hp, qi: (b * nq + qi, hp)),
            pl.BlockSpec((nq, 2 * LANES, KT), lambda b, hp, qi: (b, hp, 0)),
            pl.BlockSpec((seq, LANES), lambda b, hp, qi: (b, hp)),
            pl.BlockSpec((4, A_QK_DIM), lambda b, hp, qi: (0, 0)),
            pl.BlockSpec((1, LANES), lambda b, hp, qi: (0, 0)),
        ],
        out_specs=pl.BlockSpec((tq, LANES), lambda b, hp, qi: (b * nq + qi, hp)),
        out_shape=jax.ShapeDtypeStruct((t, A_HEADS * HEAD_DIM), BF16),
        scratch_shapes=[pltpu.VMEM((2 * tq, 1), F32), pltpu.VMEM((2 * tq, 1), F32),
                        pltpu.VMEM((2 * tq, LANES), F32)],
        compiler_params=pltpu.CompilerParams(dimension_semantics=("arbitrary",) * 3,
                                             vmem_limit_bytes=VMEM_LIMIT),
        name="diff_attn",
    )(qa, kta, va, lamv, sg2)


def _moba_kernel(q_ref, kt_ref, v_ref, o_ref, m_ref, l_ref, acc_ref, km_ref):
    tq = q_ref.shape[0]
    nb = kt_ref.shape[0]
    qi = pl.program_id(2)

    @pl.when(qi == 0)
    def _():
        rr = lax.broadcasted_iota(I32, (KT, LANES), 1)
        for hh in range(2):
            def add_block(j, acc):
                pool = jnp.where(rr == SEL0 + j, 1.0 / KT, 0.0).astype(BF16)
                return acc + jnp.dot(kt_ref[j, hh * LANES:hh * LANES + HEAD_DIM, :], pool,
                                     preferred_element_type=F32)
            km = lax.fori_loop(0, nb, add_block, jnp.zeros((HEAD_DIM, LANES), F32))
            km_ref[hh] = jnp.concatenate([km, jnp.zeros((LANES - HEAD_DIM, LANES), F32)], axis=0)

    lane = lax.broadcasted_iota(I32, (tq, LANES), 1)
    r2 = lax.broadcasted_iota(I32, (tq, KT), 0)
    c2 = lax.broadcasted_iota(I32, (tq, KT), 1)
    causal = c2 <= r2
    in_sel = (lane >= SEL0) & (lane < SEL0 + MAX_BLOCKS)
    res = []
    for hh in range(2):
        q = q_ref[:, hh * LANES:(hh + 1) * LANES]
        km = km_ref[hh]
        km_hi = km.astype(BF16)
        km_lo = (km - km_hi.astype(F32)).astype(BF16)
        gate = (jnp.dot(q, km_hi, preferred_element_type=F32)
                + jnp.dot(q, km_lo, preferred_element_type=F32))
        cur = jnp.where((lane >= SEL0) & (lane < SEL0 + qi), gate, -jnp.inf)
        sel = jnp.zeros((tq, LANES), jnp.bool_)
        for _ in range(MOBA_TOPK):
            mx = jnp.max(cur, axis=-1, keepdims=True)
            first = jnp.min(jnp.where(cur == mx, lane, 4 * LANES), axis=-1, keepdims=True)
            pick = (lane == first) & (mx > -jnp.inf)
            sel = sel | pick
            cur = jnp.where(pick, -jnp.inf, cur)
        q_diag = jnp.where(in_sel, jnp.zeros_like(q), q)
        q_past = jnp.where(in_sel, jnp.where(sel, 0.0, NEG_BIG).astype(BF16), q)
        _flash_init(m_ref, l_ref, acc_ref)
        _flash_step(q_diag, kt_ref[qi, hh * LANES:(hh + 1) * LANES, :],
                    v_ref[pl.ds(pl.multiple_of(qi * KT, KT), KT), :], m_ref, l_ref, acc_ref, mask=causal)

        def body(kj, carry):
            _flash_step(q_past, kt_ref[kj, hh * LANES:(hh + 1) * LANES, :],
                        v_ref[pl.ds(pl.multiple_of(kj * KT, KT), KT), :], m_ref, l_ref, acc_ref)
            return carry

        lax.fori_loop(0, qi, body, 0)
        res.append(acc_ref[...] / l_ref[...])
    o_ref[...] = jnp.where(lane < HEAD_DIM, res[0], res[1]).astype(BF16)


def _moba_attention(qb, ktb, vb, *, bsz, seq):
    tq = KT
    nq = seq // tq
    assert nq <= MAX_BLOCKS
    t = bsz * seq
    return pl.pallas_call(
        _moba_kernel,
        grid=(bsz, B_HEADS // 2, nq),
        in_specs=[
            pl.BlockSpec((tq, 2 * LANES), lambda b, hp, qi: (b * nq + qi, hp)),
            pl.BlockSpec((nq, 2 * LANES, KT), lambda b, hp, qi: (b, hp, 0)),
            pl.BlockSpec((seq, LANES), lambda b, hp, qi: (b, hp)),
        ],
        out_specs=pl.BlockSpec((tq, LANES), lambda b, hp, qi: (b * nq + qi, hp)),
        out_shape=jax.ShapeDtypeStruct((t, B_HEADS * HEAD_DIM), BF16),
        scratch_shapes=[pltpu.VMEM((tq, 1), F32), pltpu.VMEM((tq, 1), F32),
                        pltpu.VMEM((tq, LANES), F32), pltpu.VMEM((2, LANES, LANES), F32)],
        compiler_params=pltpu.CompilerParams(dimension_semantics=("arbitrary",) * 3,
                                             vmem_limit_bytes=VMEM_LIMIT),
        name="moba_attn",
    )(qb, ktb, vb)


def _swa_kernel(sink_ref, q_ref, ktp_ref, ktc_ref, vp_ref, vc_ref, o_ref, *, slopes):
    tq = q_ref.shape[0]
    n = pl.program_id(1)
    lane = lax.broadcasted_iota(I32, (tq, LANES), 1)
    r2 = lax.broadcasted_iota(I32, (tq, 2 * KT), 0)
    c2 = lax.broadcasted_iota(I32, (tq, 2 * KT), 1)
    rel = r2 + KT - c2
    mask = (rel >= 0) & (rel < WINDOW) & ((c2 >= KT) | (n > 0))
    relf = rel.astype(F32)
    res = []
    for hq in range(C_HEADS):
        kv = hq // C_GROUP
        qp = q_ref[:, (hq // 2) * LANES:(hq // 2 + 1) * LANES]
        qm = jnp.where((lane < HEAD_DIM) == (hq % 2 == 0), qp, jnp.zeros_like(qp))
        kt = jnp.concatenate([ktp_ref[0, kv * LANES:(kv + 1) * LANES, :],
                              ktc_ref[0, kv * LANES:(kv + 1) * LANES, :]], axis=1)
        vv = jnp.concatenate([vp_ref[:, kv * LANES:(kv + 1) * LANES],
                              vc_ref[:, kv * LANES:(kv + 1) * LANES]], axis=0)
        s = jnp.dot(qm, kt, preferred_element_type=F32)
        s = jnp.where(mask, s - slopes[hq] * relf, -jnp.inf)
        sink = sink_ref[hq]
        m = jnp.maximum(jnp.max(s, axis=-1, keepdims=True), sink)
        e = jnp.exp(s - m)
        den = jnp.sum(e, axis=-1, keepdims=True) + jnp.exp(sink - m)
        p = (e / den).astype(BF16)
        res.append(jnp.dot(p, vv, preferred_element_type=F32))
    for pr in range(C_HEADS // 2):
        o_ref[:, pr * LANES:(pr + 1) * LANES] = jnp.where(lane < HEAD_DIM, res[2 * pr], res[2 * pr + 1]).astype(BF16)


def _swa_attention(sinks, qc, ktc, vc2, *, bsz, seq, slopes):
    tq = KT
    nq = seq // tq
    t = bsz * seq
    return pl.pallas_call(
        functools.partial(_swa_kernel, slopes=slopes),
        grid=(bsz, nq),
        in_specs=[
            pl.BlockSpec(memory_space=pltpu.SMEM),
            pl.BlockSpec((tq, C_HEADS * HEAD_DIM), lambda b, n: (b * nq + n, 0)),
            pl.BlockSpec((1, C_KV_HEADS * LANES, KT), lambda b, n: (b * nq + jnp.maximum(n - 1, 0), 0, 0)),
            pl.BlockSpec((1, C_KV_HEADS * LANES, KT), lambda b, n: (b * nq + n, 0, 0)),
            pl.BlockSpec((tq, C_KV_HEADS * LANES), lambda b, n: (b * nq + jnp.maximum(n - 1, 0), 0)),
            pl.BlockSpec((tq, C_KV_HEADS * LANES), lambda b, n: (b * nq + n, 0)),
        ],
        out_specs=pl.BlockSpec((tq, C_HEADS * HEAD_DIM), lambda b, n: (b * nq + n, 0)),
        out_shape=jax.ShapeDtypeStruct((t, C_HEADS * HEAD_DIM), BF16),
        compiler_params=pltpu.CompilerParams(dimension_semantics=("arbitrary",) * 2,
                                             vmem_limit_bytes=VMEM_LIMIT),
        name="swa_attn",
    )(sinks, qc, ktc, ktc, vc2, vc2)


def _outproj_kernel(oa_ref, ob_ref, oc_ref, woa_ref, wob_ref, woc_ref, x_ref, mod_ref, g_ref, wq_ref, keys_ref,
                    x1_ref, h2_ref, st_ref):
    d = x_ref.shape[1]
    mix = (jnp.dot(oa_ref[...], woa_ref[...], preferred_element_type=F32)
           + jnp.dot(ob_ref[...], wob_ref[...], preferred_element_type=F32)
           + jnp.dot(oc_ref[...], woc_ref[...], preferred_element_type=F32))
    g1 = mod_ref[0, :, 2 * d:3 * d]
    sh2 = mod_ref[0, :, 3 * d:4 * d]
    sc2 = mod_ref[0, :, 4 * d:5 * d]
    x1 = x_ref[...] + g1 * mix
    x1_ref[...] = x1
    h2 = _rms_mod(x1, g_ref[...], sc2, sh2)
    h2_ref[...] = h2
    pq = jnp.dot(h2.astype(BF16), wq_ref[...], preferred_element_type=F32).astype(BF16)
    for hp in range(2 * PEER_HEADS):
        st_ref[hp] = lax.dot_general(keys_ref[hp], pq[:, hp * LANES:(hp + 1) * LANES],
                                     (((1,), (1,)), ((), ())), preferred_element_type=F32)


def _outproj(oa, ob, oc, woa, wob, woc, x2d, mod3, g, wq, keys, *, seq, tm=256):
    t, d = x2d.shape
    nt = t // tm
    nq = wq.shape[1]
    full = lambda a: pl.BlockSpec(a.shape, lambda i: (0,) * a.ndim)
    return pl.pallas_call(
        _outproj_kernel,
        grid=(nt,),
        in_specs=[
            pl.BlockSpec((tm, oa.shape[1]), lambda i: (i, 0)),
            pl.BlockSpec((tm, ob.shape[1]), lambda i: (i, 0)),
            pl.BlockSpec((tm, oc.shape[1]), lambda i: (i, 0)),
            full(woa), full(wob), full(woc),
            pl.BlockSpec((tm, d), lambda i: (i, 0)),
            pl.BlockSpec((1, 1, mod3.shape[2]), lambda i: ((i * tm) // seq, 0, 0)),
            pl.BlockSpec((1, d), lambda i: (0, 0)),
            full(wq), full(keys),
        ],
        out_specs=[pl.BlockSpec((tm, d), lambda i: (i, 0)),
                   pl.BlockSpec((tm, d), lambda i: (i, 0)),
                   pl.BlockSpec((2 * PEER_HEADS, PEER_NKEYS, tm), lambda i: (0, 0, i))],
        out_shape=[jax.ShapeDtypeStruct((t, d), F32), jax.ShapeDtypeStruct((t, d), F32),
                   jax.ShapeDtypeStruct((2 * PEER_HEADS, PEER_NKEYS, t), F32)],
        compiler_params=pltpu.CompilerParams(dimension_semantics=("arbitrary",),
                                             vmem_limit_bytes=VMEM_LIMIT),
        name="outproj_peerq",
    )(oa, ob, oc, woa, wob, woc, x2d, mod3, g, wq, keys)


_CAND_BLOCKS = ((0, 16),) + tuple((i, 8) for i in range(1, 8))
_CAND_ROWS = 16 + 7 * 8 + 8
_BIG_I = np.int32(2 ** 30)


def _topk_rows(cur, key, val_ref, key_ref):
    def body(r, cur):
        m = jnp.max(cur, axis=0, keepdims=True)
        kmin = jnp.min(jnp.where(cur == m, key, _BIG_I), axis=0, keepdims=True)
        val_ref[pl.ds(r, 1), :] = m
        key_ref[pl.ds(r, 1), :] = kmin
        return jnp.where(key == kmin, -jnp.inf, cur)
    lax.fori_loop(0, PEER_TOPK, body, cur)


def _peer_topk_kernel(st_ref, eidx_ref, g_ref, v1_ref, k1_ref, v2_ref, k2_ref, vt_ref, kt_ref, ei_ref, gg_ref):
    tl = st_ref.shape[2]
    row = lax.broadcasted_iota(I32, (PEER_NKEYS, tl), 0)
    for h in range(PEER_HEADS):
        _topk_rows(st_ref[2 * h], row, v1_ref, k1_ref)
        _topk_rows(st_ref[2 * h + 1], row, v2_ref, k2_ref)
        sv1, si1, sv2, si2 = v1_ref[...], k1_ref[...], v2_ref[...], k2_ref[...]
        jrow8 = lax.broadcasted_iota(I32, (8, tl), 0)
        jrow16 = lax.broadcasted_iota(I32, (16, tl), 0)
        cands, keys = [], []
        for i, nj in _CAND_BLOCKS:
            jrow = jrow16 if nj == 16 else jrow8
            cands.append(sv1[i:i + 1] + sv2[0:nj])
            keys.append((i * PEER_TOPK + jrow) * (PEER_NKEYS * PEER_NKEYS) + si1[i:i + 1] * PEER_NKEYS + si2[0:nj])
        cands.append(sv1[8:16] + sv2[0:1])
        keys.append((jrow8 + 8) * (PEER_TOPK * PEER_NKEYS * PEER_NKEYS) + si1[8:16] * PEER_NKEYS + si2[0:1])
        cand = jnp.concatenate(cands, axis=0)
        ckey = jnp.concatenate(keys, axis=0)
        _topk_rows(cand, ckey, vt_ref, kt_ref)
        top = vt_ref[...]
        e = jnp.exp(top - top[0:1])
        gg_ref[h * PEER_TOPK:(h + 1) * PEER_TOPK, :] = e / jnp.sum(e, axis=0, keepdims=True)
        ei_ref[h * PEER_TOPK:(h + 1) * PEER_TOPK, :] = kt_ref[...] & (PEER_NKEYS * PEER_NKEYS - 1)
    eidx_ref[...] = ei_ref[...].T
    g_ref[...] = gg_ref[...].T


def _peer_topk(st, *, tl=128):
    t = st.shape[2]
    hk = PEER_HEADS * PEER_TOPK
    return pl.pallas_call(
        _peer_topk_kernel,
        grid=(t // tl,),
        in_specs=[pl.BlockSpec((2 * PEER_HEADS, PEER_NKEYS, tl), lambda i: (0, 0, i))],
        out_specs=[pl.BlockSpec((tl, hk), lambda i: (i, 0)), pl.BlockSpec((tl, hk), lambda i: (i, 0))],
        out_shape=[jax.ShapeDtypeStruct((t, hk), I32), jax.ShapeDtypeStruct((t, hk), F32)],
        scratch_shapes=[pltpu.VMEM((PEER_TOPK, tl), F32), pltpu.VMEM((PEER_TOPK, tl), I32),
                        pltpu.VMEM((PEER_TOPK, tl), F32), pltpu.VMEM((PEER_TOPK, tl), I32),
                        pltpu.VMEM((PEER_TOPK, tl), F32), pltpu.VMEM((PEER_TOPK, tl), I32),
                        pltpu.VMEM((hk, tl), I32), pltpu.VMEM((hk, tl), F32)],
        compiler_params=pltpu.CompilerParams(dimension_semantics=("arbitrary",),
                                             vmem_limit_bytes=VMEM_LIMIT),
        name="peer_topk",
    )(st)


def _erf(x):
    return lax.erf(x)


def _peer_gather_kernel(idx_ref, idxn_ref, h_ref, g_ref, x1_ref, mod_ref, fg_ref, uv_hbm,
                        o_ref, buf, sem, *, tq, final):
    i = pl.program_id(0)
    n = pl.num_programs(0)
    rows = tq * PEER_HEADS * PEER_TOPK
    d = h_ref.shape[1]

    def issue(iref, slot):
        def one(r, carry):
            pltpu.make_async_copy(uv_hbm.at[iref[r]], buf.at[slot, r], sem.at[slot]).start()
            return carry
        lax.fori_loop(0, rows, one, 0, unroll=8)

    @pl.when(i == 0)
    def _():
        issue(idx_ref, 0)

    slot = i % 2

    @pl.when(i + 1 < n)
    def _():
        issue(idxn_ref, 1 - slot)

    pltpu.make_async_copy(uv_hbm.at[pl.ds(0, rows)], buf.at[slot], sem.at[slot]).wait()

    hk = PEER_HEADS * PEER_TOPK
    gt = g_ref[...].T
    outs = []
    for tt in range(tq):
        ue = buf[slot, tt * hk:(tt + 1) * hk, 0:d]
        ve = buf[slot, tt * hk:(tt + 1) * hk, d:2 * d]
        a = jnp.sum(ue * h_ref[tt:tt + 1, :], axis=-1, keepdims=True)
        act = 0.5 * a * (1.0 + _erf(a * (2.0 ** -0.5)))
        w = gt[:, tt:tt + 1] * act
        outs.append(jnp.sum(w * ve, axis=0, keepdims=True))
    peer = jnp.concatenate(outs, axis=0)
    g2 = mod_ref[0, :, 5 * d:6 * d]
    y = x1_ref[...] + g2 * peer
    if final:
        ms = jnp.mean(y * y, axis=-1, keepdims=True)
        y = y * lax.rsqrt(ms + EPS) * fg_ref[...]
    o_ref[...] = y


def _peer_gather(eidx_flat, h2, g, x1, mod3, final_g, uv, *, seq, final, tq=8):
    t, d = h2.shape
    hk = PEER_HEADS * PEER_TOPK
    rows = tq * hk
    n = t // tq
    return pl.pallas_call(
        functools.partial(_peer_gather_kernel, tq=tq, final=final),
        grid=(n,),
        in_specs=[
            pl.BlockSpec((rows,), lambda i: (i,), memory_space=pltpu.SMEM),
            pl.BlockSpec((rows,), lambda i: (jnp.minimum(i + 1, n - 1),), memory_space=pltpu.SMEM),
            pl.BlockSpec((tq, d), lambda i: (i, 0)),
            pl.BlockSpec((tq, hk), lambda i: (i, 0)),
            pl.BlockSpec((tq, d), lambda i: (i, 0)),
            pl.BlockSpec((1, 1, mod3.shape[2]), lambda i: ((i * tq) // seq, 0, 0)),
            pl.BlockSpec((1, d), lambda i: (0, 0)),
            pl.BlockSpec(memory_space=pl.ANY),
        ],
        out_specs=pl.BlockSpec((tq, d), lambda i: (i, 0)),
        out_shape=jax.ShapeDtypeStruct((t, d), F32),
        scratch_shapes=[pltpu.VMEM((2, rows, 2 * d), F32), pltpu.SemaphoreType.DMA((2,))],
        compiler_params=pltpu.CompilerParams(dimension_semantics=("arbitrary",),
                                             vmem_limit_bytes=VMEM_LIMIT),
        name="peer_gather",
    )(eidx_flat, eidx_flat, h2, g, x1, mod3, final_g, uv)


def kernel(x, c, norm1_g, norm2_g, w_ada, b_ada, w_in, w_out, lam_q1, lam_k1, lam_q2, lam_k2, subln_g, sinks,
           peer_wq, peer_keys, peer_u, peer_v, final_g):
    bsz, seq, d = x.shape
    depth = w_in.shape[0]
    t = bsz * seq
    slopes = _alibi_slopes()
    sl_c = [float(s) for s in slopes[:C_HEADS]]
    qa_bias = _slope_bias_row(slopes[C_HEADS:C_HEADS + A_HEADS])
    qb_bias = _slope_bias_row(slopes[C_HEADS + A_HEADS:])
    mods = _adaln_mods(c, w_ada, b_ada)
    x2d = x.reshape(t, d)
    fg = final_g.reshape(1, d)
    av, bw = A_HEADS * HEAD_DIM, B_HEADS * HEAD_DIM
    for l in range(depth):
        lam_init = 0.8 - 0.6 * math.exp(-0.3 * l)
        mod3 = mods[l].reshape(bsz, 1, N_MOD * d)
        wn, wt = _prep_in_weights(w_in[l])
        qa, va, qb, vb, qc, vc2, kta, ktb, ktc = _inproj(
            x2d, mod3, norm1_g[l].reshape(1, d), wn, wt, qa_bias, qb_bias, seq=seq)
        lamv = jnp.stack([lam_q1[l], lam_k1[l], lam_q2[l], lam_k2[l]]).astype(F32)
        sg2 = jnp.concatenate([subln_g[l], subln_g[l]]).reshape(1, LANES).astype(F32)
        oa = _diff_attention(qa, kta, va, lamv, sg2, bsz=bsz, seq=seq, lam_init=lam_init)
        ob = _moba_attention(qb, ktb, vb, bsz=bsz, seq=seq)
        oc = _swa_attention(sinks[l].astype(F32), qc, ktc, vc2, bsz=bsz, seq=seq, slopes=sl_c)
        wo = w_out[l].astype(BF16)
        keys = peer_keys[l].reshape(2 * PEER_HEADS, PEER_NKEYS, -1).astype(BF16)
        x1, h2, st = _outproj(oa, ob, oc, wo[:av], wo[av:av + bw], wo[av + bw:], x2d, mod3,
                              norm2_g[l].reshape(1, d), peer_wq[l].astype(BF16), keys, seq=seq)
        eidx, g = _peer_topk(st)
        uv = jnp.concatenate([peer_u[l], peer_v[l]], axis=1)
        x2d = _peer_gather(eidx.reshape(-1), h2, g, x1, mod3, fg, uv, seq=seq, final=(l == depth - 1))
    return x2d.reshape(bsz, seq, d)
```

```python
import functools
import math

import numpy as np
import jax
import jax.numpy as jnp
from jax import lax
from jax.experimental import pallas as pl
from jax.experimental.pallas import tpu as pltpu

F32 = jnp.float32
BF16 = jnp.bfloat16
I32 = jnp.int32

D_MODEL = 1024
HEAD_DIM = 64
N_HEADS_TOTAL = 16
A_HEADS = 4
B_HEADS = 6
C_HEADS = 6
C_KV_HEADS = 2
C_GROUP = 3
A_QK_DIM = 32
MOBA_BLOCK = 256
MOBA_TOPK = 3
WINDOW = 128
ALIBI_MAX = 8.0
PEER_HEADS = 8
PEER_NKEYS = 128
PEER_TOPK = 16
N_MOD = 6
EPS = 1e-6

LANES = 128
KT = 256
AUG0 = HEAD_DIM
SEL0 = AUG0 + 6
MAX_BLOCKS = LANES - SEL0
NEG_BIG = -1e30
VMEM_LIMIT = 56 * 1024 * 1024


def _alibi_slopes():
    n = N_HEADS_TOTAL
    return (2.0 ** (-ALIBI_MAX * np.arange(1, n + 1, dtype=np.float32) / n)).astype(np.float32)


def _split3(v):
    v = np.float32(v)
    hi = np.float32(np.asarray(v).astype(jnp.bfloat16).astype(np.float32))
    r = np.float32(v - hi)
    mid = np.float32(np.asarray(r).astype(jnp.bfloat16).astype(np.float32))
    lo = np.float32(np.float32(r - mid))
    lo = np.float32(np.asarray(lo).astype(jnp.bfloat16).astype(np.float32))
    return hi, mid, lo


def _slope_bias_col(slopes):
    col = np.zeros((LANES * len(slopes), 1), np.float32)
    for h, s in enumerate(slopes):
        hi, mid, lo = _split3(s)
        col[h * LANES + AUG0:h * LANES + AUG0 + 6, 0] = [hi, mid, lo, hi, mid, lo]
    return jnp.asarray(col)


def _mod_kernel(c_ref, w_ref, b_ref, o_ref):
    c = c_ref[...]
    cs = c * (1.0 / (1.0 + jnp.exp(-c)))
    o_ref[0] = jnp.dot(cs, w_ref[0], preferred_element_type=F32) + b_ref[0]


def _adaln_mods(c, w_ada, b_ada):
    depth, d, n = w_ada.shape
    bsz = c.shape[0]
    rows = -(-bsz // 8) * 8
    cp = jnp.pad(c, ((0, rows - bsz), (0, 0)))
    tn = 1536
    out = pl.pallas_call(
        _mod_kernel,
        grid=(depth, n // tn),
        in_specs=[
            pl.BlockSpec((rows, d), lambda l, j: (0, 0)),
            pl.BlockSpec((1, d, tn), lambda l, j: (l, 0, j)),
            pl.BlockSpec((1, 1, tn), lambda l, j: (l, 0, j)),
        ],
        out_specs=pl.BlockSpec((1, rows, tn), lambda l, j: (l, 0, j)),
        out_shape=jax.ShapeDtypeStruct((depth, rows, n), F32),
        compiler_params=pltpu.CompilerParams(dimension_semantics=("arbitrary", "arbitrary"),
                                             vmem_limit_bytes=VMEM_LIMIT),
        name="adaln_mods",
    )(cp, w_ada, b_ada.reshape(depth, 1, n))
    return out[:, :bsz]


NN_WIDTHS = (A_HEADS * LANES, B_HEADS * LANES, C_HEADS * HEAD_DIM, 2 * C_KV_HEADS * HEAD_DIM)
NT_ROWS = (A_HEADS * LANES, A_HEADS * HEAD_DIM, B_HEADS * LANES, B_HEADS * HEAD_DIM,
           C_KV_HEADS * LANES)


def _prep_in_weights(w):
    d = w.shape[0]
    aq, ak, av = A_HEADS * 2 * A_QK_DIM, A_HEADS * 2 * A_QK_DIM, A_HEADS * HEAD_DIM
    bw = B_HEADS * HEAD_DIM
    cq, ckv = C_HEADS * HEAD_DIM, C_KV_HEADS * HEAD_DIM
    cuts = np.cumsum([aq, ak, av, bw, bw, bw, cq, ckv]).tolist()
    qa, ka, va, qb, kb, vb, qc, kc, vc = jnp.split(w, cuts, axis=-1)

    def pad_heads(m, nh, scale):
        m = (m * scale).reshape(d, nh, HEAD_DIM)
        return jnp.pad(m, ((0, 0), (0, 0), (0, LANES - HEAD_DIM))).reshape(d, nh * LANES)

    vc2 = vc.reshape(d, C_KV_HEADS, 1, HEAD_DIM)
    vc2 = jnp.broadcast_to(vc2, (d, C_KV_HEADS, 2, HEAD_DIM)).reshape(d, 2 * ckv)
    kc2 = jnp.broadcast_to(kc.reshape(d, C_KV_HEADS, 1, HEAD_DIM), (d, C_KV_HEADS, 2, HEAD_DIM)).reshape(d, 2 * ckv)
    wn = jnp.concatenate([pad_heads(ka, A_HEADS, 1.0), pad_heads(kb, B_HEADS, 1.0),
                          qc * (HEAD_DIM ** -0.5), vc2], axis=1)
    wt = jnp.concatenate([pad_heads(qa, A_HEADS, A_QK_DIM ** -0.5), va,
                          pad_heads(qb, B_HEADS, HEAD_DIM ** -0.5), vb, kc2], axis=1).T
    return wn.astype(BF16), wt.astype(BF16)


def _rms_mod(x, g, sc, sh):
    ms = jnp.mean(x * x, axis=-1, keepdims=True)
    return (x * lax.rsqrt(ms + EPS) * g) * (1.0 + sc) + sh


def _inproj_kernel(x_ref, mod_ref, g_ref, wn_ref, wt_ref, qab_ref, qbb_ref,
                   ka_ref, kb_ref, qc_ref, vc_ref, qta_ref, vta_ref, qtb_ref, vtb_ref, ktc_ref,
                   *, tm, seq):
    d = x_ref.shape[1]
    x = x_ref[...]
    sh = mod_ref[0, :, 0:d]
    sc = mod_ref[0, :, d:2 * d]
    h = _rms_mod(x, g_ref[...], sc, sh).astype(BF16)
    pn = jnp.dot(h, wn_ref[...], preferred_element_type=F32)
    pt = lax.dot_general(wt_ref[...], h, (((1,), (1,)), ((), ())),
                         preferred_element_type=F32)

    pos = (pl.program_id(0) * tm) % seq + lax.broadcasted_iota(I32, (tm, LANES), 0)
    col = lax.broadcasted_iota(I32, (tm, LANES), 1)
    blk_id = pos // KT
    p_hi = (blk_id * KT).astype(F32)
    p_lo = (pos - blk_id * KT).astype(F32)
    aug_a = jnp.where((col >= AUG0) & (col < AUG0 + 3), p_hi,
                      jnp.where((col >= AUG0 + 3) & (col < AUG0 + 6), p_lo, 0.0))
    aug_b = jnp.where((col >= SEL0) & (col - SEL0 == blk_id), 1.0, aug_a)
    for hh in range(A_HEADS):
        ka_ref[:, hh * LANES:(hh + 1) * LANES] = (pn[:, hh * LANES:(hh + 1) * LANES] + aug_a).astype(BF16)
    o = NN_WIDTHS[0]
    for hh in range(B_HEADS):
        kb_ref[:, hh * LANES:(hh + 1) * LANES] = (pn[:, o + hh * LANES:o + (hh + 1) * LANES] + aug_b).astype(BF16)
    o += NN_WIDTHS[1]
    qc_ref[...] = pn[:, o:o + NN_WIDTHS[2]].astype(BF16)
    o += NN_WIDTHS[2]
    vc_ref[...] = pn[:, o:o + NN_WIDTHS[3]].astype(BF16)

    r0 = 0
    for ref, b, nr in zip((qta_ref, vta_ref, qtb_ref, vtb_ref, ktc_ref),
                          (qab_ref, None, qbb_ref, None, None), NT_ROWS):
        blk = pt[r0:r0 + nr, :]
        if b is not None:
            blk = blk + b[...]
        blk = blk.astype(BF16)
        for cc in range(tm // KT):
            ref[cc] = blk[:, cc * KT:(cc + 1) * KT]
        r0 += nr


def _inproj(x2d, mod3, g, wn, wt, qa_bias, qb_bias, *, seq, tm=512):
    t, d = x2d.shape
    assert seq % tm == 0 and tm % KT == 0
    nt = t // tm
    nn_total = sum(NN_WIDTHS)
    row_specs = [pl.BlockSpec((tm, wd), lambda i: (i, 0)) for wd in NN_WIDTHS]
    kt_specs = [pl.BlockSpec((tm // KT, r, KT), lambda i: (i, 0, 0)) for r in NT_ROWS]
    out_shape = ([jax.ShapeDtypeStruct((t, wd), BF16) for wd in NN_WIDTHS]
                 + [jax.ShapeDtypeStruct((t // KT, r, KT), BF16) for r in NT_ROWS])
    return pl.pallas_call(
        functools.partial(_inproj_kernel, tm=tm, seq=seq),
        grid=(nt,),
        in_specs=[
            pl.BlockSpec((tm, d), lambda i: (i, 0)),
            pl.BlockSpec((1, 1, mod3.shape[2]), lambda i: ((i * tm) // seq, 0, 0)),
            pl.BlockSpec((1, d), lambda i: (0, 0)),
            pl.BlockSpec((d, nn_total), lambda i: (0, 0)),
            pl.BlockSpec((sum(NT_ROWS), d), lambda i: (0, 0)),
            pl.BlockSpec((NT_ROWS[0], 1), lambda i: (0, 0)),
            pl.BlockSpec((NT_ROWS[2], 1), lambda i: (0, 0)),
        ],
        out_specs=row_specs + kt_specs,
        out_shape=out_shape,
        compiler_params=pltpu.CompilerParams(dimension_semantics=("arbitrary",),
                                             vmem_limit_bytes=VMEM_LIMIT),
        name="inproj",
    )(x2d, mod3, g, wn, wt, qa_bias, qb_bias)


def _flash_step(k, qt, vt, m_ref, l_ref, acc_ref, mask=None):
    s = jnp.dot(k, qt, preferred_element_type=F32)
    if mask is not None:
        s = jnp.where(mask, s, -jnp.inf)
    m_prev = m_ref[...]
    m_new = jnp.maximum(m_prev, jnp.max(s, axis=0, keepdims=True))
    alpha = jnp.exp(m_prev - m_new)
    p = jnp.exp(s - m_new)
    l_ref[...] = alpha * l_ref[...] + jnp.sum(p, axis=0, keepdims=True)
    acc_ref[...] = alpha * acc_ref[...] + jnp.dot(vt, p.astype(BF16), preferred_element_type=F32)
    m_ref[...] = m_new


def _flash_init(m_ref, l_ref, acc_ref):
    m_ref[...] = jnp.full(m_ref.shape, -jnp.inf, F32)
    l_ref[...] = jnp.zeros(l_ref.shape, F32)
    acc_ref[...] = jnp.zeros(acc_ref.shape, F32)


PAST_TILES = 2


def _flash_causal_pair(k_ref, vt_ref, q_diag, q_past, qi, causal, m_ref, l_ref, acc_ref):
    _flash_init(m_ref, l_ref, acc_ref)

    def step(kj, n, qs, mask):
        kk = k_ref[pl.ds(pl.multiple_of(kj * KT, KT), n * KT), :]
        for hh in range(2):
            rows = slice(hh * HEAD_DIM, (hh + 1) * HEAD_DIM)
            vt = vt_ref[kj, rows, :] if n == 1 else jnp.concatenate(
                [vt_ref[kj + c, rows, :] for c in range(n)], axis=1)
            _flash_step(kk[:, hh * LANES:(hh + 1) * LANES], qs[hh], vt,
                        m_ref.at[hh], l_ref.at[hh], acc_ref.at[hh], mask)

    step(qi, 1, q_diag, causal)

    def body(j, carry):
        step(j * PAST_TILES, PAST_TILES, q_past, None)
        return carry

    lax.fori_loop(0, qi // PAST_TILES, body, 0)
    for r in range(PAST_TILES - 1):
        @pl.when(qi % PAST_TILES > r)
        def _():
            step(qi - 1 - r, 1, q_past, None)


def _diff_kernel(qt_ref, k_ref, vt_ref, lamv_ref, sg_ref, o_ref, m_ref, l_ref, acc_ref, *, lam_init):
    tq = qt_ref.shape[2]
    qi = pl.program_id(2)
    lv = lamv_ref[...]
    lam = (jnp.exp(jnp.sum(lv[0:1] * lv[1:2], axis=-1, keepdims=True))
           - jnp.exp(jnp.sum(lv[2:3] * lv[3:4], axis=-1, keepdims=True)) + lam_init)
    row = lax.broadcasted_iota(I32, (LANES, tq), 0)
    kr = lax.broadcasted_iota(I32, (KT, 2 * tq), 0)
    qc = lax.broadcasted_iota(I32, (KT, 2 * tq), 1)
    causal = kr <= jnp.where(qc >= tq, qc - tq, qc)
    qs = []
    for hh in range(2):
        qt = qt_ref[0, hh * LANES:(hh + 1) * LANES, :]
        zero = jnp.zeros_like(qt)
        q1 = jnp.where((row < A_QK_DIM) | (row >= AUG0), qt, zero)
        q2 = jnp.where(row >= A_QK_DIM, qt, zero)
        qs.append(jnp.concatenate([q1, q2], axis=1))
    _flash_causal_pair(k_ref, vt_ref, qs, qs, qi, causal, m_ref, l_ref, acc_ref)
    res = []
    for hh in range(2):
        o = acc_ref[hh] / l_ref[hh]
        od = o[:, :tq] - lam * o[:, tq:]
        ms = jnp.mean(od * od, axis=0, keepdims=True)
        res.append(od * lax.rsqrt(ms + EPS) * sg_ref[...] * (1.0 - lam_init))
    o_ref[...] = jnp.concatenate(res, axis=0).T.astype(BF16)


def _diff_attention(qta, ka, vta, lamv, sg, *, bsz, seq, lam_init):
    tq = KT
    nq = seq // tq
    t = bsz * seq
    return pl.pallas_call(
        functools.partial(_diff_kernel, lam_init=lam_init),
        grid=(bsz, A_HEADS // 2, nq),
        in_specs=[
            pl.BlockSpec((1, 2 * LANES, tq), lambda b, hp, qi: (b * nq + qi, hp, 0)),
            pl.BlockSpec((seq, 2 * LANES), lambda b, hp, qi: (b, hp)),
            pl.BlockSpec((nq, 2 * HEAD_DIM, KT), lambda b, hp, qi: (b, hp, 0)),
            pl.BlockSpec((4, A_QK_DIM), lambda b, hp, qi: (0, 0)),
            pl.BlockSpec((HEAD_DIM, 1), lambda b, hp, qi: (0, 0)),
        ],
        out_specs=pl.BlockSpec((tq, LANES), lambda b, hp, qi: (b * nq + qi, hp)),
        out_shape=jax.ShapeDtypeStruct((t, A_HEADS * HEAD_DIM), BF16),
        scratch_shapes=[pltpu.VMEM((2, 1, 2 * tq), F32), pltpu.VMEM((2, 1, 2 * tq), F32),
                        pltpu.VMEM((2, HEAD_DIM, 2 * tq), F32)],
        compiler_params=pltpu.CompilerParams(dimension_semantics=("arbitrary",) * 3,
                                             vmem_limit_bytes=VMEM_LIMIT),
        name="diff_attn",
    )(qta, ka, vta, lamv, sg)


def _moba_kernel(qt_ref, k_ref, vt_ref, o_ref, m_ref, l_ref, acc_ref, km_ref):
    tq = qt_ref.shape[2]
    nb = vt_ref.shape[0]
    qi = pl.program_id(2)

    @pl.when(qi == 0)
    def _():
        lane1 = lax.broadcasted_iota(I32, (1, LANES), 1)
        for hh in range(2):
            km_ref[hh] = jnp.zeros((LANES, LANES), F32)

            def put_block(j, carry):
                blk = k_ref[pl.ds(pl.multiple_of(j * KT, KT), KT), hh * LANES:(hh + 1) * LANES].astype(F32)
                mean = jnp.sum(blk, axis=0, keepdims=True) * (1.0 / KT)
                km_ref[hh, pl.ds(SEL0 + j, 1), :] = jnp.where(lane1 < HEAD_DIM, mean, 0.0)
                return carry
            lax.fori_loop(0, nb, put_block, 0)

    row = lax.broadcasted_iota(I32, (LANES, tq), 0)
    kr = lax.broadcasted_iota(I32, (KT, tq), 0)
    qc = lax.broadcasted_iota(I32, (KT, tq), 1)
    causal = kr <= qc
    in_sel = (row >= SEL0) & (row < SEL0 + MAX_BLOCKS)
    q_diag, q_past = [], []
    for hh in range(2):
        qt = qt_ref[0, hh * LANES:(hh + 1) * LANES, :]
        km = km_ref[hh]
        km_hi = km.astype(BF16)
        km_lo = (km - km_hi.astype(F32)).astype(BF16)
        gate = (jnp.dot(km_hi, qt, preferred_element_type=F32)
                + jnp.dot(km_lo, qt, preferred_element_type=F32))
        cur = jnp.where((row >= SEL0) & (row < SEL0 + qi), gate, -jnp.inf)
        sel = jnp.zeros((LANES, tq), jnp.bool_)
        for _ in range(MOBA_TOPK):
            mx = jnp.max(cur, axis=0, keepdims=True)
            first = jnp.min(jnp.where(cur == mx, row, 4 * LANES), axis=0, keepdims=True)
            pick = (row == first) & (mx > -jnp.inf)
            sel = sel | pick
            cur = jnp.where(pick, -jnp.inf, cur)
        q_diag.append(jnp.where(in_sel, jnp.zeros_like(qt), qt))
        q_past.append(jnp.where(in_sel, jnp.where(sel, 0.0, NEG_BIG).astype(BF16), qt))
    _flash_causal_pair(k_ref, vt_ref, q_diag, q_past, qi, causal, m_ref, l_ref, acc_ref)
    res = [acc_ref[hh] / l_ref[hh] for hh in range(2)]
    o_ref[...] = jnp.concatenate(res, axis=0).T.astype(BF16)


def _moba_attention(qtb, kb, vtb, *, bsz, seq):
    tq = KT
    nq = seq // tq
    assert nq <= MAX_BLOCKS
    t = bsz * seq
    return pl.pallas_call(
        _moba_kernel,
        grid=(bsz, B_HEADS // 2, nq),
        in_specs=[
            pl.BlockSpec((1, 2 * LANES, tq), lambda b, hp, qi: (b * nq + qi, hp, 0)),
            pl.BlockSpec((seq, 2 * LANES), lambda b, hp, qi: (b, hp)),
            pl.BlockSpec((nq, 2 * HEAD_DIM, KT), lambda b, hp, qi: (b, hp, 0)),
        ],
        out_specs=pl.BlockSpec((tq, LANES), lambda b, hp, qi: (b * nq + qi, hp)),
        out_shape=jax.ShapeDtypeStruct((t, B_HEADS * HEAD_DIM), BF16),
        scratch_shapes=[pltpu.VMEM((2, 1, tq), F32), pltpu.VMEM((2, 1, tq), F32),
                        pltpu.VMEM((2, HEAD_DIM, tq), F32), pltpu.VMEM((2, LANES, LANES), F32)],
        compiler_params=pltpu.CompilerParams(dimension_semantics=("arbitrary",) * 3,
                                             vmem_limit_bytes=VMEM_LIMIT),
        name="moba_attn",
    )(qtb, kb, vtb)


def _swa_kernel(sink_ref, q_ref, ktp_ref, ktc_ref, vp_ref, vc_ref, o_ref, *, slopes):
    tq = q_ref.shape[0]
    n = pl.program_id(1)
    lane = lax.broadcasted_iota(I32, (tq, LANES), 1)
    r2 = lax.broadcasted_iota(I32, (tq, 2 * KT), 0)
    c2 = lax.broadcasted_iota(I32, (tq, 2 * KT), 1)
    rel = r2 + KT - c2
    mask = (rel >= 0) & (rel < WINDOW) & ((c2 >= KT) | (n > 0))
    relf = rel.astype(F32)
    res = []
    for hq in range(C_HEADS):
        kv = hq // C_GROUP
        qp = q_ref[:, (hq // 2) * LANES:(hq // 2 + 1) * LANES]
        qm = jnp.where((lane < HEAD_DIM) == (hq % 2 == 0), qp, jnp.zeros_like(qp))
        kt = jnp.concatenate([ktp_ref[0, kv * LANES:(kv + 1) * LANES, :],
                              ktc_ref[0, kv * LANES:(kv + 1) * LANES, :]], axis=1)
        vv = jnp.concatenate([vp_ref[:, kv * LANES:(kv + 1) * LANES],
                              vc_ref[:, kv * LANES:(kv + 1) * LANES]], axis=0)
        s = jnp.dot(qm, kt, preferred_element_type=F32)
        s = jnp.where(mask, s - slopes[hq] * relf, -jnp.inf)
        sink = sink_ref[hq]
        m = jnp.maximum(jnp.max(s, axis=-1, keepdims=True), sink)
        e = jnp.exp(s - m)
        den = jnp.sum(e, axis=-1, keepdims=True) + jnp.exp(sink - m)
        p = (e / den).astype(BF16)
        res.append(jnp.dot(p, vv, preferred_element_type=F32))
    for pr in range(C_HEADS // 2):
        o_ref[:, pr * LANES:(pr + 1) * LANES] = jnp.where(lane < HEAD_DIM, res[2 * pr], res[2 * pr + 1]).astype(BF16)


def _swa_attention(sinks, qc, ktc, vc2, *, bsz, seq, slopes):
    tq = KT
    nq = seq // tq
    t = bsz * seq
    return pl.pallas_call(
        functools.partial(_swa_kernel, slopes=slopes),
        grid=(bsz, nq),
        in_specs=[
            pl.BlockSpec(memory_space=pltpu.SMEM),
            pl.BlockSpec((tq, C_HEADS * HEAD_DIM), lambda b, n: (b * nq + n, 0)),
            pl.BlockSpec((1, C_KV_HEADS * LANES, KT), lambda b, n: (b * nq + jnp.maximum(n - 1, 0), 0, 0)),
            pl.BlockSpec((1, C_KV_HEADS * LANES, KT), lambda b, n: (b * nq + n, 0, 0)),
            pl.BlockSpec((tq, C_KV_HEADS * LANES), lambda b, n: (b * nq + jnp.maximum(n - 1, 0), 0)),
            pl.BlockSpec((tq, C_KV_HEADS * LANES), lambda b, n: (b * nq + n, 0)),
        ],
        out_specs=pl.BlockSpec((tq, C_HEADS * HEAD_DIM), lambda b, n: (b * nq + n, 0)),
        out_shape=jax.ShapeDtypeStruct((t, C_HEADS * HEAD_DIM), BF16),
        compiler_params=pltpu.CompilerParams(dimension_semantics=("arbitrary",) * 2,
                                             vmem_limit_bytes=VMEM_LIMIT),
        name="swa_attn",
    )(sinks, qc, ktc, ktc, vc2, vc2)


def _outproj_kernel(oa_ref, ob_ref, oc_ref, woa_ref, wob_ref, woc_ref, x_ref, mod_ref, g_ref, wq_ref, keys_ref,
                    x1_ref, h2_ref, st_ref):
    d = x_ref.shape[1]
    mix = (jnp.dot(oa_ref[...], woa_ref[...], preferred_element_type=F32)
           + jnp.dot(ob_ref[...], wob_ref[...], preferred_element_type=F32)
           + jnp.dot(oc_ref[...], woc_ref[...], preferred_element_type=F32))
    g1 = mod_ref[0, :, 2 * d:3 * d]
    sh2 = mod_ref[0, :, 3 * d:4 * d]
    sc2 = mod_ref[0, :, 4 * d:5 * d]
    x1 = x_ref[...] + g1 * mix
    x1_ref[...] = x1
    h2 = _rms_mod(x1, g_ref[...], sc2, sh2)
    h2_ref[...] = h2
    pq = jnp.dot(h2.astype(BF16), wq_ref[...], preferred_element_type=F32).astype(BF16)
    for hp in range(2 * PEER_HEADS):
        st_ref[hp] = lax.dot_general(keys_ref[hp], pq[:, hp * LANES:(hp + 1) * LANES],
                                     (((1,), (1,)), ((), ())), preferred_element_type=F32)


def _outproj(oa, ob, oc, woa, wob, woc, x2d, mod3, g, wq, keys, *, seq, tm=256):
    t, d = x2d.shape
    nt = t // tm
    nq = wq.shape[1]
    full = lambda a: pl.BlockSpec(a.shape, lambda i: (0,) * a.ndim)
    return pl.pallas_call(
        _outproj_kernel,
        grid=(nt,),
        in_specs=[
            pl.BlockSpec((tm, oa.shape[1]), lambda i: (i, 0)),
            pl.BlockSpec((tm, ob.shape[1]), lambda i: (i, 0)),
            pl.BlockSpec((tm, oc.shape[1]), lambda i: (i, 0)),
            full(woa), full(wob), full(woc),
            pl.BlockSpec((tm, d), lambda i: (i, 0)),
            pl.BlockSpec((1, 1, mod3.shape[2]), lambda i: ((i * tm) // seq, 0, 0)),
            pl.BlockSpec((1, d), lambda i: (0, 0)),
            full(wq), full(keys),
        ],
        out_specs=[pl.BlockSpec((tm, d), lambda i: (i, 0)),
                   pl.BlockSpec((tm, d), lambda i: (i, 0)),
                   pl.BlockSpec((2 * PEER_HEADS, PEER_NKEYS, tm), lambda i: (0, 0, i))],
        out_shape=[jax.ShapeDtypeStruct((t, d), F32), jax.ShapeDtypeStruct((t, d), F32),
                   jax.ShapeDtypeStruct((2 * PEER_HEADS, PEER_NKEYS, t), F32)],
        compiler_params=pltpu.CompilerParams(dimension_semantics=("arbitrary",),
                                             vmem_limit_bytes=VMEM_LIMIT),
        name="outproj_peerq",
    )(oa, ob, oc, woa, wob, woc, x2d, mod3, g, wq, keys)


_CAND_BLOCKS = ((0, 16),) + tuple((i, 8) for i in range(1, 8))
_CAND_ROWS = 16 + 7 * 8 + 8
_BIG_I = np.int32(2 ** 30)


def _topk_rows(cur, key, val_ref, key_ref):
    def body(r, cur):
        m = jnp.max(cur, axis=0, keepdims=True)
        kmin = jnp.min(jnp.where(cur == m, key, _BIG_I), axis=0, keepdims=True)
        val_ref[pl.ds(r, 1), :] = m
        key_ref[pl.ds(r, 1), :] = kmin
        return jnp.where(key == kmin, -jnp.inf, cur)
    lax.fori_loop(0, PEER_TOPK, body, cur)


def _peer_topk_kernel(st_ref, eidx_ref, g_ref, v1_ref, k1_ref, v2_ref, k2_ref, vt_ref, kt_ref, ei_ref, gg_ref):
    tl = st_ref.shape[2]
    row = lax.broadcasted_iota(I32, (PEER_NKEYS, tl), 0)
    for h in range(PEER_HEADS):
        _topk_rows(st_ref[2 * h], row, v1_ref, k1_ref)
        _topk_rows(st_ref[2 * h + 1], row, v2_ref, k2_ref)
        sv1, si1, sv2, si2 = v1_ref[...], k1_ref[...], v2_ref[...], k2_ref[...]
        jrow8 = lax.broadcasted_iota(I32, (8, tl), 0)
        jrow16 = lax.broadcasted_iota(I32, (16, tl), 0)
        cands, keys = [], []
        for i, nj in _CAND_BLOCKS:
            jrow = jrow16 if nj == 16 else jrow8
            cands.append(sv1[i:i + 1] + sv2[0:nj])
            keys.append((i * PEER_TOPK + jrow) * (PEER_NKEYS * PEER_NKEYS) + si1[i:i + 1] * PEER_NKEYS + si2[0:nj])
        cands.append(sv1[8:16] + sv2[0:1])
        keys.append((jrow8 + 8) * (PEER_TOPK * PEER_NKEYS * PEER_NKEYS) + si1[8:16] * PEER_NKEYS + si2[0:1])
        cand = jnp.concatenate(cands, axis=0)
        ckey = jnp.concatenate(keys, axis=0)
        _topk_rows(cand, ckey, vt_ref, kt_ref)
        top = vt_ref[...]
        e = jnp.exp(top - top[0:1])
        gg_ref[h * PEER_TOPK:(h + 1) * PEER_TOPK, :] = e / jnp.sum(e, axis=0, keepdims=True)
        ei_ref[h * PEER_TOPK:(h + 1) * PEER_TOPK, :] = kt_ref[...] & (PEER_NKEYS * PEER_NKEYS - 1)
    eidx_ref[...] = ei_ref[...].T
    g_ref[...] = gg_ref[...].T


def _peer_topk(st, *, tl=128):
    t = st.shape[2]
    hk = PEER_HEADS * PEER_TOPK
    return pl.pallas_call(
        _peer_topk_kernel,
        grid=(t // tl,),
        in_specs=[pl.BlockSpec((2 * PEER_HEADS, PEER_NKEYS, tl), lambda i: (0, 0, i))],
        out_specs=[pl.BlockSpec((tl, hk), lambda i: (i, 0)), pl.BlockSpec((tl, hk), lambda i: (i, 0))],
        out_shape=[jax.ShapeDtypeStruct((t, hk), I32), jax.ShapeDtypeStruct((t, hk), F32)],
        scratch_shapes=[pltpu.VMEM((PEER_TOPK, tl), F32), pltpu.VMEM((PEER_TOPK, tl), I32),
                        pltpu.VMEM((PEER_TOPK, tl), F32), pltpu.VMEM((PEER_TOPK, tl), I32),
                        pltpu.VMEM((PEER_TOPK, tl), F32), pltpu.VMEM((PEER_TOPK, tl), I32),
                        pltpu.VMEM((hk, tl), I32), pltpu.VMEM((hk, tl), F32)],
        compiler_params=pltpu.CompilerParams(dimension_semantics=("arbitrary",),
                                             vmem_limit_bytes=VMEM_LIMIT),
        name="peer_topk",
    )(st)


def _erf(x):
    return lax.erf(x)


def _peer_gather_kernel(idx_ref, idxn_ref, h_ref, g_ref, x1_ref, mod_ref, fg_ref, uv_hbm,
                        o_ref, buf, sem, *, tq, final):
    i = pl.program_id(0)
    n = pl.num_programs(0)
    rows = tq * PEER_HEADS * PEER_TOPK
    d = h_ref.shape[1]

    def issue(iref, slot):
        def one(r, carry):
            pltpu.make_async_copy(uv_hbm.at[iref[r]], buf.at[slot, r], sem.at[slot]).start()
            return carry
        lax.fori_loop(0, rows, one, 0, unroll=8)

    @pl.when(i == 0)
    def _():
        issue(idx_ref, 0)

    slot = i % 2

    @pl.when(i + 1 < n)
    def _():
        issue(idxn_ref, 1 - slot)

    pltpu.make_async_copy(uv_hbm.at[pl.ds(0, rows)], buf.at[slot], sem.at[slot]).wait()

    hk = PEER_HEADS * PEER_TOPK
    gt = g_ref[...].T
    outs = []
    for tt in range(tq):
        ue = buf[slot, tt * hk:(tt + 1) * hk, 0:d]
        ve = buf[slot, tt * hk:(tt + 1) * hk, d:2 * d]
        a = jnp.sum(ue * h_ref[tt:tt + 1, :], axis=-1, keepdims=True)
        act = 0.5 * a * (1.0 + _erf(a * (2.0 ** -0.5)))
        w = gt[:, tt:tt + 1] * act
        outs.append(jnp.sum(w * ve, axis=0, keepdims=True))
    peer = jnp.concatenate(outs, axis=0)
    g2 = mod_ref[0, :, 5 * d:6 * d]
    y = x1_ref[...] + g2 * peer
    if final:
        ms = jnp.mean(y * y, axis=-1, keepdims=True)
        y = y * lax.rsqrt(ms + EPS) * fg_ref[...]
    o_ref[...] = y


def _peer_gather(eidx_flat, h2, g, x1, mod3, final_g, uv, *, seq, final, tq=8):
    t, d = h2.shape
    hk = PEER_HEADS * PEER_TOPK
    rows = tq * hk
    n = t // tq
    return pl.pallas_call(
        functools.partial(_peer_gather_kernel, tq=tq, final=final),
        grid=(n,),
        in_specs=[
            pl.BlockSpec((rows,), lambda i: (i,), memory_space=pltpu.SMEM),
            pl.BlockSpec((rows,), lambda i: (jnp.minimum(i + 1, n - 1),), memory_space=pltpu.SMEM),
            pl.BlockSpec((tq, d), lambda i: (i, 0)),
            pl.BlockSpec((tq, hk), lambda i: (i, 0)),
            pl.BlockSpec((tq, d), lambda i: (i, 0)),
            pl.BlockSpec((1, 1, mod3.shape[2]), lambda i: ((i * tq) // seq, 0, 0)),
            pl.BlockSpec((1, d), lambda i: (0, 0)),
            pl.BlockSpec(memory_space=pl.ANY),
        ],
        out_specs=pl.BlockSpec((tq, d), lambda i: (i, 0)),
        out_shape=jax.ShapeDtypeStruct((t, d), F32),
        scratch_shapes=[pltpu.VMEM((2, rows, 2 * d), F32), pltpu.SemaphoreType.DMA((2,))],
        compiler_params=pltpu.CompilerParams(dimension_semantics=("arbitrary",),
                                             vmem_limit_bytes=VMEM_LIMIT),
        name="peer_gather",
    )(eidx_flat, eidx_flat, h2, g, x1, mod3, final_g, uv)


def kernel(x, c, norm1_g, norm2_g, w_ada, b_ada, w_in, w_out, lam_q1, lam_k1, lam_q2, lam_k2, subln_g, sinks,
           peer_wq, peer_keys, peer_u, peer_v, final_g):
    bsz, seq, d = x.shape
    depth = w_in.shape[0]
    t = bsz * seq
    slopes = _alibi_slopes()
    sl_c = [float(s) for s in slopes[:C_HEADS]]
    qa_bias = _slope_bias_col(slopes[C_HEADS:C_HEADS + A_HEADS])
    qb_bias = _slope_bias_col(slopes[C_HEADS + A_HEADS:])
    mods = _adaln_mods(c, w_ada, b_ada)
    x2d = x.reshape(t, d)
    fg = final_g.reshape(1, d)
    av, bw = A_HEADS * HEAD_DIM, B_HEADS * HEAD_DIM
    for l in range(depth):
        lam_init = 0.8 - 0.6 * math.exp(-0.3 * l)
        mod3 = mods[l].reshape(bsz, 1, N_MOD * d)
        wn, wt = _prep_in_weights(w_in[l])
        ka, kb, qc, vc2, qta, vta, qtb, vtb, ktc = _inproj(
            x2d, mod3, norm1_g[l].reshape(1, d), wn, wt, qa_bias, qb_bias, seq=seq)
        lamv = jnp.stack([lam_q1[l], lam_k1[l], lam_q2[l], lam_k2[l]]).astype(F32)
        sg = subln_g[l].reshape(HEAD_DIM, 1).astype(F32)
        oa = _diff_attention(qta, ka, vta, lamv, sg, bsz=bsz, seq=seq, lam_init=lam_init)
        ob = _moba_attention(qtb, kb, vtb, bsz=bsz, seq=seq)
        oc = _swa_attention(sinks[l].astype(F32), qc, ktc, vc2, bsz=bsz, seq=seq, slopes=sl_c)
        wo = w_out[l].astype(BF16)
        keys = peer_keys[l].reshape(2 * PEER_HEADS, PEER_NKEYS, -1).astype(BF16)
        x1, h2, st = _outproj(oa, ob, oc, wo[:av], wo[av:av + bw], wo[av + bw:], x2d, mod3,
                              norm2_g[l].reshape(1, d), peer_wq[l].astype(BF16), keys, seq=seq)
        eidx, g = _peer_topk(st)
        uv = jnp.concatenate([peer_u[l], peer_v[l]], axis=1)
        x2d = _peer_gather(eidx.reshape(-1), h2, g, x1, mod3, fg, uv, seq=seq, final=(l == depth - 1))
    return x2d.reshape(bsz, seq, d)
```

```python
import functools
import math

import numpy as np
import jax
import jax.numpy as jnp
from jax import lax
from jax.experimental import pallas as pl
from jax.experimental.pallas import tpu as pltpu

F32 = jnp.float32
BF16 = jnp.bfloat16
I32 = jnp.int32

D_MODEL = 1024
HEAD_DIM = 64
N_HEADS_TOTAL = 16
A_HEADS = 4
B_HEADS = 6
C_HEADS = 6
C_KV_HEADS = 2
C_GROUP = 3
A_QK_DIM = 32
MOBA_BLOCK = 256
MOBA_TOPK = 3
WINDOW = 128
ALIBI_MAX = 8.0
PEER_HEADS = 8
PEER_NKEYS = 128
PEER_TOPK = 16
N_MOD = 6
EPS = 1e-6

LANES = 128
KT = 256
AUG0 = HEAD_DIM
SEL0 = AUG0 + 6
MAX_BLOCKS = LANES - SEL0
NEG_BIG = -1e30
VMEM_LIMIT = 56 * 1024 * 1024


def _alibi_slopes():
    n = N_HEADS_TOTAL
    return (2.0 ** (-ALIBI_MAX * np.arange(1, n + 1, dtype=np.float32) / n)).astype(np.float32)


def _split3(v):
    v = np.float32(v)
    hi = np.float32(np.asarray(v).astype(jnp.bfloat16).astype(np.float32))
    r = np.float32(v - hi)
    mid = np.float32(np.asarray(r).astype(jnp.bfloat16).astype(np.float32))
    lo = np.float32(np.float32(r - mid))
    lo = np.float32(np.asarray(lo).astype(jnp.bfloat16).astype(np.float32))
    return hi, mid, lo


def _slope_bias_col(slopes):
    col = np.zeros((LANES * len(slopes), 1), np.float32)
    for h, s in enumerate(slopes):
        hi, mid, lo = _split3(s)
        col[h * LANES + AUG0:h * LANES + AUG0 + 6, 0] = [hi, mid, lo, hi, mid, lo]
    return jnp.asarray(col)


def _mod_kernel(c_ref, w_ref, b_ref, o_ref):
    c = c_ref[...]
    cs = c * (1.0 / (1.0 + jnp.exp(-c)))
    o_ref[0] = jnp.dot(cs, w_ref[0], preferred_element_type=F32) + b_ref[0]


def _adaln_mods(c, w_ada, b_ada):
    depth, d, n = w_ada.shape
    bsz = c.shape[0]
    rows = -(-bsz // 8) * 8
    cp = jnp.pad(c, ((0, rows - bsz), (0, 0)))
    tn = 1536
    out = pl.pallas_call(
        _mod_kernel,
        grid=(depth, n // tn),
        in_specs=[
            pl.BlockSpec((rows, d), lambda l, j: (0, 0)),
            pl.BlockSpec((1, d, tn), lambda l, j: (l, 0, j)),
            pl.BlockSpec((1, 1, tn), lambda l, j: (l, 0, j)),
        ],
        out_specs=pl.BlockSpec((1, rows, tn), lambda l, j: (l, 0, j)),
        out_shape=jax.ShapeDtypeStruct((depth, rows, n), F32),
        compiler_params=pltpu.CompilerParams(dimension_semantics=("arbitrary", "arbitrary"),
                                             vmem_limit_bytes=VMEM_LIMIT),
        name="adaln_mods",
    )(cp, w_ada, b_ada.reshape(depth, 1, n))
    return out[:, :bsz]


NN_WIDTHS = (A_HEADS * LANES, B_HEADS * LANES, C_HEADS * HEAD_DIM, 2 * C_KV_HEADS * HEAD_DIM)
NT_ROWS = (A_HEADS * LANES, A_HEADS * HEAD_DIM, B_HEADS * LANES, B_HEADS * HEAD_DIM,
           C_KV_HEADS * LANES)


def _prep_in_weights(w):
    d = w.shape[0]
    aq, ak, av = A_HEADS * 2 * A_QK_DIM, A_HEADS * 2 * A_QK_DIM, A_HEADS * HEAD_DIM
    bw = B_HEADS * HEAD_DIM
    cq, ckv = C_HEADS * HEAD_DIM, C_KV_HEADS * HEAD_DIM
    cuts = np.cumsum([aq, ak, av, bw, bw, bw, cq, ckv]).tolist()
    qa, ka, va, qb, kb, vb, qc, kc, vc = jnp.split(w, cuts, axis=-1)

    def pad_heads(m, nh, scale):
        m = (m * scale).reshape(d, nh, HEAD_DIM)
        return jnp.pad(m, ((0, 0), (0, 0), (0, LANES - HEAD_DIM))).reshape(d, nh * LANES)

    vc2 = vc.reshape(d, C_KV_HEADS, 1, HEAD_DIM)
    vc2 = jnp.broadcast_to(vc2, (d, C_KV_HEADS, 2, HEAD_DIM)).reshape(d, 2 * ckv)
    kc2 = jnp.broadcast_to(kc.reshape(d, C_KV_HEADS, 1, HEAD_DIM), (d, C_KV_HEADS, 2, HEAD_DIM)).reshape(d, 2 * ckv)
    wn = jnp.concatenate([pad_heads(ka, A_HEADS, 1.0), pad_heads(kb, B_HEADS, 1.0),
                          qc * (HEAD_DIM ** -0.5), vc2], axis=1)
    wt = jnp.concatenate([pad_heads(qa, A_HEADS, A_QK_DIM ** -0.5), va,
                          pad_heads(qb, B_HEADS, HEAD_DIM ** -0.5), vb, kc2], axis=1).T
    return wn.astype(BF16), wt.astype(BF16)


def _rms_mod(x, g, sc, sh):
    ms = jnp.mean(x * x, axis=-1, keepdims=True)
    return (x * lax.rsqrt(ms + EPS) * g) * (1.0 + sc) + sh


def _inproj_kernel(x_ref, mod_ref, g_ref, wn_ref, wt_ref, qab_ref, qbb_ref,
                   ka_ref, kb_ref, qc_ref, vc_ref, qta_ref, vta_ref, qtb_ref, vtb_ref, ktc_ref,
                   *, tm, seq):
    d = x_ref.shape[1]
    x = x_ref[...]
    sh = mod_ref[0, :, 0:d]
    sc = mod_ref[0, :, d:2 * d]
    h = _rms_mod(x, g_ref[...], sc, sh).astype(BF16)
    pn = jnp.dot(h, wn_ref[...], preferred_element_type=F32)
    pt = lax.dot_general(wt_ref[...], h, (((1,), (1,)), ((), ())),
                         preferred_element_type=F32)

    pos = (pl.program_id(0) * tm) % seq + lax.broadcasted_iota(I32, (tm, LANES), 0)
    col = lax.broadcasted_iota(I32, (tm, LANES), 1)
    blk_id = pos // KT
    p_hi = (blk_id * KT).astype(F32)
    p_lo = (pos - blk_id * KT).astype(F32)
    aug_a = jnp.where((col >= AUG0) & (col < AUG0 + 3), p_hi,
                      jnp.where((col >= AUG0 + 3) & (col < AUG0 + 6), p_lo, 0.0))
    aug_b = jnp.where((col >= SEL0) & (col - SEL0 == blk_id), 1.0, aug_a)
    for hh in range(A_HEADS):
        ka_ref[:, hh * LANES:(hh + 1) * LANES] = (pn[:, hh * LANES:(hh + 1) * LANES] + aug_a).astype(BF16)
    o = NN_WIDTHS[0]
    for hh in range(B_HEADS):
        kb_ref[:, hh * LANES:(hh + 1) * LANES] = (pn[:, o + hh * LANES:o + (hh + 1) * LANES] + aug_b).astype(BF16)
    o += NN_WIDTHS[1]
    qc_ref[...] = pn[:, o:o + NN_WIDTHS[2]].astype(BF16)
    o += NN_WIDTHS[2]
    vc_ref[...] = pn[:, o:o + NN_WIDTHS[3]].astype(BF16)

    r0 = 0
    for ref, b, nr in zip((qta_ref, vta_ref, qtb_ref, vtb_ref, ktc_ref),
                          (qab_ref, None, qbb_ref, None, None), NT_ROWS):
        blk = pt[r0:r0 + nr, :]
        if b is not None:
            blk = blk + b[...]
        blk = blk.astype(BF16)
        for cc in range(tm // KT):
            ref[cc] = blk[:, cc * KT:(cc + 1) * KT]
        r0 += nr


def _inproj(x2d, mod3, g, wn, wt, qa_bias, qb_bias, *, seq, tm=512):
    t, d = x2d.shape
    assert seq % tm == 0 and tm % KT == 0
    nt = t // tm
    nn_total = sum(NN_WIDTHS)
    row_specs = [pl.BlockSpec((tm, wd), lambda i: (i, 0)) for wd in NN_WIDTHS]
    kt_specs = [pl.BlockSpec((tm // KT, r, KT), lambda i: (i, 0, 0)) for r in NT_ROWS]
    out_shape = ([jax.ShapeDtypeStruct((t, wd), BF16) for wd in NN_WIDTHS]
                 + [jax.ShapeDtypeStruct((t // KT, r, KT), BF16) for r in NT_ROWS])
    return pl.pallas_call(
        functools.partial(_inproj_kernel, tm=tm, seq=seq),
        grid=(nt,),
        in_specs=[
            pl.BlockSpec((tm, d), lambda i: (i, 0)),
            pl.BlockSpec((1, 1, mod3.shape[2]), lambda i: ((i * tm) // seq, 0, 0)),
            pl.BlockSpec((1, d), lambda i: (0, 0)),
            pl.BlockSpec((d, nn_total), lambda i: (0, 0)),
            pl.BlockSpec((sum(NT_ROWS), d), lambda i: (0, 0)),
            pl.BlockSpec((NT_ROWS[0], 1), lambda i: (0, 0)),
            pl.BlockSpec((NT_ROWS[2], 1), lambda i: (0, 0)),
        ],
        out_specs=row_specs + kt_specs,
        out_shape=out_shape,
        compiler_params=pltpu.CompilerParams(dimension_semantics=("arbitrary",),
                                             vmem_limit_bytes=VMEM_LIMIT),
        name="inproj",
    )(x2d, mod3, g, wn, wt, qa_bias, qb_bias)


def _flash_step(k, qt, vt, m_ref, l_ref, acc_ref, mask=None):
    s = jnp.dot(k, qt, preferred_element_type=F32)
    if mask is not None:
        s = jnp.where(mask, s, -jnp.inf)
    m_prev = m_ref[...]
    m_new = jnp.maximum(m_prev, jnp.max(s, axis=0, keepdims=True))
    alpha = jnp.exp(m_prev - m_new)
    p = jnp.exp(s - m_new)
    l_ref[...] = alpha * l_ref[...] + jnp.sum(p, axis=0, keepdims=True)
    acc_ref[...] = alpha * acc_ref[...] + jnp.dot(vt, p.astype(BF16), preferred_element_type=F32)
    m_ref[...] = m_new


def _flash_init(m_ref, l_ref, acc_ref):
    m_ref[...] = jnp.full(m_ref.shape, -jnp.inf, F32)
    l_ref[...] = jnp.zeros(l_ref.shape, F32)
    acc_ref[...] = jnp.zeros(acc_ref.shape, F32)


PAST_TILES = 2


def _flash_causal_pair(k_ref, vt_ref, q_diag, q_past, qi, causal, m_ref, l_ref, acc_ref):
    _flash_init(m_ref, l_ref, acc_ref)

    def step(kj, n, qs, mask):
        kk = k_ref[pl.ds(pl.multiple_of(kj * KT, KT), n * KT), :]
        for hh in range(2):
            rows = slice(hh * HEAD_DIM, (hh + 1) * HEAD_DIM)
            vt = vt_ref[kj, rows, :] if n == 1 else jnp.concatenate(
                [vt_ref[kj + c, rows, :] for c in range(n)], axis=1)
            _flash_step(kk[:, hh * LANES:(hh + 1) * LANES], qs[hh], vt,
                        m_ref.at[hh], l_ref.at[hh], acc_ref.at[hh], mask)

    step(qi, 1, q_diag, causal)

    def body(j, carry):
        step(j * PAST_TILES, PAST_TILES, q_past, None)
        return carry

    lax.fori_loop(0, qi // PAST_TILES, body, 0)
    for r in range(PAST_TILES - 1):
        @pl.when(qi % PAST_TILES > r)
        def _():
            step(qi - 1 - r, 1, q_past, None)


def _diff_kernel(qt_ref, k_ref, vt_ref, lamv_ref, sg_ref, o_ref, m_ref, l_ref, acc_ref, *, lam_init):
    tq = qt_ref.shape[2]
    qi = pl.program_id(2)
    lv = lamv_ref[...]
    lam = (jnp.exp(jnp.sum(lv[0:1] * lv[1:2], axis=-1, keepdims=True))
           - jnp.exp(jnp.sum(lv[2:3] * lv[3:4], axis=-1, keepdims=True)) + lam_init)
    row = lax.broadcasted_iota(I32, (LANES, tq), 0)
    kr = lax.broadcasted_iota(I32, (KT, 2 * tq), 0)
    qc = lax.broadcasted_iota(I32, (KT, 2 * tq), 1)
    causal = kr <= jnp.where(qc >= tq, qc - tq, qc)
    qs = []
    for hh in range(2):
        qt = qt_ref[0, hh * LANES:(hh + 1) * LANES, :]
        zero = jnp.zeros_like(qt)
        q1 = jnp.where((row < A_QK_DIM) | (row >= AUG0), qt, zero)
        q2 = jnp.where(row >= A_QK_DIM, qt, zero)
        qs.append(jnp.concatenate([q1, q2], axis=1))
    _flash_causal_pair(k_ref, vt_ref, qs, qs, qi, causal, m_ref, l_ref, acc_ref)
    res = []
    for hh in range(2):
        o = acc_ref[hh] / l_ref[hh]
        od = o[:, :tq] - lam * o[:, tq:]
        ms = jnp.mean(od * od, axis=0, keepdims=True)
        res.append(od * lax.rsqrt(ms + EPS) * sg_ref[...] * (1.0 - lam_init))
    o_ref[...] = jnp.concatenate(res, axis=0).T.astype(BF16)


def _diff_attention(qta, ka, vta, lamv, sg, *, bsz, seq, lam_init):
    tq = KT
    nq = seq // tq
    t = bsz * seq
    return pl.pallas_call(
        functools.partial(_diff_kernel, lam_init=lam_init),
        grid=(bsz, A_HEADS // 2, nq),
        in_specs=[
            pl.BlockSpec((1, 2 * LANES, tq), lambda b, hp, qi: (b * nq + qi, hp, 0)),
            pl.BlockSpec((seq, 2 * LANES), lambda b, hp, qi: (b, hp)),
            pl.BlockSpec((nq, 2 * HEAD_DIM, KT), lambda b, hp, qi: (b, hp, 0)),
            pl.BlockSpec((4, A_QK_DIM), lambda b, hp, qi: (0, 0)),
            pl.BlockSpec((HEAD_DIM, 1), lambda b, hp, qi: (0, 0)),
        ],
        out_specs=pl.BlockSpec((tq, LANES), lambda b, hp, qi: (b * nq + qi, hp)),
        out_shape=jax.ShapeDtypeStruct((t, A_HEADS * HEAD_DIM), BF16),
        scratch_shapes=[pltpu.VMEM((2, 1, 2 * tq), F32), pltpu.VMEM((2, 1, 2 * tq), F32),
                        pltpu.VMEM((2, HEAD_DIM, 2 * tq), F32)],
        compiler_params=pltpu.CompilerParams(dimension_semantics=("arbitrary",) * 3,
                                             vmem_limit_bytes=VMEM_LIMIT),
        name="diff_attn",
    )(qta, ka, vta, lamv, sg)


def _moba_kernel(qt_ref, k_ref, vt_ref, o_ref, m_ref, l_ref, acc_ref, km_ref):
    tq = qt_ref.shape[2]
    nb = vt_ref.shape[0]
    qi = pl.program_id(2)

    @pl.when(qi == 0)
    def _():
        lane1 = lax.broadcasted_iota(I32, (1, LANES), 1)
        for hh in range(2):
            km_ref[hh] = jnp.zeros((LANES, LANES), F32)

            def put_block(j, carry):
                blk = k_ref[pl.ds(pl.multiple_of(j * KT, KT), KT), hh * LANES:(hh + 1) * LANES].astype(F32)
                mean = jnp.sum(blk, axis=0, keepdims=True) * (1.0 / KT)
                km_ref[hh, pl.ds(SEL0 + j, 1), :] = jnp.where(lane1 < HEAD_DIM, mean, 0.0)
                return carry
            lax.fori_loop(0, nb, put_block, 0)

    row = lax.broadcasted_iota(I32, (LANES, tq), 0)
    kr = lax.broadcasted_iota(I32, (KT, tq), 0)
    qc = lax.broadcasted_iota(I32, (KT, tq), 1)
    causal = kr <= qc
    in_sel = (row >= SEL0) & (row < SEL0 + MAX_BLOCKS)
    q_diag, q_past = [], []
    for hh in range(2):
        qt = qt_ref[0, hh * LANES:(hh + 1) * LANES, :]
        km = km_ref[hh]
        km_hi = km.astype(BF16)
        km_lo = (km - km_hi.astype(F32)).astype(BF16)
        gate = (jnp.dot(km_hi, qt, preferred_element_type=F32)
                + jnp.dot(km_lo, qt, preferred_element_type=F32))
        cur = jnp.where((row >= SEL0) & (row < SEL0 + qi), gate, -jnp.inf)
        sel = jnp.zeros((LANES, tq), jnp.bool_)
        for _ in range(MOBA_TOPK):
            mx = jnp.max(cur, axis=0, keepdims=True)
            first = jnp.min(jnp.where(cur == mx, row, 4 * LANES), axis=0, keepdims=True)
            pick = (row == first) & (mx > -jnp.inf)
            sel = sel | pick
            cur = jnp.where(pick, -jnp.inf, cur)
        q_diag.append(jnp.where(in_sel, jnp.zeros_like(qt), qt))
        q_past.append(jnp.where(in_sel, jnp.where(sel, 0.0, NEG_BIG).astype(BF16), qt))
    _flash_causal_pair(k_ref, vt_ref, q_diag, q_past, qi, causal, m_ref, l_ref, acc_ref)
    res = [acc_ref[hh] / l_ref[hh] for hh in range(2)]
    o_ref[...] = jnp.concatenate(res, axis=0).T.astype(BF16)


def _moba_attention(qtb, kb, vtb, *, bsz, seq):
    tq = KT
    nq = seq // tq
    assert nq <= MAX_BLOCKS
    t = bsz * seq
    return pl.pallas_call(
        _moba_kernel,
        grid=(bsz, B_HEADS // 2, nq),
        in_specs=[
            pl.BlockSpec((1, 2 * LANES, tq), lambda b, hp, qi: (b * nq + qi, hp, 0)),
            pl.BlockSpec((seq, 2 * LANES), lambda b, hp, qi: (b, hp)),
            pl.BlockSpec((nq, 2 * HEAD_DIM, KT), lambda b, hp, qi: (b, hp, 0)),
        ],
        out_specs=pl.BlockSpec((tq, LANES), lambda b, hp, qi: (b * nq + qi, hp)),
        out_shape=jax.ShapeDtypeStruct((t, B_HEADS * HEAD_DIM), BF16),
        scratch_shapes=[pltpu.VMEM((2, 1, tq), F32), pltpu.VMEM((2, 1, tq), F32),
                        pltpu.VMEM((2, HEAD_DIM, tq), F32), pltpu.VMEM((2, LANES, LANES), F32)],
        compiler_params=pltpu.CompilerParams(dimension_semantics=("arbitrary",) * 3,
                                             vmem_limit_bytes=VMEM_LIMIT),
        name="moba_attn",
    )(qtb, kb, vtb)


def _swa_kernel(sink_ref, q_ref, ktp_ref, ktc_ref, vp_ref, vc_ref, o_ref, *, slopes):
    tq = q_ref.shape[0]
    n = pl.program_id(1)
    lane = lax.broadcasted_iota(I32, (tq, LANES), 1)
    r2 = lax.broadcasted_iota(I32, (tq, 2 * KT), 0)
    c2 = lax.broadcasted_iota(I32, (tq, 2 * KT), 1)
    rel = r2 + KT - c2
    mask = (rel >= 0) & (rel < WINDOW) & ((c2 >= KT) | (n > 0))
    relf = rel.astype(F32)
    res = []
    for hq in range(C_HEADS):
        kv = hq // C_GROUP
        qp = q_ref[:, (hq // 2) * LANES:(hq // 2 + 1) * LANES]
        qm = jnp.where((lane < HEAD_DIM) == (hq % 2 == 0), qp, jnp.zeros_like(qp))
        kt = jnp.concatenate([ktp_ref[0, kv * LANES:(kv + 1) * LANES, :],
                              ktc_ref[0, kv * LANES:(kv + 1) * LANES, :]], axis=1)
        vv = jnp.concatenate([vp_ref[:, kv * LANES:(kv + 1) * LANES],
                              vc_ref[:, kv * LANES:(kv + 1) * LANES]], axis=0)
        s = jnp.dot(qm, kt, preferred_element_type=F32)
        s = jnp.where(mask, s - slopes[hq] * relf, -jnp.inf)
        sink = sink_ref[hq]
        m = jnp.maximum(jnp.max(s, axis=-1, keepdims=True), sink)
        e = jnp.exp(s - m)
        den = jnp.sum(e, axis=-1, keepdims=True) + jnp.exp(sink - m)
        p = (e / den).astype(BF16)
        res.append(jnp.dot(p, vv, preferred_element_type=F32))
    for pr in range(C_HEADS // 2):
        o_ref[:, pr * LANES:(pr + 1) * LANES] = jnp.where(lane < HEAD_DIM, res[2 * pr], res[2 * pr + 1]).astype(BF16)


def _swa_attention(sinks, qc, ktc, vc2, *, bsz, seq, slopes):
    tq = KT
    nq = seq // tq
    t = bsz * seq
    return pl.pallas_call(
        functools.partial(_swa_kernel, slopes=slopes),
        grid=(bsz, nq),
        in_specs=[
            pl.BlockSpec(memory_space=pltpu.SMEM),
            pl.BlockSpec((tq, C_HEADS * HEAD_DIM), lambda b, n: (b * nq + n, 0)),
            pl.BlockSpec((1, C_KV_HEADS * LANES, KT), lambda b, n: (b * nq + jnp.maximum(n - 1, 0), 0, 0)),
            pl.BlockSpec((1, C_KV_HEADS * LANES, KT), lambda b, n: (b * nq + n, 0, 0)),
            pl.BlockSpec((tq, C_KV_HEADS * LANES), lambda b, n: (b * nq + jnp.maximum(n - 1, 0), 0)),
            pl.BlockSpec((tq, C_KV_HEADS * LANES), lambda b, n: (b * nq + n, 0)),
        ],
        out_specs=pl.BlockSpec((tq, C_HEADS * HEAD_DIM), lambda b, n: (b * nq + n, 0)),
        out_shape=jax.ShapeDtypeStruct((t, C_HEADS * HEAD_DIM), BF16),
        compiler_params=pltpu.CompilerParams(dimension_semantics=("arbitrary",) * 2,
                                             vmem_limit_bytes=VMEM_LIMIT),
        name="swa_attn",
    )(sinks, qc, ktc, ktc, vc2, vc2)


def _outproj_kernel(oa_ref, ob_ref, oc_ref, woa_ref, wob_ref, woc_ref, x_ref, mod_ref, g_ref, wq_ref, keys_ref,
                    x1_ref, h2_ref, st_ref):
    d = x_ref.shape[1]
    mix = (jnp.dot(oa_ref[...], woa_ref[...], preferred_element_type=F32)
           + jnp.dot(ob_ref[...], wob_ref[...], preferred_element_type=F32)
           + jnp.dot(oc_ref[...], woc_ref[...], preferred_element_type=F32))
    g1 = mod_ref[0, :, 2 * d:3 * d]
    sh2 = mod_ref[0, :, 3 * d:4 * d]
    sc2 = mod_ref[0, :, 4 * d:5 * d]
    x1 = x_ref[...] + g1 * mix
    x1_ref[...] = x1
    h2 = _rms_mod(x1, g_ref[...], sc2, sh2)
    h2_ref[...] = h2
    pq = jnp.dot(h2.astype(BF16), wq_ref[...], preferred_element_type=F32).astype(BF16)
    for hp in range(2 * PEER_HEADS):
        st_ref[hp] = lax.dot_general(keys_ref[hp], pq[:, hp * LANES:(hp + 1) * LANES],
                                     (((1,), (1,)), ((), ())), preferred_element_type=F32)


def _outproj(oa, ob, oc, woa, wob, woc, x2d, mod3, g, wq, keys, *, seq, tm=256):
    t, d = x2d.shape
    nt = t // tm
    nq = wq.shape[1]
    full = lambda a: pl.BlockSpec(a.shape, lambda i: (0,) * a.ndim)
    return pl.pallas_call(
        _outproj_kernel,
        grid=(nt,),
        in_specs=[
            pl.BlockSpec((tm, oa.shape[1]), lambda i: (i, 0)),
            pl.BlockSpec((tm, ob.shape[1]), lambda i: (i, 0)),
            pl.BlockSpec((tm, oc.shape[1]), lambda i: (i, 0)),
            full(woa), full(wob), full(woc),
            pl.BlockSpec((tm, d), lambda i: (i, 0)),
            pl.BlockSpec((1, 1, mod3.shape[2]), lambda i: ((i * tm) // seq, 0, 0)),
            pl.BlockSpec((1, d), lambda i: (0, 0)),
            full(wq), full(keys),
        ],
        out_specs=[pl.BlockSpec((tm, d), lambda i: (i, 0)),
                   pl.BlockSpec((tm, d), lambda i: (i, 0)),
                   pl.BlockSpec((2 * PEER_HEADS, PEER_NKEYS, tm), lambda i: (0, 0, i))],
        out_shape=[jax.ShapeDtypeStruct((t, d), F32), jax.ShapeDtypeStruct((t, d), F32),
                   jax.ShapeDtypeStruct((2 * PEER_HEADS, PEER_NKEYS, t), F32)],
        compiler_params=pltpu.CompilerParams(dimension_semantics=("arbitrary",),
                                             vmem_limit_bytes=VMEM_LIMIT),
        name="outproj_peerq",
    )(oa, ob, oc, woa, wob, woc, x2d, mod3, g, wq, keys)


_CAND_BLOCKS = ((0, 16),) + tuple((i, 8) for i in range(1, 8))
_CAND_ROWS = 16 + 7 * 8 + 8
_BIG_I = np.int32(2 ** 30)


def _topk_rows(cur, key, val_ref, key_ref):
    def body(r, cur):
        m = jnp.max(cur, axis=0, keepdims=True)
        kmin = jnp.min(jnp.where(cur == m, key, _BIG_I), axis=0, keepdims=True)
        val_ref[pl.ds(r, 1), :] = m
        key_ref[pl.ds(r, 1), :] = kmin
        return jnp.where(key == kmin, -jnp.inf, cur)
    lax.fori_loop(0, PEER_TOPK, body, cur)


def _peer_topk_kernel(st_ref, eidx_ref, g_ref, v1_ref, k1_ref, v2_ref, k2_ref, vt_ref, kt_ref, ei_ref, gg_ref):
    tl = st_ref.shape[2]
    row = lax.broadcasted_iota(I32, (PEER_NKEYS, tl), 0)
    for h in range(PEER_HEADS):
        _topk_rows(st_ref[2 * h], row, v1_ref, k1_ref)
        _topk_rows(st_ref[2 * h + 1], row, v2_ref, k2_ref)
        sv1, si1, sv2, si2 = v1_ref[...], k1_ref[...], v2_ref[...], k2_ref[...]
        jrow8 = lax.broadcasted_iota(I32, (8, tl), 0)
        jrow16 = lax.broadcasted_iota(I32, (16, tl), 0)
        cands, keys = [], []
        for i, nj in _CAND_BLOCKS:
            jrow = jrow16 if nj == 16 else jrow8
            cands.append(sv1[i:i + 1] + sv2[0:nj])
            keys.append((i * PEER_TOPK + jrow) * (PEER_NKEYS * PEER_NKEYS) + si1[i:i + 1] * PEER_NKEYS + si2[0:nj])
        cands.append(sv1[8:16] + sv2[0:1])
        keys.append((jrow8 + 8) * (PEER_TOPK * PEER_NKEYS * PEER_NKEYS) + si1[8:16] * PEER_NKEYS + si2[0:1])
        cand = jnp.concatenate(cands, axis=0)
        ckey = jnp.concatenate(keys, axis=0)
        _topk_rows(cand, ckey, vt_ref, kt_ref)
        top = vt_ref[...]
        e = jnp.exp(top - top[0:1])
        gg_ref[h * PEER_TOPK:(h + 1) * PEER_TOPK, :] = e / jnp.sum(e, axis=0, keepdims=True)
        ei_ref[h * PEER_TOPK:(h + 1) * PEER_TOPK, :] = kt_ref[...] & (PEER_NKEYS * PEER_NKEYS - 1)
    eidx_ref[...] = ei_ref[...].T
    g_ref[...] = gg_ref[...].T


def _peer_topk(st, *, tl=128):
    t = st.shape[2]
    hk = PEER_HEADS * PEER_TOPK
    return pl.pallas_call(
        _peer_topk_kernel,
        grid=(t // tl,),
        in_specs=[pl.BlockSpec((2 * PEER_HEADS, PEER_NKEYS, tl), lambda i: (0, 0, i))],
        out_specs=[pl.BlockSpec((tl, hk), lambda i: (i, 0)), pl.BlockSpec((tl, hk), lambda i: (i, 0))],
        out_shape=[jax.ShapeDtypeStruct((t, hk), I32), jax.ShapeDtypeStruct((t, hk), F32)],
        scratch_shapes=[pltpu.VMEM((PEER_TOPK, tl), F32), pltpu.VMEM((PEER_TOPK, tl), I32),
                        pltpu.VMEM((PEER_TOPK, tl), F32), pltpu.VMEM((PEER_TOPK, tl), I32),
                        pltpu.VMEM((PEER_TOPK, tl), F32), pltpu.VMEM((PEER_TOPK, tl), I32),
                        pltpu.VMEM((hk, tl), I32), pltpu.VMEM((hk, tl), F32)],
        compiler_params=pltpu.CompilerParams(dimension_semantics=("arbitrary",),
                                             vmem_limit_bytes=VMEM_LIMIT),
        name="peer_topk",
    )(st)


def _erf(x):
    return lax.erf(x)


SUB = 8


def _peer_gather_kernel(idx_ref, idxn_ref, h_ref, g_ref, x1_ref, g2_ref, fg_ref, uv_hbm,
                        o_ref, buf, sem, *, tq, final):
    i = pl.program_id(0)
    n = pl.num_programs(0)
    hk = PEER_HEADS * PEER_TOPK
    groups = tq * hk // SUB
    nj = h_ref.shape[1]

    def issue(iref, slot):
        def one(ro, carry):
            for ri in range(SUB):
                pltpu.make_async_copy(uv_hbm.at[iref[ro * SUB + ri]], buf.at[slot, ro, :, ri, :],
                                      sem.at[slot]).start()
            return carry
        lax.fori_loop(0, groups, one, 0)

    @pl.when(i == 0)
    def _():
        issue(idx_ref, 0)

    slot = i % 2

    @pl.when(i + 1 < n)
    def _():
        issue(idxn_ref, 1 - slot)

    pltpu.make_async_copy(buf.at[slot], buf.at[slot], sem.at[slot]).wait()

    gt = g_ref[...].T
    gpt = hk // SUB
    for tt in range(tq):
        ue = buf[slot, tt * gpt:(tt + 1) * gpt, 0:nj]
        ve = buf[slot, tt * gpt:(tt + 1) * gpt, nj:2 * nj]
        hb = jnp.broadcast_to(h_ref[tt][:, None, :], (nj, SUB, LANES))
        a = jnp.sum(jnp.sum(ue * hb[None], axis=1), axis=-1, keepdims=True)
        act = 0.5 * a * (1.0 + _erf(a * (2.0 ** -0.5)))
        w = gt[:, tt:tt + 1].reshape(gpt, SUB, 1) * act
        peer = jnp.sum(jnp.sum(w[:, None] * ve, axis=0), axis=1)
        y = x1_ref[tt] + g2_ref[0] * peer
        if final:
            ms = jnp.mean(jnp.mean(y * y, axis=-1, keepdims=True), axis=0, keepdims=True)
            y = y * lax.rsqrt(ms + EPS) * fg_ref[...]
        o_ref[tt] = y


def _peer_gather(eidx_flat, h2, g, x1, g2, final_g, uv, *, seq, final, tq=8):
    t, nj, _ = h2.shape
    hk = PEER_HEADS * PEER_TOPK
    rows = tq * hk
    n = t // tq
    return pl.pallas_call(
        functools.partial(_peer_gather_kernel, tq=tq, final=final),
        grid=(n,),
        in_specs=[
            pl.BlockSpec((rows,), lambda i: (i,), memory_space=pltpu.SMEM),
            pl.BlockSpec((rows,), lambda i: (jnp.minimum(i + 1, n - 1),), memory_space=pltpu.SMEM),
            pl.BlockSpec((tq, nj, LANES), lambda i: (i, 0, 0)),
            pl.BlockSpec((tq, hk), lambda i: (i, 0)),
            pl.BlockSpec((tq, nj, LANES), lambda i: (i, 0, 0)),
            pl.BlockSpec((1, nj, LANES), lambda i: ((i * tq) // seq, 0, 0)),
            pl.BlockSpec((nj, LANES), lambda i: (0, 0)),
            pl.BlockSpec(memory_space=pl.ANY),
        ],
        out_specs=pl.BlockSpec((tq, nj, LANES), lambda i: (i, 0, 0)),
        out_shape=jax.ShapeDtypeStruct((t, nj, LANES), F32),
        scratch_shapes=[pltpu.VMEM((2, rows // SUB, 2 * nj, SUB, LANES), F32), pltpu.SemaphoreType.DMA((2,))],
        compiler_params=pltpu.CompilerParams(dimension_semantics=("arbitrary",),
                                             vmem_limit_bytes=VMEM_LIMIT),
        name="peer_gather",
    )(eidx_flat, eidx_flat, h2, g, x1, g2, final_g, uv)


def kernel(x, c, norm1_g, norm2_g, w_ada, b_ada, w_in, w_out, lam_q1, lam_k1, lam_q2, lam_k2, subln_g, sinks,
           peer_wq, peer_keys, peer_u, peer_v, final_g):
    bsz, seq, d = x.shape
    depth = w_in.shape[0]
    t = bsz * seq
    slopes = _alibi_slopes()
    sl_c = [float(s) for s in slopes[:C_HEADS]]
    qa_bias = _slope_bias_col(slopes[C_HEADS:C_HEADS + A_HEADS])
    qb_bias = _slope_bias_col(slopes[C_HEADS + A_HEADS:])
    mods = _adaln_mods(c, w_ada, b_ada)
    x2d = x.reshape(t, d)
    fg = final_g.reshape(1, d)
    av, bw = A_HEADS * HEAD_DIM, B_HEADS * HEAD_DIM
    for l in range(depth):
        lam_init = 0.8 - 0.6 * math.exp(-0.3 * l)
        mod3 = mods[l].reshape(bsz, 1, N_MOD * d)
        wn, wt = _prep_in_weights(w_in[l])
        ka, kb, qc, vc2, qta, vta, qtb, vtb, ktc = _inproj(
            x2d, mod3, norm1_g[l].reshape(1, d), wn, wt, qa_bias, qb_bias, seq=seq)
        lamv = jnp.stack([lam_q1[l], lam_k1[l], lam_q2[l], lam_k2[l]]).astype(F32)
        sg = subln_g[l].reshape(HEAD_DIM, 1).astype(F32)
        oa = _diff_attention(qta, ka, vta, lamv, sg, bsz=bsz, seq=seq, lam_init=lam_init)
        ob = _moba_attention(qtb, kb, vtb, bsz=bsz, seq=seq)
        oc = _swa_attention(sinks[l].astype(F32), qc, ktc, vc2, bsz=bsz, seq=seq, slopes=sl_c)
        wo = w_out[l].astype(BF16)
        keys = peer_keys[l].reshape(2 * PEER_HEADS, PEER_NKEYS, -1).astype(BF16)
        x1, h2, st = _outproj(oa, ob, oc, wo[:av], wo[av:av + bw], wo[av + bw:], x2d, mod3,
                              norm2_g[l].reshape(1, d), peer_wq[l].astype(BF16), keys, seq=seq)
        eidx, g = _peer_topk(st)
        nj = d // LANES
        uv = jnp.concatenate([peer_u[l].reshape(-1, nj, LANES), peer_v[l].reshape(-1, nj, LANES)], axis=1)
        g2 = mods[l][:, 5 * d:6 * d].reshape(bsz, nj, LANES)
        x2d = _peer_gather(eidx.reshape(-1), h2.reshape(t, nj, LANES), g, x1.reshape(t, nj, LANES), g2,
                           final_g.reshape(nj, LANES), uv, seq=seq, final=(l == depth - 1)).reshape(t, d)
    return x2d.reshape(bsz, seq, d)
```

```python
import functools
import math

import numpy as np
import jax
import jax.numpy as jnp
from jax import lax
from jax.experimental import pallas as pl
from jax.experimental.pallas import tpu as pltpu

F32 = jnp.float32
BF16 = jnp.bfloat16
I32 = jnp.int32

D_MODEL = 1024
HEAD_DIM = 64
N_HEADS_TOTAL = 16
A_HEADS = 4
B_HEADS = 6
C_HEADS = 6
C_KV_HEADS = 2
C_GROUP = 3
A_QK_DIM = 32
MOBA_BLOCK = 256
MOBA_TOPK = 3
WINDOW = 128
ALIBI_MAX = 8.0
PEER_HEADS = 8
PEER_NKEYS = 128
PEER_TOPK = 16
N_MOD = 6
EPS = 1e-6

LANES = 128
KT = 256
AUG0 = HEAD_DIM
SEL0 = AUG0 + 6
MAX_BLOCKS = LANES - SEL0
NEG_BIG = -1e30
LOG2E = math.log2(math.e)
VMEM_LIMIT = 56 * 1024 * 1024


def _alibi_slopes():
    n = N_HEADS_TOTAL
    return (2.0 ** (-ALIBI_MAX * np.arange(1, n + 1, dtype=np.float32) / n)).astype(np.float32)


def _split3(v):
    v = np.float32(v)
    hi = np.float32(np.asarray(v).astype(jnp.bfloat16).astype(np.float32))
    r = np.float32(v - hi)
    mid = np.float32(np.asarray(r).astype(jnp.bfloat16).astype(np.float32))
    lo = np.float32(np.float32(r - mid))
    lo = np.float32(np.asarray(lo).astype(jnp.bfloat16).astype(np.float32))
    return hi, mid, lo


def _slope_bias_col(slopes):
    col = np.zeros((LANES * len(slopes), 1), np.float32)
    for h, s in enumerate(slopes):
        hi, mid, lo = _split3(s)
        col[h * LANES + AUG0:h * LANES + AUG0 + 6, 0] = [hi, mid, lo, hi, mid, lo]
    return jnp.asarray(col)


def _mod_kernel(c_ref, w_ref, b_ref, o_ref):
    c = c_ref[...]
    cs = c * (1.0 / (1.0 + jnp.exp(-c)))
    o_ref[0] = jnp.dot(cs, w_ref[0], preferred_element_type=F32) + b_ref[0]


def _adaln_mods(c, w_ada, b_ada):
    depth, d, n = w_ada.shape
    bsz = c.shape[0]
    rows = -(-bsz // 8) * 8
    cp = jnp.pad(c, ((0, rows - bsz), (0, 0)))
    tn = 1536
    out = pl.pallas_call(
        _mod_kernel,
        grid=(depth, n // tn),
        in_specs=[
            pl.BlockSpec((rows, d), lambda l, j: (0, 0)),
            pl.BlockSpec((1, d, tn), lambda l, j: (l, 0, j)),
            pl.BlockSpec((1, 1, tn), lambda l, j: (l, 0, j)),
        ],
        out_specs=pl.BlockSpec((1, rows, tn), lambda l, j: (l, 0, j)),
        out_shape=jax.ShapeDtypeStruct((depth, rows, n), F32),
        compiler_params=pltpu.CompilerParams(dimension_semantics=("arbitrary", "arbitrary"),
                                             vmem_limit_bytes=VMEM_LIMIT),
        name="adaln_mods",
    )(cp, w_ada, b_ada.reshape(depth, 1, n))
    return out[:, :bsz]


NN_WIDTHS = (A_HEADS * LANES, B_HEADS * LANES, C_HEADS * HEAD_DIM, 2 * C_KV_HEADS * HEAD_DIM)
NT_ROWS = (A_HEADS * LANES, A_HEADS * HEAD_DIM, B_HEADS * LANES, B_HEADS * HEAD_DIM,
           C_KV_HEADS * LANES)


def _prep_in_weights(w):
    d = w.shape[0]
    aq, ak, av = A_HEADS * 2 * A_QK_DIM, A_HEADS * 2 * A_QK_DIM, A_HEADS * HEAD_DIM
    bw = B_HEADS * HEAD_DIM
    cq, ckv = C_HEADS * HEAD_DIM, C_KV_HEADS * HEAD_DIM
    cuts = np.cumsum([aq, ak, av, bw, bw, bw, cq, ckv]).tolist()
    qa, ka, va, qb, kb, vb, qc, kc, vc = jnp.split(w, cuts, axis=-1)

    def pad_heads(m, nh, scale):
        m = (m * scale).reshape(d, nh, HEAD_DIM)
        return jnp.pad(m, ((0, 0), (0, 0), (0, LANES - HEAD_DIM))).reshape(d, nh * LANES)

    vc2 = vc.reshape(d, C_KV_HEADS, 1, HEAD_DIM)
    vc2 = jnp.broadcast_to(vc2, (d, C_KV_HEADS, 2, HEAD_DIM)).reshape(d, 2 * ckv)
    kc2 = jnp.broadcast_to(kc.reshape(d, C_KV_HEADS, 1, HEAD_DIM), (d, C_KV_HEADS, 2, HEAD_DIM)).reshape(d, 2 * ckv)
    wn = jnp.concatenate([pad_heads(ka, A_HEADS, 1.0), pad_heads(kb, B_HEADS, 1.0),
                          qc * (HEAD_DIM ** -0.5), vc2], axis=1)
    wt = jnp.concatenate([pad_heads(qa, A_HEADS, A_QK_DIM ** -0.5 * LOG2E), va,
                          pad_heads(qb, B_HEADS, HEAD_DIM ** -0.5 * LOG2E), vb, kc2], axis=1).T
    return wn.astype(BF16), wt.astype(BF16)


def _rms_mod(x, g, sc, sh):
    ms = jnp.mean(x * x, axis=-1, keepdims=True)
    return (x * lax.rsqrt(ms + EPS) * g) * (1.0 + sc) + sh


def _inproj_kernel(x_ref, mod_ref, g_ref, wn_ref, wt_ref, qab_ref, qbb_ref,
                   ka_ref, kb_ref, qc_ref, vc_ref, qta_ref, vta_ref, qtb_ref, vtb_ref, ktc_ref,
                   *, tm, seq):
    d = x_ref.shape[1]
    x = x_ref[...]
    sh = mod_ref[0, :, 0:d]
    sc = mod_ref[0, :, d:2 * d]
    h = _rms_mod(x, g_ref[...], sc, sh).astype(BF16)
    pn = jnp.dot(h, wn_ref[...], preferred_element_type=F32)
    pt = lax.dot_general(wt_ref[...], h, (((1,), (1,)), ((), ())),
                         preferred_element_type=F32)

    pos = (pl.program_id(0) * tm) % seq + lax.broadcasted_iota(I32, (tm, LANES), 0)
    col = lax.broadcasted_iota(I32, (tm, LANES), 1)
    blk_id = pos // KT
    p_hi = (blk_id * KT).astype(F32)
    p_lo = (pos - blk_id * KT).astype(F32)
    aug_a = jnp.where((col >= AUG0) & (col < AUG0 + 3), p_hi,
                      jnp.where((col >= AUG0 + 3) & (col < AUG0 + 6), p_lo, 0.0))
    aug_b = jnp.where((col >= SEL0) & (col - SEL0 == blk_id), 1.0, aug_a)
    for hh in range(A_HEADS):
        ka_ref[:, hh * LANES:(hh + 1) * LANES] = (pn[:, hh * LANES:(hh + 1) * LANES] + aug_a).astype(BF16)
    o = NN_WIDTHS[0]
    for hh in range(B_HEADS):
        kb_ref[:, hh * LANES:(hh + 1) * LANES] = (pn[:, o + hh * LANES:o + (hh + 1) * LANES] + aug_b).astype(BF16)
    o += NN_WIDTHS[1]
    qc_ref[...] = pn[:, o:o + NN_WIDTHS[2]].astype(BF16)
    o += NN_WIDTHS[2]
    vc_ref[...] = pn[:, o:o + NN_WIDTHS[3]].astype(BF16)

    r0 = 0
    for ref, b, nr in zip((qta_ref, vta_ref, qtb_ref, vtb_ref, ktc_ref),
                          (qab_ref, None, qbb_ref, None, None), NT_ROWS):
        blk = pt[r0:r0 + nr, :]
        if b is not None:
            blk = blk + b[...]
        blk = blk.astype(BF16)
        for cc in range(tm // KT):
            ref[cc] = blk[:, cc * KT:(cc + 1) * KT]
        r0 += nr


def _inproj(x2d, mod3, g, wn, wt, qa_bias, qb_bias, *, seq, tm=512):
    t, d = x2d.shape
    assert seq % tm == 0 and tm % KT == 0
    nt = t // tm
    nn_total = sum(NN_WIDTHS)
    row_specs = [pl.BlockSpec((tm, wd), lambda i: (i, 0)) for wd in NN_WIDTHS]
    kt_specs = [pl.BlockSpec((tm // KT, r, KT), lambda i: (i, 0, 0)) for r in NT_ROWS]
    out_shape = ([jax.ShapeDtypeStruct((t, wd), BF16) for wd in NN_WIDTHS]
                 + [jax.ShapeDtypeStruct((t // KT, r, KT), BF16) for r in NT_ROWS])
    return pl.pallas_call(
        functools.partial(_inproj_kernel, tm=tm, seq=seq),
        grid=(nt,),
        in_specs=[
            pl.BlockSpec((tm, d), lambda i: (i, 0)),
            pl.BlockSpec((1, 1, mod3.shape[2]), lambda i: ((i * tm) // seq, 0, 0)),
            pl.BlockSpec((1, d), lambda i: (0, 0)),
            pl.BlockSpec((d, nn_total), lambda i: (0, 0)),
            pl.BlockSpec((sum(NT_ROWS), d), lambda i: (0, 0)),
            pl.BlockSpec((NT_ROWS[0], 1), lambda i: (0, 0)),
            pl.BlockSpec((NT_ROWS[2], 1), lambda i: (0, 0)),
        ],
        out_specs=row_specs + kt_specs,
        out_shape=out_shape,
        compiler_params=pltpu.CompilerParams(dimension_semantics=("arbitrary",),
                                             vmem_limit_bytes=VMEM_LIMIT),
        name="inproj",
    )(x2d, mod3, g, wn, wt, qa_bias, qb_bias)


def _flash_step(k, qt, vt, m_ref, l_ref, acc_ref, mask=None):
    s = jnp.dot(k, qt, preferred_element_type=F32)
    if mask is not None:
        s = jnp.where(mask, s, -jnp.inf)
    m_prev = m_ref[...]
    m_new = jnp.maximum(m_prev, jnp.max(s, axis=0, keepdims=True))
    alpha = jnp.exp2(m_prev - m_new)
    p = jnp.exp2(s - m_new)
    l_ref[...] = alpha * l_ref[...] + jnp.sum(p, axis=0, keepdims=True)
    acc_ref[...] = alpha * acc_ref[...] + jnp.dot(vt, p.astype(BF16), preferred_element_type=F32)
    m_ref[...] = m_new


def _flash_init(m_ref, l_ref, acc_ref):
    m_ref[...] = jnp.full(m_ref.shape, -jnp.inf, F32)
    l_ref[...] = jnp.zeros(l_ref.shape, F32)
    acc_ref[...] = jnp.zeros(acc_ref.shape, F32)


PAST_TILES = 4


def _flash_causal_pair(k_ref, vt_ref, q_diag, q_past, qi, causal, m_ref, l_ref, acc_ref):
    _flash_init(m_ref, l_ref, acc_ref)

    def step(kj, n, qs, mask):
        kk = k_ref[pl.ds(pl.multiple_of(kj * KT, KT), n * KT), :]
        for hh in range(2):
            rows = slice(hh * HEAD_DIM, (hh + 1) * HEAD_DIM)
            vt = vt_ref[kj, rows, :] if n == 1 else jnp.concatenate(
                [vt_ref[kj + c, rows, :] for c in range(n)], axis=1)
            _flash_step(kk[:, hh * LANES:(hh + 1) * LANES], qs[hh], vt,
                        m_ref.at[hh], l_ref.at[hh], acc_ref.at[hh], mask)

    step(qi, 1, q_diag, causal)

    def body(j, carry):
        step(j * PAST_TILES, PAST_TILES, q_past, None)
        return carry

    lax.fori_loop(0, qi // PAST_TILES, body, 0)
    for r in range(PAST_TILES - 1):
        @pl.when(qi % PAST_TILES > r)
        def _():
            step(qi - 1 - r, 1, q_past, None)


def _diff_kernel(qt_ref, k_ref, vt_ref, lamv_ref, sg_ref, o_ref, m_ref, l_ref, acc_ref, *, lam_init):
    tq = qt_ref.shape[2]
    qi = pl.program_id(2)
    lv = lamv_ref[...]
    lam = (jnp.exp(jnp.sum(lv[0:1] * lv[1:2], axis=-1, keepdims=True))
           - jnp.exp(jnp.sum(lv[2:3] * lv[3:4], axis=-1, keepdims=True)) + lam_init)
    row = lax.broadcasted_iota(I32, (LANES, tq), 0)
    kr = lax.broadcasted_iota(I32, (KT, 2 * tq), 0)
    qc = lax.broadcasted_iota(I32, (KT, 2 * tq), 1)
    causal = kr <= jnp.where(qc >= tq, qc - tq, qc)
    qs = []
    for hh in range(2):
        qt = qt_ref[0, hh * LANES:(hh + 1) * LANES, :]
        zero = jnp.zeros_like(qt)
        q1 = jnp.where((row < A_QK_DIM) | (row >= AUG0), qt, zero)
        q2 = jnp.where(row >= A_QK_DIM, qt, zero)
        qs.append(jnp.concatenate([q1, q2], axis=1))
    _flash_causal_pair(k_ref, vt_ref, qs, qs, qi, causal, m_ref, l_ref, acc_ref)
    res = []
    for hh in range(2):
        o = acc_ref[hh] / l_ref[hh]
        od = o[:, :tq] - lam * o[:, tq:]
        ms = jnp.mean(od * od, axis=0, keepdims=True)
        res.append(od * lax.rsqrt(ms + EPS) * sg_ref[...] * (1.0 - lam_init))
    o_ref[...] = jnp.concatenate(res, axis=0).T.astype(BF16)


def _diff_attention(qta, ka, vta, lamv, sg, *, bsz, seq, lam_init):
    tq = KT
    nq = seq // tq
    t = bsz * seq
    return pl.pallas_call(
        functools.partial(_diff_kernel, lam_init=lam_init),
        grid=(bsz, A_HEADS // 2, nq),
        in_specs=[
            pl.BlockSpec((1, 2 * LANES, tq), lambda b, hp, qi: (b * nq + qi, hp, 0)),
            pl.BlockSpec((seq, 2 * LANES), lambda b, hp, qi: (b, hp)),
            pl.BlockSpec((nq, 2 * HEAD_DIM, KT), lambda b, hp, qi: (b, hp, 0)),
            pl.BlockSpec((4, A_QK_DIM), lambda b, hp, qi: (0, 0)),
            pl.BlockSpec((HEAD_DIM, 1), lambda b, hp, qi: (0, 0)),
        ],
        out_specs=pl.BlockSpec((tq, LANES), lambda b, hp, qi: (b * nq + qi, hp)),
        out_shape=jax.ShapeDtypeStruct((t, A_HEADS * HEAD_DIM), BF16),
        scratch_shapes=[pltpu.VMEM((2, 1, 2 * tq), F32), pltpu.VMEM((2, 1, 2 * tq), F32),
                        pltpu.VMEM((2, HEAD_DIM, 2 * tq), F32)],
        compiler_params=pltpu.CompilerParams(dimension_semantics=("arbitrary",) * 3,
                                             vmem_limit_bytes=VMEM_LIMIT),
        name="diff_attn",
    )(qta, ka, vta, lamv, sg)


def _moba_kernel(qt_ref, k_ref, vt_ref, o_ref, m_ref, l_ref, acc_ref, km_ref):
    tq = qt_ref.shape[2]
    nb = vt_ref.shape[0]
    qi = pl.program_id(2)

    @pl.when(qi == 0)
    def _():
        lane1 = lax.broadcasted_iota(I32, (1, LANES), 1)
        for hh in range(2):
            km_ref[hh] = jnp.zeros((LANES, LANES), F32)

            def put_block(j, carry):
                blk = k_ref[pl.ds(pl.multiple_of(j * KT, KT), KT), hh * LANES:(hh + 1) * LANES].astype(F32)
                mean = jnp.sum(blk, axis=0, keepdims=True) * (1.0 / KT)
                km_ref[hh, pl.ds(SEL0 + j, 1), :] = jnp.where(lane1 < HEAD_DIM, mean, 0.0)
                return carry
            lax.fori_loop(0, nb, put_block, 0)

    row = lax.broadcasted_iota(I32, (LANES, tq), 0)
    kr = lax.broadcasted_iota(I32, (KT, tq), 0)
    qc = lax.broadcasted_iota(I32, (KT, tq), 1)
    causal = kr <= qc
    in_sel = (row >= SEL0) & (row < SEL0 + MAX_BLOCKS)
    q_diag, q_past = [], []
    for hh in range(2):
        qt = qt_ref[0, hh * LANES:(hh + 1) * LANES, :]
        km = km_ref[hh]
        km_hi = km.astype(BF16)
        km_lo = (km - km_hi.astype(F32)).astype(BF16)
        gate = (jnp.dot(km_hi, qt, preferred_element_type=F32)
                + jnp.dot(km_lo, qt, preferred_element_type=F32))
        cur = jnp.where((row >= SEL0) & (row < SEL0 + qi), gate, -jnp.inf)
        sel = jnp.zeros((LANES, tq), jnp.bool_)
        for _ in range(MOBA_TOPK):
            mx = jnp.max(cur, axis=0, keepdims=True)
            first = jnp.min(jnp.where(cur == mx, row, 4 * LANES), axis=0, keepdims=True)
            pick = (row == first) & (mx > -jnp.inf)
            sel = sel | pick
            cur = jnp.where(pick, -jnp.inf, cur)
        q_diag.append(jnp.where(in_sel, jnp.zeros_like(qt), qt))
        q_past.append(jnp.where(in_sel, jnp.where(sel, 0.0, NEG_BIG).astype(BF16), qt))
    _flash_causal_pair(k_ref, vt_ref, q_diag, q_past, qi, causal, m_ref, l_ref, acc_ref)
    res = [acc_ref[hh] / l_ref[hh] for hh in range(2)]
    o_ref[...] = jnp.concatenate(res, axis=0).T.astype(BF16)


def _moba_attention(qtb, kb, vtb, *, bsz, seq):
    tq = KT
    nq = seq // tq
    assert nq <= MAX_BLOCKS
    t = bsz * seq
    return pl.pallas_call(
        _moba_kernel,
        grid=(bsz, B_HEADS // 2, nq),
        in_specs=[
            pl.BlockSpec((1, 2 * LANES, tq), lambda b, hp, qi: (b * nq + qi, hp, 0)),
            pl.BlockSpec((seq, 2 * LANES), lambda b, hp, qi: (b, hp)),
            pl.BlockSpec((nq, 2 * HEAD_DIM, KT), lambda b, hp, qi: (b, hp, 0)),
        ],
        out_specs=pl.BlockSpec((tq, LANES), lambda b, hp, qi: (b * nq + qi, hp)),
        out_shape=jax.ShapeDtypeStruct((t, B_HEADS * HEAD_DIM), BF16),
        scratch_shapes=[pltpu.VMEM((2, 1, tq), F32), pltpu.VMEM((2, 1, tq), F32),
                        pltpu.VMEM((2, HEAD_DIM, tq), F32), pltpu.VMEM((2, LANES, LANES), F32)],
        compiler_params=pltpu.CompilerParams(dimension_semantics=("arbitrary",) * 3,
                                             vmem_limit_bytes=VMEM_LIMIT),
        name="moba_attn",
    )(qtb, kb, vtb)


def _swa_kernel(sink_ref, q_ref, ktp_ref, ktc_ref, vp_ref, vc_ref, o_ref, *, slopes):
    tq = q_ref.shape[0]
    n = pl.program_id(1)
    lane = lax.broadcasted_iota(I32, (tq, LANES), 1)
    r2 = lax.broadcasted_iota(I32, (tq, 2 * KT), 0)
    c2 = lax.broadcasted_iota(I32, (tq, 2 * KT), 1)
    rel = r2 + KT - c2
    mask = (rel >= 0) & (rel < WINDOW) & ((c2 >= KT) | (n > 0))
    relf = rel.astype(F32)
    res = []
    for hq in range(C_HEADS):
        kv = hq // C_GROUP
        qp = q_ref[:, (hq // 2) * LANES:(hq // 2 + 1) * LANES]
        qm = jnp.where((lane < HEAD_DIM) == (hq % 2 == 0), qp, jnp.zeros_like(qp))
        kt = jnp.concatenate([ktp_ref[0, kv * LANES:(kv + 1) * LANES, :],
                              ktc_ref[0, kv * LANES:(kv + 1) * LANES, :]], axis=1)
        vv = jnp.concatenate([vp_ref[:, kv * LANES:(kv + 1) * LANES],
                              vc_ref[:, kv * LANES:(kv + 1) * LANES]], axis=0)
        s = jnp.dot(qm, kt, preferred_element_type=F32)
        s = jnp.where(mask, s - slopes[hq] * relf, -jnp.inf)
        sink = sink_ref[hq]
        m = jnp.maximum(jnp.max(s, axis=-1, keepdims=True), sink)
        e = jnp.exp(s - m)
        den = jnp.sum(e, axis=-1, keepdims=True) + jnp.exp(sink - m)
        p = (e / den).astype(BF16)
        res.append(jnp.dot(p, vv, preferred_element_type=F32))
    for pr in range(C_HEADS // 2):
        o_ref[:, pr * LANES:(pr + 1) * LANES] = jnp.where(lane < HEAD_DIM, res[2 * pr], res[2 * pr + 1]).astype(BF16)


def _swa_attention(sinks, qc, ktc, vc2, *, bsz, seq, slopes):
    tq = KT
    nq = seq // tq
    t = bsz * seq
    return pl.pallas_call(
        functools.partial(_swa_kernel, slopes=slopes),
        grid=(bsz, nq),
        in_specs=[
            pl.BlockSpec(memory_space=pltpu.SMEM),
            pl.BlockSpec((tq, C_HEADS * HEAD_DIM), lambda b, n: (b * nq + n, 0)),
            pl.BlockSpec((1, C_KV_HEADS * LANES, KT), lambda b, n: (b * nq + jnp.maximum(n - 1, 0), 0, 0)),
            pl.BlockSpec((1, C_KV_HEADS * LANES, KT), lambda b, n: (b * nq + n, 0, 0)),
            pl.BlockSpec((tq, C_KV_HEADS * LANES), lambda b, n: (b * nq + jnp.maximum(n - 1, 0), 0)),
            pl.BlockSpec((tq, C_KV_HEADS * LANES), lambda b, n: (b * nq + n, 0)),
        ],
        out_specs=pl.BlockSpec((tq, C_HEADS * HEAD_DIM), lambda b, n: (b * nq + n, 0)),
        out_shape=jax.ShapeDtypeStruct((t, C_HEADS * HEAD_DIM), BF16),
        compiler_params=pltpu.CompilerParams(dimension_semantics=("arbitrary",) * 2,
                                             vmem_limit_bytes=VMEM_LIMIT),
        name="swa_attn",
    )(sinks, qc, ktc, ktc, vc2, vc2)


def _outproj_kernel(oa_ref, ob_ref, oc_ref, woa_ref, wob_ref, woc_ref, x_ref, mod_ref, g_ref, wq_ref, keys_ref,
                    x1_ref, h2_ref, st_ref):
    d = x_ref.shape[1]
    mix = (jnp.dot(oa_ref[...], woa_ref[...], preferred_element_type=F32)
           + jnp.dot(ob_ref[...], wob_ref[...], preferred_element_type=F32)
           + jnp.dot(oc_ref[...], woc_ref[...], preferred_element_type=F32))
    g1 = mod_ref[0, :, 2 * d:3 * d]
    sh2 = mod_ref[0, :, 3 * d:4 * d]
    sc2 = mod_ref[0, :, 4 * d:5 * d]
    x1 = x_ref[...] + g1 * mix
    x1_ref[...] = x1
    h2 = _rms_mod(x1, g_ref[...], sc2, sh2)
    h2_ref[...] = h2
    pq = jnp.dot(h2.astype(BF16), wq_ref[...], preferred_element_type=F32).astype(BF16)
    for hp in range(2 * PEER_HEADS):
        st_ref[hp] = lax.dot_general(keys_ref[hp], pq[:, hp * LANES:(hp + 1) * LANES],
                                     (((1,), (1,)), ((), ())), preferred_element_type=F32)


def _outproj(oa, ob, oc, woa, wob, woc, x2d, mod3, g, wq, keys, *, seq, tm=256):
    t, d = x2d.shape
    nt = t // tm
    nq = wq.shape[1]
    full = lambda a: pl.BlockSpec(a.shape, lambda i: (0,) * a.ndim)
    return pl.pallas_call(
        _outproj_kernel,
        grid=(nt,),
        in_specs=[
            pl.BlockSpec((tm, oa.shape[1]), lambda i: (i, 0)),
            pl.BlockSpec((tm, ob.shape[1]), lambda i: (i, 0)),
            pl.BlockSpec((tm, oc.shape[1]), lambda i: (i, 0)),
            full(woa), full(wob), full(woc),
            pl.BlockSpec((tm, d), lambda i: (i, 0)),
            pl.BlockSpec((1, 1, mod3.shape[2]), lambda i: ((i * tm) // seq, 0, 0)),
            pl.BlockSpec((1, d), lambda i: (0, 0)),
            full(wq), full(keys),
        ],
        out_specs=[pl.BlockSpec((tm, d), lambda i: (i, 0)),
                   pl.BlockSpec((tm, d), lambda i: (i, 0)),
                   pl.BlockSpec((2 * PEER_HEADS, PEER_NKEYS, tm), lambda i: (0, 0, i))],
        out_shape=[jax.ShapeDtypeStruct((t, d), F32), jax.ShapeDtypeStruct((t, d), F32),
                   jax.ShapeDtypeStruct((2 * PEER_HEADS, PEER_NKEYS, t), F32)],
        compiler_params=pltpu.CompilerParams(dimension_semantics=("arbitrary",),
                                             vmem_limit_bytes=VMEM_LIMIT),
        name="outproj_peerq",
    )(oa, ob, oc, woa, wob, woc, x2d, mod3, g, wq, keys)


_CAND_BLOCKS = ((0, 16),) + tuple((i, 8) for i in range(1, 8))
_CAND_ROWS = 16 + 7 * 8 + 8
_BIG_I = np.int32(2 ** 30)


def _topk_rows(curs, keys, val_refs, key_refs):
    def body(r, curs):
        out = []
        for cur, key, val_ref, key_ref in zip(curs, keys, val_refs, key_refs):
            m = jnp.max(cur, axis=0, keepdims=True)
            kmin = jnp.min(jnp.where(cur == m, key, _BIG_I), axis=0, keepdims=True)
            val_ref[pl.ds(r, 1), :] = m
            key_ref[pl.ds(r, 1), :] = kmin
            out.append(jnp.where(key == kmin, -jnp.inf, cur))
        return tuple(out)
    lax.fori_loop(0, PEER_TOPK, body, tuple(curs))


def _peer_topk_kernel(st_ref, eidx_ref, g_ref, v1_ref, k1_ref, v2_ref, k2_ref, vt_ref, kt_ref, ei_ref, gg_ref):
    tl = st_ref.shape[2]
    row = lax.broadcasted_iota(I32, (PEER_NKEYS, tl), 0)
    jrow8 = lax.broadcasted_iota(I32, (8, tl), 0)
    jrow16 = lax.broadcasted_iota(I32, (16, tl), 0)
    for h0 in range(0, PEER_HEADS, 2):
        cand, ckey = [], []
        for s in range(2):
            h = h0 + s
            _topk_rows([st_ref[2 * h], st_ref[2 * h + 1]], [row, row],
                       [v1_ref.at[s], v2_ref.at[s]], [k1_ref.at[s], k2_ref.at[s]])
            sv1, si1, sv2, si2 = v1_ref[s], k1_ref[s], v2_ref[s], k2_ref[s]
            cands, keys = [], []
            for i, nj in _CAND_BLOCKS:
                jrow = jrow16 if nj == 16 else jrow8
                cands.append(sv1[i:i + 1] + sv2[0:nj])
                keys.append((i * PEER_TOPK + jrow) * (PEER_NKEYS * PEER_NKEYS)
                            + si1[i:i + 1] * PEER_NKEYS + si2[0:nj])
            cands.append(sv1[8:16] + sv2[0:1])
            keys.append((jrow8 + 8) * (PEER_TOPK * PEER_NKEYS * PEER_NKEYS) + si1[8:16] * PEER_NKEYS + si2[0:1])
            cand.append(jnp.concatenate(cands, axis=0))
            ckey.append(jnp.concatenate(keys, axis=0))
        _topk_rows(cand, ckey, [vt_ref.at[0], vt_ref.at[1]], [kt_ref.at[0], kt_ref.at[1]])
        for s in range(2):
            h = h0 + s
            top = vt_ref[s]
            e = jnp.exp(top - top[0:1])
            gg_ref[h * PEER_TOPK:(h + 1) * PEER_TOPK, :] = e / jnp.sum(e, axis=0, keepdims=True)
            ei_ref[h * PEER_TOPK:(h + 1) * PEER_TOPK, :] = kt_ref[s] & (PEER_NKEYS * PEER_NKEYS - 1)
    eidx_ref[...] = ei_ref[...].T
    g_ref[...] = gg_ref[...].T


def _peer_topk(st, *, tl=128):
    t = st.shape[2]
    hk = PEER_HEADS * PEER_TOPK
    return pl.pallas_call(
        _peer_topk_kernel,
        grid=(t // tl,),
        in_specs=[pl.BlockSpec((2 * PEER_HEADS, PEER_NKEYS, tl), lambda i: (0, 0, i))],
        out_specs=[pl.BlockSpec((tl, hk), lambda i: (i, 0)), pl.BlockSpec((tl, hk), lambda i: (i, 0))],
        out_shape=[jax.ShapeDtypeStruct((t, hk), I32), jax.ShapeDtypeStruct((t, hk), F32)],
        scratch_shapes=[pltpu.VMEM((2, PEER_TOPK, tl), F32), pltpu.VMEM((2, PEER_TOPK, tl), I32),
                        pltpu.VMEM((2, PEER_TOPK, tl), F32), pltpu.VMEM((2, PEER_TOPK, tl), I32),
                        pltpu.VMEM((2, PEER_TOPK, tl), F32), pltpu.VMEM((2, PEER_TOPK, tl), I32),
                        pltpu.VMEM((hk, tl), I32), pltpu.VMEM((hk, tl), F32)],
        compiler_params=pltpu.CompilerParams(dimension_semantics=("arbitrary",),
                                             vmem_limit_bytes=VMEM_LIMIT),
        name="peer_topk",
    )(st)


def _erf(x):
    return lax.erf(x)


SUB = 8


def _peer_gather_kernel(idx_ref, idxn_ref, h_ref, g_ref, x1_ref, g2_ref, fg_ref, uv_hbm,
                        o_ref, buf, sem, *, tq, final):
    i = pl.program_id(0)
    n = pl.num_programs(0)
    hk = PEER_HEADS * PEER_TOPK
    groups = tq * hk // SUB
    nj = h_ref.shape[1]

    def issue(iref, slot):
        def one(ro, carry):
            for ri in range(SUB):
                pltpu.make_async_copy(uv_hbm.at[iref[ro * SUB + ri]], buf.at[slot, ro, :, ri, :],
                                      sem.at[slot]).start(priority=ri % 2)
            return carry
        lax.fori_loop(0, groups, one, 0)

    @pl.when(i == 0)
    def _():
        issue(idx_ref, 0)

    slot = i % 2

    @pl.when(i + 1 < n)
    def _():
        issue(idxn_ref, 1 - slot)

    pltpu.make_async_copy(buf.at[slot], buf.at[slot], sem.at[slot]).wait()

    gt = g_ref[...].T
    gpt = hk // SUB
    for tt in range(tq):
        ue = buf[slot, tt * gpt:(tt + 1) * gpt, 0:nj]
        ve = buf[slot, tt * gpt:(tt + 1) * gpt, nj:2 * nj]
        hb = jnp.broadcast_to(h_ref[tt][:, None, :], (nj, SUB, LANES))
        a = jnp.sum(jnp.sum(ue * hb[None], axis=1), axis=-1, keepdims=True)
        act = 0.5 * a * (1.0 + _erf(a * (2.0 ** -0.5)))
        w = gt[:, tt:tt + 1].reshape(gpt, SUB, 1) * act
        peer = jnp.sum(jnp.sum(w[:, None] * ve, axis=0), axis=1)
        y = x1_ref[tt] + g2_ref[0] * peer
        if final:
            ms = jnp.mean(jnp.mean(y * y, axis=-1, keepdims=True), axis=0, keepdims=True)
            y = y * lax.rsqrt(ms + EPS) * fg_ref[...]
        o_ref[tt] = y


def _peer_gather(eidx_flat, h2, g, x1, g2, final_g, uv, *, seq, final, tq=8):
    t, nj, _ = h2.shape
    hk = PEER_HEADS * PEER_TOPK
    rows = tq * hk
    n = t // tq
    return pl.pallas_call(
        functools.partial(_peer_gather_kernel, tq=tq, final=final),
        grid=(n,),
        in_specs=[
            pl.BlockSpec((rows,), lambda i: (i,), memory_space=pltpu.SMEM),
            pl.BlockSpec((rows,), lambda i: (jnp.minimum(i + 1, n - 1),), memory_space=pltpu.SMEM),
            pl.BlockSpec((tq, nj, LANES), lambda i: (i, 0, 0)),
            pl.BlockSpec((tq, hk), lambda i: (i, 0)),
            pl.BlockSpec((tq, nj, LANES), lambda i: (i, 0, 0)),
            pl.BlockSpec((1, nj, LANES), lambda i: ((i * tq) // seq, 0, 0)),
            pl.BlockSpec((nj, LANES), lambda i: (0, 0)),
            pl.BlockSpec(memory_space=pl.ANY),
        ],
        out_specs=pl.BlockSpec((tq, nj, LANES), lambda i: (i, 0, 0)),
        out_shape=jax.ShapeDtypeStruct((t, nj, LANES), F32),
        scratch_shapes=[pltpu.VMEM((2, rows // SUB, 2 * nj, SUB, LANES), F32), pltpu.SemaphoreType.DMA((2,))],
        compiler_params=pltpu.CompilerParams(dimension_semantics=("arbitrary",),
                                             vmem_limit_bytes=VMEM_LIMIT),
        name="peer_gather",
    )(eidx_flat, eidx_flat, h2, g, x1, g2, final_g, uv)


def kernel(x, c, norm1_g, norm2_g, w_ada, b_ada, w_in, w_out, lam_q1, lam_k1, lam_q2, lam_k2, subln_g, sinks,
           peer_wq, peer_keys, peer_u, peer_v, final_g):
    bsz, seq, d = x.shape
    depth = w_in.shape[0]
    t = bsz * seq
    slopes = _alibi_slopes()
    sl_c = [float(s) for s in slopes[:C_HEADS]]
    qa_bias = _slope_bias_col(slopes[C_HEADS:C_HEADS + A_HEADS] * np.float32(LOG2E))
    qb_bias = _slope_bias_col(slopes[C_HEADS + A_HEADS:] * np.float32(LOG2E))
    mods = _adaln_mods(c, w_ada, b_ada)
    x2d = x.reshape(t, d)
    av, bw = A_HEADS * HEAD_DIM, B_HEADS * HEAD_DIM
    for l in range(depth):
        lam_init = 0.8 - 0.6 * math.exp(-0.3 * l)
        mod3 = mods[l].reshape(bsz, 1, N_MOD * d)
        wn, wt = _prep_in_weights(w_in[l])
        ka, kb, qc, vc2, qta, vta, qtb, vtb, ktc = _inproj(
            x2d, mod3, norm1_g[l].reshape(1, d), wn, wt, qa_bias, qb_bias, seq=seq)
        lamv = jnp.stack([lam_q1[l], lam_k1[l], lam_q2[l], lam_k2[l]]).astype(F32)
        sg = subln_g[l].reshape(HEAD_DIM, 1).astype(F32)
        oa = _diff_attention(qta, ka, vta, lamv, sg, bsz=bsz, seq=seq, lam_init=lam_init)
        ob = _moba_attention(qtb, kb, vtb, bsz=bsz, seq=seq)
        oc = _swa_attention(sinks[l].astype(F32), qc, ktc, vc2, bsz=bsz, seq=seq, slopes=sl_c)
        wo = w_out[l].astype(BF16)
        keys = peer_keys[l].reshape(2 * PEER_HEADS, PEER_NKEYS, -1).astype(BF16)
        x1, h2, st = _outproj(oa, ob, oc, wo[:av], wo[av:av + bw], wo[av + bw:], x2d, mod3,
                              norm2_g[l].reshape(1, d), peer_wq[l].astype(BF16), keys, seq=seq)
        eidx, g = _peer_topk(st)
        nj = d // LANES
        uv = jnp.concatenate([peer_u[l].reshape(-1, nj, LANES), peer_v[l].reshape(-1, nj, LANES)], axis=1)
        g2 = mods[l][:, 5 * d:6 * d].reshape(bsz, nj, LANES)
        x2d = _peer_gather(eidx.reshape(-1), h2.reshape(t, nj, LANES), g, x1.reshape(t, nj, LANES), g2,
                           final_g.reshape(nj, LANES), uv, seq=seq, final=(l == depth - 1)).reshape(t, d)
    return x2d.reshape(bsz, seq, d)
```

```python
import functools
import math

import numpy as np
import jax
import jax.numpy as jnp
from jax import lax
from jax.experimental import pallas as pl
from jax.experimental.pallas import tpu as pltpu

F32 = jnp.float32
BF16 = jnp.bfloat16
I32 = jnp.int32

D_MODEL = 1024
HEAD_DIM = 64
N_HEADS_TOTAL = 16
A_HEADS = 4
B_HEADS = 6
C_HEADS = 6
C_KV_HEADS = 2
C_GROUP = 3
A_QK_DIM = 32
MOBA_BLOCK = 256
MOBA_TOPK = 3
WINDOW = 128
ALIBI_MAX = 8.0
PEER_HEADS = 8
PEER_NKEYS = 128
PEER_TOPK = 16
N_MOD = 6
EPS = 1e-6

LANES = 128
KT = 256
AUG0 = HEAD_DIM
SEL0 = AUG0 + 6
MAX_BLOCKS = LANES - SEL0
NEG_BIG = -1e30
LOG2E = math.log2(math.e)
VMEM_LIMIT = 56 * 1024 * 1024


def _alibi_slopes():
    n = N_HEADS_TOTAL
    return (2.0 ** (-ALIBI_MAX * np.arange(1, n + 1, dtype=np.float32) / n)).astype(np.float32)


def _split3(v):
    v = np.float32(v)
    hi = np.float32(np.asarray(v).astype(jnp.bfloat16).astype(np.float32))
    r = np.float32(v - hi)
    mid = np.float32(np.asarray(r).astype(jnp.bfloat16).astype(np.float32))
    lo = np.float32(np.float32(r - mid))
    lo = np.float32(np.asarray(lo).astype(jnp.bfloat16).astype(np.float32))
    return hi, mid, lo


def _slope_bias_col(slopes):
    col = np.zeros((LANES * len(slopes), 1), np.float32)
    for h, s in enumerate(slopes):
        hi, mid, lo = _split3(s)
        col[h * LANES + AUG0:h * LANES + AUG0 + 6, 0] = [hi, mid, lo, hi, mid, lo]
    return jnp.asarray(col)


def _mod_kernel(c_ref, w_ref, b_ref, o_ref):
    c = c_ref[...]
    cs = c * (1.0 / (1.0 + jnp.exp(-c)))
    o_ref[0] = jnp.dot(cs, w_ref[0], preferred_element_type=F32) + b_ref[0]


def _adaln_mods(c, w_ada, b_ada):
    depth, d, n = w_ada.shape
    bsz = c.shape[0]
    rows = -(-bsz // 8) * 8
    cp = jnp.pad(c, ((0, rows - bsz), (0, 0)))
    tn = 1536
    out = pl.pallas_call(
        _mod_kernel,
        grid=(depth, n // tn),
        in_specs=[
            pl.BlockSpec((rows, d), lambda l, j: (0, 0)),
            pl.BlockSpec((1, d, tn), lambda l, j: (l, 0, j)),
            pl.BlockSpec((1, 1, tn), lambda l, j: (l, 0, j)),
        ],
        out_specs=pl.BlockSpec((1, rows, tn), lambda l, j: (l, 0, j)),
        out_shape=jax.ShapeDtypeStruct((depth, rows, n), F32),
        compiler_params=pltpu.CompilerParams(dimension_semantics=("arbitrary", "arbitrary"),
                                             vmem_limit_bytes=VMEM_LIMIT),
        name="adaln_mods",
    )(cp, w_ada, b_ada.reshape(depth, 1, n))
    return out[:, :bsz]


NN_WIDTHS = (A_HEADS * LANES, B_HEADS * LANES, C_HEADS * HEAD_DIM, 2 * C_KV_HEADS * HEAD_DIM)
NT_ROWS = (A_HEADS * LANES, A_HEADS * HEAD_DIM, B_HEADS * LANES, B_HEADS * HEAD_DIM,
           C_KV_HEADS * LANES)


def _prep_in_weights(w):
    d = w.shape[0]
    aq, ak, av = A_HEADS * 2 * A_QK_DIM, A_HEADS * 2 * A_QK_DIM, A_HEADS * HEAD_DIM
    bw = B_HEADS * HEAD_DIM
    cq, ckv = C_HEADS * HEAD_DIM, C_KV_HEADS * HEAD_DIM
    cuts = np.cumsum([aq, ak, av, bw, bw, bw, cq, ckv]).tolist()
    qa, ka, va, qb, kb, vb, qc, kc, vc = jnp.split(w, cuts, axis=-1)

    def pad_heads(m, nh, scale):
        m = (m * scale).reshape(d, nh, HEAD_DIM)
        return jnp.pad(m, ((0, 0), (0, 0), (0, LANES - HEAD_DIM))).reshape(d, nh * LANES)

    vc2 = vc.reshape(d, C_KV_HEADS, 1, HEAD_DIM)
    vc2 = jnp.broadcast_to(vc2, (d, C_KV_HEADS, 2, HEAD_DIM)).reshape(d, 2 * ckv)
    kc2 = jnp.broadcast_to(kc.reshape(d, C_KV_HEADS, 1, HEAD_DIM), (d, C_KV_HEADS, 2, HEAD_DIM)).reshape(d, 2 * ckv)
    wn = jnp.concatenate([pad_heads(ka, A_HEADS, 1.0), pad_heads(kb, B_HEADS, 1.0),
                          qc * (HEAD_DIM ** -0.5), vc2], axis=1)
    wt = jnp.concatenate([pad_heads(qa, A_HEADS, A_QK_DIM ** -0.5 * LOG2E), va,
                          pad_heads(qb, B_HEADS, HEAD_DIM ** -0.5 * LOG2E), vb, kc2], axis=1).T
    return wn.astype(BF16), wt.astype(BF16)


def _rms_mod(x, g, sc, sh):
    ms = jnp.mean(x * x, axis=-1, keepdims=True)
    return (x * lax.rsqrt(ms + EPS) * g) * (1.0 + sc) + sh


def _inproj_kernel(x_ref, mod_ref, g_ref, wn_ref, wt_ref, qab_ref, qbb_ref,
                   ka_ref, kb_ref, qc_ref, vc_ref, qta_ref, vta_ref, qtb_ref, vtb_ref, ktc_ref,
                   *, tm, seq):
    d = x_ref.shape[1]
    x = x_ref[...]
    sh = mod_ref[0, :, 0:d]
    sc = mod_ref[0, :, d:2 * d]
    h = _rms_mod(x, g_ref[...], sc, sh).astype(BF16)
    pn = jnp.dot(h, wn_ref[...], preferred_element_type=F32)
    pt = lax.dot_general(wt_ref[...], h, (((1,), (1,)), ((), ())),
                         preferred_element_type=F32)

    pos = (pl.program_id(0) * tm) % seq + lax.broadcasted_iota(I32, (tm, LANES), 0)
    col = lax.broadcasted_iota(I32, (tm, LANES), 1)
    blk_id = pos // KT
    p_hi = (blk_id * KT).astype(F32)
    p_lo = (pos - blk_id * KT).astype(F32)
    aug_a = jnp.where((col >= AUG0) & (col < AUG0 + 3), p_hi,
                      jnp.where((col >= AUG0 + 3) & (col < AUG0 + 6), p_lo, 0.0))
    aug_b = jnp.where((col >= SEL0) & (col - SEL0 == blk_id), 1.0, aug_a)
    for hh in range(A_HEADS):
        ka_ref[:, hh * LANES:(hh + 1) * LANES] = (pn[:, hh * LANES:(hh + 1) * LANES] + aug_a).astype(BF16)
    o = NN_WIDTHS[0]
    for hh in range(B_HEADS):
        kb_ref[:, hh * LANES:(hh + 1) * LANES] = (pn[:, o + hh * LANES:o + (hh + 1) * LANES] + aug_b).astype(BF16)
    o += NN_WIDTHS[1]
    qc_ref[...] = pn[:, o:o + NN_WIDTHS[2]].astype(BF16)
    o += NN_WIDTHS[2]
    vc_ref[...] = pn[:, o:o + NN_WIDTHS[3]].astype(BF16)

    r0 = 0
    for ref, b, nr in zip((qta_ref, vta_ref, qtb_ref, vtb_ref, ktc_ref),
                          (qab_ref, None, qbb_ref, None, None), NT_ROWS):
        blk = pt[r0:r0 + nr, :]
        if b is not None:
            blk = blk + b[...]
        blk = blk.astype(BF16)
        for cc in range(tm // KT):
            ref[cc] = blk[:, cc * KT:(cc + 1) * KT]
        r0 += nr


def _inproj(x2d, mod3, g, wn, wt, qa_bias, qb_bias, *, seq, tm=512):
    t, d = x2d.shape
    assert seq % tm == 0 and tm % KT == 0
    nt = t // tm
    nn_total = sum(NN_WIDTHS)
    row_specs = [pl.BlockSpec((tm, wd), lambda i: (i, 0)) for wd in NN_WIDTHS]
    kt_specs = [pl.BlockSpec((tm // KT, r, KT), lambda i: (i, 0, 0)) for r in NT_ROWS]
    out_shape = ([jax.ShapeDtypeStruct((t, wd), BF16) for wd in NN_WIDTHS]
                 + [jax.ShapeDtypeStruct((t // KT, r, KT), BF16) for r in NT_ROWS])
    return pl.pallas_call(
        functools.partial(_inproj_kernel, tm=tm, seq=seq),
        grid=(nt,),
        in_specs=[
            pl.BlockSpec((tm, d), lambda i: (i, 0)),
            pl.BlockSpec((1, 1, mod3.shape[2]), lambda i: ((i * tm) // seq, 0, 0)),
            pl.BlockSpec((1, d), lambda i: (0, 0)),
            pl.BlockSpec((d, nn_total), lambda i: (0, 0)),
            pl.BlockSpec((sum(NT_ROWS), d), lambda i: (0, 0)),
            pl.BlockSpec((NT_ROWS[0], 1), lambda i: (0, 0)),
            pl.BlockSpec((NT_ROWS[2], 1), lambda i: (0, 0)),
        ],
        out_specs=row_specs + kt_specs,
        out_shape=out_shape,
        compiler_params=pltpu.CompilerParams(dimension_semantics=("arbitrary",),
                                             vmem_limit_bytes=VMEM_LIMIT),
        name="inproj",
    )(x2d, mod3, g, wn, wt, qa_bias, qb_bias)


def _flash_step(k, qt, vt, m_ref, l_ref, acc_ref, mask=None):
    s = jnp.dot(k, qt, preferred_element_type=F32)
    if mask is not None:
        s = jnp.where(mask, s, -jnp.inf)
    m_prev = m_ref[...]
    m_new = jnp.maximum(m_prev, jnp.max(s, axis=0, keepdims=True))
    alpha = jnp.exp2(m_prev - m_new)
    p = jnp.exp2(s - m_new)
    l_ref[...] = alpha * l_ref[...] + jnp.sum(p, axis=0, keepdims=True)
    acc_ref[...] = alpha * acc_ref[...] + jnp.dot(vt, p.astype(BF16), preferred_element_type=F32)
    m_ref[...] = m_new


def _flash_init(m_ref, l_ref, acc_ref):
    m_ref[...] = jnp.full(m_ref.shape, -jnp.inf, F32)
    l_ref[...] = jnp.zeros(l_ref.shape, F32)
    acc_ref[...] = jnp.zeros(acc_ref.shape, F32)


PAST_TILES = 4


def _flash_causal_pair(k_ref, vt_ref, q_diag, q_past, qi, causal, m_ref, l_ref, acc_ref):
    _flash_init(m_ref, l_ref, acc_ref)

    def step(kj, n, qs, mask):
        kk = k_ref[pl.ds(pl.multiple_of(kj * KT, KT), n * KT), :]
        for hh in range(2):
            rows = slice(hh * HEAD_DIM, (hh + 1) * HEAD_DIM)
            vt = vt_ref[kj, rows, :] if n == 1 else jnp.concatenate(
                [vt_ref[kj + c, rows, :] for c in range(n)], axis=1)
            _flash_step(kk[:, hh * LANES:(hh + 1) * LANES], qs[hh], vt,
                        m_ref.at[hh], l_ref.at[hh], acc_ref.at[hh], mask)

    step(qi, 1, q_diag, causal)

    def body(j, carry):
        step(j * PAST_TILES, PAST_TILES, q_past, None)
        return carry

    lax.fori_loop(0, qi // PAST_TILES, body, 0)
    for r in range(PAST_TILES - 1):
        @pl.when(qi % PAST_TILES > r)
        def _():
            step(qi - 1 - r, 1, q_past, None)


def _diff_kernel(qt_ref, k_ref, vt_ref, lamv_ref, sg_ref, o_ref, m_ref, l_ref, acc_ref, *, lam_init):
    tq = qt_ref.shape[2]
    qi = pl.program_id(2)
    lv = lamv_ref[...]
    lam = (jnp.exp(jnp.sum(lv[0:1] * lv[1:2], axis=-1, keepdims=True))
           - jnp.exp(jnp.sum(lv[2:3] * lv[3:4], axis=-1, keepdims=True)) + lam_init)
    row = lax.broadcasted_iota(I32, (LANES, tq), 0)
    kr = lax.broadcasted_iota(I32, (KT, 2 * tq), 0)
    qc = lax.broadcasted_iota(I32, (KT, 2 * tq), 1)
    causal = kr <= jnp.where(qc >= tq, qc - tq, qc)
    qs = []
    for hh in range(2):
        qt = qt_ref[0, hh * LANES:(hh + 1) * LANES, :]
        zero = jnp.zeros_like(qt)
        q1 = jnp.where((row < A_QK_DIM) | (row >= AUG0), qt, zero)
        q2 = jnp.where(row >= A_QK_DIM, qt, zero)
        qs.append(jnp.concatenate([q1, q2], axis=1))
    _flash_causal_pair(k_ref, vt_ref, qs, qs, qi, causal, m_ref, l_ref, acc_ref)
    res = []
    for hh in range(2):
        o = acc_ref[hh] / l_ref[hh]
        od = o[:, :tq] - lam * o[:, tq:]
        ms = jnp.mean(od * od, axis=0, keepdims=True)
        res.append(od * lax.rsqrt(ms + EPS) * sg_ref[...] * (1.0 - lam_init))
    o_ref[...] = jnp.concatenate(res, axis=0).T.astype(BF16)


def _diff_attention(qta, ka, vta, lamv, sg, *, bsz, seq, lam_init):
    tq = KT
    nq = seq // tq
    t = bsz * seq
    return pl.pallas_call(
        functools.partial(_diff_kernel, lam_init=lam_init),
        grid=(bsz, A_HEADS // 2, nq),
        in_specs=[
            pl.BlockSpec((1, 2 * LANES, tq), lambda b, hp, qi: (b * nq + qi, hp, 0)),
            pl.BlockSpec((seq, 2 * LANES), lambda b, hp, qi: (b, hp)),
            pl.BlockSpec((nq, 2 * HEAD_DIM, KT), lambda b, hp, qi: (b, hp, 0)),
            pl.BlockSpec((4, A_QK_DIM), lambda b, hp, qi: (0, 0)),
            pl.BlockSpec((HEAD_DIM, 1), lambda b, hp, qi: (0, 0)),
        ],
        out_specs=pl.BlockSpec((tq, LANES), lambda b, hp, qi: (b * nq + qi, hp)),
        out_shape=jax.ShapeDtypeStruct((t, A_HEADS * HEAD_DIM), BF16),
        scratch_shapes=[pltpu.VMEM((2, 1, 2 * tq), F32), pltpu.VMEM((2, 1, 2 * tq), F32),
                        pltpu.VMEM((2, HEAD_DIM, 2 * tq), F32)],
        compiler_params=pltpu.CompilerParams(dimension_semantics=("arbitrary",) * 3,
                                             vmem_limit_bytes=VMEM_LIMIT),
        name="diff_attn",
    )(qta, ka, vta, lamv, sg)


def _moba_kernel(qt_ref, k_ref, vt_ref, o_ref, m_ref, l_ref, acc_ref, km_ref):
    tq = qt_ref.shape[2]
    nb = vt_ref.shape[0]
    qi = pl.program_id(2)

    @pl.when(qi == 0)
    def _():
        lane1 = lax.broadcasted_iota(I32, (1, LANES), 1)
        for hh in range(2):
            km_ref[hh] = jnp.zeros((LANES, LANES), F32)

            def put_block(j, carry):
                blk = k_ref[pl.ds(pl.multiple_of(j * KT, KT), KT), hh * LANES:(hh + 1) * LANES].astype(F32)
                mean = jnp.sum(blk, axis=0, keepdims=True) * (1.0 / KT)
                km_ref[hh, pl.ds(SEL0 + j, 1), :] = jnp.where(lane1 < HEAD_DIM, mean, 0.0)
                return carry
            lax.fori_loop(0, nb, put_block, 0)

    row = lax.broadcasted_iota(I32, (LANES, tq), 0)
    kr = lax.broadcasted_iota(I32, (KT, tq), 0)
    qc = lax.broadcasted_iota(I32, (KT, tq), 1)
    causal = kr <= qc
    in_sel = (row >= SEL0) & (row < SEL0 + MAX_BLOCKS)
    q_diag, q_past = [], []
    for hh in range(2):
        qt = qt_ref[0, hh * LANES:(hh + 1) * LANES, :]
        km = km_ref[hh]
        km_hi = km.astype(BF16)
        km_lo = (km - km_hi.astype(F32)).astype(BF16)
        gate = (jnp.dot(km_hi, qt, preferred_element_type=F32)
                + jnp.dot(km_lo, qt, preferred_element_type=F32))
        cur = jnp.where((row >= SEL0) & (row < SEL0 + qi), gate, -jnp.inf)
        sel = jnp.zeros((LANES, tq), jnp.bool_)
        for _ in range(MOBA_TOPK):
            mx = jnp.max(cur, axis=0, keepdims=True)
            first = jnp.min(jnp.where(cur == mx, row, 4 * LANES), axis=0, keepdims=True)
            pick = (row == first) & (mx > -jnp.inf)
            sel = sel | pick
            cur = jnp.where(pick, -jnp.inf, cur)
        q_diag.append(jnp.where(in_sel, jnp.zeros_like(qt), qt))
        q_past.append(jnp.where(in_sel, jnp.where(sel, 0.0, NEG_BIG).astype(BF16), qt))
    _flash_causal_pair(k_ref, vt_ref, q_diag, q_past, qi, causal, m_ref, l_ref, acc_ref)
    res = [acc_ref[hh] / l_ref[hh] for hh in range(2)]
    o_ref[...] = jnp.concatenate(res, axis=0).T.astype(BF16)


def _moba_attention(qtb, kb, vtb, *, bsz, seq):
    tq = KT
    nq = seq // tq
    assert nq <= MAX_BLOCKS
    t = bsz * seq
    return pl.pallas_call(
        _moba_kernel,
        grid=(bsz, B_HEADS // 2, nq),
        in_specs=[
            pl.BlockSpec((1, 2 * LANES, tq), lambda b, hp, qi: (b * nq + qi, hp, 0)),
            pl.BlockSpec((seq, 2 * LANES), lambda b, hp, qi: (b, hp)),
            pl.BlockSpec((nq, 2 * HEAD_DIM, KT), lambda b, hp, qi: (b, hp, 0)),
        ],
        out_specs=pl.BlockSpec((tq, LANES), lambda b, hp, qi: (b * nq + qi, hp)),
        out_shape=jax.ShapeDtypeStruct((t, B_HEADS * HEAD_DIM), BF16),
        scratch_shapes=[pltpu.VMEM((2, 1, tq), F32), pltpu.VMEM((2, 1, tq), F32),
                        pltpu.VMEM((2, HEAD_DIM, tq), F32), pltpu.VMEM((2, LANES, LANES), F32)],
        compiler_params=pltpu.CompilerParams(dimension_semantics=("arbitrary",) * 3,
                                             vmem_limit_bytes=VMEM_LIMIT),
        name="moba_attn",
    )(qtb, kb, vtb)


def _swa_kernel(sink_ref, q_ref, ktp_ref, ktc_ref, vp_ref, vc_ref, o_ref, *, slopes):
    tq = q_ref.shape[0]
    n = pl.program_id(1)
    lane = lax.broadcasted_iota(I32, (tq, LANES), 1)
    r2 = lax.broadcasted_iota(I32, (tq, 2 * KT), 0)
    c2 = lax.broadcasted_iota(I32, (tq, 2 * KT), 1)
    rel = r2 + KT - c2
    mask = (rel >= 0) & (rel < WINDOW) & ((c2 >= KT) | (n > 0))
    relf = rel.astype(F32)
    res = []
    for hq in range(C_HEADS):
        kv = hq // C_GROUP
        qp = q_ref[:, (hq // 2) * LANES:(hq // 2 + 1) * LANES]
        qm = jnp.where((lane < HEAD_DIM) == (hq % 2 == 0), qp, jnp.zeros_like(qp))
        kt = jnp.concatenate([ktp_ref[0, kv * LANES:(kv + 1) * LANES, :],
                              ktc_ref[0, kv * LANES:(kv + 1) * LANES, :]], axis=1)
        vv = jnp.concatenate([vp_ref[:, kv * LANES:(kv + 1) * LANES],
                              vc_ref[:, kv * LANES:(kv + 1) * LANES]], axis=0)
        s = jnp.dot(qm, kt, preferred_element_type=F32)
        s = jnp.where(mask, s - slopes[hq] * relf, -jnp.inf)
        sink = sink_ref[hq]
        m = jnp.maximum(jnp.max(s, axis=-1, keepdims=True), sink)
        e = jnp.exp(s - m)
        den = jnp.sum(e, axis=-1, keepdims=True) + jnp.exp(sink - m)
        p = (e / den).astype(BF16)
        res.append(jnp.dot(p, vv, preferred_element_type=F32))
    for pr in range(C_HEADS // 2):
        o_ref[:, pr * LANES:(pr + 1) * LANES] = jnp.where(lane < HEAD_DIM, res[2 * pr], res[2 * pr + 1]).astype(BF16)


def _swa_attention(sinks, qc, ktc, vc2, *, bsz, seq, slopes):
    tq = KT
    nq = seq // tq
    t = bsz * seq
    return pl.pallas_call(
        functools.partial(_swa_kernel, slopes=slopes),
        grid=(bsz, nq),
        in_specs=[
            pl.BlockSpec(memory_space=pltpu.SMEM),
            pl.BlockSpec((tq, C_HEADS * HEAD_DIM), lambda b, n: (b * nq + n, 0)),
            pl.BlockSpec((1, C_KV_HEADS * LANES, KT), lambda b, n: (b * nq + jnp.maximum(n - 1, 0), 0, 0)),
            pl.BlockSpec((1, C_KV_HEADS * LANES, KT), lambda b, n: (b * nq + n, 0, 0)),
            pl.BlockSpec((tq, C_KV_HEADS * LANES), lambda b, n: (b * nq + jnp.maximum(n - 1, 0), 0)),
            pl.BlockSpec((tq, C_KV_HEADS * LANES), lambda b, n: (b * nq + n, 0)),
        ],
        out_specs=pl.BlockSpec((tq, C_HEADS * HEAD_DIM), lambda b, n: (b * nq + n, 0)),
        out_shape=jax.ShapeDtypeStruct((t, C_HEADS * HEAD_DIM), BF16),
        compiler_params=pltpu.CompilerParams(dimension_semantics=("arbitrary",) * 2,
                                             vmem_limit_bytes=VMEM_LIMIT),
        name="swa_attn",
    )(sinks, qc, ktc, ktc, vc2, vc2)


def _outproj_kernel(oa_ref, ob_ref, oc_ref, woa_ref, wob_ref, woc_ref, x_ref, mod_ref, g_ref, wq_ref, keys_ref,
                    x1_ref, h2_ref, st_ref):
    d = x_ref.shape[1]
    mix = (jnp.dot(oa_ref[...], woa_ref[...], preferred_element_type=F32)
           + jnp.dot(ob_ref[...], wob_ref[...], preferred_element_type=F32)
           + jnp.dot(oc_ref[...], woc_ref[...], preferred_element_type=F32))
    g1 = mod_ref[0, :, 2 * d:3 * d]
    sh2 = mod_ref[0, :, 3 * d:4 * d]
    sc2 = mod_ref[0, :, 4 * d:5 * d]
    x1 = x_ref[...] + g1 * mix
    x1_ref[...] = x1
    h2 = _rms_mod(x1, g_ref[...], sc2, sh2)
    h2_ref[...] = h2
    pq = jnp.dot(h2.astype(BF16), wq_ref[...], preferred_element_type=F32).astype(BF16)
    for hp in range(2 * PEER_HEADS):
        st_ref[hp] = lax.dot_general(keys_ref[hp], pq[:, hp * LANES:(hp + 1) * LANES],
                                     (((1,), (1,)), ((), ())), preferred_element_type=F32)


def _outproj(oa, ob, oc, woa, wob, woc, x2d, mod3, g, wq, keys, *, seq, tm=256):
    t, d = x2d.shape
    nt = t // tm
    nq = wq.shape[1]
    full = lambda a: pl.BlockSpec(a.shape, lambda i: (0,) * a.ndim)
    return pl.pallas_call(
        _outproj_kernel,
        grid=(nt,),
        in_specs=[
            pl.BlockSpec((tm, oa.shape[1]), lambda i: (i, 0)),
            pl.BlockSpec((tm, ob.shape[1]), lambda i: (i, 0)),
            pl.BlockSpec((tm, oc.shape[1]), lambda i: (i, 0)),
            full(woa), full(wob), full(woc),
            pl.BlockSpec((tm, d), lambda i: (i, 0)),
            pl.BlockSpec((1, 1, mod3.shape[2]), lambda i: ((i * tm) // seq, 0, 0)),
            pl.BlockSpec((1, d), lambda i: (0, 0)),
            full(wq), full(keys),
        ],
        out_specs=[pl.BlockSpec((tm, d), lambda i: (i, 0)),
                   pl.BlockSpec((tm, d), lambda i: (i, 0)),
                   pl.BlockSpec((2 * PEER_HEADS, PEER_NKEYS, tm), lambda i: (0, 0, i))],
        out_shape=[jax.ShapeDtypeStruct((t, d), F32), jax.ShapeDtypeStruct((t, d), F32),
                   jax.ShapeDtypeStruct((2 * PEER_HEADS, PEER_NKEYS, t), F32)],
        compiler_params=pltpu.CompilerParams(dimension_semantics=("arbitrary",),
                                             vmem_limit_bytes=VMEM_LIMIT),
        name="outproj_peerq",
    )(oa, ob, oc, woa, wob, woc, x2d, mod3, g, wq, keys)


_CAND_BLOCKS = ((0, 16),) + tuple((i, 8) for i in range(1, 8))
_CAND_ROWS = 16 + 7 * 8 + 8
_BIG_I = np.int32(2 ** 30)


def _topk_rows(curs, keys, val_refs, key_refs):
    def body(r, curs):
        out = []
        for cur, key, val_ref, key_ref in zip(curs, keys, val_refs, key_refs):
            m = jnp.max(cur, axis=0, keepdims=True)
            kmin = jnp.min(jnp.where(cur == m, key, _BIG_I), axis=0, keepdims=True)
            val_ref[pl.ds(r, 1), :] = m
            key_ref[pl.ds(r, 1), :] = kmin
            out.append(jnp.where(key == kmin, -jnp.inf, cur))
        return tuple(out)
    lax.fori_loop(0, PEER_TOPK, body, tuple(curs))


def _peer_topk_kernel(st_ref, off_ref, sh_ref, g_ref, v1_ref, k1_ref, v2_ref, k2_ref, vt_ref, kt_ref, ei_ref, gg_ref):
    tl = st_ref.shape[2]
    row = lax.broadcasted_iota(I32, (PEER_NKEYS, tl), 0)
    jrow8 = lax.broadcasted_iota(I32, (8, tl), 0)
    jrow16 = lax.broadcasted_iota(I32, (16, tl), 0)
    for h0 in range(0, PEER_HEADS, 2):
        cand, ckey = [], []
        for s in range(2):
            h = h0 + s
            _topk_rows([st_ref[2 * h], st_ref[2 * h + 1]], [row, row],
                       [v1_ref.at[s], v2_ref.at[s]], [k1_ref.at[s], k2_ref.at[s]])
            sv1, si1, sv2, si2 = v1_ref[s], k1_ref[s], v2_ref[s], k2_ref[s]
            cands, keys = [], []
            for i, nj in _CAND_BLOCKS:
                jrow = jrow16 if nj == 16 else jrow8
                cands.append(sv1[i:i + 1] + sv2[0:nj])
                keys.append((i * PEER_TOPK + jrow) * (PEER_NKEYS * PEER_NKEYS)
                            + si1[i:i + 1] * PEER_NKEYS + si2[0:nj])
            cands.append(sv1[8:16] + sv2[0:1])
            keys.append((jrow8 + 8) * (PEER_TOPK * PEER_NKEYS * PEER_NKEYS) + si1[8:16] * PEER_NKEYS + si2[0:1])
            cand.append(jnp.concatenate(cands, axis=0))
            ckey.append(jnp.concatenate(keys, axis=0))
        _topk_rows(cand, ckey, [vt_ref.at[0], vt_ref.at[1]], [kt_ref.at[0], kt_ref.at[1]])
        for s in range(2):
            h = h0 + s
            top = vt_ref[s]
            e = jnp.exp(top - top[0:1])
            gg_ref[h * PEER_TOPK:(h + 1) * PEER_TOPK, :] = e / jnp.sum(e, axis=0, keepdims=True)
            ei_ref[h * PEER_TOPK:(h + 1) * PEER_TOPK, :] = kt_ref[s] & (PEER_NKEYS * PEER_NKEYS - 1)
    ei = ei_ref[...].T
    off_ref[...] = (ei >> 1) * SUB
    sh_ref[...] = 16 - ((ei & 1) << 4)
    g_ref[...] = gg_ref[...].T


def _peer_topk(st, *, tl=128):
    t = st.shape[2]
    hk = PEER_HEADS * PEER_TOPK
    return pl.pallas_call(
        _peer_topk_kernel,
        grid=(t // tl,),
        in_specs=[pl.BlockSpec((2 * PEER_HEADS, PEER_NKEYS, tl), lambda i: (0, 0, i))],
        out_specs=[pl.BlockSpec((tl, hk), lambda i: (i, 0))] * 3,
        out_shape=[jax.ShapeDtypeStruct((t, hk), I32), jax.ShapeDtypeStruct((t, hk), I32),
                   jax.ShapeDtypeStruct((t, hk), F32)],
        scratch_shapes=[pltpu.VMEM((2, PEER_TOPK, tl), F32), pltpu.VMEM((2, PEER_TOPK, tl), I32),
                        pltpu.VMEM((2, PEER_TOPK, tl), F32), pltpu.VMEM((2, PEER_TOPK, tl), I32),
                        pltpu.VMEM((2, PEER_TOPK, tl), F32), pltpu.VMEM((2, PEER_TOPK, tl), I32),
                        pltpu.VMEM((hk, tl), I32), pltpu.VMEM((hk, tl), F32)],
        compiler_params=pltpu.CompilerParams(dimension_semantics=("arbitrary",),
                                             vmem_limit_bytes=VMEM_LIMIT),
        name="peer_topk",
    )(st)


def _erf(x):
    return lax.erf(x)


SUB = 8


def _peer_gather_kernel(idx_ref, idxn_ref, h_ref, g_ref, x1_ref, g2_ref, fg_ref, uv_hbm,
                        o_ref, buf, sem, *, tq, final):
    i = pl.program_id(0)
    n = pl.num_programs(0)
    hk = PEER_HEADS * PEER_TOPK
    groups = tq * hk // SUB
    nj = h_ref.shape[1]

    def issue(iref, slot):
        def one(ro, carry):
            for ri in range(SUB):
                pltpu.make_async_copy(uv_hbm.at[iref[ro * SUB + ri]], buf.at[slot, ro, :, ri, :],
                                      sem.at[slot]).start(priority=ri % 2)
            return carry
        lax.fori_loop(0, groups, one, 0)

    @pl.when(i == 0)
    def _():
        issue(idx_ref, 0)

    slot = i % 2

    @pl.when(i + 1 < n)
    def _():
        issue(idxn_ref, 1 - slot)

    pltpu.make_async_copy(buf.at[slot], buf.at[slot], sem.at[slot]).wait()

    gt = g_ref[...].T
    gpt = hk // SUB
    for tt in range(tq):
        ue = buf[slot, tt * gpt:(tt + 1) * gpt, 0:nj]
        ve = buf[slot, tt * gpt:(tt + 1) * gpt, nj:2 * nj]
        hb = jnp.broadcast_to(h_ref[tt][:, None, :], (nj, SUB, LANES))
        a = jnp.sum(jnp.sum(ue * hb[None], axis=1), axis=-1, keepdims=True)
        act = 0.5 * a * (1.0 + _erf(a * (2.0 ** -0.5)))
        w = gt[:, tt:tt + 1].reshape(gpt, SUB, 1) * act
        peer = jnp.sum(jnp.sum(w[:, None] * ve, axis=0), axis=1)
        y = x1_ref[tt] + g2_ref[0] * peer
        if final:
            ms = jnp.mean(jnp.mean(y * y, axis=-1, keepdims=True), axis=0, keepdims=True)
            y = y * lax.rsqrt(ms + EPS) * fg_ref[...]
        o_ref[tt] = y


def _peer_gather(eidx_flat, h2, g, x1, g2, final_g, uv, *, seq, final, tq=8):
    t, nj, _ = h2.shape
    hk = PEER_HEADS * PEER_TOPK
    rows = tq * hk
    n = t // tq
    return pl.pallas_call(
        functools.partial(_peer_gather_kernel, tq=tq, final=final),
        grid=(n,),
        in_specs=[
            pl.BlockSpec((rows,), lambda i: (i,), memory_space=pltpu.SMEM),
            pl.BlockSpec((rows,), lambda i: (jnp.minimum(i + 1, n - 1),), memory_space=pltpu.SMEM),
            pl.BlockSpec((tq, nj, LANES), lambda i: (i, 0, 0)),
            pl.BlockSpec((tq, hk), lambda i: (i, 0)),
            pl.BlockSpec((tq, nj, LANES), lambda i: (i, 0, 0)),
            pl.BlockSpec((1, nj, LANES), lambda i: ((i * tq) // seq, 0, 0)),
            pl.BlockSpec((nj, LANES), lambda i: (0, 0)),
            pl.BlockSpec(memory_space=pl.ANY),
        ],
        out_specs=pl.BlockSpec((tq, nj, LANES), lambda i: (i, 0, 0)),
        out_shape=jax.ShapeDtypeStruct((t, nj, LANES), F32),
        scratch_shapes=[pltpu.VMEM((2, rows // SUB, 2 * nj, SUB, LANES), F32), pltpu.SemaphoreType.DMA((2,))],
        compiler_params=pltpu.CompilerParams(dimension_semantics=("arbitrary",),
                                             vmem_limit_bytes=VMEM_LIMIT),
        name="peer_gather",
    )(eidx_flat, eidx_flat, h2, g, x1, g2, final_g, uv)


def _pack_expert_table(w):
    n, d = w.shape
    b = lax.bitcast_convert_type(w.astype(BF16), jnp.uint16).astype(jnp.uint32).reshape(n // 2, 2, d // LANES, LANES)
    return (b[:, 0] | (b[:, 1] << 16)).reshape(n // 2 * (d // LANES), LANES)


def _load_table_once(tab_hbm, tab, sem):
    @pl.when(pl.program_id(0) == 0)
    def _():
        cp = pltpu.make_async_copy(tab_hbm, tab, sem)
        cp.start()
        cp.wait()


def _expert_tile(tab, off, sh):
    x = tab[pl.ds(pl.multiple_of(off, SUB), SUB), :]
    return pltpu.bitcast((x << sh.astype(jnp.uint32)) & jnp.uint32(0xFFFF0000), F32)


def _group_matrix(rows, cols):
    r = lax.broadcasted_iota(I32, (rows, cols), 0)
    c = lax.broadcasted_iota(I32, (rows, cols), 1)
    return (c // (cols // rows) == r).astype(BF16)


def _split2(x):
    hi = x.astype(BF16)
    return hi, (x - hi.astype(F32)).astype(BF16)


def _peer_u_kernel(off_ref, sh_ref, h_ref, g_ref, tab_hbm, w_ref, tab, sem, a_sc):
    tq, hk = g_ref.shape
    _load_table_once(tab_hbm, tab, sem)
    gmat = _group_matrix(hk, hk * SUB)
    ones = jnp.ones((SUB, LANES), BF16)

    def token(tt, carry):
        hb = h_ref[tt]
        prods = [_expert_tile(tab, off_ref[tt * hk + k], sh_ref[tt * hk + k]) * hb for k in range(hk)]
        pst = jnp.concatenate(prods, axis=0).astype(BF16)
        part = jnp.dot(gmat, pst, preferred_element_type=F32)
        p_hi, p_lo = _split2(part)
        nt = (((1,), (1,)), ((), ()))
        a = (lax.dot_general(ones, p_hi, nt, preferred_element_type=F32)
             + lax.dot_general(ones, p_lo, nt, preferred_element_type=F32))
        a_sc[pl.ds(tt, 1), :] = a[0:1]
        return carry

    lax.fori_loop(0, tq, token, 0, unroll=2)
    a = a_sc[...]
    w_ref[...] = g_ref[...] * (0.5 * a * (1.0 + _erf(a * (2.0 ** -0.5))))


def _peer_u(off_flat, sh_flat, h2, g, utab, *, tq=8):
    t, nj, _ = h2.shape
    hk = g.shape[1]
    return pl.pallas_call(
        _peer_u_kernel,
        grid=(t // tq,),
        in_specs=[
            pl.BlockSpec((tq * hk,), lambda i: (i,), memory_space=pltpu.SMEM),
            pl.BlockSpec((tq * hk,), lambda i: (i,), memory_space=pltpu.SMEM),
            pl.BlockSpec((tq, nj, LANES), lambda i: (i, 0, 0)),
            pl.BlockSpec((tq, hk), lambda i: (i, 0)),
            pl.BlockSpec(memory_space=pl.ANY),
        ],
        out_specs=pl.BlockSpec((tq, hk), lambda i: (i, 0)),
        out_shape=jax.ShapeDtypeStruct((t, hk), F32),
        scratch_shapes=[pltpu.VMEM(utab.shape, jnp.uint32), pltpu.SemaphoreType.DMA(()),
                        pltpu.VMEM((tq, hk), F32)],
        compiler_params=pltpu.CompilerParams(dimension_semantics=("arbitrary",),
                                             vmem_limit_bytes=VMEM_LIMIT),
        name="peer_u",
    )(off_flat, sh_flat, h2, g, utab)


def _peer_v_kernel(off_ref, sh_ref, w_ref, x1_ref, g2_ref, fg_ref, tab_hbm, o_ref, tab, sem, wl_sc, *, final):
    tq, hk = w_ref.shape
    _load_table_once(tab_hbm, tab, sem)
    rep = _group_matrix(hk, hk * SUB)
    w_hi, w_lo = _split2(w_ref[...])
    wl_sc[0] = jnp.dot(w_hi, rep, preferred_element_type=F32)
    wl_sc[1] = jnp.dot(w_lo, rep, preferred_element_type=F32)
    srow = lax.broadcasted_iota(I32, (SUB, hk * SUB), 0)
    scol = lax.broadcasted_iota(I32, (SUB, hk * SUB), 1)
    mine = (scol % SUB) == srow

    def token(tt, carry):
        vst = jnp.concatenate([_expert_tile(tab, off_ref[tt * hk + k], sh_ref[tt * hk + k]) for k in range(hk)],
                              axis=0).astype(BF16)
        lhs = jnp.concatenate([jnp.where(mine, wl_sc[0, pl.ds(tt, 1), :], 0.0),
                               jnp.where(mine, wl_sc[1, pl.ds(tt, 1), :], 0.0)], axis=0).astype(BF16)
        r = jnp.dot(lhs, vst, preferred_element_type=F32)
        y = x1_ref[tt] + g2_ref[0] * (r[0:SUB] + r[SUB:2 * SUB])
        if final:
            ms = jnp.mean(jnp.mean(y * y, axis=-1, keepdims=True), axis=0, keepdims=True)
            y = y * lax.rsqrt(ms + EPS) * fg_ref[...]
        o_ref[tt] = y
        return carry

    lax.fori_loop(0, tq, token, 0, unroll=2)


def _peer_v(off_flat, sh_flat, w, x1, g2, final_g, vtab, *, seq, final, tq=8):
    t, nj, _ = x1.shape
    hk = w.shape[1]
    return pl.pallas_call(
        functools.partial(_peer_v_kernel, final=final),
        grid=(t // tq,),
        in_specs=[
            pl.BlockSpec((tq * hk,), lambda i: (i,), memory_space=pltpu.SMEM),
            pl.BlockSpec((tq * hk,), lambda i: (i,), memory_space=pltpu.SMEM),
            pl.BlockSpec((tq, hk), lambda i: (i, 0)),
            pl.BlockSpec((tq, nj, LANES), lambda i: (i, 0, 0)),
            pl.BlockSpec((1, nj, LANES), lambda i: ((i * tq) // seq, 0, 0)),
            pl.BlockSpec((nj, LANES), lambda i: (0, 0)),
            pl.BlockSpec(memory_space=pl.ANY),
        ],
        out_specs=pl.BlockSpec((tq, nj, LANES), lambda i: (i, 0, 0)),
        out_shape=jax.ShapeDtypeStruct((t, nj, LANES), F32),
        scratch_shapes=[pltpu.VMEM(vtab.shape, jnp.uint32), pltpu.SemaphoreType.DMA(()),
                        pltpu.VMEM((2, tq, hk * SUB), F32)],
        compiler_params=pltpu.CompilerParams(dimension_semantics=("arbitrary",),
                                             vmem_limit_bytes=VMEM_LIMIT),
        name="peer_v",
    )(off_flat, sh_flat, w, x1, g2, final_g, vtab)


def kernel(x, c, norm1_g, norm2_g, w_ada, b_ada, w_in, w_out, lam_q1, lam_k1, lam_q2, lam_k2, subln_g, sinks,
           peer_wq, peer_keys, peer_u, peer_v, final_g):
    bsz, seq, d = x.shape
    depth = w_in.shape[0]
    t = bsz * seq
    slopes = _alibi_slopes()
    sl_c = [float(s) for s in slopes[:C_HEADS]]
    qa_bias = _slope_bias_col(slopes[C_HEADS:C_HEADS + A_HEADS] * np.float32(LOG2E))
    qb_bias = _slope_bias_col(slopes[C_HEADS + A_HEADS:] * np.float32(LOG2E))
    mods = _adaln_mods(c, w_ada, b_ada)
    x2d = x.reshape(t, d)
    av, bw = A_HEADS * HEAD_DIM, B_HEADS * HEAD_DIM
    for l in range(depth):
        lam_init = 0.8 - 0.6 * math.exp(-0.3 * l)
        mod3 = mods[l].reshape(bsz, 1, N_MOD * d)
        wn, wt = _prep_in_weights(w_in[l])
        ka, kb, qc, vc2, qta, vta, qtb, vtb, ktc = _inproj(
            x2d, mod3, norm1_g[l].reshape(1, d), wn, wt, qa_bias, qb_bias, seq=seq)
        lamv = jnp.stack([lam_q1[l], lam_k1[l], lam_q2[l], lam_k2[l]]).astype(F32)
        sg = subln_g[l].reshape(HEAD_DIM, 1).astype(F32)
        oa = _diff_attention(qta, ka, vta, lamv, sg, bsz=bsz, seq=seq, lam_init=lam_init)
        ob = _moba_attention(qtb, kb, vtb, bsz=bsz, seq=seq)
        oc = _swa_attention(sinks[l].astype(F32), qc, ktc, vc2, bsz=bsz, seq=seq, slopes=sl_c)
        wo = w_out[l].astype(BF16)
        keys = peer_keys[l].reshape(2 * PEER_HEADS, PEER_NKEYS, -1).astype(BF16)
        x1, h2, st = _outproj(oa, ob, oc, wo[:av], wo[av:av + bw], wo[av + bw:], x2d, mod3,
                              norm2_g[l].reshape(1, d), peer_wq[l].astype(BF16), keys, seq=seq)
        toff, tsh, g = _peer_topk(st)
        nj = d // LANES
        g2 = mods[l][:, 5 * d:6 * d].reshape(bsz, nj, LANES)
        toff, tsh = toff.reshape(-1), tsh.reshape(-1)
        w = _peer_u(toff, tsh, h2.reshape(t, nj, LANES), g, _pack_expert_table(peer_u[l]))
        x2d = _peer_v(toff, tsh, w, x1.reshape(t, nj, LANES), g2, final_g.reshape(nj, LANES),
                      _pack_expert_table(peer_v[l]), seq=seq, final=(l == depth - 1)).reshape(t, d)
    return x2d.reshape(bsz, seq, d)
```

```python
import functools
import math

import numpy as np
import jax
import jax.numpy as jnp
from jax import lax
from jax.experimental import pallas as pl
from jax.experimental.pallas import tpu as pltpu

F32 = jnp.float32
BF16 = jnp.bfloat16
I32 = jnp.int32

D_MODEL = 1024
HEAD_DIM = 64
N_HEADS_TOTAL = 16
A_HEADS = 4
B_HEADS = 6
C_HEADS = 6
C_KV_HEADS = 2
C_GROUP = 3
A_QK_DIM = 32
MOBA_BLOCK = 256
MOBA_TOPK = 3
WINDOW = 128
ALIBI_MAX = 8.0
PEER_HEADS = 8
PEER_NKEYS = 128
PEER_TOPK = 16
N_MOD = 6
EPS = 1e-6

LANES = 128
KT = 256
AUG0 = HEAD_DIM
SEL0 = AUG0 + 6
MAX_BLOCKS = LANES - SEL0
NEG_BIG = -1e30
LOG2E = math.log2(math.e)
VMEM_LIMIT = 56 * 1024 * 1024


def _alibi_slopes():
    n = N_HEADS_TOTAL
    return (2.0 ** (-ALIBI_MAX * np.arange(1, n + 1, dtype=np.float32) / n)).astype(np.float32)


def _split3(v):
    v = np.float32(v)
    hi = np.float32(np.asarray(v).astype(jnp.bfloat16).astype(np.float32))
    r = np.float32(v - hi)
    mid = np.float32(np.asarray(r).astype(jnp.bfloat16).astype(np.float32))
    lo = np.float32(np.float32(r - mid))
    lo = np.float32(np.asarray(lo).astype(jnp.bfloat16).astype(np.float32))
    return hi, mid, lo


def _slope_bias_col(slopes):
    col = np.zeros((LANES * len(slopes), 1), np.float32)
    for h, s in enumerate(slopes):
        hi, mid, lo = _split3(s)
        col[h * LANES + AUG0:h * LANES + AUG0 + 6, 0] = [hi, mid, lo, hi, mid, lo]
    return jnp.asarray(col)


def _mod_kernel(c_ref, w_ref, b_ref, o_ref):
    c = c_ref[...]
    cs = c * (1.0 / (1.0 + jnp.exp(-c)))
    o_ref[0] = jnp.dot(cs, w_ref[0], preferred_element_type=F32) + b_ref[0]


def _adaln_mods(c, w_ada, b_ada):
    depth, d, n = w_ada.shape
    bsz = c.shape[0]
    rows = -(-bsz // 8) * 8
    cp = jnp.pad(c, ((0, rows - bsz), (0, 0)))
    tn = 1536
    out = pl.pallas_call(
        _mod_kernel,
        grid=(depth, n // tn),
        in_specs=[
            pl.BlockSpec((rows, d), lambda l, j: (0, 0)),
            pl.BlockSpec((1, d, tn), lambda l, j: (l, 0, j)),
            pl.BlockSpec((1, 1, tn), lambda l, j: (l, 0, j)),
        ],
        out_specs=pl.BlockSpec((1, rows, tn), lambda l, j: (l, 0, j)),
        out_shape=jax.ShapeDtypeStruct((depth, rows, n), F32),
        compiler_params=pltpu.CompilerParams(dimension_semantics=("arbitrary", "arbitrary"),
                                             vmem_limit_bytes=VMEM_LIMIT),
        name="adaln_mods",
    )(cp, w_ada, b_ada.reshape(depth, 1, n))
    return out[:, :bsz]


NN_WIDTHS = (A_HEADS * LANES, B_HEADS * LANES, C_HEADS * HEAD_DIM, 2 * C_KV_HEADS * HEAD_DIM)
NT_ROWS = (A_HEADS * LANES, A_HEADS * HEAD_DIM, B_HEADS * LANES, B_HEADS * HEAD_DIM,
           C_KV_HEADS * LANES)


def _prep_in_weights(w):
    d = w.shape[0]
    aq, ak, av = A_HEADS * 2 * A_QK_DIM, A_HEADS * 2 * A_QK_DIM, A_HEADS * HEAD_DIM
    bw = B_HEADS * HEAD_DIM
    cq, ckv = C_HEADS * HEAD_DIM, C_KV_HEADS * HEAD_DIM
    cuts = np.cumsum([aq, ak, av, bw, bw, bw, cq, ckv]).tolist()
    qa, ka, va, qb, kb, vb, qc, kc, vc = jnp.split(w, cuts, axis=-1)

    def pad_heads(m, nh, scale):
        m = (m * scale).reshape(d, nh, HEAD_DIM)
        return jnp.pad(m, ((0, 0), (0, 0), (0, LANES - HEAD_DIM))).reshape(d, nh * LANES)

    vc2 = vc.reshape(d, C_KV_HEADS, 1, HEAD_DIM)
    vc2 = jnp.broadcast_to(vc2, (d, C_KV_HEADS, 2, HEAD_DIM)).reshape(d, 2 * ckv)
    kc2 = jnp.broadcast_to(kc.reshape(d, C_KV_HEADS, 1, HEAD_DIM), (d, C_KV_HEADS, 2, HEAD_DIM)).reshape(d, 2 * ckv)
    wn = jnp.concatenate([pad_heads(ka, A_HEADS, 1.0), pad_heads(kb, B_HEADS, 1.0),
                          qc * (HEAD_DIM ** -0.5), vc2], axis=1)
    wt = jnp.concatenate([pad_heads(qa, A_HEADS, A_QK_DIM ** -0.5 * LOG2E), va,
                          pad_heads(qb, B_HEADS, HEAD_DIM ** -0.5 * LOG2E), vb, kc2], axis=1).T
    return wn.astype(BF16), wt.astype(BF16)


def _rms_mod(x, g, sc, sh):
    ms = jnp.mean(x * x, axis=-1, keepdims=True)
    return (x * lax.rsqrt(ms + EPS) * g) * (1.0 + sc) + sh


def _inproj_kernel(x_ref, mod_ref, g_ref, wn_ref, wt_ref, qab_ref, qbb_ref,
                   ka_ref, kb_ref, qc_ref, vc_ref, qta_ref, vta_ref, qtb_ref, vtb_ref, ktc_ref,
                   *, tm, seq):
    d = x_ref.shape[1]
    x = x_ref[...]
    sh = mod_ref[0, :, 0:d]
    sc = mod_ref[0, :, d:2 * d]
    h = _rms_mod(x, g_ref[...], sc, sh).astype(BF16)
    pn = jnp.dot(h, wn_ref[...], preferred_element_type=F32)
    pt = lax.dot_general(wt_ref[...], h, (((1,), (1,)), ((), ())),
                         preferred_element_type=F32)

    pos = (pl.program_id(0) * tm) % seq + lax.broadcasted_iota(I32, (tm, LANES), 0)
    col = lax.broadcasted_iota(I32, (tm, LANES), 1)
    blk_id = pos // KT
    p_hi = (blk_id * KT).astype(F32)
    p_lo = (pos - blk_id * KT).astype(F32)
    aug_a = jnp.where((col >= AUG0) & (col < AUG0 + 3), p_hi,
                      jnp.where((col >= AUG0 + 3) & (col < AUG0 + 6), p_lo, 0.0))
    aug_b = jnp.where((col >= SEL0) & (col - SEL0 == blk_id), 1.0, aug_a)
    for hh in range(A_HEADS):
        ka_ref[:, hh * LANES:(hh + 1) * LANES] = (pn[:, hh * LANES:(hh + 1) * LANES] + aug_a).astype(BF16)
    o = NN_WIDTHS[0]
    for hh in range(B_HEADS):
        kb_ref[:, hh * LANES:(hh + 1) * LANES] = (pn[:, o + hh * LANES:o + (hh + 1) * LANES] + aug_b).astype(BF16)
    o += NN_WIDTHS[1]
    qc_ref[...] = pn[:, o:o + NN_WIDTHS[2]].astype(BF16)
    o += NN_WIDTHS[2]
    vc_ref[...] = pn[:, o:o + NN_WIDTHS[3]].astype(BF16)

    r0 = 0
    for ref, b, nr in zip((qta_ref, vta_ref, qtb_ref, vtb_ref, ktc_ref),
                          (qab_ref, None, qbb_ref, None, None), NT_ROWS):
        blk = pt[r0:r0 + nr, :]
        if b is not None:
            blk = blk + b[...]
        blk = blk.astype(BF16)
        for cc in range(tm // KT):
            ref[cc] = blk[:, cc * KT:(cc + 1) * KT]
        r0 += nr


def _inproj(x2d, mod3, g, wn, wt, qa_bias, qb_bias, *, seq, tm=512):
    t, d = x2d.shape
    assert seq % tm == 0 and tm % KT == 0
    nt = t // tm
    nn_total = sum(NN_WIDTHS)
    row_specs = [pl.BlockSpec((tm, wd), lambda i: (i, 0)) for wd in NN_WIDTHS]
    kt_specs = [pl.BlockSpec((tm // KT, r, KT), lambda i: (i, 0, 0)) for r in NT_ROWS]
    out_shape = ([jax.ShapeDtypeStruct((t, wd), BF16) for wd in NN_WIDTHS]
                 + [jax.ShapeDtypeStruct((t // KT, r, KT), BF16) for r in NT_ROWS])
    return pl.pallas_call(
        functools.partial(_inproj_kernel, tm=tm, seq=seq),
        grid=(nt,),
        in_specs=[
            pl.BlockSpec((tm, d), lambda i: (i, 0)),
            pl.BlockSpec((1, 1, mod3.shape[2]), lambda i: ((i * tm) // seq, 0, 0)),
            pl.BlockSpec((1, d), lambda i: (0, 0)),
            pl.BlockSpec((d, nn_total), lambda i: (0, 0)),
            pl.BlockSpec((sum(NT_ROWS), d), lambda i: (0, 0)),
            pl.BlockSpec((NT_ROWS[0], 1), lambda i: (0, 0)),
            pl.BlockSpec((NT_ROWS[2], 1), lambda i: (0, 0)),
        ],
        out_specs=row_specs + kt_specs,
        out_shape=out_shape,
        compiler_params=pltpu.CompilerParams(dimension_semantics=("arbitrary",),
                                             vmem_limit_bytes=VMEM_LIMIT),
        name="inproj",
    )(x2d, mod3, g, wn, wt, qa_bias, qb_bias)


def _flash_step(k, qt, vt, m_ref, l_ref, acc_ref, mask=None):
    s = jnp.dot(k, qt, preferred_element_type=F32)
    if mask is not None:
        s = jnp.where(mask, s, -jnp.inf)
    m_prev = m_ref[...]
    m_new = jnp.maximum(m_prev, jnp.max(s, axis=0, keepdims=True))
    alpha = jnp.exp2(m_prev - m_new)
    p = jnp.exp2(s - m_new)
    l_ref[...] = alpha * l_ref[...] + jnp.sum(p, axis=0, keepdims=True)
    acc_ref[...] = alpha * acc_ref[...] + jnp.dot(vt, p.astype(BF16), preferred_element_type=F32)
    m_ref[...] = m_new


def _flash_init(m_ref, l_ref, acc_ref):
    m_ref[...] = jnp.full(m_ref.shape, -jnp.inf, F32)
    l_ref[...] = jnp.zeros(l_ref.shape, F32)
    acc_ref[...] = jnp.zeros(acc_ref.shape, F32)


PAST_TILES = 4


def _flash_causal_pair(k_ref, vt_ref, q_diag, q_past, qi, causal, m_ref, l_ref, acc_ref):
    _flash_init(m_ref, l_ref, acc_ref)

    def step(kj, n, qs, mask):
        kk = k_ref[pl.ds(pl.multiple_of(kj * KT, KT), n * KT), :]
        for hh in range(2):
            rows = slice(hh * HEAD_DIM, (hh + 1) * HEAD_DIM)
            vt = vt_ref[kj, rows, :] if n == 1 else jnp.concatenate(
                [vt_ref[kj + c, rows, :] for c in range(n)], axis=1)
            _flash_step(kk[:, hh * LANES:(hh + 1) * LANES], qs[hh], vt,
                        m_ref.at[hh], l_ref.at[hh], acc_ref.at[hh], mask)

    step(qi, 1, q_diag, causal)

    def body(j, carry):
        step(j * PAST_TILES, PAST_TILES, q_past, None)
        return carry

    lax.fori_loop(0, qi // PAST_TILES, body, 0)
    for r in range(PAST_TILES - 1):
        @pl.when(qi % PAST_TILES > r)
        def _():
            step(qi - 1 - r, 1, q_past, None)


def _diff_kernel(qt_ref, k_ref, vt_ref, lamv_ref, sg_ref, o_ref, m_ref, l_ref, acc_ref, *, lam_init):
    tq = qt_ref.shape[2]
    qi = pl.program_id(2)
    lv = lamv_ref[...]
    lam = (jnp.exp(jnp.sum(lv[0:1] * lv[1:2], axis=-1, keepdims=True))
           - jnp.exp(jnp.sum(lv[2:3] * lv[3:4], axis=-1, keepdims=True)) + lam_init)
    row = lax.broadcasted_iota(I32, (LANES, tq), 0)
    kr = lax.broadcasted_iota(I32, (KT, 2 * tq), 0)
    qc = lax.broadcasted_iota(I32, (KT, 2 * tq), 1)
    causal = kr <= jnp.where(qc >= tq, qc - tq, qc)
    qs = []
    for hh in range(2):
        qt = qt_ref[0, hh * LANES:(hh + 1) * LANES, :]
        zero = jnp.zeros_like(qt)
        q1 = jnp.where((row < A_QK_DIM) | (row >= AUG0), qt, zero)
        q2 = jnp.where(row >= A_QK_DIM, qt, zero)
        qs.append(jnp.concatenate([q1, q2], axis=1))
    _flash_causal_pair(k_ref, vt_ref, qs, qs, qi, causal, m_ref, l_ref, acc_ref)
    res = []
    for hh in range(2):
        o = acc_ref[hh] / l_ref[hh]
        od = o[:, :tq] - lam * o[:, tq:]
        ms = jnp.mean(od * od, axis=0, keepdims=True)
        res.append(od * lax.rsqrt(ms + EPS) * sg_ref[...] * (1.0 - lam_init))
    o_ref[...] = jnp.concatenate(res, axis=0).T.astype(BF16)


def _diff_attention(qta, ka, vta, lamv, sg, *, bsz, seq, lam_init):
    tq = KT
    nq = seq // tq
    t = bsz * seq
    return pl.pallas_call(
        functools.partial(_diff_kernel, lam_init=lam_init),
        grid=(bsz, A_HEADS // 2, nq),
        in_specs=[
            pl.BlockSpec((1, 2 * LANES, tq), lambda b, hp, qi: (b * nq + qi, hp, 0)),
            pl.BlockSpec((seq, 2 * LANES), lambda b, hp, qi: (b, hp)),
            pl.BlockSpec((nq, 2 * HEAD_DIM, KT), lambda b, hp, qi: (b, hp, 0)),
            pl.BlockSpec((4, A_QK_DIM), lambda b, hp, qi: (0, 0)),
            pl.BlockSpec((HEAD_DIM, 1), lambda b, hp, qi: (0, 0)),
        ],
        out_specs=pl.BlockSpec((tq, LANES), lambda b, hp, qi: (b * nq + qi, hp)),
        out_shape=jax.ShapeDtypeStruct((t, A_HEADS * HEAD_DIM), BF16),
        scratch_shapes=[pltpu.VMEM((2, 1, 2 * tq), F32), pltpu.VMEM((2, 1, 2 * tq), F32),
                        pltpu.VMEM((2, HEAD_DIM, 2 * tq), F32)],
        compiler_params=pltpu.CompilerParams(dimension_semantics=("arbitrary",) * 3,
                                             vmem_limit_bytes=VMEM_LIMIT),
        name="diff_attn",
    )(qta, ka, vta, lamv, sg)


def _moba_kernel(qt_ref, k_ref, vt_ref, o_ref, m_ref, l_ref, acc_ref, km_ref):
    tq = qt_ref.shape[2]
    nb = vt_ref.shape[0]
    qi = pl.program_id(2)

    @pl.when(qi == 0)
    def _():
        lane1 = lax.broadcasted_iota(I32, (1, LANES), 1)
        for hh in range(2):
            km_ref[hh] = jnp.zeros((LANES, LANES), F32)

            def put_block(j, carry):
                blk = k_ref[pl.ds(pl.multiple_of(j * KT, KT), KT), hh * LANES:(hh + 1) * LANES].astype(F32)
                mean = jnp.sum(blk, axis=0, keepdims=True) * (1.0 / KT)
                km_ref[hh, pl.ds(SEL0 + j, 1), :] = jnp.where(lane1 < HEAD_DIM, mean, 0.0)
                return carry
            lax.fori_loop(0, nb, put_block, 0)

    row = lax.broadcasted_iota(I32, (LANES, tq), 0)
    kr = lax.broadcasted_iota(I32, (KT, tq), 0)
    qc = lax.broadcasted_iota(I32, (KT, tq), 1)
    causal = kr <= qc
    in_sel = (row >= SEL0) & (row < SEL0 + MAX_BLOCKS)
    q_diag, q_past = [], []
    for hh in range(2):
        qt = qt_ref[0, hh * LANES:(hh + 1) * LANES, :]
        km = km_ref[hh]
        km_hi = km.astype(BF16)
        km_lo = (km - km_hi.astype(F32)).astype(BF16)
        gate = (jnp.dot(km_hi, qt, preferred_element_type=F32)
                + jnp.dot(km_lo, qt, preferred_element_type=F32))
        cur = jnp.where((row >= SEL0) & (row < SEL0 + qi), gate, -jnp.inf)
        sel = jnp.zeros((LANES, tq), jnp.bool_)
        for _ in range(MOBA_TOPK):
            mx = jnp.max(cur, axis=0, keepdims=True)
            first = jnp.min(jnp.where(cur == mx, row, 4 * LANES), axis=0, keepdims=True)
            pick = (row == first) & (mx > -jnp.inf)
            sel = sel | pick
            cur = jnp.where(pick, -jnp.inf, cur)
        q_diag.append(jnp.where(in_sel, jnp.zeros_like(qt), qt))
        q_past.append(jnp.where(in_sel, jnp.where(sel, 0.0, NEG_BIG).astype(BF16), qt))
    _flash_causal_pair(k_ref, vt_ref, q_diag, q_past, qi, causal, m_ref, l_ref, acc_ref)
    res = [acc_ref[hh] / l_ref[hh] for hh in range(2)]
    o_ref[...] = jnp.concatenate(res, axis=0).T.astype(BF16)


def _moba_attention(qtb, kb, vtb, *, bsz, seq):
    tq = KT
    nq = seq // tq
    assert nq <= MAX_BLOCKS
    t = bsz * seq
    return pl.pallas_call(
        _moba_kernel,
        grid=(bsz, B_HEADS // 2, nq),
        in_specs=[
            pl.BlockSpec((1, 2 * LANES, tq), lambda b, hp, qi: (b * nq + qi, hp, 0)),
            pl.BlockSpec((seq, 2 * LANES), lambda b, hp, qi: (b, hp)),
            pl.BlockSpec((nq, 2 * HEAD_DIM, KT), lambda b, hp, qi: (b, hp, 0)),
        ],
        out_specs=pl.BlockSpec((tq, LANES), lambda b, hp, qi: (b * nq + qi, hp)),
        out_shape=jax.ShapeDtypeStruct((t, B_HEADS * HEAD_DIM), BF16),
        scratch_shapes=[pltpu.VMEM((2, 1, tq), F32), pltpu.VMEM((2, 1, tq), F32),
                        pltpu.VMEM((2, HEAD_DIM, tq), F32), pltpu.VMEM((2, LANES, LANES), F32)],
        compiler_params=pltpu.CompilerParams(dimension_semantics=("arbitrary",) * 3,
                                             vmem_limit_bytes=VMEM_LIMIT),
        name="moba_attn",
    )(qtb, kb, vtb)


def _swa_kernel(sink_ref, q_ref, ktp_ref, ktc_ref, vp_ref, vc_ref, o_ref, *, slopes):
    tq = q_ref.shape[0]
    n = pl.program_id(1)
    lane = lax.broadcasted_iota(I32, (tq, LANES), 1)
    r2 = lax.broadcasted_iota(I32, (tq, 2 * KT), 0)
    c2 = lax.broadcasted_iota(I32, (tq, 2 * KT), 1)
    rel = r2 + KT - c2
    mask = (rel >= 0) & (rel < WINDOW) & ((c2 >= KT) | (n > 0))
    relf = rel.astype(F32)
    res = []
    for hq in range(C_HEADS):
        kv = hq // C_GROUP
        qp = q_ref[:, (hq // 2) * LANES:(hq // 2 + 1) * LANES]
        qm = jnp.where((lane < HEAD_DIM) == (hq % 2 == 0), qp, jnp.zeros_like(qp))
        kt = jnp.concatenate([ktp_ref[0, kv * LANES:(kv + 1) * LANES, :],
                              ktc_ref[0, kv * LANES:(kv + 1) * LANES, :]], axis=1)
        vv = jnp.concatenate([vp_ref[:, kv * LANES:(kv + 1) * LANES],
                              vc_ref[:, kv * LANES:(kv + 1) * LANES]], axis=0)
        s = jnp.dot(qm, kt, preferred_element_type=F32)
        s = jnp.where(mask, s - slopes[hq] * relf, -jnp.inf)
        sink = sink_ref[hq]
        m = jnp.maximum(jnp.max(s, axis=-1, keepdims=True), sink)
        e = jnp.exp(s - m)
        den = jnp.sum(e, axis=-1, keepdims=True) + jnp.exp(sink - m)
        p = (e / den).astype(BF16)
        res.append(jnp.dot(p, vv, preferred_element_type=F32))
    for pr in range(C_HEADS // 2):
        o_ref[:, pr * LANES:(pr + 1) * LANES] = jnp.where(lane < HEAD_DIM, res[2 * pr], res[2 * pr + 1]).astype(BF16)


def _swa_attention(sinks, qc, ktc, vc2, *, bsz, seq, slopes):
    tq = KT
    nq = seq // tq
    t = bsz * seq
    return pl.pallas_call(
        functools.partial(_swa_kernel, slopes=slopes),
        grid=(bsz, nq),
        in_specs=[
            pl.BlockSpec(memory_space=pltpu.SMEM),
            pl.BlockSpec((tq, C_HEADS * HEAD_DIM), lambda b, n: (b * nq + n, 0)),
            pl.BlockSpec((1, C_KV_HEADS * LANES, KT), lambda b, n: (b * nq + jnp.maximum(n - 1, 0), 0, 0)),
            pl.BlockSpec((1, C_KV_HEADS * LANES, KT), lambda b, n: (b * nq + n, 0, 0)),
            pl.BlockSpec((tq, C_KV_HEADS * LANES), lambda b, n: (b * nq + jnp.maximum(n - 1, 0), 0)),
            pl.BlockSpec((tq, C_KV_HEADS * LANES), lambda b, n: (b * nq + n, 0)),
        ],
        out_specs=pl.BlockSpec((tq, C_HEADS * HEAD_DIM), lambda b, n: (b * nq + n, 0)),
        out_shape=jax.ShapeDtypeStruct((t, C_HEADS * HEAD_DIM), BF16),
        compiler_params=pltpu.CompilerParams(dimension_semantics=("arbitrary",) * 2,
                                             vmem_limit_bytes=VMEM_LIMIT),
        name="swa_attn",
    )(sinks, qc, ktc, ktc, vc2, vc2)


def _outproj_kernel(oa_ref, ob_ref, oc_ref, woa_ref, wob_ref, woc_ref, x_ref, mod_ref, g_ref, wq_ref, keys_ref,
                    x1_ref, h2_ref, st_ref):
    d = x_ref.shape[1]
    mix = (jnp.dot(oa_ref[...], woa_ref[...], preferred_element_type=F32)
           + jnp.dot(ob_ref[...], wob_ref[...], preferred_element_type=F32)
           + jnp.dot(oc_ref[...], woc_ref[...], preferred_element_type=F32))
    g1 = mod_ref[0, :, 2 * d:3 * d]
    sh2 = mod_ref[0, :, 3 * d:4 * d]
    sc2 = mod_ref[0, :, 4 * d:5 * d]
    x1 = x_ref[...] + g1 * mix
    x1_ref[...] = x1
    h2 = _rms_mod(x1, g_ref[...], sc2, sh2)
    h2_ref[...] = h2
    pq = jnp.dot(h2.astype(BF16), wq_ref[...], preferred_element_type=F32).astype(BF16)
    for hp in range(2 * PEER_HEADS):
        st_ref[hp] = lax.dot_general(keys_ref[hp], pq[:, hp * LANES:(hp + 1) * LANES],
                                     (((1,), (1,)), ((), ())), preferred_element_type=F32)


def _outproj(oa, ob, oc, woa, wob, woc, x2d, mod3, g, wq, keys, *, seq, tm=256):
    t, d = x2d.shape
    nt = t // tm
    nq = wq.shape[1]
    full = lambda a: pl.BlockSpec(a.shape, lambda i: (0,) * a.ndim)
    return pl.pallas_call(
        _outproj_kernel,
        grid=(nt,),
        in_specs=[
            pl.BlockSpec((tm, oa.shape[1]), lambda i: (i, 0)),
            pl.BlockSpec((tm, ob.shape[1]), lambda i: (i, 0)),
            pl.BlockSpec((tm, oc.shape[1]), lambda i: (i, 0)),
            full(woa), full(wob), full(woc),
            pl.BlockSpec((tm, d), lambda i: (i, 0)),
            pl.BlockSpec((1, 1, mod3.shape[2]), lambda i: ((i * tm) // seq, 0, 0)),
            pl.BlockSpec((1, d), lambda i: (0, 0)),
            full(wq), full(keys),
        ],
        out_specs=[pl.BlockSpec((tm, d), lambda i: (i, 0)),
                   pl.BlockSpec((tm, d), lambda i: (i, 0)),
                   pl.BlockSpec((2 * PEER_HEADS, PEER_NKEYS, tm), lambda i: (0, 0, i))],
        out_shape=[jax.ShapeDtypeStruct((t, d), F32), jax.ShapeDtypeStruct((t, d), F32),
                   jax.ShapeDtypeStruct((2 * PEER_HEADS, PEER_NKEYS, t), F32)],
        compiler_params=pltpu.CompilerParams(dimension_semantics=("arbitrary",),
                                             vmem_limit_bytes=VMEM_LIMIT),
        name="outproj_peerq",
    )(oa, ob, oc, woa, wob, woc, x2d, mod3, g, wq, keys)


_CAND_BLOCKS = ((0, 16),) + tuple((i, 8) for i in range(1, 8))
_CAND_ROWS = 16 + 7 * 8 + 8
_BIG_I = np.int32(2 ** 30)


def _topk_rows(curs, keys, val_refs, key_refs):
    def body(r, curs):
        out = []
        for cur, key, val_ref, key_ref in zip(curs, keys, val_refs, key_refs):
            m = jnp.max(cur, axis=0, keepdims=True)
            kmin = jnp.min(jnp.where(cur == m, key, _BIG_I), axis=0, keepdims=True)
            val_ref[pl.ds(r, 1), :] = m
            key_ref[pl.ds(r, 1), :] = kmin
            out.append(jnp.where(key == kmin, -jnp.inf, cur))
        return tuple(out)
    lax.fori_loop(0, PEER_TOPK, body, tuple(curs))


def _peer_topk_kernel(st_ref, off_ref, par_ref, g_ref, v1_ref, k1_ref, v2_ref, k2_ref, vt_ref, kt_ref, ei_ref, gg_ref):
    tl = st_ref.shape[2]
    row = lax.broadcasted_iota(I32, (PEER_NKEYS, tl), 0)
    jrow8 = lax.broadcasted_iota(I32, (8, tl), 0)
    jrow16 = lax.broadcasted_iota(I32, (16, tl), 0)
    for h0 in range(0, PEER_HEADS, 2):
        cand, ckey = [], []
        for s in range(2):
            h = h0 + s
            _topk_rows([st_ref[2 * h], st_ref[2 * h + 1]], [row, row],
                       [v1_ref.at[s], v2_ref.at[s]], [k1_ref.at[s], k2_ref.at[s]])
            sv1, si1, sv2, si2 = v1_ref[s], k1_ref[s], v2_ref[s], k2_ref[s]
            cands, keys = [], []
            for i, nj in _CAND_BLOCKS:
                jrow = jrow16 if nj == 16 else jrow8
                cands.append(sv1[i:i + 1] + sv2[0:nj])
                keys.append((i * PEER_TOPK + jrow) * (PEER_NKEYS * PEER_NKEYS)
                            + si1[i:i + 1] * PEER_NKEYS + si2[0:nj])
            cands.append(sv1[8:16] + sv2[0:1])
            keys.append((jrow8 + 8) * (PEER_TOPK * PEER_NKEYS * PEER_NKEYS) + si1[8:16] * PEER_NKEYS + si2[0:1])
            cand.append(jnp.concatenate(cands, axis=0))
            ckey.append(jnp.concatenate(keys, axis=0))
        _topk_rows(cand, ckey, [vt_ref.at[0], vt_ref.at[1]], [kt_ref.at[0], kt_ref.at[1]])
        for s in range(2):
            h = h0 + s
            top = vt_ref[s]
            e = jnp.exp(top - top[0:1])
            gg_ref[h * PEER_TOPK:(h + 1) * PEER_TOPK, :] = e / jnp.sum(e, axis=0, keepdims=True)
            ei_ref[h * PEER_TOPK:(h + 1) * PEER_TOPK, :] = kt_ref[s] & (PEER_NKEYS * PEER_NKEYS - 1)
    ei = ei_ref[...].T
    off_ref[...] = (ei >> 1) * SUB
    par_ref[...] = ei & 1
    g_ref[...] = gg_ref[...].T


def _peer_topk(st, *, tl=128):
    t = st.shape[2]
    hk = PEER_HEADS * PEER_TOPK
    return pl.pallas_call(
        _peer_topk_kernel,
        grid=(t // tl,),
        in_specs=[pl.BlockSpec((2 * PEER_HEADS, PEER_NKEYS, tl), lambda i: (0, 0, i))],
        out_specs=[pl.BlockSpec((tl, hk), lambda i: (i, 0))] * 3,
        out_shape=[jax.ShapeDtypeStruct((t, hk), I32), jax.ShapeDtypeStruct((t, hk), I32),
                   jax.ShapeDtypeStruct((t, hk), F32)],
        scratch_shapes=[pltpu.VMEM((2, PEER_TOPK, tl), F32), pltpu.VMEM((2, PEER_TOPK, tl), I32),
                        pltpu.VMEM((2, PEER_TOPK, tl), F32), pltpu.VMEM((2, PEER_TOPK, tl), I32),
                        pltpu.VMEM((2, PEER_TOPK, tl), F32), pltpu.VMEM((2, PEER_TOPK, tl), I32),
                        pltpu.VMEM((hk, tl), I32), pltpu.VMEM((hk, tl), F32)],
        compiler_params=pltpu.CompilerParams(dimension_semantics=("arbitrary",),
                                             vmem_limit_bytes=VMEM_LIMIT),
        name="peer_topk",
    )(st)


def _erf(x):
    return lax.erf(x)


SUB = 8


def _peer_gather_kernel(idx_ref, idxn_ref, h_ref, g_ref, x1_ref, g2_ref, fg_ref, uv_hbm,
                        o_ref, buf, sem, *, tq, final):
    i = pl.program_id(0)
    n = pl.num_programs(0)
    hk = PEER_HEADS * PEER_TOPK
    groups = tq * hk // SUB
    nj = h_ref.shape[1]

    def issue(iref, slot):
        def one(ro, carry):
            for ri in range(SUB):
                pltpu.make_async_copy(uv_hbm.at[iref[ro * SUB + ri]], buf.at[slot, ro, :, ri, :],
                                      sem.at[slot]).start(priority=ri % 2)
            return carry
        lax.fori_loop(0, groups, one, 0)

    @pl.when(i == 0)
    def _():
        issue(idx_ref, 0)

    slot = i % 2

    @pl.when(i + 1 < n)
    def _():
        issue(idxn_ref, 1 - slot)

    pltpu.make_async_copy(buf.at[slot], buf.at[slot], sem.at[slot]).wait()

    gt = g_ref[...].T
    gpt = hk // SUB
    for tt in range(tq):
        ue = buf[slot, tt * gpt:(tt + 1) * gpt, 0:nj]
        ve = buf[slot, tt * gpt:(tt + 1) * gpt, nj:2 * nj]
        hb = jnp.broadcast_to(h_ref[tt][:, None, :], (nj, SUB, LANES))
        a = jnp.sum(jnp.sum(ue * hb[None], axis=1), axis=-1, keepdims=True)
        act = 0.5 * a * (1.0 + _erf(a * (2.0 ** -0.5)))
        w = gt[:, tt:tt + 1].reshape(gpt, SUB, 1) * act
        peer = jnp.sum(jnp.sum(w[:, None] * ve, axis=0), axis=1)
        y = x1_ref[tt] + g2_ref[0] * peer
        if final:
            ms = jnp.mean(jnp.mean(y * y, axis=-1, keepdims=True), axis=0, keepdims=True)
            y = y * lax.rsqrt(ms + EPS) * fg_ref[...]
        o_ref[tt] = y


def _peer_gather(eidx_flat, h2, g, x1, g2, final_g, uv, *, seq, final, tq=8):
    t, nj, _ = h2.shape
    hk = PEER_HEADS * PEER_TOPK
    rows = tq * hk
    n = t // tq
    return pl.pallas_call(
        functools.partial(_peer_gather_kernel, tq=tq, final=final),
        grid=(n,),
        in_specs=[
            pl.BlockSpec((rows,), lambda i: (i,), memory_space=pltpu.SMEM),
            pl.BlockSpec((rows,), lambda i: (jnp.minimum(i + 1, n - 1),), memory_space=pltpu.SMEM),
            pl.BlockSpec((tq, nj, LANES), lambda i: (i, 0, 0)),
            pl.BlockSpec((tq, hk), lambda i: (i, 0)),
            pl.BlockSpec((tq, nj, LANES), lambda i: (i, 0, 0)),
            pl.BlockSpec((1, nj, LANES), lambda i: ((i * tq) // seq, 0, 0)),
            pl.BlockSpec((nj, LANES), lambda i: (0, 0)),
            pl.BlockSpec(memory_space=pl.ANY),
        ],
        out_specs=pl.BlockSpec((tq, nj, LANES), lambda i: (i, 0, 0)),
        out_shape=jax.ShapeDtypeStruct((t, nj, LANES), F32),
        scratch_shapes=[pltpu.VMEM((2, rows // SUB, 2 * nj, SUB, LANES), F32), pltpu.SemaphoreType.DMA((2,))],
        compiler_params=pltpu.CompilerParams(dimension_semantics=("arbitrary",),
                                             vmem_limit_bytes=VMEM_LIMIT),
        name="peer_gather",
    )(eidx_flat, eidx_flat, h2, g, x1, g2, final_g, uv)


def _pack_expert_table(w):
    n, d = w.shape
    b = lax.bitcast_convert_type(w.astype(BF16), jnp.uint16).astype(jnp.uint32).reshape(n // 2, 2, d // LANES, LANES)
    return (b[:, 0] | (b[:, 1] << 16)).reshape(n // 2 * (d // LANES), LANES)


def _load_table_once(tab_hbm, tab, sem):
    @pl.when(pl.program_id(0) == 0)
    def _():
        cp = pltpu.make_async_copy(tab_hbm, tab, sem)
        cp.start()
        cp.wait()


def _pair_tile(tab, off):
    return tab[pl.ds(pl.multiple_of(off, SUB), SUB), :]


def _group_matrix(rows, cols):
    r = lax.broadcasted_iota(I32, (rows, cols), 0)
    c = lax.broadcasted_iota(I32, (rows, cols), 1)
    return (c // (cols // rows) == r).astype(BF16)


def _split2(x):
    hi = x.astype(BF16)
    return hi, (x - hi.astype(F32)).astype(BF16)


HALF_HI = 0xFFFF0000


def _peer_u_kernel(off_ref, sh_ref, h_ref, g_ref, tab_hbm, w_ref, tab, sem, a_sc):
    tq, hk = g_ref.shape
    _load_table_once(tab_hbm, tab, sem)
    gmat = _group_matrix(hk, hk * SUB)
    ones = jnp.ones((SUB, LANES), BF16)
    nt = (((1,), (1,)), ((), ()))
    for tt in range(tq):
        hb = h_ref[tt]
        prods = []
        for k in range(hk):
            x = _pair_tile(tab, off_ref[tt * hk + k]) << sh_ref[tt * hk + k].astype(jnp.uint32)
            prods.append(pltpu.bitcast(x & jnp.uint32(HALF_HI), F32) * hb)
        pst = jnp.concatenate(prods, axis=0).astype(BF16)
        part = jnp.dot(gmat, pst, preferred_element_type=F32)
        p_hi, p_lo = _split2(part)
        a = (lax.dot_general(ones, p_hi, nt, preferred_element_type=F32)
             + lax.dot_general(ones, p_lo, nt, preferred_element_type=F32))
        a_sc[tt:tt + 1, :] = a[0:1]
    a = a_sc[...]
    w_ref[...] = g_ref[...] * (0.5 * a * (1.0 + _erf(a * (2.0 ** -0.5))))


def _peer_u(off_flat, sh_flat, h2, g, utab, *, tq=8):
    t, nj, _ = h2.shape
    hk = g.shape[1]
    return pl.pallas_call(
        _peer_u_kernel,
        grid=(t // tq,),
        in_specs=[
            pl.BlockSpec((tq * hk,), lambda i: (i,), memory_space=pltpu.SMEM),
            pl.BlockSpec((tq * hk,), lambda i: (i,), memory_space=pltpu.SMEM),
            pl.BlockSpec((tq, nj, LANES), lambda i: (i, 0, 0)),
            pl.BlockSpec((tq, hk), lambda i: (i, 0)),
            pl.BlockSpec(memory_space=pl.ANY),
        ],
        out_specs=pl.BlockSpec((tq, hk), lambda i: (i, 0)),
        out_shape=jax.ShapeDtypeStruct((t, hk), F32),
        scratch_shapes=[pltpu.VMEM(utab.shape, jnp.uint32), pltpu.SemaphoreType.DMA(()),
                        pltpu.VMEM((tq, hk), F32)],
        compiler_params=pltpu.CompilerParams(dimension_semantics=("arbitrary",),
                                             vmem_limit_bytes=VMEM_LIMIT),
        name="peer_u",
    )(off_flat, sh_flat, h2, g, utab)


PAIR_ROWS = 2 * SUB


def _peer_v_kernel(off_ref, par_ref, w_ref, x1_ref, g2_ref, fg_ref, tab_hbm, o_ref, tab, sem, wl_sc, *, final):
    tq, hk = w_ref.shape
    _load_table_once(tab_hbm, tab, sem)
    width = hk * PAIR_ROWS
    rep = _group_matrix(hk, width)
    w_hi, w_lo = _split2(w_ref[...])
    parl = jnp.dot(par_ref[...].astype(BF16), rep, preferred_element_type=F32)
    lane = lax.broadcasted_iota(I32, (tq, width), 1)
    wanted = (lane % 2).astype(F32) == parl
    wl_sc[0] = jnp.where(wanted, jnp.dot(w_hi, rep, preferred_element_type=F32), 0.0)
    wl_sc[1] = jnp.where(wanted, jnp.dot(w_lo, rep, preferred_element_type=F32), 0.0)
    srow = lax.broadcasted_iota(I32, (SUB, width), 0)
    scol = lax.broadcasted_iota(I32, (SUB, width), 1)
    mine = (scol % PAIR_ROWS) // 2 == srow
    for tt in range(tq):
        rhs = jnp.concatenate([pltpu.bitcast(_pair_tile(tab, off_ref[tt * hk + k]), BF16) for k in range(hk)],
                              axis=0)
        lhs = jnp.concatenate([jnp.where(mine, wl_sc[0, tt:tt + 1, :], 0.0),
                               jnp.where(mine, wl_sc[1, tt:tt + 1, :], 0.0)], axis=0).astype(BF16)
        r = jnp.dot(lhs, rhs, preferred_element_type=F32)
        y = x1_ref[tt] + g2_ref[0] * (r[0:SUB] + r[SUB:2 * SUB])
        if final:
            ms = jnp.mean(jnp.mean(y * y, axis=-1, keepdims=True), axis=0, keepdims=True)
            y = y * lax.rsqrt(ms + EPS) * fg_ref[...]
        o_ref[tt] = y


def _peer_v(off_flat, par, w, x1, g2, final_g, vtab, *, seq, final, tq=8):
    t, nj, _ = x1.shape
    hk = w.shape[1]
    return pl.pallas_call(
        functools.partial(_peer_v_kernel, final=final),
        grid=(t // tq,),
        in_specs=[
            pl.BlockSpec((tq * hk,), lambda i: (i,), memory_space=pltpu.SMEM),
            pl.BlockSpec((tq, hk), lambda i: (i, 0)),
            pl.BlockSpec((tq, hk), lambda i: (i, 0)),
            pl.BlockSpec((tq, nj, LANES), lambda i: (i, 0, 0)),
            pl.BlockSpec((1, nj, LANES), lambda i: ((i * tq) // seq, 0, 0)),
            pl.BlockSpec((nj, LANES), lambda i: (0, 0)),
            pl.BlockSpec(memory_space=pl.ANY),
        ],
        out_specs=pl.BlockSpec((tq, nj, LANES), lambda i: (i, 0, 0)),
        out_shape=jax.ShapeDtypeStruct((t, nj, LANES), F32),
        scratch_shapes=[pltpu.VMEM(vtab.shape, jnp.uint32), pltpu.SemaphoreType.DMA(()),
                        pltpu.VMEM((2, tq, hk * PAIR_ROWS), F32)],
        compiler_params=pltpu.CompilerParams(dimension_semantics=("arbitrary",),
                                             vmem_limit_bytes=VMEM_LIMIT),
        name="peer_v",
    )(off_flat, par, w, x1, g2, final_g, vtab)


def kernel(x, c, norm1_g, norm2_g, w_ada, b_ada, w_in, w_out, lam_q1, lam_k1, lam_q2, lam_k2, subln_g, sinks,
           peer_wq, peer_keys, peer_u, peer_v, final_g):
    bsz, seq, d = x.shape
    depth = w_in.shape[0]
    t = bsz * seq
    slopes = _alibi_slopes()
    sl_c = [float(s) for s in slopes[:C_HEADS]]
    qa_bias = _slope_bias_col(slopes[C_HEADS:C_HEADS + A_HEADS] * np.float32(LOG2E))
    qb_bias = _slope_bias_col(slopes[C_HEADS + A_HEADS:] * np.float32(LOG2E))
    mods = _adaln_mods(c, w_ada, b_ada)
    x2d = x.reshape(t, d)
    av, bw = A_HEADS * HEAD_DIM, B_HEADS * HEAD_DIM
    for l in range(depth):
        lam_init = 0.8 - 0.6 * math.exp(-0.3 * l)
        mod3 = mods[l].reshape(bsz, 1, N_MOD * d)
        wn, wt = _prep_in_weights(w_in[l])
        ka, kb, qc, vc2, qta, vta, qtb, vtb, ktc = _inproj(
            x2d, mod3, norm1_g[l].reshape(1, d), wn, wt, qa_bias, qb_bias, seq=seq)
        lamv = jnp.stack([lam_q1[l], lam_k1[l], lam_q2[l], lam_k2[l]]).astype(F32)
        sg = subln_g[l].reshape(HEAD_DIM, 1).astype(F32)
        oa = _diff_attention(qta, ka, vta, lamv, sg, bsz=bsz, seq=seq, lam_init=lam_init)
        ob = _moba_attention(qtb, kb, vtb, bsz=bsz, seq=seq)
        oc = _swa_attention(sinks[l].astype(F32), qc, ktc, vc2, bsz=bsz, seq=seq, slopes=sl_c)
        wo = w_out[l].astype(BF16)
        keys = peer_keys[l].reshape(2 * PEER_HEADS, PEER_NKEYS, -1).astype(BF16)
        x1, h2, st = _outproj(oa, ob, oc, wo[:av], wo[av:av + bw], wo[av + bw:], x2d, mod3,
                              norm2_g[l].reshape(1, d), peer_wq[l].astype(BF16), keys, seq=seq)
        toff, par, g = _peer_topk(st)
        nj = d // LANES
        g2 = mods[l][:, 5 * d:6 * d].reshape(bsz, nj, LANES)
        toff = toff.reshape(-1)
        tsh = (16 - 16 * par).reshape(-1)
        w = _peer_u(toff, tsh, h2.reshape(t, nj, LANES), g, _pack_expert_table(peer_u[l]))
        x2d = _peer_v(toff, par, w, x1.reshape(t, nj, LANES), g2, final_g.reshape(nj, LANES),
                      _pack_expert_table(peer_v[l]), seq=seq, final=(l == depth - 1)).reshape(t, d)
    return x2d.reshape(bsz, seq, d)
```

```python
import functools
import math

import numpy as np
import jax
import jax.numpy as jnp
from jax import lax
from jax.experimental import pallas as pl
from jax.experimental.pallas import tpu as pltpu

F32 = jnp.float32
BF16 = jnp.bfloat16
I32 = jnp.int32

D_MODEL = 1024
HEAD_DIM = 64
N_HEADS_TOTAL = 16
A_HEADS = 4
B_HEADS = 6
C_HEADS = 6
C_KV_HEADS = 2
C_GROUP = 3
A_QK_DIM = 32
MOBA_BLOCK = 256
MOBA_TOPK = 3
WINDOW = 128
ALIBI_MAX = 8.0
PEER_HEADS = 8
PEER_NKEYS = 128
PEER_TOPK = 16
N_MOD = 6
EPS = 1e-6

LANES = 128
KT = 256
AUG0 = HEAD_DIM
SEL0 = AUG0 + 6
MAX_BLOCKS = LANES - SEL0
NEG_BIG = -1e30
LOG2E = math.log2(math.e)
VMEM_LIMIT = 56 * 1024 * 1024


def _alibi_slopes():
    n = N_HEADS_TOTAL
    return (2.0 ** (-ALIBI_MAX * np.arange(1, n + 1, dtype=np.float32) / n)).astype(np.float32)


def _split3(v):
    v = np.float32(v)
    hi = np.float32(np.asarray(v).astype(jnp.bfloat16).astype(np.float32))
    r = np.float32(v - hi)
    mid = np.float32(np.asarray(r).astype(jnp.bfloat16).astype(np.float32))
    lo = np.float32(np.float32(r - mid))
    lo = np.float32(np.asarray(lo).astype(jnp.bfloat16).astype(np.float32))
    return hi, mid, lo


def _slope_bias_col(slopes):
    col = np.zeros((LANES * len(slopes), 1), np.float32)
    for h, s in enumerate(slopes):
        hi, mid, lo = _split3(s)
        col[h * LANES + AUG0:h * LANES + AUG0 + 6, 0] = [hi, mid, lo, hi, mid, lo]
    return jnp.asarray(col)


def _mod_kernel(c_ref, w_ref, b_ref, o_ref):
    c = c_ref[...]
    cs = c * (1.0 / (1.0 + jnp.exp(-c)))
    o_ref[0] = jnp.dot(cs, w_ref[0], preferred_element_type=F32) + b_ref[0]


def _adaln_mods(c, w_ada, b_ada):
    depth, d, n = w_ada.shape
    bsz = c.shape[0]
    rows = -(-bsz // 8) * 8
    cp = jnp.pad(c, ((0, rows - bsz), (0, 0)))
    tn = 1536
    out = pl.pallas_call(
        _mod_kernel,
        grid=(depth, n // tn),
        in_specs=[
            pl.BlockSpec((rows, d), lambda l, j: (0, 0)),
            pl.BlockSpec((1, d, tn), lambda l, j: (l, 0, j)),
            pl.BlockSpec((1, 1, tn), lambda l, j: (l, 0, j)),
        ],
        out_specs=pl.BlockSpec((1, rows, tn), lambda l, j: (l, 0, j)),
        out_shape=jax.ShapeDtypeStruct((depth, rows, n), F32),
        compiler_params=pltpu.CompilerParams(dimension_semantics=("arbitrary", "arbitrary"),
                                             vmem_limit_bytes=VMEM_LIMIT),
        name="adaln_mods",
    )(cp, w_ada, b_ada.reshape(depth, 1, n))
    return out[:, :bsz]


NN_WIDTHS = (A_HEADS * LANES, B_HEADS * LANES, C_HEADS * HEAD_DIM, 2 * C_KV_HEADS * HEAD_DIM)
NT_ROWS = (A_HEADS * LANES, A_HEADS * HEAD_DIM, B_HEADS * LANES, B_HEADS * HEAD_DIM,
           C_KV_HEADS * LANES)


def _prep_in_weights(w):
    d = w.shape[0]
    aq, ak, av = A_HEADS * 2 * A_QK_DIM, A_HEADS * 2 * A_QK_DIM, A_HEADS * HEAD_DIM
    bw = B_HEADS * HEAD_DIM
    cq, ckv = C_HEADS * HEAD_DIM, C_KV_HEADS * HEAD_DIM
    cuts = np.cumsum([aq, ak, av, bw, bw, bw, cq, ckv]).tolist()
    qa, ka, va, qb, kb, vb, qc, kc, vc = jnp.split(w, cuts, axis=-1)

    def pad_heads(m, nh, scale):
        m = (m * scale).reshape(d, nh, HEAD_DIM)
        return jnp.pad(m, ((0, 0), (0, 0), (0, LANES - HEAD_DIM))).reshape(d, nh * LANES)

    vc2 = vc.reshape(d, C_KV_HEADS, 1, HEAD_DIM)
    vc2 = jnp.broadcast_to(vc2, (d, C_KV_HEADS, 2, HEAD_DIM)).reshape(d, 2 * ckv)
    kc2 = jnp.broadcast_to(kc.reshape(d, C_KV_HEADS, 1, HEAD_DIM), (d, C_KV_HEADS, 2, HEAD_DIM)).reshape(d, 2 * ckv)
    wn = jnp.concatenate([pad_heads(ka, A_HEADS, 1.0), pad_heads(kb, B_HEADS, 1.0),
                          qc * (HEAD_DIM ** -0.5), vc2], axis=1)
    wt = jnp.concatenate([pad_heads(qa, A_HEADS, A_QK_DIM ** -0.5 * LOG2E), va,
                          pad_heads(qb, B_HEADS, HEAD_DIM ** -0.5 * LOG2E), vb, kc2], axis=1).T
    return wn.astype(BF16), wt.astype(BF16)


def _rms_mod(x, g, sc, sh):
    ms = jnp.mean(x * x, axis=-1, keepdims=True)
    return (x * lax.rsqrt(ms + EPS) * g) * (1.0 + sc) + sh


def _inproj_kernel(x_ref, mod_ref, g_ref, wn_ref, wt_ref, qab_ref, qbb_ref,
                   ka_ref, kb_ref, qc_ref, vc_ref, qta_ref, vta_ref, qtb_ref, vtb_ref, ktc_ref,
                   *, tm, seq):
    d = x_ref.shape[1]
    x = x_ref[...]
    sh = mod_ref[0, :, 0:d]
    sc = mod_ref[0, :, d:2 * d]
    h = _rms_mod(x, g_ref[...], sc, sh).astype(BF16)
    pn = jnp.dot(h, wn_ref[...], preferred_element_type=F32)
    pt = lax.dot_general(wt_ref[...], h, (((1,), (1,)), ((), ())),
                         preferred_element_type=F32)

    pos = (pl.program_id(0) * tm) % seq + lax.broadcasted_iota(I32, (tm, LANES), 0)
    col = lax.broadcasted_iota(I32, (tm, LANES), 1)
    blk_id = pos // KT
    p_hi = (blk_id * KT).astype(F32)
    p_lo = (pos - blk_id * KT).astype(F32)
    aug_a = jnp.where((col >= AUG0) & (col < AUG0 + 3), p_hi,
                      jnp.where((col >= AUG0 + 3) & (col < AUG0 + 6), p_lo, 0.0))
    aug_b = jnp.where((col >= SEL0) & (col - SEL0 == blk_id), 1.0, aug_a)
    for hh in range(A_HEADS):
        ka_ref[:, hh * LANES:(hh + 1) * LANES] = (pn[:, hh * LANES:(hh + 1) * LANES] + aug_a).astype(BF16)
    o = NN_WIDTHS[0]
    for hh in range(B_HEADS):
        kb_ref[:, hh * LANES:(hh + 1) * LANES] = (pn[:, o + hh * LANES:o + (hh + 1) * LANES] + aug_b).astype(BF16)
    o += NN_WIDTHS[1]
    qc_ref[...] = pn[:, o:o + NN_WIDTHS[2]].astype(BF16)
    o += NN_WIDTHS[2]
    vc_ref[...] = pn[:, o:o + NN_WIDTHS[3]].astype(BF16)

    r0 = 0
    for ref, b, nr in zip((qta_ref, vta_ref, qtb_ref, vtb_ref, ktc_ref),
                          (qab_ref, None, qbb_ref, None, None), NT_ROWS):
        blk = pt[r0:r0 + nr, :]
        if b is not None:
            blk = blk + b[...]
        blk = blk.astype(BF16)
        for cc in range(tm // KT):
            ref[cc] = blk[:, cc * KT:(cc + 1) * KT]
        r0 += nr


def _inproj(x2d, mod3, g, wn, wt, qa_bias, qb_bias, *, seq, tm=512):
    t, d = x2d.shape
    assert seq % tm == 0 and tm % KT == 0
    nt = t // tm
    nn_total = sum(NN_WIDTHS)
    row_specs = [pl.BlockSpec((tm, wd), lambda i: (i, 0)) for wd in NN_WIDTHS]
    kt_specs = [pl.BlockSpec((tm // KT, r, KT), lambda i: (i, 0, 0)) for r in NT_ROWS]
    out_shape = ([jax.ShapeDtypeStruct((t, wd), BF16) for wd in NN_WIDTHS]
                 + [jax.ShapeDtypeStruct((t // KT, r, KT), BF16) for r in NT_ROWS])
    return pl.pallas_call(
        functools.partial(_inproj_kernel, tm=tm, seq=seq),
        grid=(nt,),
        in_specs=[
            pl.BlockSpec((tm, d), lambda i: (i, 0)),
            pl.BlockSpec((1, 1, mod3.shape[2]), lambda i: ((i * tm) // seq, 0, 0)),
            pl.BlockSpec((1, d), lambda i: (0, 0)),
            pl.BlockSpec((d, nn_total), lambda i: (0, 0)),
            pl.BlockSpec((sum(NT_ROWS), d), lambda i: (0, 0)),
            pl.BlockSpec((NT_ROWS[0], 1), lambda i: (0, 0)),
            pl.BlockSpec((NT_ROWS[2], 1), lambda i: (0, 0)),
        ],
        out_specs=row_specs + kt_specs,
        out_shape=out_shape,
        compiler_params=pltpu.CompilerParams(dimension_semantics=("arbitrary",),
                                             vmem_limit_bytes=VMEM_LIMIT),
        name="inproj",
    )(x2d, mod3, g, wn, wt, qa_bias, qb_bias)


def _flash_step(k, qt, vt, m_ref, l_ref, acc_ref, mask=None):
    s = jnp.dot(k, qt, preferred_element_type=F32)
    if mask is not None:
        s = jnp.where(mask, s, -jnp.inf)
    m_prev = m_ref[...]
    m_new = jnp.maximum(m_prev, jnp.max(s, axis=0, keepdims=True))
    alpha = jnp.exp2(m_prev - m_new)
    p = jnp.exp2(s - m_new)
    l_ref[...] = alpha * l_ref[...] + jnp.sum(p, axis=0, keepdims=True)
    acc_ref[...] = alpha * acc_ref[...] + jnp.dot(vt, p.astype(BF16), preferred_element_type=F32)
    m_ref[...] = m_new


def _flash_init(m_ref, l_ref, acc_ref):
    m_ref[...] = jnp.full(m_ref.shape, -jnp.inf, F32)
    l_ref[...] = jnp.zeros(l_ref.shape, F32)
    acc_ref[...] = jnp.zeros(acc_ref.shape, F32)


PAST_TILES = 4


def _flash_causal_pair(k_ref, vt_ref, q_diag, q_past, qi, causal, m_ref, l_ref, acc_ref):
    _flash_init(m_ref, l_ref, acc_ref)

    def step(kj, n, qs, mask):
        kk = k_ref[pl.ds(pl.multiple_of(kj * KT, KT), n * KT), :]
        for hh in range(2):
            rows = slice(hh * HEAD_DIM, (hh + 1) * HEAD_DIM)
            vt = vt_ref[kj, rows, :] if n == 1 else jnp.concatenate(
                [vt_ref[kj + c, rows, :] for c in range(n)], axis=1)
            _flash_step(kk[:, hh * LANES:(hh + 1) * LANES], qs[hh], vt,
                        m_ref.at[hh], l_ref.at[hh], acc_ref.at[hh], mask)

    step(qi, 1, q_diag, causal)

    def body(j, carry):
        step(j * PAST_TILES, PAST_TILES, q_past, None)
        return carry

    lax.fori_loop(0, qi // PAST_TILES, body, 0)
    for r in range(PAST_TILES - 1):
        @pl.when(qi % PAST_TILES > r)
        def _():
            step(qi - 1 - r, 1, q_past, None)


def _diff_kernel(qt_ref, k_ref, vt_ref, lamv_ref, sg_ref, o_ref, m_ref, l_ref, acc_ref, *, lam_init):
    tq = qt_ref.shape[2]
    qi = pl.program_id(2)
    lv = lamv_ref[...]
    lam = (jnp.exp(jnp.sum(lv[0:1] * lv[1:2], axis=-1, keepdims=True))
           - jnp.exp(jnp.sum(lv[2:3] * lv[3:4], axis=-1, keepdims=True)) + lam_init)
    row = lax.broadcasted_iota(I32, (LANES, tq), 0)
    kr = lax.broadcasted_iota(I32, (KT, 2 * tq), 0)
    qc = lax.broadcasted_iota(I32, (KT, 2 * tq), 1)
    causal = kr <= jnp.where(qc >= tq, qc - tq, qc)
    qs = []
    for hh in range(2):
        qt = qt_ref[0, hh * LANES:(hh + 1) * LANES, :]
        zero = jnp.zeros_like(qt)
        q1 = jnp.where((row < A_QK_DIM) | (row >= AUG0), qt, zero)
        q2 = jnp.where(row >= A_QK_DIM, qt, zero)
        qs.append(jnp.concatenate([q1, q2], axis=1))
    _flash_causal_pair(k_ref, vt_ref, qs, qs, qi, causal, m_ref, l_ref, acc_ref)
    res = []
    for hh in range(2):
        o = acc_ref[hh] / l_ref[hh]
        od = o[:, :tq] - lam * o[:, tq:]
        ms = jnp.mean(od * od, axis=0, keepdims=True)
        res.append(od * lax.rsqrt(ms + EPS) * sg_ref[...] * (1.0 - lam_init))
    o_ref[...] = jnp.concatenate(res, axis=0).T.astype(BF16)


def _diff_attention(qta, ka, vta, lamv, sg, *, bsz, seq, lam_init):
    tq = KT
    nq = seq // tq
    t = bsz * seq
    return pl.pallas_call(
        functools.partial(_diff_kernel, lam_init=lam_init),
        grid=(bsz, A_HEADS // 2, nq),
        in_specs=[
            pl.BlockSpec((1, 2 * LANES, tq), lambda b, hp, qi: (b * nq + qi, hp, 0)),
            pl.BlockSpec((seq, 2 * LANES), lambda b, hp, qi: (b, hp)),
            pl.BlockSpec((nq, 2 * HEAD_DIM, KT), lambda b, hp, qi: (b, hp, 0)),
            pl.BlockSpec((4, A_QK_DIM), lambda b, hp, qi: (0, 0)),
            pl.BlockSpec((HEAD_DIM, 1), lambda b, hp, qi: (0, 0)),
        ],
        out_specs=pl.BlockSpec((tq, LANES), lambda b, hp, qi: (b * nq + qi, hp)),
        out_shape=jax.ShapeDtypeStruct((t, A_HEADS * HEAD_DIM), BF16),
        scratch_shapes=[pltpu.VMEM((2, 1, 2 * tq), F32), pltpu.VMEM((2, 1, 2 * tq), F32),
                        pltpu.VMEM((2, HEAD_DIM, 2 * tq), F32)],
        compiler_params=pltpu.CompilerParams(dimension_semantics=("arbitrary",) * 3,
                                             vmem_limit_bytes=VMEM_LIMIT),
        name="diff_attn",
    )(qta, ka, vta, lamv, sg)


def _moba_kernel(qt_ref, k_ref, vt_ref, o_ref, m_ref, l_ref, acc_ref, km_ref):
    tq = qt_ref.shape[2]
    nb = vt_ref.shape[0]
    qi = pl.program_id(2)

    @pl.when(qi == 0)
    def _():
        lane1 = lax.broadcasted_iota(I32, (1, LANES), 1)
        for hh in range(2):
            km_ref[hh] = jnp.zeros((LANES, LANES), F32)

            def put_block(j, carry):
                blk = k_ref[pl.ds(pl.multiple_of(j * KT, KT), KT), hh * LANES:(hh + 1) * LANES].astype(F32)
                mean = jnp.sum(blk, axis=0, keepdims=True) * (1.0 / KT)
                km_ref[hh, pl.ds(SEL0 + j, 1), :] = jnp.where(lane1 < HEAD_DIM, mean, 0.0)
                return carry
            lax.fori_loop(0, nb, put_block, 0)

    row = lax.broadcasted_iota(I32, (LANES, tq), 0)
    kr = lax.broadcasted_iota(I32, (KT, tq), 0)
    qc = lax.broadcasted_iota(I32, (KT, tq), 1)
    causal = kr <= qc
    in_sel = (row >= SEL0) & (row < SEL0 + MAX_BLOCKS)
    q_diag, q_past = [], []
    for hh in range(2):
        qt = qt_ref[0, hh * LANES:(hh + 1) * LANES, :]
        km = km_ref[hh]
        km_hi = km.astype(BF16)
        km_lo = (km - km_hi.astype(F32)).astype(BF16)
        gate = (jnp.dot(km_hi, qt, preferred_element_type=F32)
                + jnp.dot(km_lo, qt, preferred_element_type=F32))
        cur = jnp.where((row >= SEL0) & (row < SEL0 + qi), gate, -jnp.inf)
        sel = jnp.zeros((LANES, tq), jnp.bool_)
        for _ in range(MOBA_TOPK):
            mx = jnp.max(cur, axis=0, keepdims=True)
            first = jnp.min(jnp.where(cur == mx, row, 4 * LANES), axis=0, keepdims=True)
            pick = (row == first) & (mx > -jnp.inf)
            sel = sel | pick
            cur = jnp.where(pick, -jnp.inf, cur)
        q_diag.append(jnp.where(in_sel, jnp.zeros_like(qt), qt))
        q_past.append(jnp.where(in_sel, jnp.where(sel, 0.0, NEG_BIG).astype(BF16), qt))
    _flash_causal_pair(k_ref, vt_ref, q_diag, q_past, qi, causal, m_ref, l_ref, acc_ref)
    res = [acc_ref[hh] / l_ref[hh] for hh in range(2)]
    o_ref[...] = jnp.concatenate(res, axis=0).T.astype(BF16)


def _moba_attention(qtb, kb, vtb, *, bsz, seq):
    tq = KT
    nq = seq // tq
    assert nq <= MAX_BLOCKS
    t = bsz * seq
    return pl.pallas_call(
        _moba_kernel,
        grid=(bsz, B_HEADS // 2, nq),
        in_specs=[
            pl.BlockSpec((1, 2 * LANES, tq), lambda b, hp, qi: (b * nq + qi, hp, 0)),
            pl.BlockSpec((seq, 2 * LANES), lambda b, hp, qi: (b, hp)),
            pl.BlockSpec((nq, 2 * HEAD_DIM, KT), lambda b, hp, qi: (b, hp, 0)),
        ],
        out_specs=pl.BlockSpec((tq, LANES), lambda b, hp, qi: (b * nq + qi, hp)),
        out_shape=jax.ShapeDtypeStruct((t, B_HEADS * HEAD_DIM), BF16),
        scratch_shapes=[pltpu.VMEM((2, 1, tq), F32), pltpu.VMEM((2, 1, tq), F32),
                        pltpu.VMEM((2, HEAD_DIM, tq), F32), pltpu.VMEM((2, LANES, LANES), F32)],
        compiler_params=pltpu.CompilerParams(dimension_semantics=("arbitrary",) * 3,
                                             vmem_limit_bytes=VMEM_LIMIT),
        name="moba_attn",
    )(qtb, kb, vtb)


def _swa_kernel(sink_ref, q_ref, ktp_ref, ktc_ref, vp_ref, vc_ref, o_ref, *, slopes):
    tq = q_ref.shape[0]
    n = pl.program_id(1)
    lane = lax.broadcasted_iota(I32, (tq, LANES), 1)
    r2 = lax.broadcasted_iota(I32, (tq, 2 * KT), 0)
    c2 = lax.broadcasted_iota(I32, (tq, 2 * KT), 1)
    rel = r2 + KT - c2
    mask = (rel >= 0) & (rel < WINDOW) & ((c2 >= KT) | (n > 0))
    relf = rel.astype(F32)
    res = []
    for hq in range(C_HEADS):
        kv = hq // C_GROUP
        qp = q_ref[:, (hq // 2) * LANES:(hq // 2 + 1) * LANES]
        qm = jnp.where((lane < HEAD_DIM) == (hq % 2 == 0), qp, jnp.zeros_like(qp))
        kt = jnp.concatenate([ktp_ref[0, kv * LANES:(kv + 1) * LANES, :],
                              ktc_ref[0, kv * LANES:(kv + 1) * LANES, :]], axis=1)
        vv = jnp.concatenate([vp_ref[:, kv * LANES:(kv + 1) * LANES],
                              vc_ref[:, kv * LANES:(kv + 1) * LANES]], axis=0)
        s = jnp.dot(qm, kt, preferred_element_type=F32)
        s = jnp.where(mask, s - slopes[hq] * relf, -jnp.inf)
        sink = sink_ref[hq]
        m = jnp.maximum(jnp.max(s, axis=-1, keepdims=True), sink)
        e = jnp.exp(s - m)
        den = jnp.sum(e, axis=-1, keepdims=True) + jnp.exp(sink - m)
        p = (e / den).astype(BF16)
        res.append(jnp.dot(p, vv, preferred_element_type=F32))
    for pr in range(C_HEADS // 2):
        o_ref[:, pr * LANES:(pr + 1) * LANES] = jnp.where(lane < HEAD_DIM, res[2 * pr], res[2 * pr + 1]).astype(BF16)


def _swa_attention(sinks, qc, ktc, vc2, *, bsz, seq, slopes):
    tq = KT
    nq = seq // tq
    t = bsz * seq
    return pl.pallas_call(
        functools.partial(_swa_kernel, slopes=slopes),
        grid=(bsz, nq),
        in_specs=[
            pl.BlockSpec(memory_space=pltpu.SMEM),
            pl.BlockSpec((tq, C_HEADS * HEAD_DIM), lambda b, n: (b * nq + n, 0)),
            pl.BlockSpec((1, C_KV_HEADS * LANES, KT), lambda b, n: (b * nq + jnp.maximum(n - 1, 0), 0, 0)),
            pl.BlockSpec((1, C_KV_HEADS * LANES, KT), lambda b, n: (b * nq + n, 0, 0)),
            pl.BlockSpec((tq, C_KV_HEADS * LANES), lambda b, n: (b * nq + jnp.maximum(n - 1, 0), 0)),
            pl.BlockSpec((tq, C_KV_HEADS * LANES), lambda b, n: (b * nq + n, 0)),
        ],
        out_specs=pl.BlockSpec((tq, C_HEADS * HEAD_DIM), lambda b, n: (b * nq + n, 0)),
        out_shape=jax.ShapeDtypeStruct((t, C_HEADS * HEAD_DIM), BF16),
        compiler_params=pltpu.CompilerParams(dimension_semantics=("arbitrary",) * 2,
                                             vmem_limit_bytes=VMEM_LIMIT),
        name="swa_attn",
    )(sinks, qc, ktc, ktc, vc2, vc2)


def _outproj_kernel(oa_ref, ob_ref, oc_ref, woa_ref, wob_ref, woc_ref, x_ref, mod_ref, g_ref, wq_ref, keys_ref,
                    x1_ref, h2_ref, st_ref):
    d = x_ref.shape[1]
    mix = (jnp.dot(oa_ref[...], woa_ref[...], preferred_element_type=F32)
           + jnp.dot(ob_ref[...], wob_ref[...], preferred_element_type=F32)
           + jnp.dot(oc_ref[...], woc_ref[...], preferred_element_type=F32))
    g1 = mod_ref[0, :, 2 * d:3 * d]
    sh2 = mod_ref[0, :, 3 * d:4 * d]
    sc2 = mod_ref[0, :, 4 * d:5 * d]
    x1 = x_ref[...] + g1 * mix
    x1_ref[...] = x1
    h2 = _rms_mod(x1, g_ref[...], sc2, sh2)
    h2_ref[...] = h2
    pq = jnp.dot(h2.astype(BF16), wq_ref[...], preferred_element_type=F32).astype(BF16)
    for hp in range(2 * PEER_HEADS):
        st_ref[hp] = lax.dot_general(keys_ref[hp], pq[:, hp * LANES:(hp + 1) * LANES],
                                     (((1,), (1,)), ((), ())), preferred_element_type=F32)


def _outproj(oa, ob, oc, woa, wob, woc, x2d, mod3, g, wq, keys, *, seq, tm=256):
    t, d = x2d.shape
    nt = t // tm
    nq = wq.shape[1]
    full = lambda a: pl.BlockSpec(a.shape, lambda i: (0,) * a.ndim)
    return pl.pallas_call(
        _outproj_kernel,
        grid=(nt,),
        in_specs=[
            pl.BlockSpec((tm, oa.shape[1]), lambda i: (i, 0)),
            pl.BlockSpec((tm, ob.shape[1]), lambda i: (i, 0)),
            pl.BlockSpec((tm, oc.shape[1]), lambda i: (i, 0)),
            full(woa), full(wob), full(woc),
            pl.BlockSpec((tm, d), lambda i: (i, 0)),
            pl.BlockSpec((1, 1, mod3.shape[2]), lambda i: ((i * tm) // seq, 0, 0)),
            pl.BlockSpec((1, d), lambda i: (0, 0)),
            full(wq), full(keys),
        ],
        out_specs=[pl.BlockSpec((tm, d), lambda i: (i, 0)),
                   pl.BlockSpec((tm, d), lambda i: (i, 0)),
                   pl.BlockSpec((2 * PEER_HEADS, PEER_NKEYS, tm), lambda i: (0, 0, i))],
        out_shape=[jax.ShapeDtypeStruct((t, d), F32), jax.ShapeDtypeStruct((t, d), F32),
                   jax.ShapeDtypeStruct((2 * PEER_HEADS, PEER_NKEYS, t), F32)],
        compiler_params=pltpu.CompilerParams(dimension_semantics=("arbitrary",),
                                             vmem_limit_bytes=VMEM_LIMIT),
        name="outproj_peerq",
    )(oa, ob, oc, woa, wob, woc, x2d, mod3, g, wq, keys)


_CAND_BLOCKS = ((0, 16),) + tuple((i, 8) for i in range(1, 8))
_CAND_ROWS = 16 + 7 * 8 + 8
_BIG_I = np.int32(2 ** 30)


def _topk_rows(curs, keys, val_refs, key_refs):
    def body(r, curs):
        out = []
        for cur, key, val_ref, key_ref in zip(curs, keys, val_refs, key_refs):
            m = jnp.max(cur, axis=0, keepdims=True)
            kmin = jnp.min(jnp.where(cur == m, key, _BIG_I), axis=0, keepdims=True)
            val_ref[pl.ds(r, 1), :] = m
            key_ref[pl.ds(r, 1), :] = kmin
            out.append(jnp.where(key == kmin, -jnp.inf, cur))
        return tuple(out)
    lax.fori_loop(0, PEER_TOPK, body, tuple(curs))


def _peer_topk_kernel(st_ref, off_ref, par_ref, g_ref, v1_ref, k1_ref, v2_ref, k2_ref, vt_ref, kt_ref, ei_ref, gg_ref):
    tl = st_ref.shape[2]
    row = lax.broadcasted_iota(I32, (PEER_NKEYS, tl), 0)
    jrow8 = lax.broadcasted_iota(I32, (8, tl), 0)
    jrow16 = lax.broadcasted_iota(I32, (16, tl), 0)
    for h0 in range(0, PEER_HEADS, 2):
        cand, ckey = [], []
        for s in range(2):
            h = h0 + s
            _topk_rows([st_ref[2 * h], st_ref[2 * h + 1]], [row, row],
                       [v1_ref.at[s], v2_ref.at[s]], [k1_ref.at[s], k2_ref.at[s]])
            sv1, si1, sv2, si2 = v1_ref[s], k1_ref[s], v2_ref[s], k2_ref[s]
            cands, keys = [], []
            for i, nj in _CAND_BLOCKS:
                jrow = jrow16 if nj == 16 else jrow8
                cands.append(sv1[i:i + 1] + sv2[0:nj])
                keys.append((i * PEER_TOPK + jrow) * (PEER_NKEYS * PEER_NKEYS)
                            + si1[i:i + 1] * PEER_NKEYS + si2[0:nj])
            cands.append(sv1[8:16] + sv2[0:1])
            keys.append((jrow8 + 8) * (PEER_TOPK * PEER_NKEYS * PEER_NKEYS) + si1[8:16] * PEER_NKEYS + si2[0:1])
            cand.append(jnp.concatenate(cands, axis=0))
            ckey.append(jnp.concatenate(keys, axis=0))
        _topk_rows(cand, ckey, [vt_ref.at[0], vt_ref.at[1]], [kt_ref.at[0], kt_ref.at[1]])
        for s in range(2):
            h = h0 + s
            top = vt_ref[s]
            e = jnp.exp(top - top[0:1])
            gg_ref[h * PEER_TOPK:(h + 1) * PEER_TOPK, :] = e / jnp.sum(e, axis=0, keepdims=True)
            ei_ref[h * PEER_TOPK:(h + 1) * PEER_TOPK, :] = kt_ref[s] & (PEER_NKEYS * PEER_NKEYS - 1)
    ei = ei_ref[...].T
    off_ref[...] = (ei >> 1) * SUB
    par_ref[...] = ei & 1
    g_ref[...] = gg_ref[...].T


def _peer_topk(st, *, tl=128):
    t = st.shape[2]
    hk = PEER_HEADS * PEER_TOPK
    return pl.pallas_call(
        _peer_topk_kernel,
        grid=(t // tl,),
        in_specs=[pl.BlockSpec((2 * PEER_HEADS, PEER_NKEYS, tl), lambda i: (0, 0, i))],
        out_specs=[pl.BlockSpec((tl, hk), lambda i: (i, 0))] * 3,
        out_shape=[jax.ShapeDtypeStruct((t, hk), I32), jax.ShapeDtypeStruct((t, hk), I32),
                   jax.ShapeDtypeStruct((t, hk), F32)],
        scratch_shapes=[pltpu.VMEM((2, PEER_TOPK, tl), F32), pltpu.VMEM((2, PEER_TOPK, tl), I32),
                        pltpu.VMEM((2, PEER_TOPK, tl), F32), pltpu.VMEM((2, PEER_TOPK, tl), I32),
                        pltpu.VMEM((2, PEER_TOPK, tl), F32), pltpu.VMEM((2, PEER_TOPK, tl), I32),
                        pltpu.VMEM((hk, tl), I32), pltpu.VMEM((hk, tl), F32)],
        compiler_params=pltpu.CompilerParams(dimension_semantics=("arbitrary",),
                                             vmem_limit_bytes=VMEM_LIMIT),
        name="peer_topk",
    )(st)


def _erf(x):
    return lax.erf(x)


SUB = 8


def _peer_gather_kernel(idx_ref, idxn_ref, h_ref, g_ref, x1_ref, g2_ref, fg_ref, uv_hbm,
                        o_ref, buf, sem, *, tq, final):
    i = pl.program_id(0)
    n = pl.num_programs(0)
    hk = PEER_HEADS * PEER_TOPK
    groups = tq * hk // SUB
    nj = h_ref.shape[1]

    def issue(iref, slot):
        def one(ro, carry):
            for ri in range(SUB):
                pltpu.make_async_copy(uv_hbm.at[iref[ro * SUB + ri]], buf.at[slot, ro, :, ri, :],
                                      sem.at[slot]).start(priority=ri % 2)
            return carry
        lax.fori_loop(0, groups, one, 0)

    @pl.when(i == 0)
    def _():
        issue(idx_ref, 0)

    slot = i % 2

    @pl.when(i + 1 < n)
    def _():
        issue(idxn_ref, 1 - slot)

    pltpu.make_async_copy(buf.at[slot], buf.at[slot], sem.at[slot]).wait()

    gt = g_ref[...].T
    gpt = hk // SUB
    for tt in range(tq):
        ue = buf[slot, tt * gpt:(tt + 1) * gpt, 0:nj]
        ve = buf[slot, tt * gpt:(tt + 1) * gpt, nj:2 * nj]
        hb = jnp.broadcast_to(h_ref[tt][:, None, :], (nj, SUB, LANES))
        a = jnp.sum(jnp.sum(ue * hb[None], axis=1), axis=-1, keepdims=True)
        act = 0.5 * a * (1.0 + _erf(a * (2.0 ** -0.5)))
        w = gt[:, tt:tt + 1].reshape(gpt, SUB, 1) * act
        peer = jnp.sum(jnp.sum(w[:, None] * ve, axis=0), axis=1)
        y = x1_ref[tt] + g2_ref[0] * peer
        if final:
            ms = jnp.mean(jnp.mean(y * y, axis=-1, keepdims=True), axis=0, keepdims=True)
            y = y * lax.rsqrt(ms + EPS) * fg_ref[...]
        o_ref[tt] = y


def _peer_gather(eidx_flat, h2, g, x1, g2, final_g, uv, *, seq, final, tq=8):
    t, nj, _ = h2.shape
    hk = PEER_HEADS * PEER_TOPK
    rows = tq * hk
    n = t // tq
    return pl.pallas_call(
        functools.partial(_peer_gather_kernel, tq=tq, final=final),
        grid=(n,),
        in_specs=[
            pl.BlockSpec((rows,), lambda i: (i,), memory_space=pltpu.SMEM),
            pl.BlockSpec((rows,), lambda i: (jnp.minimum(i + 1, n - 1),), memory_space=pltpu.SMEM),
            pl.BlockSpec((tq, nj, LANES), lambda i: (i, 0, 0)),
            pl.BlockSpec((tq, hk), lambda i: (i, 0)),
            pl.BlockSpec((tq, nj, LANES), lambda i: (i, 0, 0)),
            pl.BlockSpec((1, nj, LANES), lambda i: ((i * tq) // seq, 0, 0)),
            pl.BlockSpec((nj, LANES), lambda i: (0, 0)),
            pl.BlockSpec(memory_space=pl.ANY),
        ],
        out_specs=pl.BlockSpec((tq, nj, LANES), lambda i: (i, 0, 0)),
        out_shape=jax.ShapeDtypeStruct((t, nj, LANES), F32),
        scratch_shapes=[pltpu.VMEM((2, rows // SUB, 2 * nj, SUB, LANES), F32), pltpu.SemaphoreType.DMA((2,))],
        compiler_params=pltpu.CompilerParams(dimension_semantics=("arbitrary",),
                                             vmem_limit_bytes=VMEM_LIMIT),
        name="peer_gather",
    )(eidx_flat, eidx_flat, h2, g, x1, g2, final_g, uv)


def _pack_expert_table(w):
    n, d = w.shape
    b = lax.bitcast_convert_type(w.astype(BF16), jnp.uint16).astype(jnp.uint32).reshape(n // 2, 2, d // LANES, LANES)
    return (b[:, 0] | (b[:, 1] << 16)).reshape(n // 2 * (d // LANES), LANES)


def _load_table_once(tab_hbm, tab, sem):
    @pl.when(pl.program_id(0) == 0)
    def _():
        cp = pltpu.make_async_copy(tab_hbm, tab, sem)
        cp.start()
        cp.wait()


def _pair_tile(tab, off):
    return tab[pl.ds(pl.multiple_of(off, SUB), SUB), :]


def _group_matrix(rows, cols):
    r = lax.broadcasted_iota(I32, (rows, cols), 0)
    c = lax.broadcasted_iota(I32, (rows, cols), 1)
    return (c // (cols // rows) == r).astype(BF16)


def _split2(x):
    hi = x.astype(BF16)
    return hi, (x - hi.astype(F32)).astype(BF16)


PAIR_ROWS = 2 * SUB


def _pair_rows(tab, off_ref, tt, hk):
    return jnp.concatenate([pltpu.bitcast(_pair_tile(tab, off_ref[tt * hk + k]), BF16) for k in range(hk)], axis=0)


def _own_sublane_mask(hk):
    srow = lax.broadcasted_iota(I32, (SUB, hk * PAIR_ROWS), 0)
    scol = lax.broadcasted_iota(I32, (SUB, hk * PAIR_ROWS), 1)
    return (scol % PAIR_ROWS) // 2 == srow


def _fold_matrix(hk):
    c = jnp.arange(hk * PAIR_ROWS)
    return jax.nn.one_hot((c % 2) * hk + c // PAIR_ROWS, 2 * hk, dtype=BF16)


def _peer_u_kernel(off_ref, par_ref, h_ref, g_ref, fold_ref, tab_hbm, w_ref, tab, sem, d_sc):
    tq, hk = g_ref.shape
    _load_table_once(tab_hbm, tab, sem)
    mine = _own_sublane_mask(hk)
    nt = (((1,), (1,)), ((), ()))
    for tt in range(tq):
        h_hi, h_lo = _split2(h_ref[tt])
        s = lax.dot_general(jnp.concatenate([h_hi, h_lo], axis=0), _pair_rows(tab, off_ref, tt, hk), nt,
                            preferred_element_type=F32)
        d = jnp.where(mine, s[0:SUB] + s[SUB:2 * SUB], 0.0)
        d_sc[tt:tt + 1, :] = jnp.sum(d, axis=0, keepdims=True)
    d_hi, d_lo = _split2(d_sc[...])
    a2 = jnp.dot(jnp.concatenate([d_hi, d_lo], axis=0), fold_ref[...], preferred_element_type=F32)
    a2 = a2[0:tq] + a2[tq:2 * tq]
    a = jnp.where(par_ref[...] == 0, a2[:, :hk], a2[:, hk:])
    w_ref[...] = g_ref[...] * (0.5 * a * (1.0 + _erf(a * (2.0 ** -0.5))))


def _peer_u(off_flat, par, h2, g, utab, *, tq=8):
    t, nj, _ = h2.shape
    hk = g.shape[1]
    fold = _fold_matrix(hk)
    return pl.pallas_call(
        _peer_u_kernel,
        grid=(t // tq,),
        in_specs=[
            pl.BlockSpec((tq * hk,), lambda i: (i,), memory_space=pltpu.SMEM),
            pl.BlockSpec((tq, hk), lambda i: (i, 0)),
            pl.BlockSpec((tq, nj, LANES), lambda i: (i, 0, 0)),
            pl.BlockSpec((tq, hk), lambda i: (i, 0)),
            pl.BlockSpec(fold.shape, lambda i: (0, 0)),
            pl.BlockSpec(memory_space=pl.ANY),
        ],
        out_specs=pl.BlockSpec((tq, hk), lambda i: (i, 0)),
        out_shape=jax.ShapeDtypeStruct((t, hk), F32),
        scratch_shapes=[pltpu.VMEM(utab.shape, jnp.uint32), pltpu.SemaphoreType.DMA(()),
                        pltpu.VMEM((tq, hk * PAIR_ROWS), F32)],
        compiler_params=pltpu.CompilerParams(dimension_semantics=("arbitrary",),
                                             vmem_limit_bytes=VMEM_LIMIT),
        name="peer_u",
    )(off_flat, par, h2, g, fold, utab)


def _peer_v_kernel(off_ref, par_ref, w_ref, x1_ref, g2_ref, fg_ref, tab_hbm, o_ref, tab, sem, wl_sc, *, final):
    tq, hk = w_ref.shape
    _load_table_once(tab_hbm, tab, sem)
    width = hk * PAIR_ROWS
    rep = _group_matrix(hk, width)
    w_hi, w_lo = _split2(w_ref[...])
    parl = jnp.dot(par_ref[...].astype(BF16), rep, preferred_element_type=F32)
    lane = lax.broadcasted_iota(I32, (tq, width), 1)
    wanted = (lane % 2).astype(F32) == parl
    wl_sc[0] = jnp.where(wanted, jnp.dot(w_hi, rep, preferred_element_type=F32), 0.0)
    wl_sc[1] = jnp.where(wanted, jnp.dot(w_lo, rep, preferred_element_type=F32), 0.0)
    mine = _own_sublane_mask(hk)
    for tt in range(tq):
        lhs = jnp.concatenate([jnp.where(mine, wl_sc[0, tt:tt + 1, :], 0.0),
                               jnp.where(mine, wl_sc[1, tt:tt + 1, :], 0.0)], axis=0).astype(BF16)
        r = jnp.dot(lhs, _pair_rows(tab, off_ref, tt, hk), preferred_element_type=F32)
        y = x1_ref[tt] + g2_ref[0] * (r[0:SUB] + r[SUB:2 * SUB])
        if final:
            ms = jnp.mean(jnp.mean(y * y, axis=-1, keepdims=True), axis=0, keepdims=True)
            y = y * lax.rsqrt(ms + EPS) * fg_ref[...]
        o_ref[tt] = y


def _peer_v(off_flat, par, w, x1, g2, final_g, vtab, *, seq, final, tq=8):
    t, nj, _ = x1.shape
    hk = w.shape[1]
    return pl.pallas_call(
        functools.partial(_peer_v_kernel, final=final),
        grid=(t // tq,),
        in_specs=[
            pl.BlockSpec((tq * hk,), lambda i: (i,), memory_space=pltpu.SMEM),
            pl.BlockSpec((tq, hk), lambda i: (i, 0)),
            pl.BlockSpec((tq, hk), lambda i: (i, 0)),
            pl.BlockSpec((tq, nj, LANES), lambda i: (i, 0, 0)),
            pl.BlockSpec((1, nj, LANES), lambda i: ((i * tq) // seq, 0, 0)),
            pl.BlockSpec((nj, LANES), lambda i: (0, 0)),
            pl.BlockSpec(memory_space=pl.ANY),
        ],
        out_specs=pl.BlockSpec((tq, nj, LANES), lambda i: (i, 0, 0)),
        out_shape=jax.ShapeDtypeStruct((t, nj, LANES), F32),
        scratch_shapes=[pltpu.VMEM(vtab.shape, jnp.uint32), pltpu.SemaphoreType.DMA(()),
                        pltpu.VMEM((2, tq, hk * PAIR_ROWS), F32)],
        compiler_params=pltpu.CompilerParams(dimension_semantics=("arbitrary",),
                                             vmem_limit_bytes=VMEM_LIMIT),
        name="peer_v",
    )(off_flat, par, w, x1, g2, final_g, vtab)


def kernel(x, c, norm1_g, norm2_g, w_ada, b_ada, w_in, w_out, lam_q1, lam_k1, lam_q2, lam_k2, subln_g, sinks,
           peer_wq, peer_keys, peer_u, peer_v, final_g):
    bsz, seq, d = x.shape
    depth = w_in.shape[0]
    t = bsz * seq
    slopes = _alibi_slopes()
    sl_c = [float(s) for s in slopes[:C_HEADS]]
    qa_bias = _slope_bias_col(slopes[C_HEADS:C_HEADS + A_HEADS] * np.float32(LOG2E))
    qb_bias = _slope_bias_col(slopes[C_HEADS + A_HEADS:] * np.float32(LOG2E))
    mods = _adaln_mods(c, w_ada, b_ada)
    x2d = x.reshape(t, d)
    av, bw = A_HEADS * HEAD_DIM, B_HEADS * HEAD_DIM
    for l in range(depth):
        lam_init = 0.8 - 0.6 * math.exp(-0.3 * l)
        mod3 = mods[l].reshape(bsz, 1, N_MOD * d)
        wn, wt = _prep_in_weights(w_in[l])
        ka, kb, qc, vc2, qta, vta, qtb, vtb, ktc = _inproj(
            x2d, mod3, norm1_g[l].reshape(1, d), wn, wt, qa_bias, qb_bias, seq=seq)
        lamv = jnp.stack([lam_q1[l], lam_k1[l], lam_q2[l], lam_k2[l]]).astype(F32)
        sg = subln_g[l].reshape(HEAD_DIM, 1).astype(F32)
        oa = _diff_attention(qta, ka, vta, lamv, sg, bsz=bsz, seq=seq, lam_init=lam_init)
        ob = _moba_attention(qtb, kb, vtb, bsz=bsz, seq=seq)
        oc = _swa_attention(sinks[l].astype(F32), qc, ktc, vc2, bsz=bsz, seq=seq, slopes=sl_c)
        wo = w_out[l].astype(BF16)
        keys = peer_keys[l].reshape(2 * PEER_HEADS, PEER_NKEYS, -1).astype(BF16)
        x1, h2, st = _outproj(oa, ob, oc, wo[:av], wo[av:av + bw], wo[av + bw:], x2d, mod3,
                              norm2_g[l].reshape(1, d), peer_wq[l].astype(BF16), keys, seq=seq)
        toff, par, g = _peer_topk(st)
        nj = d // LANES
        g2 = mods[l][:, 5 * d:6 * d].reshape(bsz, nj, LANES)
        toff = toff.reshape(-1)
        w = _peer_u(toff, par, h2.reshape(t, nj, LANES), g, _pack_expert_table(peer_u[l]))
        x2d = _peer_v(toff, par, w, x1.reshape(t, nj, LANES), g2, final_g.reshape(nj, LANES),
                      _pack_expert_table(peer_v[l]), seq=seq, final=(l == depth - 1)).reshape(t, d)
    return x2d.reshape(bsz, seq, d)
```

```python
import functools
import math

import numpy as np
import jax
import jax.numpy as jnp
from jax import lax
from jax.experimental import pallas as pl
from jax.experimental.pallas import tpu as pltpu

F32 = jnp.float32
BF16 = jnp.bfloat16
I32 = jnp.int32

D_MODEL = 1024
HEAD_DIM = 64
N_HEADS_TOTAL = 16
A_HEADS = 4
B_HEADS = 6
C_HEADS = 6
C_KV_HEADS = 2
C_GROUP = 3
A_QK_DIM = 32
MOBA_BLOCK = 256
MOBA_TOPK = 3
WINDOW = 128
ALIBI_MAX = 8.0
PEER_HEADS = 8
PEER_NKEYS = 128
PEER_TOPK = 16
N_MOD = 6
EPS = 1e-6

LANES = 128
KT = 256
AUG0 = HEAD_DIM
SEL0 = AUG0 + 6
MAX_BLOCKS = LANES - SEL0
NEG_BIG = -1e30
LOG2E = math.log2(math.e)
VMEM_LIMIT = 56 * 1024 * 1024


def _alibi_slopes():
    n = N_HEADS_TOTAL
    return (2.0 ** (-ALIBI_MAX * np.arange(1, n + 1, dtype=np.float32) / n)).astype(np.float32)


def _split3(v):
    v = np.float32(v)
    hi = np.float32(np.asarray(v).astype(jnp.bfloat16).astype(np.float32))
    r = np.float32(v - hi)
    mid = np.float32(np.asarray(r).astype(jnp.bfloat16).astype(np.float32))
    lo = np.float32(np.float32(r - mid))
    lo = np.float32(np.asarray(lo).astype(jnp.bfloat16).astype(np.float32))
    return hi, mid, lo


def _slope_bias_col(slopes):
    col = np.zeros((LANES * len(slopes), 1), np.float32)
    for h, s in enumerate(slopes):
        hi, mid, lo = _split3(s)
        col[h * LANES + AUG0:h * LANES + AUG0 + 6, 0] = [hi, mid, lo, hi, mid, lo]
    return jnp.asarray(col)


def _mod_kernel(c_ref, w_ref, b_ref, o_ref):
    c = c_ref[...]
    cs = c * (1.0 / (1.0 + jnp.exp(-c)))
    o_ref[0] = jnp.dot(cs, w_ref[0], preferred_element_type=F32) + b_ref[0]


def _adaln_mods(c, w_ada, b_ada):
    depth, d, n = w_ada.shape
    bsz = c.shape[0]
    rows = -(-bsz // 8) * 8
    cp = jnp.pad(c, ((0, rows - bsz), (0, 0)))
    tn = 1536
    out = pl.pallas_call(
        _mod_kernel,
        grid=(depth, n // tn),
        in_specs=[
            pl.BlockSpec((rows, d), lambda l, j: (0, 0)),
            pl.BlockSpec((1, d, tn), lambda l, j: (l, 0, j)),
            pl.BlockSpec((1, 1, tn), lambda l, j: (l, 0, j)),
        ],
        out_specs=pl.BlockSpec((1, rows, tn), lambda l, j: (l, 0, j)),
        out_shape=jax.ShapeDtypeStruct((depth, rows, n), F32),
        compiler_params=pltpu.CompilerParams(dimension_semantics=("arbitrary", "arbitrary"),
                                             vmem_limit_bytes=VMEM_LIMIT),
        name="adaln_mods",
    )(cp, w_ada, b_ada.reshape(depth, 1, n))
    return out[:, :bsz]


NN_WIDTHS = (A_HEADS * LANES, B_HEADS * LANES, C_HEADS * HEAD_DIM, 2 * C_KV_HEADS * HEAD_DIM)
NT_ROWS = (A_HEADS * LANES, A_HEADS * HEAD_DIM, B_HEADS * LANES, B_HEADS * HEAD_DIM,
           C_KV_HEADS * LANES)


def _prep_in_weights(w):
    d = w.shape[0]
    aq, ak, av = A_HEADS * 2 * A_QK_DIM, A_HEADS * 2 * A_QK_DIM, A_HEADS * HEAD_DIM
    bw = B_HEADS * HEAD_DIM
    cq, ckv = C_HEADS * HEAD_DIM, C_KV_HEADS * HEAD_DIM
    cuts = np.cumsum([aq, ak, av, bw, bw, bw, cq, ckv]).tolist()
    qa, ka, va, qb, kb, vb, qc, kc, vc = jnp.split(w, cuts, axis=-1)

    def pad_heads(m, nh, scale):
        m = (m * scale).reshape(d, nh, HEAD_DIM)
        return jnp.pad(m, ((0, 0), (0, 0), (0, LANES - HEAD_DIM))).reshape(d, nh * LANES)

    vc2 = vc.reshape(d, C_KV_HEADS, 1, HEAD_DIM)
    vc2 = jnp.broadcast_to(vc2, (d, C_KV_HEADS, 2, HEAD_DIM)).reshape(d, 2 * ckv)
    kc2 = jnp.broadcast_to(kc.reshape(d, C_KV_HEADS, 1, HEAD_DIM), (d, C_KV_HEADS, 2, HEAD_DIM)).reshape(d, 2 * ckv)
    wn = jnp.concatenate([pad_heads(ka, A_HEADS, 1.0), pad_heads(kb, B_HEADS, 1.0),
                          qc * (HEAD_DIM ** -0.5), vc2], axis=1)
    wt = jnp.concatenate([pad_heads(qa, A_HEADS, A_QK_DIM ** -0.5 * LOG2E), va,
                          pad_heads(qb, B_HEADS, HEAD_DIM ** -0.5 * LOG2E), vb, kc2], axis=1).T
    return wn.astype(BF16), wt.astype(BF16)


def _rms_mod(x, g, sc, sh):
    ms = jnp.mean(x * x, axis=-1, keepdims=True)
    return (x * lax.rsqrt(ms + EPS) * g) * (1.0 + sc) + sh


def _inproj_kernel(x_ref, mod_ref, g_ref, wn_ref, wt_ref, qab_ref, qbb_ref,
                   ka_ref, kb_ref, qc_ref, vc_ref, qta_ref, vta_ref, qtb_ref, vtb_ref, ktc_ref,
                   *, tm, seq):
    d = x_ref.shape[1]
    x = x_ref[...]
    sh = mod_ref[0, :, 0:d]
    sc = mod_ref[0, :, d:2 * d]
    h = _rms_mod(x, g_ref[...], sc, sh).astype(BF16)
    pn = jnp.dot(h, wn_ref[...], preferred_element_type=F32)
    pt = lax.dot_general(wt_ref[...], h, (((1,), (1,)), ((), ())),
                         preferred_element_type=F32)

    pos = (pl.program_id(0) * tm) % seq + lax.broadcasted_iota(I32, (tm, LANES), 0)
    col = lax.broadcasted_iota(I32, (tm, LANES), 1)
    blk_id = pos // KT
    p_hi = (blk_id * KT).astype(F32)
    p_lo = (pos - blk_id * KT).astype(F32)
    aug_a = jnp.where((col >= AUG0) & (col < AUG0 + 3), p_hi,
                      jnp.where((col >= AUG0 + 3) & (col < AUG0 + 6), p_lo, 0.0))
    aug_b = jnp.where((col >= SEL0) & (col - SEL0 == blk_id), 1.0, aug_a)
    for hh in range(A_HEADS):
        ka_ref[:, hh * LANES:(hh + 1) * LANES] = (pn[:, hh * LANES:(hh + 1) * LANES] + aug_a).astype(BF16)
    o = NN_WIDTHS[0]
    for hh in range(B_HEADS):
        kb_ref[:, hh * LANES:(hh + 1) * LANES] = (pn[:, o + hh * LANES:o + (hh + 1) * LANES] + aug_b).astype(BF16)
    o += NN_WIDTHS[1]
    qc_ref[...] = pn[:, o:o + NN_WIDTHS[2]].astype(BF16)
    o += NN_WIDTHS[2]
    vc_ref[...] = pn[:, o:o + NN_WIDTHS[3]].astype(BF16)

    r0 = 0
    for ref, b, nr in zip((qta_ref, vta_ref, qtb_ref, vtb_ref, ktc_ref),
                          (qab_ref, None, qbb_ref, None, None), NT_ROWS):
        blk = pt[r0:r0 + nr, :]
        if b is not None:
            blk = blk + b[...]
        blk = blk.astype(BF16)
        for cc in range(tm // KT):
            ref[cc] = blk[:, cc * KT:(cc + 1) * KT]
        r0 += nr


def _inproj(x2d, mod3, g, wn, wt, qa_bias, qb_bias, *, seq, tm=512):
    t, d = x2d.shape
    assert seq % tm == 0 and tm % KT == 0
    nt = t // tm
    nn_total = sum(NN_WIDTHS)
    row_specs = [pl.BlockSpec((tm, wd), lambda i: (i, 0)) for wd in NN_WIDTHS]
    kt_specs = [pl.BlockSpec((tm // KT, r, KT), lambda i: (i, 0, 0)) for r in NT_ROWS]
    out_shape = ([jax.ShapeDtypeStruct((t, wd), BF16) for wd in NN_WIDTHS]
                 + [jax.ShapeDtypeStruct((t // KT, r, KT), BF16) for r in NT_ROWS])
    return pl.pallas_call(
        functools.partial(_inproj_kernel, tm=tm, seq=seq),
        grid=(nt,),
        in_specs=[
            pl.BlockSpec((tm, d), lambda i: (i, 0)),
            pl.BlockSpec((1, 1, mod3.shape[2]), lambda i: ((i * tm) // seq, 0, 0)),
            pl.BlockSpec((1, d), lambda i: (0, 0)),
            pl.BlockSpec((d, nn_total), lambda i: (0, 0)),
            pl.BlockSpec((sum(NT_ROWS), d), lambda i: (0, 0)),
            pl.BlockSpec((NT_ROWS[0], 1), lambda i: (0, 0)),
            pl.BlockSpec((NT_ROWS[2], 1), lambda i: (0, 0)),
        ],
        out_specs=row_specs + kt_specs,
        out_shape=out_shape,
        compiler_params=pltpu.CompilerParams(dimension_semantics=("arbitrary",),
                                             vmem_limit_bytes=VMEM_LIMIT),
        name="inproj",
    )(x2d, mod3, g, wn, wt, qa_bias, qb_bias)


ACC_ROWS = HEAD_DIM + 16


def _flash_step(k, qt, vt1, m_ref, acc_ref, mask=None):
    s = jnp.dot(k, qt, preferred_element_type=F32)
    if mask is not None:
        s = jnp.where(mask, s, -jnp.inf)
    m_prev = m_ref[...]
    m_new = jnp.maximum(m_prev, jnp.max(s, axis=0, keepdims=True))
    alpha = jnp.exp2(m_prev - m_new)
    p = jnp.exp2((s - m_new).astype(BF16))
    acc_ref[...] = alpha * acc_ref[...] + jnp.dot(vt1, p, preferred_element_type=F32)
    m_ref[...] = m_new


def _flash_init(m_ref, acc_ref):
    m_ref[...] = jnp.full(m_ref.shape, -jnp.inf, F32)
    acc_ref[...] = jnp.zeros(acc_ref.shape, F32)


def _flash_finish(acc_ref, hh):
    acc = acc_ref[hh]
    return acc[0:HEAD_DIM] / acc[HEAD_DIM:HEAD_DIM + 1]


PAST_TILES = 4


def _flash_causal_pair(k_ref, vt_ref, q_diag, q_past, qi, causal, m_ref, acc_ref):
    _flash_init(m_ref, acc_ref)

    def step(kj, n, qs, mask):
        kk = k_ref[pl.ds(pl.multiple_of(kj * KT, KT), n * KT), :]
        ones = jnp.ones((ACC_ROWS - HEAD_DIM, n * KT), BF16)
        for hh in range(2):
            rows = slice(hh * HEAD_DIM, (hh + 1) * HEAD_DIM)
            vt = vt_ref[kj, rows, :] if n == 1 else jnp.concatenate(
                [vt_ref[kj + c, rows, :] for c in range(n)], axis=1)
            _flash_step(kk[:, hh * LANES:(hh + 1) * LANES], qs[hh], jnp.concatenate([vt, ones], axis=0),
                        m_ref.at[hh], acc_ref.at[hh], mask)

    step(qi, 1, q_diag, causal)

    def body(j, carry):
        step(j * PAST_TILES, PAST_TILES, q_past, None)
        return carry

    lax.fori_loop(0, qi // PAST_TILES, body, 0)
    for r in range(PAST_TILES - 1):
        @pl.when(qi % PAST_TILES > r)
        def _():
            step(qi - 1 - r, 1, q_past, None)


def _diff_kernel(qt_ref, k_ref, vt_ref, lamv_ref, sg_ref, o_ref, m_ref, acc_ref, *, lam_init):
    tq = qt_ref.shape[2]
    qi = pl.program_id(2)
    lv = lamv_ref[...]
    lam = (jnp.exp(jnp.sum(lv[0:1] * lv[1:2], axis=-1, keepdims=True))
           - jnp.exp(jnp.sum(lv[2:3] * lv[3:4], axis=-1, keepdims=True)) + lam_init)
    row = lax.broadcasted_iota(I32, (LANES, tq), 0)
    kr = lax.broadcasted_iota(I32, (KT, 2 * tq), 0)
    qc = lax.broadcasted_iota(I32, (KT, 2 * tq), 1)
    causal = kr <= jnp.where(qc >= tq, qc - tq, qc)
    qs = []
    for hh in range(2):
        qt = qt_ref[0, hh * LANES:(hh + 1) * LANES, :]
        zero = jnp.zeros_like(qt)
        q1 = jnp.where((row < A_QK_DIM) | (row >= AUG0), qt, zero)
        q2 = jnp.where(row >= A_QK_DIM, qt, zero)
        qs.append(jnp.concatenate([q1, q2], axis=1))
    _flash_causal_pair(k_ref, vt_ref, qs, qs, qi, causal, m_ref, acc_ref)
    res = []
    for hh in range(2):
        o = _flash_finish(acc_ref, hh)
        od = o[:, :tq] - lam * o[:, tq:]
        ms = jnp.mean(od * od, axis=0, keepdims=True)
        res.append(od * lax.rsqrt(ms + EPS) * sg_ref[...] * (1.0 - lam_init))
    o_ref[...] = jnp.concatenate(res, axis=0).T.astype(BF16)


def _diff_attention(qta, ka, vta, lamv, sg, *, bsz, seq, lam_init):
    tq = KT
    nq = seq // tq
    t = bsz * seq
    return pl.pallas_call(
        functools.partial(_diff_kernel, lam_init=lam_init),
        grid=(bsz, A_HEADS // 2, nq),
        in_specs=[
            pl.BlockSpec((1, 2 * LANES, tq), lambda b, hp, qi: (b * nq + qi, hp, 0)),
            pl.BlockSpec((seq, 2 * LANES), lambda b, hp, qi: (b, hp)),
            pl.BlockSpec((nq, 2 * HEAD_DIM, KT), lambda b, hp, qi: (b, hp, 0)),
            pl.BlockSpec((4, A_QK_DIM), lambda b, hp, qi: (0, 0)),
            pl.BlockSpec((HEAD_DIM, 1), lambda b, hp, qi: (0, 0)),
        ],
        out_specs=pl.BlockSpec((tq, LANES), lambda b, hp, qi: (b * nq + qi, hp)),
        out_shape=jax.ShapeDtypeStruct((t, A_HEADS * HEAD_DIM), BF16),
        scratch_shapes=[pltpu.VMEM((2, 1, 2 * tq), F32), pltpu.VMEM((2, ACC_ROWS, 2 * tq), F32)],
        compiler_params=pltpu.CompilerParams(dimension_semantics=("arbitrary",) * 3,
                                             vmem_limit_bytes=VMEM_LIMIT),
        name="diff_attn",
    )(qta, ka, vta, lamv, sg)


def _moba_kernel(qt_ref, k_ref, vt_ref, o_ref, m_ref, acc_ref, km_ref):
    tq = qt_ref.shape[2]
    nb = vt_ref.shape[0]
    qi = pl.program_id(2)

    @pl.when(qi == 0)
    def _():
        lane1 = lax.broadcasted_iota(I32, (1, LANES), 1)
        for hh in range(2):
            km_ref[hh] = jnp.zeros((LANES, LANES), F32)

            def put_block(j, carry):
                blk = k_ref[pl.ds(pl.multiple_of(j * KT, KT), KT), hh * LANES:(hh + 1) * LANES].astype(F32)
                mean = jnp.sum(blk, axis=0, keepdims=True) * (1.0 / KT)
                km_ref[hh, pl.ds(SEL0 + j, 1), :] = jnp.where(lane1 < HEAD_DIM, mean, 0.0)
                return carry
            lax.fori_loop(0, nb, put_block, 0)

    row = lax.broadcasted_iota(I32, (LANES, tq), 0)
    kr = lax.broadcasted_iota(I32, (KT, tq), 0)
    qc = lax.broadcasted_iota(I32, (KT, tq), 1)
    causal = kr <= qc
    in_sel = (row >= SEL0) & (row < SEL0 + MAX_BLOCKS)
    q_diag, q_past = [], []
    for hh in range(2):
        qt = qt_ref[0, hh * LANES:(hh + 1) * LANES, :]
        km = km_ref[hh]
        km_hi = km.astype(BF16)
        km_lo = (km - km_hi.astype(F32)).astype(BF16)
        gate = (jnp.dot(km_hi, qt, preferred_element_type=F32)
                + jnp.dot(km_lo, qt, preferred_element_type=F32))
        cur = jnp.where((row >= SEL0) & (row < SEL0 + qi), gate, -jnp.inf)
        sel = jnp.zeros((LANES, tq), jnp.bool_)
        for _ in range(MOBA_TOPK):
            mx = jnp.max(cur, axis=0, keepdims=True)
            first = jnp.min(jnp.where(cur == mx, row, 4 * LANES), axis=0, keepdims=True)
            pick = (row == first) & (mx > -jnp.inf)
            sel = sel | pick
            cur = jnp.where(pick, -jnp.inf, cur)
        q_diag.append(jnp.where(in_sel, jnp.zeros_like(qt), qt))
        q_past.append(jnp.where(in_sel, jnp.where(sel, 0.0, NEG_BIG).astype(BF16), qt))
    _flash_causal_pair(k_ref, vt_ref, q_diag, q_past, qi, causal, m_ref, acc_ref)
    res = [_flash_finish(acc_ref, hh) for hh in range(2)]
    o_ref[...] = jnp.concatenate(res, axis=0).T.astype(BF16)


def _moba_attention(qtb, kb, vtb, *, bsz, seq):
    tq = KT
    nq = seq // tq
    assert nq <= MAX_BLOCKS
    t = bsz * seq
    return pl.pallas_call(
        _moba_kernel,
        grid=(bsz, B_HEADS // 2, nq),
        in_specs=[
            pl.BlockSpec((1, 2 * LANES, tq), lambda b, hp, qi: (b * nq + qi, hp, 0)),
            pl.BlockSpec((seq, 2 * LANES), lambda b, hp, qi: (b, hp)),
            pl.BlockSpec((nq, 2 * HEAD_DIM, KT), lambda b, hp, qi: (b, hp, 0)),
        ],
        out_specs=pl.BlockSpec((tq, LANES), lambda b, hp, qi: (b * nq + qi, hp)),
        out_shape=jax.ShapeDtypeStruct((t, B_HEADS * HEAD_DIM), BF16),
        scratch_shapes=[pltpu.VMEM((2, 1, tq), F32), pltpu.VMEM((2, ACC_ROWS, tq), F32),
                        pltpu.VMEM((2, LANES, LANES), F32)],
        compiler_params=pltpu.CompilerParams(dimension_semantics=("arbitrary",) * 3,
                                             vmem_limit_bytes=VMEM_LIMIT),
        name="moba_attn",
    )(qtb, kb, vtb)


def _swa_kernel(sink_ref, q_ref, ktp_ref, ktc_ref, vp_ref, vc_ref, o_ref, *, slopes):
    tq = q_ref.shape[0]
    n = pl.program_id(1)
    lane = lax.broadcasted_iota(I32, (tq, LANES), 1)
    r2 = lax.broadcasted_iota(I32, (tq, 2 * KT), 0)
    c2 = lax.broadcasted_iota(I32, (tq, 2 * KT), 1)
    rel = r2 + KT - c2
    mask = (rel >= 0) & (rel < WINDOW) & ((c2 >= KT) | (n > 0))
    relf = rel.astype(F32)
    res = []
    for hq in range(C_HEADS):
        kv = hq // C_GROUP
        qp = q_ref[:, (hq // 2) * LANES:(hq // 2 + 1) * LANES]
        qm = jnp.where((lane < HEAD_DIM) == (hq % 2 == 0), qp, jnp.zeros_like(qp))
        kt = jnp.concatenate([ktp_ref[0, kv * LANES:(kv + 1) * LANES, :],
                              ktc_ref[0, kv * LANES:(kv + 1) * LANES, :]], axis=1)
        vv = jnp.concatenate([vp_ref[:, kv * LANES:(kv + 1) * LANES],
                              vc_ref[:, kv * LANES:(kv + 1) * LANES]], axis=0)
        s = jnp.dot(qm, kt, preferred_element_type=F32)
        s = jnp.where(mask, s - slopes[hq] * relf, -jnp.inf)
        sink = sink_ref[hq]
        m = jnp.maximum(jnp.max(s, axis=-1, keepdims=True), sink)
        e = jnp.exp(s - m)
        den = jnp.sum(e, axis=-1, keepdims=True) + jnp.exp(sink - m)
        p = (e / den).astype(BF16)
        res.append(jnp.dot(p, vv, preferred_element_type=F32))
    for pr in range(C_HEADS // 2):
        o_ref[:, pr * LANES:(pr + 1) * LANES] = jnp.where(lane < HEAD_DIM, res[2 * pr], res[2 * pr + 1]).astype(BF16)


def _swa_attention(sinks, qc, ktc, vc2, *, bsz, seq, slopes):
    tq = KT
    nq = seq // tq
    t = bsz * seq
    return pl.pallas_call(
        functools.partial(_swa_kernel, slopes=slopes),
        grid=(bsz, nq),
        in_specs=[
            pl.BlockSpec(memory_space=pltpu.SMEM),
            pl.BlockSpec((tq, C_HEADS * HEAD_DIM), lambda b, n: (b * nq + n, 0)),
            pl.BlockSpec((1, C_KV_HEADS * LANES, KT), lambda b, n: (b * nq + jnp.maximum(n - 1, 0), 0, 0)),
            pl.BlockSpec((1, C_KV_HEADS * LANES, KT), lambda b, n: (b * nq + n, 0, 0)),
            pl.BlockSpec((tq, C_KV_HEADS * LANES), lambda b, n: (b * nq + jnp.maximum(n - 1, 0), 0)),
            pl.BlockSpec((tq, C_KV_HEADS * LANES), lambda b, n: (b * nq + n, 0)),
        ],
        out_specs=pl.BlockSpec((tq, C_HEADS * HEAD_DIM), lambda b, n: (b * nq + n, 0)),
        out_shape=jax.ShapeDtypeStruct((t, C_HEADS * HEAD_DIM), BF16),
        compiler_params=pltpu.CompilerParams(dimension_semantics=("arbitrary",) * 2,
                                             vmem_limit_bytes=VMEM_LIMIT),
        name="swa_attn",
    )(sinks, qc, ktc, ktc, vc2, vc2)


def _outproj_kernel(oa_ref, ob_ref, oc_ref, woa_ref, wob_ref, woc_ref, x_ref, mod_ref, g_ref, wq_ref, keys_ref,
                    x1_ref, h2_ref, st_ref):
    d = x_ref.shape[1]
    mix = (jnp.dot(oa_ref[...], woa_ref[...], preferred_element_type=F32)
           + jnp.dot(ob_ref[...], wob_ref[...], preferred_element_type=F32)
           + jnp.dot(oc_ref[...], woc_ref[...], preferred_element_type=F32))
    g1 = mod_ref[0, :, 2 * d:3 * d]
    sh2 = mod_ref[0, :, 3 * d:4 * d]
    sc2 = mod_ref[0, :, 4 * d:5 * d]
    x1 = x_ref[...] + g1 * mix
    x1_ref[...] = x1
    h2 = _rms_mod(x1, g_ref[...], sc2, sh2)
    h2_ref[...] = h2
    pq = jnp.dot(h2.astype(BF16), wq_ref[...], preferred_element_type=F32).astype(BF16)
    for hp in range(2 * PEER_HEADS):
        st_ref[hp] = lax.dot_general(keys_ref[hp], pq[:, hp * LANES:(hp + 1) * LANES],
                                     (((1,), (1,)), ((), ())), preferred_element_type=F32)


def _outproj(oa, ob, oc, woa, wob, woc, x2d, mod3, g, wq, keys, *, seq, tm=256):
    t, d = x2d.shape
    nt = t // tm
    nq = wq.shape[1]
    full = lambda a: pl.BlockSpec(a.shape, lambda i: (0,) * a.ndim)
    return pl.pallas_call(
        _outproj_kernel,
        grid=(nt,),
        in_specs=[
            pl.BlockSpec((tm, oa.shape[1]), lambda i: (i, 0)),
            pl.BlockSpec((tm, ob.shape[1]), lambda i: (i, 0)),
            pl.BlockSpec((tm, oc.shape[1]), lambda i: (i, 0)),
            full(woa), full(wob), full(woc),
            pl.BlockSpec((tm, d), lambda i: (i, 0)),
            pl.BlockSpec((1, 1, mod3.shape[2]), lambda i: ((i * tm) // seq, 0, 0)),
            pl.BlockSpec((1, d), lambda i: (0, 0)),
            full(wq), full(keys),
        ],
        out_specs=[pl.BlockSpec((tm, d), lambda i: (i, 0)),
                   pl.BlockSpec((tm, d), lambda i: (i, 0)),
                   pl.BlockSpec((2 * PEER_HEADS, PEER_NKEYS, tm), lambda i: (0, 0, i))],
        out_shape=[jax.ShapeDtypeStruct((t, d), F32), jax.ShapeDtypeStruct((t, d), F32),
                   jax.ShapeDtypeStruct((2 * PEER_HEADS, PEER_NKEYS, t), F32)],
        compiler_params=pltpu.CompilerParams(dimension_semantics=("arbitrary",),
                                             vmem_limit_bytes=VMEM_LIMIT),
        name="outproj_peerq",
    )(oa, ob, oc, woa, wob, woc, x2d, mod3, g, wq, keys)


_CAND_BLOCKS = ((0, 16),) + tuple((i, 8) for i in range(1, 8))
_CAND_ROWS = 16 + 7 * 8 + 8
_BIG_I = np.int32(2 ** 30)


def _topk_rows(curs, keys, val_refs, key_refs):
    def body(r, curs):
        out = []
        for cur, key, val_ref, key_ref in zip(curs, keys, val_refs, key_refs):
            m = jnp.max(cur, axis=0, keepdims=True)
            kmin = jnp.min(jnp.where(cur == m, key, _BIG_I), axis=0, keepdims=True)
            val_ref[pl.ds(r, 1), :] = m
            key_ref[pl.ds(r, 1), :] = kmin
            out.append(jnp.where(key == kmin, -jnp.inf, cur))
        return tuple(out)
    lax.fori_loop(0, PEER_TOPK, body, tuple(curs))


def _peer_topk_kernel(st_ref, off_ref, par_ref, g_ref, v1_ref, k1_ref, v2_ref, k2_ref, vt_ref, kt_ref, ei_ref, gg_ref):
    tl = st_ref.shape[2]
    row = lax.broadcasted_iota(I32, (PEER_NKEYS, tl), 0)
    jrow8 = lax.broadcasted_iota(I32, (8, tl), 0)
    jrow16 = lax.broadcasted_iota(I32, (16, tl), 0)
    for h0 in range(0, PEER_HEADS, 2):
        cand, ckey = [], []
        for s in range(2):
            h = h0 + s
            _topk_rows([st_ref[2 * h], st_ref[2 * h + 1]], [row, row],
                       [v1_ref.at[s], v2_ref.at[s]], [k1_ref.at[s], k2_ref.at[s]])
            sv1, si1, sv2, si2 = v1_ref[s], k1_ref[s], v2_ref[s], k2_ref[s]
            cands, keys = [], []
            for i, nj in _CAND_BLOCKS:
                jrow = jrow16 if nj == 16 else jrow8
                cands.append(sv1[i:i + 1] + sv2[0:nj])
                keys.append((i * PEER_TOPK + jrow) * (PEER_NKEYS * PEER_NKEYS)
                            + si1[i:i + 1] * PEER_NKEYS + si2[0:nj])
            cands.append(sv1[8:16] + sv2[0:1])
            keys.append((jrow8 + 8) * (PEER_TOPK * PEER_NKEYS * PEER_NKEYS) + si1[8:16] * PEER_NKEYS + si2[0:1])
            cand.append(jnp.concatenate(cands, axis=0))
            ckey.append(jnp.concatenate(keys, axis=0))
        _topk_rows(cand, ckey, [vt_ref.at[0], vt_ref.at[1]], [kt_ref.at[0], kt_ref.at[1]])
        for s in range(2):
            h = h0 + s
            top = vt_ref[s]
            e = jnp.exp(top - top[0:1])
            gg_ref[h * PEER_TOPK:(h + 1) * PEER_TOPK, :] = e / jnp.sum(e, axis=0, keepdims=True)
            ei_ref[h * PEER_TOPK:(h + 1) * PEER_TOPK, :] = kt_ref[s] & (PEER_NKEYS * PEER_NKEYS - 1)
    ei = ei_ref[...].T
    off_ref[...] = (ei >> 1) * SUB
    par_ref[...] = ei & 1
    g_ref[...] = gg_ref[...].T


def _peer_topk(st, *, tl=128):
    t = st.shape[2]
    hk = PEER_HEADS * PEER_TOPK
    return pl.pallas_call(
        _peer_topk_kernel,
        grid=(t // tl,),
        in_specs=[pl.BlockSpec((2 * PEER_HEADS, PEER_NKEYS, tl), lambda i: (0, 0, i))],
        out_specs=[pl.BlockSpec((tl, hk), lambda i: (i, 0))] * 3,
        out_shape=[jax.ShapeDtypeStruct((t, hk), I32), jax.ShapeDtypeStruct((t, hk), I32),
                   jax.ShapeDtypeStruct((t, hk), F32)],
        scratch_shapes=[pltpu.VMEM((2, PEER_TOPK, tl), F32), pltpu.VMEM((2, PEER_TOPK, tl), I32),
                        pltpu.VMEM((2, PEER_TOPK, tl), F32), pltpu.VMEM((2, PEER_TOPK, tl), I32),
                        pltpu.VMEM((2, PEER_TOPK, tl), F32), pltpu.VMEM((2, PEER_TOPK, tl), I32),
                        pltpu.VMEM((hk, tl), I32), pltpu.VMEM((hk, tl), F32)],
        compiler_params=pltpu.CompilerParams(dimension_semantics=("arbitrary",),
                                             vmem_limit_bytes=VMEM_LIMIT),
        name="peer_topk",
    )(st)


SUB = 8


def _erf(x):
    return lax.erf(x)


def _pack_expert_table(w):
    n, d = w.shape
    b = lax.bitcast_convert_type(w.astype(BF16), jnp.uint16).astype(jnp.uint32).reshape(n // 2, 2, d // LANES, LANES)
    return (b[:, 0] | (b[:, 1] << 16)).reshape(n // 2 * (d // LANES), LANES)


def _load_table_once(tab_hbm, tab, sem):
    @pl.when(pl.program_id(0) == 0)
    def _():
        cp = pltpu.make_async_copy(tab_hbm, tab, sem)
        cp.start()
        cp.wait()


def _pair_tile(tab, off):
    return tab[pl.ds(pl.multiple_of(off, SUB), SUB), :]


def _group_matrix(rows, cols):
    r = lax.broadcasted_iota(I32, (rows, cols), 0)
    c = lax.broadcasted_iota(I32, (rows, cols), 1)
    return (c // (cols // rows) == r).astype(BF16)


def _split2(x):
    hi = x.astype(BF16)
    return hi, (x - hi.astype(F32)).astype(BF16)


PAIR_ROWS = 2 * SUB


def _pair_rows(tab, off_ref, tt, hk):
    return jnp.concatenate([pltpu.bitcast(_pair_tile(tab, off_ref[tt * hk + k]), BF16) for k in range(hk)], axis=0)


def _own_sublane_mask(hk):
    srow = lax.broadcasted_iota(I32, (SUB, hk * PAIR_ROWS), 0)
    scol = lax.broadcasted_iota(I32, (SUB, hk * PAIR_ROWS), 1)
    return (scol % PAIR_ROWS) // 2 == srow


def _fold_matrix(hk):
    c = jnp.arange(hk * PAIR_ROWS)
    return jax.nn.one_hot((c % 2) * hk + c // PAIR_ROWS, 2 * hk, dtype=BF16)


def _peer_u_kernel(off_ref, par_ref, h_ref, g_ref, fold_ref, tab_hbm, w_ref, tab, sem, d_sc):
    tq, hk = g_ref.shape
    _load_table_once(tab_hbm, tab, sem)
    mine = _own_sublane_mask(hk)
    nt = (((1,), (1,)), ((), ()))
    for tt in range(tq):
        h_hi, h_lo = _split2(h_ref[tt])
        s = lax.dot_general(jnp.concatenate([h_hi, h_lo], axis=0), _pair_rows(tab, off_ref, tt, hk), nt,
                            preferred_element_type=F32)
        d = jnp.where(mine, s[0:SUB] + s[SUB:2 * SUB], 0.0)
        d_sc[tt:tt + 1, :] = jnp.sum(d, axis=0, keepdims=True)
    d_hi, d_lo = _split2(d_sc[...])
    a2 = jnp.dot(jnp.concatenate([d_hi, d_lo], axis=0), fold_ref[...], preferred_element_type=F32)
    a2 = a2[0:tq] + a2[tq:2 * tq]
    a = jnp.where(par_ref[...] == 0, a2[:, :hk], a2[:, hk:])
    w_ref[...] = g_ref[...] * (0.5 * a * (1.0 + _erf(a * (2.0 ** -0.5))))


def _peer_u(off_flat, par, h2, g, utab, *, tq=16):
    t, nj, _ = h2.shape
    hk = g.shape[1]
    fold = _fold_matrix(hk)
    return pl.pallas_call(
        _peer_u_kernel,
        grid=(t // tq,),
        in_specs=[
            pl.BlockSpec((tq * hk,), lambda i: (i,), memory_space=pltpu.SMEM),
            pl.BlockSpec((tq, hk), lambda i: (i, 0)),
            pl.BlockSpec((tq, nj, LANES), lambda i: (i, 0, 0)),
            pl.BlockSpec((tq, hk), lambda i: (i, 0)),
            pl.BlockSpec(fold.shape, lambda i: (0, 0)),
            pl.BlockSpec(memory_space=pl.ANY),
        ],
        out_specs=pl.BlockSpec((tq, hk), lambda i: (i, 0)),
        out_shape=jax.ShapeDtypeStruct((t, hk), F32),
        scratch_shapes=[pltpu.VMEM(utab.shape, jnp.uint32), pltpu.SemaphoreType.DMA(()),
                        pltpu.VMEM((tq, hk * PAIR_ROWS), F32)],
        compiler_params=pltpu.CompilerParams(dimension_semantics=("arbitrary",),
                                             vmem_limit_bytes=VMEM_LIMIT),
        name="peer_u",
    )(off_flat, par, h2, g, fold, utab)


def _peer_v_kernel(off_ref, par_ref, w_ref, x1_ref, g2_ref, fg_ref, tab_hbm, o_ref, tab, sem, wl_sc, *, final):
    tq, hk = w_ref.shape
    _load_table_once(tab_hbm, tab, sem)
    width = hk * PAIR_ROWS
    rep = _group_matrix(hk, width)
    w_hi, w_lo = _split2(w_ref[...])
    parl = jnp.dot(par_ref[...].astype(BF16), rep, preferred_element_type=F32)
    lane = lax.broadcasted_iota(I32, (tq, width), 1)
    wanted = (lane % 2).astype(F32) == parl
    wl_sc[0] = jnp.where(wanted, jnp.dot(w_hi, rep, preferred_element_type=F32), 0.0)
    wl_sc[1] = jnp.where(wanted, jnp.dot(w_lo, rep, preferred_element_type=F32), 0.0)
    mine = _own_sublane_mask(hk)
    for tt in range(tq):
        lhs = jnp.concatenate([jnp.where(mine, wl_sc[0, tt:tt + 1, :], 0.0),
                               jnp.where(mine, wl_sc[1, tt:tt + 1, :], 0.0)], axis=0).astype(BF16)
        r = jnp.dot(lhs, _pair_rows(tab, off_ref, tt, hk), preferred_element_type=F32)
        y = x1_ref[tt] + g2_ref[0] * (r[0:SUB] + r[SUB:2 * SUB])
        if final:
            ms = jnp.mean(jnp.mean(y * y, axis=-1, keepdims=True), axis=0, keepdims=True)
            y = y * lax.rsqrt(ms + EPS) * fg_ref[...]
        o_ref[tt] = y


def _peer_v(off_flat, par, w, x1, g2, final_g, vtab, *, seq, final, tq=16):
    t, nj, _ = x1.shape
    hk = w.shape[1]
    return pl.pallas_call(
        functools.partial(_peer_v_kernel, final=final),
        grid=(t // tq,),
        in_specs=[
            pl.BlockSpec((tq * hk,), lambda i: (i,), memory_space=pltpu.SMEM),
            pl.BlockSpec((tq, hk), lambda i: (i, 0)),
            pl.BlockSpec((tq, hk), lambda i: (i, 0)),
            pl.BlockSpec((tq, nj, LANES), lambda i: (i, 0, 0)),
            pl.BlockSpec((1, nj, LANES), lambda i: ((i * tq) // seq, 0, 0)),
            pl.BlockSpec((nj, LANES), lambda i: (0, 0)),
            pl.BlockSpec(memory_space=pl.ANY),
        ],
        out_specs=pl.BlockSpec((tq, nj, LANES), lambda i: (i, 0, 0)),
        out_shape=jax.ShapeDtypeStruct((t, nj, LANES), F32),
        scratch_shapes=[pltpu.VMEM(vtab.shape, jnp.uint32), pltpu.SemaphoreType.DMA(()),
                        pltpu.VMEM((2, tq, hk * PAIR_ROWS), F32)],
        compiler_params=pltpu.CompilerParams(dimension_semantics=("arbitrary",),
                                             vmem_limit_bytes=VMEM_LIMIT),
        name="peer_v",
    )(off_flat, par, w, x1, g2, final_g, vtab)


def kernel(x, c, norm1_g, norm2_g, w_ada, b_ada, w_in, w_out, lam_q1, lam_k1, lam_q2, lam_k2, subln_g, sinks,
           peer_wq, peer_keys, peer_u, peer_v, final_g):
    bsz, seq, d = x.shape
    depth = w_in.shape[0]
    t = bsz * seq
    slopes = _alibi_slopes()
    sl_c = [float(s) for s in slopes[:C_HEADS]]
    qa_bias = _slope_bias_col(slopes[C_HEADS:C_HEADS + A_HEADS] * np.float32(LOG2E))
    qb_bias = _slope_bias_col(slopes[C_HEADS + A_HEADS:] * np.float32(LOG2E))
    mods = _adaln_mods(c, w_ada, b_ada)
    x2d = x.reshape(t, d)
    av, bw = A_HEADS * HEAD_DIM, B_HEADS * HEAD_DIM
    for l in range(depth):
        lam_init = 0.8 - 0.6 * math.exp(-0.3 * l)
        mod3 = mods[l].reshape(bsz, 1, N_MOD * d)
        wn, wt = _prep_in_weights(w_in[l])
        ka, kb, qc, vc2, qta, vta, qtb, vtb, ktc = _inproj(
            x2d, mod3, norm1_g[l].reshape(1, d), wn, wt, qa_bias, qb_bias, seq=seq)
        lamv = jnp.stack([lam_q1[l], lam_k1[l], lam_q2[l], lam_k2[l]]).astype(F32)
        sg = subln_g[l].reshape(HEAD_DIM, 1).astype(F32)
        oa = _diff_attention(qta, ka, vta, lamv, sg, bsz=bsz, seq=seq, lam_init=lam_init)
        ob = _moba_attention(qtb, kb, vtb, bsz=bsz, seq=seq)
        oc = _swa_attention(sinks[l].astype(F32), qc, ktc, vc2, bsz=bsz, seq=seq, slopes=sl_c)
        wo = w_out[l].astype(BF16)
        keys = peer_keys[l].reshape(2 * PEER_HEADS, PEER_NKEYS, -1).astype(BF16)
        x1, h2, st = _outproj(oa, ob, oc, wo[:av], wo[av:av + bw], wo[av + bw:], x2d, mod3,
                              norm2_g[l].reshape(1, d), peer_wq[l].astype(BF16), keys, seq=seq)
        toff, par, g = _peer_topk(st)
        nj = d // LANES
        g2 = mods[l][:, 5 * d:6 * d].reshape(bsz, nj, LANES)
        toff = toff.reshape(-1)
        w = _peer_u(toff, par, h2.reshape(t, nj, LANES), g, _pack_expert_table(peer_u[l]))
        x2d = _peer_v(toff, par, w, x1.reshape(t, nj, LANES), g2, final_g.reshape(nj, LANES),
                      _pack_expert_table(peer_v[l]), seq=seq, final=(l == depth - 1)).reshape(t, d)
    return x2d.reshape(bsz, seq, d)
```

```python
import functools
import math

import numpy as np
import jax
import jax.numpy as jnp
from jax import lax
from jax.experimental import pallas as pl
from jax.experimental.pallas import tpu as pltpu

F32 = jnp.float32
BF16 = jnp.bfloat16
I32 = jnp.int32

D_MODEL = 1024
HEAD_DIM = 64
N_HEADS_TOTAL = 16
A_HEADS = 4
B_HEADS = 6
C_HEADS = 6
C_KV_HEADS = 2
C_GROUP = 3
A_QK_DIM = 32
MOBA_BLOCK = 256
MOBA_TOPK = 3
WINDOW = 128
ALIBI_MAX = 8.0
PEER_HEADS = 8
PEER_NKEYS = 128
PEER_TOPK = 16
N_MOD = 6
EPS = 1e-6

LANES = 128
KT = 256
AUG0 = HEAD_DIM
SEL0 = AUG0 + 6
MAX_BLOCKS = LANES - SEL0
NEG_BIG = -1e30
LOG2E = math.log2(math.e)
VMEM_LIMIT = 56 * 1024 * 1024


def _alibi_slopes():
    n = N_HEADS_TOTAL
    return (2.0 ** (-ALIBI_MAX * np.arange(1, n + 1, dtype=np.float32) / n)).astype(np.float32)


def _split3(v):
    v = np.float32(v)
    hi = np.float32(np.asarray(v).astype(jnp.bfloat16).astype(np.float32))
    r = np.float32(v - hi)
    mid = np.float32(np.asarray(r).astype(jnp.bfloat16).astype(np.float32))
    lo = np.float32(np.float32(r - mid))
    lo = np.float32(np.asarray(lo).astype(jnp.bfloat16).astype(np.float32))
    return hi, mid, lo


def _slope_bias_col(slopes):
    col = np.zeros((LANES * len(slopes), 1), np.float32)
    for h, s in enumerate(slopes):
        hi, mid, lo = _split3(s)
        col[h * LANES + AUG0:h * LANES + AUG0 + 6, 0] = [hi, mid, lo, hi, mid, lo]
    return jnp.asarray(col)


def _mod_kernel(c_ref, w_ref, b_ref, o_ref):
    c = c_ref[...]
    cs = c * (1.0 / (1.0 + jnp.exp(-c)))
    o_ref[0] = jnp.dot(cs, w_ref[0], preferred_element_type=F32) + b_ref[0]


def _adaln_mods(c, w_ada, b_ada):
    depth, d, n = w_ada.shape
    bsz = c.shape[0]
    rows = -(-bsz // 8) * 8
    cp = jnp.pad(c, ((0, rows - bsz), (0, 0)))
    tn = 1536
    out = pl.pallas_call(
        _mod_kernel,
        grid=(depth, n // tn),
        in_specs=[
            pl.BlockSpec((rows, d), lambda l, j: (0, 0)),
            pl.BlockSpec((1, d, tn), lambda l, j: (l, 0, j)),
            pl.BlockSpec((1, 1, tn), lambda l, j: (l, 0, j)),
        ],
        out_specs=pl.BlockSpec((1, rows, tn), lambda l, j: (l, 0, j)),
        out_shape=jax.ShapeDtypeStruct((depth, rows, n), F32),
        compiler_params=pltpu.CompilerParams(dimension_semantics=("arbitrary", "arbitrary"),
                                             vmem_limit_bytes=VMEM_LIMIT),
        name="adaln_mods",
    )(cp, w_ada, b_ada.reshape(depth, 1, n))
    return out[:, :bsz]


NN_WIDTHS = (A_HEADS * LANES, B_HEADS * LANES, C_HEADS * HEAD_DIM, 2 * C_KV_HEADS * HEAD_DIM)
NT_ROWS = (A_HEADS * LANES, A_HEADS * HEAD_DIM, B_HEADS * LANES, B_HEADS * HEAD_DIM,
           C_KV_HEADS * LANES)


def _prep_in_weights(w):
    d = w.shape[0]
    aq, ak, av = A_HEADS * 2 * A_QK_DIM, A_HEADS * 2 * A_QK_DIM, A_HEADS * HEAD_DIM
    bw = B_HEADS * HEAD_DIM
    cq, ckv = C_HEADS * HEAD_DIM, C_KV_HEADS * HEAD_DIM
    cuts = np.cumsum([aq, ak, av, bw, bw, bw, cq, ckv]).tolist()
    qa, ka, va, qb, kb, vb, qc, kc, vc = jnp.split(w, cuts, axis=-1)

    def pad_heads(m, nh, scale):
        m = (m * scale).reshape(d, nh, HEAD_DIM)
        return jnp.pad(m, ((0, 0), (0, 0), (0, LANES - HEAD_DIM))).reshape(d, nh * LANES)

    vc2 = vc.reshape(d, C_KV_HEADS, 1, HEAD_DIM)
    vc2 = jnp.broadcast_to(vc2, (d, C_KV_HEADS, 2, HEAD_DIM)).reshape(d, 2 * ckv)
    kc2 = jnp.broadcast_to(kc.reshape(d, C_KV_HEADS, 1, HEAD_DIM), (d, C_KV_HEADS, 2, HEAD_DIM)).reshape(d, 2 * ckv)
    wn = jnp.concatenate([pad_heads(ka, A_HEADS, 1.0), pad_heads(kb, B_HEADS, 1.0),
                          qc * (HEAD_DIM ** -0.5), vc2], axis=1)
    wt = jnp.concatenate([pad_heads(qa, A_HEADS, A_QK_DIM ** -0.5 * LOG2E), va,
                          pad_heads(qb, B_HEADS, HEAD_DIM ** -0.5 * LOG2E), vb, kc2], axis=1).T
    return wn.astype(BF16), wt.astype(BF16)


def _rms_mod(x, g, sc, sh):
    ms = jnp.mean(x * x, axis=-1, keepdims=True)
    return (x * lax.rsqrt(ms + EPS) * g) * (1.0 + sc) + sh


def _inproj_kernel(x_ref, mod_ref, g_ref, wn_ref, wt_ref, qab_ref, qbb_ref,
                   ka_ref, kb_ref, qc_ref, vc_ref, qta_ref, vta_ref, qtb_ref, vtb_ref, ktc_ref,
                   *, tm, seq):
    d = x_ref.shape[1]
    x = x_ref[...]
    sh = mod_ref[0, :, 0:d]
    sc = mod_ref[0, :, d:2 * d]
    h = _rms_mod(x, g_ref[...], sc, sh).astype(BF16)
    pn = jnp.dot(h, wn_ref[...], preferred_element_type=F32)
    pt = lax.dot_general(wt_ref[...], h, (((1,), (1,)), ((), ())),
                         preferred_element_type=F32)

    pos = (pl.program_id(0) * tm) % seq + lax.broadcasted_iota(I32, (tm, LANES), 0)
    col = lax.broadcasted_iota(I32, (tm, LANES), 1)
    blk_id = pos // KT
    p_hi = (blk_id * KT).astype(F32)
    p_lo = (pos - blk_id * KT).astype(F32)
    aug_a = jnp.where((col >= AUG0) & (col < AUG0 + 3), p_hi,
                      jnp.where((col >= AUG0 + 3) & (col < AUG0 + 6), p_lo, 0.0))
    aug_b = jnp.where((col >= SEL0) & (col - SEL0 == blk_id), 1.0, aug_a)
    for hh in range(A_HEADS):
        ka_ref[:, hh * LANES:(hh + 1) * LANES] = (pn[:, hh * LANES:(hh + 1) * LANES] + aug_a).astype(BF16)
    o = NN_WIDTHS[0]
    for hh in range(B_HEADS):
        kb_ref[:, hh * LANES:(hh + 1) * LANES] = (pn[:, o + hh * LANES:o + (hh + 1) * LANES] + aug_b).astype(BF16)
    o += NN_WIDTHS[1]
    qc_ref[...] = pn[:, o:o + NN_WIDTHS[2]].astype(BF16)
    o += NN_WIDTHS[2]
    vc_ref[...] = pn[:, o:o + NN_WIDTHS[3]].astype(BF16)

    r0 = 0
    for ref, b, nr in zip((qta_ref, vta_ref, qtb_ref, vtb_ref, ktc_ref),
                          (qab_ref, None, qbb_ref, None, None), NT_ROWS):
        blk = pt[r0:r0 + nr, :]
        if b is not None:
            blk = blk + b[...]
        blk = blk.astype(BF16)
        for cc in range(tm // KT):
            ref[cc] = blk[:, cc * KT:(cc + 1) * KT]
        r0 += nr


def _inproj(x2d, mod3, g, wn, wt, qa_bias, qb_bias, *, seq, tm=512):
    t, d = x2d.shape
    assert seq % tm == 0 and tm % KT == 0
    nt = t // tm
    nn_total = sum(NN_WIDTHS)
    row_specs = [pl.BlockSpec((tm, wd), lambda i: (i, 0)) for wd in NN_WIDTHS]
    kt_specs = [pl.BlockSpec((tm // KT, r, KT), lambda i: (i, 0, 0)) for r in NT_ROWS]
    out_shape = ([jax.ShapeDtypeStruct((t, wd), BF16) for wd in NN_WIDTHS]
                 + [jax.ShapeDtypeStruct((t // KT, r, KT), BF16) for r in NT_ROWS])
    return pl.pallas_call(
        functools.partial(_inproj_kernel, tm=tm, seq=seq),
        grid=(nt,),
        in_specs=[
            pl.BlockSpec((tm, d), lambda i: (i, 0)),
            pl.BlockSpec((1, 1, mod3.shape[2]), lambda i: ((i * tm) // seq, 0, 0)),
            pl.BlockSpec((1, d), lambda i: (0, 0)),
            pl.BlockSpec((d, nn_total), lambda i: (0, 0)),
            pl.BlockSpec((sum(NT_ROWS), d), lambda i: (0, 0)),
            pl.BlockSpec((NT_ROWS[0], 1), lambda i: (0, 0)),
            pl.BlockSpec((NT_ROWS[2], 1), lambda i: (0, 0)),
        ],
        out_specs=row_specs + kt_specs,
        out_shape=out_shape,
        compiler_params=pltpu.CompilerParams(dimension_semantics=("arbitrary",),
                                             vmem_limit_bytes=VMEM_LIMIT),
        name="inproj",
    )(x2d, mod3, g, wn, wt, qa_bias, qb_bias)


ACC_ROWS = HEAD_DIM + 16


def _softmax_pv(s, vt1, m_ref, acc_ref):
    m_prev = m_ref[...]
    m_new = jnp.maximum(m_prev, jnp.max(s, axis=0, keepdims=True))
    alpha = jnp.exp2(m_prev - m_new)
    p = jnp.exp2((s - m_new).astype(BF16))
    acc_ref[...] = alpha * acc_ref[...] + jnp.dot(vt1, p, preferred_element_type=F32)
    m_ref[...] = m_new


def _flash_init(m_ref, acc_ref):
    m_ref[...] = jnp.full(m_ref.shape, -jnp.inf, F32)
    acc_ref[...] = jnp.zeros(acc_ref.shape, F32)


def _flash_finish(acc_ref, hh):
    acc = acc_ref[hh]
    return acc[0:HEAD_DIM] / acc[HEAD_DIM:HEAD_DIM + 1]


PAST_TILES = 4


def _flash_causal_pair(k_ref, vt_ref, q_diag, q_past, qi, causal, m_ref, acc_ref, s_ref=None):
    _flash_init(m_ref, acc_ref)
    nkeys = PAST_TILES * KT

    def vt1(kj, n, hh):
        rows = slice(hh * HEAD_DIM, (hh + 1) * HEAD_DIM)
        vt = vt_ref[kj, rows, :] if n == 1 else jnp.concatenate([vt_ref[kj + c, rows, :] for c in range(n)], axis=1)
        return jnp.concatenate([vt, jnp.ones((ACC_ROWS - HEAD_DIM, n * KT), BF16)], axis=0)

    def scores(kj, n, qs, hh):
        kk = k_ref[pl.ds(pl.multiple_of(kj * KT, KT), n * KT), hh * LANES:(hh + 1) * LANES]
        return jnp.dot(kk, qs[hh], preferred_element_type=F32)

    def step(kj, n, qs, mask=None):
        for hh in range(2):
            s = scores(kj, n, qs, hh)
            if mask is not None:
                s = jnp.where(mask, s, -jnp.inf)
            _softmax_pv(s, vt1(kj, n, hh), m_ref.at[hh], acc_ref.at[hh])

    step(qi, 1, q_diag, causal)

    if s_ref is None:
        def body(j, carry):
            step(j * PAST_TILES, PAST_TILES, q_past)
            return carry

        lax.fori_loop(0, qi // PAST_TILES, body, 0)
        for r in range(PAST_TILES - 1):
            @pl.when(qi % PAST_TILES > r)
            def _():
                step(qi - 1 - r, 1, q_past)
        return

    ngroups = k_ref.shape[0] // nkeys

    def scores_into(slot, g):
        for hh in range(2):
            s_ref[slot, hh] = scores(g * PAST_TILES, PAST_TILES, q_past, hh)

    scores_into(0, 0)

    def body(g, carry):
        scores_into((g + 1) % 2, jnp.minimum(g + 1, ngroups - 1))
        for hh in range(2):
            _softmax_pv(s_ref[g % 2, hh], vt1(g * PAST_TILES, PAST_TILES, hh), m_ref.at[hh], acc_ref.at[hh])
        return carry

    lax.fori_loop(0, (qi + PAST_TILES - 1) // PAST_TILES, body, 0)


def _diff_kernel(qt_ref, k_ref, vt_ref, lamv_ref, sg_ref, o_ref, m_ref, acc_ref, *, lam_init):
    tq = qt_ref.shape[2]
    qi = pl.program_id(2)
    lv = lamv_ref[...]
    lam = (jnp.exp(jnp.sum(lv[0:1] * lv[1:2], axis=-1, keepdims=True))
           - jnp.exp(jnp.sum(lv[2:3] * lv[3:4], axis=-1, keepdims=True)) + lam_init)
    row = lax.broadcasted_iota(I32, (LANES, tq), 0)
    kr = lax.broadcasted_iota(I32, (KT, 2 * tq), 0)
    qc = lax.broadcasted_iota(I32, (KT, 2 * tq), 1)
    causal = kr <= jnp.where(qc >= tq, qc - tq, qc)
    qs = []
    for hh in range(2):
        qt = qt_ref[0, hh * LANES:(hh + 1) * LANES, :]
        zero = jnp.zeros_like(qt)
        q1 = jnp.where((row < A_QK_DIM) | (row >= AUG0), qt, zero)
        q2 = jnp.where(row >= A_QK_DIM, qt, zero)
        qs.append(jnp.concatenate([q1, q2], axis=1))
    _flash_causal_pair(k_ref, vt_ref, qs, qs, qi, causal, m_ref, acc_ref)
    res = []
    for hh in range(2):
        o = _flash_finish(acc_ref, hh)
        od = o[:, :tq] - lam * o[:, tq:]
        ms = jnp.mean(od * od, axis=0, keepdims=True)
        res.append(od * lax.rsqrt(ms + EPS) * sg_ref[...] * (1.0 - lam_init))
    o_ref[...] = jnp.concatenate(res, axis=0).T.astype(BF16)


def _diff_attention(qta, ka, vta, lamv, sg, *, bsz, seq, lam_init):
    tq = KT
    nq = seq // tq
    t = bsz * seq
    return pl.pallas_call(
        functools.partial(_diff_kernel, lam_init=lam_init),
        grid=(bsz, A_HEADS // 2, nq),
        in_specs=[
            pl.BlockSpec((1, 2 * LANES, tq), lambda b, hp, qi: (b * nq + qi, hp, 0)),
            pl.BlockSpec((seq, 2 * LANES), lambda b, hp, qi: (b, hp)),
            pl.BlockSpec((nq, 2 * HEAD_DIM, KT), lambda b, hp, qi: (b, hp, 0)),
            pl.BlockSpec((4, A_QK_DIM), lambda b, hp, qi: (0, 0)),
            pl.BlockSpec((HEAD_DIM, 1), lambda b, hp, qi: (0, 0)),
        ],
        out_specs=pl.BlockSpec((tq, LANES), lambda b, hp, qi: (b * nq + qi, hp)),
        out_shape=jax.ShapeDtypeStruct((t, A_HEADS * HEAD_DIM), BF16),
        scratch_shapes=[pltpu.VMEM((2, 1, 2 * tq), F32), pltpu.VMEM((2, ACC_ROWS, 2 * tq), F32)],
        compiler_params=pltpu.CompilerParams(dimension_semantics=("arbitrary",) * 3,
                                             vmem_limit_bytes=VMEM_LIMIT),
        name="diff_attn",
    )(qta, ka, vta, lamv, sg)


def _moba_kernel(qt_ref, k_ref, vt_ref, o_ref, m_ref, acc_ref, km_ref, s_ref):
    tq = qt_ref.shape[2]
    nb = vt_ref.shape[0]
    qi = pl.program_id(2)

    @pl.when(qi == 0)
    def _():
        lane1 = lax.broadcasted_iota(I32, (1, LANES), 1)
        for hh in range(2):
            km_ref[hh] = jnp.zeros((LANES, LANES), F32)

            def put_block(j, carry):
                blk = k_ref[pl.ds(pl.multiple_of(j * KT, KT), KT), hh * LANES:(hh + 1) * LANES].astype(F32)
                mean = jnp.sum(blk, axis=0, keepdims=True) * (1.0 / KT)
                km_ref[hh, pl.ds(SEL0 + j, 1), :] = jnp.where(lane1 < HEAD_DIM, mean, 0.0)
                return carry
            lax.fori_loop(0, nb, put_block, 0)

    row = lax.broadcasted_iota(I32, (LANES, tq), 0)
    kr = lax.broadcasted_iota(I32, (KT, tq), 0)
    qc = lax.broadcasted_iota(I32, (KT, tq), 1)
    causal = kr <= qc
    in_sel = (row >= SEL0) & (row < SEL0 + MAX_BLOCKS)
    q_diag, q_past = [], []
    for hh in range(2):
        qt = qt_ref[0, hh * LANES:(hh + 1) * LANES, :]
        km = km_ref[hh]
        km_hi = km.astype(BF16)
        km_lo = (km - km_hi.astype(F32)).astype(BF16)
        gate = (jnp.dot(km_hi, qt, preferred_element_type=F32)
                + jnp.dot(km_lo, qt, preferred_element_type=F32))
        cur = jnp.where((row >= SEL0) & (row < SEL0 + qi), gate, -jnp.inf)
        sel = jnp.zeros((LANES, tq), jnp.bool_)
        for _ in range(MOBA_TOPK):
            mx = jnp.max(cur, axis=0, keepdims=True)
            first = jnp.min(jnp.where(cur == mx, row, 4 * LANES), axis=0, keepdims=True)
            pick = (row == first) & (mx > -jnp.inf)
            sel = sel | pick
            cur = jnp.where(pick, -jnp.inf, cur)
        q_diag.append(jnp.where(in_sel, jnp.zeros_like(qt), qt))
        q_past.append(jnp.where(in_sel, jnp.where(sel, 0.0, NEG_BIG).astype(BF16), qt))
    _flash_causal_pair(k_ref, vt_ref, q_diag, q_past, qi, causal, m_ref, acc_ref, s_ref)
    res = [_flash_finish(acc_ref, hh) for hh in range(2)]
    o_ref[...] = jnp.concatenate(res, axis=0).T.astype(BF16)


def _moba_attention(qtb, kb, vtb, *, bsz, seq):
    tq = KT
    nq = seq // tq
    assert nq <= MAX_BLOCKS and seq % (PAST_TILES * KT) == 0
    t = bsz * seq
    return pl.pallas_call(
        _moba_kernel,
        grid=(bsz, B_HEADS // 2, nq),
        in_specs=[
            pl.BlockSpec((1, 2 * LANES, tq), lambda b, hp, qi: (b * nq + qi, hp, 0)),
            pl.BlockSpec((seq, 2 * LANES), lambda b, hp, qi: (b, hp)),
            pl.BlockSpec((nq, 2 * HEAD_DIM, KT), lambda b, hp, qi: (b, hp, 0)),
        ],
        out_specs=pl.BlockSpec((tq, LANES), lambda b, hp, qi: (b * nq + qi, hp)),
        out_shape=jax.ShapeDtypeStruct((t, B_HEADS * HEAD_DIM), BF16),
        scratch_shapes=[pltpu.VMEM((2, 1, tq), F32), pltpu.VMEM((2, ACC_ROWS, tq), F32),
                        pltpu.VMEM((2, LANES, LANES), F32), pltpu.VMEM((2, 2, PAST_TILES * KT, tq), F32)],
        compiler_params=pltpu.CompilerParams(dimension_semantics=("arbitrary",) * 3,
                                             vmem_limit_bytes=VMEM_LIMIT),
        name="moba_attn",
    )(qtb, kb, vtb)


def _swa_kernel(sink_ref, q_ref, ktp_ref, ktc_ref, vp_ref, vc_ref, o_ref, *, slopes):
    tq = q_ref.shape[0]
    n = pl.program_id(1)
    lane = lax.broadcasted_iota(I32, (tq, LANES), 1)
    r2 = lax.broadcasted_iota(I32, (tq, 2 * KT), 0)
    c2 = lax.broadcasted_iota(I32, (tq, 2 * KT), 1)
    rel = r2 + KT - c2
    mask = (rel >= 0) & (rel < WINDOW) & ((c2 >= KT) | (n > 0))
    relf = rel.astype(F32)
    res = []
    for hq in range(C_HEADS):
        kv = hq // C_GROUP
        qp = q_ref[:, (hq // 2) * LANES:(hq // 2 + 1) * LANES]
        qm = jnp.where((lane < HEAD_DIM) == (hq % 2 == 0), qp, jnp.zeros_like(qp))
        kt = jnp.concatenate([ktp_ref[0, kv * LANES:(kv + 1) * LANES, :],
                              ktc_ref[0, kv * LANES:(kv + 1) * LANES, :]], axis=1)
        vv = jnp.concatenate([vp_ref[:, kv * LANES:(kv + 1) * LANES],
                              vc_ref[:, kv * LANES:(kv + 1) * LANES]], axis=0)
        s = jnp.dot(qm, kt, preferred_element_type=F32)
        s = jnp.where(mask, s - slopes[hq] * relf, -jnp.inf)
        sink = sink_ref[hq]
        m = jnp.maximum(jnp.max(s, axis=-1, keepdims=True), sink)
        e = jnp.exp(s - m)
        den = jnp.sum(e, axis=-1, keepdims=True) + jnp.exp(sink - m)
        p = (e / den).astype(BF16)
        res.append(jnp.dot(p, vv, preferred_element_type=F32))
    for pr in range(C_HEADS // 2):
        o_ref[:, pr * LANES:(pr + 1) * LANES] = jnp.where(lane < HEAD_DIM, res[2 * pr], res[2 * pr + 1]).astype(BF16)


def _swa_attention(sinks, qc, ktc, vc2, *, bsz, seq, slopes):
    tq = KT
    nq = seq // tq
    t = bsz * seq
    return pl.pallas_call(
        functools.partial(_swa_kernel, slopes=slopes),
        grid=(bsz, nq),
        in_specs=[
            pl.BlockSpec(memory_space=pltpu.SMEM),
            pl.BlockSpec((tq, C_HEADS * HEAD_DIM), lambda b, n: (b * nq + n, 0)),
            pl.BlockSpec((1, C_KV_HEADS * LANES, KT), lambda b, n: (b * nq + jnp.maximum(n - 1, 0), 0, 0)),
            pl.BlockSpec((1, C_KV_HEADS * LANES, KT), lambda b, n: (b * nq + n, 0, 0)),
            pl.BlockSpec((tq, C_KV_HEADS * LANES), lambda b, n: (b * nq + jnp.maximum(n - 1, 0), 0)),
            pl.BlockSpec((tq, C_KV_HEADS * LANES), lambda b, n: (b * nq + n, 0)),
        ],
        out_specs=pl.BlockSpec((tq, C_HEADS * HEAD_DIM), lambda b, n: (b * nq + n, 0)),
        out_shape=jax.ShapeDtypeStruct((t, C_HEADS * HEAD_DIM), BF16),
        compiler_params=pltpu.CompilerParams(dimension_semantics=("arbitrary",) * 2,
                                             vmem_limit_bytes=VMEM_LIMIT),
        name="swa_attn",
    )(sinks, qc, ktc, ktc, vc2, vc2)


def _outproj_kernel(oa_ref, ob_ref, oc_ref, woa_ref, wob_ref, woc_ref, x_ref, mod_ref, g_ref, wq_ref, keys_ref,
                    x1_ref, h2_ref, st_ref):
    d = x_ref.shape[1]
    mix = (jnp.dot(oa_ref[...], woa_ref[...], preferred_element_type=F32)
           + jnp.dot(ob_ref[...], wob_ref[...], preferred_element_type=F32)
           + jnp.dot(oc_ref[...], woc_ref[...], preferred_element_type=F32))
    g1 = mod_ref[0, :, 2 * d:3 * d]
    sh2 = mod_ref[0, :, 3 * d:4 * d]
    sc2 = mod_ref[0, :, 4 * d:5 * d]
    x1 = x_ref[...] + g1 * mix
    x1_ref[...] = x1
    h2 = _rms_mod(x1, g_ref[...], sc2, sh2)
    h2_ref[...] = h2
    pq = jnp.dot(h2.astype(BF16), wq_ref[...], preferred_element_type=F32).astype(BF16)
    for hp in range(2 * PEER_HEADS):
        st_ref[hp] = lax.dot_general(keys_ref[hp], pq[:, hp * LANES:(hp + 1) * LANES],
                                     (((1,), (1,)), ((), ())), preferred_element_type=F32)


def _outproj(oa, ob, oc, woa, wob, woc, x2d, mod3, g, wq, keys, *, seq, tm=256):
    t, d = x2d.shape
    nt = t // tm
    nq = wq.shape[1]
    full = lambda a: pl.BlockSpec(a.shape, lambda i: (0,) * a.ndim)
    return pl.pallas_call(
        _outproj_kernel,
        grid=(nt,),
        in_specs=[
            pl.BlockSpec((tm, oa.shape[1]), lambda i: (i, 0)),
            pl.BlockSpec((tm, ob.shape[1]), lambda i: (i, 0)),
            pl.BlockSpec((tm, oc.shape[1]), lambda i: (i, 0)),
            full(woa), full(wob), full(woc),
            pl.BlockSpec((tm, d), lambda i: (i, 0)),
            pl.BlockSpec((1, 1, mod3.shape[2]), lambda i: ((i * tm) // seq, 0, 0)),
            pl.BlockSpec((1, d), lambda i: (0, 0)),
            full(wq), full(keys),
        ],
        out_specs=[pl.BlockSpec((tm, d), lambda i: (i, 0)),
                   pl.BlockSpec((tm, d), lambda i: (i, 0)),
                   pl.BlockSpec((2 * PEER_HEADS, PEER_NKEYS, tm), lambda i: (0, 0, i))],
        out_shape=[jax.ShapeDtypeStruct((t, d), F32), jax.ShapeDtypeStruct((t, d), F32),
                   jax.ShapeDtypeStruct((2 * PEER_HEADS, PEER_NKEYS, t), F32)],
        compiler_params=pltpu.CompilerParams(dimension_semantics=("arbitrary",),
                                             vmem_limit_bytes=VMEM_LIMIT),
        name="outproj_peerq",
    )(oa, ob, oc, woa, wob, woc, x2d, mod3, g, wq, keys)


_CAND_BLOCKS = ((0, 16),) + tuple((i, 8) for i in range(1, 8))
_CAND_ROWS = 16 + 7 * 8 + 8
_BIG_I = np.int32(2 ** 30)


def _topk_rows(curs, keys, val_refs, key_refs):
    def body(r, curs):
        out = []
        for cur, key, val_ref, key_ref in zip(curs, keys, val_refs, key_refs):
            m = jnp.max(cur, axis=0, keepdims=True)
            kmin = jnp.min(jnp.where(cur == m, key, _BIG_I), axis=0, keepdims=True)
            val_ref[pl.ds(r, 1), :] = m
            key_ref[pl.ds(r, 1), :] = kmin
            out.append(jnp.where(key == kmin, -jnp.inf, cur))
        return tuple(out)
    lax.fori_loop(0, PEER_TOPK, body, tuple(curs))


def _peer_topk_kernel(st_ref, off_ref, par_ref, g_ref, v1_ref, k1_ref, v2_ref, k2_ref, vt_ref, kt_ref, ei_ref, gg_ref):
    tl = st_ref.shape[2]
    row = lax.broadcasted_iota(I32, (PEER_NKEYS, tl), 0)
    jrow8 = lax.broadcasted_iota(I32, (8, tl), 0)
    jrow16 = lax.broadcasted_iota(I32, (16, tl), 0)
    for h0 in range(0, PEER_HEADS, 2):
        cand, ckey = [], []
        for s in range(2):
            h = h0 + s
            _topk_rows([st_ref[2 * h], st_ref[2 * h + 1]], [row, row],
                       [v1_ref.at[s], v2_ref.at[s]], [k1_ref.at[s], k2_ref.at[s]])
            sv1, si1, sv2, si2 = v1_ref[s], k1_ref[s], v2_ref[s], k2_ref[s]
            cands, keys = [], []
            for i, nj in _CAND_BLOCKS:
                jrow = jrow16 if nj == 16 else jrow8
                cands.append(sv1[i:i + 1] + sv2[0:nj])
                keys.append((i * PEER_TOPK + jrow) * (PEER_NKEYS * PEER_NKEYS)
                            + si1[i:i + 1] * PEER_NKEYS + si2[0:nj])
            cands.append(sv1[8:16] + sv2[0:1])
            keys.append((jrow8 + 8) * (PEER_TOPK * PEER_NKEYS * PEER_NKEYS) + si1[8:16] * PEER_NKEYS + si2[0:1])
            cand.append(jnp.concatenate(cands, axis=0))
            ckey.append(jnp.concatenate(keys, axis=0))
        _topk_rows(cand, ckey, [vt_ref.at[0], vt_ref.at[1]], [kt_ref.at[0], kt_ref.at[1]])
        for s in range(2):
            h = h0 + s
            top = vt_ref[s]
            e = jnp.exp(top - top[0:1])
            gg_ref[h * PEER_TOPK:(h + 1) * PEER_TOPK, :] = e / jnp.sum(e, axis=0, keepdims=True)
            ei_ref[h * PEER_TOPK:(h + 1) * PEER_TOPK, :] = kt_ref[s] & (PEER_NKEYS * PEER_NKEYS - 1)
    ei = ei_ref[...].T
    off_ref[...] = (ei >> 1) * SUB
    par_ref[...] = ei & 1
    g_ref[...] = gg_ref[...].T


def _peer_topk(st, *, tl=128):
    t = st.shape[2]
    hk = PEER_HEADS * PEER_TOPK
    return pl.pallas_call(
        _peer_topk_kernel,
        grid=(t // tl,),
        in_specs=[pl.BlockSpec((2 * PEER_HEADS, PEER_NKEYS, tl), lambda i: (0, 0, i))],
        out_specs=[pl.BlockSpec((tl, hk), lambda i: (i, 0))] * 3,
        out_shape=[jax.ShapeDtypeStruct((t, hk), I32), jax.ShapeDtypeStruct((t, hk), I32),
                   jax.ShapeDtypeStruct((t, hk), F32)],
        scratch_shapes=[pltpu.VMEM((2, PEER_TOPK, tl), F32), pltpu.VMEM((2, PEER_TOPK, tl), I32),
                        pltpu.VMEM((2, PEER_TOPK, tl), F32), pltpu.VMEM((2, PEER_TOPK, tl), I32),
                        pltpu.VMEM((2, PEER_TOPK, tl), F32), pltpu.VMEM((2, PEER_TOPK, tl), I32),
                        pltpu.VMEM((hk, tl), I32), pltpu.VMEM((hk, tl), F32)],
        compiler_params=pltpu.CompilerParams(dimension_semantics=("arbitrary",),
                                             vmem_limit_bytes=VMEM_LIMIT),
        name="peer_topk",
    )(st)


SUB = 8


def _erf(x):
    return lax.erf(x)


def _pack_expert_table(w):
    n, d = w.shape
    b = lax.bitcast_convert_type(w.astype(BF16), jnp.uint16).astype(jnp.uint32).reshape(n // 2, 2, d // LANES, LANES)
    return (b[:, 0] | (b[:, 1] << 16)).reshape(n // 2 * (d // LANES), LANES)


def _load_table_once(tab_hbm, tab, sem):
    @pl.when(pl.program_id(0) == 0)
    def _():
        cp = pltpu.make_async_copy(tab_hbm, tab, sem)
        cp.start()
        cp.wait()


def _pair_tile(tab, off):
    return tab[pl.ds(pl.multiple_of(off, SUB), SUB), :]


def _group_matrix(rows, cols):
    r = lax.broadcasted_iota(I32, (rows, cols), 0)
    c = lax.broadcasted_iota(I32, (rows, cols), 1)
    return (c // (cols // rows) == r).astype(BF16)


def _split2(x):
    hi = x.astype(BF16)
    return hi, (x - hi.astype(F32)).astype(BF16)


PAIR_ROWS = 2 * SUB


def _pair_rows(tab, off_ref, tt, hk):
    return jnp.concatenate([pltpu.bitcast(_pair_tile(tab, off_ref[tt * hk + k]), BF16) for k in range(hk)], axis=0)


def _own_sublane_mask(hk):
    srow = lax.broadcasted_iota(I32, (SUB, hk * PAIR_ROWS), 0)
    scol = lax.broadcasted_iota(I32, (SUB, hk * PAIR_ROWS), 1)
    return (scol % PAIR_ROWS) // 2 == srow


def _fold_matrix(hk):
    c = jnp.arange(hk * PAIR_ROWS)
    return jax.nn.one_hot((c % 2) * hk + c // PAIR_ROWS, 2 * hk, dtype=BF16)


def _peer_u_kernel(off_ref, par_ref, h_ref, g_ref, fold_ref, tab_hbm, w_ref, tab, sem, d_sc):
    tq, hk = g_ref.shape
    _load_table_once(tab_hbm, tab, sem)
    mine = _own_sublane_mask(hk)
    nt = (((1,), (1,)), ((), ()))
    for tt in range(tq):
        h_hi, h_lo = _split2(h_ref[tt])
        s = lax.dot_general(jnp.concatenate([h_hi, h_lo], axis=0), _pair_rows(tab, off_ref, tt, hk), nt,
                            preferred_element_type=F32)
        d = jnp.where(mine, s[0:SUB] + s[SUB:2 * SUB], 0.0)
        d_sc[tt:tt + 1, :] = jnp.sum(d, axis=0, keepdims=True)
    d_hi, d_lo = _split2(d_sc[...])
    a2 = jnp.dot(jnp.concatenate([d_hi, d_lo], axis=0), fold_ref[...], preferred_element_type=F32)
    a2 = a2[0:tq] + a2[tq:2 * tq]
    a = jnp.where(par_ref[...] == 0, a2[:, :hk], a2[:, hk:])
    w_ref[...] = g_ref[...] * (0.5 * a * (1.0 + _erf(a * (2.0 ** -0.5))))


def _peer_u(off_flat, par, h2, g, utab, *, tq=32):
    t, nj, _ = h2.shape
    hk = g.shape[1]
    fold = _fold_matrix(hk)
    return pl.pallas_call(
        _peer_u_kernel,
        grid=(t // tq,),
        in_specs=[
            pl.BlockSpec((tq * hk,), lambda i: (i,), memory_space=pltpu.SMEM),
            pl.BlockSpec((tq, hk), lambda i: (i, 0)),
            pl.BlockSpec((tq, nj, LANES), lambda i: (i, 0, 0)),
            pl.BlockSpec((tq, hk), lambda i: (i, 0)),
            pl.BlockSpec(fold.shape, lambda i: (0, 0)),
            pl.BlockSpec(memory_space=pl.ANY),
        ],
        out_specs=pl.BlockSpec((tq, hk), lambda i: (i, 0)),
        out_shape=jax.ShapeDtypeStruct((t, hk), F32),
        scratch_shapes=[pltpu.VMEM(utab.shape, jnp.uint32), pltpu.SemaphoreType.DMA(()),
                        pltpu.VMEM((tq, hk * PAIR_ROWS), F32)],
        compiler_params=pltpu.CompilerParams(dimension_semantics=("arbitrary",),
                                             vmem_limit_bytes=VMEM_LIMIT),
        name="peer_u",
    )(off_flat, par, h2, g, fold, utab)


def _peer_v_kernel(off_ref, par_ref, w_ref, x1_ref, g2_ref, fg_ref, tab_hbm, o_ref, tab, sem, wl_sc, *, final):
    tq, hk = w_ref.shape
    _load_table_once(tab_hbm, tab, sem)
    width = hk * PAIR_ROWS
    rep = _group_matrix(hk, width)
    w_hi, w_lo = _split2(w_ref[...])
    parl = jnp.dot(par_ref[...].astype(BF16), rep, preferred_element_type=F32)
    lane = lax.broadcasted_iota(I32, (tq, width), 1)
    wanted = (lane % 2).astype(F32) == parl
    wl_sc[0] = jnp.where(wanted, jnp.dot(w_hi, rep, preferred_element_type=F32), 0.0)
    wl_sc[1] = jnp.where(wanted, jnp.dot(w_lo, rep, preferred_element_type=F32), 0.0)
    mine = _own_sublane_mask(hk)
    for tt in range(tq):
        lhs = jnp.concatenate([jnp.where(mine, wl_sc[0, tt:tt + 1, :], 0.0),
                               jnp.where(mine, wl_sc[1, tt:tt + 1, :], 0.0)], axis=0).astype(BF16)
        r = jnp.dot(lhs, _pair_rows(tab, off_ref, tt, hk), preferred_element_type=F32)
        y = x1_ref[tt] + g2_ref[0] * (r[0:SUB] + r[SUB:2 * SUB])
        if final:
            ms = jnp.mean(jnp.mean(y * y, axis=-1, keepdims=True), axis=0, keepdims=True)
            y = y * lax.rsqrt(ms + EPS) * fg_ref[...]
        o_ref[tt] = y


def _peer_v(off_flat, par, w, x1, g2, final_g, vtab, *, seq, final, tq=32):
    t, nj, _ = x1.shape
    hk = w.shape[1]
    return pl.pallas_call(
        functools.partial(_peer_v_kernel, final=final),
        grid=(t // tq,),
        in_specs=[
            pl.BlockSpec((tq * hk,), lambda i: (i,), memory_space=pltpu.SMEM),
            pl.BlockSpec((tq, hk), lambda i: (i, 0)),
            pl.BlockSpec((tq, hk), lambda i: (i, 0)),
            pl.BlockSpec((tq, nj, LANES), lambda i: (i, 0, 0)),
            pl.BlockSpec((1, nj, LANES), lambda i: ((i * tq) // seq, 0, 0)),
            pl.BlockSpec((nj, LANES), lambda i: (0, 0)),
            pl.BlockSpec(memory_space=pl.ANY),
        ],
        out_specs=pl.BlockSpec((tq, nj, LANES), lambda i: (i, 0, 0)),
        out_shape=jax.ShapeDtypeStruct((t, nj, LANES), F32),
        scratch_shapes=[pltpu.VMEM(vtab.shape, jnp.uint32), pltpu.SemaphoreType.DMA(()),
                        pltpu.VMEM((2, tq, hk * PAIR_ROWS), F32)],
        compiler_params=pltpu.CompilerParams(dimension_semantics=("arbitrary",),
                                             vmem_limit_bytes=VMEM_LIMIT),
        name="peer_v",
    )(off_flat, par, w, x1, g2, final_g, vtab)


def kernel(x, c, norm1_g, norm2_g, w_ada, b_ada, w_in, w_out, lam_q1, lam_k1, lam_q2, lam_k2, subln_g, sinks,
           peer_wq, peer_keys, peer_u, peer_v, final_g):
    bsz, seq, d = x.shape
    depth = w_in.shape[0]
    t = bsz * seq
    slopes = _alibi_slopes()
    sl_c = [float(s) for s in slopes[:C_HEADS]]
    qa_bias = _slope_bias_col(slopes[C_HEADS:C_HEADS + A_HEADS] * np.float32(LOG2E))
    qb_bias = _slope_bias_col(slopes[C_HEADS + A_HEADS:] * np.float32(LOG2E))
    mods = _adaln_mods(c, w_ada, b_ada)
    x2d = x.reshape(t, d)
    av, bw = A_HEADS * HEAD_DIM, B_HEADS * HEAD_DIM
    for l in range(depth):
        lam_init = 0.8 - 0.6 * math.exp(-0.3 * l)
        mod3 = mods[l].reshape(bsz, 1, N_MOD * d)
        wn, wt = _prep_in_weights(w_in[l])
        ka, kb, qc, vc2, qta, vta, qtb, vtb, ktc = _inproj(
            x2d, mod3, norm1_g[l].reshape(1, d), wn, wt, qa_bias, qb_bias, seq=seq)
        lamv = jnp.stack([lam_q1[l], lam_k1[l], lam_q2[l], lam_k2[l]]).astype(F32)
        sg = subln_g[l].reshape(HEAD_DIM, 1).astype(F32)
        oa = _diff_attention(qta, ka, vta, lamv, sg, bsz=bsz, seq=seq, lam_init=lam_init)
        ob = _moba_attention(qtb, kb, vtb, bsz=bsz, seq=seq)
        oc = _swa_attention(sinks[l].astype(F32), qc, ktc, vc2, bsz=bsz, seq=seq, slopes=sl_c)
        wo = w_out[l].astype(BF16)
        keys = peer_keys[l].reshape(2 * PEER_HEADS, PEER_NKEYS, -1).astype(BF16)
        x1, h2, st = _outproj(oa, ob, oc, wo[:av], wo[av:av + bw], wo[av + bw:], x2d, mod3,
                              norm2_g[l].reshape(1, d), peer_wq[l].astype(BF16), keys, seq=seq)
        toff, par, g = _peer_topk(st)
        nj = d // LANES
        g2 = mods[l][:, 5 * d:6 * d].reshape(bsz, nj, LANES)
        toff = toff.reshape(-1)
        w = _peer_u(toff, par, h2.reshape(t, nj, LANES), g, _pack_expert_table(peer_u[l]))
        x2d = _peer_v(toff, par, w, x1.reshape(t, nj, LANES), g2, final_g.reshape(nj, LANES),
                      _pack_expert_table(peer_v[l]), seq=seq, final=(l == depth - 1)).reshape(t, d)
    return x2d.reshape(bsz, seq, d)
```

```python
import functools
import math

import numpy as np
import jax
import jax.numpy as jnp
from jax import lax
from jax.experimental import pallas as pl
from jax.experimental.pallas import tpu as pltpu

F32 = jnp.float32
BF16 = jnp.bfloat16
I32 = jnp.int32

D_MODEL = 1024
HEAD_DIM = 64
N_HEADS_TOTAL = 16
A_HEADS = 4
B_HEADS = 6
C_HEADS = 6
C_KV_HEADS = 2
C_GROUP = 3
A_QK_DIM = 32
MOBA_BLOCK = 256
MOBA_TOPK = 3
WINDOW = 128
ALIBI_MAX = 8.0
PEER_HEADS = 8
PEER_NKEYS = 128
PEER_TOPK = 16
N_MOD = 6
EPS = 1e-6

LANES = 128
KT = 256
AUG0 = HEAD_DIM
SEL0 = AUG0 + 6
MAX_BLOCKS = LANES - SEL0
NEG_BIG = -1e30
LOG2E = math.log2(math.e)
VMEM_LIMIT = 56 * 1024 * 1024


def _alibi_slopes():
    n = N_HEADS_TOTAL
    return (2.0 ** (-ALIBI_MAX * np.arange(1, n + 1, dtype=np.float32) / n)).astype(np.float32)


def _split3(v):
    v = np.float32(v)
    hi = np.float32(np.asarray(v).astype(jnp.bfloat16).astype(np.float32))
    r = np.float32(v - hi)
    mid = np.float32(np.asarray(r).astype(jnp.bfloat16).astype(np.float32))
    lo = np.float32(np.float32(r - mid))
    lo = np.float32(np.asarray(lo).astype(jnp.bfloat16).astype(np.float32))
    return hi, mid, lo


def _slope_bias_col(slopes):
    col = np.zeros((LANES * len(slopes), 1), np.float32)
    for h, s in enumerate(slopes):
        hi, mid, lo = _split3(s)
        col[h * LANES + AUG0:h * LANES + AUG0 + 6, 0] = [hi, mid, lo, hi, mid, lo]
    return jnp.asarray(col)


def _mod_kernel(c_ref, w_ref, b_ref, o_ref):
    c = c_ref[...]
    cs = c * (1.0 / (1.0 + jnp.exp(-c)))
    o_ref[0] = jnp.dot(cs, w_ref[0], preferred_element_type=F32) + b_ref[0]


def _adaln_mods(c, w_ada, b_ada):
    depth, d, n = w_ada.shape
    bsz = c.shape[0]
    rows = -(-bsz // 8) * 8
    cp = jnp.pad(c, ((0, rows - bsz), (0, 0)))
    tn = 1536
    out = pl.pallas_call(
        _mod_kernel,
        grid=(depth, n // tn),
        in_specs=[
            pl.BlockSpec((rows, d), lambda l, j: (0, 0)),
            pl.BlockSpec((1, d, tn), lambda l, j: (l, 0, j)),
            pl.BlockSpec((1, 1, tn), lambda l, j: (l, 0, j)),
        ],
        out_specs=pl.BlockSpec((1, rows, tn), lambda l, j: (l, 0, j)),
        out_shape=jax.ShapeDtypeStruct((depth, rows, n), F32),
        compiler_params=pltpu.CompilerParams(dimension_semantics=("arbitrary", "arbitrary"),
                                             vmem_limit_bytes=VMEM_LIMIT),
        name="adaln_mods",
    )(cp, w_ada, b_ada.reshape(depth, 1, n))
    return out[:, :bsz]


NN_WIDTHS = (A_HEADS * LANES, B_HEADS * LANES, C_HEADS * HEAD_DIM, 2 * C_KV_HEADS * HEAD_DIM)
NT_ROWS = (A_HEADS * LANES, A_HEADS * HEAD_DIM, B_HEADS * LANES, B_HEADS * HEAD_DIM,
           C_KV_HEADS * LANES)


def _prep_in_weights(w):
    d = w.shape[0]
    aq, ak, av = A_HEADS * 2 * A_QK_DIM, A_HEADS * 2 * A_QK_DIM, A_HEADS * HEAD_DIM
    bw = B_HEADS * HEAD_DIM
    cq, ckv = C_HEADS * HEAD_DIM, C_KV_HEADS * HEAD_DIM
    cuts = np.cumsum([aq, ak, av, bw, bw, bw, cq, ckv]).tolist()
    qa, ka, va, qb, kb, vb, qc, kc, vc = jnp.split(w, cuts, axis=-1)

    def pad_heads(m, nh, scale):
        m = (m * scale).reshape(d, nh, HEAD_DIM)
        return jnp.pad(m, ((0, 0), (0, 0), (0, LANES - HEAD_DIM))).reshape(d, nh * LANES)

    vc2 = vc.reshape(d, C_KV_HEADS, 1, HEAD_DIM)
    vc2 = jnp.broadcast_to(vc2, (d, C_KV_HEADS, 2, HEAD_DIM)).reshape(d, 2 * ckv)
    kc2 = jnp.broadcast_to(kc.reshape(d, C_KV_HEADS, 1, HEAD_DIM), (d, C_KV_HEADS, 2, HEAD_DIM)).reshape(d, 2 * ckv)
    wn = jnp.concatenate([pad_heads(ka, A_HEADS, 1.0), pad_heads(kb, B_HEADS, 1.0),
                          qc * (HEAD_DIM ** -0.5), vc2], axis=1)
    wt = jnp.concatenate([pad_heads(qa, A_HEADS, A_QK_DIM ** -0.5 * LOG2E), va,
                          pad_heads(qb, B_HEADS, HEAD_DIM ** -0.5 * LOG2E), vb, kc2], axis=1).T
    return wn.astype(BF16), wt.astype(BF16)


def _rms_mod(x, g, sc, sh):
    ms = jnp.mean(x * x, axis=-1, keepdims=True)
    return (x * lax.rsqrt(ms + EPS) * g) * (1.0 + sc) + sh


def _inproj_kernel(x_ref, mod_ref, g_ref, wn_ref, wt_ref, qab_ref, qbb_ref,
                   ka_ref, kb_ref, qc_ref, vc_ref, qta_ref, vta_ref, qtb_ref, vtb_ref, ktc_ref,
                   *, tm, seq):
    d = x_ref.shape[1]
    x = x_ref[...]
    sh = mod_ref[0, :, 0:d]
    sc = mod_ref[0, :, d:2 * d]
    h = _rms_mod(x, g_ref[...], sc, sh).astype(BF16)
    pn = jnp.dot(h, wn_ref[...], preferred_element_type=F32)
    pt = lax.dot_general(wt_ref[...], h, (((1,), (1,)), ((), ())),
                         preferred_element_type=F32)

    pos = (pl.program_id(0) * tm) % seq + lax.broadcasted_iota(I32, (tm, LANES), 0)
    col = lax.broadcasted_iota(I32, (tm, LANES), 1)
    blk_id = pos // KT
    p_hi = (blk_id * KT).astype(F32)
    p_lo = (pos - blk_id * KT).astype(F32)
    aug_a = jnp.where((col >= AUG0) & (col < AUG0 + 3), p_hi,
                      jnp.where((col >= AUG0 + 3) & (col < AUG0 + 6), p_lo, 0.0))
    aug_b = jnp.where((col >= SEL0) & (col - SEL0 == blk_id), 1.0, aug_a)
    for hh in range(A_HEADS):
        ka_ref[:, hh * LANES:(hh + 1) * LANES] = (pn[:, hh * LANES:(hh + 1) * LANES] + aug_a).astype(BF16)
    o = NN_WIDTHS[0]
    for hh in range(B_HEADS):
        kb_ref[:, hh * LANES:(hh + 1) * LANES] = (pn[:, o + hh * LANES:o + (hh + 1) * LANES] + aug_b).astype(BF16)
    o += NN_WIDTHS[1]
    qc_ref[...] = pn[:, o:o + NN_WIDTHS[2]].astype(BF16)
    o += NN_WIDTHS[2]
    vc_ref[...] = pn[:, o:o + NN_WIDTHS[3]].astype(BF16)

    r0 = 0
    for ref, b, nr in zip((qta_ref, vta_ref, qtb_ref, vtb_ref, ktc_ref),
                          (qab_ref, None, qbb_ref, None, None), NT_ROWS):
        blk = pt[r0:r0 + nr, :]
        if b is not None:
            blk = blk + b[...]
        blk = blk.astype(BF16)
        for cc in range(tm // KT):
            ref[cc] = blk[:, cc * KT:(cc + 1) * KT]
        r0 += nr


def _inproj(x2d, mod3, g, wn, wt, qa_bias, qb_bias, *, seq, tm=512):
    t, d = x2d.shape
    assert seq % tm == 0 and tm % KT == 0
    nt = t // tm
    nn_total = sum(NN_WIDTHS)
    row_specs = [pl.BlockSpec((tm, wd), lambda i: (i, 0)) for wd in NN_WIDTHS]
    kt_specs = [pl.BlockSpec((tm // KT, r, KT), lambda i: (i, 0, 0)) for r in NT_ROWS]
    out_shape = ([jax.ShapeDtypeStruct((t, wd), BF16) for wd in NN_WIDTHS]
                 + [jax.ShapeDtypeStruct((t // KT, r, KT), BF16) for r in NT_ROWS])
    return pl.pallas_call(
        functools.partial(_inproj_kernel, tm=tm, seq=seq),
        grid=(nt,),
        in_specs=[
            pl.BlockSpec((tm, d), lambda i: (i, 0)),
            pl.BlockSpec((1, 1, mod3.shape[2]), lambda i: ((i * tm) // seq, 0, 0)),
            pl.BlockSpec((1, d), lambda i: (0, 0)),
            pl.BlockSpec((d, nn_total), lambda i: (0, 0)),
            pl.BlockSpec((sum(NT_ROWS), d), lambda i: (0, 0)),
            pl.BlockSpec((NT_ROWS[0], 1), lambda i: (0, 0)),
            pl.BlockSpec((NT_ROWS[2], 1), lambda i: (0, 0)),
        ],
        out_specs=row_specs + kt_specs,
        out_shape=out_shape,
        compiler_params=pltpu.CompilerParams(dimension_semantics=("arbitrary",),
                                             vmem_limit_bytes=VMEM_LIMIT),
        name="inproj",
    )(x2d, mod3, g, wn, wt, qa_bias, qb_bias)


ACC_ROWS = HEAD_DIM + 16


def _softmax_pv(s, vt1, m_ref, acc_ref):
    m_prev = m_ref[...]
    m_new = jnp.maximum(m_prev, jnp.max(s, axis=0, keepdims=True))
    alpha = jnp.exp2(m_prev - m_new)
    p = jnp.exp2((s - m_new).astype(BF16))
    acc_ref[...] = alpha * acc_ref[...] + jnp.dot(vt1, p, preferred_element_type=F32)
    m_ref[...] = m_new


def _flash_init(m_ref, acc_ref):
    m_ref[...] = jnp.full(m_ref.shape, -jnp.inf, F32)
    acc_ref[...] = jnp.zeros(acc_ref.shape, F32)


def _flash_finish(acc_ref, hh):
    acc = acc_ref[hh]
    return acc[0:HEAD_DIM] / acc[HEAD_DIM:HEAD_DIM + 1]


PAST_TILES = 4


def _flash_causal_pair(k_ref, vt_ref, q_diag, q_past, qi, causal, m_ref, acc_ref, s_ref=None):
    _flash_init(m_ref, acc_ref)
    nkeys = PAST_TILES * KT
    nh = len(q_past)

    def vt1(kj, n, hh):
        rows = slice(hh * HEAD_DIM, (hh + 1) * HEAD_DIM)
        vt = vt_ref[kj, rows, :] if n == 1 else jnp.concatenate([vt_ref[kj + c, rows, :] for c in range(n)], axis=1)
        return jnp.concatenate([vt, jnp.ones((ACC_ROWS - HEAD_DIM, n * KT), BF16)], axis=0)

    def scores(kj, n, qs, hh):
        kk = k_ref[pl.ds(pl.multiple_of(kj * KT, KT), n * KT), hh * LANES:(hh + 1) * LANES]
        return jnp.dot(kk, qs[hh], preferred_element_type=F32)

    def step(kj, n, qs, mask=None):
        for hh in range(nh):
            s = scores(kj, n, qs, hh)
            if mask is not None:
                s = jnp.where(mask, s, -jnp.inf)
            _softmax_pv(s, vt1(kj, n, hh), m_ref.at[hh], acc_ref.at[hh])

    step(qi, 1, q_diag, causal)

    if s_ref is None:
        def body(j, carry):
            step(j * PAST_TILES, PAST_TILES, q_past)
            return carry

        lax.fori_loop(0, qi // PAST_TILES, body, 0)
        for r in range(PAST_TILES - 1):
            @pl.when(qi % PAST_TILES > r)
            def _():
                step(qi - 1 - r, 1, q_past)
        return

    ngroups = k_ref.shape[0] // nkeys

    def scores_into(slot, g):
        for hh in range(nh):
            s_ref[slot, hh] = scores(g * PAST_TILES, PAST_TILES, q_past, hh)

    scores_into(0, 0)

    def body(g, carry):
        scores_into((g + 1) % 2, jnp.minimum(g + 1, ngroups - 1))
        for hh in range(nh):
            _softmax_pv(s_ref[g % 2, hh], vt1(g * PAST_TILES, PAST_TILES, hh), m_ref.at[hh], acc_ref.at[hh])
        return carry

    lax.fori_loop(0, (qi + PAST_TILES - 1) // PAST_TILES, body, 0)


def _diff_kernel(qt_ref, k_ref, vt_ref, lamv_ref, sg_ref, o_ref, m_ref, acc_ref, *, lam_init):
    tq = qt_ref.shape[2]
    qi = pl.program_id(2)
    lv = lamv_ref[...]
    lam = (jnp.exp(jnp.sum(lv[0:1] * lv[1:2], axis=-1, keepdims=True))
           - jnp.exp(jnp.sum(lv[2:3] * lv[3:4], axis=-1, keepdims=True)) + lam_init)
    row = lax.broadcasted_iota(I32, (LANES, tq), 0)
    kr = lax.broadcasted_iota(I32, (KT, 2 * tq), 0)
    qc = lax.broadcasted_iota(I32, (KT, 2 * tq), 1)
    causal = kr <= jnp.where(qc >= tq, qc - tq, qc)
    nh = qt_ref.shape[1] // LANES
    qs = []
    for hh in range(nh):
        qt = qt_ref[0, hh * LANES:(hh + 1) * LANES, :]
        zero = jnp.zeros_like(qt)
        q1 = jnp.where((row < A_QK_DIM) | (row >= AUG0), qt, zero)
        q2 = jnp.where(row >= A_QK_DIM, qt, zero)
        qs.append(jnp.concatenate([q1, q2], axis=1))
    _flash_causal_pair(k_ref, vt_ref, qs, qs, qi, causal, m_ref, acc_ref)
    res = []
    for hh in range(nh):
        o = _flash_finish(acc_ref, hh)
        od = o[:, :tq] - lam * o[:, tq:]
        ms = jnp.mean(od * od, axis=0, keepdims=True)
        res.append(od * lax.rsqrt(ms + EPS) * sg_ref[...] * (1.0 - lam_init))
    o_ref[...] = jnp.concatenate(res, axis=0).T.astype(BF16)


def _diff_attention(qta, ka, vta, lamv, sg, *, bsz, seq, lam_init, nh=A_HEADS):
    tq = KT
    nq = seq // tq
    t = bsz * seq
    return pl.pallas_call(
        functools.partial(_diff_kernel, lam_init=lam_init),
        grid=(bsz, A_HEADS // nh, nq),
        in_specs=[
            pl.BlockSpec((1, nh * LANES, tq), lambda b, hp, qi: (b * nq + qi, hp, 0)),
            pl.BlockSpec((seq, nh * LANES), lambda b, hp, qi: (b, hp)),
            pl.BlockSpec((nq, nh * HEAD_DIM, KT), lambda b, hp, qi: (b, hp, 0)),
            pl.BlockSpec((4, A_QK_DIM), lambda b, hp, qi: (0, 0)),
            pl.BlockSpec((HEAD_DIM, 1), lambda b, hp, qi: (0, 0)),
        ],
        out_specs=pl.BlockSpec((tq, nh * HEAD_DIM), lambda b, hp, qi: (b * nq + qi, hp)),
        out_shape=jax.ShapeDtypeStruct((t, A_HEADS * HEAD_DIM), BF16),
        scratch_shapes=[pltpu.VMEM((nh, 1, 2 * tq), F32), pltpu.VMEM((nh, ACC_ROWS, 2 * tq), F32)],
        compiler_params=pltpu.CompilerParams(dimension_semantics=("arbitrary",) * 3,
                                             vmem_limit_bytes=VMEM_LIMIT),
        name="diff_attn",
    )(qta, ka, vta, lamv, sg)


def _moba_kernel(qt_ref, k_ref, vt_ref, o_ref, m_ref, acc_ref, km_ref, s_ref):
    tq = qt_ref.shape[2]
    nh = qt_ref.shape[1] // LANES
    nb = vt_ref.shape[0]
    qi = pl.program_id(2)

    @pl.when(qi == 0)
    def _():
        lane1 = lax.broadcasted_iota(I32, (1, LANES), 1)
        for hh in range(nh):
            km_ref[hh] = jnp.zeros((LANES, LANES), F32)

            def put_block(j, carry):
                blk = k_ref[pl.ds(pl.multiple_of(j * KT, KT), KT), hh * LANES:(hh + 1) * LANES].astype(F32)
                mean = jnp.sum(blk, axis=0, keepdims=True) * (1.0 / KT)
                km_ref[hh, pl.ds(SEL0 + j, 1), :] = jnp.where(lane1 < HEAD_DIM, mean, 0.0)
                return carry
            lax.fori_loop(0, nb, put_block, 0)

    row = lax.broadcasted_iota(I32, (LANES, tq), 0)
    kr = lax.broadcasted_iota(I32, (KT, tq), 0)
    qc = lax.broadcasted_iota(I32, (KT, tq), 1)
    causal = kr <= qc
    in_sel = (row >= SEL0) & (row < SEL0 + MAX_BLOCKS)
    q_diag, q_past = [], []
    for hh in range(nh):
        qt = qt_ref[0, hh * LANES:(hh + 1) * LANES, :]
        km = km_ref[hh]
        km_hi = km.astype(BF16)
        km_lo = (km - km_hi.astype(F32)).astype(BF16)
        gate = (jnp.dot(km_hi, qt, preferred_element_type=F32)
                + jnp.dot(km_lo, qt, preferred_element_type=F32))
        cur = jnp.where((row >= SEL0) & (row < SEL0 + qi), gate, -jnp.inf)
        sel = jnp.zeros((LANES, tq), jnp.bool_)
        for _ in range(MOBA_TOPK):
            mx = jnp.max(cur, axis=0, keepdims=True)
            first = jnp.min(jnp.where(cur == mx, row, 4 * LANES), axis=0, keepdims=True)
            pick = (row == first) & (mx > -jnp.inf)
            sel = sel | pick
            cur = jnp.where(pick, -jnp.inf, cur)
        q_diag.append(jnp.where(in_sel, jnp.zeros_like(qt), qt))
        q_past.append(jnp.where(in_sel, jnp.where(sel, 0.0, NEG_BIG).astype(BF16), qt))
    _flash_causal_pair(k_ref, vt_ref, q_diag, q_past, qi, causal, m_ref, acc_ref, s_ref)
    res = [_flash_finish(acc_ref, hh) for hh in range(nh)]
    o_ref[...] = jnp.concatenate(res, axis=0).T.astype(BF16)


def _moba_attention(qtb, kb, vtb, *, bsz, seq, nh=B_HEADS):
    tq = KT
    nq = seq // tq
    assert nq <= MAX_BLOCKS and seq % (PAST_TILES * KT) == 0
    t = bsz * seq
    return pl.pallas_call(
        _moba_kernel,
        grid=(bsz, B_HEADS // nh, nq),
        in_specs=[
            pl.BlockSpec((1, nh * LANES, tq), lambda b, hp, qi: (b * nq + qi, hp, 0)),
            pl.BlockSpec((seq, nh * LANES), lambda b, hp, qi: (b, hp), pipeline_mode=pl.Buffered(1)),
            pl.BlockSpec((nq, nh * HEAD_DIM, KT), lambda b, hp, qi: (b, hp, 0), pipeline_mode=pl.Buffered(1)),
        ],
        out_specs=pl.BlockSpec((tq, nh * HEAD_DIM), lambda b, hp, qi: (b * nq + qi, hp)),
        out_shape=jax.ShapeDtypeStruct((t, B_HEADS * HEAD_DIM), BF16),
        scratch_shapes=[pltpu.VMEM((nh, 1, tq), F32), pltpu.VMEM((nh, ACC_ROWS, tq), F32),
                        pltpu.VMEM((nh, LANES, LANES), F32), pltpu.VMEM((2, nh, PAST_TILES * KT, tq), F32)],
        compiler_params=pltpu.CompilerParams(dimension_semantics=("arbitrary",) * 3,
                                             vmem_limit_bytes=VMEM_LIMIT),
        name="moba_attn",
    )(qtb, kb, vtb)


def _swa_kernel(sink_ref, q_ref, ktp_ref, ktc_ref, vp_ref, vc_ref, o_ref, *, slopes):
    tq = q_ref.shape[0]
    n = pl.program_id(1)
    lane = lax.broadcasted_iota(I32, (tq, LANES), 1)
    r2 = lax.broadcasted_iota(I32, (tq, 2 * KT), 0)
    c2 = lax.broadcasted_iota(I32, (tq, 2 * KT), 1)
    rel = r2 + KT - c2
    mask = (rel >= 0) & (rel < WINDOW) & ((c2 >= KT) | (n > 0))
    relf = rel.astype(F32)
    res = []
    for hq in range(C_HEADS):
        kv = hq // C_GROUP
        qp = q_ref[:, (hq // 2) * LANES:(hq // 2 + 1) * LANES]
        qm = jnp.where((lane < HEAD_DIM) == (hq % 2 == 0), qp, jnp.zeros_like(qp))
        kt = jnp.concatenate([ktp_ref[0, kv * LANES:(kv + 1) * LANES, :],
                              ktc_ref[0, kv * LANES:(kv + 1) * LANES, :]], axis=1)
        vv = jnp.concatenate([vp_ref[:, kv * LANES:(kv + 1) * LANES],
                              vc_ref[:, kv * LANES:(kv + 1) * LANES]], axis=0)
        s = jnp.dot(qm, kt, preferred_element_type=F32)
        s = jnp.where(mask, s - slopes[hq] * relf, -jnp.inf)
        sink = sink_ref[hq]
        m = jnp.maximum(jnp.max(s, axis=-1, keepdims=True), sink)
        e = jnp.exp(s - m)
        den = jnp.sum(e, axis=-1, keepdims=True) + jnp.exp(sink - m)
        p = (e / den).astype(BF16)
        res.append(jnp.dot(p, vv, preferred_element_type=F32))
    for pr in range(C_HEADS // 2):
        o_ref[:, pr * LANES:(pr + 1) * LANES] = jnp.where(lane < HEAD_DIM, res[2 * pr], res[2 * pr + 1]).astype(BF16)


def _swa_attention(sinks, qc, ktc, vc2, *, bsz, seq, slopes):
    tq = KT
    nq = seq // tq
    t = bsz * seq
    return pl.pallas_call(
        functools.partial(_swa_kernel, slopes=slopes),
        grid=(bsz, nq),
        in_specs=[
            pl.BlockSpec(memory_space=pltpu.SMEM),
            pl.BlockSpec((tq, C_HEADS * HEAD_DIM), lambda b, n: (b * nq + n, 0)),
            pl.BlockSpec((1, C_KV_HEADS * LANES, KT), lambda b, n: (b * nq + jnp.maximum(n - 1, 0), 0, 0)),
            pl.BlockSpec((1, C_KV_HEADS * LANES, KT), lambda b, n: (b * nq + n, 0, 0)),
            pl.BlockSpec((tq, C_KV_HEADS * LANES), lambda b, n: (b * nq + jnp.maximum(n - 1, 0), 0)),
            pl.BlockSpec((tq, C_KV_HEADS * LANES), lambda b, n: (b * nq + n, 0)),
        ],
        out_specs=pl.BlockSpec((tq, C_HEADS * HEAD_DIM), lambda b, n: (b * nq + n, 0)),
        out_shape=jax.ShapeDtypeStruct((t, C_HEADS * HEAD_DIM), BF16),
        compiler_params=pltpu.CompilerParams(dimension_semantics=("arbitrary",) * 2,
                                             vmem_limit_bytes=VMEM_LIMIT),
        name="swa_attn",
    )(sinks, qc, ktc, ktc, vc2, vc2)


def _outproj_kernel(oa_ref, ob_ref, oc_ref, woa_ref, wob_ref, woc_ref, x_ref, mod_ref, g_ref, wq_ref, keys_ref,
                    x1_ref, h2_ref, st_ref):
    d = x_ref.shape[1]
    mix = (jnp.dot(oa_ref[...], woa_ref[...], preferred_element_type=F32)
           + jnp.dot(ob_ref[...], wob_ref[...], preferred_element_type=F32)
           + jnp.dot(oc_ref[...], woc_ref[...], preferred_element_type=F32))
    g1 = mod_ref[0, :, 2 * d:3 * d]
    sh2 = mod_ref[0, :, 3 * d:4 * d]
    sc2 = mod_ref[0, :, 4 * d:5 * d]
    x1 = x_ref[...] + g1 * mix
    x1_ref[...] = x1
    h2 = _rms_mod(x1, g_ref[...], sc2, sh2)
    h2_ref[...] = h2
    pq = jnp.dot(h2.astype(BF16), wq_ref[...], preferred_element_type=F32).astype(BF16)
    for hp in range(2 * PEER_HEADS):
        st_ref[hp] = lax.dot_general(keys_ref[hp], pq[:, hp * LANES:(hp + 1) * LANES],
                                     (((1,), (1,)), ((), ())), preferred_element_type=F32)


def _outproj(oa, ob, oc, woa, wob, woc, x2d, mod3, g, wq, keys, *, seq, tm=256):
    t, d = x2d.shape
    nt = t // tm
    nq = wq.shape[1]
    full = lambda a: pl.BlockSpec(a.shape, lambda i: (0,) * a.ndim)
    return pl.pallas_call(
        _outproj_kernel,
        grid=(nt,),
        in_specs=[
            pl.BlockSpec((tm, oa.shape[1]), lambda i: (i, 0)),
            pl.BlockSpec((tm, ob.shape[1]), lambda i: (i, 0)),
            pl.BlockSpec((tm, oc.shape[1]), lambda i: (i, 0)),
            full(woa), full(wob), full(woc),
            pl.BlockSpec((tm, d), lambda i: (i, 0)),
            pl.BlockSpec((1, 1, mod3.shape[2]), lambda i: ((i * tm) // seq, 0, 0)),
            pl.BlockSpec((1, d), lambda i: (0, 0)),
            full(wq), full(keys),
        ],
        out_specs=[pl.BlockSpec((tm, d), lambda i: (i, 0)),
                   pl.BlockSpec((tm, d), lambda i: (i, 0)),
                   pl.BlockSpec((2 * PEER_HEADS, PEER_NKEYS, tm), lambda i: (0, 0, i))],
        out_shape=[jax.ShapeDtypeStruct((t, d), F32), jax.ShapeDtypeStruct((t, d), F32),
                   jax.ShapeDtypeStruct((2 * PEER_HEADS, PEER_NKEYS, t), F32)],
        compiler_params=pltpu.CompilerParams(dimension_semantics=("arbitrary",),
                                             vmem_limit_bytes=VMEM_LIMIT),
        name="outproj_peerq",
    )(oa, ob, oc, woa, wob, woc, x2d, mod3, g, wq, keys)


_CAND_BLOCKS = ((0, 16),) + tuple((i, 8) for i in range(1, 8))
_CAND_ROWS = 16 + 7 * 8 + 8
_BIG_I = np.int32(2 ** 30)


def _topk_rows(curs, keys, val_refs, key_refs):
    def body(r, curs):
        out = []
        for cur, key, val_ref, key_ref in zip(curs, keys, val_refs, key_refs):
            m = jnp.max(cur, axis=0, keepdims=True)
            kmin = jnp.min(jnp.where(cur == m, key, _BIG_I), axis=0, keepdims=True)
            val_ref[pl.ds(r, 1), :] = m
            key_ref[pl.ds(r, 1), :] = kmin
            out.append(jnp.where(key == kmin, -jnp.inf, cur))
        return tuple(out)
    lax.fori_loop(0, PEER_TOPK, body, tuple(curs))


def _peer_topk_kernel(st_ref, off_ref, par_ref, g_ref, v1_ref, k1_ref, v2_ref, k2_ref, vt_ref, kt_ref, ei_ref, gg_ref):
    tl = st_ref.shape[2]
    row = lax.broadcasted_iota(I32, (PEER_NKEYS, tl), 0)
    jrow8 = lax.broadcasted_iota(I32, (8, tl), 0)
    jrow16 = lax.broadcasted_iota(I32, (16, tl), 0)
    for h0 in range(0, PEER_HEADS, 2):
        cand, ckey = [], []
        for s in range(2):
            h = h0 + s
            _topk_rows([st_ref[2 * h], st_ref[2 * h + 1]], [row, row],
                       [v1_ref.at[s], v2_ref.at[s]], [k1_ref.at[s], k2_ref.at[s]])
            sv1, si1, sv2, si2 = v1_ref[s], k1_ref[s], v2_ref[s], k2_ref[s]
            cands, keys = [], []
            for i, nj in _CAND_BLOCKS:
                jrow = jrow16 if nj == 16 else jrow8
                cands.append(sv1[i:i + 1] + sv2[0:nj])
                keys.append((i * PEER_TOPK + jrow) * (PEER_NKEYS * PEER_NKEYS)
                            + si1[i:i + 1] * PEER_NKEYS + si2[0:nj])
            cands.append(sv1[8:16] + sv2[0:1])
            keys.append((jrow8 + 8) * (PEER_TOPK * PEER_NKEYS * PEER_NKEYS) + si1[8:16] * PEER_NKEYS + si2[0:1])
            cand.append(jnp.concatenate(cands, axis=0))
            ckey.append(jnp.concatenate(keys, axis=0))
        _topk_rows(cand, ckey, [vt_ref.at[0], vt_ref.at[1]], [kt_ref.at[0], kt_ref.at[1]])
        for s in range(2):
            h = h0 + s
            top = vt_ref[s]
            e = jnp.exp(top - top[0:1])
            gg_ref[h * PEER_TOPK:(h + 1) * PEER_TOPK, :] = e / jnp.sum(e, axis=0, keepdims=True)
            ei_ref[h * PEER_TOPK:(h + 1) * PEER_TOPK, :] = kt_ref[s] & (PEER_NKEYS * PEER_NKEYS - 1)
    ei = ei_ref[...].T
    off_ref[...] = (ei >> 1) * SUB
    par_ref[...] = ei & 1
    g_ref[...] = gg_ref[...].T


def _peer_topk(st, *, tl=128):
    t = st.shape[2]
    hk = PEER_HEADS * PEER_TOPK
    return pl.pallas_call(
        _peer_topk_kernel,
        grid=(t // tl,),
        in_specs=[pl.BlockSpec((2 * PEER_HEADS, PEER_NKEYS, tl), lambda i: (0, 0, i))],
        out_specs=[pl.BlockSpec((tl, hk), lambda i: (i, 0))] * 3,
        out_shape=[jax.ShapeDtypeStruct((t, hk), I32), jax.ShapeDtypeStruct((t, hk), I32),
                   jax.ShapeDtypeStruct((t, hk), F32)],
        scratch_shapes=[pltpu.VMEM((2, PEER_TOPK, tl), F32), pltpu.VMEM((2, PEER_TOPK, tl), I32),
                        pltpu.VMEM((2, PEER_TOPK, tl), F32), pltpu.VMEM((2, PEER_TOPK, tl), I32),
                        pltpu.VMEM((2, PEER_TOPK, tl), F32), pltpu.VMEM((2, PEER_TOPK, tl), I32),
                        pltpu.VMEM((hk, tl), I32), pltpu.VMEM((hk, tl), F32)],
        compiler_params=pltpu.CompilerParams(dimension_semantics=("arbitrary",),
                                             vmem_limit_bytes=VMEM_LIMIT),
        name="peer_topk",
    )(st)


SUB = 8


def _erf(x):
    return lax.erf(x)


def _pack_expert_table(w):
    n, d = w.shape
    b = lax.bitcast_convert_type(w.astype(BF16), jnp.uint16).astype(jnp.uint32).reshape(n // 2, 2, d // LANES, LANES)
    return (b[:, 0] | (b[:, 1] << 16)).reshape(n // 2 * (d // LANES), LANES)


def _load_table_once(tab_hbm, tab, sem):
    @pl.when(pl.program_id(0) == 0)
    def _():
        cp = pltpu.make_async_copy(tab_hbm, tab, sem)
        cp.start()
        cp.wait()


def _pair_tile(tab, off):
    return tab[pl.ds(pl.multiple_of(off, SUB), SUB), :]


def _group_matrix(rows, cols):
    r = lax.broadcasted_iota(I32, (rows, cols), 0)
    c = lax.broadcasted_iota(I32, (rows, cols), 1)
    return (c // (cols // rows) == r).astype(BF16)


def _split2(x):
    hi = x.astype(BF16)
    return hi, (x - hi.astype(F32)).astype(BF16)


PAIR_ROWS = 2 * SUB


def _pair_rows(tab, off_ref, tt, hk):
    return jnp.concatenate([pltpu.bitcast(_pair_tile(tab, off_ref[tt * hk + k]), BF16) for k in range(hk)], axis=0)


def _own_sublane_mask(hk):
    srow = lax.broadcasted_iota(I32, (SUB, hk * PAIR_ROWS), 0)
    scol = lax.broadcasted_iota(I32, (SUB, hk * PAIR_ROWS), 1)
    return (scol % PAIR_ROWS) // 2 == srow


def _fold_matrix(hk):
    c = jnp.arange(hk * PAIR_ROWS)
    return jax.nn.one_hot((c % 2) * hk + c // PAIR_ROWS, 2 * hk, dtype=BF16)


def _peer_u_kernel(off_ref, par_ref, h_ref, g_ref, fold_ref, tab_hbm, w_ref, tab, sem, d_sc):
    tq, hk = g_ref.shape
    _load_table_once(tab_hbm, tab, sem)
    mine = _own_sublane_mask(hk)
    nt = (((1,), (1,)), ((), ()))
    for tt in range(tq):
        h_hi, h_lo = _split2(h_ref[tt])
        s = lax.dot_general(jnp.concatenate([h_hi, h_lo], axis=0), _pair_rows(tab, off_ref, tt, hk), nt,
                            preferred_element_type=F32)
        d = jnp.where(mine, s[0:SUB] + s[SUB:2 * SUB], 0.0)
        d_sc[tt:tt + 1, :] = jnp.sum(d, axis=0, keepdims=True)
    d_hi, d_lo = _split2(d_sc[...])
    a2 = jnp.dot(jnp.concatenate([d_hi, d_lo], axis=0), fold_ref[...], preferred_element_type=F32)
    a2 = a2[0:tq] + a2[tq:2 * tq]
    a = jnp.where(par_ref[...] == 0, a2[:, :hk], a2[:, hk:])
    w_ref[...] = g_ref[...] * (0.5 * a * (1.0 + _erf(a * (2.0 ** -0.5))))


def _peer_u(off_flat, par, h2, g, utab, *, tq=64):
    t, nj, _ = h2.shape
    hk = g.shape[1]
    fold = _fold_matrix(hk)
    return pl.pallas_call(
        _peer_u_kernel,
        grid=(t // tq,),
        in_specs=[
            pl.BlockSpec((tq * hk,), lambda i: (i,), memory_space=pltpu.SMEM),
            pl.BlockSpec((tq, hk), lambda i: (i, 0)),
            pl.BlockSpec((tq, nj, LANES), lambda i: (i, 0, 0)),
            pl.BlockSpec((tq, hk), lambda i: (i, 0)),
            pl.BlockSpec(fold.shape, lambda i: (0, 0)),
            pl.BlockSpec(memory_space=pl.ANY),
        ],
        out_specs=pl.BlockSpec((tq, hk), lambda i: (i, 0)),
        out_shape=jax.ShapeDtypeStruct((t, hk), F32),
        scratch_shapes=[pltpu.VMEM(utab.shape, jnp.uint32), pltpu.SemaphoreType.DMA(()),
                        pltpu.VMEM((tq, hk * PAIR_ROWS), F32)],
        compiler_params=pltpu.CompilerParams(dimension_semantics=("arbitrary",),
                                             vmem_limit_bytes=VMEM_LIMIT),
        name="peer_u",
    )(off_flat, par, h2, g, fold, utab)


def _peer_v_kernel(off_ref, par_ref, w_ref, x1_ref, g2_ref, fg_ref, tab_hbm, o_ref, tab, sem, wl_sc, *, final):
    tq, hk = w_ref.shape
    _load_table_once(tab_hbm, tab, sem)
    width = hk * PAIR_ROWS
    rep = _group_matrix(hk, width)
    w_hi, w_lo = _split2(w_ref[...])
    parl = jnp.dot(par_ref[...].astype(BF16), rep, preferred_element_type=F32)
    lane = lax.broadcasted_iota(I32, (tq, width), 1)
    wanted = (lane % 2).astype(F32) == parl
    wl_sc[0] = jnp.where(wanted, jnp.dot(w_hi, rep, preferred_element_type=F32), 0.0)
    wl_sc[1] = jnp.where(wanted, jnp.dot(w_lo, rep, preferred_element_type=F32), 0.0)
    mine = _own_sublane_mask(hk)
    for tt in range(tq):
        lhs = jnp.concatenate([jnp.where(mine, wl_sc[0, tt:tt + 1, :], 0.0),
                               jnp.where(mine, wl_sc[1, tt:tt + 1, :], 0.0)], axis=0).astype(BF16)
        r = jnp.dot(lhs, _pair_rows(tab, off_ref, tt, hk), preferred_element_type=F32)
        y = x1_ref[tt] + g2_ref[0] * (r[0:SUB] + r[SUB:2 * SUB])
        if final:
            ms = jnp.mean(jnp.mean(y * y, axis=-1, keepdims=True), axis=0, keepdims=True)
            y = y * lax.rsqrt(ms + EPS) * fg_ref[...]
        o_ref[tt] = y


def _peer_v(off_flat, par, w, x1, g2, final_g, vtab, *, seq, final, tq=64):
    t, nj, _ = x1.shape
    hk = w.shape[1]
    return pl.pallas_call(
        functools.partial(_peer_v_kernel, final=final),
        grid=(t // tq,),
        in_specs=[
            pl.BlockSpec((tq * hk,), lambda i: (i,), memory_space=pltpu.SMEM),
            pl.BlockSpec((tq, hk), lambda i: (i, 0)),
            pl.BlockSpec((tq, hk), lambda i: (i, 0)),
            pl.BlockSpec((tq, nj, LANES), lambda i: (i, 0, 0)),
            pl.BlockSpec((1, nj, LANES), lambda i: ((i * tq) // seq, 0, 0)),
            pl.BlockSpec((nj, LANES), lambda i: (0, 0)),
            pl.BlockSpec(memory_space=pl.ANY),
        ],
        out_specs=pl.BlockSpec((tq, nj, LANES), lambda i: (i, 0, 0)),
        out_shape=jax.ShapeDtypeStruct((t, nj, LANES), F32),
        scratch_shapes=[pltpu.VMEM(vtab.shape, jnp.uint32), pltpu.SemaphoreType.DMA(()),
                        pltpu.VMEM((2, tq, hk * PAIR_ROWS), F32)],
        compiler_params=pltpu.CompilerParams(dimension_semantics=("arbitrary",),
                                             vmem_limit_bytes=VMEM_LIMIT),
        name="peer_v",
    )(off_flat, par, w, x1, g2, final_g, vtab)


def kernel(x, c, norm1_g, norm2_g, w_ada, b_ada, w_in, w_out, lam_q1, lam_k1, lam_q2, lam_k2, subln_g, sinks,
           peer_wq, peer_keys, peer_u, peer_v, final_g):
    bsz, seq, d = x.shape
    depth = w_in.shape[0]
    t = bsz * seq
    slopes = _alibi_slopes()
    sl_c = [float(s) for s in slopes[:C_HEADS]]
    qa_bias = _slope_bias_col(slopes[C_HEADS:C_HEADS + A_HEADS] * np.float32(LOG2E))
    qb_bias = _slope_bias_col(slopes[C_HEADS + A_HEADS:] * np.float32(LOG2E))
    mods = _adaln_mods(c, w_ada, b_ada)
    x2d = x.reshape(t, d)
    av, bw = A_HEADS * HEAD_DIM, B_HEADS * HEAD_DIM
    for l in range(depth):
        lam_init = 0.8 - 0.6 * math.exp(-0.3 * l)
        mod3 = mods[l].reshape(bsz, 1, N_MOD * d)
        wn, wt = _prep_in_weights(w_in[l])
        ka, kb, qc, vc2, qta, vta, qtb, vtb, ktc = _inproj(
            x2d, mod3, norm1_g[l].reshape(1, d), wn, wt, qa_bias, qb_bias, seq=seq)
        lamv = jnp.stack([lam_q1[l], lam_k1[l], lam_q2[l], lam_k2[l]]).astype(F32)
        sg = subln_g[l].reshape(HEAD_DIM, 1).astype(F32)
        oa = _diff_attention(qta, ka, vta, lamv, sg, bsz=bsz, seq=seq, lam_init=lam_init)
        ob = _moba_attention(qtb, kb, vtb, bsz=bsz, seq=seq)
        oc = _swa_attention(sinks[l].astype(F32), qc, ktc, vc2, bsz=bsz, seq=seq, slopes=sl_c)
        wo = w_out[l].astype(BF16)
        keys = peer_keys[l].reshape(2 * PEER_HEADS, PEER_NKEYS, -1).astype(BF16)
        x1, h2, st = _outproj(oa, ob, oc, wo[:av], wo[av:av + bw], wo[av + bw:], x2d, mod3,
                              norm2_g[l].reshape(1, d), peer_wq[l].astype(BF16), keys, seq=seq)
        toff, par, g = _peer_topk(st)
        nj = d // LANES
        g2 = mods[l][:, 5 * d:6 * d].reshape(bsz, nj, LANES)
        toff = toff.reshape(-1)
        w = _peer_u(toff, par, h2.reshape(t, nj, LANES), g, _pack_expert_table(peer_u[l]))
        x2d = _peer_v(toff, par, w, x1.reshape(t, nj, LANES), g2, final_g.reshape(nj, LANES),
                      _pack_expert_table(peer_v[l]), seq=seq, final=(l == depth - 1)).reshape(t, d)
    return x2d.reshape(bsz, seq, d)
```

```python
import functools
import math

import numpy as np
import jax
import jax.numpy as jnp
from jax import lax
from jax.experimental import pallas as pl
from jax.experimental.pallas import tpu as pltpu

F32 = jnp.float32
BF16 = jnp.bfloat16
I32 = jnp.int32

D_MODEL = 1024
HEAD_DIM = 64
N_HEADS_TOTAL = 16
A_HEADS = 4
B_HEADS = 6
C_HEADS = 6
C_KV_HEADS = 2
C_GROUP = 3
A_QK_DIM = 32
MOBA_BLOCK = 256
MOBA_TOPK = 3
WINDOW = 128
ALIBI_MAX = 8.0
PEER_HEADS = 8
PEER_NKEYS = 128
PEER_TOPK = 16
N_MOD = 6
EPS = 1e-6

LANES = 128
KT = 256
AUG0 = HEAD_DIM
SEL0 = AUG0 + 6
MAX_BLOCKS = LANES - SEL0
NEG_BIG = -1e30
LOG2E = math.log2(math.e)
VMEM_LIMIT = 56 * 1024 * 1024


def _alibi_slopes():
    n = N_HEADS_TOTAL
    return (2.0 ** (-ALIBI_MAX * np.arange(1, n + 1, dtype=np.float32) / n)).astype(np.float32)


def _split3(v):
    v = np.float32(v)
    hi = np.float32(np.asarray(v).astype(jnp.bfloat16).astype(np.float32))
    r = np.float32(v - hi)
    mid = np.float32(np.asarray(r).astype(jnp.bfloat16).astype(np.float32))
    lo = np.float32(np.float32(r - mid))
    lo = np.float32(np.asarray(lo).astype(jnp.bfloat16).astype(np.float32))
    return hi, mid, lo


def _slope_bias_col(slopes):
    col = np.zeros((LANES * len(slopes), 1), np.float32)
    for h, s in enumerate(slopes):
        hi, mid, lo = _split3(s)
        col[h * LANES + AUG0:h * LANES + AUG0 + 6, 0] = [hi, mid, lo, hi, mid, lo]
    return jnp.asarray(col)


def _mod_kernel(c_ref, w_ref, b_ref, o_ref):
    c = c_ref[...]
    cs = c * (1.0 / (1.0 + jnp.exp(-c)))
    o_ref[0] = jnp.dot(cs, w_ref[0], preferred_element_type=F32) + b_ref[0]


def _adaln_mods(c, w_ada, b_ada):
    depth, d, n = w_ada.shape
    bsz = c.shape[0]
    rows = -(-bsz // 8) * 8
    cp = jnp.pad(c, ((0, rows - bsz), (0, 0)))
    tn = 1536
    out = pl.pallas_call(
        _mod_kernel,
        grid=(depth, n // tn),
        in_specs=[
            pl.BlockSpec((rows, d), lambda l, j: (0, 0)),
            pl.BlockSpec((1, d, tn), lambda l, j: (l, 0, j)),
            pl.BlockSpec((1, 1, tn), lambda l, j: (l, 0, j)),
        ],
        out_specs=pl.BlockSpec((1, rows, tn), lambda l, j: (l, 0, j)),
        out_shape=jax.ShapeDtypeStruct((depth, rows, n), F32),
        compiler_params=pltpu.CompilerParams(dimension_semantics=("arbitrary", "arbitrary"),
                                             vmem_limit_bytes=VMEM_LIMIT),
        name="adaln_mods",
    )(cp, w_ada, b_ada.reshape(depth, 1, n))
    return out[:, :bsz]


NN_WIDTHS = (A_HEADS * LANES, B_HEADS * LANES, C_HEADS * HEAD_DIM, 2 * C_KV_HEADS * HEAD_DIM)
NT_ROWS = (A_HEADS * LANES, A_HEADS * HEAD_DIM, B_HEADS * LANES, B_HEADS * HEAD_DIM,
           C_KV_HEADS * LANES)


def _prep_in_weights(w):
    d = w.shape[0]
    aq, ak, av = A_HEADS * 2 * A_QK_DIM, A_HEADS * 2 * A_QK_DIM, A_HEADS * HEAD_DIM
    bw = B_HEADS * HEAD_DIM
    cq, ckv = C_HEADS * HEAD_DIM, C_KV_HEADS * HEAD_DIM
    cuts = np.cumsum([aq, ak, av, bw, bw, bw, cq, ckv]).tolist()
    qa, ka, va, qb, kb, vb, qc, kc, vc = jnp.split(w, cuts, axis=-1)

    def pad_heads(m, nh, scale):
        m = (m * scale).reshape(d, nh, HEAD_DIM)
        return jnp.pad(m, ((0, 0), (0, 0), (0, LANES - HEAD_DIM))).reshape(d, nh * LANES)

    vc2 = vc.reshape(d, C_KV_HEADS, 1, HEAD_DIM)
    vc2 = jnp.broadcast_to(vc2, (d, C_KV_HEADS, 2, HEAD_DIM)).reshape(d, 2 * ckv)
    kc2 = jnp.broadcast_to(kc.reshape(d, C_KV_HEADS, 1, HEAD_DIM), (d, C_KV_HEADS, 2, HEAD_DIM)).reshape(d, 2 * ckv)
    wn = jnp.concatenate([pad_heads(ka, A_HEADS, 1.0), pad_heads(kb, B_HEADS, 1.0),
                          qc * (HEAD_DIM ** -0.5), vc2], axis=1)
    wt = jnp.concatenate([pad_heads(qa, A_HEADS, A_QK_DIM ** -0.5 * LOG2E), va,
                          pad_heads(qb, B_HEADS, HEAD_DIM ** -0.5 * LOG2E), vb, kc2], axis=1).T
    return wn.astype(BF16), wt.astype(BF16)


def _rms_mod(x, g, sc, sh):
    ms = jnp.mean(x * x, axis=-1, keepdims=True)
    return (x * lax.rsqrt(ms + EPS) * g) * (1.0 + sc) + sh


def _inproj_kernel(x_ref, mod_ref, g_ref, wn_ref, wt_ref, qab_ref, qbb_ref,
                   ka_ref, kb_ref, qc_ref, vc_ref, qta_ref, vta_ref, qtb_ref, vtb_ref, ktc_ref,
                   *, tm, seq):
    d = x_ref.shape[1]
    x = x_ref[...]
    sh = mod_ref[0, :, 0:d]
    sc = mod_ref[0, :, d:2 * d]
    h = _rms_mod(x, g_ref[...], sc, sh).astype(BF16)
    pn = jnp.dot(h, wn_ref[...], preferred_element_type=F32)
    pt = lax.dot_general(wt_ref[...], h, (((1,), (1,)), ((), ())),
                         preferred_element_type=F32)

    pos = (pl.program_id(0) * tm) % seq + lax.broadcasted_iota(I32, (tm, LANES), 0)
    col = lax.broadcasted_iota(I32, (tm, LANES), 1)
    blk_id = pos // KT
    p_hi = (blk_id * KT).astype(F32)
    p_lo = (pos - blk_id * KT).astype(F32)
    aug_a = jnp.where((col >= AUG0) & (col < AUG0 + 3), p_hi,
                      jnp.where((col >= AUG0 + 3) & (col < AUG0 + 6), p_lo, 0.0))
    aug_b = jnp.where((col >= SEL0) & (col - SEL0 == blk_id), 1.0, aug_a)
    for hh in range(A_HEADS):
        ka_ref[:, hh * LANES:(hh + 1) * LANES] = (pn[:, hh * LANES:(hh + 1) * LANES] + aug_a).astype(BF16)
    o = NN_WIDTHS[0]
    for hh in range(B_HEADS):
        kb_ref[:, hh * LANES:(hh + 1) * LANES] = (pn[:, o + hh * LANES:o + (hh + 1) * LANES] + aug_b).astype(BF16)
    o += NN_WIDTHS[1]
    qc_ref[...] = pn[:, o:o + NN_WIDTHS[2]].astype(BF16)
    o += NN_WIDTHS[2]
    vc_ref[...] = pn[:, o:o + NN_WIDTHS[3]].astype(BF16)

    r0 = 0
    for ref, b, nr in zip((qta_ref, vta_ref, qtb_ref, vtb_ref, ktc_ref),
                          (qab_ref, None, qbb_ref, None, None), NT_ROWS):
        blk = pt[r0:r0 + nr, :]
        if b is not None:
            blk = blk + b[...]
        blk = blk.astype(BF16)
        for cc in range(tm // KT):
            ref[cc] = blk[:, cc * KT:(cc + 1) * KT]
        r0 += nr


def _inproj(x2d, mod3, g, wn, wt, qa_bias, qb_bias, *, seq, tm=512):
    t, d = x2d.shape
    assert seq % tm == 0 and tm % KT == 0
    nt = t // tm
    nn_total = sum(NN_WIDTHS)
    row_specs = [pl.BlockSpec((tm, wd), lambda i: (i, 0)) for wd in NN_WIDTHS]
    kt_specs = [pl.BlockSpec((tm // KT, r, KT), lambda i: (i, 0, 0)) for r in NT_ROWS]
    out_shape = ([jax.ShapeDtypeStruct((t, wd), BF16) for wd in NN_WIDTHS]
                 + [jax.ShapeDtypeStruct((t // KT, r, KT), BF16) for r in NT_ROWS])
    return pl.pallas_call(
        functools.partial(_inproj_kernel, tm=tm, seq=seq),
        grid=(nt,),
        in_specs=[
            pl.BlockSpec((tm, d), lambda i: (i, 0)),
            pl.BlockSpec((1, 1, mod3.shape[2]), lambda i: ((i * tm) // seq, 0, 0)),
            pl.BlockSpec((1, d), lambda i: (0, 0)),
            pl.BlockSpec((d, nn_total), lambda i: (0, 0)),
            pl.BlockSpec((sum(NT_ROWS), d), lambda i: (0, 0)),
            pl.BlockSpec((NT_ROWS[0], 1), lambda i: (0, 0)),
            pl.BlockSpec((NT_ROWS[2], 1), lambda i: (0, 0)),
        ],
        out_specs=row_specs + kt_specs,
        out_shape=out_shape,
        compiler_params=pltpu.CompilerParams(dimension_semantics=("arbitrary",),
                                             vmem_limit_bytes=VMEM_LIMIT),
        name="inproj",
    )(x2d, mod3, g, wn, wt, qa_bias, qb_bias)


ACC_ROWS = HEAD_DIM + 16


def _softmax_pv(s, vt1, m_ref, acc_ref):
    m_prev = m_ref[...]
    m_new = jnp.maximum(m_prev, jnp.max(s, axis=0, keepdims=True))
    alpha = jnp.exp2(m_prev - m_new)
    p = jnp.exp2((s - m_new).astype(BF16))
    acc_ref[...] = alpha * acc_ref[...] + jnp.dot(vt1, p, preferred_element_type=F32)
    m_ref[...] = m_new


def _flash_init(m_ref, acc_ref):
    m_ref[...] = jnp.full(m_ref.shape, -jnp.inf, F32)
    acc_ref[...] = jnp.zeros(acc_ref.shape, F32)


def _flash_finish(acc_ref, hh):
    acc = acc_ref[hh]
    return acc[0:HEAD_DIM] / acc[HEAD_DIM:HEAD_DIM + 1]


PAST_TILES = 4


def _flash_causal_pair(k_ref, vt_ref, q_diag, q_past, qi, causal, m_ref, acc_ref, s_ref=None):
    _flash_init(m_ref, acc_ref)
    nkeys = PAST_TILES * KT
    nh = len(q_past)

    def vt1(kj, n, hh):
        rows = slice(hh * HEAD_DIM, (hh + 1) * HEAD_DIM)
        vt = vt_ref[kj, rows, :] if n == 1 else jnp.concatenate([vt_ref[kj + c, rows, :] for c in range(n)], axis=1)
        return jnp.concatenate([vt, jnp.ones((ACC_ROWS - HEAD_DIM, n * KT), BF16)], axis=0)

    def scores(kj, n, qs, hh):
        kk = k_ref[pl.ds(pl.multiple_of(kj * KT, KT), n * KT), hh * LANES:(hh + 1) * LANES]
        return jnp.dot(kk, qs[hh], preferred_element_type=F32)

    def step(kj, n, qs, mask=None):
        for hh in range(nh):
            s = scores(kj, n, qs, hh)
            if mask is not None:
                s = jnp.where(mask, s, -jnp.inf)
            _softmax_pv(s, vt1(kj, n, hh), m_ref.at[hh], acc_ref.at[hh])

    step(qi, 1, q_diag, causal)

    if s_ref is None:
        def body(j, carry):
            step(j * PAST_TILES, PAST_TILES, q_past)
            return carry

        lax.fori_loop(0, qi // PAST_TILES, body, 0)
        for r in range(PAST_TILES - 1):
            @pl.when(qi % PAST_TILES > r)
            def _():
                step(qi - 1 - r, 1, q_past)
        return

    ngroups = k_ref.shape[0] // nkeys

    def scores_into(slot, g):
        for hh in range(nh):
            s_ref[slot, hh] = scores(g * PAST_TILES, PAST_TILES, q_past, hh)

    scores_into(0, 0)

    def body(g, carry):
        scores_into((g + 1) % 2, jnp.minimum(g + 1, ngroups - 1))
        for hh in range(nh):
            _softmax_pv(s_ref[g % 2, hh], vt1(g * PAST_TILES, PAST_TILES, hh), m_ref.at[hh], acc_ref.at[hh])
        return carry

    lax.fori_loop(0, (qi + PAST_TILES - 1) // PAST_TILES, body, 0)


def _diff_kernel(qt_ref, k_ref, vt_ref, lamv_ref, sg_ref, o_ref, m_ref, acc_ref, *, lam_init):
    tq = qt_ref.shape[2]
    qi = pl.program_id(2)
    lv = lamv_ref[...]
    lam = (jnp.exp(jnp.sum(lv[0:1] * lv[1:2], axis=-1, keepdims=True))
           - jnp.exp(jnp.sum(lv[2:3] * lv[3:4], axis=-1, keepdims=True)) + lam_init)
    row = lax.broadcasted_iota(I32, (LANES, tq), 0)
    kr = lax.broadcasted_iota(I32, (KT, 2 * tq), 0)
    qc = lax.broadcasted_iota(I32, (KT, 2 * tq), 1)
    causal = kr <= jnp.where(qc >= tq, qc - tq, qc)
    nh = qt_ref.shape[1] // LANES
    qs = []
    for hh in range(nh):
        qt = qt_ref[0, hh * LANES:(hh + 1) * LANES, :]
        zero = jnp.zeros_like(qt)
        q1 = jnp.where((row < A_QK_DIM) | (row >= AUG0), qt, zero)
        q2 = jnp.where(row >= A_QK_DIM, qt, zero)
        qs.append(jnp.concatenate([q1, q2], axis=1))
    _flash_causal_pair(k_ref, vt_ref, qs, qs, qi, causal, m_ref, acc_ref)
    res = []
    for hh in range(nh):
        o = _flash_finish(acc_ref, hh)
        od = o[:, :tq] - lam * o[:, tq:]
        ms = jnp.mean(od * od, axis=0, keepdims=True)
        res.append(od * lax.rsqrt(ms + EPS) * sg_ref[...] * (1.0 - lam_init))
    o_ref[...] = jnp.concatenate(res, axis=0).T.astype(BF16)


def _diff_attention(qta, ka, vta, lamv, sg, *, bsz, seq, lam_init, nh=A_HEADS):
    tq = KT
    nq = seq // tq
    t = bsz * seq
    return pl.pallas_call(
        functools.partial(_diff_kernel, lam_init=lam_init),
        grid=(bsz, A_HEADS // nh, nq),
        in_specs=[
            pl.BlockSpec((1, nh * LANES, tq), lambda b, hp, qi: (b * nq + qi, hp, 0)),
            pl.BlockSpec((seq, nh * LANES), lambda b, hp, qi: (b, hp)),
            pl.BlockSpec((nq, nh * HEAD_DIM, KT), lambda b, hp, qi: (b, hp, 0)),
            pl.BlockSpec((4, A_QK_DIM), lambda b, hp, qi: (0, 0)),
            pl.BlockSpec((HEAD_DIM, 1), lambda b, hp, qi: (0, 0)),
        ],
        out_specs=pl.BlockSpec((tq, nh * HEAD_DIM), lambda b, hp, qi: (b * nq + qi, hp)),
        out_shape=jax.ShapeDtypeStruct((t, A_HEADS * HEAD_DIM), BF16),
        scratch_shapes=[pltpu.VMEM((nh, 1, 2 * tq), F32), pltpu.VMEM((nh, ACC_ROWS, 2 * tq), F32)],
        compiler_params=pltpu.CompilerParams(dimension_semantics=("arbitrary",) * 3,
                                             vmem_limit_bytes=VMEM_LIMIT),
        name="diff_attn",
    )(qta, ka, vta, lamv, sg)


def _moba_kernel(qt_ref, k_ref, vt_ref, o_ref, m_ref, acc_ref, km_ref, s_ref):
    tq = qt_ref.shape[2]
    nh = qt_ref.shape[1] // LANES
    nb = vt_ref.shape[0]
    qi = pl.program_id(2)

    @pl.when(qi == 0)
    def _():
        lane1 = lax.broadcasted_iota(I32, (1, LANES), 1)
        for hh in range(nh):
            km_ref[hh] = jnp.zeros((LANES, LANES), F32)

            def put_block(j, carry):
                blk = k_ref[pl.ds(pl.multiple_of(j * KT, KT), KT), hh * LANES:(hh + 1) * LANES].astype(F32)
                mean = jnp.sum(blk, axis=0, keepdims=True) * (1.0 / KT)
                km_ref[hh, pl.ds(SEL0 + j, 1), :] = jnp.where(lane1 < HEAD_DIM, mean, 0.0)
                return carry
            lax.fori_loop(0, nb, put_block, 0)

    row = lax.broadcasted_iota(I32, (LANES, tq), 0)
    kr = lax.broadcasted_iota(I32, (KT, tq), 0)
    qc = lax.broadcasted_iota(I32, (KT, tq), 1)
    causal = kr <= qc
    in_sel = (row >= SEL0) & (row < SEL0 + MAX_BLOCKS)
    q_diag, q_past = [], []
    for hh in range(nh):
        qt = qt_ref[0, hh * LANES:(hh + 1) * LANES, :]
        km = km_ref[hh]
        km_hi = km.astype(BF16)
        km_lo = (km - km_hi.astype(F32)).astype(BF16)
        gate = (jnp.dot(km_hi, qt, preferred_element_type=F32)
                + jnp.dot(km_lo, qt, preferred_element_type=F32))
        cur = jnp.where((row >= SEL0) & (row < SEL0 + qi), gate, -jnp.inf)
        sel = jnp.zeros((LANES, tq), jnp.bool_)
        for _ in range(MOBA_TOPK):
            mx = jnp.max(cur, axis=0, keepdims=True)
            first = jnp.min(jnp.where(cur == mx, row, 4 * LANES), axis=0, keepdims=True)
            pick = (row == first) & (mx > -jnp.inf)
            sel = sel | pick
            cur = jnp.where(pick, -jnp.inf, cur)
        q_diag.append(jnp.where(in_sel, jnp.zeros_like(qt), qt))
        q_past.append(jnp.where(in_sel, jnp.where(sel, 0.0, NEG_BIG).astype(BF16), qt))
    _flash_causal_pair(k_ref, vt_ref, q_diag, q_past, qi, causal, m_ref, acc_ref, s_ref)
    res = [_flash_finish(acc_ref, hh) for hh in range(nh)]
    o_ref[...] = jnp.concatenate(res, axis=0).T.astype(BF16)


def _moba_attention(qtb, kb, vtb, *, bsz, seq, nh=B_HEADS):
    tq = KT
    nq = seq // tq
    assert nq <= MAX_BLOCKS and seq % (PAST_TILES * KT) == 0
    t = bsz * seq
    return pl.pallas_call(
        _moba_kernel,
        grid=(bsz, B_HEADS // nh, nq),
        in_specs=[
            pl.BlockSpec((1, nh * LANES, tq), lambda b, hp, qi: (b * nq + qi, hp, 0)),
            pl.BlockSpec((seq, nh * LANES), lambda b, hp, qi: (b, hp), pipeline_mode=pl.Buffered(1)),
            pl.BlockSpec((nq, nh * HEAD_DIM, KT), lambda b, hp, qi: (b, hp, 0), pipeline_mode=pl.Buffered(1)),
        ],
        out_specs=pl.BlockSpec((tq, nh * HEAD_DIM), lambda b, hp, qi: (b * nq + qi, hp)),
        out_shape=jax.ShapeDtypeStruct((t, B_HEADS * HEAD_DIM), BF16),
        scratch_shapes=[pltpu.VMEM((nh, 1, tq), F32), pltpu.VMEM((nh, ACC_ROWS, tq), F32),
                        pltpu.VMEM((nh, LANES, LANES), F32), pltpu.VMEM((2, nh, PAST_TILES * KT, tq), F32)],
        compiler_params=pltpu.CompilerParams(dimension_semantics=("arbitrary",) * 3,
                                             vmem_limit_bytes=VMEM_LIMIT),
        name="moba_attn",
    )(qtb, kb, vtb)


def _swa_kernel(sink_ref, q_ref, ktp_ref, ktc_ref, vp_ref, vc_ref, o_ref, *, slopes):
    tq = q_ref.shape[0]
    n = pl.program_id(1)
    lane = lax.broadcasted_iota(I32, (tq, LANES), 1)
    r2 = lax.broadcasted_iota(I32, (tq, 2 * KT), 0)
    c2 = lax.broadcasted_iota(I32, (tq, 2 * KT), 1)
    rel = r2 + KT - c2
    mask = (rel >= 0) & (rel < WINDOW) & ((c2 >= KT) | (n > 0))
    relf = rel.astype(F32)
    res = []
    for hq in range(C_HEADS):
        kv = hq // C_GROUP
        qp = q_ref[:, (hq // 2) * LANES:(hq // 2 + 1) * LANES]
        qm = jnp.where((lane < HEAD_DIM) == (hq % 2 == 0), qp, jnp.zeros_like(qp))
        kt = jnp.concatenate([ktp_ref[0, kv * LANES:(kv + 1) * LANES, :],
                              ktc_ref[0, kv * LANES:(kv + 1) * LANES, :]], axis=1)
        vv = jnp.concatenate([vp_ref[:, kv * LANES:(kv + 1) * LANES],
                              vc_ref[:, kv * LANES:(kv + 1) * LANES]], axis=0)
        s = jnp.dot(qm, kt, preferred_element_type=F32)
        s = jnp.where(mask, s - slopes[hq] * relf, -jnp.inf)
        sink = sink_ref[hq]
        m = jnp.maximum(jnp.max(s, axis=-1, keepdims=True), sink)
        e = jnp.exp(s - m)
        den = jnp.sum(e, axis=-1, keepdims=True) + jnp.exp(sink - m)
        p = (e / den).astype(BF16)
        res.append(jnp.dot(p, vv, preferred_element_type=F32))
    for pr in range(C_HEADS // 2):
        o_ref[:, pr * LANES:(pr + 1) * LANES] = jnp.where(lane < HEAD_DIM, res[2 * pr], res[2 * pr + 1]).astype(BF16)


def _swa_attention(sinks, qc, ktc, vc2, *, bsz, seq, slopes):
    tq = KT
    nq = seq // tq
    t = bsz * seq
    return pl.pallas_call(
        functools.partial(_swa_kernel, slopes=slopes),
        grid=(bsz, nq),
        in_specs=[
            pl.BlockSpec(memory_space=pltpu.SMEM),
            pl.BlockSpec((tq, C_HEADS * HEAD_DIM), lambda b, n: (b * nq + n, 0)),
            pl.BlockSpec((1, C_KV_HEADS * LANES, KT), lambda b, n: (b * nq + jnp.maximum(n - 1, 0), 0, 0)),
            pl.BlockSpec((1, C_KV_HEADS * LANES, KT), lambda b, n: (b * nq + n, 0, 0)),
            pl.BlockSpec((tq, C_KV_HEADS * LANES), lambda b, n: (b * nq + jnp.maximum(n - 1, 0), 0)),
            pl.BlockSpec((tq, C_KV_HEADS * LANES), lambda b, n: (b * nq + n, 0)),
        ],
        out_specs=pl.BlockSpec((tq, C_HEADS * HEAD_DIM), lambda b, n: (b * nq + n, 0)),
        out_shape=jax.ShapeDtypeStruct((t, C_HEADS * HEAD_DIM), BF16),
        compiler_params=pltpu.CompilerParams(dimension_semantics=("arbitrary",) * 2,
                                             vmem_limit_bytes=VMEM_LIMIT),
        name="swa_attn",
    )(sinks, qc, ktc, ktc, vc2, vc2)


def _outproj_kernel(oa_ref, ob_ref, oc_ref, woa_ref, wob_ref, woc_ref, x_ref, mod_ref, g_ref, wq_ref, keys_ref,
                    x1_ref, h2_ref, st_ref):
    d = x_ref.shape[1]
    mix = (jnp.dot(oa_ref[...], woa_ref[...], preferred_element_type=F32)
           + jnp.dot(ob_ref[...], wob_ref[...], preferred_element_type=F32)
           + jnp.dot(oc_ref[...], woc_ref[...], preferred_element_type=F32))
    g1 = mod_ref[0, :, 2 * d:3 * d]
    sh2 = mod_ref[0, :, 3 * d:4 * d]
    sc2 = mod_ref[0, :, 4 * d:5 * d]
    x1 = x_ref[...] + g1 * mix
    x1_ref[...] = x1
    h2 = _rms_mod(x1, g_ref[...], sc2, sh2)
    h2_ref[...] = h2
    pq = jnp.dot(h2.astype(BF16), wq_ref[...], preferred_element_type=F32).astype(BF16)
    for hp in range(2 * PEER_HEADS):
        st_ref[hp] = lax.dot_general(keys_ref[hp], pq[:, hp * LANES:(hp + 1) * LANES],
                                     (((1,), (1,)), ((), ())), preferred_element_type=F32)


def _outproj(oa, ob, oc, woa, wob, woc, x2d, mod3, g, wq, keys, *, seq, tm=256):
    t, d = x2d.shape
    nt = t // tm
    nq = wq.shape[1]
    full = lambda a: pl.BlockSpec(a.shape, lambda i: (0,) * a.ndim)
    return pl.pallas_call(
        _outproj_kernel,
        grid=(nt,),
        in_specs=[
            pl.BlockSpec((tm, oa.shape[1]), lambda i: (i, 0)),
            pl.BlockSpec((tm, ob.shape[1]), lambda i: (i, 0)),
            pl.BlockSpec((tm, oc.shape[1]), lambda i: (i, 0)),
            full(woa), full(wob), full(woc),
            pl.BlockSpec((tm, d), lambda i: (i, 0)),
            pl.BlockSpec((1, 1, mod3.shape[2]), lambda i: ((i * tm) // seq, 0, 0)),
            pl.BlockSpec((1, d), lambda i: (0, 0)),
            full(wq), full(keys),
        ],
        out_specs=[pl.BlockSpec((tm, d), lambda i: (i, 0)),
                   pl.BlockSpec((tm, d), lambda i: (i, 0)),
                   pl.BlockSpec((2 * PEER_HEADS, PEER_NKEYS, tm), lambda i: (0, 0, i))],
        out_shape=[jax.ShapeDtypeStruct((t, d), F32), jax.ShapeDtypeStruct((t, d), F32),
                   jax.ShapeDtypeStruct((2 * PEER_HEADS, PEER_NKEYS, t), F32)],
        compiler_params=pltpu.CompilerParams(dimension_semantics=("arbitrary",),
                                             vmem_limit_bytes=VMEM_LIMIT),
        name="outproj_peerq",
    )(oa, ob, oc, woa, wob, woc, x2d, mod3, g, wq, keys)


_CAND_BLOCKS = ((0, 16),) + tuple((i, 8) for i in range(1, 8))
_CAND_ROWS = 16 + 7 * 8 + 8
_BIG_I = np.int32(2 ** 30)


TOPK_TL = LANES


def _topk_rows(curs, keys, val_refs, key_refs, unroll):
    def body(r, curs):
        out = []
        for cur, key, val_ref, key_ref in zip(curs, keys, val_refs, key_refs):
            m = jnp.max(cur, axis=0, keepdims=True)
            kmin = jnp.min(jnp.where(cur == m, key, _BIG_I), axis=0, keepdims=True)
            val_ref[pl.ds(r, 1), :] = m
            key_ref[pl.ds(r, 1), :] = kmin
            out.append(jnp.where(key == kmin, -jnp.inf, cur))
        return tuple(out)
    lax.fori_loop(0, PEER_TOPK, body, tuple(curs), unroll=unroll)


def _topk_scratch():
    pair = [pltpu.VMEM((2, PEER_TOPK, TOPK_TL), F32), pltpu.VMEM((2, PEER_TOPK, TOPK_TL), I32)]
    hk = PEER_HEADS * PEER_TOPK
    return pair * 3 + [pltpu.VMEM((hk, TOPK_TL), I32), pltpu.VMEM((hk, TOPK_TL), F32)]


def _topk_head_pair(st_ref, h0, scratch, unroll=False):
    v1_ref, k1_ref, v2_ref, k2_ref, vt_ref, kt_ref, ei_ref, gg_ref = scratch
    tl = st_ref.shape[2]
    row = lax.broadcasted_iota(I32, (PEER_NKEYS, tl), 0)
    jrow8 = lax.broadcasted_iota(I32, (8, tl), 0)
    jrow16 = lax.broadcasted_iota(I32, (16, tl), 0)
    cand, ckey = [], []
    for s in range(2):
        h = h0 + s
        _topk_rows([st_ref[2 * h], st_ref[2 * h + 1]], [row, row],
                   [v1_ref.at[s], v2_ref.at[s]], [k1_ref.at[s], k2_ref.at[s]], unroll)
        sv1, si1, sv2, si2 = v1_ref[s], k1_ref[s], v2_ref[s], k2_ref[s]
        cands, keys = [], []
        for i, nj in _CAND_BLOCKS:
            jrow = jrow16 if nj == 16 else jrow8
            cands.append(sv1[i:i + 1] + sv2[0:nj])
            keys.append((i * PEER_TOPK + jrow) * (PEER_NKEYS * PEER_NKEYS)
                        + si1[i:i + 1] * PEER_NKEYS + si2[0:nj])
        cands.append(sv1[8:16] + sv2[0:1])
        keys.append((jrow8 + 8) * (PEER_TOPK * PEER_NKEYS * PEER_NKEYS) + si1[8:16] * PEER_NKEYS + si2[0:1])
        cand.append(jnp.concatenate(cands, axis=0))
        ckey.append(jnp.concatenate(keys, axis=0))
    _topk_rows(cand, ckey, [vt_ref.at[0], vt_ref.at[1]], [kt_ref.at[0], kt_ref.at[1]], unroll)
    for s in range(2):
        rows = pl.ds(pl.multiple_of((h0 + s) * PEER_TOPK, PEER_TOPK), PEER_TOPK)
        top = vt_ref[s]
        e = jnp.exp(top - top[0:1])
        gg_ref[rows, :] = e / jnp.sum(e, axis=0, keepdims=True)
        ei_ref[rows, :] = kt_ref[s] & (PEER_NKEYS * PEER_NKEYS - 1)


def _topk_emit(scratch, off_ref, par_ref, g_ref):
    ei = scratch[6][...].T
    off_ref[...] = (ei >> 1) * SUB
    par_ref[...] = ei & 1
    g_ref[...] = scratch[7][...].T


def _peer_topk_kernel(st_ref, off_ref, par_ref, g_ref, *scratch):
    for h0 in range(0, PEER_HEADS, 2):
        _topk_head_pair(st_ref, h0, scratch)
    _topk_emit(scratch, off_ref, par_ref, g_ref)


def _peer_topk(st, blk0, nblk):
    hk = PEER_HEADS * PEER_TOPK
    t = nblk * TOPK_TL
    return pl.pallas_call(
        _peer_topk_kernel,
        grid=(nblk,),
        in_specs=[pl.BlockSpec((2 * PEER_HEADS, PEER_NKEYS, TOPK_TL), lambda i: (0, 0, blk0 + i))],
        out_specs=[pl.BlockSpec((TOPK_TL, hk), lambda i: (i, 0))] * 3,
        out_shape=[jax.ShapeDtypeStruct((t, hk), I32), jax.ShapeDtypeStruct((t, hk), I32),
                   jax.ShapeDtypeStruct((t, hk), F32)],
        scratch_shapes=_topk_scratch(),
        compiler_params=pltpu.CompilerParams(dimension_semantics=("arbitrary",),
                                             vmem_limit_bytes=VMEM_LIMIT),
        name="peer_topk",
    )(st)


SUB = 8


def _erf(x):
    return lax.erf(x)


def _pack_expert_table(w):
    n, d = w.shape
    b = lax.bitcast_convert_type(w.astype(BF16), jnp.uint16).astype(jnp.uint32).reshape(n // 2, 2, d // LANES, LANES)
    return (b[:, 0] | (b[:, 1] << 16)).reshape(n // 2 * (d // LANES), LANES)


def _load_table_once(tab_hbm, tab, sem):
    @pl.when(pl.program_id(0) == 0)
    def _():
        cp = pltpu.make_async_copy(tab_hbm, tab, sem)
        cp.start()
        cp.wait()


def _pair_tile(tab, off):
    return tab[pl.ds(pl.multiple_of(off, SUB), SUB), :]


def _group_matrix(rows, cols):
    r = lax.broadcasted_iota(I32, (rows, cols), 0)
    c = lax.broadcasted_iota(I32, (rows, cols), 1)
    return (c // (cols // rows) == r).astype(BF16)


def _split2(x):
    hi = x.astype(BF16)
    return hi, (x - hi.astype(F32)).astype(BF16)


PAIR_ROWS = 2 * SUB


def _pair_rows(tab, off_ref, tt, hk):
    return jnp.concatenate([pltpu.bitcast(_pair_tile(tab, off_ref[tt * hk + k]), BF16) for k in range(hk)], axis=0)


def _own_sublane_mask(hk):
    srow = lax.broadcasted_iota(I32, (SUB, hk * PAIR_ROWS), 0)
    scol = lax.broadcasted_iota(I32, (SUB, hk * PAIR_ROWS), 1)
    return (scol % PAIR_ROWS) // 2 == srow


def _fold_matrix(hk):
    c = jnp.arange(hk * PAIR_ROWS)
    return jax.nn.one_hot((c % 2) * hk + c // PAIR_ROWS, 2 * hk, dtype=BF16)


def _peer_u_body(off_ref, par_ref, h_ref, g_ref, fold_ref, w_ref, tab, d_sc):
    tq, hk = g_ref.shape
    mine = _own_sublane_mask(hk)
    nt = (((1,), (1,)), ((), ()))
    for tt in range(tq):
        h_hi, h_lo = _split2(h_ref[tt])
        s = lax.dot_general(jnp.concatenate([h_hi, h_lo], axis=0), _pair_rows(tab, off_ref, tt, hk), nt,
                            preferred_element_type=F32)
        d = jnp.where(mine, s[0:SUB] + s[SUB:2 * SUB], 0.0)
        d_sc[tt:tt + 1, :] = jnp.sum(d, axis=0, keepdims=True)
    d_hi, d_lo = _split2(d_sc[...])
    a2 = jnp.dot(jnp.concatenate([d_hi, d_lo], axis=0), fold_ref[...], preferred_element_type=F32)
    a2 = a2[0:tq] + a2[tq:2 * tq]
    a = jnp.where(par_ref[...] == 0, a2[:, :hk], a2[:, hk:])
    w_ref[...] = g_ref[...] * (0.5 * a * (1.0 + _erf(a * (2.0 ** -0.5))))


U_TQ = 64


def _peer_u_kernel(off_ref, par_ref, h_ref, g_ref, fold_ref, tab_hbm, w_ref, tab, sem, d_sc):
    _load_table_once(tab_hbm, tab, sem)
    _peer_u_body(off_ref, par_ref, h_ref, g_ref, fold_ref, w_ref, tab, d_sc)


def _peer_u_topk_kernel(off_ref, par_ref, h_ref, g_ref, fold_ref, st_ref, tab_hbm,
                        w_ref, off2_ref, par2_ref, g2_ref, tab, sem, d_sc, *scratch):
    half = pl.program_id(0) % 2
    _load_table_once(tab_hbm, tab, sem)
    for q in range(2):
        _topk_head_pair(st_ref, (2 * half + q) * 2, scratch, unroll=True)
    _peer_u_body(off_ref, par_ref, h_ref, g_ref, fold_ref, w_ref, tab, d_sc)

    @pl.when(half == 1)
    def _():
        _topk_emit(scratch, off2_ref, par2_ref, g2_ref)


def _peer_u(off_flat, par, g, h2, utab, tok0, st=None, st_blk0=None):
    t, hk = g.shape
    nj = h2.shape[1]
    tq = U_TQ
    assert t % (2 * tq) == 0 and 2 * tq == TOPK_TL and tok0 % tq == 0
    fold = _fold_matrix(hk)
    in_specs = [
        pl.BlockSpec((tq * hk,), lambda i: (i,), memory_space=pltpu.SMEM),
        pl.BlockSpec((tq, hk), lambda i: (i, 0)),
        pl.BlockSpec((tq, nj, LANES), lambda i: (tok0 // tq + i, 0, 0)),
        pl.BlockSpec((tq, hk), lambda i: (i, 0)),
        pl.BlockSpec(fold.shape, lambda i: (0, 0)),
    ]
    out_specs = [pl.BlockSpec((tq, hk), lambda i: (i, 0))]
    out_shape = [jax.ShapeDtypeStruct((t, hk), F32)]
    scratch = [pltpu.VMEM(utab.shape, jnp.uint32), pltpu.SemaphoreType.DMA(()),
               pltpu.VMEM((tq, hk * PAIR_ROWS), F32)]
    args = [off_flat, par, h2, g, fold]
    if st is not None:
        in_specs.append(pl.BlockSpec((2 * PEER_HEADS, PEER_NKEYS, TOPK_TL), lambda i: (0, 0, st_blk0 + i // 2)))
        out_specs += [pl.BlockSpec((TOPK_TL, hk), lambda i: (i // 2, 0))] * 3
        out_shape += [jax.ShapeDtypeStruct((t, hk), I32), jax.ShapeDtypeStruct((t, hk), I32),
                      jax.ShapeDtypeStruct((t, hk), F32)]
        scratch += _topk_scratch()
        args.append(st)
    in_specs.append(pl.BlockSpec(memory_space=pl.ANY))
    args.append(utab)
    return pl.pallas_call(
        _peer_u_kernel if st is None else _peer_u_topk_kernel,
        grid=(t // tq,),
        in_specs=in_specs,
        out_specs=out_specs,
        out_shape=out_shape,
        scratch_shapes=scratch,
        compiler_params=pltpu.CompilerParams(dimension_semantics=("arbitrary",),
                                             vmem_limit_bytes=VMEM_LIMIT),
        name="peer_u" if st is None else "peer_u_topk",
    )(*args)


def _peer_v_kernel(off_ref, par_ref, w_ref, x1_ref, g2_ref, fg_ref, tab_hbm, o_ref, tab, sem, wl_sc, *, final):
    tq, hk = w_ref.shape
    _load_table_once(tab_hbm, tab, sem)
    width = hk * PAIR_ROWS
    rep = _group_matrix(hk, width)
    w_hi, w_lo = _split2(w_ref[...])
    parl = jnp.dot(par_ref[...].astype(BF16), rep, preferred_element_type=F32)
    lane = lax.broadcasted_iota(I32, (tq, width), 1)
    wanted = (lane % 2).astype(F32) == parl
    wl_sc[0] = jnp.where(wanted, jnp.dot(w_hi, rep, preferred_element_type=F32), 0.0)
    wl_sc[1] = jnp.where(wanted, jnp.dot(w_lo, rep, preferred_element_type=F32), 0.0)
    mine = _own_sublane_mask(hk)
    for tt in range(tq):
        lhs = jnp.concatenate([jnp.where(mine, wl_sc[0, tt:tt + 1, :], 0.0),
                               jnp.where(mine, wl_sc[1, tt:tt + 1, :], 0.0)], axis=0).astype(BF16)
        r = jnp.dot(lhs, _pair_rows(tab, off_ref, tt, hk), preferred_element_type=F32)
        y = x1_ref[tt] + g2_ref[0] * (r[0:SUB] + r[SUB:2 * SUB])
        if final:
            ms = jnp.mean(jnp.mean(y * y, axis=-1, keepdims=True), axis=0, keepdims=True)
            y = y * lax.rsqrt(ms + EPS) * fg_ref[...]
        o_ref[tt] = y


def _peer_v(off_flat, par, w, x1, g2, final_g, vtab, *, seq, final, tq=64):
    t, nj, _ = x1.shape
    hk = w.shape[1]
    return pl.pallas_call(
        functools.partial(_peer_v_kernel, final=final),
        grid=(t // tq,),
        in_specs=[
            pl.BlockSpec((tq * hk,), lambda i: (i,), memory_space=pltpu.SMEM),
            pl.BlockSpec((tq, hk), lambda i: (i, 0)),
            pl.BlockSpec((tq, hk), lambda i: (i, 0)),
            pl.BlockSpec((tq, nj, LANES), lambda i: (i, 0, 0)),
            pl.BlockSpec((1, nj, LANES), lambda i: ((i * tq) // seq, 0, 0)),
            pl.BlockSpec((nj, LANES), lambda i: (0, 0)),
            pl.BlockSpec(memory_space=pl.ANY),
        ],
        out_specs=pl.BlockSpec((tq, nj, LANES), lambda i: (i, 0, 0)),
        out_shape=jax.ShapeDtypeStruct((t, nj, LANES), F32),
        scratch_shapes=[pltpu.VMEM(vtab.shape, jnp.uint32), pltpu.SemaphoreType.DMA(()),
                        pltpu.VMEM((2, tq, hk * PAIR_ROWS), F32)],
        compiler_params=pltpu.CompilerParams(dimension_semantics=("arbitrary",),
                                             vmem_limit_bytes=VMEM_LIMIT),
        name="peer_v",
    )(off_flat, par, w, x1, g2, final_g, vtab)


def kernel(x, c, norm1_g, norm2_g, w_ada, b_ada, w_in, w_out, lam_q1, lam_k1, lam_q2, lam_k2, subln_g, sinks,
           peer_wq, peer_keys, peer_u, peer_v, final_g):
    bsz, seq, d = x.shape
    depth = w_in.shape[0]
    t = bsz * seq
    slopes = _alibi_slopes()
    sl_c = [float(s) for s in slopes[:C_HEADS]]
    qa_bias = _slope_bias_col(slopes[C_HEADS:C_HEADS + A_HEADS] * np.float32(LOG2E))
    qb_bias = _slope_bias_col(slopes[C_HEADS + A_HEADS:] * np.float32(LOG2E))
    mods = _adaln_mods(c, w_ada, b_ada)
    x2d = x.reshape(t, d)
    av, bw = A_HEADS * HEAD_DIM, B_HEADS * HEAD_DIM
    for l in range(depth):
        lam_init = 0.8 - 0.6 * math.exp(-0.3 * l)
        mod3 = mods[l].reshape(bsz, 1, N_MOD * d)
        wn, wt = _prep_in_weights(w_in[l])
        ka, kb, qc, vc2, qta, vta, qtb, vtb, ktc = _inproj(
            x2d, mod3, norm1_g[l].reshape(1, d), wn, wt, qa_bias, qb_bias, seq=seq)
        lamv = jnp.stack([lam_q1[l], lam_k1[l], lam_q2[l], lam_k2[l]]).astype(F32)
        sg = subln_g[l].reshape(HEAD_DIM, 1).astype(F32)
        oa = _diff_attention(qta, ka, vta, lamv, sg, bsz=bsz, seq=seq, lam_init=lam_init)
        ob = _moba_attention(qtb, kb, vtb, bsz=bsz, seq=seq)
        oc = _swa_attention(sinks[l].astype(F32), qc, ktc, vc2, bsz=bsz, seq=seq, slopes=sl_c)
        wo = w_out[l].astype(BF16)
        keys = peer_keys[l].reshape(2 * PEER_HEADS, PEER_NKEYS, -1).astype(BF16)
        x1, h2, st = _outproj(oa, ob, oc, wo[:av], wo[av:av + bw], wo[av + bw:], x2d, mod3,
                              norm2_g[l].reshape(1, d), peer_wq[l].astype(BF16), keys, seq=seq)
        nj = d // LANES
        g2 = mods[l][:, 5 * d:6 * d].reshape(bsz, nj, LANES)
        assert seq % TOPK_TL == 0
        blocks = seq // TOPK_TL
        utab = _pack_expert_table(peer_u[l])
        h2t = h2.reshape(t, nj, LANES)
        sel = [_peer_topk(st, 0, blocks)]
        ws = []
        for b in range(bsz):
            off_b, par_b, g_b = sel[b]
            if b + 1 < bsz:
                w_b, *nxt = _peer_u(off_b.reshape(-1), par_b, g_b, h2t, utab, b * seq, st, (b + 1) * blocks)
                sel.append(nxt)
            else:
                (w_b,) = _peer_u(off_b.reshape(-1), par_b, g_b, h2t, utab, b * seq)
            ws.append(w_b)
        toff = jnp.concatenate([s_[0] for s_ in sel]).reshape(-1)
        par = jnp.concatenate([s_[1] for s_ in sel])
        w = jnp.concatenate(ws)
        x2d = _peer_v(toff, par, w, x1.reshape(t, nj, LANES), g2, final_g.reshape(nj, LANES),
                      _pack_expert_table(peer_v[l]), seq=seq, final=(l == depth - 1)).reshape(t, d)
    return x2d.reshape(bsz, seq, d)
```

```python
import functools
import math

import numpy as np
import jax
import jax.numpy as jnp
from jax import lax
from jax.experimental import pallas as pl
from jax.experimental.pallas import tpu as pltpu

F32 = jnp.float32
BF16 = jnp.bfloat16
I32 = jnp.int32

D_MODEL = 1024
HEAD_DIM = 64
N_HEADS_TOTAL = 16
A_HEADS = 4
B_HEADS = 6
C_HEADS = 6
C_KV_HEADS = 2
C_GROUP = 3
A_QK_DIM = 32
MOBA_BLOCK = 256
MOBA_TOPK = 3
WINDOW = 128
ALIBI_MAX = 8.0
PEER_HEADS = 8
PEER_NKEYS = 128
PEER_TOPK = 16
N_MOD = 6
EPS = 1e-6

LANES = 128
KT = 256
AUG0 = HEAD_DIM
SEL0 = AUG0 + 6
MAX_BLOCKS = LANES - SEL0
NEG_BIG = -1e30
LOG2E = math.log2(math.e)
VMEM_LIMIT = 56 * 1024 * 1024


def _alibi_slopes():
    n = N_HEADS_TOTAL
    return (2.0 ** (-ALIBI_MAX * np.arange(1, n + 1, dtype=np.float32) / n)).astype(np.float32)


def _split3(v):
    v = np.float32(v)
    hi = np.float32(np.asarray(v).astype(jnp.bfloat16).astype(np.float32))
    r = np.float32(v - hi)
    mid = np.float32(np.asarray(r).astype(jnp.bfloat16).astype(np.float32))
    lo = np.float32(np.float32(r - mid))
    lo = np.float32(np.asarray(lo).astype(jnp.bfloat16).astype(np.float32))
    return hi, mid, lo


def _slope_bias_col(slopes):
    col = np.zeros((LANES * len(slopes), 1), np.float32)
    for h, s in enumerate(slopes):
        hi, mid, lo = _split3(s)
        col[h * LANES + AUG0:h * LANES + AUG0 + 6, 0] = [hi, mid, lo, hi, mid, lo]
    return jnp.asarray(col)


def _mod_kernel(c_ref, w_ref, b_ref, o_ref):
    c = c_ref[...]
    cs = c * (1.0 / (1.0 + jnp.exp(-c)))
    o_ref[0] = jnp.dot(cs, w_ref[0], preferred_element_type=F32) + b_ref[0]


def _adaln_mods(c, w_ada, b_ada):
    depth, d, n = w_ada.shape
    bsz = c.shape[0]
    rows = -(-bsz // 8) * 8
    cp = jnp.pad(c, ((0, rows - bsz), (0, 0)))
    tn = 1536
    out = pl.pallas_call(
        _mod_kernel,
        grid=(depth, n // tn),
        in_specs=[
            pl.BlockSpec((rows, d), lambda l, j: (0, 0)),
            pl.BlockSpec((1, d, tn), lambda l, j: (l, 0, j)),
            pl.BlockSpec((1, 1, tn), lambda l, j: (l, 0, j)),
        ],
        out_specs=pl.BlockSpec((1, rows, tn), lambda l, j: (l, 0, j)),
        out_shape=jax.ShapeDtypeStruct((depth, rows, n), F32),
        compiler_params=pltpu.CompilerParams(dimension_semantics=("arbitrary", "arbitrary"),
                                             vmem_limit_bytes=VMEM_LIMIT),
        name="adaln_mods",
    )(cp, w_ada, b_ada.reshape(depth, 1, n))
    return out[:, :bsz]


NN_WIDTHS = (A_HEADS * LANES, B_HEADS * LANES, C_HEADS * HEAD_DIM, 2 * C_KV_HEADS * HEAD_DIM)
NT_ROWS = (A_HEADS * LANES, A_HEADS * HEAD_DIM, B_HEADS * LANES, B_HEADS * HEAD_DIM,
           C_KV_HEADS * LANES)


def _prep_in_weights(w):
    d = w.shape[0]
    aq, ak, av = A_HEADS * 2 * A_QK_DIM, A_HEADS * 2 * A_QK_DIM, A_HEADS * HEAD_DIM
    bw = B_HEADS * HEAD_DIM
    cq, ckv = C_HEADS * HEAD_DIM, C_KV_HEADS * HEAD_DIM
    cuts = np.cumsum([aq, ak, av, bw, bw, bw, cq, ckv]).tolist()
    qa, ka, va, qb, kb, vb, qc, kc, vc = jnp.split(w, cuts, axis=-1)

    def pad_heads(m, nh, scale):
        m = (m * scale).reshape(d, nh, HEAD_DIM)
        return jnp.pad(m, ((0, 0), (0, 0), (0, LANES - HEAD_DIM))).reshape(d, nh * LANES)

    vc2 = vc.reshape(d, C_KV_HEADS, 1, HEAD_DIM)
    vc2 = jnp.broadcast_to(vc2, (d, C_KV_HEADS, 2, HEAD_DIM)).reshape(d, 2 * ckv)
    kc2 = jnp.broadcast_to(kc.reshape(d, C_KV_HEADS, 1, HEAD_DIM), (d, C_KV_HEADS, 2, HEAD_DIM)).reshape(d, 2 * ckv)
    wn = jnp.concatenate([pad_heads(ka, A_HEADS, 1.0), pad_heads(kb, B_HEADS, 1.0),
                          qc * (HEAD_DIM ** -0.5), vc2], axis=1)
    wt = jnp.concatenate([pad_heads(qa, A_HEADS, A_QK_DIM ** -0.5 * LOG2E), va,
                          pad_heads(qb, B_HEADS, HEAD_DIM ** -0.5 * LOG2E), vb, kc2], axis=1).T
    return wn.astype(BF16), wt.astype(BF16)


def _rms_mod(x, g, sc, sh):
    ms = jnp.mean(x * x, axis=-1, keepdims=True)
    return (x * lax.rsqrt(ms + EPS) * g) * (1.0 + sc) + sh


def _inproj_kernel(x_ref, mod_ref, g_ref, wn_ref, wt_ref, qab_ref, qbb_ref,
                   ka_ref, kb_ref, qc_ref, vc_ref, qta_ref, vta_ref, qtb_ref, vtb_ref, ktc_ref,
                   *, tm, seq):
    d = x_ref.shape[1]
    x = x_ref[...]
    sh = mod_ref[0, :, 0:d]
    sc = mod_ref[0, :, d:2 * d]
    h = _rms_mod(x, g_ref[...], sc, sh).astype(BF16)
    pn = jnp.dot(h, wn_ref[...], preferred_element_type=F32)
    pt = lax.dot_general(wt_ref[...], h, (((1,), (1,)), ((), ())),
                         preferred_element_type=F32)

    pos = (pl.program_id(0) * tm) % seq + lax.broadcasted_iota(I32, (tm, LANES), 0)
    col = lax.broadcasted_iota(I32, (tm, LANES), 1)
    blk_id = pos // KT
    p_hi = (blk_id * KT).astype(F32)
    p_lo = (pos - blk_id * KT).astype(F32)
    aug_a = jnp.where((col >= AUG0) & (col < AUG0 + 3), p_hi,
                      jnp.where((col >= AUG0 + 3) & (col < AUG0 + 6), p_lo, 0.0))
    aug_b = jnp.where((col >= SEL0) & (col - SEL0 == blk_id), 1.0, aug_a)
    for hh in range(A_HEADS):
        ka_ref[:, hh * LANES:(hh + 1) * LANES] = (pn[:, hh * LANES:(hh + 1) * LANES] + aug_a).astype(BF16)
    o = NN_WIDTHS[0]
    for hh in range(B_HEADS):
        kb_ref[:, hh * LANES:(hh + 1) * LANES] = (pn[:, o + hh * LANES:o + (hh + 1) * LANES] + aug_b).astype(BF16)
    o += NN_WIDTHS[1]
    qc_ref[...] = pn[:, o:o + NN_WIDTHS[2]].astype(BF16)
    o += NN_WIDTHS[2]
    vc_ref[...] = pn[:, o:o + NN_WIDTHS[3]].astype(BF16)

    r0 = 0
    for ref, b, nr in zip((qta_ref, vta_ref, qtb_ref, vtb_ref, ktc_ref),
                          (qab_ref, None, qbb_ref, None, None), NT_ROWS):
        blk = pt[r0:r0 + nr, :]
        if b is not None:
            blk = blk + b[...]
        blk = blk.astype(BF16)
        for cc in range(tm // KT):
            ref[cc] = blk[:, cc * KT:(cc + 1) * KT]
        r0 += nr


def _inproj(x2d, mod3, g, wn, wt, qa_bias, qb_bias, *, seq, tm=512):
    t, d = x2d.shape
    assert seq % tm == 0 and tm % KT == 0
    nt = t // tm
    nn_total = sum(NN_WIDTHS)
    row_specs = [pl.BlockSpec((tm, wd), lambda i: (i, 0)) for wd in NN_WIDTHS]
    kt_specs = [pl.BlockSpec((tm // KT, r, KT), lambda i: (i, 0, 0)) for r in NT_ROWS]
    out_shape = ([jax.ShapeDtypeStruct((t, wd), BF16) for wd in NN_WIDTHS]
                 + [jax.ShapeDtypeStruct((t // KT, r, KT), BF16) for r in NT_ROWS])
    return pl.pallas_call(
        functools.partial(_inproj_kernel, tm=tm, seq=seq),
        grid=(nt,),
        in_specs=[
            pl.BlockSpec((tm, d), lambda i: (i, 0)),
            pl.BlockSpec((1, 1, mod3.shape[2]), lambda i: ((i * tm) // seq, 0, 0)),
            pl.BlockSpec((1, d), lambda i: (0, 0)),
            pl.BlockSpec((d, nn_total), lambda i: (0, 0)),
            pl.BlockSpec((sum(NT_ROWS), d), lambda i: (0, 0)),
            pl.BlockSpec((NT_ROWS[0], 1), lambda i: (0, 0)),
            pl.BlockSpec((NT_ROWS[2], 1), lambda i: (0, 0)),
        ],
        out_specs=row_specs + kt_specs,
        out_shape=out_shape,
        compiler_params=pltpu.CompilerParams(dimension_semantics=("arbitrary",),
                                             vmem_limit_bytes=VMEM_LIMIT),
        name="inproj",
    )(x2d, mod3, g, wn, wt, qa_bias, qb_bias)


ACC_ROWS = HEAD_DIM + 16


def _softmax_pv(s, vt1, m_ref, acc_ref):
    m_prev = m_ref[...]
    m_new = jnp.maximum(m_prev, jnp.max(s, axis=0, keepdims=True))
    alpha = jnp.exp2(m_prev - m_new)
    p = jnp.exp2((s - m_new).astype(BF16))
    acc_ref[...] = alpha * acc_ref[...] + jnp.dot(vt1, p, preferred_element_type=F32)
    m_ref[...] = m_new


def _flash_init(m_ref, acc_ref):
    m_ref[...] = jnp.full(m_ref.shape, -jnp.inf, F32)
    acc_ref[...] = jnp.zeros(acc_ref.shape, F32)


def _flash_finish(acc_ref, hh):
    acc = acc_ref[hh]
    return acc[0:HEAD_DIM] / acc[HEAD_DIM:HEAD_DIM + 1]


PAST_TILES = 4


def _flash_causal_pair(k_ref, vt_ref, q_diag, q_past, qi, causal, m_ref, acc_ref, s_ref=None):
    _flash_init(m_ref, acc_ref)
    nkeys = PAST_TILES * KT
    nh = len(q_past)

    def vt1(kj, n, hh):
        rows = slice(hh * HEAD_DIM, (hh + 1) * HEAD_DIM)
        vt = vt_ref[kj, rows, :] if n == 1 else jnp.concatenate([vt_ref[kj + c, rows, :] for c in range(n)], axis=1)
        return jnp.concatenate([vt, jnp.ones((ACC_ROWS - HEAD_DIM, n * KT), BF16)], axis=0)

    def scores(kj, n, qs, hh):
        kk = k_ref[pl.ds(pl.multiple_of(kj * KT, KT), n * KT), hh * LANES:(hh + 1) * LANES]
        return jnp.dot(kk, qs[hh], preferred_element_type=F32)

    def step(kj, n, qs, mask=None):
        for hh in range(nh):
            s = scores(kj, n, qs, hh)
            if mask is not None:
                s = jnp.where(mask, s, -jnp.inf)
            _softmax_pv(s, vt1(kj, n, hh), m_ref.at[hh], acc_ref.at[hh])

    step(qi, 1, q_diag, causal)

    if s_ref is None:
        def body(j, carry):
            step(j * PAST_TILES, PAST_TILES, q_past)
            return carry

        lax.fori_loop(0, qi // PAST_TILES, body, 0)
        for r in range(PAST_TILES - 1):
            @pl.when(qi % PAST_TILES > r)
            def _():
                step(qi - 1 - r, 1, q_past)
        return

    ngroups = k_ref.shape[0] // nkeys

    def scores_into(slot, g):
        for hh in range(nh):
            s_ref[slot, hh] = scores(g * PAST_TILES, PAST_TILES, q_past, hh)

    scores_into(0, 0)

    def body(g, carry):
        scores_into((g + 1) % 2, jnp.minimum(g + 1, ngroups - 1))
        for hh in range(nh):
            _softmax_pv(s_ref[g % 2, hh], vt1(g * PAST_TILES, PAST_TILES, hh), m_ref.at[hh], acc_ref.at[hh])
        return carry

    lax.fori_loop(0, (qi + PAST_TILES - 1) // PAST_TILES, body, 0)


def _diff_kernel(qt_ref, k_ref, vt_ref, lamv_ref, sg_ref, o_ref, m_ref, acc_ref, *, lam_init):
    tq = qt_ref.shape[2]
    qi = pl.program_id(2)
    lv = lamv_ref[...]
    lam = (jnp.exp(jnp.sum(lv[0:1] * lv[1:2], axis=-1, keepdims=True))
           - jnp.exp(jnp.sum(lv[2:3] * lv[3:4], axis=-1, keepdims=True)) + lam_init)
    row = lax.broadcasted_iota(I32, (LANES, tq), 0)
    kr = lax.broadcasted_iota(I32, (KT, 2 * tq), 0)
    qc = lax.broadcasted_iota(I32, (KT, 2 * tq), 1)
    causal = kr <= jnp.where(qc >= tq, qc - tq, qc)
    nh = qt_ref.shape[1] // LANES
    qs = []
    for hh in range(nh):
        qt = qt_ref[0, hh * LANES:(hh + 1) * LANES, :]
        zero = jnp.zeros_like(qt)
        q1 = jnp.where((row < A_QK_DIM) | (row >= AUG0), qt, zero)
        q2 = jnp.where(row >= A_QK_DIM, qt, zero)
        qs.append(jnp.concatenate([q1, q2], axis=1))
    _flash_causal_pair(k_ref, vt_ref, qs, qs, qi, causal, m_ref, acc_ref)
    res = []
    for hh in range(nh):
        o = _flash_finish(acc_ref, hh)
        od = o[:, :tq] - lam * o[:, tq:]
        ms = jnp.mean(od * od, axis=0, keepdims=True)
        res.append(od * lax.rsqrt(ms + EPS) * sg_ref[...] * (1.0 - lam_init))
    o_ref[...] = jnp.concatenate(res, axis=0).T.astype(BF16)


def _diff_attention(qta, ka, vta, lamv, sg, *, bsz, seq, lam_init, nh=A_HEADS):
    tq = KT
    nq = seq // tq
    t = bsz * seq
    return pl.pallas_call(
        functools.partial(_diff_kernel, lam_init=lam_init),
        grid=(bsz, A_HEADS // nh, nq),
        in_specs=[
            pl.BlockSpec((1, nh * LANES, tq), lambda b, hp, qi: (b * nq + qi, hp, 0)),
            pl.BlockSpec((seq, nh * LANES), lambda b, hp, qi: (b, hp)),
            pl.BlockSpec((nq, nh * HEAD_DIM, KT), lambda b, hp, qi: (b, hp, 0)),
            pl.BlockSpec((4, A_QK_DIM), lambda b, hp, qi: (0, 0)),
            pl.BlockSpec((HEAD_DIM, 1), lambda b, hp, qi: (0, 0)),
        ],
        out_specs=pl.BlockSpec((tq, nh * HEAD_DIM), lambda b, hp, qi: (b * nq + qi, hp)),
        out_shape=jax.ShapeDtypeStruct((t, A_HEADS * HEAD_DIM), BF16),
        scratch_shapes=[pltpu.VMEM((nh, 1, 2 * tq), F32), pltpu.VMEM((nh, ACC_ROWS, 2 * tq), F32)],
        compiler_params=pltpu.CompilerParams(dimension_semantics=("arbitrary",) * 3,
                                             vmem_limit_bytes=VMEM_LIMIT),
        name="diff_attn",
    )(qta, ka, vta, lamv, sg)


def _moba_kernel(qt_ref, k_ref, vt_ref, o_ref, m_ref, acc_ref, km_ref, s_ref):
    tq = qt_ref.shape[2]
    nh = qt_ref.shape[1] // LANES
    nb = vt_ref.shape[0]
    qi = pl.program_id(2)

    @pl.when(qi == 0)
    def _():
        lane1 = lax.broadcasted_iota(I32, (1, LANES), 1)
        for hh in range(nh):
            km_ref[hh] = jnp.zeros((LANES, LANES), F32)

            def put_block(j, carry):
                blk = k_ref[pl.ds(pl.multiple_of(j * KT, KT), KT), hh * LANES:(hh + 1) * LANES].astype(F32)
                mean = jnp.sum(blk, axis=0, keepdims=True) * (1.0 / KT)
                km_ref[hh, pl.ds(SEL0 + j, 1), :] = jnp.where(lane1 < HEAD_DIM, mean, 0.0)
                return carry
            lax.fori_loop(0, nb, put_block, 0)

    row = lax.broadcasted_iota(I32, (LANES, tq), 0)
    kr = lax.broadcasted_iota(I32, (KT, tq), 0)
    qc = lax.broadcasted_iota(I32, (KT, tq), 1)
    causal = kr <= qc
    in_sel = (row >= SEL0) & (row < SEL0 + MAX_BLOCKS)
    q_diag, q_past = [], []
    for hh in range(nh):
        qt = qt_ref[0, hh * LANES:(hh + 1) * LANES, :]
        km = km_ref[hh]
        km_hi = km.astype(BF16)
        km_lo = (km - km_hi.astype(F32)).astype(BF16)
        gate = (jnp.dot(km_hi, qt, preferred_element_type=F32)
                + jnp.dot(km_lo, qt, preferred_element_type=F32))
        cur = jnp.where((row >= SEL0) & (row < SEL0 + qi), gate, -jnp.inf)
        sel = jnp.zeros((LANES, tq), jnp.bool_)
        for _ in range(MOBA_TOPK):
            mx = jnp.max(cur, axis=0, keepdims=True)
            first = jnp.min(jnp.where(cur == mx, row, 4 * LANES), axis=0, keepdims=True)
            pick = (row == first) & (mx > -jnp.inf)
            sel = sel | pick
            cur = jnp.where(pick, -jnp.inf, cur)
        q_diag.append(jnp.where(in_sel, jnp.zeros_like(qt), qt))
        q_past.append(jnp.where(in_sel, jnp.where(sel, 0.0, NEG_BIG).astype(BF16), qt))
    _flash_causal_pair(k_ref, vt_ref, q_diag, q_past, qi, causal, m_ref, acc_ref, s_ref)
    res = [_flash_finish(acc_ref, hh) for hh in range(nh)]
    o_ref[...] = jnp.concatenate(res, axis=0).T.astype(BF16)


def _moba_attention(qtb, kb, vtb, *, bsz, seq, nh=B_HEADS):
    tq = KT
    nq = seq // tq
    assert nq <= MAX_BLOCKS and seq % (PAST_TILES * KT) == 0
    t = bsz * seq
    return pl.pallas_call(
        _moba_kernel,
        grid=(bsz, B_HEADS // nh, nq),
        in_specs=[
            pl.BlockSpec((1, nh * LANES, tq), lambda b, hp, qi: (b * nq + qi, hp, 0)),
            pl.BlockSpec((seq, nh * LANES), lambda b, hp, qi: (b, hp), pipeline_mode=pl.Buffered(1)),
            pl.BlockSpec((nq, nh * HEAD_DIM, KT), lambda b, hp, qi: (b, hp, 0), pipeline_mode=pl.Buffered(1)),
        ],
        out_specs=pl.BlockSpec((tq, nh * HEAD_DIM), lambda b, hp, qi: (b * nq + qi, hp)),
        out_shape=jax.ShapeDtypeStruct((t, B_HEADS * HEAD_DIM), BF16),
        scratch_shapes=[pltpu.VMEM((nh, 1, tq), F32), pltpu.VMEM((nh, ACC_ROWS, tq), F32),
                        pltpu.VMEM((nh, LANES, LANES), F32), pltpu.VMEM((2, nh, PAST_TILES * KT, tq), F32)],
        compiler_params=pltpu.CompilerParams(dimension_semantics=("arbitrary",) * 3,
                                             vmem_limit_bytes=VMEM_LIMIT),
        name="moba_attn",
    )(qtb, kb, vtb)


def _swa_kernel(sink_ref, q_ref, ktp_ref, ktc_ref, vp_ref, vc_ref, o_ref, *, slopes):
    tq = q_ref.shape[0]
    n = pl.program_id(1)
    lane = lax.broadcasted_iota(I32, (tq, LANES), 1)
    r2 = lax.broadcasted_iota(I32, (tq, 2 * KT), 0)
    c2 = lax.broadcasted_iota(I32, (tq, 2 * KT), 1)
    rel = r2 + KT - c2
    mask = (rel >= 0) & (rel < WINDOW) & ((c2 >= KT) | (n > 0))
    relf = rel.astype(F32)
    res = []
    for hq in range(C_HEADS):
        kv = hq // C_GROUP
        qp = q_ref[:, (hq // 2) * LANES:(hq // 2 + 1) * LANES]
        qm = jnp.where((lane < HEAD_DIM) == (hq % 2 == 0), qp, jnp.zeros_like(qp))
        kt = jnp.concatenate([ktp_ref[0, kv * LANES:(kv + 1) * LANES, :],
                              ktc_ref[0, kv * LANES:(kv + 1) * LANES, :]], axis=1)
        vv = jnp.concatenate([vp_ref[:, kv * LANES:(kv + 1) * LANES],
                              vc_ref[:, kv * LANES:(kv + 1) * LANES]], axis=0)
        s = jnp.dot(qm, kt, preferred_element_type=F32)
        s = jnp.where(mask, s - slopes[hq] * relf, -jnp.inf)
        sink = sink_ref[hq]
        m = jnp.maximum(jnp.max(s, axis=-1, keepdims=True), sink)
        e = jnp.exp(s - m)
        den = jnp.sum(e, axis=-1, keepdims=True) + jnp.exp(sink - m)
        p = (e / den).astype(BF16)
        res.append(jnp.dot(p, vv, preferred_element_type=F32))
    for pr in range(C_HEADS // 2):
        o_ref[:, pr * LANES:(pr + 1) * LANES] = jnp.where(lane < HEAD_DIM, res[2 * pr], res[2 * pr + 1]).astype(BF16)


def _swa_attention(sinks, qc, ktc, vc2, *, bsz, seq, slopes):
    tq = KT
    nq = seq // tq
    t = bsz * seq
    return pl.pallas_call(
        functools.partial(_swa_kernel, slopes=slopes),
        grid=(bsz, nq),
        in_specs=[
            pl.BlockSpec(memory_space=pltpu.SMEM),
            pl.BlockSpec((tq, C_HEADS * HEAD_DIM), lambda b, n: (b * nq + n, 0)),
            pl.BlockSpec((1, C_KV_HEADS * LANES, KT), lambda b, n: (b * nq + jnp.maximum(n - 1, 0), 0, 0)),
            pl.BlockSpec((1, C_KV_HEADS * LANES, KT), lambda b, n: (b * nq + n, 0, 0)),
            pl.BlockSpec((tq, C_KV_HEADS * LANES), lambda b, n: (b * nq + jnp.maximum(n - 1, 0), 0)),
            pl.BlockSpec((tq, C_KV_HEADS * LANES), lambda b, n: (b * nq + n, 0)),
        ],
        out_specs=pl.BlockSpec((tq, C_HEADS * HEAD_DIM), lambda b, n: (b * nq + n, 0)),
        out_shape=jax.ShapeDtypeStruct((t, C_HEADS * HEAD_DIM), BF16),
        compiler_params=pltpu.CompilerParams(dimension_semantics=("arbitrary",) * 2,
                                             vmem_limit_bytes=VMEM_LIMIT),
        name="swa_attn",
    )(sinks, qc, ktc, ktc, vc2, vc2)


def _outproj_kernel(oa_ref, ob_ref, oc_ref, woa_ref, wob_ref, woc_ref, x_ref, mod_ref, g_ref, wq_ref, keys_ref,
                    x1_ref, h2_ref, st_ref):
    d = x_ref.shape[1]
    mix = (jnp.dot(oa_ref[...], woa_ref[...], preferred_element_type=F32)
           + jnp.dot(ob_ref[...], wob_ref[...], preferred_element_type=F32)
           + jnp.dot(oc_ref[...], woc_ref[...], preferred_element_type=F32))
    g1 = mod_ref[0, :, 2 * d:3 * d]
    sh2 = mod_ref[0, :, 3 * d:4 * d]
    sc2 = mod_ref[0, :, 4 * d:5 * d]
    x1 = x_ref[...] + g1 * mix
    x1_ref[...] = x1
    h2 = _rms_mod(x1, g_ref[...], sc2, sh2)
    nj = d // LANES
    for j in range(nj):
        h2_ref[pl.ds(j, x1.shape[0], stride=nj), :] = h2[:, j * LANES:(j + 1) * LANES]
    pq = jnp.dot(h2.astype(BF16), wq_ref[...], preferred_element_type=F32).astype(BF16)
    for hp in range(2 * PEER_HEADS):
        st_ref[hp] = lax.dot_general(keys_ref[hp], pq[:, hp * LANES:(hp + 1) * LANES],
                                     (((1,), (1,)), ((), ())), preferred_element_type=F32)


def _outproj(oa, ob, oc, woa, wob, woc, x2d, mod3, g, wq, keys, *, seq, tm=256):
    t, d = x2d.shape
    nt = t // tm
    nq = wq.shape[1]
    full = lambda a: pl.BlockSpec(a.shape, lambda i: (0,) * a.ndim)
    return pl.pallas_call(
        _outproj_kernel,
        grid=(nt,),
        in_specs=[
            pl.BlockSpec((tm, oa.shape[1]), lambda i: (i, 0)),
            pl.BlockSpec((tm, ob.shape[1]), lambda i: (i, 0)),
            pl.BlockSpec((tm, oc.shape[1]), lambda i: (i, 0)),
            full(woa), full(wob), full(woc),
            pl.BlockSpec((tm, d), lambda i: (i, 0)),
            pl.BlockSpec((1, 1, mod3.shape[2]), lambda i: ((i * tm) // seq, 0, 0)),
            pl.BlockSpec((1, d), lambda i: (0, 0)),
            full(wq), full(keys),
        ],
        out_specs=[pl.BlockSpec((tm, d), lambda i: (i, 0)),
                   pl.BlockSpec((tm * (d // LANES), LANES), lambda i: (i, 0)),
                   pl.BlockSpec((2 * PEER_HEADS, PEER_NKEYS, tm), lambda i: (0, 0, i))],
        out_shape=[jax.ShapeDtypeStruct((t, d), F32), jax.ShapeDtypeStruct((t * (d // LANES), LANES), F32),
                   jax.ShapeDtypeStruct((2 * PEER_HEADS, PEER_NKEYS, t), F32)],
        compiler_params=pltpu.CompilerParams(dimension_semantics=("arbitrary",),
                                             vmem_limit_bytes=VMEM_LIMIT),
        name="outproj_peerq",
    )(oa, ob, oc, woa, wob, woc, x2d, mod3, g, wq, keys)


_CAND_BLOCKS = ((0, 16),) + tuple((i, 8) for i in range(1, 8))
_CAND_ROWS = 16 + 7 * 8 + 8
_BIG_I = np.int32(2 ** 30)


TOPK_TL = LANES


def _topk_rows(curs, keys, val_refs, key_refs, unroll):
    def body(r, curs):
        out = []
        for cur, key, val_ref, key_ref in zip(curs, keys, val_refs, key_refs):
            m = jnp.max(cur, axis=0, keepdims=True)
            kmin = jnp.min(jnp.where(cur == m, key, _BIG_I), axis=0, keepdims=True)
            val_ref[pl.ds(r, 1), :] = m
            key_ref[pl.ds(r, 1), :] = kmin
            out.append(jnp.where(key == kmin, -jnp.inf, cur))
        return tuple(out)
    lax.fori_loop(0, PEER_TOPK, body, tuple(curs), unroll=unroll)


def _topk_scratch():
    pair = [pltpu.VMEM((2, PEER_TOPK, TOPK_TL), F32), pltpu.VMEM((2, PEER_TOPK, TOPK_TL), I32)]
    hk = PEER_HEADS * PEER_TOPK
    return pair * 3 + [pltpu.VMEM((hk, TOPK_TL), I32), pltpu.VMEM((hk, TOPK_TL), F32)]


def _topk_head_pair(st_ref, h0, scratch, unroll=False):
    v1_ref, k1_ref, v2_ref, k2_ref, vt_ref, kt_ref, ei_ref, gg_ref = scratch
    tl = st_ref.shape[2]
    row = lax.broadcasted_iota(I32, (PEER_NKEYS, tl), 0)
    jrow8 = lax.broadcasted_iota(I32, (8, tl), 0)
    jrow16 = lax.broadcasted_iota(I32, (16, tl), 0)
    cand, ckey = [], []
    for s in range(2):
        h = h0 + s
        _topk_rows([st_ref[2 * h], st_ref[2 * h + 1]], [row, row],
                   [v1_ref.at[s], v2_ref.at[s]], [k1_ref.at[s], k2_ref.at[s]], unroll)
        sv1, si1, sv2, si2 = v1_ref[s], k1_ref[s], v2_ref[s], k2_ref[s]
        cands, keys = [], []
        for i, nj in _CAND_BLOCKS:
            jrow = jrow16 if nj == 16 else jrow8
            cands.append(sv1[i:i + 1] + sv2[0:nj])
            keys.append((i * PEER_TOPK + jrow) * (PEER_NKEYS * PEER_NKEYS)
                        + si1[i:i + 1] * PEER_NKEYS + si2[0:nj])
        cands.append(sv1[8:16] + sv2[0:1])
        keys.append((jrow8 + 8) * (PEER_TOPK * PEER_NKEYS * PEER_NKEYS) + si1[8:16] * PEER_NKEYS + si2[0:1])
        cand.append(jnp.concatenate(cands, axis=0))
        ckey.append(jnp.concatenate(keys, axis=0))
    _topk_rows(cand, ckey, [vt_ref.at[0], vt_ref.at[1]], [kt_ref.at[0], kt_ref.at[1]], unroll)
    for s in range(2):
        rows = pl.ds(pl.multiple_of((h0 + s) * PEER_TOPK, PEER_TOPK), PEER_TOPK)
        top = vt_ref[s]
        e = jnp.exp(top - top[0:1])
        gg_ref[rows, :] = e / jnp.sum(e, axis=0, keepdims=True)
        ei_ref[rows, :] = kt_ref[s] & (PEER_NKEYS * PEER_NKEYS - 1)


def _topk_emit(scratch, off_ref, par_ref, g_ref):
    ei = scratch[6][...].T
    off_ref[...] = (ei & (PEER_HALF - 1)) * SUB
    par_ref[...] = ei // PEER_HALF
    g_ref[...] = scratch[7][...].T


def _peer_topk_kernel(st_ref, off_ref, par_ref, g_ref, *scratch):
    for h0 in range(0, PEER_HEADS, 2):
        _topk_head_pair(st_ref, h0, scratch)
    _topk_emit(scratch, off_ref, par_ref, g_ref)


def _peer_topk(st, blk0, nblk):
    hk = PEER_HEADS * PEER_TOPK
    t = nblk * TOPK_TL
    return pl.pallas_call(
        _peer_topk_kernel,
        grid=(nblk,),
        in_specs=[pl.BlockSpec((2 * PEER_HEADS, PEER_NKEYS, TOPK_TL), lambda i: (0, 0, blk0 + i))],
        out_specs=[pl.BlockSpec((TOPK_TL, hk), lambda i: (i, 0))] * 3,
        out_shape=[jax.ShapeDtypeStruct((t, hk), I32), jax.ShapeDtypeStruct((t, hk), I32),
                   jax.ShapeDtypeStruct((t, hk), F32)],
        scratch_shapes=_topk_scratch(),
        compiler_params=pltpu.CompilerParams(dimension_semantics=("arbitrary",),
                                             vmem_limit_bytes=VMEM_LIMIT),
        name="peer_topk",
    )(st)


SUB = 8


def _erf(x):
    return lax.erf(x)


HALF_HI = 0xFFFF0000


PEER_HALF = PEER_NKEYS * PEER_NKEYS // 2


def _pack_kernel(lo_ref, hi_ref, o_ref):
    pairs = lo_ref.shape[0]
    nj = lo_ref.shape[1] // LANES
    for j in range(nj):
        cols = slice(j * LANES, (j + 1) * LANES)
        lo = pltpu.bitcast(lo_ref[:, cols].astype(BF16).astype(F32), jnp.uint32)
        hi = pltpu.bitcast(hi_ref[:, cols].astype(BF16).astype(F32), jnp.uint32)
        o_ref[pl.ds(j, pairs, stride=nj), :] = (lo >> 16) | (hi & jnp.uint32(HALF_HI))


def _pack_expert_table(w, pairs=256):
    n, d = w.shape
    nj = d // LANES
    half_blocks = n // 2 // pairs
    return pl.pallas_call(
        _pack_kernel,
        grid=(half_blocks,),
        in_specs=[pl.BlockSpec((pairs, d), lambda i: (i, 0)),
                  pl.BlockSpec((pairs, d), lambda i: (half_blocks + i, 0))],
        out_specs=pl.BlockSpec((pairs * nj, LANES), lambda i: (i, 0)),
        out_shape=jax.ShapeDtypeStruct((n // 2 * nj, LANES), jnp.uint32),
        compiler_params=pltpu.CompilerParams(dimension_semantics=("arbitrary",),
                                             vmem_limit_bytes=VMEM_LIMIT),
        name="pack_table",
    )(w, w)


def _load_table_once(tab_hbm, tab, sem):
    @pl.when(pl.program_id(0) == 0)
    def _():
        cp = pltpu.make_async_copy(tab_hbm, tab, sem)
        cp.start()
        cp.wait()


def _pair_tile(tab, off):
    return tab[pl.ds(pl.multiple_of(off, SUB), SUB), :]


def _group_matrix(rows, cols):
    r = lax.broadcasted_iota(I32, (rows, cols), 0)
    c = lax.broadcasted_iota(I32, (rows, cols), 1)
    return (c // (cols // rows) == r).astype(BF16)


def _split2(x):
    hi = x.astype(BF16)
    return hi, (x - hi.astype(F32)).astype(BF16)


PAIR_ROWS = 2 * SUB


def _pair_rows(tab, off_ref, tt, hk):
    return jnp.concatenate([pltpu.bitcast(_pair_tile(tab, off_ref[tt * hk + k]), BF16) for k in range(hk)], axis=0)


def _own_sublane_mask(hk):
    srow = lax.broadcasted_iota(I32, (SUB, hk * PAIR_ROWS), 0)
    scol = lax.broadcasted_iota(I32, (SUB, hk * PAIR_ROWS), 1)
    return (scol % PAIR_ROWS) // 2 == srow


def _fold_matrix(hk):
    c = jnp.arange(hk * PAIR_ROWS)
    return jax.nn.one_hot((c % 2) * hk + c // PAIR_ROWS, 2 * hk, dtype=BF16)


def _peer_u_body(off_ref, par_ref, h_ref, g_ref, fold_ref, w_ref, tab, d_sc):
    tq, hk = g_ref.shape
    mine = _own_sublane_mask(hk)
    nt = (((1,), (1,)), ((), ()))
    for tt in range(tq):
        h_hi, h_lo = _split2(h_ref[tt])
        s = lax.dot_general(jnp.concatenate([h_hi, h_lo], axis=0), _pair_rows(tab, off_ref, tt, hk), nt,
                            preferred_element_type=F32)
        d = jnp.where(mine, s[0:SUB] + s[SUB:2 * SUB], 0.0)
        d_sc[tt:tt + 1, :] = jnp.sum(d, axis=0, keepdims=True)
    d_hi, d_lo = _split2(d_sc[...])
    a2 = jnp.dot(jnp.concatenate([d_hi, d_lo], axis=0), fold_ref[...], preferred_element_type=F32)
    a2 = a2[0:tq] + a2[tq:2 * tq]
    a = jnp.where(par_ref[...] == 0, a2[:, :hk], a2[:, hk:])
    w_ref[...] = g_ref[...] * (0.5 * a * (1.0 + _erf(a * (2.0 ** -0.5))))


U_TQ = 64


def _peer_u_kernel(off_ref, par_ref, h_ref, g_ref, fold_ref, tab_hbm, w_ref, tab, sem, d_sc):
    _load_table_once(tab_hbm, tab, sem)
    _peer_u_body(off_ref, par_ref, h_ref, g_ref, fold_ref, w_ref, tab, d_sc)


def _peer_u_topk_kernel(off_ref, par_ref, h_ref, g_ref, fold_ref, st_ref, tab_hbm,
                        w_ref, off2_ref, par2_ref, g2_ref, tab, sem, d_sc, *scratch):
    half = pl.program_id(0) % 2
    _load_table_once(tab_hbm, tab, sem)
    for q in range(2):
        _topk_head_pair(st_ref, (2 * half + q) * 2, scratch, unroll=True)
    _peer_u_body(off_ref, par_ref, h_ref, g_ref, fold_ref, w_ref, tab, d_sc)

    @pl.when(half == 1)
    def _():
        _topk_emit(scratch, off2_ref, par2_ref, g2_ref)


def _peer_u(off_flat, par, g, h2, utab, tok0, st=None, st_blk0=None):
    t, hk = g.shape
    nj = h2.shape[1]
    tq = U_TQ
    assert t % (2 * tq) == 0 and 2 * tq == TOPK_TL and tok0 % tq == 0
    fold = _fold_matrix(hk)
    in_specs = [
        pl.BlockSpec((tq * hk,), lambda i: (i,), memory_space=pltpu.SMEM),
        pl.BlockSpec((tq, hk), lambda i: (i, 0)),
        pl.BlockSpec((tq, nj, LANES), lambda i: (tok0 // tq + i, 0, 0)),
        pl.BlockSpec((tq, hk), lambda i: (i, 0)),
        pl.BlockSpec(fold.shape, lambda i: (0, 0)),
    ]
    out_specs = [pl.BlockSpec((tq, hk), lambda i: (i, 0))]
    out_shape = [jax.ShapeDtypeStruct((t, hk), F32)]
    scratch = [pltpu.VMEM(utab.shape, jnp.uint32), pltpu.SemaphoreType.DMA(()),
               pltpu.VMEM((tq, hk * PAIR_ROWS), F32)]
    args = [off_flat, par, h2, g, fold]
    if st is not None:
        in_specs.append(pl.BlockSpec((2 * PEER_HEADS, PEER_NKEYS, TOPK_TL), lambda i: (0, 0, st_blk0 + i // 2)))
        out_specs += [pl.BlockSpec((TOPK_TL, hk), lambda i: (i // 2, 0))] * 3
        out_shape += [jax.ShapeDtypeStruct((t, hk), I32), jax.ShapeDtypeStruct((t, hk), I32),
                      jax.ShapeDtypeStruct((t, hk), F32)]
        scratch += _topk_scratch()
        args.append(st)
    in_specs.append(pl.BlockSpec(memory_space=pl.ANY))
    args.append(utab)
    return pl.pallas_call(
        _peer_u_kernel if st is None else _peer_u_topk_kernel,
        grid=(t // tq,),
        in_specs=in_specs,
        out_specs=out_specs,
        out_shape=out_shape,
        scratch_shapes=scratch,
        compiler_params=pltpu.CompilerParams(dimension_semantics=("arbitrary",),
                                             vmem_limit_bytes=VMEM_LIMIT),
        name="peer_u" if st is None else "peer_u_topk",
    )(*args)


def _peer_v_kernel(off_ref, par_ref, w_ref, x1_ref, mod_ref, fg_ref, tab_hbm, o_ref, tab, sem, wl_sc, p_sc, *, final):
    tq, hk = w_ref.shape
    d = x1_ref.shape[1]
    nj = d // LANES
    _load_table_once(tab_hbm, tab, sem)
    width = hk * PAIR_ROWS
    rep = _group_matrix(hk, width)
    w_hi, w_lo = _split2(w_ref[...])
    parl = jnp.dot(par_ref[...].astype(BF16), rep, preferred_element_type=F32)
    lane = lax.broadcasted_iota(I32, (tq, width), 1)
    wanted = (lane % 2).astype(F32) == parl
    wl_sc[0] = jnp.where(wanted, jnp.dot(w_hi, rep, preferred_element_type=F32), 0.0)
    wl_sc[1] = jnp.where(wanted, jnp.dot(w_lo, rep, preferred_element_type=F32), 0.0)
    mine = _own_sublane_mask(hk)
    for tt in range(tq):
        lhs = jnp.concatenate([jnp.where(mine, wl_sc[0, tt:tt + 1, :], 0.0),
                               jnp.where(mine, wl_sc[1, tt:tt + 1, :], 0.0)], axis=0).astype(BF16)
        r = jnp.dot(lhs, _pair_rows(tab, off_ref, tt, hk), preferred_element_type=F32)
        p_sc[tt * SUB:(tt + 1) * SUB, :] = r[0:SUB] + r[SUB:2 * SUB]
    peer = jnp.concatenate([p_sc[pl.ds(j, tq, stride=nj), :] for j in range(nj)], axis=1)
    y = x1_ref[...] + mod_ref[0, :, 5 * d:6 * d] * peer
    if final:
        y = y * lax.rsqrt(jnp.mean(y * y, axis=-1, keepdims=True) + EPS) * fg_ref[...]
    o_ref[...] = y


def _peer_v(off_flat, par, w, x1, mod3, final_g, vtab, *, seq, final, tq=64):
    t, d = x1.shape
    hk = w.shape[1]
    return pl.pallas_call(
        functools.partial(_peer_v_kernel, final=final),
        grid=(t // tq,),
        in_specs=[
            pl.BlockSpec((tq * hk,), lambda i: (i,), memory_space=pltpu.SMEM),
            pl.BlockSpec((tq, hk), lambda i: (i, 0)),
            pl.BlockSpec((tq, hk), lambda i: (i, 0)),
            pl.BlockSpec((tq, d), lambda i: (i, 0)),
            pl.BlockSpec((1, 1, mod3.shape[2]), lambda i: ((i * tq) // seq, 0, 0)),
            pl.BlockSpec((1, d), lambda i: (0, 0)),
            pl.BlockSpec(memory_space=pl.ANY),
        ],
        out_specs=pl.BlockSpec((tq, d), lambda i: (i, 0)),
        out_shape=jax.ShapeDtypeStruct((t, d), F32),
        scratch_shapes=[pltpu.VMEM(vtab.shape, jnp.uint32), pltpu.SemaphoreType.DMA(()),
                        pltpu.VMEM((2, tq, hk * PAIR_ROWS), F32), pltpu.VMEM((tq * (d // LANES), LANES), F32)],
        compiler_params=pltpu.CompilerParams(dimension_semantics=("arbitrary",),
                                             vmem_limit_bytes=VMEM_LIMIT),
        name="peer_v",
    )(off_flat, par, w, x1, mod3, final_g, vtab)


def kernel(x, c, norm1_g, norm2_g, w_ada, b_ada, w_in, w_out, lam_q1, lam_k1, lam_q2, lam_k2, subln_g, sinks,
           peer_wq, peer_keys, peer_u, peer_v, final_g):
    bsz, seq, d = x.shape
    depth = w_in.shape[0]
    t = bsz * seq
    slopes = _alibi_slopes()
    sl_c = [float(s) for s in slopes[:C_HEADS]]
    qa_bias = _slope_bias_col(slopes[C_HEADS:C_HEADS + A_HEADS] * np.float32(LOG2E))
    qb_bias = _slope_bias_col(slopes[C_HEADS + A_HEADS:] * np.float32(LOG2E))
    mods = _adaln_mods(c, w_ada, b_ada)
    x2d = x.reshape(t, d)
    av, bw = A_HEADS * HEAD_DIM, B_HEADS * HEAD_DIM
    for l in range(depth):
        lam_init = 0.8 - 0.6 * math.exp(-0.3 * l)
        mod3 = mods[l].reshape(bsz, 1, N_MOD * d)
        wn, wt = _prep_in_weights(w_in[l])
        ka, kb, qc, vc2, qta, vta, qtb, vtb, ktc = _inproj(
            x2d, mod3, norm1_g[l].reshape(1, d), wn, wt, qa_bias, qb_bias, seq=seq)
        lamv = jnp.stack([lam_q1[l], lam_k1[l], lam_q2[l], lam_k2[l]]).astype(F32)
        sg = subln_g[l].reshape(HEAD_DIM, 1).astype(F32)
        oa = _diff_attention(qta, ka, vta, lamv, sg, bsz=bsz, seq=seq, lam_init=lam_init)
        ob = _moba_attention(qtb, kb, vtb, bsz=bsz, seq=seq)
        oc = _swa_attention(sinks[l].astype(F32), qc, ktc, vc2, bsz=bsz, seq=seq, slopes=sl_c)
        wo = w_out[l].astype(BF16)
        keys = peer_keys[l].reshape(2 * PEER_HEADS, PEER_NKEYS, -1).astype(BF16)
        x1, h2, st = _outproj(oa, ob, oc, wo[:av], wo[av:av + bw], wo[av + bw:], x2d, mod3,
                              norm2_g[l].reshape(1, d), peer_wq[l].astype(BF16), keys, seq=seq)
        assert seq % TOPK_TL == 0
        blocks = seq // TOPK_TL
        utab = _pack_expert_table(peer_u[l])
        h2t = h2.reshape(t, d // LANES, LANES)
        sel = [_peer_topk(st, 0, blocks)]
        ws = []
        for b in range(bsz):
            off_b, par_b, g_b = sel[b]
            if b + 1 < bsz:
                w_b, *nxt = _peer_u(off_b.reshape(-1), par_b, g_b, h2t, utab, b * seq, st, (b + 1) * blocks)
                sel.append(nxt)
            else:
                (w_b,) = _peer_u(off_b.reshape(-1), par_b, g_b, h2t, utab, b * seq)
            ws.append(w_b)
        toff = jnp.concatenate([s_[0] for s_ in sel]).reshape(-1)
        par = jnp.concatenate([s_[1] for s_ in sel])
        w = jnp.concatenate(ws)
        x2d = _peer_v(toff, par, w, x1, mod3, final_g.reshape(1, d), _pack_expert_table(peer_v[l]),
                      seq=seq, final=(l == depth - 1))
    return x2d.reshape(bsz, seq, d)
```

```python
import functools
import math

import numpy as np
import jax
import jax.numpy as jnp
from jax import lax
from jax.experimental import pallas as pl
from jax.experimental.pallas import tpu as pltpu

F32 = jnp.float32
BF16 = jnp.bfloat16
I32 = jnp.int32

D_MODEL = 1024
HEAD_DIM = 64
N_HEADS_TOTAL = 16
A_HEADS = 4
B_HEADS = 6
C_HEADS = 6
C_KV_HEADS = 2
C_GROUP = 3
A_QK_DIM = 32
MOBA_BLOCK = 256
MOBA_TOPK = 3
WINDOW = 128
ALIBI_MAX = 8.0
PEER_HEADS = 8
PEER_NKEYS = 128
PEER_TOPK = 16
N_MOD = 6
EPS = 1e-6

LANES = 128
KT = 256
AUG0 = HEAD_DIM
SEL0 = AUG0 + 6
MAX_BLOCKS = LANES - SEL0
NEG_BIG = -1e30
LOG2E = math.log2(math.e)
VMEM_LIMIT = 56 * 1024 * 1024


def _alibi_slopes():
    n = N_HEADS_TOTAL
    return (2.0 ** (-ALIBI_MAX * np.arange(1, n + 1, dtype=np.float32) / n)).astype(np.float32)


def _split3(v):
    v = np.float32(v)
    hi = np.float32(np.asarray(v).astype(jnp.bfloat16).astype(np.float32))
    r = np.float32(v - hi)
    mid = np.float32(np.asarray(r).astype(jnp.bfloat16).astype(np.float32))
    lo = np.float32(np.float32(r - mid))
    lo = np.float32(np.asarray(lo).astype(jnp.bfloat16).astype(np.float32))
    return hi, mid, lo


def _slope_bias_col(slopes):
    col = np.zeros((LANES * len(slopes), 1), np.float32)
    for h, s in enumerate(slopes):
        hi, mid, lo = _split3(s)
        col[h * LANES + AUG0:h * LANES + AUG0 + 6, 0] = [hi, mid, lo, hi, mid, lo]
    return jnp.asarray(col)


def _mod_kernel(c_ref, w_ref, b_ref, o_ref):
    c = c_ref[...]
    cs = c * (1.0 / (1.0 + jnp.exp(-c)))
    o_ref[0] = jnp.dot(cs, w_ref[0], preferred_element_type=F32) + b_ref[0]


def _adaln_mods(c, w_ada, b_ada):
    depth, d, n = w_ada.shape
    bsz = c.shape[0]
    rows = -(-bsz // 8) * 8
    cp = jnp.pad(c, ((0, rows - bsz), (0, 0)))
    tn = 1536
    out = pl.pallas_call(
        _mod_kernel,
        grid=(depth, n // tn),
        in_specs=[
            pl.BlockSpec((rows, d), lambda l, j: (0, 0)),
            pl.BlockSpec((1, d, tn), lambda l, j: (l, 0, j)),
            pl.BlockSpec((1, 1, tn), lambda l, j: (l, 0, j)),
        ],
        out_specs=pl.BlockSpec((1, rows, tn), lambda l, j: (l, 0, j)),
        out_shape=jax.ShapeDtypeStruct((depth, rows, n), F32),
        compiler_params=pltpu.CompilerParams(dimension_semantics=("arbitrary", "arbitrary"),
                                             vmem_limit_bytes=VMEM_LIMIT),
        name="adaln_mods",
    )(cp, w_ada, b_ada.reshape(depth, 1, n))
    return out[:, :bsz]


NN_WIDTHS = (A_HEADS * LANES, B_HEADS * LANES, C_HEADS * HEAD_DIM, 2 * C_KV_HEADS * HEAD_DIM)
NT_ROWS = (A_HEADS * LANES, A_HEADS * HEAD_DIM, B_HEADS * LANES, B_HEADS * HEAD_DIM,
           C_KV_HEADS * LANES)


def _prep_in_weights(w):
    d = w.shape[0]
    aq, ak, av = A_HEADS * 2 * A_QK_DIM, A_HEADS * 2 * A_QK_DIM, A_HEADS * HEAD_DIM
    bw = B_HEADS * HEAD_DIM
    cq, ckv = C_HEADS * HEAD_DIM, C_KV_HEADS * HEAD_DIM
    cuts = np.cumsum([aq, ak, av, bw, bw, bw, cq, ckv]).tolist()
    qa, ka, va, qb, kb, vb, qc, kc, vc = jnp.split(w, cuts, axis=-1)

    def pad_heads(m, nh, scale):
        m = (m * scale).reshape(d, nh, HEAD_DIM)
        return jnp.pad(m, ((0, 0), (0, 0), (0, LANES - HEAD_DIM))).reshape(d, nh * LANES)

    vc2 = vc.reshape(d, C_KV_HEADS, 1, HEAD_DIM)
    vc2 = jnp.broadcast_to(vc2, (d, C_KV_HEADS, 2, HEAD_DIM)).reshape(d, 2 * ckv)
    kc2 = jnp.broadcast_to(kc.reshape(d, C_KV_HEADS, 1, HEAD_DIM), (d, C_KV_HEADS, 2, HEAD_DIM)).reshape(d, 2 * ckv)
    wn = jnp.concatenate([pad_heads(ka, A_HEADS, 1.0), pad_heads(kb, B_HEADS, 1.0),
                          qc * (HEAD_DIM ** -0.5), vc2], axis=1)
    wt = jnp.concatenate([pad_heads(qa, A_HEADS, A_QK_DIM ** -0.5 * LOG2E), va,
                          pad_heads(qb, B_HEADS, HEAD_DIM ** -0.5 * LOG2E), vb, kc2], axis=1).T
    return wn.astype(BF16), wt.astype(BF16)


def _rms_mod(x, g, sc, sh):
    ms = jnp.mean(x * x, axis=-1, keepdims=True)
    return (x * lax.rsqrt(ms + EPS) * g) * (1.0 + sc) + sh


def _inproj_kernel(x_ref, mod_ref, g_ref, wn_ref, wt_ref, qab_ref, qbb_ref,
                   ka_ref, kb_ref, qc_ref, vc_ref, qta_ref, vta_ref, qtb_ref, vtb_ref, ktc_ref,
                   *, tm, seq):
    d = x_ref.shape[1]
    x = x_ref[...]
    sh = mod_ref[0, :, 0:d]
    sc = mod_ref[0, :, d:2 * d]
    h = _rms_mod(x, g_ref[...], sc, sh).astype(BF16)
    pn = jnp.dot(h, wn_ref[...], preferred_element_type=F32)
    pt = lax.dot_general(wt_ref[...], h, (((1,), (1,)), ((), ())),
                         preferred_element_type=F32)

    pos = (pl.program_id(0) * tm) % seq + lax.broadcasted_iota(I32, (tm, LANES), 0)
    col = lax.broadcasted_iota(I32, (tm, LANES), 1)
    blk_id = pos // KT
    p_hi = (blk_id * KT).astype(F32)
    p_lo = (pos - blk_id * KT).astype(F32)
    aug_a = jnp.where((col >= AUG0) & (col < AUG0 + 3), p_hi,
                      jnp.where((col >= AUG0 + 3) & (col < AUG0 + 6), p_lo, 0.0))
    aug_b = jnp.where((col >= SEL0) & (col - SEL0 == blk_id), 1.0, aug_a)
    for hh in range(A_HEADS):
        ka_ref[:, hh * LANES:(hh + 1) * LANES] = (pn[:, hh * LANES:(hh + 1) * LANES] + aug_a).astype(BF16)
    o = NN_WIDTHS[0]
    for hh in range(B_HEADS):
        kb_ref[:, hh * LANES:(hh + 1) * LANES] = (pn[:, o + hh * LANES:o + (hh + 1) * LANES] + aug_b).astype(BF16)
    o += NN_WIDTHS[1]
    qc_ref[...] = pn[:, o:o + NN_WIDTHS[2]].astype(BF16)
    o += NN_WIDTHS[2]
    vc_ref[...] = pn[:, o:o + NN_WIDTHS[3]].astype(BF16)

    r0 = 0
    for ref, b, nr in zip((qta_ref, vta_ref, qtb_ref, vtb_ref, ktc_ref),
                          (qab_ref, None, qbb_ref, None, None), NT_ROWS):
        blk = pt[r0:r0 + nr, :]
        if b is not None:
            blk = blk + b[...]
        blk = blk.astype(BF16)
        for cc in range(tm // KT):
            ref[cc] = blk[:, cc * KT:(cc + 1) * KT]
        r0 += nr


def _inproj(x2d, mod3, g, wn, wt, qa_bias, qb_bias, *, seq, tm=512):
    t, d = x2d.shape
    assert seq % tm == 0 and tm % KT == 0
    nt = t // tm
    nn_total = sum(NN_WIDTHS)
    row_specs = [pl.BlockSpec((tm, wd), lambda i: (i, 0)) for wd in NN_WIDTHS]
    kt_specs = [pl.BlockSpec((tm // KT, r, KT), lambda i: (i, 0, 0)) for r in NT_ROWS]
    out_shape = ([jax.ShapeDtypeStruct((t, wd), BF16) for wd in NN_WIDTHS]
                 + [jax.ShapeDtypeStruct((t // KT, r, KT), BF16) for r in NT_ROWS])
    return pl.pallas_call(
        functools.partial(_inproj_kernel, tm=tm, seq=seq),
        grid=(nt,),
        in_specs=[
            pl.BlockSpec((tm, d), lambda i: (i, 0)),
            pl.BlockSpec((1, 1, mod3.shape[2]), lambda i: ((i * tm) // seq, 0, 0)),
            pl.BlockSpec((1, d), lambda i: (0, 0)),
            pl.BlockSpec((d, nn_total), lambda i: (0, 0)),
            pl.BlockSpec((sum(NT_ROWS), d), lambda i: (0, 0)),
            pl.BlockSpec((NT_ROWS[0], 1), lambda i: (0, 0)),
            pl.BlockSpec((NT_ROWS[2], 1), lambda i: (0, 0)),
        ],
        out_specs=row_specs + kt_specs,
        out_shape=out_shape,
        compiler_params=pltpu.CompilerParams(dimension_semantics=("arbitrary",),
                                             vmem_limit_bytes=VMEM_LIMIT),
        name="inproj",
    )(x2d, mod3, g, wn, wt, qa_bias, qb_bias)


ACC_ROWS = HEAD_DIM + 16


def _softmax_pv(ss, vt1s, m_ref, acc_ref):
    nh = len(ss)
    m_prev = [m_ref[hh] for hh in range(nh)]
    m_new = [jnp.maximum(m_prev[hh], jnp.max(ss[hh], axis=0, keepdims=True)) for hh in range(nh)]
    ps = [jnp.exp2((ss[hh] - m_new[hh]).astype(BF16)) for hh in range(nh)]
    pv = [jnp.dot(vt1s[hh], ps[hh], preferred_element_type=F32) for hh in range(nh)]
    for hh in range(nh):
        acc_ref[hh] = jnp.exp2(m_prev[hh] - m_new[hh]) * acc_ref[hh] + pv[hh]
        m_ref[hh] = m_new[hh]


def _flash_init(m_ref, acc_ref):
    m_ref[...] = jnp.full(m_ref.shape, -jnp.inf, F32)
    acc_ref[...] = jnp.zeros(acc_ref.shape, F32)


def _flash_finish(acc_ref, hh):
    acc = acc_ref[hh]
    return acc[0:HEAD_DIM] / acc[HEAD_DIM:HEAD_DIM + 1]


PAST_TILES = 4


def _flash_causal_pair(k_ref, vt_ref, q_diag, q_past, qi, causal, m_ref, acc_ref, s_ref=None):
    _flash_init(m_ref, acc_ref)
    nkeys = PAST_TILES * KT
    nh = len(q_past)

    def vt1(kj, n, hh):
        rows = slice(hh * HEAD_DIM, (hh + 1) * HEAD_DIM)
        vt = vt_ref[kj, rows, :] if n == 1 else jnp.concatenate([vt_ref[kj + c, rows, :] for c in range(n)], axis=1)
        return jnp.concatenate([vt, jnp.ones((ACC_ROWS - HEAD_DIM, n * KT), BF16)], axis=0)

    def scores(kj, n, qs, hh):
        kk = k_ref[pl.ds(pl.multiple_of(kj * KT, KT), n * KT), hh * LANES:(hh + 1) * LANES]
        return jnp.dot(kk, qs[hh], preferred_element_type=F32)

    def step(kj, n, qs, mask=None):
        ss = [scores(kj, n, qs, hh) for hh in range(nh)]
        if mask is not None:
            ss = [jnp.where(mask, s, -jnp.inf) for s in ss]
        _softmax_pv(ss, [vt1(kj, n, hh) for hh in range(nh)], m_ref, acc_ref)

    step(qi, 1, q_diag, causal)

    if s_ref is None:
        def body(j, carry):
            step(j * PAST_TILES, PAST_TILES, q_past)
            return carry

        lax.fori_loop(0, qi // PAST_TILES, body, 0)
        for r in range(PAST_TILES - 1):
            @pl.when(qi % PAST_TILES > r)
            def _():
                step(qi - 1 - r, 1, q_past)
        return

    ngroups = k_ref.shape[0] // nkeys

    def scores_into(slot, g):
        for hh in range(nh):
            s_ref[slot, hh] = scores(g * PAST_TILES, PAST_TILES, q_past, hh)

    scores_into(0, 0)

    def body(g, carry):
        scores_into((g + 1) % 2, jnp.minimum(g + 1, ngroups - 1))
        _softmax_pv([s_ref[g % 2, hh] for hh in range(nh)],
                    [vt1(g * PAST_TILES, PAST_TILES, hh) for hh in range(nh)], m_ref, acc_ref)
        return carry

    lax.fori_loop(0, (qi + PAST_TILES - 1) // PAST_TILES, body, 0)


def _diff_kernel(qt_ref, k_ref, vt_ref, lamv_ref, sg_ref, o_ref, m_ref, acc_ref, *, lam_init):
    tq = qt_ref.shape[2]
    qi = pl.program_id(2)
    lv = lamv_ref[...]
    lam = (jnp.exp(jnp.sum(lv[0:1] * lv[1:2], axis=-1, keepdims=True))
           - jnp.exp(jnp.sum(lv[2:3] * lv[3:4], axis=-1, keepdims=True)) + lam_init)
    row = lax.broadcasted_iota(I32, (LANES, tq), 0)
    kr = lax.broadcasted_iota(I32, (KT, 2 * tq), 0)
    qc = lax.broadcasted_iota(I32, (KT, 2 * tq), 1)
    causal = kr <= jnp.where(qc >= tq, qc - tq, qc)
    nh = qt_ref.shape[1] // LANES
    qs = []
    for hh in range(nh):
        qt = qt_ref[0, hh * LANES:(hh + 1) * LANES, :]
        zero = jnp.zeros_like(qt)
        q1 = jnp.where((row < A_QK_DIM) | (row >= AUG0), qt, zero)
        q2 = jnp.where(row >= A_QK_DIM, qt, zero)
        qs.append(jnp.concatenate([q1, q2], axis=1))
    _flash_causal_pair(k_ref, vt_ref, qs, qs, qi, causal, m_ref, acc_ref)
    res = []
    for hh in range(nh):
        o = _flash_finish(acc_ref, hh)
        od = o[:, :tq] - lam * o[:, tq:]
        ms = jnp.mean(od * od, axis=0, keepdims=True)
        res.append(od * lax.rsqrt(ms + EPS) * sg_ref[...] * (1.0 - lam_init))
    o_ref[...] = jnp.concatenate(res, axis=0).T.astype(BF16)


def _diff_attention(qta, ka, vta, lamv, sg, *, bsz, seq, lam_init, nh=A_HEADS):
    tq = KT
    nq = seq // tq
    t = bsz * seq
    return pl.pallas_call(
        functools.partial(_diff_kernel, lam_init=lam_init),
        grid=(bsz, A_HEADS // nh, nq),
        in_specs=[
            pl.BlockSpec((1, nh * LANES, tq), lambda b, hp, qi: (b * nq + qi, hp, 0)),
            pl.BlockSpec((seq, nh * LANES), lambda b, hp, qi: (b, hp)),
            pl.BlockSpec((nq, nh * HEAD_DIM, KT), lambda b, hp, qi: (b, hp, 0)),
            pl.BlockSpec((4, A_QK_DIM), lambda b, hp, qi: (0, 0)),
            pl.BlockSpec((HEAD_DIM, 1), lambda b, hp, qi: (0, 0)),
        ],
        out_specs=pl.BlockSpec((tq, nh * HEAD_DIM), lambda b, hp, qi: (b * nq + qi, hp)),
        out_shape=jax.ShapeDtypeStruct((t, A_HEADS * HEAD_DIM), BF16),
        scratch_shapes=[pltpu.VMEM((nh, 1, 2 * tq), F32), pltpu.VMEM((nh, ACC_ROWS, 2 * tq), F32)],
        compiler_params=pltpu.CompilerParams(dimension_semantics=("arbitrary",) * 3,
                                             vmem_limit_bytes=VMEM_LIMIT),
        name="diff_attn",
    )(qta, ka, vta, lamv, sg)


def _moba_kernel(qt_ref, k_ref, vt_ref, o_ref, m_ref, acc_ref, km_ref, s_ref):
    tq = qt_ref.shape[2]
    nh = qt_ref.shape[1] // LANES
    nb = vt_ref.shape[0]
    qi = pl.program_id(2)

    @pl.when(qi == 0)
    def _():
        lane1 = lax.broadcasted_iota(I32, (1, LANES), 1)
        for hh in range(nh):
            km_ref[hh] = jnp.zeros((LANES, LANES), F32)

            def put_block(j, carry):
                blk = k_ref[pl.ds(pl.multiple_of(j * KT, KT), KT), hh * LANES:(hh + 1) * LANES].astype(F32)
                mean = jnp.sum(blk, axis=0, keepdims=True) * (1.0 / KT)
                km_ref[hh, pl.ds(SEL0 + j, 1), :] = jnp.where(lane1 < HEAD_DIM, mean, 0.0)
                return carry
            lax.fori_loop(0, nb, put_block, 0)

    row = lax.broadcasted_iota(I32, (LANES, tq), 0)
    kr = lax.broadcasted_iota(I32, (KT, tq), 0)
    qc = lax.broadcasted_iota(I32, (KT, tq), 1)
    causal = kr <= qc
    in_sel = (row >= SEL0) & (row < SEL0 + MAX_BLOCKS)
    q_diag, q_past = [], []
    for hh in range(nh):
        qt = qt_ref[0, hh * LANES:(hh + 1) * LANES, :]
        km = km_ref[hh]
        km_hi = km.astype(BF16)
        km_lo = (km - km_hi.astype(F32)).astype(BF16)
        gate = (jnp.dot(km_hi, qt, preferred_element_type=F32)
                + jnp.dot(km_lo, qt, preferred_element_type=F32))
        cur = jnp.where((row >= SEL0) & (row < SEL0 + qi), gate, -jnp.inf)
        sel = jnp.zeros((LANES, tq), jnp.bool_)
        for _ in range(MOBA_TOPK):
            mx = jnp.max(cur, axis=0, keepdims=True)
            first = jnp.min(jnp.where(cur == mx, row, 4 * LANES), axis=0, keepdims=True)
            pick = (row == first) & (mx > -jnp.inf)
            sel = sel | pick
            cur = jnp.where(pick, -jnp.inf, cur)
        q_diag.append(jnp.where(in_sel, jnp.zeros_like(qt), qt))
        q_past.append(jnp.where(in_sel, jnp.where(sel, 0.0, NEG_BIG).astype(BF16), qt))
    _flash_causal_pair(k_ref, vt_ref, q_diag, q_past, qi, causal, m_ref, acc_ref, s_ref)
    res = [_flash_finish(acc_ref, hh) for hh in range(nh)]
    o_ref[...] = jnp.concatenate(res, axis=0).T.astype(BF16)


def _moba_attention(qtb, kb, vtb, *, bsz, seq, nh=B_HEADS):
    tq = KT
    nq = seq // tq
    assert nq <= MAX_BLOCKS and seq % (PAST_TILES * KT) == 0
    t = bsz * seq
    return pl.pallas_call(
        _moba_kernel,
        grid=(bsz, B_HEADS // nh, nq),
        in_specs=[
            pl.BlockSpec((1, nh * LANES, tq), lambda b, hp, qi: (b * nq + qi, hp, 0)),
            pl.BlockSpec((seq, nh * LANES), lambda b, hp, qi: (b, hp), pipeline_mode=pl.Buffered(1)),
            pl.BlockSpec((nq, nh * HEAD_DIM, KT), lambda b, hp, qi: (b, hp, 0), pipeline_mode=pl.Buffered(1)),
        ],
        out_specs=pl.BlockSpec((tq, nh * HEAD_DIM), lambda b, hp, qi: (b * nq + qi, hp)),
        out_shape=jax.ShapeDtypeStruct((t, B_HEADS * HEAD_DIM), BF16),
        scratch_shapes=[pltpu.VMEM((nh, 1, tq), F32), pltpu.VMEM((nh, ACC_ROWS, tq), F32),
                        pltpu.VMEM((nh, LANES, LANES), F32), pltpu.VMEM((2, nh, PAST_TILES * KT, tq), F32)],
        compiler_params=pltpu.CompilerParams(dimension_semantics=("arbitrary",) * 3,
                                             vmem_limit_bytes=VMEM_LIMIT),
        name="moba_attn",
    )(qtb, kb, vtb)


def _swa_kernel(sink_ref, q_ref, ktp_ref, ktc_ref, vp_ref, vc_ref, o_ref, *, slopes):
    tq = q_ref.shape[0]
    n = pl.program_id(1)
    lane = lax.broadcasted_iota(I32, (tq, LANES), 1)
    r2 = lax.broadcasted_iota(I32, (tq, 2 * KT), 0)
    c2 = lax.broadcasted_iota(I32, (tq, 2 * KT), 1)
    rel = r2 + KT - c2
    mask = (rel >= 0) & (rel < WINDOW) & ((c2 >= KT) | (n > 0))
    relf = rel.astype(F32)
    res = []
    for hq in range(C_HEADS):
        kv = hq // C_GROUP
        qp = q_ref[:, (hq // 2) * LANES:(hq // 2 + 1) * LANES]
        qm = jnp.where((lane < HEAD_DIM) == (hq % 2 == 0), qp, jnp.zeros_like(qp))
        kt = jnp.concatenate([ktp_ref[0, kv * LANES:(kv + 1) * LANES, :],
                              ktc_ref[0, kv * LANES:(kv + 1) * LANES, :]], axis=1)
        vv = jnp.concatenate([vp_ref[:, kv * LANES:(kv + 1) * LANES],
                              vc_ref[:, kv * LANES:(kv + 1) * LANES]], axis=0)
        s = jnp.dot(qm, kt, preferred_element_type=F32)
        s = jnp.where(mask, s - slopes[hq] * relf, -jnp.inf)
        sink = sink_ref[hq]
        m = jnp.maximum(jnp.max(s, axis=-1, keepdims=True), sink)
        e = jnp.exp(s - m)
        den = jnp.sum(e, axis=-1, keepdims=True) + jnp.exp(sink - m)
        p = (e / den).astype(BF16)
        res.append(jnp.dot(p, vv, preferred_element_type=F32))
    for pr in range(C_HEADS // 2):
        o_ref[:, pr * LANES:(pr + 1) * LANES] = jnp.where(lane < HEAD_DIM, res[2 * pr], res[2 * pr + 1]).astype(BF16)


def _swa_attention(sinks, qc, ktc, vc2, *, bsz, seq, slopes):
    tq = KT
    nq = seq // tq
    t = bsz * seq
    return pl.pallas_call(
        functools.partial(_swa_kernel, slopes=slopes),
        grid=(bsz, nq),
        in_specs=[
            pl.BlockSpec(memory_space=pltpu.SMEM),
            pl.BlockSpec((tq, C_HEADS * HEAD_DIM), lambda b, n: (b * nq + n, 0)),
            pl.BlockSpec((1, C_KV_HEADS * LANES, KT), lambda b, n: (b * nq + jnp.maximum(n - 1, 0), 0, 0)),
            pl.BlockSpec((1, C_KV_HEADS * LANES, KT), lambda b, n: (b * nq + n, 0, 0)),
            pl.BlockSpec((tq, C_KV_HEADS * LANES), lambda b, n: (b * nq + jnp.maximum(n - 1, 0), 0)),
            pl.BlockSpec((tq, C_KV_HEADS * LANES), lambda b, n: (b * nq + n, 0)),
        ],
        out_specs=pl.BlockSpec((tq, C_HEADS * HEAD_DIM), lambda b, n: (b * nq + n, 0)),
        out_shape=jax.ShapeDtypeStruct((t, C_HEADS * HEAD_DIM), BF16),
        compiler_params=pltpu.CompilerParams(dimension_semantics=("arbitrary",) * 2,
                                             vmem_limit_bytes=VMEM_LIMIT),
        name="swa_attn",
    )(sinks, qc, ktc, ktc, vc2, vc2)


def _outproj_kernel(oa_ref, ob_ref, oc_ref, woa_ref, wob_ref, woc_ref, x_ref, mod_ref, g_ref, wq_ref, keys_ref,
                    x1_ref, h2_ref, st_ref):
    d = x_ref.shape[1]
    mix = (jnp.dot(oa_ref[...], woa_ref[...], preferred_element_type=F32)
           + jnp.dot(ob_ref[...], wob_ref[...], preferred_element_type=F32)
           + jnp.dot(oc_ref[...], woc_ref[...], preferred_element_type=F32))
    g1 = mod_ref[0, :, 2 * d:3 * d]
    sh2 = mod_ref[0, :, 3 * d:4 * d]
    sc2 = mod_ref[0, :, 4 * d:5 * d]
    x1 = x_ref[...] + g1 * mix
    x1_ref[...] = x1
    h2 = _rms_mod(x1, g_ref[...], sc2, sh2)
    nj = d // LANES
    for j in range(nj):
        h2_ref[pl.ds(j, x1.shape[0], stride=nj), :] = h2[:, j * LANES:(j + 1) * LANES]
    pq = jnp.dot(h2.astype(BF16), wq_ref[...], preferred_element_type=F32).astype(BF16)
    for hp in range(2 * PEER_HEADS):
        st_ref[hp] = lax.dot_general(keys_ref[hp], pq[:, hp * LANES:(hp + 1) * LANES],
                                     (((1,), (1,)), ((), ())), preferred_element_type=F32)


def _outproj(oa, ob, oc, woa, wob, woc, x2d, mod3, g, wq, keys, *, seq, tm=256):
    t, d = x2d.shape
    nt = t // tm
    nq = wq.shape[1]
    full = lambda a: pl.BlockSpec(a.shape, lambda i: (0,) * a.ndim)
    return pl.pallas_call(
        _outproj_kernel,
        grid=(nt,),
        in_specs=[
            pl.BlockSpec((tm, oa.shape[1]), lambda i: (i, 0)),
            pl.BlockSpec((tm, ob.shape[1]), lambda i: (i, 0)),
            pl.BlockSpec((tm, oc.shape[1]), lambda i: (i, 0)),
            full(woa), full(wob), full(woc),
            pl.BlockSpec((tm, d), lambda i: (i, 0)),
            pl.BlockSpec((1, 1, mod3.shape[2]), lambda i: ((i * tm) // seq, 0, 0)),
            pl.BlockSpec((1, d), lambda i: (0, 0)),
            full(wq), full(keys),
        ],
        out_specs=[pl.BlockSpec((tm, d), lambda i: (i, 0)),
                   pl.BlockSpec((tm * (d // LANES), LANES), lambda i: (i, 0)),
                   pl.BlockSpec((2 * PEER_HEADS, PEER_NKEYS, tm), lambda i: (0, 0, i))],
        out_shape=[jax.ShapeDtypeStruct((t, d), F32), jax.ShapeDtypeStruct((t * (d // LANES), LANES), F32),
                   jax.ShapeDtypeStruct((2 * PEER_HEADS, PEER_NKEYS, t), F32)],
        compiler_params=pltpu.CompilerParams(dimension_semantics=("arbitrary",),
                                             vmem_limit_bytes=VMEM_LIMIT),
        name="outproj_peerq",
    )(oa, ob, oc, woa, wob, woc, x2d, mod3, g, wq, keys)


_CAND_BLOCKS = ((0, 16),) + tuple((i, 8) for i in range(1, 8))
_CAND_ROWS = 16 + 7 * 8 + 8
_BIG_I = np.int32(2 ** 30)


TOPK_TL = LANES


def _topk_rows(curs, keys, val_refs, key_refs, unroll):
    def body(r, curs):
        out = []
        for cur, key, val_ref, key_ref in zip(curs, keys, val_refs, key_refs):
            m = jnp.max(cur, axis=0, keepdims=True)
            kmin = jnp.min(jnp.where(cur == m, key, _BIG_I), axis=0, keepdims=True)
            val_ref[pl.ds(r, 1), :] = m
            key_ref[pl.ds(r, 1), :] = kmin
            out.append(jnp.where(key == kmin, -jnp.inf, cur))
        return tuple(out)
    lax.fori_loop(0, PEER_TOPK, body, tuple(curs), unroll=unroll)


def _topk_scratch():
    pair = [pltpu.VMEM((2, PEER_TOPK, TOPK_TL), F32), pltpu.VMEM((2, PEER_TOPK, TOPK_TL), I32)]
    hk = PEER_HEADS * PEER_TOPK
    return pair * 3 + [pltpu.VMEM((hk, TOPK_TL), I32), pltpu.VMEM((hk, TOPK_TL), F32)]


def _topk_head_pair(st_ref, h0, scratch, unroll=False):
    v1_ref, k1_ref, v2_ref, k2_ref, vt_ref, kt_ref, ei_ref, gg_ref = scratch
    tl = st_ref.shape[2]
    row = lax.broadcasted_iota(I32, (PEER_NKEYS, tl), 0)
    jrow8 = lax.broadcasted_iota(I32, (8, tl), 0)
    jrow16 = lax.broadcasted_iota(I32, (16, tl), 0)
    cand, ckey = [], []
    for s in range(2):
        h = h0 + s
        _topk_rows([st_ref[2 * h], st_ref[2 * h + 1]], [row, row],
                   [v1_ref.at[s], v2_ref.at[s]], [k1_ref.at[s], k2_ref.at[s]], unroll)
        sv1, si1, sv2, si2 = v1_ref[s], k1_ref[s], v2_ref[s], k2_ref[s]
        cands, keys = [], []
        for i, nj in _CAND_BLOCKS:
            jrow = jrow16 if nj == 16 else jrow8
            cands.append(sv1[i:i + 1] + sv2[0:nj])
            keys.append((i * PEER_TOPK + jrow) * (PEER_NKEYS * PEER_NKEYS)
                        + si1[i:i + 1] * PEER_NKEYS + si2[0:nj])
        cands.append(sv1[8:16] + sv2[0:1])
        keys.append((jrow8 + 8) * (PEER_TOPK * PEER_NKEYS * PEER_NKEYS) + si1[8:16] * PEER_NKEYS + si2[0:1])
        cand.append(jnp.concatenate(cands, axis=0))
        ckey.append(jnp.concatenate(keys, axis=0))
    _topk_rows(cand, ckey, [vt_ref.at[0], vt_ref.at[1]], [kt_ref.at[0], kt_ref.at[1]], unroll)
    for s in range(2):
        rows = pl.ds(pl.multiple_of((h0 + s) * PEER_TOPK, PEER_TOPK), PEER_TOPK)
        top = vt_ref[s]
        e = jnp.exp(top - top[0:1])
        gg_ref[rows, :] = e / jnp.sum(e, axis=0, keepdims=True)
        ei_ref[rows, :] = kt_ref[s] & (PEER_NKEYS * PEER_NKEYS - 1)


def _topk_emit(scratch, off_ref, par_ref, g_ref):
    ei = scratch[6][...].T
    off_ref[...] = (ei & (PEER_HALF - 1)) * SUB
    par_ref[...] = ei // PEER_HALF
    g_ref[...] = scratch[7][...].T


def _peer_topk_kernel(st_ref, off_ref, par_ref, g_ref, *scratch):
    for h0 in range(0, PEER_HEADS, 2):
        _topk_head_pair(st_ref, h0, scratch)
    _topk_emit(scratch, off_ref, par_ref, g_ref)


def _peer_topk(st, blk0, nblk):
    hk = PEER_HEADS * PEER_TOPK
    t = nblk * TOPK_TL
    return pl.pallas_call(
        _peer_topk_kernel,
        grid=(nblk,),
        in_specs=[pl.BlockSpec((2 * PEER_HEADS, PEER_NKEYS, TOPK_TL), lambda i: (0, 0, blk0 + i))],
        out_specs=[pl.BlockSpec((TOPK_TL, hk), lambda i: (i, 0))] * 3,
        out_shape=[jax.ShapeDtypeStruct((t, hk), I32), jax.ShapeDtypeStruct((t, hk), I32),
                   jax.ShapeDtypeStruct((t, hk), F32)],
        scratch_shapes=_topk_scratch(),
        compiler_params=pltpu.CompilerParams(dimension_semantics=("arbitrary",),
                                             vmem_limit_bytes=VMEM_LIMIT),
        name="peer_topk",
    )(st)


SUB = 8


def _erf(x):
    return lax.erf(x)


HALF_HI = 0xFFFF0000


PEER_HALF = PEER_NKEYS * PEER_NKEYS // 2


def _pack_kernel(lo_ref, hi_ref, o_ref):
    pairs = lo_ref.shape[0]
    nj = lo_ref.shape[1] // LANES
    for j in range(nj):
        cols = slice(j * LANES, (j + 1) * LANES)
        lo = pltpu.bitcast(lo_ref[:, cols].astype(BF16).astype(F32), jnp.uint32)
        hi = pltpu.bitcast(hi_ref[:, cols].astype(BF16).astype(F32), jnp.uint32)
        o_ref[pl.ds(j, pairs, stride=nj), :] = (lo >> 16) | (hi & jnp.uint32(HALF_HI))


def _pack_expert_table(w, pairs=256):
    n, d = w.shape
    nj = d // LANES
    half_blocks = n // 2 // pairs
    return pl.pallas_call(
        _pack_kernel,
        grid=(half_blocks,),
        in_specs=[pl.BlockSpec((pairs, d), lambda i: (i, 0)),
                  pl.BlockSpec((pairs, d), lambda i: (half_blocks + i, 0))],
        out_specs=pl.BlockSpec((pairs * nj, LANES), lambda i: (i, 0)),
        out_shape=jax.ShapeDtypeStruct((n // 2 * nj, LANES), jnp.uint32),
        compiler_params=pltpu.CompilerParams(dimension_semantics=("arbitrary",),
                                             vmem_limit_bytes=VMEM_LIMIT),
        name="pack_table",
    )(w, w)


def _load_table_once(tab_hbm, tab, sem):
    @pl.when(pl.program_id(0) == 0)
    def _():
        cp = pltpu.make_async_copy(tab_hbm, tab, sem)
        cp.start()
        cp.wait()


def _pair_tile(tab, off):
    return tab[pl.ds(pl.multiple_of(off, SUB), SUB), :]


def _group_matrix(rows, cols):
    r = lax.broadcasted_iota(I32, (rows, cols), 0)
    c = lax.broadcasted_iota(I32, (rows, cols), 1)
    return (c // (cols // rows) == r).astype(BF16)


def _split2(x):
    hi = x.astype(BF16)
    return hi, (x - hi.astype(F32)).astype(BF16)


PAIR_ROWS = 2 * SUB


def _pair_rows(tab, off_ref, tt, hk):
    return jnp.concatenate([pltpu.bitcast(_pair_tile(tab, off_ref[tt * hk + k]), BF16) for k in range(hk)], axis=0)


def _own_sublane_mask(hk):
    srow = lax.broadcasted_iota(I32, (SUB, hk * PAIR_ROWS), 0)
    scol = lax.broadcasted_iota(I32, (SUB, hk * PAIR_ROWS), 1)
    return (scol % PAIR_ROWS) // 2 == srow


def _fold_matrix(hk):
    c = jnp.arange(hk * PAIR_ROWS)
    return jax.nn.one_hot((c % 2) * hk + c // PAIR_ROWS, 2 * hk, dtype=BF16)


def _peer_u_body(off_ref, par_ref, h_ref, g_ref, fold_ref, w_ref, tab, d_sc):
    tq, hk = g_ref.shape
    mine = _own_sublane_mask(hk)
    nt = (((1,), (1,)), ((), ()))
    for tt in range(tq):
        h_hi, h_lo = _split2(h_ref[tt])
        s = lax.dot_general(jnp.concatenate([h_hi, h_lo], axis=0), _pair_rows(tab, off_ref, tt, hk), nt,
                            preferred_element_type=F32)
        d = jnp.where(mine, s[0:SUB] + s[SUB:2 * SUB], 0.0)
        d_sc[tt:tt + 1, :] = jnp.sum(d, axis=0, keepdims=True)
    d_hi, d_lo = _split2(d_sc[...])
    a2 = jnp.dot(jnp.concatenate([d_hi, d_lo], axis=0), fold_ref[...], preferred_element_type=F32)
    a2 = a2[0:tq] + a2[tq:2 * tq]
    a = jnp.where(par_ref[...] == 0, a2[:, :hk], a2[:, hk:])
    w_ref[...] = g_ref[...] * (0.5 * a * (1.0 + _erf(a * (2.0 ** -0.5))))


U_TQ = 64


def _peer_u_kernel(off_ref, par_ref, h_ref, g_ref, fold_ref, tab_hbm, w_ref, tab, sem, d_sc):
    _load_table_once(tab_hbm, tab, sem)
    _peer_u_body(off_ref, par_ref, h_ref, g_ref, fold_ref, w_ref, tab, d_sc)


def _peer_u_topk_kernel(off_ref, par_ref, h_ref, g_ref, fold_ref, st_ref, tab_hbm,
                        w_ref, off2_ref, par2_ref, g2_ref, tab, sem, d_sc, *scratch):
    half = pl.program_id(0) % 2
    _load_table_once(tab_hbm, tab, sem)
    for q in range(2):
        _topk_head_pair(st_ref, (2 * half + q) * 2, scratch, unroll=True)
    _peer_u_body(off_ref, par_ref, h_ref, g_ref, fold_ref, w_ref, tab, d_sc)

    @pl.when(half == 1)
    def _():
        _topk_emit(scratch, off2_ref, par2_ref, g2_ref)


def _peer_u(off_flat, par, g, h2, utab, tok0, st=None, st_blk0=None):
    t, hk = g.shape
    nj = h2.shape[1]
    tq = U_TQ
    assert t % (2 * tq) == 0 and 2 * tq == TOPK_TL and tok0 % tq == 0
    fold = _fold_matrix(hk)
    in_specs = [
        pl.BlockSpec((tq * hk,), lambda i: (i,), memory_space=pltpu.SMEM),
        pl.BlockSpec((tq, hk), lambda i: (i, 0)),
        pl.BlockSpec((tq, nj, LANES), lambda i: (tok0 // tq + i, 0, 0)),
        pl.BlockSpec((tq, hk), lambda i: (i, 0)),
        pl.BlockSpec(fold.shape, lambda i: (0, 0)),
    ]
    out_specs = [pl.BlockSpec((tq, hk), lambda i: (i, 0))]
    out_shape = [jax.ShapeDtypeStruct((t, hk), F32)]
    scratch = [pltpu.VMEM(utab.shape, jnp.uint32), pltpu.SemaphoreType.DMA(()),
               pltpu.VMEM((tq, hk * PAIR_ROWS), F32)]
    args = [off_flat, par, h2, g, fold]
    if st is not None:
        in_specs.append(pl.BlockSpec((2 * PEER_HEADS, PEER_NKEYS, TOPK_TL), lambda i: (0, 0, st_blk0 + i // 2)))
        out_specs += [pl.BlockSpec((TOPK_TL, hk), lambda i: (i // 2, 0))] * 3
        out_shape += [jax.ShapeDtypeStruct((t, hk), I32), jax.ShapeDtypeStruct((t, hk), I32),
                      jax.ShapeDtypeStruct((t, hk), F32)]
        scratch += _topk_scratch()
        args.append(st)
    in_specs.append(pl.BlockSpec(memory_space=pl.ANY))
    args.append(utab)
    return pl.pallas_call(
        _peer_u_kernel if st is None else _peer_u_topk_kernel,
        grid=(t // tq,),
        in_specs=in_specs,
        out_specs=out_specs,
        out_shape=out_shape,
        scratch_shapes=scratch,
        compiler_params=pltpu.CompilerParams(dimension_semantics=("arbitrary",),
                                             vmem_limit_bytes=VMEM_LIMIT),
        name="peer_u" if st is None else "peer_u_topk",
    )(*args)


def _peer_v_kernel(off_ref, par_ref, w_ref, x1_ref, mod_ref, fg_ref, tab_hbm, o_ref, tab, sem, wl_sc, p_sc, *, final):
    tq, hk = w_ref.shape
    d = x1_ref.shape[1]
    nj = d // LANES
    _load_table_once(tab_hbm, tab, sem)
    width = hk * PAIR_ROWS
    rep = _group_matrix(hk, width)
    w_hi, w_lo = _split2(w_ref[...])
    parl = jnp.dot(par_ref[...].astype(BF16), rep, preferred_element_type=F32)
    lane = lax.broadcasted_iota(I32, (tq, width), 1)
    wanted = (lane % 2).astype(F32) == parl
    wl_sc[0] = jnp.where(wanted, jnp.dot(w_hi, rep, preferred_element_type=F32), 0.0)
    wl_sc[1] = jnp.where(wanted, jnp.dot(w_lo, rep, preferred_element_type=F32), 0.0)
    mine = _own_sublane_mask(hk)
    for tt in range(tq):
        lhs = jnp.concatenate([jnp.where(mine, wl_sc[0, tt:tt + 1, :], 0.0),
                               jnp.where(mine, wl_sc[1, tt:tt + 1, :], 0.0)], axis=0).astype(BF16)
        r = jnp.dot(lhs, _pair_rows(tab, off_ref, tt, hk), preferred_element_type=F32)
        p_sc[tt * SUB:(tt + 1) * SUB, :] = r[0:SUB] + r[SUB:2 * SUB]
    peer = jnp.concatenate([p_sc[pl.ds(j, tq, stride=nj), :] for j in range(nj)], axis=1)
    y = x1_ref[...] + mod_ref[0, :, 5 * d:6 * d] * peer
    if final:
        y = y * lax.rsqrt(jnp.mean(y * y, axis=-1, keepdims=True) + EPS) * fg_ref[...]
    o_ref[...] = y


def _peer_v(off_flat, par, w, x1, mod3, final_g, vtab, *, seq, final, tq=64):
    t, d = x1.shape
    hk = w.shape[1]
    return pl.pallas_call(
        functools.partial(_peer_v_kernel, final=final),
        grid=(t // tq,),
        in_specs=[
            pl.BlockSpec((tq * hk,), lambda i: (i,), memory_space=pltpu.SMEM),
            pl.BlockSpec((tq, hk), lambda i: (i, 0)),
            pl.BlockSpec((tq, hk), lambda i: (i, 0)),
            pl.BlockSpec((tq, d), lambda i: (i, 0)),
            pl.BlockSpec((1, 1, mod3.shape[2]), lambda i: ((i * tq) // seq, 0, 0)),
            pl.BlockSpec((1, d), lambda i: (0, 0)),
            pl.BlockSpec(memory_space=pl.ANY),
        ],
        out_specs=pl.BlockSpec((tq, d), lambda i: (i, 0)),
        out_shape=jax.ShapeDtypeStruct((t, d), F32),
        scratch_shapes=[pltpu.VMEM(vtab.shape, jnp.uint32), pltpu.SemaphoreType.DMA(()),
                        pltpu.VMEM((2, tq, hk * PAIR_ROWS), F32), pltpu.VMEM((tq * (d // LANES), LANES), F32)],
        compiler_params=pltpu.CompilerParams(dimension_semantics=("arbitrary",),
                                             vmem_limit_bytes=VMEM_LIMIT),
        name="peer_v",
    )(off_flat, par, w, x1, mod3, final_g, vtab)


def kernel(x, c, norm1_g, norm2_g, w_ada, b_ada, w_in, w_out, lam_q1, lam_k1, lam_q2, lam_k2, subln_g, sinks,
           peer_wq, peer_keys, peer_u, peer_v, final_g):
    bsz, seq, d = x.shape
    depth = w_in.shape[0]
    t = bsz * seq
    slopes = _alibi_slopes()
    sl_c = [float(s) for s in slopes[:C_HEADS]]
    qa_bias = _slope_bias_col(slopes[C_HEADS:C_HEADS + A_HEADS] * np.float32(LOG2E))
    qb_bias = _slope_bias_col(slopes[C_HEADS + A_HEADS:] * np.float32(LOG2E))
    mods = _adaln_mods(c, w_ada, b_ada)
    x2d = x.reshape(t, d)
    av, bw = A_HEADS * HEAD_DIM, B_HEADS * HEAD_DIM
    for l in range(depth):
        lam_init = 0.8 - 0.6 * math.exp(-0.3 * l)
        mod3 = mods[l].reshape(bsz, 1, N_MOD * d)
        wn, wt = _prep_in_weights(w_in[l])
        ka, kb, qc, vc2, qta, vta, qtb, vtb, ktc = _inproj(
            x2d, mod3, norm1_g[l].reshape(1, d), wn, wt, qa_bias, qb_bias, seq=seq)
        lamv = jnp.stack([lam_q1[l], lam_k1[l], lam_q2[l], lam_k2[l]]).astype(F32)
        sg = subln_g[l].reshape(HEAD_DIM, 1).astype(F32)
        oa = _diff_attention(qta, ka, vta, lamv, sg, bsz=bsz, seq=seq, lam_init=lam_init)
        ob = _moba_attention(qtb, kb, vtb, bsz=bsz, seq=seq)
        oc = _swa_attention(sinks[l].astype(F32), qc, ktc, vc2, bsz=bsz, seq=seq, slopes=sl_c)
        wo = w_out[l].astype(BF16)
        keys = peer_keys[l].reshape(2 * PEER_HEADS, PEER_NKEYS, -1).astype(BF16)
        x1, h2, st = _outproj(oa, ob, oc, wo[:av], wo[av:av + bw], wo[av + bw:], x2d, mod3,
                              norm2_g[l].reshape(1, d), peer_wq[l].astype(BF16), keys, seq=seq)
        assert seq % TOPK_TL == 0
        blocks = seq // TOPK_TL
        utab = _pack_expert_table(peer_u[l])
        h2t = h2.reshape(t, d // LANES, LANES)
        sel = [_peer_topk(st, 0, blocks)]
        ws = []
        for b in range(bsz):
            off_b, par_b, g_b = sel[b]
            if b + 1 < bsz:
                w_b, *nxt = _peer_u(off_b.reshape(-1), par_b, g_b, h2t, utab, b * seq, st, (b + 1) * blocks)
                sel.append(nxt)
            else:
                (w_b,) = _peer_u(off_b.reshape(-1), par_b, g_b, h2t, utab, b * seq)
            ws.append(w_b)
        toff = jnp.concatenate([s_[0] for s_ in sel]).reshape(-1)
        par = jnp.concatenate([s_[1] for s_ in sel])
        w = jnp.concatenate(ws)
        x2d = _peer_v(toff, par, w, x1, mod3, final_g.reshape(1, d), _pack_expert_table(peer_v[l]),
                      seq=seq, final=(l == depth - 1))
    return x2d.reshape(bsz, seq, d)
```

```python
import functools
import math

import numpy as np
import jax
import jax.numpy as jnp
from jax import lax
from jax.experimental import pallas as pl
from jax.experimental.pallas import tpu as pltpu

F32 = jnp.float32
BF16 = jnp.bfloat16
I32 = jnp.int32

D_MODEL = 1024
HEAD_DIM = 64
N_HEADS_TOTAL = 16
A_HEADS = 4
B_HEADS = 6
C_HEADS = 6
C_KV_HEADS = 2
C_GROUP = 3
A_QK_DIM = 32
MOBA_BLOCK = 256
MOBA_TOPK = 3
WINDOW = 128
ALIBI_MAX = 8.0
PEER_HEADS = 8
PEER_NKEYS = 128
PEER_TOPK = 16
N_MOD = 6
EPS = 1e-6

LANES = 128
KT = 256
AUG0 = HEAD_DIM
SEL0 = AUG0 + 6
MAX_BLOCKS = LANES - SEL0
NEG_BIG = -1e30
LOG2E = math.log2(math.e)
VMEM_LIMIT = 56 * 1024 * 1024


def _alibi_slopes():
    n = N_HEADS_TOTAL
    return (2.0 ** (-ALIBI_MAX * np.arange(1, n + 1, dtype=np.float32) / n)).astype(np.float32)


def _split3(v):
    v = np.float32(v)
    hi = np.float32(np.asarray(v).astype(jnp.bfloat16).astype(np.float32))
    r = np.float32(v - hi)
    mid = np.float32(np.asarray(r).astype(jnp.bfloat16).astype(np.float32))
    lo = np.float32(np.float32(r - mid))
    lo = np.float32(np.asarray(lo).astype(jnp.bfloat16).astype(np.float32))
    return hi, mid, lo


def _slope_bias_col(slopes):
    col = np.zeros((LANES * len(slopes), 1), np.float32)
    for h, s in enumerate(slopes):
        hi, mid, lo = _split3(s)
        col[h * LANES + AUG0:h * LANES + AUG0 + 6, 0] = [hi, mid, lo, hi, mid, lo]
    return jnp.asarray(col)


def _mod_kernel(c_ref, w_ref, b_ref, o_ref):
    c = c_ref[...]
    cs = c * (1.0 / (1.0 + jnp.exp(-c)))
    o_ref[0] = jnp.dot(cs, w_ref[0], preferred_element_type=F32) + b_ref[0]


def _adaln_mods(c, w_ada, b_ada):
    depth, d, n = w_ada.shape
    bsz = c.shape[0]
    rows = -(-bsz // 8) * 8
    cp = jnp.pad(c, ((0, rows - bsz), (0, 0)))
    tn = 1536
    out = pl.pallas_call(
        _mod_kernel,
        grid=(depth, n // tn),
        in_specs=[
            pl.BlockSpec((rows, d), lambda l, j: (0, 0)),
            pl.BlockSpec((1, d, tn), lambda l, j: (l, 0, j)),
            pl.BlockSpec((1, 1, tn), lambda l, j: (l, 0, j)),
        ],
        out_specs=pl.BlockSpec((1, rows, tn), lambda l, j: (l, 0, j)),
        out_shape=jax.ShapeDtypeStruct((depth, rows, n), F32),
        compiler_params=pltpu.CompilerParams(dimension_semantics=("arbitrary", "arbitrary"),
                                             vmem_limit_bytes=VMEM_LIMIT),
        name="adaln_mods",
    )(cp, w_ada, b_ada.reshape(depth, 1, n))
    return out[:, :bsz]


NN_WIDTHS = (A_HEADS * LANES, B_HEADS * LANES, C_HEADS * HEAD_DIM, 2 * C_KV_HEADS * HEAD_DIM)
NT_ROWS = (A_HEADS * LANES, A_HEADS * HEAD_DIM, B_HEADS * LANES, B_HEADS * HEAD_DIM,
           C_KV_HEADS * LANES)


def _prep_in_weights(w):
    d = w.shape[0]
    aq, ak, av = A_HEADS * 2 * A_QK_DIM, A_HEADS * 2 * A_QK_DIM, A_HEADS * HEAD_DIM
    bw = B_HEADS * HEAD_DIM
    cq, ckv = C_HEADS * HEAD_DIM, C_KV_HEADS * HEAD_DIM
    cuts = np.cumsum([aq, ak, av, bw, bw, bw, cq, ckv]).tolist()
    qa, ka, va, qb, kb, vb, qc, kc, vc = jnp.split(w, cuts, axis=-1)

    def pad_heads(m, nh, scale):
        m = (m * scale).reshape(d, nh, HEAD_DIM)
        return jnp.pad(m, ((0, 0), (0, 0), (0, LANES - HEAD_DIM))).reshape(d, nh * LANES)

    vc2 = vc.reshape(d, C_KV_HEADS, 1, HEAD_DIM)
    vc2 = jnp.broadcast_to(vc2, (d, C_KV_HEADS, 2, HEAD_DIM)).reshape(d, 2 * ckv)
    kc2 = jnp.broadcast_to(kc.reshape(d, C_KV_HEADS, 1, HEAD_DIM), (d, C_KV_HEADS, 2, HEAD_DIM)).reshape(d, 2 * ckv)
    wn = jnp.concatenate([pad_heads(ka, A_HEADS, 1.0), pad_heads(kb, B_HEADS, 1.0),
                          qc * (HEAD_DIM ** -0.5), vc2], axis=1)
    wt = jnp.concatenate([pad_heads(qa, A_HEADS, A_QK_DIM ** -0.5 * LOG2E), va,
                          pad_heads(qb, B_HEADS, HEAD_DIM ** -0.5 * LOG2E), vb, kc2], axis=1).T
    return wn.astype(BF16), wt.astype(BF16)


def _rms_mod(x, g, sc, sh):
    ms = jnp.mean(x * x, axis=-1, keepdims=True)
    return (x * lax.rsqrt(ms + EPS) * g) * (1.0 + sc) + sh


def _inproj_kernel(x_ref, mod_ref, g_ref, wn_ref, wt_ref, qab_ref, qbb_ref,
                   ka_ref, kb_ref, qc_ref, vc_ref, qta_ref, vta_ref, qtb_ref, vtb_ref, ktc_ref,
                   *, tm, seq):
    d = x_ref.shape[1]
    x = x_ref[...]
    sh = mod_ref[0, :, 0:d]
    sc = mod_ref[0, :, d:2 * d]
    h = _rms_mod(x, g_ref[...], sc, sh).astype(BF16)
    pn = jnp.dot(h, wn_ref[...], preferred_element_type=F32)
    pt = lax.dot_general(wt_ref[...], h, (((1,), (1,)), ((), ())),
                         preferred_element_type=F32)

    pos = (pl.program_id(0) * tm) % seq + lax.broadcasted_iota(I32, (tm, LANES), 0)
    col = lax.broadcasted_iota(I32, (tm, LANES), 1)
    blk_id = pos // KT
    p_hi = (blk_id * KT).astype(F32)
    p_lo = (pos - blk_id * KT).astype(F32)
    aug_a = jnp.where((col >= AUG0) & (col < AUG0 + 3), p_hi,
                      jnp.where((col >= AUG0 + 3) & (col < AUG0 + 6), p_lo, 0.0))
    aug_b = jnp.where((col >= SEL0) & (col - SEL0 == blk_id), 1.0, aug_a)
    for hh in range(A_HEADS):
        ka_ref[:, hh * LANES:(hh + 1) * LANES] = (pn[:, hh * LANES:(hh + 1) * LANES] + aug_a).astype(BF16)
    o = NN_WIDTHS[0]
    for hh in range(B_HEADS):
        kb_ref[:, hh * LANES:(hh + 1) * LANES] = (pn[:, o + hh * LANES:o + (hh + 1) * LANES] + aug_b).astype(BF16)
    o += NN_WIDTHS[1]
    qc_ref[...] = pn[:, o:o + NN_WIDTHS[2]].astype(BF16)
    o += NN_WIDTHS[2]
    vc_ref[...] = pn[:, o:o + NN_WIDTHS[3]].astype(BF16)

    r0 = 0
    for ref, b, nr in zip((qta_ref, vta_ref, qtb_ref, vtb_ref, ktc_ref),
                          (qab_ref, None, qbb_ref, None, None), NT_ROWS):
        blk = pt[r0:r0 + nr, :]
        if b is not None:
            blk = blk + b[...]
        blk = blk.astype(BF16)
        for cc in range(tm // KT):
            ref[cc] = blk[:, cc * KT:(cc + 1) * KT]
        r0 += nr


def _inproj(x2d, mod3, g, wn, wt, qa_bias, qb_bias, *, seq, tm=512):
    t, d = x2d.shape
    assert seq % tm == 0 and tm % KT == 0
    nt = t // tm
    nn_total = sum(NN_WIDTHS)
    row_specs = [pl.BlockSpec((tm, wd), lambda i: (i, 0)) for wd in NN_WIDTHS]
    kt_specs = [pl.BlockSpec((tm // KT, r, KT), lambda i: (i, 0, 0)) for r in NT_ROWS]
    out_shape = ([jax.ShapeDtypeStruct((t, wd), BF16) for wd in NN_WIDTHS]
                 + [jax.ShapeDtypeStruct((t // KT, r, KT), BF16) for r in NT_ROWS])
    return pl.pallas_call(
        functools.partial(_inproj_kernel, tm=tm, seq=seq),
        grid=(nt,),
        in_specs=[
            pl.BlockSpec((tm, d), lambda i: (i, 0)),
            pl.BlockSpec((1, 1, mod3.shape[2]), lambda i: ((i * tm) // seq, 0, 0)),
            pl.BlockSpec((1, d), lambda i: (0, 0)),
            pl.BlockSpec((d, nn_total), lambda i: (0, 0)),
            pl.BlockSpec((sum(NT_ROWS), d), lambda i: (0, 0)),
            pl.BlockSpec((NT_ROWS[0], 1), lambda i: (0, 0)),
            pl.BlockSpec((NT_ROWS[2], 1), lambda i: (0, 0)),
        ],
        out_specs=row_specs + kt_specs,
        out_shape=out_shape,
        compiler_params=pltpu.CompilerParams(dimension_semantics=("arbitrary",),
                                             vmem_limit_bytes=VMEM_LIMIT),
        name="inproj",
    )(x2d, mod3, g, wn, wt, qa_bias, qb_bias)


ACC_ROWS = HEAD_DIM + 16


def _softmax_pv(ss, vt1s, m_ref, acc_ref):
    nh = len(ss)
    m_prev = [m_ref[hh] for hh in range(nh)]
    m_new = [jnp.maximum(m_prev[hh], jnp.max(ss[hh], axis=0, keepdims=True)) for hh in range(nh)]
    ps = [jnp.exp2((ss[hh] - m_new[hh]).astype(BF16)) for hh in range(nh)]
    pv = [jnp.dot(vt1s[hh], ps[hh], preferred_element_type=F32) for hh in range(nh)]
    for hh in range(nh):
        acc_ref[hh] = jnp.exp2(m_prev[hh] - m_new[hh]) * acc_ref[hh] + pv[hh]
        m_ref[hh] = m_new[hh]


def _flash_init(m_ref, acc_ref):
    m_ref[...] = jnp.full(m_ref.shape, -jnp.inf, F32)
    acc_ref[...] = jnp.zeros(acc_ref.shape, F32)


def _flash_finish(acc_ref, hh):
    acc = acc_ref[hh]
    return acc[0:HEAD_DIM] / acc[HEAD_DIM:HEAD_DIM + 1]


PAST_TILES = 4
SERIAL_TILES = 2


def _flash_causal_pair(k_ref, vt_ref, q_diag, q_past, qi, causal, m_ref, acc_ref, s_ref=None):
    _flash_init(m_ref, acc_ref)
    nkeys = PAST_TILES * KT
    nh = len(q_past)

    def vt1(kj, n, hh):
        rows = slice(hh * HEAD_DIM, (hh + 1) * HEAD_DIM)
        vt = vt_ref[kj, rows, :] if n == 1 else jnp.concatenate([vt_ref[kj + c, rows, :] for c in range(n)], axis=1)
        return jnp.concatenate([vt, jnp.ones((ACC_ROWS - HEAD_DIM, n * KT), BF16)], axis=0)

    def scores(kj, n, qs, hh):
        kk = k_ref[pl.ds(pl.multiple_of(kj * KT, KT), n * KT), hh * LANES:(hh + 1) * LANES]
        return jnp.dot(kk, qs[hh], preferred_element_type=F32)

    def step(kj, n, qs, mask=None):
        ss = [scores(kj, n, qs, hh) for hh in range(nh)]
        if mask is not None:
            ss = [jnp.where(mask, s, -jnp.inf) for s in ss]
        _softmax_pv(ss, [vt1(kj, n, hh) for hh in range(nh)], m_ref, acc_ref)

    step(qi, 1, q_diag, causal)

    if s_ref is None:
        def body(j, carry):
            step(j * SERIAL_TILES, SERIAL_TILES, q_past)
            return carry

        lax.fori_loop(0, qi // SERIAL_TILES, body, 0)
        for r in range(SERIAL_TILES - 1):
            @pl.when(qi % SERIAL_TILES > r)
            def _():
                step(qi - 1 - r, 1, q_past)
        return

    ngroups = k_ref.shape[0] // nkeys

    def scores_into(slot, g):
        for hh in range(nh):
            s_ref[slot, hh] = scores(g * PAST_TILES, PAST_TILES, q_past, hh)

    scores_into(0, 0)

    def body(g, carry):
        scores_into((g + 1) % 2, jnp.minimum(g + 1, ngroups - 1))
        _softmax_pv([s_ref[g % 2, hh] for hh in range(nh)],
                    [vt1(g * PAST_TILES, PAST_TILES, hh) for hh in range(nh)], m_ref, acc_ref)
        return carry

    lax.fori_loop(0, (qi + PAST_TILES - 1) // PAST_TILES, body, 0)


def _diff_kernel(qt_ref, k_ref, vt_ref, lamv_ref, sg_ref, o_ref, m_ref, acc_ref, *, lam_init):
    tq = qt_ref.shape[2]
    qi = pl.program_id(2)
    lv = lamv_ref[...]
    lam = (jnp.exp(jnp.sum(lv[0:1] * lv[1:2], axis=-1, keepdims=True))
           - jnp.exp(jnp.sum(lv[2:3] * lv[3:4], axis=-1, keepdims=True)) + lam_init)
    row = lax.broadcasted_iota(I32, (LANES, tq), 0)
    kr = lax.broadcasted_iota(I32, (KT, 2 * tq), 0)
    qc = lax.broadcasted_iota(I32, (KT, 2 * tq), 1)
    causal = kr <= jnp.where(qc >= tq, qc - tq, qc)
    nh = qt_ref.shape[1] // LANES
    qs = []
    for hh in range(nh):
        qt = qt_ref[0, hh * LANES:(hh + 1) * LANES, :]
        zero = jnp.zeros_like(qt)
        q1 = jnp.where((row < A_QK_DIM) | (row >= AUG0), qt, zero)
        q2 = jnp.where(row >= A_QK_DIM, qt, zero)
        qs.append(jnp.concatenate([q1, q2], axis=1))
    _flash_causal_pair(k_ref, vt_ref, qs, qs, qi, causal, m_ref, acc_ref)
    res = []
    for hh in range(nh):
        o = _flash_finish(acc_ref, hh)
        od = o[:, :tq] - lam * o[:, tq:]
        ms = jnp.mean(od * od, axis=0, keepdims=True)
        res.append(od * lax.rsqrt(ms + EPS) * sg_ref[...] * (1.0 - lam_init))
    o_ref[...] = jnp.concatenate(res, axis=0).T.astype(BF16)


def _diff_attention(qta, ka, vta, lamv, sg, *, bsz, seq, lam_init, nh=A_HEADS):
    tq = KT
    nq = seq // tq
    t = bsz * seq
    return pl.pallas_call(
        functools.partial(_diff_kernel, lam_init=lam_init),
        grid=(bsz, A_HEADS // nh, nq),
        in_specs=[
            pl.BlockSpec((1, nh * LANES, tq), lambda b, hp, qi: (b * nq + qi, hp, 0)),
            pl.BlockSpec((seq, nh * LANES), lambda b, hp, qi: (b, hp)),
            pl.BlockSpec((nq, nh * HEAD_DIM, KT), lambda b, hp, qi: (b, hp, 0)),
            pl.BlockSpec((4, A_QK_DIM), lambda b, hp, qi: (0, 0)),
            pl.BlockSpec((HEAD_DIM, 1), lambda b, hp, qi: (0, 0)),
        ],
        out_specs=pl.BlockSpec((tq, nh * HEAD_DIM), lambda b, hp, qi: (b * nq + qi, hp)),
        out_shape=jax.ShapeDtypeStruct((t, A_HEADS * HEAD_DIM), BF16),
        scratch_shapes=[pltpu.VMEM((nh, 1, 2 * tq), F32), pltpu.VMEM((nh, ACC_ROWS, 2 * tq), F32)],
        compiler_params=pltpu.CompilerParams(dimension_semantics=("arbitrary",) * 3,
                                             vmem_limit_bytes=VMEM_LIMIT),
        name="diff_attn",
    )(qta, ka, vta, lamv, sg)


def _moba_kernel(qt_ref, k_ref, vt_ref, o_ref, m_ref, acc_ref, km_ref, s_ref):
    tq = qt_ref.shape[2]
    nh = qt_ref.shape[1] // LANES
    nb = vt_ref.shape[0]
    qi = pl.program_id(2)

    @pl.when(qi == 0)
    def _():
        lane1 = lax.broadcasted_iota(I32, (1, LANES), 1)
        for hh in range(nh):
            km_ref[hh] = jnp.zeros((LANES, LANES), F32)

            def put_block(j, carry):
                blk = k_ref[pl.ds(pl.multiple_of(j * KT, KT), KT), hh * LANES:(hh + 1) * LANES].astype(F32)
                mean = jnp.sum(blk, axis=0, keepdims=True) * (1.0 / KT)
                km_ref[hh, pl.ds(SEL0 + j, 1), :] = jnp.where(lane1 < HEAD_DIM, mean, 0.0)
                return carry
            lax.fori_loop(0, nb, put_block, 0)

    kr = lax.broadcasted_iota(I32, (KT, tq), 0)
    qc = lax.broadcasted_iota(I32, (KT, tq), 1)
    causal = kr <= qc
    nsel = -(-(SEL0 - AUG0 + nb) // 16) * 16
    band = slice(AUG0, AUG0 + nsel)
    row = lax.broadcasted_iota(I32, (nsel, tq), 0) + AUG0
    in_sel = row >= SEL0
    qts = [qt_ref[0, hh * LANES:(hh + 1) * LANES, :] for hh in range(nh)]
    curs = []
    for hh in range(nh):
        km_hi, km_lo = _split2(km_ref[hh, band, :])
        gate = (jnp.dot(km_hi, qts[hh], preferred_element_type=F32)
                + jnp.dot(km_lo, qts[hh], preferred_element_type=F32))
        curs.append(jnp.where(in_sel & (row < SEL0 + qi), gate, -jnp.inf))
    sels = [jnp.zeros((nsel, tq), jnp.bool_)] * nh
    for _ in range(MOBA_TOPK):
        for hh in range(nh):
            mx = jnp.max(curs[hh], axis=0, keepdims=True)
            first = jnp.min(jnp.where(curs[hh] == mx, row, 4 * LANES), axis=0, keepdims=True)
            pick = (row == first) & (mx > -jnp.inf)
            sels[hh] = sels[hh] | pick
            curs[hh] = jnp.where(pick, -jnp.inf, curs[hh])
    q_diag, q_past = [], []
    for hh in range(nh):
        qt, mid = qts[hh], qts[hh][band]
        rest = [qt[AUG0 + nsel:]] if AUG0 + nsel < LANES else []
        q_diag.append(jnp.concatenate([qt[:AUG0], jnp.where(in_sel, jnp.zeros_like(mid), mid)] + rest, axis=0))
        bias = jnp.where(sels[hh], 0.0, NEG_BIG).astype(BF16)
        q_past.append(jnp.concatenate([qt[:AUG0], jnp.where(in_sel, bias, mid)] + rest, axis=0))
    _flash_causal_pair(k_ref, vt_ref, q_diag, q_past, qi, causal, m_ref, acc_ref, s_ref)
    res = [_flash_finish(acc_ref, hh) for hh in range(nh)]
    o_ref[...] = jnp.concatenate(res, axis=0).T.astype(BF16)


def _moba_attention(qtb, kb, vtb, *, bsz, seq, nh=B_HEADS):
    tq = KT
    nq = seq // tq
    assert nq <= MAX_BLOCKS and seq % (PAST_TILES * KT) == 0
    t = bsz * seq
    return pl.pallas_call(
        _moba_kernel,
        grid=(bsz, B_HEADS // nh, nq),
        in_specs=[
            pl.BlockSpec((1, nh * LANES, tq), lambda b, hp, qi: (b * nq + qi, hp, 0)),
            pl.BlockSpec((seq, nh * LANES), lambda b, hp, qi: (b, hp), pipeline_mode=pl.Buffered(1)),
            pl.BlockSpec((nq, nh * HEAD_DIM, KT), lambda b, hp, qi: (b, hp, 0), pipeline_mode=pl.Buffered(1)),
        ],
        out_specs=pl.BlockSpec((tq, nh * HEAD_DIM), lambda b, hp, qi: (b * nq + qi, hp)),
        out_shape=jax.ShapeDtypeStruct((t, B_HEADS * HEAD_DIM), BF16),
        scratch_shapes=[pltpu.VMEM((nh, 1, tq), F32), pltpu.VMEM((nh, ACC_ROWS, tq), F32),
                        pltpu.VMEM((nh, LANES, LANES), F32), pltpu.VMEM((2, nh, PAST_TILES * KT, tq), F32)],
        compiler_params=pltpu.CompilerParams(dimension_semantics=("arbitrary",) * 3,
                                             vmem_limit_bytes=VMEM_LIMIT),
        name="moba_attn",
    )(qtb, kb, vtb)


def _swa_kernel(sink_ref, q_ref, ktp_ref, ktc_ref, vp_ref, vc_ref, o_ref, *, slopes):
    tq = q_ref.shape[0]
    n = pl.program_id(1)
    lane = lax.broadcasted_iota(I32, (tq, LANES), 1)
    r2 = lax.broadcasted_iota(I32, (tq, 2 * KT), 0)
    c2 = lax.broadcasted_iota(I32, (tq, 2 * KT), 1)
    rel = r2 + KT - c2
    mask = (rel >= 0) & (rel < WINDOW) & ((c2 >= KT) | (n > 0))
    relf = rel.astype(F32)
    res = []
    for hq in range(C_HEADS):
        kv = hq // C_GROUP
        qp = q_ref[:, (hq // 2) * LANES:(hq // 2 + 1) * LANES]
        qm = jnp.where((lane < HEAD_DIM) == (hq % 2 == 0), qp, jnp.zeros_like(qp))
        kt = jnp.concatenate([ktp_ref[0, kv * LANES:(kv + 1) * LANES, :],
                              ktc_ref[0, kv * LANES:(kv + 1) * LANES, :]], axis=1)
        vv = jnp.concatenate([vp_ref[:, kv * LANES:(kv + 1) * LANES],
                              vc_ref[:, kv * LANES:(kv + 1) * LANES]], axis=0)
        s = jnp.dot(qm, kt, preferred_element_type=F32)
        s = jnp.where(mask, s - slopes[hq] * relf, -jnp.inf)
        sink = sink_ref[hq]
        m = jnp.maximum(jnp.max(s, axis=-1, keepdims=True), sink)
        e = jnp.exp(s - m)
        den = jnp.sum(e, axis=-1, keepdims=True) + jnp.exp(sink - m)
        p = (e / den).astype(BF16)
        res.append(jnp.dot(p, vv, preferred_element_type=F32))
    for pr in range(C_HEADS // 2):
        o_ref[:, pr * LANES:(pr + 1) * LANES] = jnp.where(lane < HEAD_DIM, res[2 * pr], res[2 * pr + 1]).astype(BF16)


def _swa_attention(sinks, qc, ktc, vc2, *, bsz, seq, slopes):
    tq = KT
    nq = seq // tq
    t = bsz * seq
    return pl.pallas_call(
        functools.partial(_swa_kernel, slopes=slopes),
        grid=(bsz, nq),
        in_specs=[
            pl.BlockSpec(memory_space=pltpu.SMEM),
            pl.BlockSpec((tq, C_HEADS * HEAD_DIM), lambda b, n: (b * nq + n, 0)),
            pl.BlockSpec((1, C_KV_HEADS * LANES, KT), lambda b, n: (b * nq + jnp.maximum(n - 1, 0), 0, 0)),
            pl.BlockSpec((1, C_KV_HEADS * LANES, KT), lambda b, n: (b * nq + n, 0, 0)),
            pl.BlockSpec((tq, C_KV_HEADS * LANES), lambda b, n: (b * nq + jnp.maximum(n - 1, 0), 0)),
            pl.BlockSpec((tq, C_KV_HEADS * LANES), lambda b, n: (b * nq + n, 0)),
        ],
        out_specs=pl.BlockSpec((tq, C_HEADS * HEAD_DIM), lambda b, n: (b * nq + n, 0)),
        out_shape=jax.ShapeDtypeStruct((t, C_HEADS * HEAD_DIM), BF16),
        compiler_params=pltpu.CompilerParams(dimension_semantics=("arbitrary",) * 2,
                                             vmem_limit_bytes=VMEM_LIMIT),
        name="swa_attn",
    )(sinks, qc, ktc, ktc, vc2, vc2)


def _outproj_kernel(oa_ref, ob_ref, oc_ref, woa_ref, wob_ref, woc_ref, x_ref, mod_ref, g_ref, wq_ref, keys_ref,
                    x1_ref, h2_ref, st_ref):
    d = x_ref.shape[1]
    mix = (jnp.dot(oa_ref[...], woa_ref[...], preferred_element_type=F32)
           + jnp.dot(ob_ref[...], wob_ref[...], preferred_element_type=F32)
           + jnp.dot(oc_ref[...], woc_ref[...], preferred_element_type=F32))
    g1 = mod_ref[0, :, 2 * d:3 * d]
    sh2 = mod_ref[0, :, 3 * d:4 * d]
    sc2 = mod_ref[0, :, 4 * d:5 * d]
    x1 = x_ref[...] + g1 * mix
    x1_ref[...] = x1
    h2 = _rms_mod(x1, g_ref[...], sc2, sh2)
    nj = d // LANES
    for j in range(nj):
        h2_ref[pl.ds(j, x1.shape[0], stride=nj), :] = h2[:, j * LANES:(j + 1) * LANES]
    pq = jnp.dot(h2.astype(BF16), wq_ref[...], preferred_element_type=F32).astype(BF16)
    for hp in range(2 * PEER_HEADS):
        st_ref[hp] = lax.dot_general(keys_ref[hp], pq[:, hp * LANES:(hp + 1) * LANES],
                                     (((1,), (1,)), ((), ())), preferred_element_type=F32)


def _outproj(oa, ob, oc, woa, wob, woc, x2d, mod3, g, wq, keys, *, seq, tm=256):
    t, d = x2d.shape
    nt = t // tm
    nq = wq.shape[1]
    full = lambda a: pl.BlockSpec(a.shape, lambda i: (0,) * a.ndim)
    return pl.pallas_call(
        _outproj_kernel,
        grid=(nt,),
        in_specs=[
            pl.BlockSpec((tm, oa.shape[1]), lambda i: (i, 0)),
            pl.BlockSpec((tm, ob.shape[1]), lambda i: (i, 0)),
            pl.BlockSpec((tm, oc.shape[1]), lambda i: (i, 0)),
            full(woa), full(wob), full(woc),
            pl.BlockSpec((tm, d), lambda i: (i, 0)),
            pl.BlockSpec((1, 1, mod3.shape[2]), lambda i: ((i * tm) // seq, 0, 0)),
            pl.BlockSpec((1, d), lambda i: (0, 0)),
            full(wq), full(keys),
        ],
        out_specs=[pl.BlockSpec((tm, d), lambda i: (i, 0)),
                   pl.BlockSpec((tm * (d // LANES), LANES), lambda i: (i, 0)),
                   pl.BlockSpec((2 * PEER_HEADS, PEER_NKEYS, tm), lambda i: (0, 0, i))],
        out_shape=[jax.ShapeDtypeStruct((t, d), F32), jax.ShapeDtypeStruct((t * (d // LANES), LANES), F32),
                   jax.ShapeDtypeStruct((2 * PEER_HEADS, PEER_NKEYS, t), F32)],
        compiler_params=pltpu.CompilerParams(dimension_semantics=("arbitrary",),
                                             vmem_limit_bytes=VMEM_LIMIT),
        name="outproj_peerq",
    )(oa, ob, oc, woa, wob, woc, x2d, mod3, g, wq, keys)


_CAND_BLOCKS = ((0, 16),) + tuple((i, 8) for i in range(1, 8))
_CAND_ROWS = 16 + 7 * 8 + 8
_BIG_I = np.int32(2 ** 30)


TOPK_TL = LANES


def _topk_rows(curs, keys, val_refs, key_refs, unroll):
    def body(r, curs):
        out = []
        for cur, key, val_ref, key_ref in zip(curs, keys, val_refs, key_refs):
            m = jnp.max(cur, axis=0, keepdims=True)
            kmin = jnp.min(jnp.where(cur == m, key, _BIG_I), axis=0, keepdims=True)
            val_ref[pl.ds(r, 1), :] = m
            key_ref[pl.ds(r, 1), :] = kmin
            out.append(jnp.where(key == kmin, -jnp.inf, cur))
        return tuple(out)
    lax.fori_loop(0, PEER_TOPK, body, tuple(curs), unroll=unroll)


def _topk_scratch():
    pair = [pltpu.VMEM((2, PEER_TOPK, TOPK_TL), F32), pltpu.VMEM((2, PEER_TOPK, TOPK_TL), I32)]
    hk = PEER_HEADS * PEER_TOPK
    return pair * 3 + [pltpu.VMEM((hk, TOPK_TL), I32), pltpu.VMEM((hk, TOPK_TL), F32)]


def _product_candidates(sv1, si1, sv2, si2):
    tl = sv1.shape[1]
    jrow8 = lax.broadcasted_iota(I32, (8, tl), 0)
    jrow16 = lax.broadcasted_iota(I32, (16, tl), 0)
    cands, keys = [], []
    for i, nj in _CAND_BLOCKS:
        jrow = jrow16 if nj == 16 else jrow8
        cands.append(sv1[i:i + 1] + sv2[0:nj])
        keys.append((i * PEER_TOPK + jrow) * (PEER_NKEYS * PEER_NKEYS) + si1[i:i + 1] * PEER_NKEYS + si2[0:nj])
    cands.append(sv1[8:16] + sv2[0:1])
    keys.append((jrow8 + 8) * (PEER_TOPK * PEER_NKEYS * PEER_NKEYS) + si1[8:16] * PEER_NKEYS + si2[0:1])
    return jnp.concatenate(cands, axis=0), jnp.concatenate(keys, axis=0)


def _store_gates(h, top, keys, ei_ref, gg_ref):
    rows = pl.ds(pl.multiple_of(h * PEER_TOPK, PEER_TOPK), PEER_TOPK)
    e = jnp.exp(top - top[0:1])
    gg_ref[rows, :] = e / jnp.sum(e, axis=0, keepdims=True)
    ei_ref[rows, :] = keys & (PEER_NKEYS * PEER_NKEYS - 1)


def _topk_head_pair(st_ref, h0, scratch, unroll=False):
    v1_ref, k1_ref, v2_ref, k2_ref, vt_ref, kt_ref, ei_ref, gg_ref = scratch
    row = lax.broadcasted_iota(I32, (PEER_NKEYS, st_ref.shape[2]), 0)
    cand, ckey = [], []
    for s in range(2):
        h = h0 + s
        _topk_rows([st_ref[2 * h], st_ref[2 * h + 1]], [row, row],
                   [v1_ref.at[s], v2_ref.at[s]], [k1_ref.at[s], k2_ref.at[s]], unroll)
        c, k = _product_candidates(v1_ref[s], k1_ref[s], v2_ref[s], k2_ref[s])
        cand.append(c)
        ckey.append(k)
    _topk_rows(cand, ckey, [vt_ref.at[0], vt_ref.at[1]], [kt_ref.at[0], kt_ref.at[1]], unroll)
    for s in range(2):
        _store_gates(h0 + s, vt_ref[s], kt_ref[s], ei_ref, gg_ref)


def _topk_emit(scratch, off_ref, par_ref, g_ref):
    ei = scratch[6][...].T
    off_ref[...] = (ei & (PEER_HALF - 1)) * SUB
    par_ref[...] = ei // PEER_HALF
    g_ref[...] = scratch[7][...].T


def _peer_topk_kernel(st_ref, off_ref, par_ref, g_ref, *scratch):
    for h0 in range(0, PEER_HEADS, 2):
        _topk_head_pair(st_ref, h0, scratch)
    _topk_emit(scratch, off_ref, par_ref, g_ref)


def _peer_topk(st, blk0, nblk):
    hk = PEER_HEADS * PEER_TOPK
    t = nblk * TOPK_TL
    return pl.pallas_call(
        _peer_topk_kernel,
        grid=(nblk,),
        in_specs=[pl.BlockSpec((2 * PEER_HEADS, PEER_NKEYS, TOPK_TL), lambda i: (0, 0, blk0 + i))],
        out_specs=[pl.BlockSpec((TOPK_TL, hk), lambda i: (i, 0))] * 3,
        out_shape=[jax.ShapeDtypeStruct((t, hk), I32), jax.ShapeDtypeStruct((t, hk), I32),
                   jax.ShapeDtypeStruct((t, hk), F32)],
        scratch_shapes=_topk_scratch(),
        compiler_params=pltpu.CompilerParams(dimension_semantics=("arbitrary",),
                                             vmem_limit_bytes=VMEM_LIMIT),
        name="peer_topk",
    )(st)


SUB = 8


def _erf(x):
    return lax.erf(x)


HALF_HI = 0xFFFF0000


PEER_HALF = PEER_NKEYS * PEER_NKEYS // 2


def _pack_kernel(lo_ref, hi_ref, o_ref):
    pairs = lo_ref.shape[0]
    nj = lo_ref.shape[1] // LANES
    for j in range(nj):
        cols = slice(j * LANES, (j + 1) * LANES)
        lo = pltpu.bitcast(lo_ref[:, cols].astype(BF16).astype(F32), jnp.uint32)
        hi = pltpu.bitcast(hi_ref[:, cols].astype(BF16).astype(F32), jnp.uint32)
        o_ref[pl.ds(j, pairs, stride=nj), :] = (lo >> 16) | (hi & jnp.uint32(HALF_HI))


def _pack_expert_table(w, pairs=256):
    n, d = w.shape
    nj = d // LANES
    half_blocks = n // 2 // pairs
    return pl.pallas_call(
        _pack_kernel,
        grid=(half_blocks,),
        in_specs=[pl.BlockSpec((pairs, d), lambda i: (i, 0)),
                  pl.BlockSpec((pairs, d), lambda i: (half_blocks + i, 0))],
        out_specs=pl.BlockSpec((pairs * nj, LANES), lambda i: (i, 0)),
        out_shape=jax.ShapeDtypeStruct((n // 2 * nj, LANES), jnp.uint32),
        compiler_params=pltpu.CompilerParams(dimension_semantics=("arbitrary",),
                                             vmem_limit_bytes=VMEM_LIMIT),
        name="pack_table",
    )(w, w)


def _load_table_once(tab_hbm, tab, sem):
    @pl.when(pl.program_id(0) == 0)
    def _():
        cp = pltpu.make_async_copy(tab_hbm, tab, sem)
        cp.start()
        cp.wait()


def _pair_tile(tab, off):
    return tab[pl.ds(pl.multiple_of(off, SUB), SUB), :]


def _group_matrix(rows, cols):
    r = lax.broadcasted_iota(I32, (rows, cols), 0)
    c = lax.broadcasted_iota(I32, (rows, cols), 1)
    return (c // (cols // rows) == r).astype(BF16)


def _split2(x):
    hi = x.astype(BF16)
    return hi, (x - hi.astype(F32)).astype(BF16)


PAIR_ROWS = 2 * SUB


def _pair_rows(tab, off_ref, tt, hk):
    return jnp.concatenate([pltpu.bitcast(_pair_tile(tab, off_ref[tt * hk + k]), BF16) for k in range(hk)], axis=0)


def _own_sublane_mask(hk):
    srow = lax.broadcasted_iota(I32, (SUB, hk * PAIR_ROWS), 0)
    scol = lax.broadcasted_iota(I32, (SUB, hk * PAIR_ROWS), 1)
    return (scol % PAIR_ROWS) // 2 == srow


def _fold_matrix(hk):
    c = jnp.arange(hk * PAIR_ROWS)
    return jax.nn.one_hot((c % 2) * hk + c // PAIR_ROWS, 2 * hk, dtype=BF16)


def _peer_u_body(off_ref, par_ref, h_ref, g_ref, fold_ref, w_ref, tab, d_sc):
    tq, hk = g_ref.shape
    mine = _own_sublane_mask(hk)
    nt = (((1,), (1,)), ((), ()))
    for tt in range(tq):
        h_hi, h_lo = _split2(h_ref[tt])
        s = lax.dot_general(jnp.concatenate([h_hi, h_lo], axis=0), _pair_rows(tab, off_ref, tt, hk), nt,
                            preferred_element_type=F32)
        d = jnp.where(mine, s[0:SUB] + s[SUB:2 * SUB], 0.0)
        d_sc[tt:tt + 1, :] = jnp.sum(d, axis=0, keepdims=True)
    d_hi, d_lo = _split2(d_sc[...])
    a2 = jnp.dot(jnp.concatenate([d_hi, d_lo], axis=0), fold_ref[...], preferred_element_type=F32)
    a2 = a2[0:tq] + a2[tq:2 * tq]
    a = jnp.where(par_ref[...] == 0, a2[:, :hk], a2[:, hk:])
    w_ref[...] = g_ref[...] * (0.5 * a * (1.0 + _erf(a * (2.0 ** -0.5))))


U_TQ = 64


def _peer_u_kernel(off_ref, par_ref, h_ref, g_ref, fold_ref, tab_hbm, w_ref, tab, sem, d_sc):
    _load_table_once(tab_hbm, tab, sem)
    _peer_u_body(off_ref, par_ref, h_ref, g_ref, fold_ref, w_ref, tab, d_sc)


def _peer_u_topk_kernel(off_ref, par_ref, h_ref, g_ref, fold_ref, st_ref, tab_hbm,
                        w_ref, off2_ref, par2_ref, g2_ref, tab, sem, d_sc, *scratch):
    half = pl.program_id(0) % 2
    _load_table_once(tab_hbm, tab, sem)
    for q in range(2):
        _topk_head_pair(st_ref, (2 * half + q) * 2, scratch, unroll=True)
    _peer_u_body(off_ref, par_ref, h_ref, g_ref, fold_ref, w_ref, tab, d_sc)

    @pl.when(half == 1)
    def _():
        _topk_emit(scratch, off2_ref, par2_ref, g2_ref)


def _peer_u(off_flat, par, g, h2, utab, tok0, st=None, st_blk0=None):
    t, hk = g.shape
    nj = h2.shape[1]
    tq = U_TQ
    assert t % (2 * tq) == 0 and 2 * tq == TOPK_TL and tok0 % tq == 0
    fold = _fold_matrix(hk)
    in_specs = [
        pl.BlockSpec((tq * hk,), lambda i: (i,), memory_space=pltpu.SMEM),
        pl.BlockSpec((tq, hk), lambda i: (i, 0)),
        pl.BlockSpec((tq, nj, LANES), lambda i: (tok0 // tq + i, 0, 0)),
        pl.BlockSpec((tq, hk), lambda i: (i, 0)),
        pl.BlockSpec(fold.shape, lambda i: (0, 0)),
    ]
    out_specs = [pl.BlockSpec((tq, hk), lambda i: (i, 0))]
    out_shape = [jax.ShapeDtypeStruct((t, hk), F32)]
    scratch = [pltpu.VMEM(utab.shape, jnp.uint32), pltpu.SemaphoreType.DMA(()),
               pltpu.VMEM((tq, hk * PAIR_ROWS), F32)]
    args = [off_flat, par, h2, g, fold]
    if st is not None:
        in_specs.append(pl.BlockSpec((2 * PEER_HEADS, PEER_NKEYS, TOPK_TL), lambda i: (0, 0, st_blk0 + i // 2)))
        out_specs += [pl.BlockSpec((TOPK_TL, hk), lambda i: (i // 2, 0))] * 3
        out_shape += [jax.ShapeDtypeStruct((t, hk), I32), jax.ShapeDtypeStruct((t, hk), I32),
                      jax.ShapeDtypeStruct((t, hk), F32)]
        scratch += _topk_scratch()
        args.append(st)
    in_specs.append(pl.BlockSpec(memory_space=pl.ANY))
    args.append(utab)
    return pl.pallas_call(
        _peer_u_kernel if st is None else _peer_u_topk_kernel,
        grid=(t // tq,),
        in_specs=in_specs,
        out_specs=out_specs,
        out_shape=out_shape,
        scratch_shapes=scratch,
        compiler_params=pltpu.CompilerParams(dimension_semantics=("arbitrary",),
                                             vmem_limit_bytes=VMEM_LIMIT),
        name="peer_u" if st is None else "peer_u_topk",
    )(*args)


def _peer_v_kernel(off_ref, par_ref, w_ref, x1_ref, mod_ref, fg_ref, tab_hbm, o_ref, tab, sem, wl_sc, p_sc, *, final):
    tq, hk = w_ref.shape
    d = x1_ref.shape[1]
    nj = d // LANES
    _load_table_once(tab_hbm, tab, sem)
    width = hk * PAIR_ROWS
    rep = _group_matrix(hk, width)
    w_hi, w_lo = _split2(w_ref[...])
    parl = jnp.dot(par_ref[...].astype(BF16), rep, preferred_element_type=F32)
    lane = lax.broadcasted_iota(I32, (tq, width), 1)
    wanted = (lane % 2).astype(F32) == parl
    wl_sc[0] = jnp.where(wanted, jnp.dot(w_hi, rep, preferred_element_type=F32), 0.0)
    wl_sc[1] = jnp.where(wanted, jnp.dot(w_lo, rep, preferred_element_type=F32), 0.0)
    mine = _own_sublane_mask(hk)
    for tt in range(tq):
        lhs = jnp.concatenate([jnp.where(mine, wl_sc[0, tt:tt + 1, :], 0.0),
                               jnp.where(mine, wl_sc[1, tt:tt + 1, :], 0.0)], axis=0).astype(BF16)
        r = jnp.dot(lhs, _pair_rows(tab, off_ref, tt, hk), preferred_element_type=F32)
        p_sc[tt * SUB:(tt + 1) * SUB, :] = r[0:SUB] + r[SUB:2 * SUB]
    peer = jnp.concatenate([p_sc[pl.ds(j, tq, stride=nj), :] for j in range(nj)], axis=1)
    y = x1_ref[...] + mod_ref[0, :, 5 * d:6 * d] * peer
    if final:
        y = y * lax.rsqrt(jnp.mean(y * y, axis=-1, keepdims=True) + EPS) * fg_ref[...]
    o_ref[...] = y


def _peer_v(off_flat, par, w, x1, mod3, final_g, vtab, *, seq, final, tq=64):
    t, d = x1.shape
    hk = w.shape[1]
    return pl.pallas_call(
        functools.partial(_peer_v_kernel, final=final),
        grid=(t // tq,),
        in_specs=[
            pl.BlockSpec((tq * hk,), lambda i: (i,), memory_space=pltpu.SMEM),
            pl.BlockSpec((tq, hk), lambda i: (i, 0)),
            pl.BlockSpec((tq, hk), lambda i: (i, 0)),
            pl.BlockSpec((tq, d), lambda i: (i, 0)),
            pl.BlockSpec((1, 1, mod3.shape[2]), lambda i: ((i * tq) // seq, 0, 0)),
            pl.BlockSpec((1, d), lambda i: (0, 0)),
            pl.BlockSpec(memory_space=pl.ANY),
        ],
        out_specs=pl.BlockSpec((tq, d), lambda i: (i, 0)),
        out_shape=jax.ShapeDtypeStruct((t, d), F32),
        scratch_shapes=[pltpu.VMEM(vtab.shape, jnp.uint32), pltpu.SemaphoreType.DMA(()),
                        pltpu.VMEM((2, tq, hk * PAIR_ROWS), F32), pltpu.VMEM((tq * (d // LANES), LANES), F32)],
        compiler_params=pltpu.CompilerParams(dimension_semantics=("arbitrary",),
                                             vmem_limit_bytes=VMEM_LIMIT),
        name="peer_v",
    )(off_flat, par, w, x1, mod3, final_g, vtab)


def kernel(x, c, norm1_g, norm2_g, w_ada, b_ada, w_in, w_out, lam_q1, lam_k1, lam_q2, lam_k2, subln_g, sinks,
           peer_wq, peer_keys, peer_u, peer_v, final_g):
    bsz, seq, d = x.shape
    depth = w_in.shape[0]
    t = bsz * seq
    slopes = _alibi_slopes()
    sl_c = [float(s) for s in slopes[:C_HEADS]]
    qa_bias = _slope_bias_col(slopes[C_HEADS:C_HEADS + A_HEADS] * np.float32(LOG2E))
    qb_bias = _slope_bias_col(slopes[C_HEADS + A_HEADS:] * np.float32(LOG2E))
    mods = _adaln_mods(c, w_ada, b_ada)
    x2d = x.reshape(t, d)
    av, bw = A_HEADS * HEAD_DIM, B_HEADS * HEAD_DIM
    for l in range(depth):
        lam_init = 0.8 - 0.6 * math.exp(-0.3 * l)
        mod3 = mods[l].reshape(bsz, 1, N_MOD * d)
        wn, wt = _prep_in_weights(w_in[l])
        ka, kb, qc, vc2, qta, vta, qtb, vtb, ktc = _inproj(
            x2d, mod3, norm1_g[l].reshape(1, d), wn, wt, qa_bias, qb_bias, seq=seq)
        lamv = jnp.stack([lam_q1[l], lam_k1[l], lam_q2[l], lam_k2[l]]).astype(F32)
        sg = subln_g[l].reshape(HEAD_DIM, 1).astype(F32)
        oa = _diff_attention(qta, ka, vta, lamv, sg, bsz=bsz, seq=seq, lam_init=lam_init)
        ob = _moba_attention(qtb, kb, vtb, bsz=bsz, seq=seq)
        oc = _swa_attention(sinks[l].astype(F32), qc, ktc, vc2, bsz=bsz, seq=seq, slopes=sl_c)
        wo = w_out[l].astype(BF16)
        keys = peer_keys[l].reshape(2 * PEER_HEADS, PEER_NKEYS, -1).astype(BF16)
        x1, h2, st = _outproj(oa, ob, oc, wo[:av], wo[av:av + bw], wo[av + bw:], x2d, mod3,
                              norm2_g[l].reshape(1, d), peer_wq[l].astype(BF16), keys, seq=seq)
        assert seq % TOPK_TL == 0
        blocks = seq // TOPK_TL
        utab = _pack_expert_table(peer_u[l])
        h2t = h2.reshape(t, d // LANES, LANES)
        sel = [_peer_topk(st, 0, blocks)]
        ws = []
        for b in range(bsz):
            off_b, par_b, g_b = sel[b]
            if b + 1 < bsz:
                w_b, *nxt = _peer_u(off_b.reshape(-1), par_b, g_b, h2t, utab, b * seq, st, (b + 1) * blocks)
                sel.append(nxt)
            else:
                (w_b,) = _peer_u(off_b.reshape(-1), par_b, g_b, h2t, utab, b * seq)
            ws.append(w_b)
        toff = jnp.concatenate([s_[0] for s_ in sel]).reshape(-1)
        par = jnp.concatenate([s_[1] for s_ in sel])
        w = jnp.concatenate(ws)
        x2d = _peer_v(toff, par, w, x1, mod3, final_g.reshape(1, d), _pack_expert_table(peer_v[l]),
                      seq=seq, final=(l == depth - 1))
    return x2d.reshape(bsz, seq, d)
```

```python
import functools
import math

import numpy as np
import jax
import jax.numpy as jnp
from jax import lax
from jax.experimental import pallas as pl
from jax.experimental.pallas import tpu as pltpu

F32 = jnp.float32
BF16 = jnp.bfloat16
I32 = jnp.int32

D_MODEL = 1024
HEAD_DIM = 64
N_HEADS_TOTAL = 16
A_HEADS = 4
B_HEADS = 6
C_HEADS = 6
C_KV_HEADS = 2
C_GROUP = 3
A_QK_DIM = 32
MOBA_BLOCK = 256
MOBA_TOPK = 3
WINDOW = 128
ALIBI_MAX = 8.0
PEER_HEADS = 8
PEER_NKEYS = 128
PEER_TOPK = 16
N_MOD = 6
EPS = 1e-6

LANES = 128
KT = 256
AUG0 = HEAD_DIM
SEL0 = AUG0 + 6
MAX_BLOCKS = LANES - SEL0
NEG_BIG = -1e30
LOG2E = math.log2(math.e)
VMEM_LIMIT = 56 * 1024 * 1024


def _alibi_slopes():
    n = N_HEADS_TOTAL
    return (2.0 ** (-ALIBI_MAX * np.arange(1, n + 1, dtype=np.float32) / n)).astype(np.float32)


def _split3(v):
    v = np.float32(v)
    hi = np.float32(np.asarray(v).astype(jnp.bfloat16).astype(np.float32))
    r = np.float32(v - hi)
    mid = np.float32(np.asarray(r).astype(jnp.bfloat16).astype(np.float32))
    lo = np.float32(np.float32(r - mid))
    lo = np.float32(np.asarray(lo).astype(jnp.bfloat16).astype(np.float32))
    return hi, mid, lo


def _slope_bias_col(slopes):
    col = np.zeros((LANES * len(slopes), 1), np.float32)
    for h, s in enumerate(slopes):
        hi, mid, lo = _split3(s)
        col[h * LANES + AUG0:h * LANES + AUG0 + 6, 0] = [hi, mid, lo, hi, mid, lo]
    return jnp.asarray(col)


def _mod_kernel(c_ref, w_ref, b_ref, o_ref):
    c = c_ref[...]
    cs = c * (1.0 / (1.0 + jnp.exp(-c)))
    o_ref[0] = jnp.dot(cs, w_ref[0], preferred_element_type=F32) + b_ref[0]


def _adaln_mods(c, w_ada, b_ada):
    depth, d, n = w_ada.shape
    bsz = c.shape[0]
    rows = -(-bsz // 8) * 8
    cp = jnp.pad(c, ((0, rows - bsz), (0, 0)))
    tn = 1536
    out = pl.pallas_call(
        _mod_kernel,
        grid=(depth, n // tn),
        in_specs=[
            pl.BlockSpec((rows, d), lambda l, j: (0, 0)),
            pl.BlockSpec((1, d, tn), lambda l, j: (l, 0, j)),
            pl.BlockSpec((1, 1, tn), lambda l, j: (l, 0, j)),
        ],
        out_specs=pl.BlockSpec((1, rows, tn), lambda l, j: (l, 0, j)),
        out_shape=jax.ShapeDtypeStruct((depth, rows, n), F32),
        compiler_params=pltpu.CompilerParams(dimension_semantics=("arbitrary", "arbitrary"),
                                             vmem_limit_bytes=VMEM_LIMIT),
        name="adaln_mods",
    )(cp, w_ada, b_ada.reshape(depth, 1, n))
    return out[:, :bsz]


NN_WIDTHS = (A_HEADS * LANES, B_HEADS * LANES, C_HEADS * HEAD_DIM, 2 * C_KV_HEADS * HEAD_DIM)
NT_ROWS = (A_HEADS * LANES, A_HEADS * HEAD_DIM, B_HEADS * LANES, B_HEADS * HEAD_DIM,
           C_KV_HEADS * LANES)


def _prep_in_weights(w):
    d = w.shape[0]
    aq, ak, av = A_HEADS * 2 * A_QK_DIM, A_HEADS * 2 * A_QK_DIM, A_HEADS * HEAD_DIM
    bw = B_HEADS * HEAD_DIM
    cq, ckv = C_HEADS * HEAD_DIM, C_KV_HEADS * HEAD_DIM
    cuts = np.cumsum([aq, ak, av, bw, bw, bw, cq, ckv]).tolist()
    qa, ka, va, qb, kb, vb, qc, kc, vc = jnp.split(w, cuts, axis=-1)

    def pad_heads(m, nh, scale):
        m = (m * scale).reshape(d, nh, HEAD_DIM)
        return jnp.pad(m, ((0, 0), (0, 0), (0, LANES - HEAD_DIM))).reshape(d, nh * LANES)

    vc2 = vc.reshape(d, C_KV_HEADS, 1, HEAD_DIM)
    vc2 = jnp.broadcast_to(vc2, (d, C_KV_HEADS, 2, HEAD_DIM)).reshape(d, 2 * ckv)
    kc2 = jnp.broadcast_to(kc.reshape(d, C_KV_HEADS, 1, HEAD_DIM), (d, C_KV_HEADS, 2, HEAD_DIM)).reshape(d, 2 * ckv)
    wn = jnp.concatenate([pad_heads(ka, A_HEADS, 1.0), pad_heads(kb, B_HEADS, 1.0),
                          qc * (HEAD_DIM ** -0.5), vc2], axis=1)
    wt = jnp.concatenate([pad_heads(qa, A_HEADS, A_QK_DIM ** -0.5 * LOG2E), va,
                          pad_heads(qb, B_HEADS, HEAD_DIM ** -0.5 * LOG2E), vb, kc2], axis=1).T
    return wn.astype(BF16), wt.astype(BF16)


def _rms_mod(x, g, sc, sh):
    ms = jnp.mean(x * x, axis=-1, keepdims=True)
    return (x * lax.rsqrt(ms + EPS) * g) * (1.0 + sc) + sh


def _inproj_kernel(x_ref, mod_ref, g_ref, wn_ref, wt_ref, qab_ref, qbb_ref,
                   ka_ref, kb_ref, qc_ref, vc_ref, qta_ref, vta_ref, qtb_ref, vtb_ref, ktc_ref,
                   *, tm, seq):
    d = x_ref.shape[1]
    x = x_ref[...]
    sh = mod_ref[0, :, 0:d]
    sc = mod_ref[0, :, d:2 * d]
    h = _rms_mod(x, g_ref[...], sc, sh).astype(BF16)
    pn = jnp.dot(h, wn_ref[...], preferred_element_type=F32)
    pt = lax.dot_general(wt_ref[...], h, (((1,), (1,)), ((), ())),
                         preferred_element_type=F32)

    pos = (pl.program_id(0) * tm) % seq + lax.broadcasted_iota(I32, (tm, LANES), 0)
    col = lax.broadcasted_iota(I32, (tm, LANES), 1)
    blk_id = pos // KT
    p_hi = (blk_id * KT).astype(F32)
    p_lo = (pos - blk_id * KT).astype(F32)
    aug_a = jnp.where((col >= AUG0) & (col < AUG0 + 3), p_hi,
                      jnp.where((col >= AUG0 + 3) & (col < AUG0 + 6), p_lo, 0.0))
    aug_b = jnp.where((col >= SEL0) & (col - SEL0 == blk_id), 1.0, aug_a)
    for hh in range(A_HEADS):
        ka_ref[:, hh * LANES:(hh + 1) * LANES] = (pn[:, hh * LANES:(hh + 1) * LANES] + aug_a).astype(BF16)
    o = NN_WIDTHS[0]
    for hh in range(B_HEADS):
        kb_ref[:, hh * LANES:(hh + 1) * LANES] = (pn[:, o + hh * LANES:o + (hh + 1) * LANES] + aug_b).astype(BF16)
    o += NN_WIDTHS[1]
    qc_ref[...] = pn[:, o:o + NN_WIDTHS[2]].astype(BF16)
    o += NN_WIDTHS[2]
    vc_ref[...] = pn[:, o:o + NN_WIDTHS[3]].astype(BF16)

    r0 = 0
    for ref, b, nr in zip((qta_ref, vta_ref, qtb_ref, vtb_ref, ktc_ref),
                          (qab_ref, None, qbb_ref, None, None), NT_ROWS):
        blk = pt[r0:r0 + nr, :]
        if b is not None:
            blk = blk + b[...]
        blk = blk.astype(BF16)
        for cc in range(tm // KT):
            ref[cc] = blk[:, cc * KT:(cc + 1) * KT]
        r0 += nr


def _inproj(x2d, mod3, g, wn, wt, qa_bias, qb_bias, *, seq, tm=512):
    t, d = x2d.shape
    assert seq % tm == 0 and tm % KT == 0
    nt = t // tm
    nn_total = sum(NN_WIDTHS)
    row_specs = [pl.BlockSpec((tm, wd), lambda i: (i, 0)) for wd in NN_WIDTHS]
    kt_specs = [pl.BlockSpec((tm // KT, r, KT), lambda i: (i, 0, 0)) for r in NT_ROWS]
    out_shape = ([jax.ShapeDtypeStruct((t, wd), BF16) for wd in NN_WIDTHS]
                 + [jax.ShapeDtypeStruct((t // KT, r, KT), BF16) for r in NT_ROWS])
    return pl.pallas_call(
        functools.partial(_inproj_kernel, tm=tm, seq=seq),
        grid=(nt,),
        in_specs=[
            pl.BlockSpec((tm, d), lambda i: (i, 0)),
            pl.BlockSpec((1, 1, mod3.shape[2]), lambda i: ((i * tm) // seq, 0, 0)),
            pl.BlockSpec((1, d), lambda i: (0, 0)),
            pl.BlockSpec((d, nn_total), lambda i: (0, 0)),
            pl.BlockSpec((sum(NT_ROWS), d), lambda i: (0, 0)),
            pl.BlockSpec((NT_ROWS[0], 1), lambda i: (0, 0)),
            pl.BlockSpec((NT_ROWS[2], 1), lambda i: (0, 0)),
        ],
        out_specs=row_specs + kt_specs,
        out_shape=out_shape,
        compiler_params=pltpu.CompilerParams(dimension_semantics=("arbitrary",),
                                             vmem_limit_bytes=VMEM_LIMIT),
        name="inproj",
    )(x2d, mod3, g, wn, wt, qa_bias, qb_bias)


ACC_ROWS = HEAD_DIM + 16


def _softmax_pv(ss, vt1s, m_ref, acc_ref):
    nh = len(ss)
    m_prev = [m_ref[hh] for hh in range(nh)]
    m_new = [jnp.maximum(m_prev[hh], jnp.max(ss[hh], axis=0, keepdims=True)) for hh in range(nh)]
    ps = [jnp.exp2((ss[hh] - m_new[hh]).astype(BF16)) for hh in range(nh)]
    pv = [jnp.dot(vt1s[hh], ps[hh], preferred_element_type=F32) for hh in range(nh)]
    for hh in range(nh):
        acc_ref[hh] = jnp.exp2(m_prev[hh] - m_new[hh]) * acc_ref[hh] + pv[hh]
        m_ref[hh] = m_new[hh]


def _flash_init(m_ref, acc_ref):
    m_ref[...] = jnp.full(m_ref.shape, -jnp.inf, F32)
    acc_ref[...] = jnp.zeros(acc_ref.shape, F32)


def _flash_finish(acc_ref, hh):
    acc = acc_ref[hh]
    return acc[0:HEAD_DIM] / acc[HEAD_DIM:HEAD_DIM + 1]


PAST_TILES = 4
SERIAL_TILES = 2


def _flash_causal_pair(k_ref, vt_ref, q_diag, q_past, qi, causal, m_ref, acc_ref, s_ref=None):
    _flash_init(m_ref, acc_ref)
    nkeys = PAST_TILES * KT
    nh = len(q_past)

    def vt1(kj, n, hh):
        rows = slice(hh * HEAD_DIM, (hh + 1) * HEAD_DIM)
        vt = vt_ref[kj, rows, :] if n == 1 else jnp.concatenate([vt_ref[kj + c, rows, :] for c in range(n)], axis=1)
        return jnp.concatenate([vt, jnp.ones((ACC_ROWS - HEAD_DIM, n * KT), BF16)], axis=0)

    def scores(kj, n, qs, hh):
        kk = k_ref[pl.ds(pl.multiple_of(kj * KT, KT), n * KT), hh * LANES:(hh + 1) * LANES]
        return jnp.dot(kk, qs[hh], preferred_element_type=F32)

    def step(kj, n, qs, mask=None):
        ss = [scores(kj, n, qs, hh) for hh in range(nh)]
        if mask is not None:
            ss = [jnp.where(mask, s, -jnp.inf) for s in ss]
        _softmax_pv(ss, [vt1(kj, n, hh) for hh in range(nh)], m_ref, acc_ref)

    step(qi, 1, q_diag, causal)

    if s_ref is None:
        def body(j, carry):
            step(j * SERIAL_TILES, SERIAL_TILES, q_past)
            return carry

        lax.fori_loop(0, qi // SERIAL_TILES, body, 0)
        for r in range(SERIAL_TILES - 1):
            @pl.when(qi % SERIAL_TILES > r)
            def _():
                step(qi - 1 - r, 1, q_past)
        return

    ngroups = k_ref.shape[0] // nkeys

    def scores_into(slot, g):
        for hh in range(nh):
            s_ref[slot, hh] = scores(g * PAST_TILES, PAST_TILES, q_past, hh)

    scores_into(0, 0)

    def body(g, carry):
        scores_into((g + 1) % 2, jnp.minimum(g + 1, ngroups - 1))
        _softmax_pv([s_ref[g % 2, hh] for hh in range(nh)],
                    [vt1(g * PAST_TILES, PAST_TILES, hh) for hh in range(nh)], m_ref, acc_ref)
        return carry

    lax.fori_loop(0, (qi + PAST_TILES - 1) // PAST_TILES, body, 0)


def _diff_kernel(qt_ref, k_ref, vt_ref, lamv_ref, sg_ref, o_ref, m_ref, acc_ref, *, lam_init):
    tq = qt_ref.shape[2]
    qi = pl.program_id(2)
    lv = lamv_ref[...]
    lam = (jnp.exp(jnp.sum(lv[0:1] * lv[1:2], axis=-1, keepdims=True))
           - jnp.exp(jnp.sum(lv[2:3] * lv[3:4], axis=-1, keepdims=True)) + lam_init)
    row = lax.broadcasted_iota(I32, (LANES, tq), 0)
    kr = lax.broadcasted_iota(I32, (KT, 2 * tq), 0)
    qc = lax.broadcasted_iota(I32, (KT, 2 * tq), 1)
    causal = kr <= jnp.where(qc >= tq, qc - tq, qc)
    nh = qt_ref.shape[1] // LANES
    qs = []
    for hh in range(nh):
        qt = qt_ref[0, hh * LANES:(hh + 1) * LANES, :]
        zero = jnp.zeros_like(qt)
        q1 = jnp.where((row < A_QK_DIM) | (row >= AUG0), qt, zero)
        q2 = jnp.where(row >= A_QK_DIM, qt, zero)
        qs.append(jnp.concatenate([q1, q2], axis=1))
    _flash_causal_pair(k_ref, vt_ref, qs, qs, qi, causal, m_ref, acc_ref)
    res = []
    for hh in range(nh):
        o = _flash_finish(acc_ref, hh)
        od = o[:, :tq] - lam * o[:, tq:]
        ms = jnp.mean(od * od, axis=0, keepdims=True)
        res.append(od * lax.rsqrt(ms + EPS) * sg_ref[...] * (1.0 - lam_init))
    o_ref[...] = jnp.concatenate(res, axis=0).T.astype(BF16)


def _diff_attention(qta, ka, vta, lamv, sg, *, bsz, seq, lam_init, nh=A_HEADS):
    tq = KT
    nq = seq // tq
    t = bsz * seq
    return pl.pallas_call(
        functools.partial(_diff_kernel, lam_init=lam_init),
        grid=(bsz, A_HEADS // nh, nq),
        in_specs=[
            pl.BlockSpec((1, nh * LANES, tq), lambda b, hp, qi: (b * nq + qi, hp, 0)),
            pl.BlockSpec((seq, nh * LANES), lambda b, hp, qi: (b, hp)),
            pl.BlockSpec((nq, nh * HEAD_DIM, KT), lambda b, hp, qi: (b, hp, 0)),
            pl.BlockSpec((4, A_QK_DIM), lambda b, hp, qi: (0, 0)),
            pl.BlockSpec((HEAD_DIM, 1), lambda b, hp, qi: (0, 0)),
        ],
        out_specs=pl.BlockSpec((tq, nh * HEAD_DIM), lambda b, hp, qi: (b * nq + qi, hp)),
        out_shape=jax.ShapeDtypeStruct((t, A_HEADS * HEAD_DIM), BF16),
        scratch_shapes=[pltpu.VMEM((nh, 1, 2 * tq), F32), pltpu.VMEM((nh, ACC_ROWS, 2 * tq), F32)],
        compiler_params=pltpu.CompilerParams(dimension_semantics=("arbitrary",) * 3,
                                             vmem_limit_bytes=VMEM_LIMIT),
        name="diff_attn",
    )(qta, ka, vta, lamv, sg)


def _moba_kernel(qt_ref, k_ref, vt_ref, o_ref, m_ref, acc_ref, km_ref, s_ref):
    tq = qt_ref.shape[2]
    nh = qt_ref.shape[1] // LANES
    nb = vt_ref.shape[0]
    qi = pl.program_id(2)

    @pl.when(qi == 0)
    def _():
        lane1 = lax.broadcasted_iota(I32, (1, LANES), 1)
        for hh in range(nh):
            km_ref[hh] = jnp.zeros((LANES, LANES), F32)

            def put_block(j, carry):
                blk = k_ref[pl.ds(pl.multiple_of(j * KT, KT), KT), hh * LANES:(hh + 1) * LANES].astype(F32)
                mean = jnp.sum(blk, axis=0, keepdims=True) * (1.0 / KT)
                km_ref[hh, pl.ds(SEL0 + j, 1), :] = jnp.where(lane1 < HEAD_DIM, mean, 0.0)
                return carry
            lax.fori_loop(0, nb, put_block, 0)

    kr = lax.broadcasted_iota(I32, (KT, tq), 0)
    qc = lax.broadcasted_iota(I32, (KT, tq), 1)
    causal = kr <= qc
    nsel = -(-(SEL0 - AUG0 + nb) // 16) * 16
    band = slice(AUG0, AUG0 + nsel)
    row = lax.broadcasted_iota(I32, (nsel, tq), 0) + AUG0
    in_sel = row >= SEL0
    qts = [qt_ref[0, hh * LANES:(hh + 1) * LANES, :] for hh in range(nh)]
    curs = []
    for hh in range(nh):
        km_hi, km_lo = _split2(km_ref[hh, band, :])
        gate = (jnp.dot(km_hi, qts[hh], preferred_element_type=F32)
                + jnp.dot(km_lo, qts[hh], preferred_element_type=F32))
        curs.append(jnp.where(in_sel & (row < SEL0 + qi), gate, -jnp.inf))
    sels = [jnp.zeros((nsel, tq), jnp.bool_)] * nh
    for _ in range(MOBA_TOPK):
        for hh in range(nh):
            mx = jnp.max(curs[hh], axis=0, keepdims=True)
            first = jnp.min(jnp.where(curs[hh] == mx, row, 4 * LANES), axis=0, keepdims=True)
            pick = (row == first) & (mx > -jnp.inf)
            sels[hh] = sels[hh] | pick
            curs[hh] = jnp.where(pick, -jnp.inf, curs[hh])
    q_diag, q_past = [], []
    for hh in range(nh):
        qt, mid = qts[hh], qts[hh][band]
        rest = [qt[AUG0 + nsel:]] if AUG0 + nsel < LANES else []
        q_diag.append(jnp.concatenate([qt[:AUG0], jnp.where(in_sel, jnp.zeros_like(mid), mid)] + rest, axis=0))
        bias = jnp.where(sels[hh], 0.0, NEG_BIG).astype(BF16)
        q_past.append(jnp.concatenate([qt[:AUG0], jnp.where(in_sel, bias, mid)] + rest, axis=0))
    _flash_causal_pair(k_ref, vt_ref, q_diag, q_past, qi, causal, m_ref, acc_ref, s_ref)
    res = [_flash_finish(acc_ref, hh) for hh in range(nh)]
    o_ref[...] = jnp.concatenate(res, axis=0).T.astype(BF16)


def _moba_attention(qtb, kb, vtb, *, bsz, seq, nh=B_HEADS):
    tq = KT
    nq = seq // tq
    assert nq <= MAX_BLOCKS and seq % (PAST_TILES * KT) == 0
    t = bsz * seq
    return pl.pallas_call(
        _moba_kernel,
        grid=(bsz, B_HEADS // nh, nq),
        in_specs=[
            pl.BlockSpec((1, nh * LANES, tq), lambda b, hp, qi: (b * nq + qi, hp, 0)),
            pl.BlockSpec((seq, nh * LANES), lambda b, hp, qi: (b, hp), pipeline_mode=pl.Buffered(1)),
            pl.BlockSpec((nq, nh * HEAD_DIM, KT), lambda b, hp, qi: (b, hp, 0), pipeline_mode=pl.Buffered(1)),
        ],
        out_specs=pl.BlockSpec((tq, nh * HEAD_DIM), lambda b, hp, qi: (b * nq + qi, hp)),
        out_shape=jax.ShapeDtypeStruct((t, B_HEADS * HEAD_DIM), BF16),
        scratch_shapes=[pltpu.VMEM((nh, 1, tq), F32), pltpu.VMEM((nh, ACC_ROWS, tq), F32),
                        pltpu.VMEM((nh, LANES, LANES), F32), pltpu.VMEM((2, nh, PAST_TILES * KT, tq), F32)],
        compiler_params=pltpu.CompilerParams(dimension_semantics=("arbitrary",) * 3,
                                             vmem_limit_bytes=VMEM_LIMIT),
        name="moba_attn",
    )(qtb, kb, vtb)


def _swa_kernel(sink_ref, q_ref, ktp_ref, ktc_ref, vp_ref, vc_ref, o_ref, *, slopes):
    w = WINDOW
    n = pl.program_id(1)
    lane = lax.broadcasted_iota(I32, (w, LANES), 1)
    r2 = lax.broadcasted_iota(I32, (w, 2 * w), 0)
    c2 = lax.broadcasted_iota(I32, (w, 2 * w), 1)
    rel = r2 + w - c2
    band = (rel >= 0) & (rel < w)
    relf = rel.astype(F32)
    for half in range(KT // w):
        mask = band & ((c2 >= w) | (n > 0)) if half == 0 else band
        rows = slice(half * w, (half + 1) * w)
        res = []
        for hq in range(C_HEADS):
            kv = hq // C_GROUP
            kvl = slice(kv * LANES, (kv + 1) * LANES)
            qp = q_ref[rows, (hq // 2) * LANES:(hq // 2 + 1) * LANES]
            qm = jnp.where((lane < HEAD_DIM) == (hq % 2 == 0), qp, jnp.zeros_like(qp))
            if half == 0:
                kt = jnp.concatenate([ktp_ref[0, kvl, KT - w:KT], ktc_ref[0, kvl, 0:w]], axis=1)
                vv = jnp.concatenate([vp_ref[KT - w:KT, kvl], vc_ref[0:w, kvl]], axis=0)
            else:
                kt = ktc_ref[0, kvl, (half - 1) * w:(half + 1) * w]
                vv = vc_ref[(half - 1) * w:(half + 1) * w, kvl]
            s = jnp.dot(qm, kt, preferred_element_type=F32)
            s = jnp.where(mask, s - slopes[hq] * relf, -jnp.inf)
            sink = sink_ref[hq]
            m = jnp.maximum(jnp.max(s, axis=-1, keepdims=True), sink)
            e = jnp.exp(s - m)
            den = jnp.sum(e, axis=-1, keepdims=True) + jnp.exp(sink - m)
            p = (e / den).astype(BF16)
            res.append(jnp.dot(p, vv, preferred_element_type=F32))
        for pr in range(C_HEADS // 2):
            o_ref[rows, pr * LANES:(pr + 1) * LANES] = jnp.where(lane < HEAD_DIM, res[2 * pr],
                                                                 res[2 * pr + 1]).astype(BF16)


def _swa_attention(sinks, qc, ktc, vc2, *, bsz, seq, slopes):
    tq = KT
    nq = seq // tq
    t = bsz * seq
    return pl.pallas_call(
        functools.partial(_swa_kernel, slopes=slopes),
        grid=(bsz, nq),
        in_specs=[
            pl.BlockSpec(memory_space=pltpu.SMEM),
            pl.BlockSpec((tq, C_HEADS * HEAD_DIM), lambda b, n: (b * nq + n, 0)),
            pl.BlockSpec((1, C_KV_HEADS * LANES, KT), lambda b, n: (b * nq + jnp.maximum(n - 1, 0), 0, 0)),
            pl.BlockSpec((1, C_KV_HEADS * LANES, KT), lambda b, n: (b * nq + n, 0, 0)),
            pl.BlockSpec((tq, C_KV_HEADS * LANES), lambda b, n: (b * nq + jnp.maximum(n - 1, 0), 0)),
            pl.BlockSpec((tq, C_KV_HEADS * LANES), lambda b, n: (b * nq + n, 0)),
        ],
        out_specs=pl.BlockSpec((tq, C_HEADS * HEAD_DIM), lambda b, n: (b * nq + n, 0)),
        out_shape=jax.ShapeDtypeStruct((t, C_HEADS * HEAD_DIM), BF16),
        compiler_params=pltpu.CompilerParams(dimension_semantics=("arbitrary",) * 2,
                                             vmem_limit_bytes=VMEM_LIMIT),
        name="swa_attn",
    )(sinks, qc, ktc, ktc, vc2, vc2)


def _outproj_kernel(oa_ref, ob_ref, oc_ref, woa_ref, wob_ref, woc_ref, x_ref, mod_ref, g_ref, wq_ref, keys_ref,
                    x1_ref, h2_ref, st_ref):
    d = x_ref.shape[1]
    mix = (jnp.dot(oa_ref[...], woa_ref[...], preferred_element_type=F32)
           + jnp.dot(ob_ref[...], wob_ref[...], preferred_element_type=F32)
           + jnp.dot(oc_ref[...], woc_ref[...], preferred_element_type=F32))
    g1 = mod_ref[0, :, 2 * d:3 * d]
    sh2 = mod_ref[0, :, 3 * d:4 * d]
    sc2 = mod_ref[0, :, 4 * d:5 * d]
    x1 = x_ref[...] + g1 * mix
    x1_ref[...] = x1
    h2 = _rms_mod(x1, g_ref[...], sc2, sh2)
    nj = d // LANES
    for j in range(nj):
        h2_ref[pl.ds(j, x1.shape[0], stride=nj), :] = h2[:, j * LANES:(j + 1) * LANES]
    pq = jnp.dot(h2.astype(BF16), wq_ref[...], preferred_element_type=F32).astype(BF16)
    for hp in range(2 * PEER_HEADS):
        st_ref[hp] = lax.dot_general(keys_ref[hp], pq[:, hp * LANES:(hp + 1) * LANES],
                                     (((1,), (1,)), ((), ())), preferred_element_type=F32)


def _outproj(oa, ob, oc, woa, wob, woc, x2d, mod3, g, wq, keys, *, seq, tm=256):
    t, d = x2d.shape
    nt = t // tm
    nq = wq.shape[1]
    full = lambda a: pl.BlockSpec(a.shape, lambda i: (0,) * a.ndim)
    return pl.pallas_call(
        _outproj_kernel,
        grid=(nt,),
        in_specs=[
            pl.BlockSpec((tm, oa.shape[1]), lambda i: (i, 0)),
            pl.BlockSpec((tm, ob.shape[1]), lambda i: (i, 0)),
            pl.BlockSpec((tm, oc.shape[1]), lambda i: (i, 0)),
            full(woa), full(wob), full(woc),
            pl.BlockSpec((tm, d), lambda i: (i, 0)),
            pl.BlockSpec((1, 1, mod3.shape[2]), lambda i: ((i * tm) // seq, 0, 0)),
            pl.BlockSpec((1, d), lambda i: (0, 0)),
            full(wq), full(keys),
        ],
        out_specs=[pl.BlockSpec((tm, d), lambda i: (i, 0)),
                   pl.BlockSpec((tm * (d // LANES), LANES), lambda i: (i, 0)),
                   pl.BlockSpec((2 * PEER_HEADS, PEER_NKEYS, tm), lambda i: (0, 0, i))],
        out_shape=[jax.ShapeDtypeStruct((t, d), F32), jax.ShapeDtypeStruct((t * (d // LANES), LANES), F32),
                   jax.ShapeDtypeStruct((2 * PEER_HEADS, PEER_NKEYS, t), F32)],
        compiler_params=pltpu.CompilerParams(dimension_semantics=("arbitrary",),
                                             vmem_limit_bytes=VMEM_LIMIT),
        name="outproj_peerq",
    )(oa, ob, oc, woa, wob, woc, x2d, mod3, g, wq, keys)


_CAND_BLOCKS = ((0, 16),) + tuple((i, 8) for i in range(1, 8))
_CAND_ROWS = 16 + 7 * 8 + 8
_BIG_I = np.int32(2 ** 30)


TOPK_TL = LANES


def _topk_rows(curs, keys, val_refs, key_refs, unroll):
    def body(r, curs):
        out = []
        for cur, key, val_ref, key_ref in zip(curs, keys, val_refs, key_refs):
            m = jnp.max(cur, axis=0, keepdims=True)
            kmin = jnp.min(jnp.where(cur == m, key, _BIG_I), axis=0, keepdims=True)
            val_ref[pl.ds(r, 1), :] = m
            key_ref[pl.ds(r, 1), :] = kmin
            out.append(jnp.where(key == kmin, -jnp.inf, cur))
        return tuple(out)
    lax.fori_loop(0, PEER_TOPK, body, tuple(curs), unroll=unroll)


def _topk_scratch():
    pair = [pltpu.VMEM((2, PEER_TOPK, TOPK_TL), F32), pltpu.VMEM((2, PEER_TOPK, TOPK_TL), I32)]
    hk = PEER_HEADS * PEER_TOPK
    return pair * 3 + [pltpu.VMEM((hk, TOPK_TL), I32), pltpu.VMEM((hk, TOPK_TL), F32)]


def _product_candidates(sv1, si1, sv2, si2):
    tl = sv1.shape[1]
    jrow8 = lax.broadcasted_iota(I32, (8, tl), 0)
    jrow16 = lax.broadcasted_iota(I32, (16, tl), 0)
    cands, keys = [], []
    for i, nj in _CAND_BLOCKS:
        jrow = jrow16 if nj == 16 else jrow8
        cands.append(sv1[i:i + 1] + sv2[0:nj])
        keys.append((i * PEER_TOPK + jrow) * (PEER_NKEYS * PEER_NKEYS) + si1[i:i + 1] * PEER_NKEYS + si2[0:nj])
    cands.append(sv1[8:16] + sv2[0:1])
    keys.append((jrow8 + 8) * (PEER_TOPK * PEER_NKEYS * PEER_NKEYS) + si1[8:16] * PEER_NKEYS + si2[0:1])
    return jnp.concatenate(cands, axis=0), jnp.concatenate(keys, axis=0)


def _store_gates(h, top, keys, ei_ref, gg_ref):
    rows = pl.ds(pl.multiple_of(h * PEER_TOPK, PEER_TOPK), PEER_TOPK)
    e = jnp.exp(top - top[0:1])
    gg_ref[rows, :] = e / jnp.sum(e, axis=0, keepdims=True)
    ei_ref[rows, :] = keys & (PEER_NKEYS * PEER_NKEYS - 1)


def _topk_head_pair(st_ref, h0, scratch, unroll=False):
    v1_ref, k1_ref, v2_ref, k2_ref, vt_ref, kt_ref, ei_ref, gg_ref = scratch
    row = lax.broadcasted_iota(I32, (PEER_NKEYS, st_ref.shape[2]), 0)
    cand, ckey = [], []
    for s in range(2):
        h = h0 + s
        _topk_rows([st_ref[2 * h], st_ref[2 * h + 1]], [row, row],
                   [v1_ref.at[s], v2_ref.at[s]], [k1_ref.at[s], k2_ref.at[s]], unroll)
        c, k = _product_candidates(v1_ref[s], k1_ref[s], v2_ref[s], k2_ref[s])
        cand.append(c)
        ckey.append(k)
    _topk_rows(cand, ckey, [vt_ref.at[0], vt_ref.at[1]], [kt_ref.at[0], kt_ref.at[1]], unroll)
    for s in range(2):
        _store_gates(h0 + s, vt_ref[s], kt_ref[s], ei_ref, gg_ref)


def _topk_emit(scratch, off_ref, par_ref, g_ref):
    ei = scratch[6][...].T
    off_ref[...] = (ei & (PEER_HALF - 1)) * SUB
    par_ref[...] = ei // PEER_HALF
    g_ref[...] = scratch[7][...].T


def _peer_topk_kernel(st_ref, off_ref, par_ref, g_ref, *scratch):
    for h0 in range(0, PEER_HEADS, 2):
        _topk_head_pair(st_ref, h0, scratch)
    _topk_emit(scratch, off_ref, par_ref, g_ref)


def _peer_topk(st, blk0, nblk):
    hk = PEER_HEADS * PEER_TOPK
    t = nblk * TOPK_TL
    return pl.pallas_call(
        _peer_topk_kernel,
        grid=(nblk,),
        in_specs=[pl.BlockSpec((2 * PEER_HEADS, PEER_NKEYS, TOPK_TL), lambda i: (0, 0, blk0 + i))],
        out_specs=[pl.BlockSpec((TOPK_TL, hk), lambda i: (i, 0))] * 3,
        out_shape=[jax.ShapeDtypeStruct((t, hk), I32), jax.ShapeDtypeStruct((t, hk), I32),
                   jax.ShapeDtypeStruct((t, hk), F32)],
        scratch_shapes=_topk_scratch(),
        compiler_params=pltpu.CompilerParams(dimension_semantics=("arbitrary",),
                                             vmem_limit_bytes=VMEM_LIMIT),
        name="peer_topk",
    )(st)


SUB = 8


def _erf(x):
    return lax.erf(x)


HALF_HI = 0xFFFF0000


PEER_HALF = PEER_NKEYS * PEER_NKEYS // 2


def _pack_kernel(lo_ref, hi_ref, o_ref):
    pairs = lo_ref.shape[0]
    nj = lo_ref.shape[1] // LANES
    for j in range(nj):
        cols = slice(j * LANES, (j + 1) * LANES)
        lo = pltpu.bitcast(lo_ref[:, cols].astype(BF16).astype(F32), jnp.uint32)
        hi = pltpu.bitcast(hi_ref[:, cols].astype(BF16).astype(F32), jnp.uint32)
        o_ref[pl.ds(j, pairs, stride=nj), :] = (lo >> 16) | (hi & jnp.uint32(HALF_HI))


def _pack_expert_table(w, pairs=256):
    n, d = w.shape
    nj = d // LANES
    half_blocks = n // 2 // pairs
    return pl.pallas_call(
        _pack_kernel,
        grid=(half_blocks,),
        in_specs=[pl.BlockSpec((pairs, d), lambda i: (i, 0)),
                  pl.BlockSpec((pairs, d), lambda i: (half_blocks + i, 0))],
        out_specs=pl.BlockSpec((pairs * nj, LANES), lambda i: (i, 0)),
        out_shape=jax.ShapeDtypeStruct((n // 2 * nj, LANES), jnp.uint32),
        compiler_params=pltpu.CompilerParams(dimension_semantics=("arbitrary",),
                                             vmem_limit_bytes=VMEM_LIMIT),
        name="pack_table",
    )(w, w)


def _load_table_once(tab_hbm, tab, sem):
    @pl.when(pl.program_id(0) == 0)
    def _():
        cp = pltpu.make_async_copy(tab_hbm, tab, sem)
        cp.start()
        cp.wait()


def _pair_tile(tab, off):
    return tab[pl.ds(pl.multiple_of(off, SUB), SUB), :]


def _group_matrix(rows, cols):
    r = lax.broadcasted_iota(I32, (rows, cols), 0)
    c = lax.broadcasted_iota(I32, (rows, cols), 1)
    return (c // (cols // rows) == r).astype(BF16)


def _split2(x):
    hi = x.astype(BF16)
    return hi, (x - hi.astype(F32)).astype(BF16)


PAIR_ROWS = 2 * SUB


def _pair_rows(tab, off_ref, tt, hk):
    return jnp.concatenate([pltpu.bitcast(_pair_tile(tab, off_ref[tt * hk + k]), BF16) for k in range(hk)], axis=0)


def _own_sublane_mask(hk):
    srow = lax.broadcasted_iota(I32, (SUB, hk * PAIR_ROWS), 0)
    scol = lax.broadcasted_iota(I32, (SUB, hk * PAIR_ROWS), 1)
    return (scol % PAIR_ROWS) // 2 == srow


def _fold_matrix(hk):
    c = jnp.arange(hk * PAIR_ROWS)
    return jax.nn.one_hot((c % 2) * hk + c // PAIR_ROWS, 2 * hk, dtype=BF16)


def _peer_u_body(off_ref, par_ref, h_ref, g_ref, fold_ref, w_ref, tab, d_sc):
    tq, hk = g_ref.shape
    mine = _own_sublane_mask(hk)
    nt = (((1,), (1,)), ((), ()))
    for tt in range(tq):
        h_hi, h_lo = _split2(h_ref[tt])
        s = lax.dot_general(jnp.concatenate([h_hi, h_lo], axis=0), _pair_rows(tab, off_ref, tt, hk), nt,
                            preferred_element_type=F32)
        d = jnp.where(mine, s[0:SUB] + s[SUB:2 * SUB], 0.0)
        d_sc[tt:tt + 1, :] = jnp.sum(d, axis=0, keepdims=True)
    d_hi, d_lo = _split2(d_sc[...])
    a2 = jnp.dot(jnp.concatenate([d_hi, d_lo], axis=0), fold_ref[...], preferred_element_type=F32)
    a2 = a2[0:tq] + a2[tq:2 * tq]
    a = jnp.where(par_ref[...] == 0, a2[:, :hk], a2[:, hk:])
    w_ref[...] = g_ref[...] * (0.5 * a * (1.0 + _erf(a * (2.0 ** -0.5))))


U_TQ = 64


def _peer_u_kernel(off_ref, par_ref, h_ref, g_ref, fold_ref, tab_hbm, w_ref, tab, sem, d_sc):
    _load_table_once(tab_hbm, tab, sem)
    _peer_u_body(off_ref, par_ref, h_ref, g_ref, fold_ref, w_ref, tab, d_sc)


def _peer_u_topk_kernel(off_ref, par_ref, h_ref, g_ref, fold_ref, st_ref, tab_hbm,
                        w_ref, off2_ref, par2_ref, g2_ref, tab, sem, d_sc, *scratch):
    half = pl.program_id(0) % 2
    _load_table_once(tab_hbm, tab, sem)
    for q in range(2):
        _topk_head_pair(st_ref, (2 * half + q) * 2, scratch, unroll=True)
    _peer_u_body(off_ref, par_ref, h_ref, g_ref, fold_ref, w_ref, tab, d_sc)

    @pl.when(half == 1)
    def _():
        _topk_emit(scratch, off2_ref, par2_ref, g2_ref)


def _peer_u(off_flat, par, g, h2, utab, tok0, st=None, st_blk0=None):
    t, hk = g.shape
    nj = h2.shape[1]
    tq = U_TQ
    assert t % (2 * tq) == 0 and 2 * tq == TOPK_TL and tok0 % tq == 0
    fold = _fold_matrix(hk)
    in_specs = [
        pl.BlockSpec((tq * hk,), lambda i: (i,), memory_space=pltpu.SMEM),
        pl.BlockSpec((tq, hk), lambda i: (i, 0)),
        pl.BlockSpec((tq, nj, LANES), lambda i: (tok0 // tq + i, 0, 0)),
        pl.BlockSpec((tq, hk), lambda i: (i, 0)),
        pl.BlockSpec(fold.shape, lambda i: (0, 0)),
    ]
    out_specs = [pl.BlockSpec((tq, hk), lambda i: (i, 0))]
    out_shape = [jax.ShapeDtypeStruct((t, hk), F32)]
    scratch = [pltpu.VMEM(utab.shape, jnp.uint32), pltpu.SemaphoreType.DMA(()),
               pltpu.VMEM((tq, hk * PAIR_ROWS), F32)]
    args = [off_flat, par, h2, g, fold]
    if st is not None:
        in_specs.append(pl.BlockSpec((2 * PEER_HEADS, PEER_NKEYS, TOPK_TL), lambda i: (0, 0, st_blk0 + i // 2)))
        out_specs += [pl.BlockSpec((TOPK_TL, hk), lambda i: (i // 2, 0))] * 3
        out_shape += [jax.ShapeDtypeStruct((t, hk), I32), jax.ShapeDtypeStruct((t, hk), I32),
                      jax.ShapeDtypeStruct((t, hk), F32)]
        scratch += _topk_scratch()
        args.append(st)
    in_specs.append(pl.BlockSpec(memory_space=pl.ANY))
    args.append(utab)
    return pl.pallas_call(
        _peer_u_kernel if st is None else _peer_u_topk_kernel,
        grid=(t // tq,),
        in_specs=in_specs,
        out_specs=out_specs,
        out_shape=out_shape,
        scratch_shapes=scratch,
        compiler_params=pltpu.CompilerParams(dimension_semantics=("arbitrary",),
                                             vmem_limit_bytes=VMEM_LIMIT),
        name="peer_u" if st is None else "peer_u_topk",
    )(*args)


def _peer_v_kernel(off_ref, par_ref, w_ref, x1_ref, mod_ref, fg_ref, tab_hbm, o_ref, tab, sem, wl_sc, p_sc, *, final):
    tq, hk = w_ref.shape
    d = x1_ref.shape[1]
    nj = d // LANES
    _load_table_once(tab_hbm, tab, sem)
    width = hk * PAIR_ROWS
    rep = _group_matrix(hk, width)
    w_hi, w_lo = _split2(w_ref[...])
    parl = jnp.dot(par_ref[...].astype(BF16), rep, preferred_element_type=F32)
    lane = lax.broadcasted_iota(I32, (tq, width), 1)
    wanted = (lane % 2).astype(F32) == parl
    wl_sc[0] = jnp.where(wanted, jnp.dot(w_hi, rep, preferred_element_type=F32), 0.0)
    wl_sc[1] = jnp.where(wanted, jnp.dot(w_lo, rep, preferred_element_type=F32), 0.0)
    mine = _own_sublane_mask(hk)
    for tt in range(tq):
        lhs = jnp.concatenate([jnp.where(mine, wl_sc[0, tt:tt + 1, :], 0.0),
                               jnp.where(mine, wl_sc[1, tt:tt + 1, :], 0.0)], axis=0).astype(BF16)
        r = jnp.dot(lhs, _pair_rows(tab, off_ref, tt, hk), preferred_element_type=F32)
        p_sc[tt * SUB:(tt + 1) * SUB, :] = r[0:SUB] + r[SUB:2 * SUB]
    peer = jnp.concatenate([p_sc[pl.ds(j, tq, stride=nj), :] for j in range(nj)], axis=1)
    y = x1_ref[...] + mod_ref[0, :, 5 * d:6 * d] * peer
    if final:
        y = y * lax.rsqrt(jnp.mean(y * y, axis=-1, keepdims=True) + EPS) * fg_ref[...]
    o_ref[...] = y


def _peer_v(off_flat, par, w, x1, mod3, final_g, vtab, *, seq, final, tq=64):
    t, d = x1.shape
    hk = w.shape[1]
    return pl.pallas_call(
        functools.partial(_peer_v_kernel, final=final),
        grid=(t // tq,),
        in_specs=[
            pl.BlockSpec((tq * hk,), lambda i: (i,), memory_space=pltpu.SMEM),
            pl.BlockSpec((tq, hk), lambda i: (i, 0)),
            pl.BlockSpec((tq, hk), lambda i: (i, 0)),
            pl.BlockSpec((tq, d), lambda i: (i, 0)),
            pl.BlockSpec((1, 1, mod3.shape[2]), lambda i: ((i * tq) // seq, 0, 0)),
            pl.BlockSpec((1, d), lambda i: (0, 0)),
            pl.BlockSpec(memory_space=pl.ANY),
        ],
        out_specs=pl.BlockSpec((tq, d), lambda i: (i, 0)),
        out_shape=jax.ShapeDtypeStruct((t, d), F32),
        scratch_shapes=[pltpu.VMEM(vtab.shape, jnp.uint32), pltpu.SemaphoreType.DMA(()),
                        pltpu.VMEM((2, tq, hk * PAIR_ROWS), F32), pltpu.VMEM((tq * (d // LANES), LANES), F32)],
        compiler_params=pltpu.CompilerParams(dimension_semantics=("arbitrary",),
                                             vmem_limit_bytes=VMEM_LIMIT),
        name="peer_v",
    )(off_flat, par, w, x1, mod3, final_g, vtab)


def kernel(x, c, norm1_g, norm2_g, w_ada, b_ada, w_in, w_out, lam_q1, lam_k1, lam_q2, lam_k2, subln_g, sinks,
           peer_wq, peer_keys, peer_u, peer_v, final_g):
    bsz, seq, d = x.shape
    depth = w_in.shape[0]
    t = bsz * seq
    slopes = _alibi_slopes()
    sl_c = [float(s) for s in slopes[:C_HEADS]]
    qa_bias = _slope_bias_col(slopes[C_HEADS:C_HEADS + A_HEADS] * np.float32(LOG2E))
    qb_bias = _slope_bias_col(slopes[C_HEADS + A_HEADS:] * np.float32(LOG2E))
    mods = _adaln_mods(c, w_ada, b_ada)
    x2d = x.reshape(t, d)
    av, bw = A_HEADS * HEAD_DIM, B_HEADS * HEAD_DIM
    for l in range(depth):
        lam_init = 0.8 - 0.6 * math.exp(-0.3 * l)
        mod3 = mods[l].reshape(bsz, 1, N_MOD * d)
        wn, wt = _prep_in_weights(w_in[l])
        ka, kb, qc, vc2, qta, vta, qtb, vtb, ktc = _inproj(
            x2d, mod3, norm1_g[l].reshape(1, d), wn, wt, qa_bias, qb_bias, seq=seq)
        lamv = jnp.stack([lam_q1[l], lam_k1[l], lam_q2[l], lam_k2[l]]).astype(F32)
        sg = subln_g[l].reshape(HEAD_DIM, 1).astype(F32)
        oa = _diff_attention(qta, ka, vta, lamv, sg, bsz=bsz, seq=seq, lam_init=lam_init)
        ob = _moba_attention(qtb, kb, vtb, bsz=bsz, seq=seq)
        oc = _swa_attention(sinks[l].astype(F32), qc, ktc, vc2, bsz=bsz, seq=seq, slopes=sl_c)
        wo = w_out[l].astype(BF16)
        keys = peer_keys[l].reshape(2 * PEER_HEADS, PEER_NKEYS, -1).astype(BF16)
        x1, h2, st = _outproj(oa, ob, oc, wo[:av], wo[av:av + bw], wo[av + bw:], x2d, mod3,
                              norm2_g[l].reshape(1, d), peer_wq[l].astype(BF16), keys, seq=seq)
        assert seq % TOPK_TL == 0
        blocks = seq // TOPK_TL
        utab = _pack_expert_table(peer_u[l])
        h2t = h2.reshape(t, d // LANES, LANES)
        sel = [_peer_topk(st, 0, blocks)]
        ws = []
        for b in range(bsz):
            off_b, par_b, g_b = sel[b]
            if b + 1 < bsz:
                w_b, *nxt = _peer_u(off_b.reshape(-1), par_b, g_b, h2t, utab, b * seq, st, (b + 1) * blocks)
                sel.append(nxt)
            else:
                (w_b,) = _peer_u(off_b.reshape(-1), par_b, g_b, h2t, utab, b * seq)
            ws.append(w_b)
        toff = jnp.concatenate([s_[0] for s_ in sel]).reshape(-1)
        par = jnp.concatenate([s_[1] for s_ in sel])
        w = jnp.concatenate(ws)
        x2d = _peer_v(toff, par, w, x1, mod3, final_g.reshape(1, d), _pack_expert_table(peer_v[l]),
                      seq=seq, final=(l == depth - 1))
    return x2d.reshape(bsz, seq, d)
```

```python
import functools
import math

import numpy as np
import jax
import jax.numpy as jnp
from jax import lax
from jax.experimental import pallas as pl
from jax.experimental.pallas import tpu as pltpu

F32 = jnp.float32
BF16 = jnp.bfloat16
I32 = jnp.int32

D_MODEL = 1024
HEAD_DIM = 64
N_HEADS_TOTAL = 16
A_HEADS = 4
B_HEADS = 6
C_HEADS = 6
C_KV_HEADS = 2
C_GROUP = 3
A_QK_DIM = 32
MOBA_BLOCK = 256
MOBA_TOPK = 3
WINDOW = 128
ALIBI_MAX = 8.0
PEER_HEADS = 8
PEER_NKEYS = 128
PEER_TOPK = 16
N_MOD = 6
EPS = 1e-6

LANES = 128
KT = 256
AUG0 = HEAD_DIM
SEL0 = AUG0 + 6
MAX_BLOCKS = LANES - SEL0
NEG_BIG = -1e30
LOG2E = math.log2(math.e)
VMEM_LIMIT = 56 * 1024 * 1024


def _alibi_slopes():
    n = N_HEADS_TOTAL
    return (2.0 ** (-ALIBI_MAX * np.arange(1, n + 1, dtype=np.float32) / n)).astype(np.float32)


def _split3(v):
    v = np.float32(v)
    hi = np.float32(np.asarray(v).astype(jnp.bfloat16).astype(np.float32))
    r = np.float32(v - hi)
    mid = np.float32(np.asarray(r).astype(jnp.bfloat16).astype(np.float32))
    lo = np.float32(np.float32(r - mid))
    lo = np.float32(np.asarray(lo).astype(jnp.bfloat16).astype(np.float32))
    return hi, mid, lo


def _slope_bias_col(slopes):
    col = np.zeros((LANES * len(slopes), 1), np.float32)
    for h, s in enumerate(slopes):
        hi, mid, lo = _split3(s)
        col[h * LANES + AUG0:h * LANES + AUG0 + 6, 0] = [hi, mid, lo, hi, mid, lo]
    return jnp.asarray(col)


def _mod_kernel(c_ref, w_ref, b_ref, o_ref):
    c = c_ref[...]
    cs = c * (1.0 / (1.0 + jnp.exp(-c)))
    o_ref[0] = jnp.dot(cs, w_ref[0], preferred_element_type=F32) + b_ref[0]


def _adaln_mods(c, w_ada, b_ada):
    depth, d, n = w_ada.shape
    bsz = c.shape[0]
    rows = -(-bsz // 8) * 8
    cp = jnp.pad(c, ((0, rows - bsz), (0, 0)))
    tn = 1536
    out = pl.pallas_call(
        _mod_kernel,
        grid=(depth, n // tn),
        in_specs=[
            pl.BlockSpec((rows, d), lambda l, j: (0, 0)),
            pl.BlockSpec((1, d, tn), lambda l, j: (l, 0, j)),
            pl.BlockSpec((1, 1, tn), lambda l, j: (l, 0, j)),
        ],
        out_specs=pl.BlockSpec((1, rows, tn), lambda l, j: (l, 0, j)),
        out_shape=jax.ShapeDtypeStruct((depth, rows, n), F32),
        compiler_params=pltpu.CompilerParams(dimension_semantics=("arbitrary", "arbitrary"),
                                             vmem_limit_bytes=VMEM_LIMIT),
        name="adaln_mods",
    )(cp, w_ada, b_ada.reshape(depth, 1, n))
    return out[:, :bsz]


NN_WIDTHS = (A_HEADS * LANES, B_HEADS * LANES, C_HEADS * HEAD_DIM, 2 * C_KV_HEADS * HEAD_DIM)
NT_ROWS = (A_HEADS * LANES, A_HEADS * HEAD_DIM, B_HEADS * LANES, B_HEADS * HEAD_DIM,
           C_KV_HEADS * LANES)


def _prep_in_weights(w):
    d = w.shape[0]
    aq, ak, av = A_HEADS * 2 * A_QK_DIM, A_HEADS * 2 * A_QK_DIM, A_HEADS * HEAD_DIM
    bw = B_HEADS * HEAD_DIM
    cq, ckv = C_HEADS * HEAD_DIM, C_KV_HEADS * HEAD_DIM
    cuts = np.cumsum([aq, ak, av, bw, bw, bw, cq, ckv]).tolist()
    qa, ka, va, qb, kb, vb, qc, kc, vc = jnp.split(w, cuts, axis=-1)

    def pad_heads(m, nh, scale):
        m = (m * scale).reshape(d, nh, HEAD_DIM)
        return jnp.pad(m, ((0, 0), (0, 0), (0, LANES - HEAD_DIM))).reshape(d, nh * LANES)

    vc2 = vc.reshape(d, C_KV_HEADS, 1, HEAD_DIM)
    vc2 = jnp.broadcast_to(vc2, (d, C_KV_HEADS, 2, HEAD_DIM)).reshape(d, 2 * ckv)
    kc2 = jnp.broadcast_to(kc.reshape(d, C_KV_HEADS, 1, HEAD_DIM), (d, C_KV_HEADS, 2, HEAD_DIM)).reshape(d, 2 * ckv)
    wn = jnp.concatenate([pad_heads(ka, A_HEADS, 1.0), pad_heads(kb, B_HEADS, 1.0),
                          qc * (HEAD_DIM ** -0.5), vc2], axis=1)
    wt = jnp.concatenate([pad_heads(qa, A_HEADS, A_QK_DIM ** -0.5 * LOG2E), va,
                          pad_heads(qb, B_HEADS, HEAD_DIM ** -0.5 * LOG2E), vb, kc2], axis=1).T
    return wn.astype(BF16), wt.astype(BF16)


def _rms_mod(x, g, sc, sh):
    ms = jnp.mean(x * x, axis=-1, keepdims=True)
    return (x * lax.rsqrt(ms + EPS) * g) * (1.0 + sc) + sh


def _inproj_kernel(x_ref, mod_ref, g_ref, wn_ref, wt_ref, qab_ref, qbb_ref,
                   ka_ref, kb_ref, qc_ref, vc_ref, qta_ref, vta_ref, qtb_ref, vtb_ref, ktc_ref,
                   *, tm, seq):
    d = x_ref.shape[1]
    x = x_ref[...]
    sh = mod_ref[0, :, 0:d]
    sc = mod_ref[0, :, d:2 * d]
    h = _rms_mod(x, g_ref[...], sc, sh).astype(BF16)
    pn = jnp.dot(h, wn_ref[...], preferred_element_type=F32)
    pt = lax.dot_general(wt_ref[...], h, (((1,), (1,)), ((), ())),
                         preferred_element_type=F32)

    pos = (pl.program_id(0) * tm) % seq + lax.broadcasted_iota(I32, (tm, LANES), 0)
    col = lax.broadcasted_iota(I32, (tm, LANES), 1)
    blk_id = pos // KT
    p_hi = (blk_id * KT).astype(F32)
    p_lo = (pos - blk_id * KT).astype(F32)
    aug_a = jnp.where((col >= AUG0) & (col < AUG0 + 3), p_hi,
                      jnp.where((col >= AUG0 + 3) & (col < AUG0 + 6), p_lo, 0.0))
    aug_b = jnp.where((col >= SEL0) & (col - SEL0 == blk_id), 1.0, aug_a)
    for hh in range(A_HEADS):
        ka_ref[:, hh * LANES:(hh + 1) * LANES] = (pn[:, hh * LANES:(hh + 1) * LANES] + aug_a).astype(BF16)
    o = NN_WIDTHS[0]
    for hh in range(B_HEADS):
        kb_ref[:, hh * LANES:(hh + 1) * LANES] = (pn[:, o + hh * LANES:o + (hh + 1) * LANES] + aug_b).astype(BF16)
    o += NN_WIDTHS[1]
    qc_ref[...] = pn[:, o:o + NN_WIDTHS[2]].astype(BF16)
    o += NN_WIDTHS[2]
    vc_ref[...] = pn[:, o:o + NN_WIDTHS[3]].astype(BF16)

    r0 = 0
    for ref, b, nr in zip((qta_ref, vta_ref, qtb_ref, vtb_ref, ktc_ref),
                          (qab_ref, None, qbb_ref, None, None), NT_ROWS):
        blk = pt[r0:r0 + nr, :]
        if b is not None:
            blk = blk + b[...]
        blk = blk.astype(BF16)
        for cc in range(tm // KT):
            ref[cc] = blk[:, cc * KT:(cc + 1) * KT]
        r0 += nr


def _inproj(x2d, mod3, g, wn, wt, qa_bias, qb_bias, *, seq, tm=512):
    t, d = x2d.shape
    assert seq % tm == 0 and tm % KT == 0
    nt = t // tm
    nn_total = sum(NN_WIDTHS)
    row_specs = [pl.BlockSpec((tm, wd), lambda i: (i, 0)) for wd in NN_WIDTHS]
    kt_specs = [pl.BlockSpec((tm // KT, r, KT), lambda i: (i, 0, 0)) for r in NT_ROWS]
    out_shape = ([jax.ShapeDtypeStruct((t, wd), BF16) for wd in NN_WIDTHS]
                 + [jax.ShapeDtypeStruct((t // KT, r, KT), BF16) for r in NT_ROWS])
    return pl.pallas_call(
        functools.partial(_inproj_kernel, tm=tm, seq=seq),
        grid=(nt,),
        in_specs=[
            pl.BlockSpec((tm, d), lambda i: (i, 0)),
            pl.BlockSpec((1, 1, mod3.shape[2]), lambda i: ((i * tm) // seq, 0, 0)),
            pl.BlockSpec((1, d), lambda i: (0, 0)),
            pl.BlockSpec((d, nn_total), lambda i: (0, 0)),
            pl.BlockSpec((sum(NT_ROWS), d), lambda i: (0, 0)),
            pl.BlockSpec((NT_ROWS[0], 1), lambda i: (0, 0)),
            pl.BlockSpec((NT_ROWS[2], 1), lambda i: (0, 0)),
        ],
        out_specs=row_specs + kt_specs,
        out_shape=out_shape,
        compiler_params=pltpu.CompilerParams(dimension_semantics=("arbitrary",),
                                             vmem_limit_bytes=VMEM_LIMIT),
        name="inproj",
    )(x2d, mod3, g, wn, wt, qa_bias, qb_bias)


ACC_ROWS = HEAD_DIM + 16


def _softmax_pv(ss, vt1s, m_ref, acc_ref):
    nh = len(ss)
    m_prev = [m_ref[hh] for hh in range(nh)]
    m_new = [jnp.maximum(m_prev[hh], jnp.max(ss[hh], axis=0, keepdims=True)) for hh in range(nh)]
    ps = [jnp.exp2((ss[hh] - m_new[hh]).astype(BF16)) for hh in range(nh)]
    pv = [jnp.dot(vt1s[hh], ps[hh], preferred_element_type=F32) for hh in range(nh)]
    for hh in range(nh):
        acc_ref[hh] = jnp.exp2(m_prev[hh] - m_new[hh]) * acc_ref[hh] + pv[hh]
        m_ref[hh] = m_new[hh]


def _flash_init(m_ref, acc_ref):
    m_ref[...] = jnp.full(m_ref.shape, -jnp.inf, F32)
    acc_ref[...] = jnp.zeros(acc_ref.shape, F32)


def _flash_finish(acc_ref, hh):
    acc = acc_ref[hh]
    return acc[0:HEAD_DIM] / acc[HEAD_DIM:HEAD_DIM + 1]


PAST_TILES = 4
SERIAL_TILES = 2


def _flash_causal_pair(k_ref, vt_ref, q_diag, q_past, qi, causal, m_ref, acc_ref, s_ref=None):
    _flash_init(m_ref, acc_ref)
    nkeys = PAST_TILES * KT
    nh = len(q_past)

    def vt1(kj, n, hh):
        rows = slice(hh * HEAD_DIM, (hh + 1) * HEAD_DIM)
        vt = vt_ref[kj, rows, :] if n == 1 else jnp.concatenate([vt_ref[kj + c, rows, :] for c in range(n)], axis=1)
        return jnp.concatenate([vt, jnp.ones((ACC_ROWS - HEAD_DIM, n * KT), BF16)], axis=0)

    def scores(kj, n, qs, hh):
        kk = k_ref[pl.ds(pl.multiple_of(kj * KT, KT), n * KT), hh * LANES:(hh + 1) * LANES]
        return jnp.dot(kk, qs[hh], preferred_element_type=F32)

    def step(kj, n, qs, mask=None):
        ss = [scores(kj, n, qs, hh) for hh in range(nh)]
        if mask is not None:
            ss = [jnp.where(mask, s, -jnp.inf) for s in ss]
        _softmax_pv(ss, [vt1(kj, n, hh) for hh in range(nh)], m_ref, acc_ref)

    step(qi, 1, q_diag, causal)

    if s_ref is None:
        def body(j, carry):
            step(j * SERIAL_TILES, SERIAL_TILES, q_past)
            return carry

        lax.fori_loop(0, qi // SERIAL_TILES, body, 0)
        for r in range(SERIAL_TILES - 1):
            @pl.when(qi % SERIAL_TILES > r)
            def _():
                step(qi - 1 - r, 1, q_past)
        return

    ngroups = k_ref.shape[0] // nkeys

    def scores_into(slot, g):
        for hh in range(nh):
            s_ref[slot, hh] = scores(g * PAST_TILES, PAST_TILES, q_past, hh)

    scores_into(0, 0)

    def body(g, carry):
        scores_into((g + 1) % 2, jnp.minimum(g + 1, ngroups - 1))
        _softmax_pv([s_ref[g % 2, hh] for hh in range(nh)],
                    [vt1(g * PAST_TILES, PAST_TILES, hh) for hh in range(nh)], m_ref, acc_ref)
        return carry

    lax.fori_loop(0, (qi + PAST_TILES - 1) // PAST_TILES, body, 0)


def _diff_kernel(qt_ref, k_ref, vt_ref, lamv_ref, sg_ref, o_ref, m_ref, acc_ref, *, lam_init):
    tq = qt_ref.shape[2]
    qi = pl.program_id(2)
    lv = lamv_ref[...]
    lam = (jnp.exp(jnp.sum(lv[0:1] * lv[1:2], axis=-1, keepdims=True))
           - jnp.exp(jnp.sum(lv[2:3] * lv[3:4], axis=-1, keepdims=True)) + lam_init)
    row = lax.broadcasted_iota(I32, (LANES, tq), 0)
    kr = lax.broadcasted_iota(I32, (KT, 2 * tq), 0)
    qc = lax.broadcasted_iota(I32, (KT, 2 * tq), 1)
    causal = kr <= jnp.where(qc >= tq, qc - tq, qc)
    nh = qt_ref.shape[1] // LANES
    qs = []
    for hh in range(nh):
        qt = qt_ref[0, hh * LANES:(hh + 1) * LANES, :]
        zero = jnp.zeros_like(qt)
        q1 = jnp.where((row < A_QK_DIM) | (row >= AUG0), qt, zero)
        q2 = jnp.where(row >= A_QK_DIM, qt, zero)
        qs.append(jnp.concatenate([q1, q2], axis=1))
    _flash_causal_pair(k_ref, vt_ref, qs, qs, qi, causal, m_ref, acc_ref)
    res = []
    for hh in range(nh):
        o = _flash_finish(acc_ref, hh)
        od = o[:, :tq] - lam * o[:, tq:]
        ms = jnp.mean(od * od, axis=0, keepdims=True)
        res.append(od * lax.rsqrt(ms + EPS) * sg_ref[...] * (1.0 - lam_init))
    o_ref[...] = jnp.concatenate(res, axis=0).T.astype(BF16)


def _diff_attention(qta, ka, vta, lamv, sg, *, bsz, seq, lam_init, nh=A_HEADS):
    tq = KT
    nq = seq // tq
    t = bsz * seq
    return pl.pallas_call(
        functools.partial(_diff_kernel, lam_init=lam_init),
        grid=(bsz, A_HEADS // nh, nq),
        in_specs=[
            pl.BlockSpec((1, nh * LANES, tq), lambda b, hp, qi: (b * nq + qi, hp, 0)),
            pl.BlockSpec((seq, nh * LANES), lambda b, hp, qi: (b, hp)),
            pl.BlockSpec((nq, nh * HEAD_DIM, KT), lambda b, hp, qi: (b, hp, 0)),
            pl.BlockSpec((4, A_QK_DIM), lambda b, hp, qi: (0, 0)),
            pl.BlockSpec((HEAD_DIM, 1), lambda b, hp, qi: (0, 0)),
        ],
        out_specs=pl.BlockSpec((tq, nh * HEAD_DIM), lambda b, hp, qi: (b * nq + qi, hp)),
        out_shape=jax.ShapeDtypeStruct((t, A_HEADS * HEAD_DIM), BF16),
        scratch_shapes=[pltpu.VMEM((nh, 1, 2 * tq), F32), pltpu.VMEM((nh, ACC_ROWS, 2 * tq), F32)],
        compiler_params=pltpu.CompilerParams(dimension_semantics=("arbitrary",) * 3,
                                             vmem_limit_bytes=VMEM_LIMIT),
        name="diff_attn",
    )(qta, ka, vta, lamv, sg)


def _moba_kernel(qt_ref, k_ref, vt_ref, o_ref, m_ref, acc_ref, km_ref, s_ref):
    tq = qt_ref.shape[2]
    nh = qt_ref.shape[1] // LANES
    nb = vt_ref.shape[0]
    qi = pl.program_id(2)

    @pl.when(qi == 0)
    def _():
        lane1 = lax.broadcasted_iota(I32, (1, LANES), 1)
        for hh in range(nh):
            km_ref[hh] = jnp.zeros((LANES, LANES), F32)

            def put_block(j, carry):
                blk = k_ref[pl.ds(pl.multiple_of(j * KT, KT), KT), hh * LANES:(hh + 1) * LANES].astype(F32)
                mean = jnp.sum(blk, axis=0, keepdims=True) * (1.0 / KT)
                km_ref[hh, pl.ds(SEL0 + j, 1), :] = jnp.where(lane1 < HEAD_DIM, mean, 0.0)
                return carry
            lax.fori_loop(0, nb, put_block, 0)

    kr = lax.broadcasted_iota(I32, (KT, tq), 0)
    qc = lax.broadcasted_iota(I32, (KT, tq), 1)
    causal = kr <= qc
    nsel = -(-(SEL0 - AUG0 + nb) // 16) * 16
    band = slice(AUG0, AUG0 + nsel)
    row = lax.broadcasted_iota(I32, (nsel, tq), 0) + AUG0
    in_sel = row >= SEL0
    qts = [qt_ref[0, hh * LANES:(hh + 1) * LANES, :] for hh in range(nh)]
    curs = []
    for hh in range(nh):
        km_hi, km_lo = _split2(km_ref[hh, band, :])
        gate = (jnp.dot(km_hi, qts[hh], preferred_element_type=F32)
                + jnp.dot(km_lo, qts[hh], preferred_element_type=F32))
        curs.append(jnp.where(in_sel & (row < SEL0 + qi), gate, -jnp.inf))
    sels = [jnp.zeros((nsel, tq), jnp.bool_)] * nh
    for _ in range(MOBA_TOPK):
        for hh in range(nh):
            mx = jnp.max(curs[hh], axis=0, keepdims=True)
            first = jnp.min(jnp.where(curs[hh] == mx, row, 4 * LANES), axis=0, keepdims=True)
            pick = (row == first) & (mx > -jnp.inf)
            sels[hh] = sels[hh] | pick
            curs[hh] = jnp.where(pick, -jnp.inf, curs[hh])
    q_diag, q_past = [], []
    for hh in range(nh):
        qt, mid = qts[hh], qts[hh][band]
        rest = [qt[AUG0 + nsel:]] if AUG0 + nsel < LANES else []
        q_diag.append(jnp.concatenate([qt[:AUG0], jnp.where(in_sel, jnp.zeros_like(mid), mid)] + rest, axis=0))
        bias = jnp.where(sels[hh], 0.0, NEG_BIG).astype(BF16)
        q_past.append(jnp.concatenate([qt[:AUG0], jnp.where(in_sel, bias, mid)] + rest, axis=0))
    _flash_causal_pair(k_ref, vt_ref, q_diag, q_past, qi, causal, m_ref, acc_ref, s_ref)
    res = [_flash_finish(acc_ref, hh) for hh in range(nh)]
    o_ref[...] = jnp.concatenate(res, axis=0).T.astype(BF16)


def _moba_attention(qtb, kb, vtb, *, bsz, seq, nh=B_HEADS):
    tq = KT
    nq = seq // tq
    assert nq <= MAX_BLOCKS and seq % (PAST_TILES * KT) == 0
    t = bsz * seq
    return pl.pallas_call(
        _moba_kernel,
        grid=(bsz, B_HEADS // nh, nq),
        in_specs=[
            pl.BlockSpec((1, nh * LANES, tq), lambda b, hp, qi: (b * nq + qi, hp, 0)),
            pl.BlockSpec((seq, nh * LANES), lambda b, hp, qi: (b, hp), pipeline_mode=pl.Buffered(1)),
            pl.BlockSpec((nq, nh * HEAD_DIM, KT), lambda b, hp, qi: (b, hp, 0), pipeline_mode=pl.Buffered(1)),
        ],
        out_specs=pl.BlockSpec((tq, nh * HEAD_DIM), lambda b, hp, qi: (b * nq + qi, hp)),
        out_shape=jax.ShapeDtypeStruct((t, B_HEADS * HEAD_DIM), BF16),
        scratch_shapes=[pltpu.VMEM((nh, 1, tq), F32), pltpu.VMEM((nh, ACC_ROWS, tq), F32),
                        pltpu.VMEM((nh, LANES, LANES), F32), pltpu.VMEM((2, nh, PAST_TILES * KT, tq), F32)],
        compiler_params=pltpu.CompilerParams(dimension_semantics=("arbitrary",) * 3,
                                             vmem_limit_bytes=VMEM_LIMIT),
        name="moba_attn",
    )(qtb, kb, vtb)


def _swa_kernel(sink_ref, q_ref, ktp_ref, ktc_ref, vp_ref, vc_ref, o_ref, *, slopes):
    w = WINDOW
    n = pl.program_id(1)
    lane = lax.broadcasted_iota(I32, (w, LANES), 1)
    r2 = lax.broadcasted_iota(I32, (w, 2 * w), 0)
    c2 = lax.broadcasted_iota(I32, (w, 2 * w), 1)
    rel = r2 + w - c2
    band = (rel >= 0) & (rel < w)
    relf = rel.astype(F32)
    for half in range(KT // w):
        mask = band & ((c2 >= w) | (n > 0)) if half == 0 else band
        rows = slice(half * w, (half + 1) * w)
        res = []
        for hq in range(C_HEADS):
            kv = hq // C_GROUP
            kvl = slice(kv * LANES, (kv + 1) * LANES)
            qp = q_ref[rows, (hq // 2) * LANES:(hq // 2 + 1) * LANES]
            qm = jnp.where((lane < HEAD_DIM) == (hq % 2 == 0), qp, jnp.zeros_like(qp))
            if half == 0:
                kt = jnp.concatenate([ktp_ref[0, kvl, KT - w:KT], ktc_ref[0, kvl, 0:w]], axis=1)
                vv = jnp.concatenate([vp_ref[KT - w:KT, kvl], vc_ref[0:w, kvl]], axis=0)
            else:
                kt = ktc_ref[0, kvl, (half - 1) * w:(half + 1) * w]
                vv = vc_ref[(half - 1) * w:(half + 1) * w, kvl]
            s = jnp.dot(qm, kt, preferred_element_type=F32)
            s = jnp.where(mask, s - slopes[hq] * relf, -jnp.inf)
            sink = sink_ref[hq]
            m = jnp.maximum(jnp.max(s, axis=-1, keepdims=True), sink)
            e = jnp.exp(s - m)
            den = jnp.sum(e, axis=-1, keepdims=True) + jnp.exp(sink - m)
            p = (e / den).astype(BF16)
            res.append(jnp.dot(p, vv, preferred_element_type=F32))
        for pr in range(C_HEADS // 2):
            o_ref[rows, pr * LANES:(pr + 1) * LANES] = jnp.where(lane < HEAD_DIM, res[2 * pr],
                                                                 res[2 * pr + 1]).astype(BF16)


def _swa_attention(sinks, qc, ktc, vc2, *, bsz, seq, slopes):
    tq = KT
    nq = seq // tq
    t = bsz * seq
    return pl.pallas_call(
        functools.partial(_swa_kernel, slopes=slopes),
        grid=(bsz, nq),
        in_specs=[
            pl.BlockSpec(memory_space=pltpu.SMEM),
            pl.BlockSpec((tq, C_HEADS * HEAD_DIM), lambda b, n: (b * nq + n, 0)),
            pl.BlockSpec((1, C_KV_HEADS * LANES, KT), lambda b, n: (b * nq + jnp.maximum(n - 1, 0), 0, 0)),
            pl.BlockSpec((1, C_KV_HEADS * LANES, KT), lambda b, n: (b * nq + n, 0, 0)),
            pl.BlockSpec((tq, C_KV_HEADS * LANES), lambda b, n: (b * nq + jnp.maximum(n - 1, 0), 0)),
            pl.BlockSpec((tq, C_KV_HEADS * LANES), lambda b, n: (b * nq + n, 0)),
        ],
        out_specs=pl.BlockSpec((tq, C_HEADS * HEAD_DIM), lambda b, n: (b * nq + n, 0)),
        out_shape=jax.ShapeDtypeStruct((t, C_HEADS * HEAD_DIM), BF16),
        compiler_params=pltpu.CompilerParams(dimension_semantics=("arbitrary",) * 2,
                                             vmem_limit_bytes=VMEM_LIMIT),
        name="swa_attn",
    )(sinks, qc, ktc, ktc, vc2, vc2)


def _outproj_kernel(oa_ref, ob_ref, oc_ref, woa_ref, wob_ref, woc_ref, x_ref, mod_ref, g_ref, wq_ref, keys_ref,
                    x1_ref, h2_ref, st_ref):
    d = x_ref.shape[1]
    mix = (jnp.dot(oa_ref[...], woa_ref[...], preferred_element_type=F32)
           + jnp.dot(ob_ref[...], wob_ref[...], preferred_element_type=F32)
           + jnp.dot(oc_ref[...], woc_ref[...], preferred_element_type=F32))
    g1 = mod_ref[0, :, 2 * d:3 * d]
    sh2 = mod_ref[0, :, 3 * d:4 * d]
    sc2 = mod_ref[0, :, 4 * d:5 * d]
    x1 = x_ref[...] + g1 * mix
    x1_ref[...] = x1
    h2 = _rms_mod(x1, g_ref[...], sc2, sh2)
    nj = d // LANES
    for j in range(nj):
        h2_ref[pl.ds(j, x1.shape[0], stride=nj), :] = h2[:, j * LANES:(j + 1) * LANES]
    pq = jnp.dot(h2.astype(BF16), wq_ref[...], preferred_element_type=F32).astype(BF16)
    for hp in range(2 * PEER_HEADS):
        st_ref[hp] = lax.dot_general(keys_ref[hp], pq[:, hp * LANES:(hp + 1) * LANES],
                                     (((1,), (1,)), ((), ())), preferred_element_type=F32)


def _outproj(oa, ob, oc, woa, wob, woc, x2d, mod3, g, wq, keys, *, seq, tm=256):
    t, d = x2d.shape
    nt = t // tm
    nq = wq.shape[1]
    full = lambda a: pl.BlockSpec(a.shape, lambda i: (0,) * a.ndim)
    return pl.pallas_call(
        _outproj_kernel,
        grid=(nt,),
        in_specs=[
            pl.BlockSpec((tm, oa.shape[1]), lambda i: (i, 0)),
            pl.BlockSpec((tm, ob.shape[1]), lambda i: (i, 0)),
            pl.BlockSpec((tm, oc.shape[1]), lambda i: (i, 0)),
            full(woa), full(wob), full(woc),
            pl.BlockSpec((tm, d), lambda i: (i, 0)),
            pl.BlockSpec((1, 1, mod3.shape[2]), lambda i: ((i * tm) // seq, 0, 0)),
            pl.BlockSpec((1, d), lambda i: (0, 0)),
            full(wq), full(keys),
        ],
        out_specs=[pl.BlockSpec((tm, d), lambda i: (i, 0)),
                   pl.BlockSpec((tm * (d // LANES), LANES), lambda i: (i, 0)),
                   pl.BlockSpec((2 * PEER_HEADS, PEER_NKEYS, tm), lambda i: (0, 0, i))],
        out_shape=[jax.ShapeDtypeStruct((t, d), F32), jax.ShapeDtypeStruct((t * (d // LANES), LANES), F32),
                   jax.ShapeDtypeStruct((2 * PEER_HEADS, PEER_NKEYS, t), F32)],
        compiler_params=pltpu.CompilerParams(dimension_semantics=("arbitrary",),
                                             vmem_limit_bytes=VMEM_LIMIT),
        name="outproj_peerq",
    )(oa, ob, oc, woa, wob, woc, x2d, mod3, g, wq, keys)


_CAND_BLOCKS = ((0, 16),) + tuple((i, 8) for i in range(1, 8))
_CAND_ROWS = 16 + 7 * 8 + 8
_BIG_I = np.int32(2 ** 30)


TOPK_TL = LANES


def _topk_rows(curs, keys, val_refs, key_refs, unroll):
    def body(r, curs):
        out = []
        for cur, key, val_ref, key_ref in zip(curs, keys, val_refs, key_refs):
            m = jnp.max(cur, axis=0, keepdims=True)
            kmin = jnp.min(jnp.where(cur == m, key, _BIG_I), axis=0, keepdims=True)
            val_ref[pl.ds(r, 1), :] = m
            key_ref[pl.ds(r, 1), :] = kmin
            out.append(jnp.where(key == kmin, -jnp.inf, cur))
        return tuple(out)
    lax.fori_loop(0, PEER_TOPK, body, tuple(curs), unroll=unroll)


def _topk_scratch():
    pair = [pltpu.VMEM((2, PEER_TOPK, TOPK_TL), F32), pltpu.VMEM((2, PEER_TOPK, TOPK_TL), I32)]
    hk = PEER_HEADS * PEER_TOPK
    return pair * 3 + [pltpu.VMEM((hk, TOPK_TL), I32), pltpu.VMEM((hk, TOPK_TL), F32)]


def _product_candidates(sv1, si1, sv2, si2):
    tl = sv1.shape[1]
    jrow8 = lax.broadcasted_iota(I32, (8, tl), 0)
    jrow16 = lax.broadcasted_iota(I32, (16, tl), 0)
    cands, keys = [], []
    for i, nj in _CAND_BLOCKS:
        jrow = jrow16 if nj == 16 else jrow8
        cands.append(sv1[i:i + 1] + sv2[0:nj])
        keys.append((i * PEER_TOPK + jrow) * (PEER_NKEYS * PEER_NKEYS) + si1[i:i + 1] * PEER_NKEYS + si2[0:nj])
    cands.append(sv1[8:16] + sv2[0:1])
    keys.append((jrow8 + 8) * (PEER_TOPK * PEER_NKEYS * PEER_NKEYS) + si1[8:16] * PEER_NKEYS + si2[0:1])
    return jnp.concatenate(cands, axis=0), jnp.concatenate(keys, axis=0)


def _store_gates(h, top, keys, ei_ref, gg_ref):
    rows = pl.ds(pl.multiple_of(h * PEER_TOPK, PEER_TOPK), PEER_TOPK)
    e = jnp.exp(top - top[0:1])
    gg_ref[rows, :] = e / jnp.sum(e, axis=0, keepdims=True)
    ei_ref[rows, :] = keys & (PEER_NKEYS * PEER_NKEYS - 1)


def _topk_head_pair(st_ref, h0, scratch, unroll=False):
    v1_ref, k1_ref, v2_ref, k2_ref, vt_ref, kt_ref, ei_ref, gg_ref = scratch
    row = lax.broadcasted_iota(I32, (PEER_NKEYS, st_ref.shape[2]), 0)
    cand, ckey = [], []
    for s in range(2):
        h = h0 + s
        _topk_rows([st_ref[2 * h], st_ref[2 * h + 1]], [row, row],
                   [v1_ref.at[s], v2_ref.at[s]], [k1_ref.at[s], k2_ref.at[s]], unroll)
        c, k = _product_candidates(v1_ref[s], k1_ref[s], v2_ref[s], k2_ref[s])
        cand.append(c)
        ckey.append(k)
    _topk_rows(cand, ckey, [vt_ref.at[0], vt_ref.at[1]], [kt_ref.at[0], kt_ref.at[1]], unroll)
    for s in range(2):
        _store_gates(h0 + s, vt_ref[s], kt_ref[s], ei_ref, gg_ref)


def _topk_emit(scratch, off_ref, par_ref, g_ref):
    ei = scratch[6][...].T
    off_ref[...] = (ei & (PEER_HALF - 1)) * SUB
    par_ref[...] = ei // PEER_HALF
    g_ref[...] = scratch[7][...].T


def _peer_topk_kernel(st_ref, off_ref, par_ref, g_ref, *scratch):
    for h0 in range(0, PEER_HEADS, 2):
        _topk_head_pair(st_ref, h0, scratch)
    _topk_emit(scratch, off_ref, par_ref, g_ref)


def _peer_topk(st, blk0, nblk):
    hk = PEER_HEADS * PEER_TOPK
    t = nblk * TOPK_TL
    return pl.pallas_call(
        _peer_topk_kernel,
        grid=(nblk,),
        in_specs=[pl.BlockSpec((2 * PEER_HEADS, PEER_NKEYS, TOPK_TL), lambda i: (0, 0, blk0 + i))],
        out_specs=[pl.BlockSpec((TOPK_TL, hk), lambda i: (i, 0))] * 3,
        out_shape=[jax.ShapeDtypeStruct((t, hk), I32), jax.ShapeDtypeStruct((t, hk), I32),
                   jax.ShapeDtypeStruct((t, hk), F32)],
        scratch_shapes=_topk_scratch(),
        compiler_params=pltpu.CompilerParams(dimension_semantics=("arbitrary",),
                                             vmem_limit_bytes=VMEM_LIMIT),
        name="peer_topk",
    )(st)


SUB = 8


def _erf(x):
    return lax.erf(x)


HALF_HI = 0xFFFF0000


PEER_HALF = PEER_NKEYS * PEER_NKEYS // 2


def _pack_kernel(lo_ref, hi_ref, o_ref):
    pairs = lo_ref.shape[0]
    nj = lo_ref.shape[1] // LANES
    for j in range(nj):
        cols = slice(j * LANES, (j + 1) * LANES)
        lo = pltpu.bitcast(lo_ref[:, cols].astype(BF16).astype(F32), jnp.uint32)
        hi = pltpu.bitcast(hi_ref[:, cols].astype(BF16).astype(F32), jnp.uint32)
        o_ref[pl.ds(j, pairs, stride=nj), :] = (lo >> 16) | (hi & jnp.uint32(HALF_HI))


def _pack_expert_table(w, pairs=256):
    n, d = w.shape
    nj = d // LANES
    half_blocks = n // 2 // pairs
    return pl.pallas_call(
        _pack_kernel,
        grid=(half_blocks,),
        in_specs=[pl.BlockSpec((pairs, d), lambda i: (i, 0)),
                  pl.BlockSpec((pairs, d), lambda i: (half_blocks + i, 0))],
        out_specs=pl.BlockSpec((pairs * nj, LANES), lambda i: (i, 0)),
        out_shape=jax.ShapeDtypeStruct((n // 2 * nj, LANES), jnp.uint32),
        compiler_params=pltpu.CompilerParams(dimension_semantics=("arbitrary",),
                                             vmem_limit_bytes=VMEM_LIMIT),
        name="pack_table",
    )(w, w)


def _load_table_once(tab_hbm, tab, sem):
    @pl.when(pl.program_id(0) == 0)
    def _():
        cp = pltpu.make_async_copy(tab_hbm, tab, sem)
        cp.start()
        cp.wait()


def _pair_tile(tab, off):
    return tab[pl.ds(pl.multiple_of(off, SUB), SUB), :]


def _group_matrix(rows, cols):
    r = lax.broadcasted_iota(I32, (rows, cols), 0)
    c = lax.broadcasted_iota(I32, (rows, cols), 1)
    return (c // (cols // rows) == r).astype(BF16)


def _split2(x):
    hi = x.astype(BF16)
    return hi, (x - hi.astype(F32)).astype(BF16)


PAIR_ROWS = 2 * SUB


def _pair_rows(tab, off_ref, tt, hk):
    return jnp.concatenate([pltpu.bitcast(_pair_tile(tab, off_ref[tt * hk + k]), BF16) for k in range(hk)], axis=0)


def _own_sublane_mask(hk):
    srow = lax.broadcasted_iota(I32, (SUB, hk * PAIR_ROWS), 0)
    scol = lax.broadcasted_iota(I32, (SUB, hk * PAIR_ROWS), 1)
    return (scol % PAIR_ROWS) // 2 == srow


def _fold_matrix(hk):
    c = jnp.arange(hk * PAIR_ROWS)
    return jax.nn.one_hot((c % 2) * hk + c // PAIR_ROWS, 2 * hk, dtype=BF16)


def _peer_u_body(off_ref, par_ref, h_ref, g_ref, fold_ref, w_ref, tab, d_sc):
    tq, hk = g_ref.shape
    mine = _own_sublane_mask(hk)
    nt = (((1,), (1,)), ((), ()))
    for tt in range(tq):
        h_hi, h_lo = _split2(h_ref[tt])
        s = lax.dot_general(jnp.concatenate([h_hi, h_lo], axis=0), _pair_rows(tab, off_ref, tt, hk), nt,
                            preferred_element_type=F32)
        d = jnp.where(mine, s[0:SUB] + s[SUB:2 * SUB], 0.0)
        d_sc[tt:tt + 1, :] = jnp.sum(d, axis=0, keepdims=True)
    d_hi, d_lo = _split2(d_sc[...])
    a2 = jnp.dot(jnp.concatenate([d_hi, d_lo], axis=0), fold_ref[...], preferred_element_type=F32)
    a2 = a2[0:tq] + a2[tq:2 * tq]
    a = jnp.where(par_ref[...] == 0, a2[:, :hk], a2[:, hk:])
    w_ref[...] = g_ref[...] * (0.5 * a * (1.0 + _erf(a * (2.0 ** -0.5))))


U_TQ = 64


def _peer_u_kernel(off_ref, par_ref, h_ref, g_ref, fold_ref, tab_hbm, w_ref, tab, sem, d_sc):
    _load_table_once(tab_hbm, tab, sem)
    _peer_u_body(off_ref, par_ref, h_ref, g_ref, fold_ref, w_ref, tab, d_sc)


def _peer_u_topk_kernel(off_ref, par_ref, h_ref, g_ref, fold_ref, st_ref, tab_hbm,
                        w_ref, off2_ref, par2_ref, g2_ref, tab, sem, d_sc, *scratch):
    half = pl.program_id(0) % 2
    _load_table_once(tab_hbm, tab, sem)
    for q in range(2):
        _topk_head_pair(st_ref, (2 * half + q) * 2, scratch, unroll=True)
    _peer_u_body(off_ref, par_ref, h_ref, g_ref, fold_ref, w_ref, tab, d_sc)

    @pl.when(half == 1)
    def _():
        _topk_emit(scratch, off2_ref, par2_ref, g2_ref)


def _peer_u(off_flat, par, g, h2, utab, tok0, st=None, st_blk0=None):
    t, hk = g.shape
    nj = h2.shape[1]
    tq = U_TQ
    assert t % (2 * tq) == 0 and 2 * tq == TOPK_TL and tok0 % tq == 0
    fold = _fold_matrix(hk)
    in_specs = [
        pl.BlockSpec((tq * hk,), lambda i: (i,), memory_space=pltpu.SMEM),
        pl.BlockSpec((tq, hk), lambda i: (i, 0)),
        pl.BlockSpec((tq, nj, LANES), lambda i: (tok0 // tq + i, 0, 0)),
        pl.BlockSpec((tq, hk), lambda i: (i, 0)),
        pl.BlockSpec(fold.shape, lambda i: (0, 0)),
    ]
    out_specs = [pl.BlockSpec((tq, hk), lambda i: (i, 0))]
    out_shape = [jax.ShapeDtypeStruct((t, hk), F32)]
    scratch = [pltpu.VMEM(utab.shape, jnp.uint32), pltpu.SemaphoreType.DMA(()),
               pltpu.VMEM((tq, hk * PAIR_ROWS), F32)]
    args = [off_flat, par, h2, g, fold]
    if st is not None:
        in_specs.append(pl.BlockSpec((2 * PEER_HEADS, PEER_NKEYS, TOPK_TL), lambda i: (0, 0, st_blk0 + i // 2)))
        out_specs += [pl.BlockSpec((TOPK_TL, hk), lambda i: (i // 2, 0))] * 3
        out_shape += [jax.ShapeDtypeStruct((t, hk), I32), jax.ShapeDtypeStruct((t, hk), I32),
                      jax.ShapeDtypeStruct((t, hk), F32)]
        scratch += _topk_scratch()
        args.append(st)
    in_specs.append(pl.BlockSpec(memory_space=pl.ANY))
    args.append(utab)
    return pl.pallas_call(
        _peer_u_kernel if st is None else _peer_u_topk_kernel,
        grid=(t // tq,),
        in_specs=in_specs,
        out_specs=out_specs,
        out_shape=out_shape,
        scratch_shapes=scratch,
        compiler_params=pltpu.CompilerParams(dimension_semantics=("arbitrary",),
                                             vmem_limit_bytes=VMEM_LIMIT),
        name="peer_u" if st is None else "peer_u_topk",
    )(*args)


def _peer_v_body(off_ref, par_ref, w_ref, x1_ref, mod_ref, fg_ref, o_ref, tab, wl_sc, p_sc, final):
    tq, hk = w_ref.shape
    d = x1_ref.shape[1]
    nj = d // LANES
    width = hk * PAIR_ROWS
    rep = _group_matrix(hk, width)
    w_hi, w_lo = _split2(w_ref[...])
    parl = jnp.dot(par_ref[...].astype(BF16), rep, preferred_element_type=F32)
    lane = lax.broadcasted_iota(I32, (tq, width), 1)
    wanted = (lane % 2).astype(F32) == parl
    wl_sc[0] = jnp.where(wanted, jnp.dot(w_hi, rep, preferred_element_type=F32), 0.0)
    wl_sc[1] = jnp.where(wanted, jnp.dot(w_lo, rep, preferred_element_type=F32), 0.0)
    mine = _own_sublane_mask(hk)
    for tt in range(tq):
        lhs = jnp.concatenate([jnp.where(mine, wl_sc[0, tt:tt + 1, :], 0.0),
                               jnp.where(mine, wl_sc[1, tt:tt + 1, :], 0.0)], axis=0).astype(BF16)
        r = jnp.dot(lhs, _pair_rows(tab, off_ref, tt, hk), preferred_element_type=F32)
        p_sc[tt * SUB:(tt + 1) * SUB, :] = r[0:SUB] + r[SUB:2 * SUB]
    peer = jnp.concatenate([p_sc[pl.ds(j, tq, stride=nj), :] for j in range(nj)], axis=1)
    y = x1_ref[...] + mod_ref[0, :, 5 * d:6 * d] * peer
    if final:
        y = y * lax.rsqrt(jnp.mean(y * y, axis=-1, keepdims=True) + EPS) * fg_ref[...]
    o_ref[...] = y


def _peer_v_kernel(off_ref, par_ref, w_ref, x1_ref, mod_ref, fg_ref, tab_hbm, o_ref, tab, sem, wl_sc, p_sc, *, final):
    _load_table_once(tab_hbm, tab, sem)
    _peer_v_body(off_ref, par_ref, w_ref, x1_ref, mod_ref, fg_ref, o_ref, tab, wl_sc, p_sc, final)


def _peer_v_topk_kernel(off_ref, par_ref, w_ref, x1_ref, mod_ref, fg_ref, st_ref, tab_hbm,
                        o_ref, off2_ref, par2_ref, g2_ref, tab, sem, wl_sc, p_sc, *scratch, final):
    half = pl.program_id(0) % 2
    _load_table_once(tab_hbm, tab, sem)
    for q in range(2):
        _topk_head_pair(st_ref, (2 * half + q) * 2, scratch, unroll=True)
    _peer_v_body(off_ref, par_ref, w_ref, x1_ref, mod_ref, fg_ref, o_ref, tab, wl_sc, p_sc, final)

    @pl.when(half == 1)
    def _():
        _topk_emit(scratch, off2_ref, par2_ref, g2_ref)


def _peer_v(off_flat, par, w, x1, mod3, final_g, vtab, tok0, *, seq, final, st=None, st_blk0=None):
    t, hk = w.shape
    d = x1.shape[1]
    tq = U_TQ
    assert t % (2 * tq) == 0 and 2 * tq == TOPK_TL and tok0 % tq == 0 and tok0 // seq == (tok0 + t - 1) // seq
    in_specs = [
        pl.BlockSpec((tq * hk,), lambda i: (i,), memory_space=pltpu.SMEM),
        pl.BlockSpec((tq, hk), lambda i: (i, 0)),
        pl.BlockSpec((tq, hk), lambda i: (i, 0)),
        pl.BlockSpec((tq, d), lambda i: (tok0 // tq + i, 0)),
        pl.BlockSpec((1, 1, mod3.shape[2]), lambda i: (tok0 // seq, 0, 0)),
        pl.BlockSpec((1, d), lambda i: (0, 0)),
    ]
    out_specs = [pl.BlockSpec((tq, d), lambda i: (i, 0))]
    out_shape = [jax.ShapeDtypeStruct((t, d), F32)]
    scratch = [pltpu.VMEM(vtab.shape, jnp.uint32), pltpu.SemaphoreType.DMA(()),
               pltpu.VMEM((2, tq, hk * PAIR_ROWS), F32), pltpu.VMEM((tq * (d // LANES), LANES), F32)]
    args = [off_flat, par, w, x1, mod3, final_g]
    if st is not None:
        in_specs.append(pl.BlockSpec((2 * PEER_HEADS, PEER_NKEYS, TOPK_TL), lambda i: (0, 0, st_blk0 + i // 2)))
        out_specs += [pl.BlockSpec((TOPK_TL, hk), lambda i: (i // 2, 0))] * 3
        out_shape += [jax.ShapeDtypeStruct((t, hk), I32), jax.ShapeDtypeStruct((t, hk), I32),
                      jax.ShapeDtypeStruct((t, hk), F32)]
        scratch += _topk_scratch()
        args.append(st)
    in_specs.append(pl.BlockSpec(memory_space=pl.ANY))
    args.append(vtab)
    return pl.pallas_call(
        functools.partial(_peer_v_kernel if st is None else _peer_v_topk_kernel, final=final),
        grid=(t // tq,),
        in_specs=in_specs,
        out_specs=out_specs,
        out_shape=out_shape,
        scratch_shapes=scratch,
        compiler_params=pltpu.CompilerParams(dimension_semantics=("arbitrary",),
                                             vmem_limit_bytes=VMEM_LIMIT),
        name="peer_v" if st is None else "peer_v_topk",
    )(*args)


def kernel(x, c, norm1_g, norm2_g, w_ada, b_ada, w_in, w_out, lam_q1, lam_k1, lam_q2, lam_k2, subln_g, sinks,
           peer_wq, peer_keys, peer_u, peer_v, final_g):
    bsz, seq, d = x.shape
    depth = w_in.shape[0]
    t = bsz * seq
    slopes = _alibi_slopes()
    sl_c = [float(s) for s in slopes[:C_HEADS]]
    qa_bias = _slope_bias_col(slopes[C_HEADS:C_HEADS + A_HEADS] * np.float32(LOG2E))
    qb_bias = _slope_bias_col(slopes[C_HEADS + A_HEADS:] * np.float32(LOG2E))
    mods = _adaln_mods(c, w_ada, b_ada)
    x2d = x.reshape(t, d)
    av, bw = A_HEADS * HEAD_DIM, B_HEADS * HEAD_DIM
    for l in range(depth):
        lam_init = 0.8 - 0.6 * math.exp(-0.3 * l)
        mod3 = mods[l].reshape(bsz, 1, N_MOD * d)
        wn, wt = _prep_in_weights(w_in[l])
        ka, kb, qc, vc2, qta, vta, qtb, vtb, ktc = _inproj(
            x2d, mod3, norm1_g[l].reshape(1, d), wn, wt, qa_bias, qb_bias, seq=seq)
        lamv = jnp.stack([lam_q1[l], lam_k1[l], lam_q2[l], lam_k2[l]]).astype(F32)
        sg = subln_g[l].reshape(HEAD_DIM, 1).astype(F32)
        oa = _diff_attention(qta, ka, vta, lamv, sg, bsz=bsz, seq=seq, lam_init=lam_init)
        ob = _moba_attention(qtb, kb, vtb, bsz=bsz, seq=seq)
        oc = _swa_attention(sinks[l].astype(F32), qc, ktc, vc2, bsz=bsz, seq=seq, slopes=sl_c)
        wo = w_out[l].astype(BF16)
        keys = peer_keys[l].reshape(2 * PEER_HEADS, PEER_NKEYS, -1).astype(BF16)
        x1, h2, st = _outproj(oa, ob, oc, wo[:av], wo[av:av + bw], wo[av + bw:], x2d, mod3,
                              norm2_g[l].reshape(1, d), peer_wq[l].astype(BF16), keys, seq=seq)
        assert seq % TOPK_TL == 0
        blocks = seq // TOPK_TL
        utab = _pack_expert_table(peer_u[l])
        vtab = _pack_expert_table(peer_v[l])
        h2t = h2.reshape(t, d // LANES, LANES)
        off_b, par_b, g_b = _peer_topk(st, 0, blocks)
        outs = []
        for b in range(bsz):
            (w_b,) = _peer_u(off_b.reshape(-1), par_b, g_b, h2t, utab, b * seq)
            more = dict(st=st, st_blk0=(b + 1) * blocks) if b + 1 < bsz else {}
            y_b, *nxt = _peer_v(off_b.reshape(-1), par_b, w_b, x1, mod3, final_g.reshape(1, d), vtab, b * seq,
                                seq=seq, final=(l == depth - 1), **more)
            outs.append(y_b)
            if nxt:
                off_b, par_b, g_b = nxt
        x2d = jnp.concatenate(outs)
    return x2d.reshape(bsz, seq, d)
```

```python
import functools
import math

import numpy as np
import jax
import jax.numpy as jnp
from jax import lax
from jax.experimental import pallas as pl
from jax.experimental.pallas import tpu as pltpu

F32 = jnp.float32
BF16 = jnp.bfloat16
I32 = jnp.int32

D_MODEL = 1024
HEAD_DIM = 64
N_HEADS_TOTAL = 16
A_HEADS = 4
B_HEADS = 6
C_HEADS = 6
C_KV_HEADS = 2
C_GROUP = 3
A_QK_DIM = 32
MOBA_BLOCK = 256
MOBA_TOPK = 3
WINDOW = 128
ALIBI_MAX = 8.0
PEER_HEADS = 8
PEER_NKEYS = 128
PEER_TOPK = 16
N_MOD = 6
EPS = 1e-6

LANES = 128
KT = 256
AUG0 = HEAD_DIM
SEL0 = AUG0 + 6
MAX_BLOCKS = LANES - SEL0
NEG_BIG = -1e30
LOG2E = math.log2(math.e)
VMEM_LIMIT = 56 * 1024 * 1024


def _alibi_slopes():
    n = N_HEADS_TOTAL
    return (2.0 ** (-ALIBI_MAX * np.arange(1, n + 1, dtype=np.float32) / n)).astype(np.float32)


def _split3(v):
    v = np.float32(v)
    hi = np.float32(np.asarray(v).astype(jnp.bfloat16).astype(np.float32))
    r = np.float32(v - hi)
    mid = np.float32(np.asarray(r).astype(jnp.bfloat16).astype(np.float32))
    lo = np.float32(np.float32(r - mid))
    lo = np.float32(np.asarray(lo).astype(jnp.bfloat16).astype(np.float32))
    return hi, mid, lo


def _slope_bias_col(slopes):
    col = np.zeros((LANES * len(slopes), 1), np.float32)
    for h, s in enumerate(slopes):
        hi, mid, lo = _split3(s)
        col[h * LANES + AUG0:h * LANES + AUG0 + 6, 0] = [hi, mid, lo, hi, mid, lo]
    return jnp.asarray(col)


def _mod_kernel(c_ref, w_ref, b_ref, o_ref):
    c = c_ref[...]
    cs = c * (1.0 / (1.0 + jnp.exp(-c)))
    o_ref[0] = jnp.dot(cs, w_ref[0], preferred_element_type=F32) + b_ref[0]


def _adaln_mods(c, w_ada, b_ada):
    depth, d, n = w_ada.shape
    bsz = c.shape[0]
    rows = -(-bsz // 8) * 8
    cp = jnp.pad(c, ((0, rows - bsz), (0, 0)))
    tn = 1536
    out = pl.pallas_call(
        _mod_kernel,
        grid=(depth, n // tn),
        in_specs=[
            pl.BlockSpec((rows, d), lambda l, j: (0, 0)),
            pl.BlockSpec((1, d, tn), lambda l, j: (l, 0, j)),
            pl.BlockSpec((1, 1, tn), lambda l, j: (l, 0, j)),
        ],
        out_specs=pl.BlockSpec((1, rows, tn), lambda l, j: (l, 0, j)),
        out_shape=jax.ShapeDtypeStruct((depth, rows, n), F32),
        compiler_params=pltpu.CompilerParams(dimension_semantics=("arbitrary", "arbitrary"),
                                             vmem_limit_bytes=VMEM_LIMIT),
        name="adaln_mods",
    )(cp, w_ada, b_ada.reshape(depth, 1, n))
    return out[:, :bsz]


NN_WIDTHS = (A_HEADS * LANES, B_HEADS * LANES, C_HEADS * HEAD_DIM, 2 * C_KV_HEADS * HEAD_DIM)
NT_ROWS = (A_HEADS * LANES, A_HEADS * HEAD_DIM, B_HEADS * LANES, B_HEADS * HEAD_DIM,
           C_KV_HEADS * LANES)


def _prep_in_weights(w):
    d = w.shape[0]
    aq, ak, av = A_HEADS * 2 * A_QK_DIM, A_HEADS * 2 * A_QK_DIM, A_HEADS * HEAD_DIM
    bw = B_HEADS * HEAD_DIM
    cq, ckv = C_HEADS * HEAD_DIM, C_KV_HEADS * HEAD_DIM
    cuts = np.cumsum([aq, ak, av, bw, bw, bw, cq, ckv]).tolist()
    qa, ka, va, qb, kb, vb, qc, kc, vc = jnp.split(w, cuts, axis=-1)

    def pad_heads(m, nh, scale):
        m = (m * scale).reshape(d, nh, HEAD_DIM)
        return jnp.pad(m, ((0, 0), (0, 0), (0, LANES - HEAD_DIM))).reshape(d, nh * LANES)

    vc2 = vc.reshape(d, C_KV_HEADS, 1, HEAD_DIM)
    vc2 = jnp.broadcast_to(vc2, (d, C_KV_HEADS, 2, HEAD_DIM)).reshape(d, 2 * ckv)
    kc2 = jnp.broadcast_to(kc.reshape(d, C_KV_HEADS, 1, HEAD_DIM), (d, C_KV_HEADS, 2, HEAD_DIM)).reshape(d, 2 * ckv)
    wn = jnp.concatenate([pad_heads(ka, A_HEADS, 1.0), pad_heads(kb, B_HEADS, 1.0),
                          qc * (HEAD_DIM ** -0.5), vc2], axis=1)
    wt = jnp.concatenate([pad_heads(qa, A_HEADS, A_QK_DIM ** -0.5 * LOG2E), va,
                          pad_heads(qb, B_HEADS, HEAD_DIM ** -0.5 * LOG2E), vb, kc2], axis=1).T
    return wn.astype(BF16), wt.astype(BF16)


def _rms_mod(x, g, sc, sh):
    ms = jnp.mean(x * x, axis=-1, keepdims=True)
    return (x * lax.rsqrt(ms + EPS) * g) * (1.0 + sc) + sh


def _inproj_kernel(x_ref, mod_ref, g_ref, wn_ref, wt_ref, qab_ref, qbb_ref,
                   ka_ref, kb_ref, qc_ref, vc_ref, qta_ref, vta_ref, qtb_ref, vtb_ref, ktc_ref,
                   *, tm, seq):
    d = x_ref.shape[1]
    x = x_ref[...]
    sh = mod_ref[0, :, 0:d]
    sc = mod_ref[0, :, d:2 * d]
    h = _rms_mod(x, g_ref[...], sc, sh).astype(BF16)
    pn = jnp.dot(h, wn_ref[...], preferred_element_type=F32)
    pt = lax.dot_general(wt_ref[...], h, (((1,), (1,)), ((), ())),
                         preferred_element_type=F32)

    pos = (pl.program_id(0) * tm) % seq + lax.broadcasted_iota(I32, (tm, LANES), 0)
    col = lax.broadcasted_iota(I32, (tm, LANES), 1)
    blk_id = pos // KT
    p_hi = (blk_id * KT).astype(F32)
    p_lo = (pos - blk_id * KT).astype(F32)
    aug_a = jnp.where((col >= AUG0) & (col < AUG0 + 3), p_hi,
                      jnp.where((col >= AUG0 + 3) & (col < AUG0 + 6), p_lo, 0.0))
    aug_b = jnp.where((col >= SEL0) & (col - SEL0 == blk_id), 1.0, aug_a)
    for hh in range(A_HEADS):
        ka_ref[:, hh * LANES:(hh + 1) * LANES] = (pn[:, hh * LANES:(hh + 1) * LANES] + aug_a).astype(BF16)
    o = NN_WIDTHS[0]
    for hh in range(B_HEADS):
        kb_ref[:, hh * LANES:(hh + 1) * LANES] = (pn[:, o + hh * LANES:o + (hh + 1) * LANES] + aug_b).astype(BF16)
    o += NN_WIDTHS[1]
    qc_ref[...] = pn[:, o:o + NN_WIDTHS[2]].astype(BF16)
    o += NN_WIDTHS[2]
    vc_ref[...] = pn[:, o:o + NN_WIDTHS[3]].astype(BF16)

    r0 = 0
    for ref, b, nr in zip((qta_ref, vta_ref, qtb_ref, vtb_ref, ktc_ref),
                          (qab_ref, None, qbb_ref, None, None), NT_ROWS):
        blk = pt[r0:r0 + nr, :]
        if b is not None:
            blk = blk + b[...]
        blk = blk.astype(BF16)
        for cc in range(tm // KT):
            ref[cc] = blk[:, cc * KT:(cc + 1) * KT]
        r0 += nr


def _inproj(x2d, mod3, g, wn, wt, qa_bias, qb_bias, *, seq, tm=512):
    t, d = x2d.shape
    assert seq % tm == 0 and tm % KT == 0
    nt = t // tm
    nn_total = sum(NN_WIDTHS)
    row_specs = [pl.BlockSpec((tm, wd), lambda i: (i, 0)) for wd in NN_WIDTHS]
    kt_specs = [pl.BlockSpec((tm // KT, r, KT), lambda i: (i, 0, 0)) for r in NT_ROWS]
    out_shape = ([jax.ShapeDtypeStruct((t, wd), BF16) for wd in NN_WIDTHS]
                 + [jax.ShapeDtypeStruct((t // KT, r, KT), BF16) for r in NT_ROWS])
    return pl.pallas_call(
        functools.partial(_inproj_kernel, tm=tm, seq=seq),
        grid=(nt,),
        in_specs=[
            pl.BlockSpec((tm, d), lambda i: (i, 0)),
            pl.BlockSpec((1, 1, mod3.shape[2]), lambda i: ((i * tm) // seq, 0, 0)),
            pl.BlockSpec((1, d), lambda i: (0, 0)),
            pl.BlockSpec((d, nn_total), lambda i: (0, 0)),
            pl.BlockSpec((sum(NT_ROWS), d), lambda i: (0, 0)),
            pl.BlockSpec((NT_ROWS[0], 1), lambda i: (0, 0)),
            pl.BlockSpec((NT_ROWS[2], 1), lambda i: (0, 0)),
        ],
        out_specs=row_specs + kt_specs,
        out_shape=out_shape,
        compiler_params=pltpu.CompilerParams(dimension_semantics=("arbitrary",),
                                             vmem_limit_bytes=VMEM_LIMIT),
        name="inproj",
    )(x2d, mod3, g, wn, wt, qa_bias, qb_bias)


ACC_ROWS = HEAD_DIM + 16


def _softmax_pv(ss, vt1s, m_ref, acc_ref):
    nh = len(ss)
    m_prev = [m_ref[hh] for hh in range(nh)]
    m_new = [jnp.maximum(m_prev[hh], jnp.max(ss[hh], axis=0, keepdims=True)) for hh in range(nh)]
    ps = [jnp.exp2((ss[hh] - m_new[hh]).astype(BF16)) for hh in range(nh)]
    pv = [jnp.dot(vt1s[hh], ps[hh], preferred_element_type=F32) for hh in range(nh)]
    for hh in range(nh):
        acc_ref[hh] = jnp.exp2(m_prev[hh] - m_new[hh]) * acc_ref[hh] + pv[hh]
        m_ref[hh] = m_new[hh]


def _flash_init(m_ref, acc_ref):
    m_ref[...] = jnp.full(m_ref.shape, -jnp.inf, F32)
    acc_ref[...] = jnp.zeros(acc_ref.shape, F32)


def _flash_finish(acc_ref, hh):
    acc = acc_ref[hh]
    return acc[0:HEAD_DIM] / acc[HEAD_DIM:HEAD_DIM + 1]


PAST_TILES = 4
SERIAL_TILES = 2


def _flash_causal_pair(k_ref, vt_ref, q_diag, q_past, qi, causal, m_ref, acc_ref, s_ref=None):
    _flash_init(m_ref, acc_ref)
    nkeys = PAST_TILES * KT
    nh = len(q_past)

    def vt1(kj, n, hh):
        rows = slice(hh * HEAD_DIM, (hh + 1) * HEAD_DIM)
        vt = vt_ref[kj, rows, :] if n == 1 else jnp.concatenate([vt_ref[kj + c, rows, :] for c in range(n)], axis=1)
        return jnp.concatenate([vt, jnp.ones((ACC_ROWS - HEAD_DIM, n * KT), BF16)], axis=0)

    def scores(kj, n, qs, hh):
        kk = k_ref[pl.ds(pl.multiple_of(kj * KT, KT), n * KT), hh * LANES:(hh + 1) * LANES]
        return jnp.dot(kk, qs[hh], preferred_element_type=F32)

    def step(kj, n, qs, mask=None):
        ss = [scores(kj, n, qs, hh) for hh in range(nh)]
        if mask is not None:
            ss = [jnp.where(mask, s, -jnp.inf) for s in ss]
        _softmax_pv(ss, [vt1(kj, n, hh) for hh in range(nh)], m_ref, acc_ref)

    step(qi, 1, q_diag, causal)

    if s_ref is None:
        def body(j, carry):
            step(j * SERIAL_TILES, SERIAL_TILES, q_past)
            return carry

        lax.fori_loop(0, qi // SERIAL_TILES, body, 0)
        for r in range(SERIAL_TILES - 1):
            @pl.when(qi % SERIAL_TILES > r)
            def _():
                step(qi - 1 - r, 1, q_past)
        return

    ngroups = k_ref.shape[0] // nkeys

    def scores_into(slot, g):
        for hh in range(nh):
            s_ref[slot, hh] = scores(g * PAST_TILES, PAST_TILES, q_past, hh)

    scores_into(0, 0)

    def body(g, carry):
        scores_into((g + 1) % 2, jnp.minimum(g + 1, ngroups - 1))
        _softmax_pv([s_ref[g % 2, hh] for hh in range(nh)],
                    [vt1(g * PAST_TILES, PAST_TILES, hh) for hh in range(nh)], m_ref, acc_ref)
        return carry

    lax.fori_loop(0, (qi + PAST_TILES - 1) // PAST_TILES, body, 0)


def _diff_kernel(qt_ref, k_ref, vt_ref, lamv_ref, sg_ref, o_ref, m_ref, acc_ref, *, lam_init):
    tq = qt_ref.shape[2]
    qi = pl.program_id(2)
    lv = lamv_ref[...]
    lam = (jnp.exp(jnp.sum(lv[0:1] * lv[1:2], axis=-1, keepdims=True))
           - jnp.exp(jnp.sum(lv[2:3] * lv[3:4], axis=-1, keepdims=True)) + lam_init)
    row = lax.broadcasted_iota(I32, (LANES, tq), 0)
    kr = lax.broadcasted_iota(I32, (KT, 2 * tq), 0)
    qc = lax.broadcasted_iota(I32, (KT, 2 * tq), 1)
    causal = kr <= jnp.where(qc >= tq, qc - tq, qc)
    nh = qt_ref.shape[1] // LANES
    qs = []
    for hh in range(nh):
        qt = qt_ref[0, hh * LANES:(hh + 1) * LANES, :]
        zero = jnp.zeros_like(qt)
        q1 = jnp.where((row < A_QK_DIM) | (row >= AUG0), qt, zero)
        q2 = jnp.where(row >= A_QK_DIM, qt, zero)
        qs.append(jnp.concatenate([q1, q2], axis=1))
    _flash_causal_pair(k_ref, vt_ref, qs, qs, qi, causal, m_ref, acc_ref)
    res = []
    for hh in range(nh):
        o = _flash_finish(acc_ref, hh)
        od = o[:, :tq] - lam * o[:, tq:]
        ms = jnp.mean(od * od, axis=0, keepdims=True)
        res.append(od * lax.rsqrt(ms + EPS) * sg_ref[...] * (1.0 - lam_init))
    o_ref[...] = jnp.concatenate(res, axis=0).T.astype(BF16)


def _diff_attention(qta, ka, vta, lamv, sg, *, bsz, seq, lam_init, nh=A_HEADS):
    tq = KT
    nq = seq // tq
    t = bsz * seq
    return pl.pallas_call(
        functools.partial(_diff_kernel, lam_init=lam_init),
        grid=(bsz, A_HEADS // nh, nq),
        in_specs=[
            pl.BlockSpec((1, nh * LANES, tq), lambda b, hp, qi: (b * nq + qi, hp, 0)),
            pl.BlockSpec((seq, nh * LANES), lambda b, hp, qi: (b, hp)),
            pl.BlockSpec((nq, nh * HEAD_DIM, KT), lambda b, hp, qi: (b, hp, 0)),
            pl.BlockSpec((4, A_QK_DIM), lambda b, hp, qi: (0, 0)),
            pl.BlockSpec((HEAD_DIM, 1), lambda b, hp, qi: (0, 0)),
        ],
        out_specs=pl.BlockSpec((tq, nh * HEAD_DIM), lambda b, hp, qi: (b * nq + qi, hp)),
        out_shape=jax.ShapeDtypeStruct((t, A_HEADS * HEAD_DIM), BF16),
        scratch_shapes=[pltpu.VMEM((nh, 1, 2 * tq), F32), pltpu.VMEM((nh, ACC_ROWS, 2 * tq), F32)],
        compiler_params=pltpu.CompilerParams(dimension_semantics=("arbitrary",) * 3,
                                             vmem_limit_bytes=VMEM_LIMIT),
        name="diff_attn",
    )(qta, ka, vta, lamv, sg)


def _moba_kernel(qt_ref, k_ref, vt_ref, o_ref, m_ref, acc_ref, km_ref, s_ref):
    tq = qt_ref.shape[2]
    nh = qt_ref.shape[1] // LANES
    nb = vt_ref.shape[0]
    qi = pl.program_id(2)

    @pl.when(qi == 0)
    def _():
        lane1 = lax.broadcasted_iota(I32, (1, LANES), 1)
        for hh in range(nh):
            km_ref[hh] = jnp.zeros((LANES, LANES), F32)

            def put_block(j, carry):
                blk = k_ref[pl.ds(pl.multiple_of(j * KT, KT), KT), hh * LANES:(hh + 1) * LANES].astype(F32)
                mean = jnp.sum(blk, axis=0, keepdims=True) * (1.0 / KT)
                km_ref[hh, pl.ds(SEL0 + j, 1), :] = jnp.where(lane1 < HEAD_DIM, mean, 0.0)
                return carry
            lax.fori_loop(0, nb, put_block, 0)

    kr = lax.broadcasted_iota(I32, (KT, tq), 0)
    qc = lax.broadcasted_iota(I32, (KT, tq), 1)
    causal = kr <= qc
    nsel = -(-(SEL0 - AUG0 + nb) // 16) * 16
    band = slice(AUG0, AUG0 + nsel)
    row = lax.broadcasted_iota(I32, (nsel, tq), 0) + AUG0
    in_sel = row >= SEL0
    qts = [qt_ref[0, hh * LANES:(hh + 1) * LANES, :] for hh in range(nh)]
    curs = []
    for hh in range(nh):
        km_hi, km_lo = _split2(km_ref[hh, band, :])
        gate = (jnp.dot(km_hi, qts[hh], preferred_element_type=F32)
                + jnp.dot(km_lo, qts[hh], preferred_element_type=F32))
        curs.append(jnp.where(in_sel & (row < SEL0 + qi), gate, -jnp.inf))
    sels = [jnp.zeros((nsel, tq), jnp.bool_)] * nh
    for _ in range(MOBA_TOPK):
        for hh in range(nh):
            mx = jnp.max(curs[hh], axis=0, keepdims=True)
            first = jnp.min(jnp.where(curs[hh] == mx, row, 4 * LANES), axis=0, keepdims=True)
            pick = (row == first) & (mx > -jnp.inf)
            sels[hh] = sels[hh] | pick
            curs[hh] = jnp.where(pick, -jnp.inf, curs[hh])
    q_diag, q_past = [], []
    for hh in range(nh):
        qt, mid = qts[hh], qts[hh][band]
        rest = [qt[AUG0 + nsel:]] if AUG0 + nsel < LANES else []
        q_diag.append(jnp.concatenate([qt[:AUG0], jnp.where(in_sel, jnp.zeros_like(mid), mid)] + rest, axis=0))
        bias = jnp.where(sels[hh], 0.0, NEG_BIG).astype(BF16)
        q_past.append(jnp.concatenate([qt[:AUG0], jnp.where(in_sel, bias, mid)] + rest, axis=0))
    _flash_causal_pair(k_ref, vt_ref, q_diag, q_past, qi, causal, m_ref, acc_ref, s_ref)
    res = [_flash_finish(acc_ref, hh) for hh in range(nh)]
    o_ref[...] = jnp.concatenate(res, axis=0).T.astype(BF16)


def _moba_attention(qtb, kb, vtb, *, bsz, seq, nh=B_HEADS):
    tq = KT
    nq = seq // tq
    assert nq <= MAX_BLOCKS and seq % (PAST_TILES * KT) == 0
    t = bsz * seq
    return pl.pallas_call(
        _moba_kernel,
        grid=(bsz, B_HEADS // nh, nq),
        in_specs=[
            pl.BlockSpec((1, nh * LANES, tq), lambda b, hp, qi: (b * nq + qi, hp, 0)),
            pl.BlockSpec((seq, nh * LANES), lambda b, hp, qi: (b, hp), pipeline_mode=pl.Buffered(1)),
            pl.BlockSpec((nq, nh * HEAD_DIM, KT), lambda b, hp, qi: (b, hp, 0), pipeline_mode=pl.Buffered(1)),
        ],
        out_specs=pl.BlockSpec((tq, nh * HEAD_DIM), lambda b, hp, qi: (b * nq + qi, hp)),
        out_shape=jax.ShapeDtypeStruct((t, B_HEADS * HEAD_DIM), BF16),
        scratch_shapes=[pltpu.VMEM((nh, 1, tq), F32), pltpu.VMEM((nh, ACC_ROWS, tq), F32),
                        pltpu.VMEM((nh, LANES, LANES), F32), pltpu.VMEM((2, nh, PAST_TILES * KT, tq), F32)],
        compiler_params=pltpu.CompilerParams(dimension_semantics=("arbitrary",) * 3,
                                             vmem_limit_bytes=VMEM_LIMIT),
        name="moba_attn",
    )(qtb, kb, vtb)


def _swa_kernel(sink_ref, q_ref, ktp_ref, ktc_ref, vp_ref, vc_ref, o_ref, *, slopes):
    w = WINDOW
    n = pl.program_id(1)
    lane = lax.broadcasted_iota(I32, (w, LANES), 1)
    r2 = lax.broadcasted_iota(I32, (w, 2 * w), 0)
    c2 = lax.broadcasted_iota(I32, (w, 2 * w), 1)
    rel = r2 + w - c2
    band = (rel >= 0) & (rel < w)
    relf = rel.astype(F32)
    for half in range(KT // w):
        mask = band & ((c2 >= w) | (n > 0)) if half == 0 else band
        rows = slice(half * w, (half + 1) * w)
        res = []
        for hq in range(C_HEADS):
            kv = hq // C_GROUP
            kvl = slice(kv * LANES, (kv + 1) * LANES)
            qp = q_ref[rows, (hq // 2) * LANES:(hq // 2 + 1) * LANES]
            qm = jnp.where((lane < HEAD_DIM) == (hq % 2 == 0), qp, jnp.zeros_like(qp))
            if half == 0:
                kt = jnp.concatenate([ktp_ref[0, kvl, KT - w:KT], ktc_ref[0, kvl, 0:w]], axis=1)
                vv = jnp.concatenate([vp_ref[KT - w:KT, kvl], vc_ref[0:w, kvl]], axis=0)
            else:
                kt = ktc_ref[0, kvl, (half - 1) * w:(half + 1) * w]
                vv = vc_ref[(half - 1) * w:(half + 1) * w, kvl]
            s = jnp.dot(qm, kt, preferred_element_type=F32)
            s = jnp.where(mask, s - slopes[hq] * relf, -jnp.inf)
            sink = sink_ref[hq]
            m = jnp.maximum(jnp.max(s, axis=-1, keepdims=True), sink)
            e = jnp.exp(s - m)
            den = jnp.sum(e, axis=-1, keepdims=True) + jnp.exp(sink - m)
            p = (e / den).astype(BF16)
            res.append(jnp.dot(p, vv, preferred_element_type=F32))
        for pr in range(C_HEADS // 2):
            o_ref[rows, pr * LANES:(pr + 1) * LANES] = jnp.where(lane < HEAD_DIM, res[2 * pr],
                                                                 res[2 * pr + 1]).astype(BF16)


def _swa_attention(sinks, qc, ktc, vc2, *, bsz, seq, slopes):
    tq = KT
    nq = seq // tq
    t = bsz * seq
    return pl.pallas_call(
        functools.partial(_swa_kernel, slopes=slopes),
        grid=(bsz, nq),
        in_specs=[
            pl.BlockSpec(memory_space=pltpu.SMEM),
            pl.BlockSpec((tq, C_HEADS * HEAD_DIM), lambda b, n: (b * nq + n, 0)),
            pl.BlockSpec((1, C_KV_HEADS * LANES, KT), lambda b, n: (b * nq + jnp.maximum(n - 1, 0), 0, 0)),
            pl.BlockSpec((1, C_KV_HEADS * LANES, KT), lambda b, n: (b * nq + n, 0, 0)),
            pl.BlockSpec((tq, C_KV_HEADS * LANES), lambda b, n: (b * nq + jnp.maximum(n - 1, 0), 0)),
            pl.BlockSpec((tq, C_KV_HEADS * LANES), lambda b, n: (b * nq + n, 0)),
        ],
        out_specs=pl.BlockSpec((tq, C_HEADS * HEAD_DIM), lambda b, n: (b * nq + n, 0)),
        out_shape=jax.ShapeDtypeStruct((t, C_HEADS * HEAD_DIM), BF16),
        compiler_params=pltpu.CompilerParams(dimension_semantics=("arbitrary",) * 2,
                                             vmem_limit_bytes=VMEM_LIMIT),
        name="swa_attn",
    )(sinks, qc, ktc, ktc, vc2, vc2)


def _outproj_kernel(oa_ref, ob_ref, oc_ref, woa_ref, wob_ref, woc_ref, x_ref, mod_ref, g_ref, wq_ref, keys_ref,
                    x1_ref, h2_ref, st_ref):
    d = x_ref.shape[1]
    mix = (jnp.dot(oa_ref[...], woa_ref[...], preferred_element_type=F32)
           + jnp.dot(ob_ref[...], wob_ref[...], preferred_element_type=F32)
           + jnp.dot(oc_ref[...], woc_ref[...], preferred_element_type=F32))
    g1 = mod_ref[0, :, 2 * d:3 * d]
    sh2 = mod_ref[0, :, 3 * d:4 * d]
    sc2 = mod_ref[0, :, 4 * d:5 * d]
    x1 = x_ref[...] + g1 * mix
    x1_ref[...] = x1
    h2 = _rms_mod(x1, g_ref[...], sc2, sh2)
    nj = d // LANES
    for j in range(nj):
        h2_ref[pl.ds(j, x1.shape[0], stride=nj), :] = h2[:, j * LANES:(j + 1) * LANES]
    pq = jnp.dot(h2.astype(BF16), wq_ref[...], preferred_element_type=F32).astype(BF16)
    for hp in range(2 * PEER_HEADS):
        st_ref[hp] = lax.dot_general(keys_ref[hp], pq[:, hp * LANES:(hp + 1) * LANES],
                                     (((1,), (1,)), ((), ())), preferred_element_type=F32)


def _outproj(oa, ob, oc, woa, wob, woc, x2d, mod3, g, wq, keys, *, seq, tm=256):
    t, d = x2d.shape
    nt = t // tm
    nq = wq.shape[1]
    full = lambda a: pl.BlockSpec(a.shape, lambda i: (0,) * a.ndim)
    return pl.pallas_call(
        _outproj_kernel,
        grid=(nt,),
        in_specs=[
            pl.BlockSpec((tm, oa.shape[1]), lambda i: (i, 0)),
            pl.BlockSpec((tm, ob.shape[1]), lambda i: (i, 0)),
            pl.BlockSpec((tm, oc.shape[1]), lambda i: (i, 0)),
            full(woa), full(wob), full(woc),
            pl.BlockSpec((tm, d), lambda i: (i, 0)),
            pl.BlockSpec((1, 1, mod3.shape[2]), lambda i: ((i * tm) // seq, 0, 0)),
            pl.BlockSpec((1, d), lambda i: (0, 0)),
            full(wq), full(keys),
        ],
        out_specs=[pl.BlockSpec((tm, d), lambda i: (i, 0)),
                   pl.BlockSpec((tm * (d // LANES), LANES), lambda i: (i, 0)),
                   pl.BlockSpec((2 * PEER_HEADS, PEER_NKEYS, tm), lambda i: (0, 0, i))],
        out_shape=[jax.ShapeDtypeStruct((t, d), F32), jax.ShapeDtypeStruct((t * (d // LANES), LANES), F32),
                   jax.ShapeDtypeStruct((2 * PEER_HEADS, PEER_NKEYS, t), F32)],
        compiler_params=pltpu.CompilerParams(dimension_semantics=("arbitrary",),
                                             vmem_limit_bytes=VMEM_LIMIT),
        name="outproj_peerq",
    )(oa, ob, oc, woa, wob, woc, x2d, mod3, g, wq, keys)


_CAND_BLOCKS = ((0, 16),) + tuple((i, 8) for i in range(1, 8))
_CAND_ROWS = 16 + 7 * 8 + 8
_BIG_I = np.int32(2 ** 30)


TOPK_TL = LANES


def _topk_rows(curs, keys, val_refs, key_refs, unroll):
    def body(r, curs):
        out = []
        for cur, key, val_ref, key_ref in zip(curs, keys, val_refs, key_refs):
            m = jnp.max(cur, axis=0, keepdims=True)
            kmin = jnp.min(jnp.where(cur == m, key, _BIG_I), axis=0, keepdims=True)
            val_ref[pl.ds(r, 1), :] = m
            key_ref[pl.ds(r, 1), :] = kmin
            out.append(jnp.where(key == kmin, -jnp.inf, cur))
        return tuple(out)
    lax.fori_loop(0, PEER_TOPK, body, tuple(curs), unroll=unroll)


def _topk_scratch():
    pair = [pltpu.VMEM((2, PEER_TOPK, TOPK_TL), F32), pltpu.VMEM((2, PEER_TOPK, TOPK_TL), I32)]
    hk = PEER_HEADS * PEER_TOPK
    return pair * 3 + [pltpu.VMEM((hk, TOPK_TL), I32), pltpu.VMEM((hk, TOPK_TL), F32)]


def _product_candidates(sv1, si1, sv2, si2):
    tl = sv1.shape[1]
    jrow8 = lax.broadcasted_iota(I32, (8, tl), 0)
    jrow16 = lax.broadcasted_iota(I32, (16, tl), 0)
    cands, keys = [], []
    for i, nj in _CAND_BLOCKS:
        jrow = jrow16 if nj == 16 else jrow8
        cands.append(sv1[i:i + 1] + sv2[0:nj])
        keys.append((i * PEER_TOPK + jrow) * (PEER_NKEYS * PEER_NKEYS) + si1[i:i + 1] * PEER_NKEYS + si2[0:nj])
    cands.append(sv1[8:16] + sv2[0:1])
    keys.append((jrow8 + 8) * (PEER_TOPK * PEER_NKEYS * PEER_NKEYS) + si1[8:16] * PEER_NKEYS + si2[0:1])
    return jnp.concatenate(cands, axis=0), jnp.concatenate(keys, axis=0)


def _store_gates(h, top, keys, ei_ref, gg_ref):
    rows = pl.ds(pl.multiple_of(h * PEER_TOPK, PEER_TOPK), PEER_TOPK)
    e = jnp.exp(top - top[0:1])
    gg_ref[rows, :] = e / jnp.sum(e, axis=0, keepdims=True)
    ei_ref[rows, :] = keys & (PEER_NKEYS * PEER_NKEYS - 1)


def _topk_head_pair(st_ref, h0, scratch, unroll=False):
    v1_ref, k1_ref, v2_ref, k2_ref, vt_ref, kt_ref, ei_ref, gg_ref = scratch
    row = lax.broadcasted_iota(I32, (PEER_NKEYS, st_ref.shape[2]), 0)
    cand, ckey = [], []
    for s in range(2):
        h = h0 + s
        _topk_rows([st_ref[2 * h], st_ref[2 * h + 1]], [row, row],
                   [v1_ref.at[s], v2_ref.at[s]], [k1_ref.at[s], k2_ref.at[s]], unroll)
        c, k = _product_candidates(v1_ref[s], k1_ref[s], v2_ref[s], k2_ref[s])
        cand.append(c)
        ckey.append(k)
    _topk_rows(cand, ckey, [vt_ref.at[0], vt_ref.at[1]], [kt_ref.at[0], kt_ref.at[1]], unroll)
    for s in range(2):
        _store_gates(h0 + s, vt_ref[s], kt_ref[s], ei_ref, gg_ref)


def _topk_emit(scratch, off_ref, par_ref, g_ref):
    ei = scratch[6][...].T
    off_ref[...] = (ei & (PEER_HALF - 1)) * SUB
    par_ref[...] = ei // PEER_HALF
    g_ref[...] = scratch[7][...].T


def _peer_topk_kernel(st_ref, off_ref, par_ref, g_ref, *scratch):
    for h0 in range(0, PEER_HEADS, 2):
        _topk_head_pair(st_ref, h0, scratch)
    _topk_emit(scratch, off_ref, par_ref, g_ref)


def _peer_topk(st, blk0, nblk):
    hk = PEER_HEADS * PEER_TOPK
    t = nblk * TOPK_TL
    return pl.pallas_call(
        _peer_topk_kernel,
        grid=(nblk,),
        in_specs=[pl.BlockSpec((2 * PEER_HEADS, PEER_NKEYS, TOPK_TL), lambda i: (0, 0, blk0 + i))],
        out_specs=[pl.BlockSpec((TOPK_TL, hk), lambda i: (i, 0))] * 3,
        out_shape=[jax.ShapeDtypeStruct((t, hk), I32), jax.ShapeDtypeStruct((t, hk), I32),
                   jax.ShapeDtypeStruct((t, hk), F32)],
        scratch_shapes=_topk_scratch(),
        compiler_params=pltpu.CompilerParams(dimension_semantics=("arbitrary",),
                                             vmem_limit_bytes=VMEM_LIMIT),
        name="peer_topk",
    )(st)


SUB = 8


def _erf(x):
    return lax.erf(x)


HALF_HI = 0xFFFF0000


PEER_HALF = PEER_NKEYS * PEER_NKEYS // 2


def _pack_kernel(lo_ref, hi_ref, o_ref):
    pairs = lo_ref.shape[0]
    nj = lo_ref.shape[1] // LANES
    for j in range(nj):
        cols = slice(j * LANES, (j + 1) * LANES)
        lo = pltpu.bitcast(lo_ref[:, cols].astype(BF16).astype(F32), jnp.uint32)
        hi = pltpu.bitcast(hi_ref[:, cols].astype(BF16).astype(F32), jnp.uint32)
        o_ref[pl.ds(j, pairs, stride=nj), :] = (lo >> 16) | (hi & jnp.uint32(HALF_HI))


def _pack_expert_table(w, pairs=256):
    n, d = w.shape
    nj = d // LANES
    half_blocks = n // 2 // pairs
    return pl.pallas_call(
        _pack_kernel,
        grid=(half_blocks,),
        in_specs=[pl.BlockSpec((pairs, d), lambda i: (i, 0)),
                  pl.BlockSpec((pairs, d), lambda i: (half_blocks + i, 0))],
        out_specs=pl.BlockSpec((pairs * nj, LANES), lambda i: (i, 0)),
        out_shape=jax.ShapeDtypeStruct((n // 2 * nj, LANES), jnp.uint32),
        compiler_params=pltpu.CompilerParams(dimension_semantics=("arbitrary",),
                                             vmem_limit_bytes=VMEM_LIMIT),
        name="pack_table",
    )(w, w)


def _load_table_once(tab_hbm, tab, sem):
    @pl.when(pl.program_id(0) == 0)
    def _():
        cp = pltpu.make_async_copy(tab_hbm, tab, sem)
        cp.start()
        cp.wait()


def _pair_tile(tab, off):
    return tab[pl.ds(pl.multiple_of(off, SUB), SUB), :]


def _group_matrix(rows, cols):
    r = lax.broadcasted_iota(I32, (rows, cols), 0)
    c = lax.broadcasted_iota(I32, (rows, cols), 1)
    return (c // (cols // rows) == r).astype(BF16)


def _split2(x):
    hi = x.astype(BF16)
    return hi, (x - hi.astype(F32)).astype(BF16)


PAIR_ROWS = 2 * SUB


def _pair_rows(tab, off_ref, tt, hk):
    return jnp.concatenate([pltpu.bitcast(_pair_tile(tab, off_ref[tt * hk + k]), BF16) for k in range(hk)], axis=0)


def _own_sublane_mask(hk):
    srow = lax.broadcasted_iota(I32, (SUB, hk * PAIR_ROWS), 0)
    scol = lax.broadcasted_iota(I32, (SUB, hk * PAIR_ROWS), 1)
    return (scol % PAIR_ROWS) // 2 == srow


def _fold_matrix(hk):
    c = jnp.arange(hk * PAIR_ROWS)
    return jax.nn.one_hot((c % 2) * hk + c // PAIR_ROWS, 2 * hk, dtype=BF16)


def _peer_u_kernel(off_ref, par_ref, h_ref, g_ref, fold_ref, tab_hbm, w_ref, tab, sem, d_sc):
    tq, hk = g_ref.shape
    _load_table_once(tab_hbm, tab, sem)
    mine = _own_sublane_mask(hk)
    nt = (((1,), (1,)), ((), ()))
    for tt in range(tq):
        h_hi, h_lo = _split2(h_ref[tt])
        s = lax.dot_general(jnp.concatenate([h_hi, h_lo], axis=0), _pair_rows(tab, off_ref, tt, hk), nt,
                            preferred_element_type=F32)
        d = jnp.where(mine, s[0:SUB] + s[SUB:2 * SUB], 0.0)
        d_sc[tt:tt + 1, :] = jnp.sum(d, axis=0, keepdims=True)
    d_hi, d_lo = _split2(d_sc[...])
    a2 = jnp.dot(jnp.concatenate([d_hi, d_lo], axis=0), fold_ref[...], preferred_element_type=F32)
    a2 = a2[0:tq] + a2[tq:2 * tq]
    a = jnp.where(par_ref[...] == 0, a2[:, :hk], a2[:, hk:])
    w_ref[...] = g_ref[...] * (0.5 * a * (1.0 + _erf(a * (2.0 ** -0.5))))


U_TQ = 64


def _peer_u(off_flat, par, g, h2, utab, tok0):
    t, hk = g.shape
    nj = h2.shape[1]
    tq = U_TQ
    assert t % tq == 0 and tok0 % tq == 0
    fold = _fold_matrix(hk)
    return pl.pallas_call(
        _peer_u_kernel,
        grid=(t // tq,),
        in_specs=[
            pl.BlockSpec((tq * hk,), lambda i: (i,), memory_space=pltpu.SMEM),
            pl.BlockSpec((tq, hk), lambda i: (i, 0)),
            pl.BlockSpec((tq, nj, LANES), lambda i: (tok0 // tq + i, 0, 0)),
            pl.BlockSpec((tq, hk), lambda i: (i, 0)),
            pl.BlockSpec(fold.shape, lambda i: (0, 0)),
            pl.BlockSpec(memory_space=pl.ANY),
        ],
        out_specs=pl.BlockSpec((tq, hk), lambda i: (i, 0)),
        out_shape=jax.ShapeDtypeStruct((t, hk), F32),
        scratch_shapes=[pltpu.VMEM(utab.shape, jnp.uint32), pltpu.SemaphoreType.DMA(()),
                        pltpu.VMEM((tq, hk * PAIR_ROWS), F32)],
        compiler_params=pltpu.CompilerParams(dimension_semantics=("arbitrary",),
                                             vmem_limit_bytes=VMEM_LIMIT),
        name="peer_u",
    )(off_flat, par, h2, g, fold, utab)


def _peer_v_body(off_ref, par_ref, w_ref, x1_ref, mod_ref, fg_ref, o_ref, tab, wl_sc, p_sc, final):
    tq, hk = w_ref.shape
    d = x1_ref.shape[1]
    nj = d // LANES
    width = hk * PAIR_ROWS
    rep = _group_matrix(hk, width)
    w_hi, w_lo = _split2(w_ref[...])
    parl = jnp.dot(par_ref[...].astype(BF16), rep, preferred_element_type=F32)
    lane = lax.broadcasted_iota(I32, (tq, width), 1)
    wanted = (lane % 2).astype(F32) == parl
    wl_sc[0] = jnp.where(wanted, jnp.dot(w_hi, rep, preferred_element_type=F32), 0.0)
    wl_sc[1] = jnp.where(wanted, jnp.dot(w_lo, rep, preferred_element_type=F32), 0.0)
    mine = _own_sublane_mask(hk)
    for tt in range(tq):
        lhs = jnp.concatenate([jnp.where(mine, wl_sc[0, tt:tt + 1, :], 0.0),
                               jnp.where(mine, wl_sc[1, tt:tt + 1, :], 0.0)], axis=0).astype(BF16)
        r = jnp.dot(lhs, _pair_rows(tab, off_ref, tt, hk), preferred_element_type=F32)
        p_sc[tt * SUB:(tt + 1) * SUB, :] = r[0:SUB] + r[SUB:2 * SUB]
    peer = jnp.concatenate([p_sc[pl.ds(j, tq, stride=nj), :] for j in range(nj)], axis=1)
    y = x1_ref[...] + mod_ref[0, :, 5 * d:6 * d] * peer
    if final:
        y = y * lax.rsqrt(jnp.mean(y * y, axis=-1, keepdims=True) + EPS) * fg_ref[...]
    o_ref[...] = y


def _peer_v_kernel(off_ref, par_ref, w_ref, x1_ref, mod_ref, fg_ref, tab_hbm, o_ref, tab, sem, wl_sc, p_sc, *, final):
    _load_table_once(tab_hbm, tab, sem)
    _peer_v_body(off_ref, par_ref, w_ref, x1_ref, mod_ref, fg_ref, o_ref, tab, wl_sc, p_sc, final)


def _peer_v_topk_kernel(off_ref, par_ref, w_ref, x1_ref, mod_ref, fg_ref, st_ref, tab_hbm,
                        o_ref, off2_ref, par2_ref, g2_ref, tab, sem, wl_sc, p_sc, *scratch, final):
    half = pl.program_id(0) % 2
    _load_table_once(tab_hbm, tab, sem)
    for q in range(2):
        _topk_head_pair(st_ref, (2 * half + q) * 2, scratch, unroll=True)
    _peer_v_body(off_ref, par_ref, w_ref, x1_ref, mod_ref, fg_ref, o_ref, tab, wl_sc, p_sc, final)

    @pl.when(half == 1)
    def _():
        _topk_emit(scratch, off2_ref, par2_ref, g2_ref)


def _peer_v(off_flat, par, w, x1, mod3, final_g, vtab, tok0, *, seq, final, st=None, st_blk0=None):
    t, hk = w.shape
    d = x1.shape[1]
    tq = U_TQ
    assert t % (2 * tq) == 0 and 2 * tq == TOPK_TL and tok0 % tq == 0 and tok0 // seq == (tok0 + t - 1) // seq
    in_specs = [
        pl.BlockSpec((tq * hk,), lambda i: (i,), memory_space=pltpu.SMEM),
        pl.BlockSpec((tq, hk), lambda i: (i, 0)),
        pl.BlockSpec((tq, hk), lambda i: (i, 0)),
        pl.BlockSpec((tq, d), lambda i: (tok0 // tq + i, 0)),
        pl.BlockSpec((1, 1, mod3.shape[2]), lambda i: (tok0 // seq, 0, 0)),
        pl.BlockSpec((1, d), lambda i: (0, 0)),
    ]
    out_specs = [pl.BlockSpec((tq, d), lambda i: (tok0 // tq + i, 0))]
    out_shape = [jax.ShapeDtypeStruct(x1.shape, F32)]
    scratch = [pltpu.VMEM(vtab.shape, jnp.uint32), pltpu.SemaphoreType.DMA(()),
               pltpu.VMEM((2, tq, hk * PAIR_ROWS), F32), pltpu.VMEM((tq * (d // LANES), LANES), F32)]
    args = [off_flat, par, w, x1, mod3, final_g]
    if st is not None:
        in_specs.append(pl.BlockSpec((2 * PEER_HEADS, PEER_NKEYS, TOPK_TL), lambda i: (0, 0, st_blk0 + i // 2)))
        out_specs += [pl.BlockSpec((TOPK_TL, hk), lambda i: (i // 2, 0))] * 3
        out_shape += [jax.ShapeDtypeStruct((t, hk), I32), jax.ShapeDtypeStruct((t, hk), I32),
                      jax.ShapeDtypeStruct((t, hk), F32)]
        scratch += _topk_scratch()
        args.append(st)
    in_specs.append(pl.BlockSpec(memory_space=pl.ANY))
    args.append(vtab)
    return pl.pallas_call(
        functools.partial(_peer_v_kernel if st is None else _peer_v_topk_kernel, final=final),
        grid=(t // tq,),
        in_specs=in_specs,
        out_specs=out_specs,
        out_shape=out_shape,
        scratch_shapes=scratch,
        input_output_aliases={3: 0},
        compiler_params=pltpu.CompilerParams(dimension_semantics=("arbitrary",),
                                             vmem_limit_bytes=VMEM_LIMIT),
        name="peer_v" if st is None else "peer_v_topk",
    )(*args)


def kernel(x, c, norm1_g, norm2_g, w_ada, b_ada, w_in, w_out, lam_q1, lam_k1, lam_q2, lam_k2, subln_g, sinks,
           peer_wq, peer_keys, peer_u, peer_v, final_g):
    bsz, seq, d = x.shape
    depth = w_in.shape[0]
    t = bsz * seq
    slopes = _alibi_slopes()
    sl_c = [float(s) for s in slopes[:C_HEADS]]
    qa_bias = _slope_bias_col(slopes[C_HEADS:C_HEADS + A_HEADS] * np.float32(LOG2E))
    qb_bias = _slope_bias_col(slopes[C_HEADS + A_HEADS:] * np.float32(LOG2E))
    mods = _adaln_mods(c, w_ada, b_ada)
    x2d = x.reshape(t, d)
    av, bw = A_HEADS * HEAD_DIM, B_HEADS * HEAD_DIM
    for l in range(depth):
        lam_init = 0.8 - 0.6 * math.exp(-0.3 * l)
        mod3 = mods[l].reshape(bsz, 1, N_MOD * d)
        wn, wt = _prep_in_weights(w_in[l])
        ka, kb, qc, vc2, qta, vta, qtb, vtb, ktc = _inproj(
            x2d, mod3, norm1_g[l].reshape(1, d), wn, wt, qa_bias, qb_bias, seq=seq)
        lamv = jnp.stack([lam_q1[l], lam_k1[l], lam_q2[l], lam_k2[l]]).astype(F32)
        sg = subln_g[l].reshape(HEAD_DIM, 1).astype(F32)
        oa = _diff_attention(qta, ka, vta, lamv, sg, bsz=bsz, seq=seq, lam_init=lam_init)
        ob = _moba_attention(qtb, kb, vtb, bsz=bsz, seq=seq)
        oc = _swa_attention(sinks[l].astype(F32), qc, ktc, vc2, bsz=bsz, seq=seq, slopes=sl_c)
        wo = w_out[l].astype(BF16)
        keys = peer_keys[l].reshape(2 * PEER_HEADS, PEER_NKEYS, -1).astype(BF16)
        x1, h2, st = _outproj(oa, ob, oc, wo[:av], wo[av:av + bw], wo[av + bw:], x2d, mod3,
                              norm2_g[l].reshape(1, d), peer_wq[l].astype(BF16), keys, seq=seq)
        assert seq % TOPK_TL == 0
        blocks = seq // TOPK_TL
        utab = _pack_expert_table(peer_u[l])
        vtab = _pack_expert_table(peer_v[l])
        h2t = h2.reshape(t, d // LANES, LANES)
        off_b, par_b, g_b = _peer_topk(st, 0, blocks)
        x2d = x1
        for b in range(bsz):
            w_b = _peer_u(off_b.reshape(-1), par_b, g_b, h2t, utab, b * seq)
            more = dict(st=st, st_blk0=(b + 1) * blocks) if b + 1 < bsz else {}
            x2d, *nxt = _peer_v(off_b.reshape(-1), par_b, w_b, x2d, mod3, final_g.reshape(1, d), vtab, b * seq,
                                seq=seq, final=(l == depth - 1), **more)
            if nxt:
                off_b, par_b, g_b = nxt
    return x2d.reshape(bsz, seq, d)
```

```python
import functools
import math

import numpy as np
import jax
import jax.numpy as jnp
from jax import lax
from jax.experimental import pallas as pl
from jax.experimental.pallas import tpu as pltpu

F32 = jnp.float32
BF16 = jnp.bfloat16
I32 = jnp.int32

D_MODEL = 1024
HEAD_DIM = 64
N_HEADS_TOTAL = 16
A_HEADS = 4
B_HEADS = 6
C_HEADS = 6
C_KV_HEADS = 2
C_GROUP = 3
A_QK_DIM = 32
MOBA_BLOCK = 256
MOBA_TOPK = 3
WINDOW = 128
ALIBI_MAX = 8.0
PEER_HEADS = 8
PEER_NKEYS = 128
PEER_TOPK = 16
N_MOD = 6
EPS = 1e-6

LANES = 128
KT = 256
AUG0 = HEAD_DIM
SEL0 = AUG0 + 6
MAX_BLOCKS = LANES - SEL0
NEG_BIG = -1e30
LOG2E = math.log2(math.e)
VMEM_LIMIT = 56 * 1024 * 1024


def _alibi_slopes():
    n = N_HEADS_TOTAL
    return (2.0 ** (-ALIBI_MAX * np.arange(1, n + 1, dtype=np.float32) / n)).astype(np.float32)


def _split3(v):
    v = np.float32(v)
    hi = np.float32(np.asarray(v).astype(jnp.bfloat16).astype(np.float32))
    r = np.float32(v - hi)
    mid = np.float32(np.asarray(r).astype(jnp.bfloat16).astype(np.float32))
    lo = np.float32(np.float32(r - mid))
    lo = np.float32(np.asarray(lo).astype(jnp.bfloat16).astype(np.float32))
    return hi, mid, lo


def _slope_bias_col(slopes):
    col = np.zeros((LANES * len(slopes), 1), np.float32)
    for h, s in enumerate(slopes):
        hi, mid, lo = _split3(s)
        col[h * LANES + AUG0:h * LANES + AUG0 + 6, 0] = [hi, mid, lo, hi, mid, lo]
    return jnp.asarray(col)


def _mod_kernel(c_ref, w_ref, b_ref, o_ref):
    c = c_ref[...]
    cs = c * (1.0 / (1.0 + jnp.exp(-c)))
    o_ref[0] = jnp.dot(cs, w_ref[0], preferred_element_type=F32) + b_ref[0]


def _adaln_mods(c, w_ada, b_ada):
    depth, d, n = w_ada.shape
    bsz = c.shape[0]
    rows = -(-bsz // 8) * 8
    cp = jnp.pad(c, ((0, rows - bsz), (0, 0)))
    tn = 1536
    out = pl.pallas_call(
        _mod_kernel,
        grid=(depth, n // tn),
        in_specs=[
            pl.BlockSpec((rows, d), lambda l, j: (0, 0)),
            pl.BlockSpec((1, d, tn), lambda l, j: (l, 0, j)),
            pl.BlockSpec((1, 1, tn), lambda l, j: (l, 0, j)),
        ],
        out_specs=pl.BlockSpec((1, rows, tn), lambda l, j: (l, 0, j)),
        out_shape=jax.ShapeDtypeStruct((depth, rows, n), F32),
        compiler_params=pltpu.CompilerParams(dimension_semantics=("arbitrary", "arbitrary"),
                                             vmem_limit_bytes=VMEM_LIMIT),
        name="adaln_mods",
    )(cp, w_ada, b_ada.reshape(depth, 1, n))
    return out[:, :bsz]


NN_WIDTHS = (A_HEADS * LANES, B_HEADS * LANES, C_HEADS * HEAD_DIM, 2 * C_KV_HEADS * HEAD_DIM)
NT_ROWS = (A_HEADS * LANES, A_HEADS * HEAD_DIM, B_HEADS * LANES, B_HEADS * HEAD_DIM,
           C_KV_HEADS * LANES)


def _prep_in_weights(w):
    d = w.shape[0]
    aq, ak, av = A_HEADS * 2 * A_QK_DIM, A_HEADS * 2 * A_QK_DIM, A_HEADS * HEAD_DIM
    bw = B_HEADS * HEAD_DIM
    cq, ckv = C_HEADS * HEAD_DIM, C_KV_HEADS * HEAD_DIM
    cuts = np.cumsum([aq, ak, av, bw, bw, bw, cq, ckv]).tolist()
    qa, ka, va, qb, kb, vb, qc, kc, vc = jnp.split(w, cuts, axis=-1)

    def pad_heads(m, nh, scale):
        m = (m * scale).reshape(d, nh, HEAD_DIM)
        return jnp.pad(m, ((0, 0), (0, 0), (0, LANES - HEAD_DIM))).reshape(d, nh * LANES)

    vc2 = vc.reshape(d, C_KV_HEADS, 1, HEAD_DIM)
    vc2 = jnp.broadcast_to(vc2, (d, C_KV_HEADS, 2, HEAD_DIM)).reshape(d, 2 * ckv)
    kc2 = jnp.broadcast_to(kc.reshape(d, C_KV_HEADS, 1, HEAD_DIM), (d, C_KV_HEADS, 2, HEAD_DIM)).reshape(d, 2 * ckv)
    wn = jnp.concatenate([pad_heads(ka, A_HEADS, 1.0), pad_heads(kb, B_HEADS, 1.0),
                          qc * (HEAD_DIM ** -0.5), vc2], axis=1)
    wt = jnp.concatenate([pad_heads(qa, A_HEADS, A_QK_DIM ** -0.5 * LOG2E), va,
                          pad_heads(qb, B_HEADS, HEAD_DIM ** -0.5 * LOG2E), vb, kc2], axis=1).T
    return wn.astype(BF16), wt.astype(BF16)


def _rms_mod(x, g, sc, sh):
    ms = jnp.mean(x * x, axis=-1, keepdims=True)
    return (x * lax.rsqrt(ms + EPS) * g) * (1.0 + sc) + sh


def _inproj_kernel(x_ref, mod_ref, g_ref, wn_ref, wt_ref, qab_ref, qbb_ref,
                   ka_ref, kb_ref, qc_ref, vc_ref, qta_ref, vta_ref, qtb_ref, vtb_ref, ktc_ref,
                   *, tm, seq):
    d = x_ref.shape[1]
    x = x_ref[...]
    sh = mod_ref[0, :, 0:d]
    sc = mod_ref[0, :, d:2 * d]
    h = _rms_mod(x, g_ref[...], sc, sh).astype(BF16)
    pn = jnp.dot(h, wn_ref[...], preferred_element_type=F32)
    pt = lax.dot_general(wt_ref[...], h, (((1,), (1,)), ((), ())),
                         preferred_element_type=F32)

    pos = (pl.program_id(0) * tm) % seq + lax.broadcasted_iota(I32, (tm, LANES), 0)
    col = lax.broadcasted_iota(I32, (tm, LANES), 1)
    blk_id = pos // KT
    p_hi = (blk_id * KT).astype(F32)
    p_lo = (pos - blk_id * KT).astype(F32)
    aug_a = jnp.where((col >= AUG0) & (col < AUG0 + 3), p_hi,
                      jnp.where((col >= AUG0 + 3) & (col < AUG0 + 6), p_lo, 0.0))
    aug_b = jnp.where((col >= SEL0) & (col - SEL0 == blk_id), 1.0, aug_a)
    for hh in range(A_HEADS):
        ka_ref[:, hh * LANES:(hh + 1) * LANES] = (pn[:, hh * LANES:(hh + 1) * LANES] + aug_a).astype(BF16)
    o = NN_WIDTHS[0]
    for hh in range(B_HEADS):
        kb_ref[:, hh * LANES:(hh + 1) * LANES] = (pn[:, o + hh * LANES:o + (hh + 1) * LANES] + aug_b).astype(BF16)
    o += NN_WIDTHS[1]
    qc_ref[...] = pn[:, o:o + NN_WIDTHS[2]].astype(BF16)
    o += NN_WIDTHS[2]
    vc_ref[...] = pn[:, o:o + NN_WIDTHS[3]].astype(BF16)

    r0 = 0
    for ref, b, nr in zip((qta_ref, vta_ref, qtb_ref, vtb_ref, ktc_ref),
                          (qab_ref, None, qbb_ref, None, None), NT_ROWS):
        blk = pt[r0:r0 + nr, :]
        if b is not None:
            blk = blk + b[...]
        blk = blk.astype(BF16)
        for cc in range(tm // KT):
            ref[cc] = blk[:, cc * KT:(cc + 1) * KT]
        r0 += nr


def _inproj(x2d, mod3, g, wn, wt, qa_bias, qb_bias, *, seq, tm=512):
    t, d = x2d.shape
    assert seq % tm == 0 and tm % KT == 0
    nt = t // tm
    nn_total = sum(NN_WIDTHS)
    row_specs = [pl.BlockSpec((tm, wd), lambda i: (i, 0)) for wd in NN_WIDTHS]
    kt_specs = [pl.BlockSpec((tm // KT, r, KT), lambda i: (i, 0, 0)) for r in NT_ROWS]
    out_shape = ([jax.ShapeDtypeStruct((t, wd), BF16) for wd in NN_WIDTHS]
                 + [jax.ShapeDtypeStruct((t // KT, r, KT), BF16) for r in NT_ROWS])
    return pl.pallas_call(
        functools.partial(_inproj_kernel, tm=tm, seq=seq),
        grid=(nt,),
        in_specs=[
            pl.BlockSpec((tm, d), lambda i: (i, 0)),
            pl.BlockSpec((1, 1, mod3.shape[2]), lambda i: ((i * tm) // seq, 0, 0)),
            pl.BlockSpec((1, d), lambda i: (0, 0)),
            pl.BlockSpec((d, nn_total), lambda i: (0, 0)),
            pl.BlockSpec((sum(NT_ROWS), d), lambda i: (0, 0)),
            pl.BlockSpec((NT_ROWS[0], 1), lambda i: (0, 0)),
            pl.BlockSpec((NT_ROWS[2], 1), lambda i: (0, 0)),
        ],
        out_specs=row_specs + kt_specs,
        out_shape=out_shape,
        compiler_params=pltpu.CompilerParams(dimension_semantics=("arbitrary",),
                                             vmem_limit_bytes=VMEM_LIMIT),
        name="inproj",
    )(x2d, mod3, g, wn, wt, qa_bias, qb_bias)


ACC_ROWS = HEAD_DIM + 16


def _softmax_pv(ss, vt1s, m_ref, acc_ref):
    nh = len(ss)
    m_prev = [m_ref[hh] for hh in range(nh)]
    m_new = [jnp.maximum(m_prev[hh], jnp.max(ss[hh], axis=0, keepdims=True)) for hh in range(nh)]
    ps = [jnp.exp2((ss[hh] - m_new[hh]).astype(BF16)) for hh in range(nh)]
    pv = [jnp.dot(vt1s[hh], ps[hh], preferred_element_type=F32) for hh in range(nh)]
    for hh in range(nh):
        acc_ref[hh] = jnp.exp2(m_prev[hh] - m_new[hh]) * acc_ref[hh] + pv[hh]
        m_ref[hh] = m_new[hh]


def _flash_init(m_ref, acc_ref):
    m_ref[...] = jnp.full(m_ref.shape, -jnp.inf, F32)
    acc_ref[...] = jnp.zeros(acc_ref.shape, F32)


def _flash_finish(acc_ref, hh):
    acc = acc_ref[hh]
    return acc[0:HEAD_DIM] / acc[HEAD_DIM:HEAD_DIM + 1]


PAST_TILES = 4
SERIAL_TILES = 2


def _flash_causal_pair(k_ref, vt_ref, q_diag, q_past, qi, causal, m_ref, acc_ref, s_ref=None):
    _flash_init(m_ref, acc_ref)
    nkeys = PAST_TILES * KT
    nh = len(q_past)

    def vt1(kj, n, hh):
        rows = slice(hh * HEAD_DIM, (hh + 1) * HEAD_DIM)
        vt = vt_ref[kj, rows, :] if n == 1 else jnp.concatenate([vt_ref[kj + c, rows, :] for c in range(n)], axis=1)
        return jnp.concatenate([vt, jnp.ones((ACC_ROWS - HEAD_DIM, n * KT), BF16)], axis=0)

    def scores(kj, n, qs, hh):
        kk = k_ref[pl.ds(pl.multiple_of(kj * KT, KT), n * KT), hh * LANES:(hh + 1) * LANES]
        return jnp.dot(kk, qs[hh], preferred_element_type=F32)

    def step(kj, n, qs, mask=None):
        ss = [scores(kj, n, qs, hh) for hh in range(nh)]
        if mask is not None:
            ss = [jnp.where(mask, s, -jnp.inf) for s in ss]
        _softmax_pv(ss, [vt1(kj, n, hh) for hh in range(nh)], m_ref, acc_ref)

    step(qi, 1, q_diag, causal)

    if s_ref is None:
        def body(j, carry):
            step(j * SERIAL_TILES, SERIAL_TILES, q_past)
            return carry

        lax.fori_loop(0, qi // SERIAL_TILES, body, 0)
        for r in range(SERIAL_TILES - 1):
            @pl.when(qi % SERIAL_TILES > r)
            def _():
                step(qi - 1 - r, 1, q_past)
        return

    ngroups = k_ref.shape[0] // nkeys

    def scores_into(slot, g):
        for hh in range(nh):
            s_ref[slot, hh] = scores(g * PAST_TILES, PAST_TILES, q_past, hh)

    scores_into(0, 0)

    def body(g, carry):
        scores_into((g + 1) % 2, jnp.minimum(g + 1, ngroups - 1))
        _softmax_pv([s_ref[g % 2, hh] for hh in range(nh)],
                    [vt1(g * PAST_TILES, PAST_TILES, hh) for hh in range(nh)], m_ref, acc_ref)
        return carry

    lax.fori_loop(0, (qi + PAST_TILES - 1) // PAST_TILES, body, 0)


def _diff_kernel(qt_ref, k_ref, vt_ref, lamv_ref, sg_ref, o_ref, m_ref, acc_ref, *, lam_init):
    tq = qt_ref.shape[2]
    qi = pl.program_id(2)
    lv = lamv_ref[...]
    lam = (jnp.exp(jnp.sum(lv[0:1] * lv[1:2], axis=-1, keepdims=True))
           - jnp.exp(jnp.sum(lv[2:3] * lv[3:4], axis=-1, keepdims=True)) + lam_init)
    row = lax.broadcasted_iota(I32, (LANES, tq), 0)
    kr = lax.broadcasted_iota(I32, (KT, 2 * tq), 0)
    qc = lax.broadcasted_iota(I32, (KT, 2 * tq), 1)
    causal = kr <= jnp.where(qc >= tq, qc - tq, qc)
    nh = qt_ref.shape[1] // LANES
    qs = []
    for hh in range(nh):
        qt = qt_ref[0, hh * LANES:(hh + 1) * LANES, :]
        zero = jnp.zeros_like(qt)
        q1 = jnp.where((row < A_QK_DIM) | (row >= AUG0), qt, zero)
        q2 = jnp.where(row >= A_QK_DIM, qt, zero)
        qs.append(jnp.concatenate([q1, q2], axis=1))
    _flash_causal_pair(k_ref, vt_ref, qs, qs, qi, causal, m_ref, acc_ref)
    res = []
    for hh in range(nh):
        o = _flash_finish(acc_ref, hh)
        od = o[:, :tq] - lam * o[:, tq:]
        ms = jnp.mean(od * od, axis=0, keepdims=True)
        res.append(od * lax.rsqrt(ms + EPS) * sg_ref[...] * (1.0 - lam_init))
    o_ref[...] = jnp.concatenate(res, axis=0).T.astype(BF16)


def _diff_attention(qta, ka, vta, lamv, sg, *, bsz, seq, lam_init, nh=A_HEADS):
    tq = KT
    nq = seq // tq
    t = bsz * seq
    return pl.pallas_call(
        functools.partial(_diff_kernel, lam_init=lam_init),
        grid=(bsz, A_HEADS // nh, nq),
        in_specs=[
            pl.BlockSpec((1, nh * LANES, tq), lambda b, hp, qi: (b * nq + qi, hp, 0)),
            pl.BlockSpec((seq, nh * LANES), lambda b, hp, qi: (b, hp)),
            pl.BlockSpec((nq, nh * HEAD_DIM, KT), lambda b, hp, qi: (b, hp, 0)),
            pl.BlockSpec((4, A_QK_DIM), lambda b, hp, qi: (0, 0)),
            pl.BlockSpec((HEAD_DIM, 1), lambda b, hp, qi: (0, 0)),
        ],
        out_specs=pl.BlockSpec((tq, nh * HEAD_DIM), lambda b, hp, qi: (b * nq + qi, hp)),
        out_shape=jax.ShapeDtypeStruct((t, A_HEADS * HEAD_DIM), BF16),
        scratch_shapes=[pltpu.VMEM((nh, 1, 2 * tq), F32), pltpu.VMEM((nh, ACC_ROWS, 2 * tq), F32)],
        compiler_params=pltpu.CompilerParams(dimension_semantics=("arbitrary",) * 3,
                                             vmem_limit_bytes=VMEM_LIMIT),
        name="diff_attn",
    )(qta, ka, vta, lamv, sg)


def _moba_kernel(qt_ref, k_ref, vt_ref, o_ref, m_ref, acc_ref, km_ref, s_ref):
    tq = qt_ref.shape[2]
    nh = qt_ref.shape[1] // LANES
    nb = vt_ref.shape[0]
    qi = pl.program_id(2)

    @pl.when(qi == 0)
    def _():
        lane1 = lax.broadcasted_iota(I32, (1, LANES), 1)
        for hh in range(nh):
            km_ref[hh] = jnp.zeros((LANES, LANES), F32)

            def put_block(j, carry):
                blk = k_ref[pl.ds(pl.multiple_of(j * KT, KT), KT), hh * LANES:(hh + 1) * LANES].astype(F32)
                mean = jnp.sum(blk, axis=0, keepdims=True) * (1.0 / KT)
                km_ref[hh, pl.ds(SEL0 + j, 1), :] = jnp.where(lane1 < HEAD_DIM, mean, 0.0)
                return carry
            lax.fori_loop(0, nb, put_block, 0)

    kr = lax.broadcasted_iota(I32, (KT, tq), 0)
    qc = lax.broadcasted_iota(I32, (KT, tq), 1)
    causal = kr <= qc
    nsel = -(-(SEL0 - AUG0 + nb) // 16) * 16
    band = slice(AUG0, AUG0 + nsel)
    row = lax.broadcasted_iota(I32, (nsel, tq), 0) + AUG0
    in_sel = row >= SEL0
    qts = [qt_ref[0, hh * LANES:(hh + 1) * LANES, :] for hh in range(nh)]
    curs = []
    for hh in range(nh):
        km_hi, km_lo = _split2(km_ref[hh, band, :])
        gate = (jnp.dot(km_hi, qts[hh], preferred_element_type=F32)
                + jnp.dot(km_lo, qts[hh], preferred_element_type=F32))
        curs.append(jnp.where(in_sel & (row < SEL0 + qi), gate, -jnp.inf))
    sels = [jnp.zeros((nsel, tq), jnp.bool_)] * nh
    for _ in range(MOBA_TOPK):
        for hh in range(nh):
            mx = jnp.max(curs[hh], axis=0, keepdims=True)
            first = jnp.min(jnp.where(curs[hh] == mx, row, 4 * LANES), axis=0, keepdims=True)
            pick = (row == first) & (mx > -jnp.inf)
            sels[hh] = sels[hh] | pick
            curs[hh] = jnp.where(pick, -jnp.inf, curs[hh])
    q_diag, q_past = [], []
    for hh in range(nh):
        qt, mid = qts[hh], qts[hh][band]
        rest = [qt[AUG0 + nsel:]] if AUG0 + nsel < LANES else []
        q_diag.append(jnp.concatenate([qt[:AUG0], jnp.where(in_sel, jnp.zeros_like(mid), mid)] + rest, axis=0))
        bias = jnp.where(sels[hh], 0.0, NEG_BIG).astype(BF16)
        q_past.append(jnp.concatenate([qt[:AUG0], jnp.where(in_sel, bias, mid)] + rest, axis=0))
    _flash_causal_pair(k_ref, vt_ref, q_diag, q_past, qi, causal, m_ref, acc_ref, s_ref)
    res = [_flash_finish(acc_ref, hh) for hh in range(nh)]
    o_ref[...] = jnp.concatenate(res, axis=0).T.astype(BF16)


def _moba_attention(qtb, kb, vtb, *, bsz, seq, nh=B_HEADS):
    tq = KT
    nq = seq // tq
    assert nq <= MAX_BLOCKS and seq % (PAST_TILES * KT) == 0
    t = bsz * seq
    return pl.pallas_call(
        _moba_kernel,
        grid=(bsz, B_HEADS // nh, nq),
        in_specs=[
            pl.BlockSpec((1, nh * LANES, tq), lambda b, hp, qi: (b * nq + qi, hp, 0)),
            pl.BlockSpec((seq, nh * LANES), lambda b, hp, qi: (b, hp), pipeline_mode=pl.Buffered(1)),
            pl.BlockSpec((nq, nh * HEAD_DIM, KT), lambda b, hp, qi: (b, hp, 0), pipeline_mode=pl.Buffered(1)),
        ],
        out_specs=pl.BlockSpec((tq, nh * HEAD_DIM), lambda b, hp, qi: (b * nq + qi, hp)),
        out_shape=jax.ShapeDtypeStruct((t, B_HEADS * HEAD_DIM), BF16),
        scratch_shapes=[pltpu.VMEM((nh, 1, tq), F32), pltpu.VMEM((nh, ACC_ROWS, tq), F32),
                        pltpu.VMEM((nh, LANES, LANES), F32), pltpu.VMEM((2, nh, PAST_TILES * KT, tq), F32)],
        compiler_params=pltpu.CompilerParams(dimension_semantics=("arbitrary",) * 3,
                                             vmem_limit_bytes=VMEM_LIMIT),
        name="moba_attn",
    )(qtb, kb, vtb)


def _swa_kernel(sink_ref, q_ref, ktp_ref, ktc_ref, vp_ref, vc_ref, o_ref, *, slopes):
    w = WINDOW
    n = pl.program_id(1)
    lane = lax.broadcasted_iota(I32, (w, LANES), 1)
    r2 = lax.broadcasted_iota(I32, (w, 2 * w), 0)
    c2 = lax.broadcasted_iota(I32, (w, 2 * w), 1)
    rel = r2 + w - c2
    band = (rel >= 0) & (rel < w)
    relf = rel.astype(F32)
    for half in range(KT // w):
        mask = band & ((c2 >= w) | (n > 0)) if half == 0 else band
        rows = slice(half * w, (half + 1) * w)
        res = []
        for hq in range(C_HEADS):
            kv = hq // C_GROUP
            kvl = slice(kv * LANES, (kv + 1) * LANES)
            qp = q_ref[rows, (hq // 2) * LANES:(hq // 2 + 1) * LANES]
            qm = jnp.where((lane < HEAD_DIM) == (hq % 2 == 0), qp, jnp.zeros_like(qp))
            if half == 0:
                kt = jnp.concatenate([ktp_ref[0, kvl, KT - w:KT], ktc_ref[0, kvl, 0:w]], axis=1)
                vv = jnp.concatenate([vp_ref[KT - w:KT, kvl], vc_ref[0:w, kvl]], axis=0)
            else:
                kt = ktc_ref[0, kvl, (half - 1) * w:(half + 1) * w]
                vv = vc_ref[(half - 1) * w:(half + 1) * w, kvl]
            s = jnp.dot(qm, kt, preferred_element_type=F32)
            s = jnp.where(mask, s - slopes[hq] * relf, -jnp.inf)
            sink = sink_ref[hq]
            m = jnp.maximum(jnp.max(s, axis=-1, keepdims=True), sink)
            e = jnp.exp(s - m)
            den = jnp.sum(e, axis=-1, keepdims=True) + jnp.exp(sink - m)
            p = (e / den).astype(BF16)
            res.append(jnp.dot(p, vv, preferred_element_type=F32))
        for pr in range(C_HEADS // 2):
            o_ref[rows, pr * LANES:(pr + 1) * LANES] = jnp.where(lane < HEAD_DIM, res[2 * pr],
                                                                 res[2 * pr + 1]).astype(BF16)


def _swa_attention(sinks, qc, ktc, vc2, *, bsz, seq, slopes):
    tq = KT
    nq = seq // tq
    t = bsz * seq
    return pl.pallas_call(
        functools.partial(_swa_kernel, slopes=slopes),
        grid=(bsz, nq),
        in_specs=[
            pl.BlockSpec(memory_space=pltpu.SMEM),
            pl.BlockSpec((tq, C_HEADS * HEAD_DIM), lambda b, n: (b * nq + n, 0)),
            pl.BlockSpec((1, C_KV_HEADS * LANES, KT), lambda b, n: (b * nq + jnp.maximum(n - 1, 0), 0, 0)),
            pl.BlockSpec((1, C_KV_HEADS * LANES, KT), lambda b, n: (b * nq + n, 0, 0)),
            pl.BlockSpec((tq, C_KV_HEADS * LANES), lambda b, n: (b * nq + jnp.maximum(n - 1, 0), 0)),
            pl.BlockSpec((tq, C_KV_HEADS * LANES), lambda b, n: (b * nq + n, 0)),
        ],
        out_specs=pl.BlockSpec((tq, C_HEADS * HEAD_DIM), lambda b, n: (b * nq + n, 0)),
        out_shape=jax.ShapeDtypeStruct((t, C_HEADS * HEAD_DIM), BF16),
        compiler_params=pltpu.CompilerParams(dimension_semantics=("arbitrary",) * 2,
                                             vmem_limit_bytes=VMEM_LIMIT),
        name="swa_attn",
    )(sinks, qc, ktc, ktc, vc2, vc2)


def _outproj_kernel(oa_ref, ob_ref, oc_ref, woa_ref, wob_ref, woc_ref, x_ref, mod_ref, g_ref, wq_ref, keys_ref,
                    x1_ref, h2_ref, st_ref):
    d = x_ref.shape[1]
    mix = (jnp.dot(oa_ref[...], woa_ref[...], preferred_element_type=F32)
           + jnp.dot(ob_ref[...], wob_ref[...], preferred_element_type=F32)
           + jnp.dot(oc_ref[...], woc_ref[...], preferred_element_type=F32))
    g1 = mod_ref[0, :, 2 * d:3 * d]
    sh2 = mod_ref[0, :, 3 * d:4 * d]
    sc2 = mod_ref[0, :, 4 * d:5 * d]
    x1 = x_ref[...] + g1 * mix
    x1_ref[...] = x1
    h2 = _rms_mod(x1, g_ref[...], sc2, sh2)
    nj = d // LANES
    for j in range(nj):
        h2_ref[pl.ds(j, x1.shape[0], stride=nj), :] = h2[:, j * LANES:(j + 1) * LANES]
    pq = jnp.dot(h2.astype(BF16), wq_ref[...], preferred_element_type=F32).astype(BF16)
    for hp in range(2 * PEER_HEADS):
        st_ref[hp] = lax.dot_general(keys_ref[hp], pq[:, hp * LANES:(hp + 1) * LANES],
                                     (((1,), (1,)), ((), ())), preferred_element_type=F32)


def _outproj(oa, ob, oc, woa, wob, woc, x2d, mod3, g, wq, keys, *, seq, tm=256):
    t, d = x2d.shape
    nt = t // tm
    nq = wq.shape[1]
    full = lambda a: pl.BlockSpec(a.shape, lambda i: (0,) * a.ndim)
    return pl.pallas_call(
        _outproj_kernel,
        grid=(nt,),
        in_specs=[
            pl.BlockSpec((tm, oa.shape[1]), lambda i: (i, 0)),
            pl.BlockSpec((tm, ob.shape[1]), lambda i: (i, 0)),
            pl.BlockSpec((tm, oc.shape[1]), lambda i: (i, 0)),
            full(woa), full(wob), full(woc),
            pl.BlockSpec((tm, d), lambda i: (i, 0)),
            pl.BlockSpec((1, 1, mod3.shape[2]), lambda i: ((i * tm) // seq, 0, 0)),
            pl.BlockSpec((1, d), lambda i: (0, 0)),
            full(wq), full(keys),
        ],
        out_specs=[pl.BlockSpec((tm, d), lambda i: (i, 0)),
                   pl.BlockSpec((tm * (d // LANES), LANES), lambda i: (i, 0)),
                   pl.BlockSpec((2 * PEER_HEADS, PEER_NKEYS, tm), lambda i: (0, 0, i))],
        out_shape=[jax.ShapeDtypeStruct((t, d), F32), jax.ShapeDtypeStruct((t * (d // LANES), LANES), F32),
                   jax.ShapeDtypeStruct((2 * PEER_HEADS, PEER_NKEYS, t), F32)],
        compiler_params=pltpu.CompilerParams(dimension_semantics=("arbitrary",),
                                             vmem_limit_bytes=VMEM_LIMIT),
        name="outproj_peerq",
    )(oa, ob, oc, woa, wob, woc, x2d, mod3, g, wq, keys)


_CAND_BLOCKS = ((0, 16),) + tuple((i, 8) for i in range(1, 8))
_CAND_ROWS = 16 + 7 * 8 + 8
_BIG_I = np.int32(2 ** 30)


TOPK_TL = LANES


def _topk_rows(curs, keys, val_refs, key_refs, unroll):
    def body(r, curs):
        out = []
        for cur, key, val_ref, key_ref in zip(curs, keys, val_refs, key_refs):
            m = jnp.max(cur, axis=0, keepdims=True)
            kmin = jnp.min(jnp.where(cur == m, key, _BIG_I), axis=0, keepdims=True)
            val_ref[pl.ds(r, 1), :] = m
            key_ref[pl.ds(r, 1), :] = kmin
            out.append(jnp.where(key == kmin, -jnp.inf, cur))
        return tuple(out)
    lax.fori_loop(0, PEER_TOPK, body, tuple(curs), unroll=unroll)


def _topk_scratch():
    pair = [pltpu.VMEM((2, PEER_TOPK, TOPK_TL), F32), pltpu.VMEM((2, PEER_TOPK, TOPK_TL), I32)]
    hk = PEER_HEADS * PEER_TOPK
    return pair * 3 + [pltpu.VMEM((hk, TOPK_TL), I32), pltpu.VMEM((hk, TOPK_TL), F32)]


def _product_candidates(sv1, si1, sv2, si2):
    tl = sv1.shape[1]
    jrow8 = lax.broadcasted_iota(I32, (8, tl), 0)
    jrow16 = lax.broadcasted_iota(I32, (16, tl), 0)
    cands, keys = [], []
    for i, nj in _CAND_BLOCKS:
        jrow = jrow16 if nj == 16 else jrow8
        cands.append(sv1[i:i + 1] + sv2[0:nj])
        keys.append((i * PEER_TOPK + jrow) * (PEER_NKEYS * PEER_NKEYS) + si1[i:i + 1] * PEER_NKEYS + si2[0:nj])
    cands.append(sv1[8:16] + sv2[0:1])
    keys.append((jrow8 + 8) * (PEER_TOPK * PEER_NKEYS * PEER_NKEYS) + si1[8:16] * PEER_NKEYS + si2[0:1])
    return jnp.concatenate(cands, axis=0), jnp.concatenate(keys, axis=0)


def _store_gates(h, top, keys, ei_ref, gg_ref):
    rows = pl.ds(pl.multiple_of(h * PEER_TOPK, PEER_TOPK), PEER_TOPK)
    e = jnp.exp(top - top[0:1])
    gg_ref[rows, :] = e / jnp.sum(e, axis=0, keepdims=True)
    ei_ref[rows, :] = keys & (PEER_NKEYS * PEER_NKEYS - 1)


def _topk_head_pair(st_ref, h0, scratch, unroll=False):
    v1_ref, k1_ref, v2_ref, k2_ref, vt_ref, kt_ref, ei_ref, gg_ref = scratch
    row = lax.broadcasted_iota(I32, (PEER_NKEYS, st_ref.shape[2]), 0)
    cand, ckey = [], []
    for s in range(2):
        h = h0 + s
        _topk_rows([st_ref[2 * h], st_ref[2 * h + 1]], [row, row],
                   [v1_ref.at[s], v2_ref.at[s]], [k1_ref.at[s], k2_ref.at[s]], unroll)
        c, k = _product_candidates(v1_ref[s], k1_ref[s], v2_ref[s], k2_ref[s])
        cand.append(c)
        ckey.append(k)
    _topk_rows(cand, ckey, [vt_ref.at[0], vt_ref.at[1]], [kt_ref.at[0], kt_ref.at[1]], unroll)
    for s in range(2):
        _store_gates(h0 + s, vt_ref[s], kt_ref[s], ei_ref, gg_ref)


def _topk_emit(scratch, off_ref, par_ref, g_ref):
    ei = scratch[6][...].T
    off_ref[...] = (ei & (PEER_HALF - 1)) * SUB
    par_ref[...] = ei // PEER_HALF
    g_ref[...] = scratch[7][...].T


def _peer_topk_kernel(st_ref, off_ref, par_ref, g_ref, *scratch):
    for h0 in range(0, PEER_HEADS, 2):
        _topk_head_pair(st_ref, h0, scratch)
    _topk_emit(scratch, off_ref, par_ref, g_ref)


def _peer_topk(st, blk0, nblk):
    hk = PEER_HEADS * PEER_TOPK
    t = nblk * TOPK_TL
    return pl.pallas_call(
        _peer_topk_kernel,
        grid=(nblk,),
        in_specs=[pl.BlockSpec((2 * PEER_HEADS, PEER_NKEYS, TOPK_TL), lambda i: (0, 0, blk0 + i))],
        out_specs=[pl.BlockSpec((TOPK_TL, hk), lambda i: (i, 0))] * 3,
        out_shape=[jax.ShapeDtypeStruct((t, hk), I32), jax.ShapeDtypeStruct((t, hk), I32),
                   jax.ShapeDtypeStruct((t, hk), F32)],
        scratch_shapes=_topk_scratch(),
        compiler_params=pltpu.CompilerParams(dimension_semantics=("arbitrary",),
                                             vmem_limit_bytes=VMEM_LIMIT),
        name="peer_topk",
    )(st)


SUB = 8


def _erf(x):
    return lax.erf(x)


HALF_HI = 0xFFFF0000


PEER_HALF = PEER_NKEYS * PEER_NKEYS // 2


def _pack_kernel(lo_ref, hi_ref, o_ref):
    pairs = lo_ref.shape[0]
    nj = lo_ref.shape[1] // LANES
    for j in range(nj):
        cols = slice(j * LANES, (j + 1) * LANES)
        lo = pltpu.bitcast(lo_ref[:, cols].astype(BF16).astype(F32), jnp.uint32)
        hi = pltpu.bitcast(hi_ref[:, cols].astype(BF16).astype(F32), jnp.uint32)
        o_ref[pl.ds(j, pairs, stride=nj), :] = (lo >> 16) | (hi & jnp.uint32(HALF_HI))


def _pack_expert_table(w, pairs=256):
    n, d = w.shape
    nj = d // LANES
    half_blocks = n // 2 // pairs
    return pl.pallas_call(
        _pack_kernel,
        grid=(half_blocks,),
        in_specs=[pl.BlockSpec((pairs, d), lambda i: (i, 0)),
                  pl.BlockSpec((pairs, d), lambda i: (half_blocks + i, 0))],
        out_specs=pl.BlockSpec((pairs * nj, LANES), lambda i: (i, 0)),
        out_shape=jax.ShapeDtypeStruct((n // 2 * nj, LANES), jnp.uint32),
        compiler_params=pltpu.CompilerParams(dimension_semantics=("arbitrary",),
                                             vmem_limit_bytes=VMEM_LIMIT),
        name="pack_table",
    )(w, w)


def _load_table_once(tab_hbm, tab, sem):
    @pl.when(pl.program_id(0) == 0)
    def _():
        cp = pltpu.make_async_copy(tab_hbm, tab, sem)
        cp.start()
        cp.wait()


def _pair_tile(tab, off):
    return tab[pl.ds(pl.multiple_of(off, SUB), SUB), :]


def _group_matrix(rows, cols):
    r = lax.broadcasted_iota(I32, (rows, cols), 0)
    c = lax.broadcasted_iota(I32, (rows, cols), 1)
    return (c // (cols // rows) == r).astype(BF16)


def _split2(x):
    hi = x.astype(BF16)
    return hi, (x - hi.astype(F32)).astype(BF16)


PAIR_ROWS = 2 * SUB


def _pair_rows(tab, off_ref, tt, hk):
    return jnp.concatenate([pltpu.bitcast(_pair_tile(tab, off_ref[tt * hk + k]), BF16) for k in range(hk)], axis=0)


def _own_sublane_mask(hk):
    srow = lax.broadcasted_iota(I32, (SUB, hk * PAIR_ROWS), 0)
    scol = lax.broadcasted_iota(I32, (SUB, hk * PAIR_ROWS), 1)
    return (scol % PAIR_ROWS) // 2 == srow


def _fold_matrix(hk):
    c = jnp.arange(hk * PAIR_ROWS)
    return jax.nn.one_hot((c % 2) * hk + c // PAIR_ROWS, 2 * hk, dtype=BF16)


def _peer_u_kernel(off_ref, par_ref, h_ref, g_ref, fold_ref, tab_hbm, w_ref, tab, sem, d_sc):
    tq, hk = g_ref.shape
    _load_table_once(tab_hbm, tab, sem)
    mine = _own_sublane_mask(hk)
    nt = (((1,), (1,)), ((), ()))
    for tt in range(tq):
        h_hi, h_lo = _split2(h_ref[tt])
        s = lax.dot_general(jnp.concatenate([h_hi, h_lo], axis=0), _pair_rows(tab, off_ref, tt, hk), nt,
                            preferred_element_type=F32)
        d = jnp.where(mine, s[0:SUB] + s[SUB:2 * SUB], 0.0)
        d_sc[tt:tt + 1, :] = jnp.sum(d, axis=0, keepdims=True)
    d_hi, d_lo = _split2(d_sc[...])
    a2 = jnp.dot(jnp.concatenate([d_hi, d_lo], axis=0), fold_ref[...], preferred_element_type=F32)
    a2 = a2[0:tq] + a2[tq:2 * tq]
    a = jnp.where(par_ref[...] == 0, a2[:, :hk], a2[:, hk:])
    w_ref[...] = g_ref[...] * (0.5 * a * (1.0 + _erf(a * (2.0 ** -0.5))))


U_TQ = 128
V_TQ = 64


def _peer_u(off_flat, par, g, h2, utab, tok0):
    t, hk = g.shape
    nj = h2.shape[1]
    tq = U_TQ
    assert t % tq == 0 and tok0 % tq == 0
    fold = _fold_matrix(hk)
    return pl.pallas_call(
        _peer_u_kernel,
        grid=(t // tq,),
        in_specs=[
            pl.BlockSpec((tq * hk,), lambda i: (i,), memory_space=pltpu.SMEM),
            pl.BlockSpec((tq, hk), lambda i: (i, 0)),
            pl.BlockSpec((tq, nj, LANES), lambda i: (tok0 // tq + i, 0, 0)),
            pl.BlockSpec((tq, hk), lambda i: (i, 0)),
            pl.BlockSpec(fold.shape, lambda i: (0, 0)),
            pl.BlockSpec(memory_space=pl.ANY),
        ],
        out_specs=pl.BlockSpec((tq, hk), lambda i: (i, 0)),
        out_shape=jax.ShapeDtypeStruct((t, hk), F32),
        scratch_shapes=[pltpu.VMEM(utab.shape, jnp.uint32), pltpu.SemaphoreType.DMA(()),
                        pltpu.VMEM((tq, hk * PAIR_ROWS), F32)],
        compiler_params=pltpu.CompilerParams(dimension_semantics=("arbitrary",),
                                             vmem_limit_bytes=VMEM_LIMIT),
        name="peer_u",
    )(off_flat, par, h2, g, fold, utab)


def _peer_v_body(off_ref, par_ref, w_ref, x1_ref, mod_ref, fg_ref, o_ref, tab, wl_sc, p_sc, final):
    tq, hk = w_ref.shape
    d = x1_ref.shape[1]
    nj = d // LANES
    width = hk * PAIR_ROWS
    rep = _group_matrix(hk, width)
    w_hi, w_lo = _split2(w_ref[...])
    parl = jnp.dot(par_ref[...].astype(BF16), rep, preferred_element_type=F32)
    lane = lax.broadcasted_iota(I32, (tq, width), 1)
    wanted = (lane % 2).astype(F32) == parl
    wl_sc[0] = jnp.where(wanted, jnp.dot(w_hi, rep, preferred_element_type=F32), 0.0)
    wl_sc[1] = jnp.where(wanted, jnp.dot(w_lo, rep, preferred_element_type=F32), 0.0)
    mine = _own_sublane_mask(hk)
    for tt in range(tq):
        lhs = jnp.concatenate([jnp.where(mine, wl_sc[0, tt:tt + 1, :], 0.0),
                               jnp.where(mine, wl_sc[1, tt:tt + 1, :], 0.0)], axis=0).astype(BF16)
        r = jnp.dot(lhs, _pair_rows(tab, off_ref, tt, hk), preferred_element_type=F32)
        p_sc[tt * SUB:(tt + 1) * SUB, :] = r[0:SUB] + r[SUB:2 * SUB]
    peer = jnp.concatenate([p_sc[pl.ds(j, tq, stride=nj), :] for j in range(nj)], axis=1)
    y = x1_ref[...] + mod_ref[0, :, 5 * d:6 * d] * peer
    if final:
        y = y * lax.rsqrt(jnp.mean(y * y, axis=-1, keepdims=True) + EPS) * fg_ref[...]
    o_ref[...] = y


def _peer_v_kernel(off_ref, par_ref, w_ref, x1_ref, mod_ref, fg_ref, tab_hbm, o_ref, tab, sem, wl_sc, p_sc, *, final):
    _load_table_once(tab_hbm, tab, sem)
    _peer_v_body(off_ref, par_ref, w_ref, x1_ref, mod_ref, fg_ref, o_ref, tab, wl_sc, p_sc, final)


def _peer_v_topk_kernel(off_ref, par_ref, w_ref, x1_ref, mod_ref, fg_ref, st_ref, tab_hbm,
                        o_ref, off2_ref, par2_ref, g2_ref, tab, sem, wl_sc, p_sc, *scratch, final):
    steps = TOPK_TL // w_ref.shape[0]
    pairs = PEER_HEADS // 2 // steps
    part = pl.program_id(0) % steps
    _load_table_once(tab_hbm, tab, sem)
    for q in range(pairs):
        _topk_head_pair(st_ref, (pairs * part + q) * 2, scratch, unroll=True)
    _peer_v_body(off_ref, par_ref, w_ref, x1_ref, mod_ref, fg_ref, o_ref, tab, wl_sc, p_sc, final)

    @pl.when(part == steps - 1)
    def _():
        _topk_emit(scratch, off2_ref, par2_ref, g2_ref)


def _peer_v(off_flat, par, w, x1, mod3, final_g, vtab, tok0, *, seq, final, st=None, st_blk0=None):
    t, hk = w.shape
    d = x1.shape[1]
    tq = V_TQ
    steps = TOPK_TL // tq
    assert t % TOPK_TL == 0 and steps * tq == TOPK_TL and (PEER_HEADS // 2) % steps == 0
    assert tok0 % tq == 0 and tok0 // seq == (tok0 + t - 1) // seq
    in_specs = [
        pl.BlockSpec((tq * hk,), lambda i: (i,), memory_space=pltpu.SMEM),
        pl.BlockSpec((tq, hk), lambda i: (i, 0)),
        pl.BlockSpec((tq, hk), lambda i: (i, 0)),
        pl.BlockSpec((tq, d), lambda i: (tok0 // tq + i, 0)),
        pl.BlockSpec((1, 1, mod3.shape[2]), lambda i: (tok0 // seq, 0, 0)),
        pl.BlockSpec((1, d), lambda i: (0, 0)),
    ]
    out_specs = [pl.BlockSpec((tq, d), lambda i: (tok0 // tq + i, 0))]
    out_shape = [jax.ShapeDtypeStruct(x1.shape, F32)]
    scratch = [pltpu.VMEM(vtab.shape, jnp.uint32), pltpu.SemaphoreType.DMA(()),
               pltpu.VMEM((2, tq, hk * PAIR_ROWS), F32), pltpu.VMEM((tq * (d // LANES), LANES), F32)]
    args = [off_flat, par, w, x1, mod3, final_g]
    if st is not None:
        in_specs.append(pl.BlockSpec((2 * PEER_HEADS, PEER_NKEYS, TOPK_TL), lambda i: (0, 0, st_blk0 + i // steps)))
        out_specs += [pl.BlockSpec((TOPK_TL, hk), lambda i: (i // steps, 0))] * 3
        out_shape += [jax.ShapeDtypeStruct((t, hk), I32), jax.ShapeDtypeStruct((t, hk), I32),
                      jax.ShapeDtypeStruct((t, hk), F32)]
        scratch += _topk_scratch()
        args.append(st)
    in_specs.append(pl.BlockSpec(memory_space=pl.ANY))
    args.append(vtab)
    return pl.pallas_call(
        functools.partial(_peer_v_kernel if st is None else _peer_v_topk_kernel, final=final),
        grid=(t // tq,),
        in_specs=in_specs,
        out_specs=out_specs,
        out_shape=out_shape,
        scratch_shapes=scratch,
        input_output_aliases={3: 0},
        compiler_params=pltpu.CompilerParams(dimension_semantics=("arbitrary",),
                                             vmem_limit_bytes=VMEM_LIMIT),
        name="peer_v" if st is None else "peer_v_topk",
    )(*args)


def kernel(x, c, norm1_g, norm2_g, w_ada, b_ada, w_in, w_out, lam_q1, lam_k1, lam_q2, lam_k2, subln_g, sinks,
           peer_wq, peer_keys, peer_u, peer_v, final_g):
    bsz, seq, d = x.shape
    depth = w_in.shape[0]
    t = bsz * seq
    slopes = _alibi_slopes()
    sl_c = [float(s) for s in slopes[:C_HEADS]]
    qa_bias = _slope_bias_col(slopes[C_HEADS:C_HEADS + A_HEADS] * np.float32(LOG2E))
    qb_bias = _slope_bias_col(slopes[C_HEADS + A_HEADS:] * np.float32(LOG2E))
    mods = _adaln_mods(c, w_ada, b_ada)
    x2d = x.reshape(t, d)
    av, bw = A_HEADS * HEAD_DIM, B_HEADS * HEAD_DIM
    for l in range(depth):
        lam_init = 0.8 - 0.6 * math.exp(-0.3 * l)
        mod3 = mods[l].reshape(bsz, 1, N_MOD * d)
        wn, wt = _prep_in_weights(w_in[l])
        ka, kb, qc, vc2, qta, vta, qtb, vtb, ktc = _inproj(
            x2d, mod3, norm1_g[l].reshape(1, d), wn, wt, qa_bias, qb_bias, seq=seq)
        lamv = jnp.stack([lam_q1[l], lam_k1[l], lam_q2[l], lam_k2[l]]).astype(F32)
        sg = subln_g[l].reshape(HEAD_DIM, 1).astype(F32)
        oa = _diff_attention(qta, ka, vta, lamv, sg, bsz=bsz, seq=seq, lam_init=lam_init)
        ob = _moba_attention(qtb, kb, vtb, bsz=bsz, seq=seq)
        oc = _swa_attention(sinks[l].astype(F32), qc, ktc, vc2, bsz=bsz, seq=seq, slopes=sl_c)
        wo = w_out[l].astype(BF16)
        keys = peer_keys[l].reshape(2 * PEER_HEADS, PEER_NKEYS, -1).astype(BF16)
        x1, h2, st = _outproj(oa, ob, oc, wo[:av], wo[av:av + bw], wo[av + bw:], x2d, mod3,
                              norm2_g[l].reshape(1, d), peer_wq[l].astype(BF16), keys, seq=seq)
        assert seq % TOPK_TL == 0
        blocks = seq // TOPK_TL
        utab = _pack_expert_table(peer_u[l])
        vtab = _pack_expert_table(peer_v[l])
        h2t = h2.reshape(t, d // LANES, LANES)
        off_b, par_b, g_b = _peer_topk(st, 0, blocks)
        x2d = x1
        for b in range(bsz):
            w_b = _peer_u(off_b.reshape(-1), par_b, g_b, h2t, utab, b * seq)
            more = dict(st=st, st_blk0=(b + 1) * blocks) if b + 1 < bsz else {}
            x2d, *nxt = _peer_v(off_b.reshape(-1), par_b, w_b, x2d, mod3, final_g.reshape(1, d), vtab, b * seq,
                                seq=seq, final=(l == depth - 1), **more)
            if nxt:
                off_b, par_b, g_b = nxt
    return x2d.reshape(bsz, seq, d)
```

```python
import functools
import math

import numpy as np
import jax
import jax.numpy as jnp
from jax import lax
from jax.experimental import pallas as pl
from jax.experimental.pallas import tpu as pltpu

F32 = jnp.float32
BF16 = jnp.bfloat16
I32 = jnp.int32

D_MODEL = 1024
HEAD_DIM = 64
N_HEADS_TOTAL = 16
A_HEADS = 4
B_HEADS = 6
C_HEADS = 6
C_KV_HEADS = 2
C_GROUP = 3
A_QK_DIM = 32
MOBA_BLOCK = 256
MOBA_TOPK = 3
WINDOW = 128
ALIBI_MAX = 8.0
PEER_HEADS = 8
PEER_NKEYS = 128
PEER_TOPK = 16
N_MOD = 6
EPS = 1e-6

LANES = 128
KT = 256
AUG0 = HEAD_DIM
SEL0 = AUG0 + 6
MAX_BLOCKS = LANES - SEL0
NEG_BIG = -1e30
LOG2E = math.log2(math.e)
VMEM_LIMIT = 56 * 1024 * 1024


def _alibi_slopes():
    n = N_HEADS_TOTAL
    return (2.0 ** (-ALIBI_MAX * np.arange(1, n + 1, dtype=np.float32) / n)).astype(np.float32)


def _split3(v):
    v = np.float32(v)
    hi = np.float32(np.asarray(v).astype(jnp.bfloat16).astype(np.float32))
    r = np.float32(v - hi)
    mid = np.float32(np.asarray(r).astype(jnp.bfloat16).astype(np.float32))
    lo = np.float32(np.float32(r - mid))
    lo = np.float32(np.asarray(lo).astype(jnp.bfloat16).astype(np.float32))
    return hi, mid, lo


def _slope_bias_col(slopes):
    col = np.zeros((LANES * len(slopes), 1), np.float32)
    for h, s in enumerate(slopes):
        hi, mid, lo = _split3(s)
        col[h * LANES + AUG0:h * LANES + AUG0 + 6, 0] = [hi, mid, lo, hi, mid, lo]
    return jnp.asarray(col)


def _mod_kernel(c_ref, w_ref, b_ref, o_ref):
    c = c_ref[...]
    cs = c * (1.0 / (1.0 + jnp.exp(-c)))
    o_ref[0] = jnp.dot(cs, w_ref[0], preferred_element_type=F32) + b_ref[0]


def _adaln_mods(c, w_ada, b_ada):
    depth, d, n = w_ada.shape
    bsz = c.shape[0]
    rows = -(-bsz // 8) * 8
    cp = jnp.pad(c, ((0, rows - bsz), (0, 0)))
    tn = 1536
    out = pl.pallas_call(
        _mod_kernel,
        grid=(depth, n // tn),
        in_specs=[
            pl.BlockSpec((rows, d), lambda l, j: (0, 0)),
            pl.BlockSpec((1, d, tn), lambda l, j: (l, 0, j)),
            pl.BlockSpec((1, 1, tn), lambda l, j: (l, 0, j)),
        ],
        out_specs=pl.BlockSpec((1, rows, tn), lambda l, j: (l, 0, j)),
        out_shape=jax.ShapeDtypeStruct((depth, rows, n), F32),
        compiler_params=pltpu.CompilerParams(dimension_semantics=("arbitrary", "arbitrary"),
                                             vmem_limit_bytes=VMEM_LIMIT),
        name="adaln_mods",
    )(cp, w_ada, b_ada.reshape(depth, 1, n))
    return out[:, :bsz]


NN_WIDTHS = (A_HEADS * LANES, B_HEADS * LANES, C_HEADS * HEAD_DIM, 2 * C_KV_HEADS * HEAD_DIM)
NT_ROWS = (A_HEADS * LANES, A_HEADS * HEAD_DIM, B_HEADS * LANES, B_HEADS * HEAD_DIM,
           C_KV_HEADS * LANES)


def _prep_in_weights(w):
    d = w.shape[0]
    aq, ak, av = A_HEADS * 2 * A_QK_DIM, A_HEADS * 2 * A_QK_DIM, A_HEADS * HEAD_DIM
    bw = B_HEADS * HEAD_DIM
    cq, ckv = C_HEADS * HEAD_DIM, C_KV_HEADS * HEAD_DIM
    cuts = np.cumsum([aq, ak, av, bw, bw, bw, cq, ckv]).tolist()
    qa, ka, va, qb, kb, vb, qc, kc, vc = jnp.split(w, cuts, axis=-1)

    def pad_heads(m, nh, scale):
        m = (m * scale).reshape(d, nh, HEAD_DIM)
        return jnp.pad(m, ((0, 0), (0, 0), (0, LANES - HEAD_DIM))).reshape(d, nh * LANES)

    vc2 = vc.reshape(d, C_KV_HEADS, 1, HEAD_DIM)
    vc2 = jnp.broadcast_to(vc2, (d, C_KV_HEADS, 2, HEAD_DIM)).reshape(d, 2 * ckv)
    kc2 = jnp.broadcast_to(kc.reshape(d, C_KV_HEADS, 1, HEAD_DIM), (d, C_KV_HEADS, 2, HEAD_DIM)).reshape(d, 2 * ckv)
    wn = jnp.concatenate([pad_heads(ka, A_HEADS, 1.0), pad_heads(kb, B_HEADS, 1.0),
                          qc * (HEAD_DIM ** -0.5), vc2], axis=1)
    wt = jnp.concatenate([pad_heads(qa, A_HEADS, A_QK_DIM ** -0.5 * LOG2E), va,
                          pad_heads(qb, B_HEADS, HEAD_DIM ** -0.5 * LOG2E), vb, kc2], axis=1).T
    return wn.astype(BF16), wt.astype(BF16)


def _rms_mod(x, g, sc, sh):
    ms = jnp.mean(x * x, axis=-1, keepdims=True)
    return (x * lax.rsqrt(ms + EPS) * g) * (1.0 + sc) + sh


def _inproj_kernel(x_ref, mod_ref, g_ref, wn_ref, wt_ref, qab_ref, qbb_ref,
                   ka_ref, kb_ref, qc_ref, vc_ref, qta_ref, vta_ref, qtb_ref, vtb_ref, ktc_ref,
                   *, tm, seq):
    d = x_ref.shape[1]
    x = x_ref[...]
    sh = mod_ref[0, :, 0:d]
    sc = mod_ref[0, :, d:2 * d]
    h = _rms_mod(x, g_ref[...], sc, sh).astype(BF16)
    pn = jnp.dot(h, wn_ref[...], preferred_element_type=F32)
    pt = lax.dot_general(wt_ref[...], h, (((1,), (1,)), ((), ())),
                         preferred_element_type=F32)

    pos = (pl.program_id(0) * tm) % seq + lax.broadcasted_iota(I32, (tm, LANES), 0)
    col = lax.broadcasted_iota(I32, (tm, LANES), 1)
    blk_id = pos // KT
    p_hi = (blk_id * KT).astype(F32)
    p_lo = (pos - blk_id * KT).astype(F32)
    aug_a = jnp.where((col >= AUG0) & (col < AUG0 + 3), p_hi,
                      jnp.where((col >= AUG0 + 3) & (col < AUG0 + 6), p_lo, 0.0))
    aug_b = jnp.where((col >= SEL0) & (col - SEL0 == blk_id), 1.0, aug_a)
    for hh in range(A_HEADS):
        ka_ref[:, hh * LANES:(hh + 1) * LANES] = (pn[:, hh * LANES:(hh + 1) * LANES] + aug_a).astype(BF16)
    o = NN_WIDTHS[0]
    for hh in range(B_HEADS):
        kb_ref[:, hh * LANES:(hh + 1) * LANES] = (pn[:, o + hh * LANES:o + (hh + 1) * LANES] + aug_b).astype(BF16)
    o += NN_WIDTHS[1]
    qc_ref[...] = pn[:, o:o + NN_WIDTHS[2]].astype(BF16)
    o += NN_WIDTHS[2]
    vc_ref[...] = pn[:, o:o + NN_WIDTHS[3]].astype(BF16)

    r0 = 0
    for ref, b, nr in zip((qta_ref, vta_ref, qtb_ref, vtb_ref, ktc_ref),
                          (qab_ref, None, qbb_ref, None, None), NT_ROWS):
        blk = pt[r0:r0 + nr, :]
        if b is not None:
            blk = blk + b[...]
        blk = blk.astype(BF16)
        for cc in range(tm // KT):
            ref[cc] = blk[:, cc * KT:(cc + 1) * KT]
        r0 += nr


def _inproj(x2d, mod3, g, wn, wt, qa_bias, qb_bias, *, seq, tm=512):
    t, d = x2d.shape
    assert seq % tm == 0 and tm % KT == 0
    nt = t // tm
    nn_total = sum(NN_WIDTHS)
    row_specs = [pl.BlockSpec((tm, wd), lambda i: (i, 0)) for wd in NN_WIDTHS]
    kt_specs = [pl.BlockSpec((tm // KT, r, KT), lambda i: (i, 0, 0)) for r in NT_ROWS]
    out_shape = ([jax.ShapeDtypeStruct((t, wd), BF16) for wd in NN_WIDTHS]
                 + [jax.ShapeDtypeStruct((t // KT, r, KT), BF16) for r in NT_ROWS])
    return pl.pallas_call(
        functools.partial(_inproj_kernel, tm=tm, seq=seq),
        grid=(nt,),
        in_specs=[
            pl.BlockSpec((tm, d), lambda i: (i, 0)),
            pl.BlockSpec((1, 1, mod3.shape[2]), lambda i: ((i * tm) // seq, 0, 0)),
            pl.BlockSpec((1, d), lambda i: (0, 0)),
            pl.BlockSpec((d, nn_total), lambda i: (0, 0)),
            pl.BlockSpec((sum(NT_ROWS), d), lambda i: (0, 0)),
            pl.BlockSpec((NT_ROWS[0], 1), lambda i: (0, 0)),
            pl.BlockSpec((NT_ROWS[2], 1), lambda i: (0, 0)),
        ],
        out_specs=row_specs + kt_specs,
        out_shape=out_shape,
        compiler_params=pltpu.CompilerParams(dimension_semantics=("arbitrary",),
                                             vmem_limit_bytes=VMEM_LIMIT),
        name="inproj",
    )(x2d, mod3, g, wn, wt, qa_bias, qb_bias)


ACC_ROWS = HEAD_DIM + 16


def _softmax_pv(ss, vt1s, m_ref, acc_ref):
    nh = len(ss)
    m_prev = [m_ref[hh] for hh in range(nh)]
    m_new = [jnp.maximum(m_prev[hh], jnp.max(ss[hh], axis=0, keepdims=True)) for hh in range(nh)]
    ps = [jnp.exp2((ss[hh] - m_new[hh]).astype(BF16)) for hh in range(nh)]
    pv = [jnp.dot(vt1s[hh], ps[hh], preferred_element_type=F32) for hh in range(nh)]
    for hh in range(nh):
        acc_ref[hh] = jnp.exp2(m_prev[hh] - m_new[hh]) * acc_ref[hh] + pv[hh]
        m_ref[hh] = m_new[hh]


def _flash_init(m_ref, acc_ref):
    m_ref[...] = jnp.full(m_ref.shape, -jnp.inf, F32)
    acc_ref[...] = jnp.zeros(acc_ref.shape, F32)


def _flash_finish(acc_ref, hh):
    acc = acc_ref[hh]
    return acc[0:HEAD_DIM] / acc[HEAD_DIM:HEAD_DIM + 1]


PAST_TILES = 4
SERIAL_TILES = 2


def _flash_causal_pair(k_ref, vt_ref, q_diag, q_past, qi, causal, m_ref, acc_ref, s_ref=None):
    _flash_init(m_ref, acc_ref)
    nkeys = PAST_TILES * KT
    nh = len(q_past)

    def vt1(kj, n, hh):
        rows = slice(hh * HEAD_DIM, (hh + 1) * HEAD_DIM)
        vt = vt_ref[kj, rows, :] if n == 1 else jnp.concatenate([vt_ref[kj + c, rows, :] for c in range(n)], axis=1)
        return jnp.concatenate([vt, jnp.ones((ACC_ROWS - HEAD_DIM, n * KT), BF16)], axis=0)

    def scores(kj, n, qs, hh):
        kk = k_ref[pl.ds(pl.multiple_of(kj * KT, KT), n * KT), hh * LANES:(hh + 1) * LANES]
        return jnp.dot(kk, qs[hh], preferred_element_type=F32)

    def step(kj, n, qs, mask=None):
        ss = [scores(kj, n, qs, hh) for hh in range(nh)]
        if mask is not None:
            ss = [jnp.where(mask, s, -jnp.inf) for s in ss]
        _softmax_pv(ss, [vt1(kj, n, hh) for hh in range(nh)], m_ref, acc_ref)

    step(qi, 1, q_diag, causal)

    if s_ref is None:
        def body(j, carry):
            step(j * SERIAL_TILES, SERIAL_TILES, q_past)
            return carry

        lax.fori_loop(0, qi // SERIAL_TILES, body, 0)
        for r in range(SERIAL_TILES - 1):
            @pl.when(qi % SERIAL_TILES > r)
            def _():
                step(qi - 1 - r, 1, q_past)
        return

    ngroups = k_ref.shape[0] // nkeys

    def scores_into(slot, g):
        for hh in range(nh):
            s_ref[slot, hh] = scores(g * PAST_TILES, PAST_TILES, q_past, hh)

    scores_into(0, 0)

    def body(g, carry):
        scores_into((g + 1) % 2, jnp.minimum(g + 1, ngroups - 1))
        _softmax_pv([s_ref[g % 2, hh] for hh in range(nh)],
                    [vt1(g * PAST_TILES, PAST_TILES, hh) for hh in range(nh)], m_ref, acc_ref)
        return carry

    lax.fori_loop(0, (qi + PAST_TILES - 1) // PAST_TILES, body, 0)


def _diff_kernel(qt_ref, k_ref, vt_ref, lamv_ref, sg_ref, o_ref, m_ref, acc_ref, *, lam_init):
    tq = qt_ref.shape[2]
    qi = pl.program_id(2)
    lv = lamv_ref[...]
    lam = (jnp.exp(jnp.sum(lv[0:1] * lv[1:2], axis=-1, keepdims=True))
           - jnp.exp(jnp.sum(lv[2:3] * lv[3:4], axis=-1, keepdims=True)) + lam_init)
    row = lax.broadcasted_iota(I32, (LANES, tq), 0)
    kr = lax.broadcasted_iota(I32, (KT, 2 * tq), 0)
    qc = lax.broadcasted_iota(I32, (KT, 2 * tq), 1)
    causal = kr <= jnp.where(qc >= tq, qc - tq, qc)
    nh = qt_ref.shape[1] // LANES
    qs = []
    for hh in range(nh):
        qt = qt_ref[0, hh * LANES:(hh + 1) * LANES, :]
        zero = jnp.zeros_like(qt)
        q1 = jnp.where((row < A_QK_DIM) | (row >= AUG0), qt, zero)
        q2 = jnp.where(row >= A_QK_DIM, qt, zero)
        qs.append(jnp.concatenate([q1, q2], axis=1))
    _flash_causal_pair(k_ref, vt_ref, qs, qs, qi, causal, m_ref, acc_ref)
    res = []
    for hh in range(nh):
        o = _flash_finish(acc_ref, hh)
        od = o[:, :tq] - lam * o[:, tq:]
        ms = jnp.mean(od * od, axis=0, keepdims=True)
        res.append(od * lax.rsqrt(ms + EPS) * sg_ref[...] * (1.0 - lam_init))
    o_ref[...] = jnp.concatenate(res, axis=0).T.astype(BF16)


def _diff_attention(qta, ka, vta, lamv, sg, *, bsz, seq, lam_init, nh=A_HEADS):
    tq = KT
    nq = seq // tq
    t = bsz * seq
    return pl.pallas_call(
        functools.partial(_diff_kernel, lam_init=lam_init),
        grid=(bsz, A_HEADS // nh, nq),
        in_specs=[
            pl.BlockSpec((1, nh * LANES, tq), lambda b, hp, qi: (b * nq + qi, hp, 0)),
            pl.BlockSpec((seq, nh * LANES), lambda b, hp, qi: (b, hp)),
            pl.BlockSpec((nq, nh * HEAD_DIM, KT), lambda b, hp, qi: (b, hp, 0)),
            pl.BlockSpec((4, A_QK_DIM), lambda b, hp, qi: (0, 0)),
            pl.BlockSpec((HEAD_DIM, 1), lambda b, hp, qi: (0, 0)),
        ],
        out_specs=pl.BlockSpec((tq, nh * HEAD_DIM), lambda b, hp, qi: (b * nq + qi, hp)),
        out_shape=jax.ShapeDtypeStruct((t, A_HEADS * HEAD_DIM), BF16),
        scratch_shapes=[pltpu.VMEM((nh, 1, 2 * tq), F32), pltpu.VMEM((nh, ACC_ROWS, 2 * tq), F32)],
        compiler_params=pltpu.CompilerParams(dimension_semantics=("arbitrary",) * 3,
                                             vmem_limit_bytes=VMEM_LIMIT),
        name="diff_attn",
    )(qta, ka, vta, lamv, sg)


def _moba_kernel(qt_ref, k_ref, vt_ref, o_ref, m_ref, acc_ref, km_ref, s_ref):
    tq = qt_ref.shape[2]
    nh = qt_ref.shape[1] // LANES
    nb = vt_ref.shape[0]
    qi = pl.program_id(2)

    @pl.when(qi == 0)
    def _():
        lane1 = lax.broadcasted_iota(I32, (1, LANES), 1)
        for hh in range(nh):
            km_ref[hh] = jnp.zeros((LANES, LANES), F32)

            def put_block(j, carry):
                blk = k_ref[pl.ds(pl.multiple_of(j * KT, KT), KT), hh * LANES:(hh + 1) * LANES].astype(F32)
                mean = jnp.sum(blk, axis=0, keepdims=True) * (1.0 / KT)
                km_ref[hh, pl.ds(SEL0 + j, 1), :] = jnp.where(lane1 < HEAD_DIM, mean, 0.0)
                return carry
            lax.fori_loop(0, nb, put_block, 0)

    kr = lax.broadcasted_iota(I32, (KT, tq), 0)
    qc = lax.broadcasted_iota(I32, (KT, tq), 1)
    causal = kr <= qc
    nsel = -(-(SEL0 - AUG0 + nb) // 16) * 16
    band = slice(AUG0, AUG0 + nsel)
    row = lax.broadcasted_iota(I32, (nsel, tq), 0) + AUG0
    in_sel = row >= SEL0
    qts = [qt_ref[0, hh * LANES:(hh + 1) * LANES, :] for hh in range(nh)]
    curs = []
    for hh in range(nh):
        km_hi, km_lo = _split2(km_ref[hh, band, :])
        gate = (jnp.dot(km_hi, qts[hh], preferred_element_type=F32)
                + jnp.dot(km_lo, qts[hh], preferred_element_type=F32))
        curs.append(jnp.where(in_sel & (row < SEL0 + qi), gate, -jnp.inf))
    sels = [jnp.zeros((nsel, tq), jnp.bool_)] * nh
    for _ in range(MOBA_TOPK):
        for hh in range(nh):
            mx = jnp.max(curs[hh], axis=0, keepdims=True)
            first = jnp.min(jnp.where(curs[hh] == mx, row, 4 * LANES), axis=0, keepdims=True)
            pick = (row == first) & (mx > -jnp.inf)
            sels[hh] = sels[hh] | pick
            curs[hh] = jnp.where(pick, -jnp.inf, curs[hh])
    q_diag, q_past = [], []
    for hh in range(nh):
        qt, mid = qts[hh], qts[hh][band]
        rest = [qt[AUG0 + nsel:]] if AUG0 + nsel < LANES else []
        q_diag.append(jnp.concatenate([qt[:AUG0], jnp.where(in_sel, jnp.zeros_like(mid), mid)] + rest, axis=0))
        bias = jnp.where(sels[hh], 0.0, NEG_BIG).astype(BF16)
        q_past.append(jnp.concatenate([qt[:AUG0], jnp.where(in_sel, bias, mid)] + rest, axis=0))
    _flash_causal_pair(k_ref, vt_ref, q_diag, q_past, qi, causal, m_ref, acc_ref, s_ref)
    res = [_flash_finish(acc_ref, hh) for hh in range(nh)]
    o_ref[...] = jnp.concatenate(res, axis=0).T.astype(BF16)


def _moba_attention(qtb, kb, vtb, *, bsz, seq, nh=B_HEADS):
    tq = KT
    nq = seq // tq
    assert nq <= MAX_BLOCKS and seq % (PAST_TILES * KT) == 0
    t = bsz * seq
    return pl.pallas_call(
        _moba_kernel,
        grid=(bsz, B_HEADS // nh, nq),
        in_specs=[
            pl.BlockSpec((1, nh * LANES, tq), lambda b, hp, qi: (b * nq + qi, hp, 0)),
            pl.BlockSpec((seq, nh * LANES), lambda b, hp, qi: (b, hp), pipeline_mode=pl.Buffered(1)),
            pl.BlockSpec((nq, nh * HEAD_DIM, KT), lambda b, hp, qi: (b, hp, 0), pipeline_mode=pl.Buffered(1)),
        ],
        out_specs=pl.BlockSpec((tq, nh * HEAD_DIM), lambda b, hp, qi: (b * nq + qi, hp)),
        out_shape=jax.ShapeDtypeStruct((t, B_HEADS * HEAD_DIM), BF16),
        scratch_shapes=[pltpu.VMEM((nh, 1, tq), F32), pltpu.VMEM((nh, ACC_ROWS, tq), F32),
                        pltpu.VMEM((nh, LANES, LANES), F32), pltpu.VMEM((2, nh, PAST_TILES * KT, tq), F32)],
        compiler_params=pltpu.CompilerParams(dimension_semantics=("arbitrary",) * 3,
                                             vmem_limit_bytes=VMEM_LIMIT),
        name="moba_attn",
    )(qtb, kb, vtb)


def _swa_kernel(sink_ref, q_ref, ktp_ref, ktc_ref, vp_ref, vc_ref, o_ref, *, slopes):
    w = WINDOW
    n = pl.program_id(1)
    lane = lax.broadcasted_iota(I32, (w, LANES), 1)
    r2 = lax.broadcasted_iota(I32, (w, 2 * w), 0)
    c2 = lax.broadcasted_iota(I32, (w, 2 * w), 1)
    rel = r2 + w - c2
    band = (rel >= 0) & (rel < w)
    relf = rel.astype(F32)
    for half in range(KT // w):
        mask = band & ((c2 >= w) | (n > 0)) if half == 0 else band
        rows = slice(half * w, (half + 1) * w)
        res = []
        for hq in range(C_HEADS):
            kv = hq // C_GROUP
            kvl = slice(kv * LANES, (kv + 1) * LANES)
            qp = q_ref[rows, (hq // 2) * LANES:(hq // 2 + 1) * LANES]
            qm = jnp.where((lane < HEAD_DIM) == (hq % 2 == 0), qp, jnp.zeros_like(qp))
            if half == 0:
                kt = jnp.concatenate([ktp_ref[0, kvl, KT - w:KT], ktc_ref[0, kvl, 0:w]], axis=1)
                vv = jnp.concatenate([vp_ref[KT - w:KT, kvl], vc_ref[0:w, kvl]], axis=0)
            else:
                kt = ktc_ref[0, kvl, (half - 1) * w:(half + 1) * w]
                vv = vc_ref[(half - 1) * w:(half + 1) * w, kvl]
            s = jnp.dot(qm, kt, preferred_element_type=F32)
            s = jnp.where(mask, s - slopes[hq] * relf, -jnp.inf)
            sink = sink_ref[hq]
            m = jnp.maximum(jnp.max(s, axis=-1, keepdims=True), sink)
            e = jnp.exp(s - m)
            den = jnp.sum(e, axis=-1, keepdims=True) + jnp.exp(sink - m)
            p = (e / den).astype(BF16)
            res.append(jnp.dot(p, vv, preferred_element_type=F32))
        for pr in range(C_HEADS // 2):
            o_ref[rows, pr * LANES:(pr + 1) * LANES] = jnp.where(lane < HEAD_DIM, res[2 * pr],
                                                                 res[2 * pr + 1]).astype(BF16)


def _swa_attention(sinks, qc, ktc, vc2, *, bsz, seq, slopes):
    tq = KT
    nq = seq // tq
    t = bsz * seq
    return pl.pallas_call(
        functools.partial(_swa_kernel, slopes=slopes),
        grid=(bsz, nq),
        in_specs=[
            pl.BlockSpec(memory_space=pltpu.SMEM),
            pl.BlockSpec((tq, C_HEADS * HEAD_DIM), lambda b, n: (b * nq + n, 0)),
            pl.BlockSpec((1, C_KV_HEADS * LANES, KT), lambda b, n: (b * nq + jnp.maximum(n - 1, 0), 0, 0)),
            pl.BlockSpec((1, C_KV_HEADS * LANES, KT), lambda b, n: (b * nq + n, 0, 0)),
            pl.BlockSpec((tq, C_KV_HEADS * LANES), lambda b, n: (b * nq + jnp.maximum(n - 1, 0), 0)),
            pl.BlockSpec((tq, C_KV_HEADS * LANES), lambda b, n: (b * nq + n, 0)),
        ],
        out_specs=pl.BlockSpec((tq, C_HEADS * HEAD_DIM), lambda b, n: (b * nq + n, 0)),
        out_shape=jax.ShapeDtypeStruct((t, C_HEADS * HEAD_DIM), BF16),
        compiler_params=pltpu.CompilerParams(dimension_semantics=("arbitrary",) * 2,
                                             vmem_limit_bytes=VMEM_LIMIT),
        name="swa_attn",
    )(sinks, qc, ktc, ktc, vc2, vc2)


def _outproj_kernel(oa_ref, ob_ref, oc_ref, woa_ref, wob_ref, woc_ref, x_ref, mod_ref, g_ref, wq_ref, keys_ref,
                    x1_ref, h2_ref, st_ref):
    d = x_ref.shape[1]
    mix = (jnp.dot(oa_ref[...], woa_ref[...], preferred_element_type=F32)
           + jnp.dot(ob_ref[...], wob_ref[...], preferred_element_type=F32)
           + jnp.dot(oc_ref[...], woc_ref[...], preferred_element_type=F32))
    g1 = mod_ref[0, :, 2 * d:3 * d]
    sh2 = mod_ref[0, :, 3 * d:4 * d]
    sc2 = mod_ref[0, :, 4 * d:5 * d]
    x1 = x_ref[...] + g1 * mix
    x1_ref[...] = x1
    h2 = _rms_mod(x1, g_ref[...], sc2, sh2)
    nj = d // LANES
    for j in range(nj):
        h2_ref[pl.ds(j, x1.shape[0], stride=nj), :] = h2[:, j * LANES:(j + 1) * LANES]
    pq = jnp.dot(h2.astype(BF16), wq_ref[...], preferred_element_type=F32).astype(BF16)
    for hp in range(2 * PEER_HEADS):
        st_ref[hp] = lax.dot_general(keys_ref[hp], pq[:, hp * LANES:(hp + 1) * LANES],
                                     (((1,), (1,)), ((), ())), preferred_element_type=F32)


def _outproj(oa, ob, oc, woa, wob, woc, x2d, mod3, g, wq, keys, *, seq, tm=256):
    t, d = x2d.shape
    nt = t // tm
    nq = wq.shape[1]
    full = lambda a: pl.BlockSpec(a.shape, lambda i: (0,) * a.ndim)
    return pl.pallas_call(
        _outproj_kernel,
        grid=(nt,),
        in_specs=[
            pl.BlockSpec((tm, oa.shape[1]), lambda i: (i, 0)),
            pl.BlockSpec((tm, ob.shape[1]), lambda i: (i, 0)),
            pl.BlockSpec((tm, oc.shape[1]), lambda i: (i, 0)),
            full(woa), full(wob), full(woc),
            pl.BlockSpec((tm, d), lambda i: (i, 0)),
            pl.BlockSpec((1, 1, mod3.shape[2]), lambda i: ((i * tm) // seq, 0, 0)),
            pl.BlockSpec((1, d), lambda i: (0, 0)),
            full(wq), full(keys),
        ],
        out_specs=[pl.BlockSpec((tm, d), lambda i: (i, 0)),
                   pl.BlockSpec((tm * (d // LANES), LANES), lambda i: (i, 0)),
                   pl.BlockSpec((2 * PEER_HEADS, PEER_NKEYS, tm), lambda i: (0, 0, i))],
        out_shape=[jax.ShapeDtypeStruct((t, d), F32), jax.ShapeDtypeStruct((t * (d // LANES), LANES), F32),
                   jax.ShapeDtypeStruct((2 * PEER_HEADS, PEER_NKEYS, t), F32)],
        compiler_params=pltpu.CompilerParams(dimension_semantics=("arbitrary",),
                                             vmem_limit_bytes=VMEM_LIMIT),
        name="outproj_peerq",
    )(oa, ob, oc, woa, wob, woc, x2d, mod3, g, wq, keys)


_CAND_BLOCKS = ((0, 16),) + tuple((i, 8) for i in range(1, 8))
_CAND_ROWS = 16 + 7 * 8 + 8
_BIG_I = np.int32(2 ** 30)


TOPK_TL = LANES


def _topk_rows(curs, keys, val_refs, key_refs, unroll):
    def body(r, curs):
        out = []
        for cur, key, val_ref, key_ref in zip(curs, keys, val_refs, key_refs):
            m = jnp.max(cur, axis=0, keepdims=True)
            kmin = jnp.min(jnp.where(cur == m, key, _BIG_I), axis=0, keepdims=True)
            val_ref[pl.ds(r, 1), :] = m
            key_ref[pl.ds(r, 1), :] = kmin
            out.append(jnp.where(key == kmin, -jnp.inf, cur))
        return tuple(out)
    lax.fori_loop(0, PEER_TOPK, body, tuple(curs), unroll=unroll)


def _topk_scratch():
    pair = [pltpu.VMEM((2, PEER_TOPK, TOPK_TL), F32), pltpu.VMEM((2, PEER_TOPK, TOPK_TL), I32)]
    hk = PEER_HEADS * PEER_TOPK
    return pair * 3 + [pltpu.VMEM((hk, TOPK_TL), I32), pltpu.VMEM((hk, TOPK_TL), F32)]


def _product_candidates(sv1, si1, sv2, si2):
    tl = sv1.shape[1]
    jrow8 = lax.broadcasted_iota(I32, (8, tl), 0)
    jrow16 = lax.broadcasted_iota(I32, (16, tl), 0)
    cands, keys = [], []
    for i, nj in _CAND_BLOCKS:
        jrow = jrow16 if nj == 16 else jrow8
        cands.append(sv1[i:i + 1] + sv2[0:nj])
        keys.append((i * PEER_TOPK + jrow) * (PEER_NKEYS * PEER_NKEYS) + si1[i:i + 1] * PEER_NKEYS + si2[0:nj])
    cands.append(sv1[8:16] + sv2[0:1])
    keys.append((jrow8 + 8) * (PEER_TOPK * PEER_NKEYS * PEER_NKEYS) + si1[8:16] * PEER_NKEYS + si2[0:1])
    return jnp.concatenate(cands, axis=0), jnp.concatenate(keys, axis=0)


def _store_gates(h, top, keys, ei_ref, gg_ref):
    rows = pl.ds(pl.multiple_of(h * PEER_TOPK, PEER_TOPK), PEER_TOPK)
    e = jnp.exp(top - top[0:1])
    gg_ref[rows, :] = e / jnp.sum(e, axis=0, keepdims=True)
    ei_ref[rows, :] = keys & (PEER_NKEYS * PEER_NKEYS - 1)


def _topk_head_pair(st_ref, h0, scratch, unroll=False):
    v1_ref, k1_ref, v2_ref, k2_ref, vt_ref, kt_ref, ei_ref, gg_ref = scratch
    row = lax.broadcasted_iota(I32, (PEER_NKEYS, st_ref.shape[2]), 0)
    cand, ckey = [], []
    for s in range(2):
        h = h0 + s
        _topk_rows([st_ref[2 * h], st_ref[2 * h + 1]], [row, row],
                   [v1_ref.at[s], v2_ref.at[s]], [k1_ref.at[s], k2_ref.at[s]], unroll)
        c, k = _product_candidates(v1_ref[s], k1_ref[s], v2_ref[s], k2_ref[s])
        cand.append(c)
        ckey.append(k)
    _topk_rows(cand, ckey, [vt_ref.at[0], vt_ref.at[1]], [kt_ref.at[0], kt_ref.at[1]], unroll)
    for s in range(2):
        _store_gates(h0 + s, vt_ref[s], kt_ref[s], ei_ref, gg_ref)


def _topk_emit(scratch, off_ref, par_ref, g_ref):
    ei = scratch[6][...].T
    off_ref[...] = (ei & (PEER_HALF - 1)) * SUB
    par_ref[...] = ei // PEER_HALF
    g_ref[...] = scratch[7][...].T


def _peer_topk_kernel(st_ref, off_ref, par_ref, g_ref, *scratch):
    for h0 in range(0, PEER_HEADS, 2):
        _topk_head_pair(st_ref, h0, scratch, unroll=True)
    _topk_emit(scratch, off_ref, par_ref, g_ref)


def _peer_topk(st, blk0, nblk):
    hk = PEER_HEADS * PEER_TOPK
    t = nblk * TOPK_TL
    return pl.pallas_call(
        _peer_topk_kernel,
        grid=(nblk,),
        in_specs=[pl.BlockSpec((2 * PEER_HEADS, PEER_NKEYS, TOPK_TL), lambda i: (0, 0, blk0 + i))],
        out_specs=[pl.BlockSpec((TOPK_TL, hk), lambda i: (i, 0))] * 3,
        out_shape=[jax.ShapeDtypeStruct((t, hk), I32), jax.ShapeDtypeStruct((t, hk), I32),
                   jax.ShapeDtypeStruct((t, hk), F32)],
        scratch_shapes=_topk_scratch(),
        compiler_params=pltpu.CompilerParams(dimension_semantics=("arbitrary",),
                                             vmem_limit_bytes=VMEM_LIMIT),
        name="peer_topk",
    )(st)


SUB = 8


def _erf(x):
    return lax.erf(x)


HALF_HI = 0xFFFF0000


PEER_HALF = PEER_NKEYS * PEER_NKEYS // 2


def _pack_kernel(lo_ref, hi_ref, o_ref):
    pairs = lo_ref.shape[0]
    nj = lo_ref.shape[1] // LANES
    for j in range(nj):
        cols = slice(j * LANES, (j + 1) * LANES)
        lo = pltpu.bitcast(lo_ref[:, cols].astype(BF16).astype(F32), jnp.uint32)
        hi = pltpu.bitcast(hi_ref[:, cols].astype(BF16).astype(F32), jnp.uint32)
        o_ref[pl.ds(j, pairs, stride=nj), :] = (lo >> 16) | (hi & jnp.uint32(HALF_HI))


def _pack_expert_table(w, pairs=256):
    n, d = w.shape
    nj = d // LANES
    half_blocks = n // 2 // pairs
    return pl.pallas_call(
        _pack_kernel,
        grid=(half_blocks,),
        in_specs=[pl.BlockSpec((pairs, d), lambda i: (i, 0)),
                  pl.BlockSpec((pairs, d), lambda i: (half_blocks + i, 0))],
        out_specs=pl.BlockSpec((pairs * nj, LANES), lambda i: (i, 0)),
        out_shape=jax.ShapeDtypeStruct((n // 2 * nj, LANES), jnp.uint32),
        compiler_params=pltpu.CompilerParams(dimension_semantics=("arbitrary",),
                                             vmem_limit_bytes=VMEM_LIMIT),
        name="pack_table",
    )(w, w)


def _load_table_once(tab_hbm, tab, sem):
    @pl.when(pl.program_id(0) == 0)
    def _():
        cp = pltpu.make_async_copy(tab_hbm, tab, sem)
        cp.start()
        cp.wait()


def _pair_tile(tab, off):
    return tab[pl.ds(pl.multiple_of(off, SUB), SUB), :]


def _group_matrix(rows, cols):
    r = lax.broadcasted_iota(I32, (rows, cols), 0)
    c = lax.broadcasted_iota(I32, (rows, cols), 1)
    return (c // (cols // rows) == r).astype(BF16)


def _split2(x):
    hi = x.astype(BF16)
    return hi, (x - hi.astype(F32)).astype(BF16)


PAIR_ROWS = 2 * SUB


def _pair_rows(tab, off_ref, tt, hk):
    return jnp.concatenate([pltpu.bitcast(_pair_tile(tab, off_ref[tt * hk + k]), BF16) for k in range(hk)], axis=0)


def _own_sublane_mask(hk):
    srow = lax.broadcasted_iota(I32, (SUB, hk * PAIR_ROWS), 0)
    scol = lax.broadcasted_iota(I32, (SUB, hk * PAIR_ROWS), 1)
    return (scol % PAIR_ROWS) // 2 == srow


def _fold_matrix(hk):
    c = jnp.arange(hk * PAIR_ROWS)
    return jax.nn.one_hot((c % 2) * hk + c // PAIR_ROWS, 2 * hk, dtype=BF16)


def _peer_u_kernel(off_ref, par_ref, h_ref, g_ref, fold_ref, tab_hbm, w_ref, tab, sem, d_sc):
    tq, hk = g_ref.shape
    _load_table_once(tab_hbm, tab, sem)
    mine = _own_sublane_mask(hk)
    nt = (((1,), (1,)), ((), ()))
    for tt in range(tq):
        h_hi, h_lo = _split2(h_ref[tt])
        s = lax.dot_general(jnp.concatenate([h_hi, h_lo], axis=0), _pair_rows(tab, off_ref, tt, hk), nt,
                            preferred_element_type=F32)
        d = jnp.where(mine, s[0:SUB] + s[SUB:2 * SUB], 0.0)
        d_sc[tt:tt + 1, :] = jnp.sum(d, axis=0, keepdims=True)
    d_hi, d_lo = _split2(d_sc[...])
    a2 = jnp.dot(jnp.concatenate([d_hi, d_lo], axis=0), fold_ref[...], preferred_element_type=F32)
    a2 = a2[0:tq] + a2[tq:2 * tq]
    a = jnp.where(par_ref[...] == 0, a2[:, :hk], a2[:, hk:])
    w_ref[...] = g_ref[...] * (0.5 * a * (1.0 + _erf(a * (2.0 ** -0.5))))


U_TQ = 128
V_TQ = 64


def _peer_u(off_flat, par, g, h2, utab, tok0):
    t, hk = g.shape
    nj = h2.shape[1]
    tq = U_TQ
    assert t % tq == 0 and tok0 % tq == 0
    fold = _fold_matrix(hk)
    return pl.pallas_call(
        _peer_u_kernel,
        grid=(t // tq,),
        in_specs=[
            pl.BlockSpec((tq * hk,), lambda i: (i,), memory_space=pltpu.SMEM),
            pl.BlockSpec((tq, hk), lambda i: (i, 0)),
            pl.BlockSpec((tq, nj, LANES), lambda i: (tok0 // tq + i, 0, 0)),
            pl.BlockSpec((tq, hk), lambda i: (i, 0)),
            pl.BlockSpec(fold.shape, lambda i: (0, 0)),
            pl.BlockSpec(memory_space=pl.ANY),
        ],
        out_specs=pl.BlockSpec((tq, hk), lambda i: (i, 0)),
        out_shape=jax.ShapeDtypeStruct((t, hk), F32),
        scratch_shapes=[pltpu.VMEM(utab.shape, jnp.uint32), pltpu.SemaphoreType.DMA(()),
                        pltpu.VMEM((tq, hk * PAIR_ROWS), F32)],
        compiler_params=pltpu.CompilerParams(dimension_semantics=("arbitrary",),
                                             vmem_limit_bytes=VMEM_LIMIT),
        name="peer_u",
    )(off_flat, par, h2, g, fold, utab)


def _peer_v_body(off_ref, par_ref, w_ref, x1_ref, mod_ref, fg_ref, o_ref, tab, wl_sc, p_sc, final):
    tq, hk = w_ref.shape
    d = x1_ref.shape[1]
    nj = d // LANES
    width = hk * PAIR_ROWS
    rep = _group_matrix(hk, width)
    w_hi, w_lo = _split2(w_ref[...])
    parl = jnp.dot(par_ref[...].astype(BF16), rep, preferred_element_type=F32)
    lane = lax.broadcasted_iota(I32, (tq, width), 1)
    wanted = (lane % 2).astype(F32) == parl
    wl_sc[0] = jnp.where(wanted, jnp.dot(w_hi, rep, preferred_element_type=F32), 0.0)
    wl_sc[1] = jnp.where(wanted, jnp.dot(w_lo, rep, preferred_element_type=F32), 0.0)
    mine = _own_sublane_mask(hk)
    for tt in range(tq):
        lhs = jnp.concatenate([jnp.where(mine, wl_sc[0, tt:tt + 1, :], 0.0),
                               jnp.where(mine, wl_sc[1, tt:tt + 1, :], 0.0)], axis=0).astype(BF16)
        r = jnp.dot(lhs, _pair_rows(tab, off_ref, tt, hk), preferred_element_type=F32)
        p_sc[tt * SUB:(tt + 1) * SUB, :] = r[0:SUB] + r[SUB:2 * SUB]
    peer = jnp.concatenate([p_sc[pl.ds(j, tq, stride=nj), :] for j in range(nj)], axis=1)
    y = x1_ref[...] + mod_ref[0, :, 5 * d:6 * d] * peer
    if final:
        y = y * lax.rsqrt(jnp.mean(y * y, axis=-1, keepdims=True) + EPS) * fg_ref[...]
    o_ref[...] = y


def _peer_v_kernel(off_ref, par_ref, w_ref, x1_ref, mod_ref, fg_ref, tab_hbm, o_ref, tab, sem, wl_sc, p_sc, *, final):
    _load_table_once(tab_hbm, tab, sem)
    _peer_v_body(off_ref, par_ref, w_ref, x1_ref, mod_ref, fg_ref, o_ref, tab, wl_sc, p_sc, final)


def _peer_v_topk_kernel(off_ref, par_ref, w_ref, x1_ref, mod_ref, fg_ref, st_ref, tab_hbm,
                        o_ref, off2_ref, par2_ref, g2_ref, tab, sem, wl_sc, p_sc, *scratch, final):
    steps = TOPK_TL // w_ref.shape[0]
    pairs = PEER_HEADS // 2 // steps
    part = pl.program_id(0) % steps
    _load_table_once(tab_hbm, tab, sem)
    for q in range(pairs):
        _topk_head_pair(st_ref, (pairs * part + q) * 2, scratch, unroll=True)
    _peer_v_body(off_ref, par_ref, w_ref, x1_ref, mod_ref, fg_ref, o_ref, tab, wl_sc, p_sc, final)

    @pl.when(part == steps - 1)
    def _():
        _topk_emit(scratch, off2_ref, par2_ref, g2_ref)


def _peer_v(off_flat, par, w, x1, mod3, final_g, vtab, tok0, *, seq, final, st=None, st_blk0=None):
    t, hk = w.shape
    d = x1.shape[1]
    tq = V_TQ
    steps = TOPK_TL // tq
    assert t % TOPK_TL == 0 and steps * tq == TOPK_TL and (PEER_HEADS // 2) % steps == 0
    assert tok0 % tq == 0 and tok0 // seq == (tok0 + t - 1) // seq
    in_specs = [
        pl.BlockSpec((tq * hk,), lambda i: (i,), memory_space=pltpu.SMEM),
        pl.BlockSpec((tq, hk), lambda i: (i, 0)),
        pl.BlockSpec((tq, hk), lambda i: (i, 0)),
        pl.BlockSpec((tq, d), lambda i: (tok0 // tq + i, 0)),
        pl.BlockSpec((1, 1, mod3.shape[2]), lambda i: (tok0 // seq, 0, 0)),
        pl.BlockSpec((1, d), lambda i: (0, 0)),
    ]
    out_specs = [pl.BlockSpec((tq, d), lambda i: (tok0 // tq + i, 0))]
    out_shape = [jax.ShapeDtypeStruct(x1.shape, F32)]
    scratch = [pltpu.VMEM(vtab.shape, jnp.uint32), pltpu.SemaphoreType.DMA(()),
               pltpu.VMEM((2, tq, hk * PAIR_ROWS), F32), pltpu.VMEM((tq * (d // LANES), LANES), F32)]
    args = [off_flat, par, w, x1, mod3, final_g]
    if st is not None:
        in_specs.append(pl.BlockSpec((2 * PEER_HEADS, PEER_NKEYS, TOPK_TL), lambda i: (0, 0, st_blk0 + i // steps)))
        out_specs += [pl.BlockSpec((TOPK_TL, hk), lambda i: (i // steps, 0))] * 3
        out_shape += [jax.ShapeDtypeStruct((t, hk), I32), jax.ShapeDtypeStruct((t, hk), I32),
                      jax.ShapeDtypeStruct((t, hk), F32)]
        scratch += _topk_scratch()
        args.append(st)
    in_specs.append(pl.BlockSpec(memory_space=pl.ANY))
    args.append(vtab)
    return pl.pallas_call(
        functools.partial(_peer_v_kernel if st is None else _peer_v_topk_kernel, final=final),
        grid=(t // tq,),
        in_specs=in_specs,
        out_specs=out_specs,
        out_shape=out_shape,
        scratch_shapes=scratch,
        input_output_aliases={3: 0},
        compiler_params=pltpu.CompilerParams(dimension_semantics=("arbitrary",),
                                             vmem_limit_bytes=VMEM_LIMIT),
        name="peer_v" if st is None else "peer_v_topk",
    )(*args)


def kernel(x, c, norm1_g, norm2_g, w_ada, b_ada, w_in, w_out, lam_q1, lam_k1, lam_q2, lam_k2, subln_g, sinks,
           peer_wq, peer_keys, peer_u, peer_v, final_g):
    bsz, seq, d = x.shape
    depth = w_in.shape[0]
    t = bsz * seq
    slopes = _alibi_slopes()
    sl_c = [float(s) for s in slopes[:C_HEADS]]
    qa_bias = _slope_bias_col(slopes[C_HEADS:C_HEADS + A_HEADS] * np.float32(LOG2E))
    qb_bias = _slope_bias_col(slopes[C_HEADS + A_HEADS:] * np.float32(LOG2E))
    mods = _adaln_mods(c, w_ada, b_ada)
    x2d = x.reshape(t, d)
    av, bw = A_HEADS * HEAD_DIM, B_HEADS * HEAD_DIM
    for l in range(depth):
        lam_init = 0.8 - 0.6 * math.exp(-0.3 * l)
        mod3 = mods[l].reshape(bsz, 1, N_MOD * d)
        wn, wt = _prep_in_weights(w_in[l])
        ka, kb, qc, vc2, qta, vta, qtb, vtb, ktc = _inproj(
            x2d, mod3, norm1_g[l].reshape(1, d), wn, wt, qa_bias, qb_bias, seq=seq)
        lamv = jnp.stack([lam_q1[l], lam_k1[l], lam_q2[l], lam_k2[l]]).astype(F32)
        sg = subln_g[l].reshape(HEAD_DIM, 1).astype(F32)
        oa = _diff_attention(qta, ka, vta, lamv, sg, bsz=bsz, seq=seq, lam_init=lam_init)
        ob = _moba_attention(qtb, kb, vtb, bsz=bsz, seq=seq)
        oc = _swa_attention(sinks[l].astype(F32), qc, ktc, vc2, bsz=bsz, seq=seq, slopes=sl_c)
        wo = w_out[l].astype(BF16)
        keys = peer_keys[l].reshape(2 * PEER_HEADS, PEER_NKEYS, -1).astype(BF16)
        x1, h2, st = _outproj(oa, ob, oc, wo[:av], wo[av:av + bw], wo[av + bw:], x2d, mod3,
                              norm2_g[l].reshape(1, d), peer_wq[l].astype(BF16), keys, seq=seq)
        assert seq % TOPK_TL == 0
        blocks = seq // TOPK_TL
        utab = _pack_expert_table(peer_u[l])
        vtab = _pack_expert_table(peer_v[l])
        h2t = h2.reshape(t, d // LANES, LANES)
        off_b, par_b, g_b = _peer_topk(st, 0, blocks)
        x2d = x1
        for b in range(bsz):
            w_b = _peer_u(off_b.reshape(-1), par_b, g_b, h2t, utab, b * seq)
            more = dict(st=st, st_blk0=(b + 1) * blocks) if b + 1 < bsz else {}
            x2d, *nxt = _peer_v(off_b.reshape(-1), par_b, w_b, x2d, mod3, final_g.reshape(1, d), vtab, b * seq,
                                seq=seq, final=(l == depth - 1), **more)
            if nxt:
                off_b, par_b, g_b = nxt
    return x2d.reshape(bsz, seq, d)
```

```python
import functools
import math

import numpy as np
import jax
import jax.numpy as jnp
from jax import lax
from jax.experimental import pallas as pl
from jax.experimental.pallas import tpu as pltpu

F32 = jnp.float32
BF16 = jnp.bfloat16
I32 = jnp.int32

HEAD_DIM = 64
N_HEADS_TOTAL = 16
A_HEADS = 4
B_HEADS = 6
C_HEADS = 6
C_KV_HEADS = 2
C_GROUP = 3
A_QK_DIM = 32
MOBA_BLOCK = 256
MOBA_TOPK = 3
WINDOW = 128
ALIBI_MAX = 8.0
PEER_HEADS = 8
PEER_NKEYS = 128
PEER_TOPK = 16
N_MOD = 6
EPS = 1e-6

LANES = 128
KT = MOBA_BLOCK
AUG0 = HEAD_DIM
SEL0 = AUG0 + 6
MAX_BLOCKS = LANES - SEL0
NEG_BIG = -1e30
LOG2E = math.log2(math.e)
VMEM_LIMIT = 56 * 1024 * 1024


def _alibi_slopes():
    n = N_HEADS_TOTAL
    return (2.0 ** (-ALIBI_MAX * np.arange(1, n + 1, dtype=np.float32) / n)).astype(np.float32)


def _split3(v):
    v = np.float32(v)
    hi = np.float32(np.asarray(v).astype(jnp.bfloat16).astype(np.float32))
    r = np.float32(v - hi)
    mid = np.float32(np.asarray(r).astype(jnp.bfloat16).astype(np.float32))
    lo = np.float32(np.float32(r - mid))
    lo = np.float32(np.asarray(lo).astype(jnp.bfloat16).astype(np.float32))
    return hi, mid, lo


def _slope_bias_col(slopes):
    col = np.zeros((LANES * len(slopes), 1), np.float32)
    for h, s in enumerate(slopes):
        hi, mid, lo = _split3(s)
        col[h * LANES + AUG0:h * LANES + AUG0 + 6, 0] = [hi, mid, lo, hi, mid, lo]
    return jnp.asarray(col)


def _mod_kernel(c_ref, w_ref, b_ref, o_ref):
    c = c_ref[...]
    cs = c * (1.0 / (1.0 + jnp.exp(-c)))
    o_ref[0] = jnp.dot(cs, w_ref[0], preferred_element_type=F32) + b_ref[0]


def _adaln_mods(c, w_ada, b_ada):
    depth, d, n = w_ada.shape
    bsz = c.shape[0]
    rows = -(-bsz // 8) * 8
    cp = jnp.pad(c, ((0, rows - bsz), (0, 0)))
    tn = 1536
    out = pl.pallas_call(
        _mod_kernel,
        grid=(depth, n // tn),
        in_specs=[
            pl.BlockSpec((rows, d), lambda l, j: (0, 0)),
            pl.BlockSpec((1, d, tn), lambda l, j: (l, 0, j)),
            pl.BlockSpec((1, 1, tn), lambda l, j: (l, 0, j)),
        ],
        out_specs=pl.BlockSpec((1, rows, tn), lambda l, j: (l, 0, j)),
        out_shape=jax.ShapeDtypeStruct((depth, rows, n), F32),
        compiler_params=pltpu.CompilerParams(dimension_semantics=("arbitrary", "arbitrary"),
                                             vmem_limit_bytes=VMEM_LIMIT),
        name="adaln_mods",
    )(cp, w_ada, b_ada.reshape(depth, 1, n))
    return out[:, :bsz]


NN_WIDTHS = (A_HEADS * LANES, B_HEADS * LANES, C_HEADS * HEAD_DIM, 2 * C_KV_HEADS * HEAD_DIM)
NT_ROWS = (A_HEADS * LANES, A_HEADS * HEAD_DIM, B_HEADS * LANES, B_HEADS * HEAD_DIM,
           C_KV_HEADS * LANES)


def _prep_in_weights(w):
    d = w.shape[0]
    aq, ak, av = A_HEADS * 2 * A_QK_DIM, A_HEADS * 2 * A_QK_DIM, A_HEADS * HEAD_DIM
    bw = B_HEADS * HEAD_DIM
    cq, ckv = C_HEADS * HEAD_DIM, C_KV_HEADS * HEAD_DIM
    cuts = np.cumsum([aq, ak, av, bw, bw, bw, cq, ckv]).tolist()
    qa, ka, va, qb, kb, vb, qc, kc, vc = jnp.split(w, cuts, axis=-1)

    def pad_heads(m, nh, scale):
        m = (m * scale).reshape(d, nh, HEAD_DIM)
        return jnp.pad(m, ((0, 0), (0, 0), (0, LANES - HEAD_DIM))).reshape(d, nh * LANES)

    vc2 = vc.reshape(d, C_KV_HEADS, 1, HEAD_DIM)
    vc2 = jnp.broadcast_to(vc2, (d, C_KV_HEADS, 2, HEAD_DIM)).reshape(d, 2 * ckv)
    kc2 = jnp.broadcast_to(kc.reshape(d, C_KV_HEADS, 1, HEAD_DIM), (d, C_KV_HEADS, 2, HEAD_DIM)).reshape(d, 2 * ckv)
    wn = jnp.concatenate([pad_heads(ka, A_HEADS, 1.0), pad_heads(kb, B_HEADS, 1.0),
                          qc * (HEAD_DIM ** -0.5), vc2], axis=1)
    wt = jnp.concatenate([pad_heads(qa, A_HEADS, A_QK_DIM ** -0.5 * LOG2E), va,
                          pad_heads(qb, B_HEADS, HEAD_DIM ** -0.5 * LOG2E), vb, kc2], axis=1).T
    return wn.astype(BF16), wt.astype(BF16)


def _rms_mod(x, g, sc, sh):
    ms = jnp.mean(x * x, axis=-1, keepdims=True)
    return (x * lax.rsqrt(ms + EPS) * g) * (1.0 + sc) + sh


def _inproj_kernel(x_ref, mod_ref, g_ref, wn_ref, wt_ref, qab_ref, qbb_ref,
                   ka_ref, kb_ref, qc_ref, vc_ref, qta_ref, vta_ref, qtb_ref, vtb_ref, ktc_ref,
                   *, tm, seq):
    d = x_ref.shape[1]
    x = x_ref[...]
    sh = mod_ref[0, :, 0:d]
    sc = mod_ref[0, :, d:2 * d]
    h = _rms_mod(x, g_ref[...], sc, sh).astype(BF16)
    pn = jnp.dot(h, wn_ref[...], preferred_element_type=F32)
    pt = lax.dot_general(wt_ref[...], h, (((1,), (1,)), ((), ())),
                         preferred_element_type=F32)

    pos = (pl.program_id(0) * tm) % seq + lax.broadcasted_iota(I32, (tm, LANES), 0)
    col = lax.broadcasted_iota(I32, (tm, LANES), 1)
    blk_id = pos // KT
    p_hi = (blk_id * KT).astype(F32)
    p_lo = (pos - blk_id * KT).astype(F32)
    aug_a = jnp.where((col >= AUG0) & (col < AUG0 + 3), p_hi,
                      jnp.where((col >= AUG0 + 3) & (col < AUG0 + 6), p_lo, 0.0))
    aug_b = jnp.where((col >= SEL0) & (col - SEL0 == blk_id), 1.0, aug_a)
    for hh in range(A_HEADS):
        ka_ref[:, hh * LANES:(hh + 1) * LANES] = (pn[:, hh * LANES:(hh + 1) * LANES] + aug_a).astype(BF16)
    o = NN_WIDTHS[0]
    for hh in range(B_HEADS):
        kb_ref[:, hh * LANES:(hh + 1) * LANES] = (pn[:, o + hh * LANES:o + (hh + 1) * LANES] + aug_b).astype(BF16)
    o += NN_WIDTHS[1]
    qc_ref[...] = pn[:, o:o + NN_WIDTHS[2]].astype(BF16)
    o += NN_WIDTHS[2]
    vc_ref[...] = pn[:, o:o + NN_WIDTHS[3]].astype(BF16)

    r0 = 0
    for ref, b, nr in zip((qta_ref, vta_ref, qtb_ref, vtb_ref, ktc_ref),
                          (qab_ref, None, qbb_ref, None, None), NT_ROWS):
        blk = pt[r0:r0 + nr, :]
        if b is not None:
            blk = blk + b[...]
        blk = blk.astype(BF16)
        for cc in range(tm // KT):
            ref[cc] = blk[:, cc * KT:(cc + 1) * KT]
        r0 += nr


def _inproj(x2d, mod3, g, wn, wt, qa_bias, qb_bias, *, seq, tm=512):
    t, d = x2d.shape
    assert seq % tm == 0 and tm % KT == 0
    nt = t // tm
    nn_total = sum(NN_WIDTHS)
    row_specs = [pl.BlockSpec((tm, wd), lambda i: (i, 0)) for wd in NN_WIDTHS]
    kt_specs = [pl.BlockSpec((tm // KT, r, KT), lambda i: (i, 0, 0)) for r in NT_ROWS]
    out_shape = ([jax.ShapeDtypeStruct((t, wd), BF16) for wd in NN_WIDTHS]
                 + [jax.ShapeDtypeStruct((t // KT, r, KT), BF16) for r in NT_ROWS])
    return pl.pallas_call(
        functools.partial(_inproj_kernel, tm=tm, seq=seq),
        grid=(nt,),
        in_specs=[
            pl.BlockSpec((tm, d), lambda i: (i, 0)),
            pl.BlockSpec((1, 1, mod3.shape[2]), lambda i: ((i * tm) // seq, 0, 0)),
            pl.BlockSpec((1, d), lambda i: (0, 0)),
            pl.BlockSpec((d, nn_total), lambda i: (0, 0)),
            pl.BlockSpec((sum(NT_ROWS), d), lambda i: (0, 0)),
            pl.BlockSpec((NT_ROWS[0], 1), lambda i: (0, 0)),
            pl.BlockSpec((NT_ROWS[2], 1), lambda i: (0, 0)),
        ],
        out_specs=row_specs + kt_specs,
        out_shape=out_shape,
        compiler_params=pltpu.CompilerParams(dimension_semantics=("arbitrary",),
                                             vmem_limit_bytes=VMEM_LIMIT),
        name="inproj",
    )(x2d, mod3, g, wn, wt, qa_bias, qb_bias)


ACC_ROWS = HEAD_DIM + 16


def _softmax_pv(ss, vt1s, m_ref, acc_ref):
    nh = len(ss)
    m_prev = [m_ref[hh] for hh in range(nh)]
    m_new = [jnp.maximum(m_prev[hh], jnp.max(ss[hh], axis=0, keepdims=True)) for hh in range(nh)]
    ps = [jnp.exp2((ss[hh] - m_new[hh]).astype(BF16)) for hh in range(nh)]
    pv = [jnp.dot(vt1s[hh], ps[hh], preferred_element_type=F32) for hh in range(nh)]
    for hh in range(nh):
        acc_ref[hh] = jnp.exp2(m_prev[hh] - m_new[hh]) * acc_ref[hh] + pv[hh]
        m_ref[hh] = m_new[hh]


def _flash_init(m_ref, acc_ref):
    m_ref[...] = jnp.full(m_ref.shape, -jnp.inf, F32)
    acc_ref[...] = jnp.zeros(acc_ref.shape, F32)


def _flash_finish(acc_ref, hh):
    acc = acc_ref[hh]
    return acc[0:HEAD_DIM] / acc[HEAD_DIM:HEAD_DIM + 1]


PAST_TILES = 4
SERIAL_TILES = 2


def _flash_causal_pair(k_ref, vt_ref, q_diag, q_past, qi, causal, m_ref, acc_ref, s_ref=None):
    _flash_init(m_ref, acc_ref)
    nkeys = PAST_TILES * KT
    nh = len(q_past)

    def vt1(kj, n, hh):
        rows = slice(hh * HEAD_DIM, (hh + 1) * HEAD_DIM)
        vt = vt_ref[kj, rows, :] if n == 1 else jnp.concatenate([vt_ref[kj + c, rows, :] for c in range(n)], axis=1)
        return jnp.concatenate([vt, jnp.ones((ACC_ROWS - HEAD_DIM, n * KT), BF16)], axis=0)

    def scores(kj, n, qs, hh):
        kk = k_ref[pl.ds(pl.multiple_of(kj * KT, KT), n * KT), hh * LANES:(hh + 1) * LANES]
        return jnp.dot(kk, qs[hh], preferred_element_type=F32)

    def step(kj, n, qs, mask=None):
        ss = [scores(kj, n, qs, hh) for hh in range(nh)]
        if mask is not None:
            ss = [jnp.where(mask, s, -jnp.inf) for s in ss]
        _softmax_pv(ss, [vt1(kj, n, hh) for hh in range(nh)], m_ref, acc_ref)

    step(qi, 1, q_diag, causal)

    if s_ref is None:
        def body(j, carry):
            step(j * SERIAL_TILES, SERIAL_TILES, q_past)
            return carry

        lax.fori_loop(0, qi // SERIAL_TILES, body, 0)
        for r in range(SERIAL_TILES - 1):
            @pl.when(qi % SERIAL_TILES > r)
            def _():
                step(qi - 1 - r, 1, q_past)
        return

    ngroups = k_ref.shape[0] // nkeys

    def scores_into(slot, g):
        for hh in range(nh):
            s_ref[slot, hh] = scores(g * PAST_TILES, PAST_TILES, q_past, hh)

    scores_into(0, 0)

    def body(g, carry):
        scores_into((g + 1) % 2, jnp.minimum(g + 1, ngroups - 1))
        _softmax_pv([s_ref[g % 2, hh] for hh in range(nh)],
                    [vt1(g * PAST_TILES, PAST_TILES, hh) for hh in range(nh)], m_ref, acc_ref)
        return carry

    lax.fori_loop(0, (qi + PAST_TILES - 1) // PAST_TILES, body, 0)


def _diff_kernel(qt_ref, k_ref, vt_ref, lamv_ref, sg_ref, o_ref, m_ref, acc_ref, *, lam_init):
    tq = qt_ref.shape[2]
    qi = pl.program_id(2)
    lv = lamv_ref[...]
    lam = (jnp.exp(jnp.sum(lv[0:1] * lv[1:2], axis=-1, keepdims=True))
           - jnp.exp(jnp.sum(lv[2:3] * lv[3:4], axis=-1, keepdims=True)) + lam_init)
    row = lax.broadcasted_iota(I32, (LANES, tq), 0)
    kr = lax.broadcasted_iota(I32, (KT, 2 * tq), 0)
    qc = lax.broadcasted_iota(I32, (KT, 2 * tq), 1)
    causal = kr <= jnp.where(qc >= tq, qc - tq, qc)
    nh = qt_ref.shape[1] // LANES
    qs = []
    for hh in range(nh):
        qt = qt_ref[0, hh * LANES:(hh + 1) * LANES, :]
        zero = jnp.zeros_like(qt)
        q1 = jnp.where((row < A_QK_DIM) | (row >= AUG0), qt, zero)
        q2 = jnp.where(row >= A_QK_DIM, qt, zero)
        qs.append(jnp.concatenate([q1, q2], axis=1))
    _flash_causal_pair(k_ref, vt_ref, qs, qs, qi, causal, m_ref, acc_ref)
    res = []
    for hh in range(nh):
        o = _flash_finish(acc_ref, hh)
        od = o[:, :tq] - lam * o[:, tq:]
        ms = jnp.mean(od * od, axis=0, keepdims=True)
        res.append(od * lax.rsqrt(ms + EPS) * sg_ref[...] * (1.0 - lam_init))
    o_ref[...] = jnp.concatenate(res, axis=0).T.astype(BF16)


def _diff_attention(qta, ka, vta, lamv, sg, *, bsz, seq, lam_init, nh=A_HEADS):
    tq = KT
    nq = seq // tq
    t = bsz * seq
    return pl.pallas_call(
        functools.partial(_diff_kernel, lam_init=lam_init),
        grid=(bsz, A_HEADS // nh, nq),
        in_specs=[
            pl.BlockSpec((1, nh * LANES, tq), lambda b, hp, qi: (b * nq + qi, hp, 0)),
            pl.BlockSpec((seq, nh * LANES), lambda b, hp, qi: (b, hp)),
            pl.BlockSpec((nq, nh * HEAD_DIM, KT), lambda b, hp, qi: (b, hp, 0)),
            pl.BlockSpec((4, A_QK_DIM), lambda b, hp, qi: (0, 0)),
            pl.BlockSpec((HEAD_DIM, 1), lambda b, hp, qi: (0, 0)),
        ],
        out_specs=pl.BlockSpec((tq, nh * HEAD_DIM), lambda b, hp, qi: (b * nq + qi, hp)),
        out_shape=jax.ShapeDtypeStruct((t, A_HEADS * HEAD_DIM), BF16),
        scratch_shapes=[pltpu.VMEM((nh, 1, 2 * tq), F32), pltpu.VMEM((nh, ACC_ROWS, 2 * tq), F32)],
        compiler_params=pltpu.CompilerParams(dimension_semantics=("arbitrary",) * 3,
                                             vmem_limit_bytes=VMEM_LIMIT),
        name="diff_attn",
    )(qta, ka, vta, lamv, sg)


def _moba_kernel(qt_ref, k_ref, vt_ref, o_ref, m_ref, acc_ref, km_ref, s_ref):
    tq = qt_ref.shape[2]
    nh = qt_ref.shape[1] // LANES
    nb = vt_ref.shape[0]
    qi = pl.program_id(2)

    @pl.when(qi == 0)
    def _():
        lane1 = lax.broadcasted_iota(I32, (1, LANES), 1)
        for hh in range(nh):
            km_ref[hh] = jnp.zeros((LANES, LANES), F32)

            def put_block(j, carry):
                blk = k_ref[pl.ds(pl.multiple_of(j * KT, KT), KT), hh * LANES:(hh + 1) * LANES].astype(F32)
                mean = jnp.sum(blk, axis=0, keepdims=True) * (1.0 / KT)
                km_ref[hh, pl.ds(SEL0 + j, 1), :] = jnp.where(lane1 < HEAD_DIM, mean, 0.0)
                return carry
            lax.fori_loop(0, nb, put_block, 0)

    kr = lax.broadcasted_iota(I32, (KT, tq), 0)
    qc = lax.broadcasted_iota(I32, (KT, tq), 1)
    causal = kr <= qc
    nsel = -(-(SEL0 - AUG0 + nb) // 16) * 16
    band = slice(AUG0, AUG0 + nsel)
    row = lax.broadcasted_iota(I32, (nsel, tq), 0) + AUG0
    in_sel = row >= SEL0
    qts = [qt_ref[0, hh * LANES:(hh + 1) * LANES, :] for hh in range(nh)]
    curs = []
    for hh in range(nh):
        km_hi, km_lo = _split2(km_ref[hh, band, :])
        gate = (jnp.dot(km_hi, qts[hh], preferred_element_type=F32)
                + jnp.dot(km_lo, qts[hh], preferred_element_type=F32))
        curs.append(jnp.where(in_sel & (row < SEL0 + qi), gate, -jnp.inf))
    sels = [jnp.zeros((nsel, tq), jnp.bool_)] * nh
    for _ in range(MOBA_TOPK):
        for hh in range(nh):
            mx = jnp.max(curs[hh], axis=0, keepdims=True)
            first = jnp.min(jnp.where(curs[hh] == mx, row, 4 * LANES), axis=0, keepdims=True)
            pick = (row == first) & (mx > -jnp.inf)
            sels[hh] = sels[hh] | pick
            curs[hh] = jnp.where(pick, -jnp.inf, curs[hh])
    q_diag, q_past = [], []
    for hh in range(nh):
        qt, mid = qts[hh], qts[hh][band]
        rest = [qt[AUG0 + nsel:]] if AUG0 + nsel < LANES else []
        q_diag.append(jnp.concatenate([qt[:AUG0], jnp.where(in_sel, jnp.zeros_like(mid), mid)] + rest, axis=0))
        bias = jnp.where(sels[hh], 0.0, NEG_BIG).astype(BF16)
        q_past.append(jnp.concatenate([qt[:AUG0], jnp.where(in_sel, bias, mid)] + rest, axis=0))
    _flash_causal_pair(k_ref, vt_ref, q_diag, q_past, qi, causal, m_ref, acc_ref, s_ref)
    res = [_flash_finish(acc_ref, hh) for hh in range(nh)]
    o_ref[...] = jnp.concatenate(res, axis=0).T.astype(BF16)


def _moba_attention(qtb, kb, vtb, *, bsz, seq, nh=B_HEADS):
    tq = KT
    nq = seq // tq
    assert nq <= MAX_BLOCKS and seq % (PAST_TILES * KT) == 0
    t = bsz * seq
    return pl.pallas_call(
        _moba_kernel,
        grid=(bsz, B_HEADS // nh, nq),
        in_specs=[
            pl.BlockSpec((1, nh * LANES, tq), lambda b, hp, qi: (b * nq + qi, hp, 0)),
            pl.BlockSpec((seq, nh * LANES), lambda b, hp, qi: (b, hp), pipeline_mode=pl.Buffered(1)),
            pl.BlockSpec((nq, nh * HEAD_DIM, KT), lambda b, hp, qi: (b, hp, 0), pipeline_mode=pl.Buffered(1)),
        ],
        out_specs=pl.BlockSpec((tq, nh * HEAD_DIM), lambda b, hp, qi: (b * nq + qi, hp)),
        out_shape=jax.ShapeDtypeStruct((t, B_HEADS * HEAD_DIM), BF16),
        scratch_shapes=[pltpu.VMEM((nh, 1, tq), F32), pltpu.VMEM((nh, ACC_ROWS, tq), F32),
                        pltpu.VMEM((nh, LANES, LANES), F32), pltpu.VMEM((2, nh, PAST_TILES * KT, tq), F32)],
        compiler_params=pltpu.CompilerParams(dimension_semantics=("arbitrary",) * 3,
                                             vmem_limit_bytes=VMEM_LIMIT),
        name="moba_attn",
    )(qtb, kb, vtb)


def _swa_kernel(sink_ref, q_ref, ktp_ref, ktc_ref, vp_ref, vc_ref, o_ref, *, slopes):
    w = WINDOW
    n = pl.program_id(1)
    lane = lax.broadcasted_iota(I32, (w, LANES), 1)
    r2 = lax.broadcasted_iota(I32, (w, 2 * w), 0)
    c2 = lax.broadcasted_iota(I32, (w, 2 * w), 1)
    rel = r2 + w - c2
    band = (rel >= 0) & (rel < w)
    relf = rel.astype(F32)
    for half in range(KT // w):
        mask = band & ((c2 >= w) | (n > 0)) if half == 0 else band
        rows = slice(half * w, (half + 1) * w)
        res = []
        for hq in range(C_HEADS):
            kv = hq // C_GROUP
            kvl = slice(kv * LANES, (kv + 1) * LANES)
            qp = q_ref[rows, (hq // 2) * LANES:(hq // 2 + 1) * LANES]
            qm = jnp.where((lane < HEAD_DIM) == (hq % 2 == 0), qp, jnp.zeros_like(qp))
            if half == 0:
                kt = jnp.concatenate([ktp_ref[0, kvl, KT - w:KT], ktc_ref[0, kvl, 0:w]], axis=1)
                vv = jnp.concatenate([vp_ref[KT - w:KT, kvl], vc_ref[0:w, kvl]], axis=0)
            else:
                kt = ktc_ref[0, kvl, (half - 1) * w:(half + 1) * w]
                vv = vc_ref[(half - 1) * w:(half + 1) * w, kvl]
            s = jnp.dot(qm, kt, preferred_element_type=F32)
            s = jnp.where(mask, s - slopes[hq] * relf, -jnp.inf)
            sink = sink_ref[hq]
            m = jnp.maximum(jnp.max(s, axis=-1, keepdims=True), sink)
            e = jnp.exp(s - m)
            den = jnp.sum(e, axis=-1, keepdims=True) + jnp.exp(sink - m)
            p = (e / den).astype(BF16)
            res.append(jnp.dot(p, vv, preferred_element_type=F32))
        for pr in range(C_HEADS // 2):
            o_ref[rows, pr * LANES:(pr + 1) * LANES] = jnp.where(lane < HEAD_DIM, res[2 * pr],
                                                                 res[2 * pr + 1]).astype(BF16)


def _swa_attention(sinks, qc, ktc, vc2, *, bsz, seq, slopes):
    tq = KT
    nq = seq // tq
    t = bsz * seq
    return pl.pallas_call(
        functools.partial(_swa_kernel, slopes=slopes),
        grid=(bsz, nq),
        in_specs=[
            pl.BlockSpec(memory_space=pltpu.SMEM),
            pl.BlockSpec((tq, C_HEADS * HEAD_DIM), lambda b, n: (b * nq + n, 0)),
            pl.BlockSpec((1, C_KV_HEADS * LANES, KT), lambda b, n: (b * nq + jnp.maximum(n - 1, 0), 0, 0)),
            pl.BlockSpec((1, C_KV_HEADS * LANES, KT), lambda b, n: (b * nq + n, 0, 0)),
            pl.BlockSpec((tq, C_KV_HEADS * LANES), lambda b, n: (b * nq + jnp.maximum(n - 1, 0), 0)),
            pl.BlockSpec((tq, C_KV_HEADS * LANES), lambda b, n: (b * nq + n, 0)),
        ],
        out_specs=pl.BlockSpec((tq, C_HEADS * HEAD_DIM), lambda b, n: (b * nq + n, 0)),
        out_shape=jax.ShapeDtypeStruct((t, C_HEADS * HEAD_DIM), BF16),
        compiler_params=pltpu.CompilerParams(dimension_semantics=("arbitrary",) * 2,
                                             vmem_limit_bytes=VMEM_LIMIT),
        name="swa_attn",
    )(sinks, qc, ktc, ktc, vc2, vc2)


def _outproj_kernel(oa_ref, ob_ref, oc_ref, woa_ref, wob_ref, woc_ref, x_ref, mod_ref, g_ref, wq_ref, keys_ref,
                    x1_ref, h2_ref, st_ref):
    d = x_ref.shape[1]
    mix = (jnp.dot(oa_ref[...], woa_ref[...], preferred_element_type=F32)
           + jnp.dot(ob_ref[...], wob_ref[...], preferred_element_type=F32)
           + jnp.dot(oc_ref[...], woc_ref[...], preferred_element_type=F32))
    g1 = mod_ref[0, :, 2 * d:3 * d]
    sh2 = mod_ref[0, :, 3 * d:4 * d]
    sc2 = mod_ref[0, :, 4 * d:5 * d]
    x1 = x_ref[...] + g1 * mix
    x1_ref[...] = x1
    h2 = _rms_mod(x1, g_ref[...], sc2, sh2)
    nj = d // LANES
    for j in range(nj):
        h2_ref[pl.ds(j, x1.shape[0], stride=nj), :] = h2[:, j * LANES:(j + 1) * LANES]
    pq = jnp.dot(h2.astype(BF16), wq_ref[...], preferred_element_type=F32).astype(BF16)
    for hp in range(2 * PEER_HEADS):
        st_ref[hp] = lax.dot_general(keys_ref[hp], pq[:, hp * LANES:(hp + 1) * LANES],
                                     (((1,), (1,)), ((), ())), preferred_element_type=F32)


def _outproj(oa, ob, oc, woa, wob, woc, x2d, mod3, g, wq, keys, *, seq, tm=256):
    t, d = x2d.shape
    nt = t // tm
    nq = wq.shape[1]
    full = lambda a: pl.BlockSpec(a.shape, lambda i: (0,) * a.ndim)
    return pl.pallas_call(
        _outproj_kernel,
        grid=(nt,),
        in_specs=[
            pl.BlockSpec((tm, oa.shape[1]), lambda i: (i, 0)),
            pl.BlockSpec((tm, ob.shape[1]), lambda i: (i, 0)),
            pl.BlockSpec((tm, oc.shape[1]), lambda i: (i, 0)),
            full(woa), full(wob), full(woc),
            pl.BlockSpec((tm, d), lambda i: (i, 0)),
            pl.BlockSpec((1, 1, mod3.shape[2]), lambda i: ((i * tm) // seq, 0, 0)),
            pl.BlockSpec((1, d), lambda i: (0, 0)),
            full(wq), full(keys),
        ],
        out_specs=[pl.BlockSpec((tm, d), lambda i: (i, 0)),
                   pl.BlockSpec((tm * (d // LANES), LANES), lambda i: (i, 0)),
                   pl.BlockSpec((2 * PEER_HEADS, PEER_NKEYS, tm), lambda i: (0, 0, i))],
        out_shape=[jax.ShapeDtypeStruct((t, d), F32), jax.ShapeDtypeStruct((t * (d // LANES), LANES), F32),
                   jax.ShapeDtypeStruct((2 * PEER_HEADS, PEER_NKEYS, t), F32)],
        compiler_params=pltpu.CompilerParams(dimension_semantics=("arbitrary",),
                                             vmem_limit_bytes=VMEM_LIMIT),
        name="outproj_peerq",
    )(oa, ob, oc, woa, wob, woc, x2d, mod3, g, wq, keys)


_CAND_BLOCKS = ((0, 16),) + tuple((i, 8) for i in range(1, 8))
_CAND_ROWS = 16 + 7 * 8 + 8
_BIG_I = np.int32(2 ** 30)


TOPK_TL = LANES


def _topk_rows(curs, keys, val_refs, key_refs):
    def body(r, curs):
        out = []
        for cur, key, val_ref, key_ref in zip(curs, keys, val_refs, key_refs):
            m = jnp.max(cur, axis=0, keepdims=True)
            kmin = jnp.min(jnp.where(cur == m, key, _BIG_I), axis=0, keepdims=True)
            val_ref[pl.ds(r, 1), :] = m
            key_ref[pl.ds(r, 1), :] = kmin
            out.append(jnp.where(key == kmin, -jnp.inf, cur))
        return tuple(out)
    lax.fori_loop(0, PEER_TOPK, body, tuple(curs), unroll=True)


def _topk_scratch():
    pair = [pltpu.VMEM((2, PEER_TOPK, TOPK_TL), F32), pltpu.VMEM((2, PEER_TOPK, TOPK_TL), I32)]
    hk = PEER_HEADS * PEER_TOPK
    return pair * 3 + [pltpu.VMEM((hk, TOPK_TL), I32), pltpu.VMEM((hk, TOPK_TL), F32)]


def _product_candidates(sv1, si1, sv2, si2):
    tl = sv1.shape[1]
    jrow8 = lax.broadcasted_iota(I32, (8, tl), 0)
    jrow16 = lax.broadcasted_iota(I32, (16, tl), 0)
    cands, keys = [], []
    for i, nj in _CAND_BLOCKS:
        jrow = jrow16 if nj == 16 else jrow8
        cands.append(sv1[i:i + 1] + sv2[0:nj])
        keys.append((i * PEER_TOPK + jrow) * (PEER_NKEYS * PEER_NKEYS) + si1[i:i + 1] * PEER_NKEYS + si2[0:nj])
    cands.append(sv1[8:16] + sv2[0:1])
    keys.append((jrow8 + 8) * (PEER_TOPK * PEER_NKEYS * PEER_NKEYS) + si1[8:16] * PEER_NKEYS + si2[0:1])
    return jnp.concatenate(cands, axis=0), jnp.concatenate(keys, axis=0)


def _store_gates(h, top, keys, ei_ref, gg_ref):
    rows = pl.ds(pl.multiple_of(h * PEER_TOPK, PEER_TOPK), PEER_TOPK)
    e = jnp.exp(top - top[0:1])
    gg_ref[rows, :] = e / jnp.sum(e, axis=0, keepdims=True)
    ei_ref[rows, :] = keys & (PEER_NKEYS * PEER_NKEYS - 1)


def _topk_head_pair(st_ref, h0, scratch):
    v1_ref, k1_ref, v2_ref, k2_ref, vt_ref, kt_ref, ei_ref, gg_ref = scratch
    row = lax.broadcasted_iota(I32, (PEER_NKEYS, st_ref.shape[2]), 0)
    cand, ckey = [], []
    for s in range(2):
        h = h0 + s
        _topk_rows([st_ref[2 * h], st_ref[2 * h + 1]], [row, row],
                   [v1_ref.at[s], v2_ref.at[s]], [k1_ref.at[s], k2_ref.at[s]])
        c, k = _product_candidates(v1_ref[s], k1_ref[s], v2_ref[s], k2_ref[s])
        cand.append(c)
        ckey.append(k)
    _topk_rows(cand, ckey, [vt_ref.at[0], vt_ref.at[1]], [kt_ref.at[0], kt_ref.at[1]])
    for s in range(2):
        _store_gates(h0 + s, vt_ref[s], kt_ref[s], ei_ref, gg_ref)


def _topk_emit(scratch, off_ref, par_ref, g_ref):
    ei = scratch[6][...].T
    off_ref[...] = (ei & (PEER_HALF - 1)) * SUB
    par_ref[...] = ei // PEER_HALF
    g_ref[...] = scratch[7][...].T


def _peer_topk_kernel(st_ref, off_ref, par_ref, g_ref, *scratch):
    for h0 in range(0, PEER_HEADS, 2):
        _topk_head_pair(st_ref, h0, scratch)
    _topk_emit(scratch, off_ref, par_ref, g_ref)


def _peer_topk(st, blk0, nblk):
    hk = PEER_HEADS * PEER_TOPK
    t = nblk * TOPK_TL
    return pl.pallas_call(
        _peer_topk_kernel,
        grid=(nblk,),
        in_specs=[pl.BlockSpec((2 * PEER_HEADS, PEER_NKEYS, TOPK_TL), lambda i: (0, 0, blk0 + i))],
        out_specs=[pl.BlockSpec((TOPK_TL, hk), lambda i: (i, 0))] * 3,
        out_shape=[jax.ShapeDtypeStruct((t, hk), I32), jax.ShapeDtypeStruct((t, hk), I32),
                   jax.ShapeDtypeStruct((t, hk), F32)],
        scratch_shapes=_topk_scratch(),
        compiler_params=pltpu.CompilerParams(dimension_semantics=("arbitrary",),
                                             vmem_limit_bytes=VMEM_LIMIT),
        name="peer_topk",
    )(st)


SUB = 8


def _erf(x):
    return lax.erf(x)


HALF_HI = 0xFFFF0000


PEER_HALF = PEER_NKEYS * PEER_NKEYS // 2


def _pack_kernel(lo_ref, hi_ref, o_ref):
    pairs = lo_ref.shape[0]
    nj = lo_ref.shape[1] // LANES
    for j in range(nj):
        cols = slice(j * LANES, (j + 1) * LANES)
        lo = pltpu.bitcast(lo_ref[:, cols].astype(BF16).astype(F32), jnp.uint32)
        hi = pltpu.bitcast(hi_ref[:, cols].astype(BF16).astype(F32), jnp.uint32)
        o_ref[pl.ds(j, pairs, stride=nj), :] = (lo >> 16) | (hi & jnp.uint32(HALF_HI))


def _pack_expert_table(w, pairs=256):
    n, d = w.shape
    nj = d // LANES
    half_blocks = n // 2 // pairs
    return pl.pallas_call(
        _pack_kernel,
        grid=(half_blocks,),
        in_specs=[pl.BlockSpec((pairs, d), lambda i: (i, 0)),
                  pl.BlockSpec((pairs, d), lambda i: (half_blocks + i, 0))],
        out_specs=pl.BlockSpec((pairs * nj, LANES), lambda i: (i, 0)),
        out_shape=jax.ShapeDtypeStruct((n // 2 * nj, LANES), jnp.uint32),
        compiler_params=pltpu.CompilerParams(dimension_semantics=("arbitrary",),
                                             vmem_limit_bytes=VMEM_LIMIT),
        name="pack_table",
    )(w, w)


def _load_table_once(tab_hbm, tab, sem):
    @pl.when(pl.program_id(0) == 0)
    def _():
        cp = pltpu.make_async_copy(tab_hbm, tab, sem)
        cp.start()
        cp.wait()


def _pair_tile(tab, off):
    return tab[pl.ds(pl.multiple_of(off, SUB), SUB), :]


def _group_matrix(rows, cols):
    r = lax.broadcasted_iota(I32, (rows, cols), 0)
    c = lax.broadcasted_iota(I32, (rows, cols), 1)
    return (c // (cols // rows) == r).astype(BF16)


def _split2(x):
    hi = x.astype(BF16)
    return hi, (x - hi.astype(F32)).astype(BF16)


PAIR_ROWS = 2 * SUB


def _pair_rows(tab, off_ref, tt, hk):
    return jnp.concatenate([pltpu.bitcast(_pair_tile(tab, off_ref[tt * hk + k]), BF16) for k in range(hk)], axis=0)


def _own_sublane_mask(hk):
    srow = lax.broadcasted_iota(I32, (SUB, hk * PAIR_ROWS), 0)
    scol = lax.broadcasted_iota(I32, (SUB, hk * PAIR_ROWS), 1)
    return (scol % PAIR_ROWS) // 2 == srow


def _fold_matrix(hk):
    c = jnp.arange(hk * PAIR_ROWS)
    return jax.nn.one_hot((c % 2) * hk + c // PAIR_ROWS, 2 * hk, dtype=BF16)


def _peer_u_kernel(off_ref, par_ref, h_ref, g_ref, fold_ref, tab_hbm, w_ref, tab, sem, d_sc):
    tq, hk = g_ref.shape
    _load_table_once(tab_hbm, tab, sem)
    mine = _own_sublane_mask(hk)
    nt = (((1,), (1,)), ((), ()))
    for tt in range(tq):
        h_hi, h_lo = _split2(h_ref[tt])
        s = lax.dot_general(jnp.concatenate([h_hi, h_lo], axis=0), _pair_rows(tab, off_ref, tt, hk), nt,
                            preferred_element_type=F32)
        d = jnp.where(mine, s[0:SUB] + s[SUB:2 * SUB], 0.0)
        d_sc[tt:tt + 1, :] = jnp.sum(d, axis=0, keepdims=True)
    d_hi, d_lo = _split2(d_sc[...])
    a2 = jnp.dot(jnp.concatenate([d_hi, d_lo], axis=0), fold_ref[...], preferred_element_type=F32)
    a2 = a2[0:tq] + a2[tq:2 * tq]
    a = jnp.where(par_ref[...] == 0, a2[:, :hk], a2[:, hk:])
    w_ref[...] = g_ref[...] * (0.5 * a * (1.0 + _erf(a * (2.0 ** -0.5))))


U_TQ = 128
V_TQ = 64


def _peer_u(off_flat, par, g, h2, utab, tok0):
    t, hk = g.shape
    nj = h2.shape[1]
    tq = U_TQ
    assert t % tq == 0 and tok0 % tq == 0
    fold = _fold_matrix(hk)
    return pl.pallas_call(
        _peer_u_kernel,
        grid=(t // tq,),
        in_specs=[
            pl.BlockSpec((tq * hk,), lambda i: (i,), memory_space=pltpu.SMEM),
            pl.BlockSpec((tq, hk), lambda i: (i, 0)),
            pl.BlockSpec((tq, nj, LANES), lambda i: (tok0 // tq + i, 0, 0)),
            pl.BlockSpec((tq, hk), lambda i: (i, 0)),
            pl.BlockSpec(fold.shape, lambda i: (0, 0)),
            pl.BlockSpec(memory_space=pl.ANY),
        ],
        out_specs=pl.BlockSpec((tq, hk), lambda i: (i, 0)),
        out_shape=jax.ShapeDtypeStruct((t, hk), F32),
        scratch_shapes=[pltpu.VMEM(utab.shape, jnp.uint32), pltpu.SemaphoreType.DMA(()),
                        pltpu.VMEM((tq, hk * PAIR_ROWS), F32)],
        compiler_params=pltpu.CompilerParams(dimension_semantics=("arbitrary",),
                                             vmem_limit_bytes=VMEM_LIMIT),
        name="peer_u",
    )(off_flat, par, h2, g, fold, utab)


def _peer_v_body(off_ref, par_ref, w_ref, x1_ref, mod_ref, fg_ref, o_ref, tab, wl_sc, p_sc, final):
    tq, hk = w_ref.shape
    d = x1_ref.shape[1]
    nj = d // LANES
    width = hk * PAIR_ROWS
    rep = _group_matrix(hk, width)
    w_hi, w_lo = _split2(w_ref[...])
    parl = jnp.dot(par_ref[...].astype(BF16), rep, preferred_element_type=F32)
    lane = lax.broadcasted_iota(I32, (tq, width), 1)
    wanted = (lane % 2).astype(F32) == parl
    wl_sc[0] = jnp.where(wanted, jnp.dot(w_hi, rep, preferred_element_type=F32), 0.0)
    wl_sc[1] = jnp.where(wanted, jnp.dot(w_lo, rep, preferred_element_type=F32), 0.0)
    mine = _own_sublane_mask(hk)
    for tt in range(tq):
        lhs = jnp.concatenate([jnp.where(mine, wl_sc[0, tt:tt + 1, :], 0.0),
                               jnp.where(mine, wl_sc[1, tt:tt + 1, :], 0.0)], axis=0).astype(BF16)
        r = jnp.dot(lhs, _pair_rows(tab, off_ref, tt, hk), preferred_element_type=F32)
        p_sc[tt * SUB:(tt + 1) * SUB, :] = r[0:SUB] + r[SUB:2 * SUB]
    peer = jnp.concatenate([p_sc[pl.ds(j, tq, stride=nj), :] for j in range(nj)], axis=1)
    y = x1_ref[...] + mod_ref[0, :, 5 * d:6 * d] * peer
    if final:
        y = y * lax.rsqrt(jnp.mean(y * y, axis=-1, keepdims=True) + EPS) * fg_ref[...]
    o_ref[...] = y


def _peer_v_kernel(off_ref, par_ref, w_ref, x1_ref, mod_ref, fg_ref, tab_hbm, o_ref, tab, sem, wl_sc, p_sc, *, final):
    _load_table_once(tab_hbm, tab, sem)
    _peer_v_body(off_ref, par_ref, w_ref, x1_ref, mod_ref, fg_ref, o_ref, tab, wl_sc, p_sc, final)


def _peer_v_topk_kernel(off_ref, par_ref, w_ref, x1_ref, mod_ref, fg_ref, st_ref, tab_hbm,
                        o_ref, off2_ref, par2_ref, g2_ref, tab, sem, wl_sc, p_sc, *scratch, final):
    steps = TOPK_TL // w_ref.shape[0]
    pairs = PEER_HEADS // 2 // steps
    part = pl.program_id(0) % steps
    _load_table_once(tab_hbm, tab, sem)
    for q in range(pairs):
        _topk_head_pair(st_ref, (pairs * part + q) * 2, scratch)
    _peer_v_body(off_ref, par_ref, w_ref, x1_ref, mod_ref, fg_ref, o_ref, tab, wl_sc, p_sc, final)

    @pl.when(part == steps - 1)
    def _():
        _topk_emit(scratch, off2_ref, par2_ref, g2_ref)


def _peer_v(off_flat, par, w, x1, mod3, final_g, vtab, tok0, *, seq, final, st=None, st_blk0=None):
    t, hk = w.shape
    d = x1.shape[1]
    tq = V_TQ
    steps = TOPK_TL // tq
    assert t % TOPK_TL == 0 and steps * tq == TOPK_TL and (PEER_HEADS // 2) % steps == 0
    assert tok0 % tq == 0 and tok0 // seq == (tok0 + t - 1) // seq
    in_specs = [
        pl.BlockSpec((tq * hk,), lambda i: (i,), memory_space=pltpu.SMEM),
        pl.BlockSpec((tq, hk), lambda i: (i, 0)),
        pl.BlockSpec((tq, hk), lambda i: (i, 0)),
        pl.BlockSpec((tq, d), lambda i: (tok0 // tq + i, 0)),
        pl.BlockSpec((1, 1, mod3.shape[2]), lambda i: (tok0 // seq, 0, 0)),
        pl.BlockSpec((1, d), lambda i: (0, 0)),
    ]
    out_specs = [pl.BlockSpec((tq, d), lambda i: (tok0 // tq + i, 0))]
    out_shape = [jax.ShapeDtypeStruct(x1.shape, F32)]
    scratch = [pltpu.VMEM(vtab.shape, jnp.uint32), pltpu.SemaphoreType.DMA(()),
               pltpu.VMEM((2, tq, hk * PAIR_ROWS), F32), pltpu.VMEM((tq * (d // LANES), LANES), F32)]
    args = [off_flat, par, w, x1, mod3, final_g]
    if st is not None:
        in_specs.append(pl.BlockSpec((2 * PEER_HEADS, PEER_NKEYS, TOPK_TL), lambda i: (0, 0, st_blk0 + i // steps)))
        out_specs += [pl.BlockSpec((TOPK_TL, hk), lambda i: (i // steps, 0))] * 3
        out_shape += [jax.ShapeDtypeStruct((t, hk), I32), jax.ShapeDtypeStruct((t, hk), I32),
                      jax.ShapeDtypeStruct((t, hk), F32)]
        scratch += _topk_scratch()
        args.append(st)
    in_specs.append(pl.BlockSpec(memory_space=pl.ANY))
    args.append(vtab)
    return pl.pallas_call(
        functools.partial(_peer_v_kernel if st is None else _peer_v_topk_kernel, final=final),
        grid=(t // tq,),
        in_specs=in_specs,
        out_specs=out_specs,
        out_shape=out_shape,
        scratch_shapes=scratch,
        input_output_aliases={3: 0},
        compiler_params=pltpu.CompilerParams(dimension_semantics=("arbitrary",),
                                             vmem_limit_bytes=VMEM_LIMIT),
        name="peer_v" if st is None else "peer_v_topk",
    )(*args)


def kernel(x, c, norm1_g, norm2_g, w_ada, b_ada, w_in, w_out, lam_q1, lam_k1, lam_q2, lam_k2, subln_g, sinks,
           peer_wq, peer_keys, peer_u, peer_v, final_g):
    bsz, seq, d = x.shape
    depth = w_in.shape[0]
    t = bsz * seq
    slopes = _alibi_slopes()
    sl_c = [float(s) for s in slopes[:C_HEADS]]
    qa_bias = _slope_bias_col(slopes[C_HEADS:C_HEADS + A_HEADS] * np.float32(LOG2E))
    qb_bias = _slope_bias_col(slopes[C_HEADS + A_HEADS:] * np.float32(LOG2E))
    mods = _adaln_mods(c, w_ada, b_ada)
    x2d = x.reshape(t, d)
    av, bw = A_HEADS * HEAD_DIM, B_HEADS * HEAD_DIM
    for l in range(depth):
        lam_init = 0.8 - 0.6 * math.exp(-0.3 * l)
        mod3 = mods[l].reshape(bsz, 1, N_MOD * d)
        wn, wt = _prep_in_weights(w_in[l])
        ka, kb, qc, vc2, qta, vta, qtb, vtb, ktc = _inproj(
            x2d, mod3, norm1_g[l].reshape(1, d), wn, wt, qa_bias, qb_bias, seq=seq)
        lamv = jnp.stack([lam_q1[l], lam_k1[l], lam_q2[l], lam_k2[l]]).astype(F32)
        sg = subln_g[l].reshape(HEAD_DIM, 1).astype(F32)
        oa = _diff_attention(qta, ka, vta, lamv, sg, bsz=bsz, seq=seq, lam_init=lam_init)
        ob = _moba_attention(qtb, kb, vtb, bsz=bsz, seq=seq)
        oc = _swa_attention(sinks[l].astype(F32), qc, ktc, vc2, bsz=bsz, seq=seq, slopes=sl_c)
        wo = w_out[l].astype(BF16)
        keys = peer_keys[l].reshape(2 * PEER_HEADS, PEER_NKEYS, -1).astype(BF16)
        x1, h2, st = _outproj(oa, ob, oc, wo[:av], wo[av:av + bw], wo[av + bw:], x2d, mod3,
                              norm2_g[l].reshape(1, d), peer_wq[l].astype(BF16), keys, seq=seq)
        assert seq % TOPK_TL == 0
        blocks = seq // TOPK_TL
        utab = _pack_expert_table(peer_u[l])
        vtab = _pack_expert_table(peer_v[l])
        h2t = h2.reshape(t, d // LANES, LANES)
        off_b, par_b, g_b = _peer_topk(st, 0, blocks)
        x2d = x1
        for b in range(bsz):
            w_b = _peer_u(off_b.reshape(-1), par_b, g_b, h2t, utab, b * seq)
            more = dict(st=st, st_blk0=(b + 1) * blocks) if b + 1 < bsz else {}
            x2d, *nxt = _peer_v(off_b.reshape(-1), par_b, w_b, x2d, mod3, final_g.reshape(1, d), vtab, b * seq,
                                seq=seq, final=(l == depth - 1), **more)
            if nxt:
                off_b, par_b, g_b = nxt
    return x2d.reshape(bsz, seq, d)
```
